```python
import jax, jax.numpy as jnp
from jax import lax
import numpy as np

D_MODEL = 2048
BATCH = 8
SEQ = 8192
DEPTH = 1

GLA_HEADS = 4
GLA_DK = (D_MODEL // 2) // GLA_HEADS
GLA_DV = D_MODEL // GLA_HEADS
GLA_GATE_RANK = 16
GLA_GATE_NORMALIZER = 16.0
GLA_CHUNK = 64
MLA_HEADS = 16
MLA_Q_RANK = D_MODEL // 4
MLA_KV_RANK = 512
MLA_NOPE = 128
MLA_ROPE = 64
MLA_V = 128
ROPE_THETA = 10000.0
Q_BLOCK = 128
FFN_HIDDEN = ((8 * D_MODEL // 3 + 255) // 256) * 256
NORM_EPS = 1e-6
N_MOD = 6

IN_WIDTHS = (GLA_HEADS * GLA_DK, GLA_HEADS * GLA_DK, GLA_HEADS * GLA_DV, GLA_HEADS * GLA_DV,
             GLA_GATE_RANK, MLA_Q_RANK, MLA_KV_RANK, MLA_ROPE, D_MODEL, D_MODEL)
IN_TOTAL = sum(IN_WIDTHS)

kernel_name = "hybrid_gla_mla_adaln_block"


def rmsnorm(x, g):
    xf = x.astype(jnp.float32)
    y = xf * lax.rsqrt(jnp.mean(xf * xf, axis=-1, keepdims=True) + NORM_EPS)
    return (y * g.astype(jnp.float32)).astype(x.dtype)


def rope(t, cos, sin):
    t1, t2 = jnp.split(t, 2, axis=-1)
    return jnp.concatenate([t1 * cos - t2 * sin, t2 * cos + t1 * sin], axis=-1)


def gla_chunked(q, k, v, log_a):
    B, H, S, dk = q.shape
    dv = v.shape[-1]
    C = GLA_CHUNK
    N = S // C

    def to_chunks(t):
        return t.astype(jnp.float32).reshape(B, H, N, C, t.shape[-1]).transpose(2, 0, 1, 3, 4)

    qc, kc, vc = to_chunks(q), to_chunks(k), to_chunks(v)
    bc = jnp.cumsum(to_chunks(log_a), axis=3)
    causal = jnp.tril(jnp.ones((C, C), dtype=bool))

    def step(state, inp):
        q_i, k_i, v_i, b_i = inp
        o_inter = jnp.einsum('bhid,bhdv->bhiv', q_i * jnp.exp(b_i), state)
        diff = b_i[:, :, :, None, :] - b_i[:, :, None, :, :]
        w = jnp.exp(jnp.where(causal[:, :, None], diff, -jnp.inf))
        att = jnp.einsum('bhid,bhjd,bhijd->bhij', q_i, k_i, w)
        o_intra = jnp.einsum('bhij,bhjv->bhiv', att, v_i)
        b_last = b_i[:, :, -1, :]
        k_dec = k_i * jnp.exp(b_last[:, :, None, :] - b_i)
        state = jnp.exp(b_last)[..., None] * state + jnp.einsum('bhjd,bhjv->bhdv', k_dec, v_i)
        return state, o_inter + o_intra

    state0 = jnp.zeros((B, H, dk, dv), jnp.float32)
    _, o = lax.scan(step, state0, (qc, kc, vc, bc))
    return o.transpose(1, 2, 0, 3, 4).reshape(B, H, S, dv).astype(v.dtype)


def mla_attention(q_nope, q_rope, k_nope, k_rope, v):
    B, S, H, dn = q_nope.shape
    R = q_rope.shape[-1]
    dv = v.shape[-1]
    NB = S // Q_BLOCK
    scale = (dn + R) ** -0.5
    qn = q_nope.reshape(B, NB, Q_BLOCK, H, dn).transpose(1, 0, 3, 2, 4)
    qr = q_rope.reshape(B, NB, Q_BLOCK, H, R).transpose(1, 0, 3, 2, 4)
    kn = k_nope.transpose(0, 2, 1, 3)
    vv = v.transpose(0, 2, 1, 3)
    key_pos = jnp.arange(S)

    def block(args):
        qn_b, qr_b, i = args
        s = (jnp.einsum('bhqd,bhkd->bhqk', qn_b, kn)
             + jnp.einsum('bhqr,bkr->bhqk', qr_b, k_rope)).astype(jnp.float32) * scale
        q_pos = i * Q_BLOCK + jnp.arange(Q_BLOCK)
        mask = key_pos[None, :] <= q_pos[:, None]
        p = jax.nn.softmax(jnp.where(mask, s, -jnp.inf), axis=-1)
        return jnp.einsum('bhqk,bhkd->bhqd', p.astype(vv.dtype), vv)

    o = lax.map(block, (qn, qr, jnp.arange(NB)))
    return o.transpose(1, 0, 3, 2, 4).reshape(B, S, H, dv)


def token_mix(h, cos, sin, w_in, gla_gk_w, gla_gk_b, gla_onorm_g, gla_wo,
              mla_q_norm_g, mla_wuq, mla_kv_norm_g, mla_wukv, mla_wo, w_out):
    B, S, _ = h.shape
    idx = [sum(IN_WIDTHS[:i + 1]) for i in range(len(IN_WIDTHS) - 1)]
    (p_q, p_k, p_v, p_g, p_gk, p_cq, p_ckv, p_kr, p_ga, p_gb) = jnp.split(h @ w_in, idx, axis=-1)

    def heads(t, d):
        return t.reshape(B, S, GLA_HEADS, d).transpose(0, 2, 1, 3)
    q = heads(p_q, GLA_DK) * (GLA_DK ** -0.5)
    k = heads(p_k, GLA_DK)
    v = heads(p_v, GLA_DV)
    log_a = jax.nn.log_sigmoid((p_gk @ gla_gk_w + gla_gk_b).astype(jnp.float32)) / GLA_GATE_NORMALIZER
    o = gla_chunked(q, k, v, heads(log_a, GLA_DK)).transpose(0, 2, 1, 3)
    o = rmsnorm(o, gla_onorm_g) * jax.nn.silu(p_g.reshape(B, S, GLA_HEADS, GLA_DV))
    y_gla = o.reshape(B, S, GLA_HEADS * GLA_DV) @ gla_wo

    q_lat = (rmsnorm(p_cq, mla_q_norm_g) @ mla_wuq).reshape(B, S, MLA_HEADS, MLA_NOPE + MLA_ROPE)
    q_nope, q_rope = q_lat[..., :MLA_NOPE], q_lat[..., MLA_NOPE:]
    q_rope = rope(q_rope, cos[:, :, None, :], sin[:, :, None, :])
    kv = (rmsnorm(p_ckv, mla_kv_norm_g) @ mla_wukv).reshape(B, S, MLA_HEADS, MLA_NOPE + MLA_V)
    k_nope, v_m = kv[..., :MLA_NOPE], kv[..., MLA_NOPE:]
    k_rope = rope(p_kr, cos, sin)
    o_m = mla_attention(q_nope, q_rope, k_nope, k_rope, v_m)
    y_mla = o_m.reshape(B, S, MLA_HEADS * MLA_V) @ mla_wo

    merged = jax.nn.sigmoid(p_ga) * y_gla + jax.nn.sigmoid(p_gb) * y_mla
    return merged @ w_out


def swiglu(h, w_in, w_down):
    g, u = jnp.split(h @ w_in, 2, axis=-1)
    return (jax.nn.silu(g) * u) @ w_down


def _fwd_setup_inputs(seed: int = 0) -> dict:
    key = jax.random.key(seed)
    ks = jax.random.split(key, 24)
    L, D = DEPTH, D_MODEL

    def nrm(k, shape, scale):
        return jax.random.normal(k, shape, jnp.float32) * scale

    def gain(k, n):
        return 1.0 + 0.02 * jax.random.normal(k, (L, n), jnp.float32)

    x = jax.random.normal(ks[0], (BATCH, SEQ, D), jnp.float32)
    c = jax.random.normal(ks[1], (BATCH, D), jnp.float32)
    offsets = jax.random.randint(ks[2], (BATCH, 1), 0, 1024, dtype=jnp.int32)
    positions = offsets + jnp.arange(SEQ, dtype=jnp.int32)[None, :]
    return {
        "x": x,
        "c": c,
        "positions": positions,
        "ada_w": nrm(ks[3], (L, D, N_MOD * D), 0.5 * D ** -0.5),
        "ada_b": nrm(ks[4], (L, N_MOD * D), 0.02),
        "norm_mix_g": gain(ks[5], D),
        "w_in": nrm(ks[6], (L, D, IN_TOTAL), D ** -0.5),
        "gla_gk_w": nrm(ks[7], (L, GLA_GATE_RANK, GLA_HEADS * GLA_DK), GLA_GATE_RANK ** -0.5),
        "gla_gk_b": nrm(ks[8], (L, GLA_HEADS * GLA_DK), 0.5),
        "gla_onorm_g": gain(ks[9], GLA_DV),
        "gla_wo": nrm(ks[10], (L, GLA_HEADS * GLA_DV, D), (GLA_HEADS * GLA_DV) ** -0.5),
        "mla_q_norm_g": gain(ks[11], MLA_Q_RANK),
        "mla_wuq": nrm(ks[12], (L, MLA_Q_RANK, MLA_HEADS * (MLA_NOPE + MLA_ROPE)), MLA_Q_RANK ** -0.5),
        "mla_kv_norm_g": gain(ks[13], MLA_KV_RANK),
        "mla_wukv": nrm(ks[14], (L, MLA_KV_RANK, MLA_HEADS * (MLA_NOPE + MLA_V)), MLA_KV_RANK ** -0.5),
        "mla_wo": nrm(ks[15], (L, MLA_HEADS * MLA_V, D), (MLA_HEADS * MLA_V) ** -0.5),
        "w_out": nrm(ks[16], (L, D, D), D ** -0.5),
        "norm_ffn_g": gain(ks[17], D),
        "ffn_w_in": nrm(ks[18], (L, D, 2 * FFN_HIDDEN), D ** -0.5),
        "ffn_w_down": nrm(ks[19], (L, FFN_HIDDEN, D), FFN_HIDDEN ** -0.5),
        "final_norm_g": 1.0 + 0.02 * jax.random.normal(ks[20], (D,), jnp.float32),
    }


def _fwd_reference(x, c, positions, ada_w, ada_b, norm_mix_g, w_in, gla_gk_w, gla_gk_b, gla_onorm_g,
              gla_wo, mla_q_norm_g, mla_wuq, mla_kv_norm_g, mla_wukv, mla_wo, w_out,
              norm_ffn_g, ffn_w_in, ffn_w_down, final_norm_g):
    B, S, D = x.shape
    inv_freq = ROPE_THETA ** (-jnp.arange(0, MLA_ROPE, 2, dtype=jnp.float32) / MLA_ROPE)
    ang = positions.astype(jnp.float32)[..., None] * inv_freq
    cos = jnp.cos(ang).astype(x.dtype)
    sin = jnp.sin(ang).astype(x.dtype)
    c_act = jax.nn.silu(c)
    for l in range(DEPTH):
        mod = (c_act @ ada_w[l] + ada_b[l]).reshape(B, N_MOD, D)[:, :, None, :]
        shift_m, scale_m, gate_m, shift_f, scale_f, gate_f = [mod[:, i] for i in range(N_MOD)]
        h = rmsnorm(x, norm_mix_g[l]) * (1.0 + scale_m) + shift_m
        x = x + gate_m * token_mix(h, cos, sin, w_in[l], gla_gk_w[l], gla_gk_b[l], gla_onorm_g[l],
                                   gla_wo[l], mla_q_norm_g[l], mla_wuq[l], mla_kv_norm_g[l],
                                   mla_wukv[l], mla_wo[l], w_out[l])
        h = rmsnorm(x, norm_ffn_g[l]) * (1.0 + scale_f) + shift_f
        x = x + gate_f * swiglu(h, ffn_w_in[l], ffn_w_down[l])
    return rmsnorm(x, final_norm_g)


import jax as _jax
import jax.numpy as _jnp

TWIN_FORMAT = 'train_step'
FWD_PARAMS = ['x', 'c', 'positions', 'ada_w', 'ada_b', 'norm_mix_g', 'w_in', 'gla_gk_w', 'gla_gk_b', 'gla_onorm_g', 'gla_wo', 'mla_q_norm_g', 'mla_wuq', 'mla_kv_norm_g', 'mla_wukv', 'mla_wo', 'w_out', 'norm_ffn_g', 'ffn_w_in', 'ffn_w_down', 'final_norm_g']
TWIN_WEIGHTS = ['ada_w', 'ada_b', 'norm_mix_g', 'w_in', 'gla_gk_w', 'gla_gk_b', 'gla_onorm_g', 'gla_wo', 'mla_q_norm_g', 'mla_wuq', 'mla_kv_norm_g', 'mla_wukv', 'mla_wo', 'w_out', 'norm_ffn_g', 'ffn_w_in', 'ffn_w_down', 'final_norm_g']
TWIN_DIFF_INPUT = 'x'
TWIN_INPUTS = ['x', 'c', 'positions', 'ada_w', 'ada_b', 'norm_mix_g', 'w_in', 'gla_gk_w', 'gla_gk_b', 'gla_onorm_g', 'gla_wo', 'mla_q_norm_g', 'mla_wuq', 'mla_kv_norm_g', 'mla_wukv', 'mla_wo', 'w_out', 'norm_ffn_g', 'ffn_w_in', 'ffn_w_down', 'final_norm_g', 'loss_target', 'm_ada_w', 'm_ada_b', 'm_norm_mix_g', 'm_w_in', 'm_gla_gk_w', 'm_gla_gk_b', 'm_gla_onorm_g', 'm_gla_wo', 'm_mla_q_norm_g', 'm_mla_wuq', 'm_mla_kv_norm_g', 'm_mla_wukv', 'm_mla_wo', 'm_w_out', 'm_norm_ffn_g', 'm_ffn_w_in', 'm_ffn_w_down', 'm_final_norm_g', 'v_ada_w', 'v_ada_b', 'v_norm_mix_g', 'v_w_in', 'v_gla_gk_w', 'v_gla_gk_b', 'v_gla_onorm_g', 'v_gla_wo', 'v_mla_q_norm_g', 'v_mla_wuq', 'v_mla_kv_norm_g', 'v_mla_wukv', 'v_mla_wo', 'v_w_out', 'v_norm_ffn_g', 'v_ffn_w_in', 'v_ffn_w_down', 'v_final_norm_g']
TWIN_OUTPUTS = ['loss', 'grad_x', 'grad_ada_w', 'grad_ada_b', 'grad_norm_mix_g', 'grad_w_in', 'grad_gla_gk_w', 'grad_gla_gk_b', 'grad_gla_onorm_g', 'grad_gla_wo', 'grad_mla_q_norm_g', 'grad_mla_wuq', 'grad_mla_kv_norm_g', 'grad_mla_wukv', 'grad_mla_wo', 'grad_w_out', 'grad_norm_ffn_g', 'grad_ffn_w_in', 'grad_ffn_w_down', 'grad_final_norm_g', 'delta_ada_w', 'delta_ada_b', 'delta_norm_mix_g', 'delta_w_in', 'delta_gla_gk_w', 'delta_gla_gk_b', 'delta_gla_onorm_g', 'delta_gla_wo', 'delta_mla_q_norm_g', 'delta_mla_wuq', 'delta_mla_kv_norm_g', 'delta_mla_wukv', 'delta_mla_wo', 'delta_w_out', 'delta_norm_ffn_g', 'delta_ffn_w_in', 'delta_ffn_w_down', 'delta_final_norm_g', 'new_m_ada_w', 'new_m_ada_b', 'new_m_norm_mix_g', 'new_m_w_in', 'new_m_gla_gk_w', 'new_m_gla_gk_b', 'new_m_gla_onorm_g', 'new_m_gla_wo', 'new_m_mla_q_norm_g', 'new_m_mla_wuq', 'new_m_mla_kv_norm_g', 'new_m_mla_wukv', 'new_m_mla_wo', 'new_m_w_out', 'new_m_norm_ffn_g', 'new_m_ffn_w_in', 'new_m_ffn_w_down', 'new_m_final_norm_g', 'new_v_ada_w', 'new_v_ada_b', 'new_v_norm_mix_g', 'new_v_w_in', 'new_v_gla_gk_w', 'new_v_gla_gk_b', 'new_v_gla_onorm_g', 'new_v_gla_wo', 'new_v_mla_q_norm_g', 'new_v_mla_wuq', 'new_v_mla_kv_norm_g', 'new_v_mla_wukv', 'new_v_mla_wo', 'new_v_w_out', 'new_v_norm_ffn_g', 'new_v_ffn_w_in', 'new_v_ffn_w_down', 'new_v_final_norm_g']
TWIN_LEAF_KINDS = {'loss': 'loss', 'grad_x': 'grad_x', 'grad_ada_w': 'grad_w', 'grad_ada_b': 'grad_w', 'grad_norm_mix_g': 'grad_w', 'grad_w_in': 'grad_w', 'grad_gla_gk_w': 'grad_w', 'grad_gla_gk_b': 'grad_w', 'grad_gla_onorm_g': 'grad_w', 'grad_gla_wo': 'grad_w', 'grad_mla_q_norm_g': 'grad_w', 'grad_mla_wuq': 'grad_w', 'grad_mla_kv_norm_g': 'grad_w', 'grad_mla_wukv': 'grad_w', 'grad_mla_wo': 'grad_w', 'grad_w_out': 'grad_w', 'grad_norm_ffn_g': 'grad_w', 'grad_ffn_w_in': 'grad_w', 'grad_ffn_w_down': 'grad_w', 'grad_final_norm_g': 'grad_w', 'delta_ada_w': 'delta_w', 'delta_ada_b': 'delta_w', 'delta_norm_mix_g': 'delta_w', 'delta_w_in': 'delta_w', 'delta_gla_gk_w': 'delta_w', 'delta_gla_gk_b': 'delta_w', 'delta_gla_onorm_g': 'delta_w', 'delta_gla_wo': 'delta_w', 'delta_mla_q_norm_g': 'delta_w', 'delta_mla_wuq': 'delta_w', 'delta_mla_kv_norm_g': 'delta_w', 'delta_mla_wukv': 'delta_w', 'delta_mla_wo': 'delta_w', 'delta_w_out': 'delta_w', 'delta_norm_ffn_g': 'delta_w', 'delta_ffn_w_in': 'delta_w', 'delta_ffn_w_down': 'delta_w', 'delta_final_norm_g': 'delta_w', 'new_m_ada_w': 'new_m', 'new_m_ada_b': 'new_m', 'new_m_norm_mix_g': 'new_m', 'new_m_w_in': 'new_m', 'new_m_gla_gk_w': 'new_m', 'new_m_gla_gk_b': 'new_m', 'new_m_gla_onorm_g': 'new_m', 'new_m_gla_wo': 'new_m', 'new_m_mla_q_norm_g': 'new_m', 'new_m_mla_wuq': 'new_m', 'new_m_mla_kv_norm_g': 'new_m', 'new_m_mla_wukv': 'new_m', 'new_m_mla_wo': 'new_m', 'new_m_w_out': 'new_m', 'new_m_norm_ffn_g': 'new_m', 'new_m_ffn_w_in': 'new_m', 'new_m_ffn_w_down': 'new_m', 'new_m_final_norm_g': 'new_m', 'new_v_ada_w': 'new_v', 'new_v_ada_b': 'new_v', 'new_v_norm_mix_g': 'new_v', 'new_v_w_in': 'new_v', 'new_v_gla_gk_w': 'new_v', 'new_v_gla_gk_b': 'new_v', 'new_v_gla_onorm_g': 'new_v', 'new_v_gla_wo': 'new_v', 'new_v_mla_q_norm_g': 'new_v', 'new_v_mla_wuq': 'new_v', 'new_v_mla_kv_norm_g': 'new_v', 'new_v_mla_wukv': 'new_v', 'new_v_mla_wo': 'new_v', 'new_v_w_out': 'new_v', 'new_v_norm_ffn_g': 'new_v', 'new_v_ffn_w_in': 'new_v', 'new_v_ffn_w_down': 'new_v', 'new_v_final_norm_g': 'new_v'}


def _forward(args):
    return _fwd_reference(*[args[k] for k in FWD_PARAMS])


def _output_shape():
    def fwd():
        inp = _fwd_setup_inputs(0)
        return _fwd_reference(*[inp[k] for k in FWD_PARAMS])
    out = _jax.eval_shape(fwd)
    return out.shape, out.dtype

N_MICROBATCH = 1
ADAM_LR = 0.001
ADAM_B1 = 0.9
ADAM_B2 = 0.999
ADAM_EPS = 1e-08
ADAM_WD = 0.01
ADAM_STEP = 10
PER_EXAMPLE_BATCH_AXIS = {'x': 0, 'c': 0, 'positions': 0, 'loss_target': 0}
SHARED_INPUTS = []
_WEIGHT_DTYPES = {'ada_w': _jnp.float32, 'ada_b': _jnp.float32, 'norm_mix_g': _jnp.float32, 'w_in': _jnp.float32, 'gla_gk_w': _jnp.float32, 'gla_gk_b': _jnp.float32, 'gla_onorm_g': _jnp.float32, 'gla_wo': _jnp.float32, 'mla_q_norm_g': _jnp.float32, 'mla_wuq': _jnp.float32, 'mla_kv_norm_g': _jnp.float32, 'mla_wukv': _jnp.float32, 'mla_wo': _jnp.float32, 'w_out': _jnp.float32, 'norm_ffn_g': _jnp.float32, 'ffn_w_in': _jnp.float32, 'ffn_w_down': _jnp.float32, 'final_norm_g': _jnp.float32}
MOMENT_SCALE = {'ada_w': 3.162062e-02, 'ada_b': 5.489544e-02, 'norm_mix_g': 2.792802e-02, 'w_in': 1.285771e-02, 'gla_gk_w': 4.528887e-03, 'gla_gk_b': 1.337598e-02, 'gla_onorm_g': 2.775088e-02, 'gla_wo': 1.372366e-02, 'mla_q_norm_g': 7.133470e-03, 'mla_wuq': 2.926099e-03, 'mla_kv_norm_g': 1.529496e-02, 'mla_wukv': 5.508016e-03, 'mla_wo': 7.272492e-03, 'w_out': 1.550787e-02, 'norm_ffn_g': 3.653642e-02, 'ffn_w_in': 1.613959e-02, 'ffn_w_down': 2.634206e-02, 'final_norm_g': 3.199453e+01}


def _to_microbatches(a, axis):
    t = _jnp.moveaxis(a, axis, 0)
    t = t.reshape((N_MICROBATCH, t.shape[0] // N_MICROBATCH) + t.shape[1:])
    return _jnp.moveaxis(t, 1, axis + 1)


def setup_inputs(seed: int = 0) -> dict:
    inp = _fwd_setup_inputs(seed)
    key = _jax.random.fold_in(_jax.random.key(seed), 7919)
    shape, _ = _output_shape()
    out = dict(inp)
    out["loss_target"] = _jax.random.normal(_jax.random.fold_in(key, 0), shape, _jnp.float32)
    for i, name in enumerate(TWIN_WEIGHTS):
        w = inp[name].astype(_jnp.float32)
        if MOMENT_SCALE is None:
            s = _jnp.sqrt(_jnp.mean(_jnp.square(w)) + 1e-30)
        else:
            s = MOMENT_SCALE[name]
        km, kv = _jax.random.split(_jax.random.fold_in(key, i + 1))
        out[name] = w
        out["m_" + name] = s * _jax.random.normal(km, w.shape, _jnp.float32)
        out["v_" + name] = (s * s) * _jax.random.uniform(kv, w.shape, _jnp.float32, 0.5, 1.5)
    if N_MICROBATCH > 1:
        for name, axis in PER_EXAMPLE_BATCH_AXIS.items():
            out[name] = _to_microbatches(out[name], axis)
    return {'x': out['x'], 'c': out['c'], 'positions': out['positions'], 'ada_w': out['ada_w'], 'ada_b': out['ada_b'], 'norm_mix_g': out['norm_mix_g'], 'w_in': out['w_in'], 'gla_gk_w': out['gla_gk_w'], 'gla_gk_b': out['gla_gk_b'], 'gla_onorm_g': out['gla_onorm_g'], 'gla_wo': out['gla_wo'], 'mla_q_norm_g': out['mla_q_norm_g'], 'mla_wuq': out['mla_wuq'], 'mla_kv_norm_g': out['mla_kv_norm_g'], 'mla_wukv': out['mla_wukv'], 'mla_wo': out['mla_wo'], 'w_out': out['w_out'], 'norm_ffn_g': out['norm_ffn_g'], 'ffn_w_in': out['ffn_w_in'], 'ffn_w_down': out['ffn_w_down'], 'final_norm_g': out['final_norm_g'], 'loss_target': out['loss_target'], 'm_ada_w': out['m_ada_w'], 'm_ada_b': out['m_ada_b'], 'm_norm_mix_g': out['m_norm_mix_g'], 'm_w_in': out['m_w_in'], 'm_gla_gk_w': out['m_gla_gk_w'], 'm_gla_gk_b': out['m_gla_gk_b'], 'm_gla_onorm_g': out['m_gla_onorm_g'], 'm_gla_wo': out['m_gla_wo'], 'm_mla_q_norm_g': out['m_mla_q_norm_g'], 'm_mla_wuq': out['m_mla_wuq'], 'm_mla_kv_norm_g': out['m_mla_kv_norm_g'], 'm_mla_wukv': out['m_mla_wukv'], 'm_mla_wo': out['m_mla_wo'], 'm_w_out': out['m_w_out'], 'm_norm_ffn_g': out['m_norm_ffn_g'], 'm_ffn_w_in': out['m_ffn_w_in'], 'm_ffn_w_down': out['m_ffn_w_down'], 'm_final_norm_g': out['m_final_norm_g'], 'v_ada_w': out['v_ada_w'], 'v_ada_b': out['v_ada_b'], 'v_norm_mix_g': out['v_norm_mix_g'], 'v_w_in': out['v_w_in'], 'v_gla_gk_w': out['v_gla_gk_w'], 'v_gla_gk_b': out['v_gla_gk_b'], 'v_gla_onorm_g': out['v_gla_onorm_g'], 'v_gla_wo': out['v_gla_wo'], 'v_mla_q_norm_g': out['v_mla_q_norm_g'], 'v_mla_wuq': out['v_mla_wuq'], 'v_mla_kv_norm_g': out['v_mla_kv_norm_g'], 'v_mla_wukv': out['v_mla_wukv'], 'v_mla_wo': out['v_mla_wo'], 'v_w_out': out['v_w_out'], 'v_norm_ffn_g': out['v_norm_ffn_g'], 'v_ffn_w_in': out['v_ffn_w_in'], 'v_ffn_w_down': out['v_ffn_w_down'], 'v_final_norm_g': out['v_final_norm_g']}


def _loss(weights, diff, rest, loss_target):
    with _jax.named_scope("forward"):
        args = {**rest, TWIN_DIFF_INPUT: diff, **{k: w.astype(_WEIGHT_DTYPES[k]) for k, w in weights.items()}}
        y = _forward(args)
    with _jax.named_scope("loss_head"):
        err = _jnp.square(y.astype(_jnp.float32) - loss_target)
        return 0.5 * _jnp.sum(_jnp.mean(err, axis=-1)) if err.ndim else 0.5 * err


def _adamw(w, g, m, v):
    m = ADAM_B1 * m + (1.0 - ADAM_B1) * g
    v = ADAM_B2 * v + (1.0 - ADAM_B2) * _jnp.square(g)
    m_hat = m / (1.0 - ADAM_B1 ** ADAM_STEP)
    v_hat = v / (1.0 - ADAM_B2 ** ADAM_STEP)
    delta = -ADAM_LR * (m_hat / (_jnp.sqrt(v_hat) + ADAM_EPS) + ADAM_WD * w)
    return delta, m, v


def reference(x, c, positions, ada_w, ada_b, norm_mix_g, w_in, gla_gk_w, gla_gk_b, gla_onorm_g, gla_wo, mla_q_norm_g, mla_wuq, mla_kv_norm_g, mla_wukv, mla_wo, w_out, norm_ffn_g, ffn_w_in, ffn_w_down, final_norm_g, loss_target, m_ada_w, m_ada_b, m_norm_mix_g, m_w_in, m_gla_gk_w, m_gla_gk_b, m_gla_onorm_g, m_gla_wo, m_mla_q_norm_g, m_mla_wuq, m_mla_kv_norm_g, m_mla_wukv, m_mla_wo, m_w_out, m_norm_ffn_g, m_ffn_w_in, m_ffn_w_down, m_final_norm_g, v_ada_w, v_ada_b, v_norm_mix_g, v_w_in, v_gla_gk_w, v_gla_gk_b, v_gla_onorm_g, v_gla_wo, v_mla_q_norm_g, v_mla_wuq, v_mla_kv_norm_g, v_mla_wukv, v_mla_wo, v_w_out, v_norm_ffn_g, v_ffn_w_in, v_ffn_w_down, v_final_norm_g):
    given = dict(x=x, c=c, positions=positions, ada_w=ada_w, ada_b=ada_b, norm_mix_g=norm_mix_g, w_in=w_in, gla_gk_w=gla_gk_w, gla_gk_b=gla_gk_b, gla_onorm_g=gla_onorm_g, gla_wo=gla_wo, mla_q_norm_g=mla_q_norm_g, mla_wuq=mla_wuq, mla_kv_norm_g=mla_kv_norm_g, mla_wukv=mla_wukv, mla_wo=mla_wo, w_out=w_out, norm_ffn_g=norm_ffn_g, ffn_w_in=ffn_w_in, ffn_w_down=ffn_w_down, final_norm_g=final_norm_g, loss_target=loss_target, m_ada_w=m_ada_w, m_ada_b=m_ada_b, m_norm_mix_g=m_norm_mix_g, m_w_in=m_w_in, m_gla_gk_w=m_gla_gk_w, m_gla_gk_b=m_gla_gk_b, m_gla_onorm_g=m_gla_onorm_g, m_gla_wo=m_gla_wo, m_mla_q_norm_g=m_mla_q_norm_g, m_mla_wuq=m_mla_wuq, m_mla_kv_norm_g=m_mla_kv_norm_g, m_mla_wukv=m_mla_wukv, m_mla_wo=m_mla_wo, m_w_out=m_w_out, m_norm_ffn_g=m_norm_ffn_g, m_ffn_w_in=m_ffn_w_in, m_ffn_w_down=m_ffn_w_down, m_final_norm_g=m_final_norm_g, v_ada_w=v_ada_w, v_ada_b=v_ada_b, v_norm_mix_g=v_norm_mix_g, v_w_in=v_w_in, v_gla_gk_w=v_gla_gk_w, v_gla_gk_b=v_gla_gk_b, v_gla_onorm_g=v_gla_onorm_g, v_gla_wo=v_gla_wo, v_mla_q_norm_g=v_mla_q_norm_g, v_mla_wuq=v_mla_wuq, v_mla_kv_norm_g=v_mla_kv_norm_g, v_mla_wukv=v_mla_wukv, v_mla_wo=v_mla_wo, v_w_out=v_w_out, v_norm_ffn_g=v_norm_ffn_g, v_ffn_w_in=v_ffn_w_in, v_ffn_w_down=v_ffn_w_down, v_final_norm_g=v_final_norm_g)
    weights = {n: given[n] for n in TWIN_WEIGHTS}
    shared = {n: given[n] for n in SHARED_INPUTS}
    per_example = {n: given[n] for n in ['x', 'c', 'positions']}
    grad_fn = _jax.value_and_grad(_loss, argnums=(0, 1))

    def one_microbatch(ex, loss_target):
        ex = dict(ex)
        diff = ex.pop(TWIN_DIFF_INPUT)
        return grad_fn(weights, diff, {**shared, **ex}, loss_target)

    if N_MICROBATCH == 1:
        loss, (grad_w, grad_x) = one_microbatch(per_example, given["loss_target"])
    else:
        def body(carry, xs):
            loss_sum, grad_sum = carry
            l_k, (gw_k, gx_k) = one_microbatch(xs[0], xs[1])
            with _jax.named_scope("update"):
                return (loss_sum + l_k, _jax.tree.map(_jnp.add, grad_sum, gw_k)), gx_k

        init = (_jnp.zeros((), _jnp.float32), _jax.tree.map(_jnp.zeros_like, weights))
        (loss, grad_w), grad_x = _jax.lax.scan(body, init, (per_example, given["loss_target"]))
    with _jax.named_scope("update"):
        delta_w, new_m, new_v = {}, {}, {}
        for n in TWIN_WEIGHTS:
            delta_w[n], new_m[n], new_v[n] = _adamw(weights[n], grad_w[n], given["m_" + n], given["v_" + n])
    return (loss, grad_x, *[grad_w[n] for n in TWIN_WEIGHTS], *[delta_w[n] for n in TWIN_WEIGHTS],
            *[new_m[n] for n in TWIN_WEIGHTS], *[new_v[n] for n in TWIN_WEIGHTS])
```

```python
import functools

import jax
import jax.numpy as jnp
from jax import lax
from jax.experimental import pallas as pl
from jax.experimental.pallas import tpu as pltpu

F32 = jnp.float32
BF16 = jnp.bfloat16

N_DEV = 8
GLA_HEADS = 4
GLA_DK = 256
GLA_DV = 512
GLA_GATE_RANK = 16
GLA_GATE_NORMALIZER = 16.0
GLA_CHUNK = 64
MLA_HEADS = 16
MLA_NOPE = 128
MLA_ROPE = 64
MLA_V = 128
MLA_QK_PAD = 256
ROPE_THETA = 10000.0
NORM_EPS = 1e-6
ATT_SCALE = (MLA_NOPE + MLA_ROPE) ** -0.5
GLA_QSCALE = GLA_DK ** -0.5

ADAM_LR = 0.001
ADAM_B1 = 0.9
ADAM_B2 = 0.999
ADAM_EPS = 1e-08
ADAM_WD = 0.01
ADAM_STEP = 10

LANE = 128
VMEM_LIMIT = 48 * 1024 * 1024
NEG = -1e30

IN_NAMES = ("q", "k", "v", "g", "gk", "cq", "ckv", "kr", "ga", "gb")
MY_ORDER = ("v", "g", "ga", "gb", "q", "k", "cq", "ckv", "gk", "kr")

WEIGHTS = ("ada_w", "ada_b", "norm_mix_g", "w_in", "gla_gk_w", "gla_gk_b", "gla_onorm_g", "gla_wo",
           "mla_q_norm_g", "mla_wuq", "mla_kv_norm_g", "mla_wukv", "mla_wo", "w_out", "norm_ffn_g",
           "ffn_w_in", "ffn_w_down", "final_norm_g")
BIG = ("w_in", "gla_gk_w", "gla_wo", "mla_wuq", "mla_wukv", "mla_wo", "w_out", "ffn_w_in", "ffn_w_down")
SMALL = ("ada_b", "norm_mix_g", "gla_gk_b", "gla_onorm_g", "mla_q_norm_g", "mla_kv_norm_g", "norm_ffn_g",
         "final_norm_g")


def _in_layout(d):
    w = dict(q=d // 2, k=d // 2, v=d, g=d, gk=GLA_GATE_RANK, cq=d // 4, ckv=512, kr=MLA_ROPE, ga=d, gb=d)
    pw = {n: -(-w[n] // LANE) * LANE for n in w}
    ref_off, o = {}, 0
    for n in IN_NAMES:
        ref_off[n] = o
        o += w[n]
    my_off, o = {}, 0
    for n in MY_ORDER:
        assert o % pw[n] == 0
        my_off[n] = o
        o += pw[n]
    return w, pw, ref_off, my_off, o


def _cparams(sem=None):
    return pltpu.CompilerParams(dimension_semantics=sem, vmem_limit_bytes=VMEM_LIMIT)


def _dot(a, b, ca=1, cb=0):
    return lax.dot_general(a, b, (((ca,), (cb,)), ((), ())), preferred_element_type=F32)


def _tile(n, cap):
    if n <= cap:
        return n
    t = (cap // LANE) * LANE
    while t >= LANE:
        if n % t == 0:
            return t
        t -= LANE
    return n


def _mm(a, b, *, ta=False, tb=False, out_dtype=F32, name):
    m, k = (a.shape[1], a.shape[0]) if ta else a.shape
    n = b.shape[0] if tb else b.shape[1]
    assert k == (b.shape[1] if tb else b.shape[0])
    tm, tn, tk = _tile(m, 1024), _tile(n, 1024), _tile(k, 512)
    nk = k // tk

    def body(a_ref, b_ref, o_ref, acc_ref):
        kk = pl.program_id(2)

        @pl.when(kk == 0)
        def _():
            acc_ref[...] = jnp.zeros_like(acc_ref)

        acc_ref[...] += _dot(a_ref[...].astype(BF16), b_ref[...].astype(BF16), 0 if ta else 1, 1 if tb else 0)

        @pl.when(kk == nk - 1)
        def _():
            o_ref[...] = acc_ref[...].astype(o_ref.dtype)

    a_spec = (pl.BlockSpec((tk, tm), lambda i, j, kk: (kk, i)) if ta
              else pl.BlockSpec((tm, tk), lambda i, j, kk: (i, kk)))
    b_spec = (pl.BlockSpec((tn, tk), lambda i, j, kk: (j, kk)) if tb
              else pl.BlockSpec((tk, tn), lambda i, j, kk: (kk, j)))
    return pl.pallas_call(
        body, name=name, grid=(m // tm, n // tn, nk), in_specs=[a_spec, b_spec],
        out_specs=pl.BlockSpec((tm, tn), lambda i, j, kk: (i, j)),
        out_shape=jax.ShapeDtypeStruct((m, n), out_dtype),
        scratch_shapes=[pltpu.VMEM((tm, tn), F32)],
        compiler_params=_cparams(("parallel", "parallel", "arbitrary")),
    )(a, b)


def _rowwise(fn, rows, vecs, outs, sums=(), *, tile, name):
    t = rows[0][0].shape[0]
    tile = min(tile, t)
    assert t % tile == 0
    n_rows, n_vecs, n_outs = len(rows), len(vecs), len(outs)

    def body(*refs):
        ins = [r[...] for r in refs[:n_rows + n_vecs]]
        res = fn(*ins)
        if not isinstance(res, (tuple, list)):
            res = (res,)
        out_refs = refs[n_rows + n_vecs:]
        for r, val in zip(out_refs[:n_outs], res[:n_outs]):
            r[...] = val.astype(r.dtype)
        if sums:
            first = pl.program_id(0) == 0
            for r, val in zip(out_refs[n_outs:], res[n_outs:]):
                @pl.when(first)
                def _(r=r):
                    r[...] = jnp.zeros_like(r)
                r[...] += val

    in_specs = [pl.BlockSpec((tile, w), lambda i, cb=cb: (i, cb)) for (_, w, cb) in rows]
    in_specs += [pl.BlockSpec(v.shape, lambda i: (0, 0)) for v in vecs]
    out_specs = [pl.BlockSpec((tile, w), lambda i: (i, 0)) for (w, _) in outs]
    out_specs += [pl.BlockSpec((1, w), lambda i: (0, 0)) for w in sums]
    out_shape = [jax.ShapeDtypeStruct((t, w), dt) for (w, dt) in outs]
    out_shape += [jax.ShapeDtypeStruct((1, w), F32) for w in sums]
    res = pl.pallas_call(
        body, name=name, grid=(t // tile,), in_specs=in_specs, out_specs=out_specs, out_shape=out_shape,
        compiler_params=_cparams(("arbitrary",)),
    )(*[r[0] for r in rows], *vecs)
    return res


def _rstd(x):
    return lax.rsqrt(jnp.mean(x * x, axis=-1, keepdims=True) + NORM_EPS)


def _sigmoid(x):
    return 1.0 / (1.0 + jnp.exp(-x))


def _rms_bwd(dxh, xh, r):
    return r * (dxh - xh * jnp.mean(dxh * xh, axis=-1, keepdims=True))


def _rope(t, tab, sign):
    cosf, sin_a, sin_b = tab[:, :LANE], tab[:, LANE:2 * LANE], tab[:, 2 * LANE:]
    return t * cosf + sign * (pltpu.roll(t, 96, 1) * sin_a + pltpu.roll(t, 32, 1) * sin_b)


def _split3(x):
    hi = x.astype(BF16)
    r1 = x - hi.astype(F32)
    mid = r1.astype(BF16)
    lo = (r1 - mid.astype(F32)).astype(BF16)
    return hi, mid, lo


def _tri_sum(tri_bf16, x):
    hi, mid, lo = _split3(x)
    return _dot(tri_bf16, hi) + _dot(tri_bf16, mid) + _dot(tri_bf16, lo)


def _dot_nt2(a, b):
    a_hi = a.astype(BF16)
    a_lo = (a - a_hi.astype(F32)).astype(BF16)
    b_hi = b.astype(BF16)
    b_lo = (b - b_hi.astype(F32)).astype(BF16)
    return _dot(a_hi, b_hi, 1, 1) + _dot(a_hi, b_lo, 1, 1) + _dot(a_lo, b_hi, 1, 1)


def _gla_specs(t, rows, lay, reverse):
    nb = t // rows
    blk = (lambda i: nb - 1 - i) if reverse else (lambda i: i)
    qb, kb = lay["q"] // GLA_DK, lay["k"] // GLA_DK
    vb = lay["v"] // GLA_DV
    return [
        pl.BlockSpec((rows, GLA_DK), lambda h, i: (blk(i), qb + h)),
        pl.BlockSpec((rows, GLA_DK), lambda h, i: (blk(i), kb + h)),
        pl.BlockSpec((rows, GLA_DV), lambda h, i: (blk(i), vb + h)),
        pl.BlockSpec((rows, GLA_DK), lambda h, i: (blk(i), h)),
    ], blk


def _gla_fwd(p, la, lay):
    t = p.shape[0]
    rows = min(512, t)
    nb, nc = t // rows, rows // GLA_CHUNK
    c64 = GLA_CHUNK

    def body(q_ref, k_ref, v_ref, la_ref, o_ref, st_ref, s_ref):
        @pl.when(pl.program_id(1) == 0)
        def _():
            s_ref[...] = jnp.zeros_like(s_ref)

        r = lax.broadcasted_iota(jnp.int32, (c64, c64), 0)
        cc = lax.broadcasted_iota(jnp.int32, (c64, c64), 1)
        tril = cc <= r
        tril_b = tril.astype(BF16)
        for c in range(nc):
            sl = pl.ds(c * c64, c64)
            b = _tri_sum(tril_b, la_ref[sl, :])
            b_last = b[c64 - 1:c64, :]
            q = q_ref[sl, :] * GLA_QSCALE
            k = k_ref[sl, :]
            v = v_ref[sl, :].astype(BF16)
            qt_f = q * jnp.exp(b)
            qt = qt_f.astype(BF16)
            kh = (k * jnp.exp(b_last - b)).astype(BF16)
            s_prev = s_ref[...]
            st_ref[0, c] = s_prev
            att = jnp.where(tril, _dot_nt2(qt_f, k * jnp.exp(-b)), 0.0)
            o_ref[sl, :] = _dot(qt, s_prev.astype(BF16), 1, 1) + _dot(att.astype(BF16), v)
            s_ref[...] = s_prev * jnp.exp(b_last) + _dot(v, kh, 0, 0)

    in_specs, _ = _gla_specs(t, rows, lay, False)
    return pl.pallas_call(
        body, name="gla_fwd", grid=(GLA_HEADS, nb), in_specs=in_specs,
        out_specs=[pl.BlockSpec((rows, GLA_DV), lambda h, i: (i, h)),
                   pl.BlockSpec((1, nc, GLA_DV, GLA_DK), lambda h, i: (h, i, 0, 0))],
        out_shape=[jax.ShapeDtypeStruct((t, GLA_HEADS * GLA_DV), F32),
                   jax.ShapeDtypeStruct((GLA_HEADS, t // c64, GLA_DV, GLA_DK), F32)],
        scratch_shapes=[pltpu.VMEM((GLA_DV, GLA_DK), F32)],
        compiler_params=_cparams(("parallel", "arbitrary")),
    )(p, p, p, la)


def _gla_bwd(p, la, do, states, lay):
    t = p.shape[0]
    rows = min(512, t)
    nb, nc = t // rows, rows // GLA_CHUNK
    c64 = GLA_CHUNK

    def body(q_ref, k_ref, v_ref, la_ref, do_ref, st_ref, dq_ref, dk_ref, dv_ref, dla_ref, ds_ref):
        @pl.when(pl.program_id(1) == 0)
        def _():
            ds_ref[...] = jnp.zeros_like(ds_ref)

        r = lax.broadcasted_iota(jnp.int32, (c64, c64), 0)
        cc = lax.broadcasted_iota(jnp.int32, (c64, c64), 1)
        tril = cc <= r
        tril_b = tril.astype(BF16)
        triu_b = (cc >= r).astype(BF16)
        for c in reversed(range(nc)):
            sl = pl.ds(c * c64, c64)
            b = _tri_sum(tril_b, la_ref[sl, :])
            b_last = b[c64 - 1:c64, :]
            eb, enb, ebl_b, ebl = jnp.exp(b), jnp.exp(-b), jnp.exp(b_last - b), jnp.exp(b_last)
            k = k_ref[sl, :]
            qt_f = q_ref[sl, :] * GLA_QSCALE * eb
            kt_f = k * enb
            kh_f = k * ebl_b
            qt, kt, kh = qt_f.astype(BF16), kt_f.astype(BF16), kh_f.astype(BF16)
            v_f = v_ref[sl, :]
            dout_f = do_ref[sl, :]
            v, dout = v_f.astype(BF16), dout_f.astype(BF16)
            s_prev = st_ref[0, c]
            ds_next = ds_ref[...]
            ds_next_b = ds_next.astype(BF16)
            att = jnp.where(tril, _dot_nt2(qt_f, kt_f), 0.0).astype(BF16)
            datt = jnp.where(tril, _dot_nt2(dout_f, v_f), 0.0).astype(BF16)
            dqt = _dot(dout, s_prev.astype(BF16)) + _dot(datt, kt)
            dkt = _dot(datt, qt, 0, 0)
            dv = _dot(att, dout, 0, 0) + _dot(kh, ds_next_b, 1, 1)
            dkh = _dot(v, ds_next_b)
            d_ebl = jnp.sum(ds_next * s_prev, axis=0, keepdims=True)
            ds_ref[...] = ds_next * ebl + _dot(dout, qt, 0, 0)
            db = dqt * qt_f - dkt * kt_f - dkh * kh_f
            db_last = ebl * d_ebl + jnp.sum(dkh * kh_f, axis=0, keepdims=True)
            dq_ref[sl, :] = (dqt * eb * GLA_QSCALE).astype(dq_ref.dtype)
            dk_ref[sl, :] = (dkt * enb + dkh * ebl_b).astype(dk_ref.dtype)
            dv_ref[sl, :] = dv.astype(dv_ref.dtype)
            dla_ref[sl, :] = _tri_sum(triu_b, db) + db_last

    in_specs, blk = _gla_specs(t, rows, lay, True)
    in_specs += [pl.BlockSpec((rows, GLA_DV), lambda h, i: (blk(i), h)),
                 pl.BlockSpec((1, nc, GLA_DV, GLA_DK), lambda h, i: (h, blk(i), 0, 0))]
    dk_spec = pl.BlockSpec((rows, GLA_DK), lambda h, i: (blk(i), h))
    return pl.pallas_call(
        body, name="gla_bwd", grid=(GLA_HEADS, nb), in_specs=in_specs,
        out_specs=[dk_spec, dk_spec, pl.BlockSpec((rows, GLA_DV), lambda h, i: (blk(i), h)), dk_spec],
        out_shape=[jax.ShapeDtypeStruct((t, GLA_HEADS * GLA_DK), BF16),
                   jax.ShapeDtypeStruct((t, GLA_HEADS * GLA_DK), BF16),
                   jax.ShapeDtypeStruct((t, GLA_HEADS * GLA_DV), BF16),
                   jax.ShapeDtypeStruct((t, GLA_HEADS * GLA_DK), F32)],
        scratch_shapes=[pltpu.VMEM((GLA_DV, GLA_DK), F32)],
        compiler_params=_cparams(("parallel", "arbitrary")),
    )(p, p, p, la, do, states)


def _causal(i, j, tq):
    row = i * tq + lax.broadcasted_iota(jnp.int32, (tq, tq), 0)
    col = j * tq + lax.broadcasted_iota(jnp.int32, (tq, tq), 1)
    return col <= row


def _flash_fwd(q, k, v):
    t = q.shape[0]
    tq = min(512, t)
    nq = t // tq
    dqk, dv = MLA_QK_PAD, MLA_V

    def body(q_ref, k_ref, v_ref, o_ref, lse_ref, m_ref, l_ref, acc_ref):
        i = pl.program_id(1)
        m_ref[...] = jnp.full_like(m_ref, NEG)
        l_ref[...] = jnp.zeros_like(l_ref)
        acc_ref[...] = jnp.zeros_like(acc_ref)
        qb = q_ref[...]

        def step(j, carry):
            off = pl.multiple_of(j * tq, tq)
            s = _dot(qb, k_ref[pl.ds(off, tq), :], 1, 1) * ATT_SCALE
            s = jnp.where(_causal(i, j, tq), s, NEG)
            m_old = m_ref[...]
            m_new = jnp.maximum(m_old, jnp.max(s, axis=1, keepdims=True))
            pr = jnp.exp(s - m_new)
            alpha = jnp.exp(m_old - m_new)
            l_ref[...] = alpha * l_ref[...] + jnp.sum(pr, axis=1, keepdims=True)
            acc_ref[...] = alpha * acc_ref[...] + _dot(pr.astype(BF16), v_ref[pl.ds(off, tq), :])
            m_ref[...] = m_new
            return carry

        lax.fori_loop(0, i + 1, step, 0)
        o_ref[...] = acc_ref[...] / l_ref[...]
        lse_ref[0] = m_ref[...] + jnp.log(l_ref[...])

    return pl.pallas_call(
        body, name="mla_flash_fwd", grid=(MLA_HEADS, nq),
        in_specs=[pl.BlockSpec((tq, dqk), lambda h, i: (i, h)),
                  pl.BlockSpec((t, dqk), lambda h, i: (0, h)),
                  pl.BlockSpec((t, dv), lambda h, i: (0, h))],
        out_specs=[pl.BlockSpec((tq, dv), lambda h, i: (i, h)),
                   pl.BlockSpec((1, tq, 1), lambda h, i: (h, i, 0))],
        out_shape=[jax.ShapeDtypeStruct((t, MLA_HEADS * dv), F32),
                   jax.ShapeDtypeStruct((MLA_HEADS, t, 1), F32)],
        scratch_shapes=[pltpu.VMEM((tq, 1), F32), pltpu.VMEM((tq, 1), F32), pltpu.VMEM((tq, dv), F32)],
        compiler_params=_cparams(("parallel", "arbitrary")),
    )(q, k, v)


def _flash_bwd_dq(q, k, v, do, o, lse):
    t = q.shape[0]
    tq = min(512, t)
    nq = t // tq
    dqk, dv = MLA_QK_PAD, MLA_V

    def body(q_ref, k_ref, v_ref, do_ref, o_ref, lse_ref, dq_ref, dl_ref, acc_ref):
        i = pl.program_id(1)
        qb = q_ref[...]
        dout = do_ref[...]
        delta = jnp.sum(dout.astype(F32) * o_ref[...], axis=1, keepdims=True)
        dl_ref[0] = delta
        lse_b = lse_ref[0]
        acc_ref[...] = jnp.zeros_like(acc_ref)

        def step(j, carry):
            off = pl.multiple_of(j * tq, tq)
            kb = k_ref[pl.ds(off, tq), :]
            s = _dot(qb, kb, 1, 1) * ATT_SCALE
            pr = jnp.exp(jnp.where(_causal(i, j, tq), s, NEG) - lse_b)
            dp = _dot(dout, v_ref[pl.ds(off, tq), :], 1, 1)
            ds = pr * (dp - delta) * ATT_SCALE
            acc_ref[...] += _dot(ds.astype(BF16), kb)
            return carry

        lax.fori_loop(0, i + 1, step, 0)
        dq_ref[...] = acc_ref[...]

    return pl.pallas_call(
        body, name="mla_flash_bwd_dq", grid=(MLA_HEADS, nq),
        in_specs=[pl.BlockSpec((tq, dqk), lambda h, i: (i, h)),
                  pl.BlockSpec((t, dqk), lambda h, i: (0, h)),
                  pl.BlockSpec((t, dv), lambda h, i: (0, h)),
                  pl.BlockSpec((tq, dv), lambda h, i: (i, h)),
                  pl.BlockSpec((tq, dv), lambda h, i: (i, h)),
                  pl.BlockSpec((1, tq, 1), lambda h, i: (h, i, 0))],
        out_specs=[pl.BlockSpec((tq, dqk), lambda h, i: (i, h)),
                   pl.BlockSpec((1, tq, 1), lambda h, i: (h, i, 0))],
        out_shape=[jax.ShapeDtypeStruct((t, MLA_HEADS * dqk), F32),
                   jax.ShapeDtypeStruct((MLA_HEADS, t, 1), F32)],
        scratch_shapes=[pltpu.VMEM((tq, dqk), F32)],
        compiler_params=_cparams(("parallel", "arbitrary")),
    )(q, k, v, do, o, lse)


def _flash_bwd_dkv(q, k, v, do, lse, delta):
    t = q.shape[0]
    tq = min(512, t)
    nq = t // tq
    dqk, dv = MLA_QK_PAD, MLA_V

    def body(k_ref, v_ref, q_ref, do_ref, lse_ref, dl_ref, dk_ref, dv_ref, dk_acc, dv_acc):
        j = pl.program_id(1)
        kb = k_ref[...]
        vb = v_ref[...]
        dk_acc[...] = jnp.zeros_like(dk_acc)
        dv_acc[...] = jnp.zeros_like(dv_acc)

        def step(i, carry):
            off = pl.multiple_of(i * tq, tq)
            qb = q_ref[pl.ds(off, tq), :]
            dout = do_ref[pl.ds(off, tq), :]
            s = _dot(qb, kb, 1, 1) * ATT_SCALE
            pr = jnp.exp(jnp.where(_causal(i, j, tq), s, NEG) - lse_ref[0, pl.ds(off, tq), :])
            dv_acc[...] += _dot(pr.astype(BF16), dout, 0, 0)
            dp = _dot(dout, vb, 1, 1)
            ds = pr * (dp - dl_ref[0, pl.ds(off, tq), :]) * ATT_SCALE
            dk_acc[...] += _dot(ds.astype(BF16), qb, 0, 0)
            return carry

        lax.fori_loop(j, nq, step, 0)
        dk_ref[...] = dk_acc[...]
        dv_ref[...] = dv_acc[...]

    return pl.pallas_call(
        body, name="mla_flash_bwd_dkv", grid=(MLA_HEADS, nq),
        in_specs=[pl.BlockSpec((tq, dqk), lambda h, j: (j, h)),
                  pl.BlockSpec((tq, dv), lambda h, j: (j, h)),
                  pl.BlockSpec((t, dqk), lambda h, j: (0, h)),
                  pl.BlockSpec((t, dv), lambda h, j: (0, h)),
                  pl.BlockSpec((1, t, 1), lambda h, j: (h, 0, 0)),
                  pl.BlockSpec((1, t, 1), lambda h, j: (h, 0, 0))],
        out_specs=[pl.BlockSpec((tq, dqk), lambda h, j: (j, h)),
                   pl.BlockSpec((tq, dv), lambda h, j: (j, h))],
        out_shape=[jax.ShapeDtypeStruct((t, MLA_HEADS * dqk), F32),
                   jax.ShapeDtypeStruct((t, MLA_HEADS * dv), F32)],
        scratch_shapes=[pltpu.VMEM((tq, dqk), F32), pltpu.VMEM((tq, dv), F32)],
        compiler_params=_cparams(("parallel", "arbitrary")),
    )(k, v, q, do, lse, delta)


def _local_step(x, target, tab, mod8, w):
    t, d = x.shape
    _, pw, _, lay, _ = _in_layout(d)
    ffn = w["ffn_w_down"].shape[0]

    def blk(arr, name):
        return (arr, pw[name], lay[name] // pw[name])

    def full(arr):
        return (arr, arr.shape[1], 0)

    g1, g2, g3 = w["norm_mix_g"], w["norm_ffn_g"], w["final_norm_g"]

    def f_ln1(xv, mod, g):
        return (xv * _rstd(xv) * g) * (1.0 + mod[1:2]) + mod[0:1]

    (h,) = _rowwise(f_ln1, [full(x)], [mod8, g1], [(d, BF16)], tile=256, name="ln1_modulate")
    p = _mm(h, w["w_in"], name="mm_in_proj")

    def f_gk(pgk, gkw, gkb):
        z = _dot(pgk.astype(BF16), gkw.astype(BF16)) + gkb
        return (jnp.minimum(z, 0.0) - jnp.log(1.0 + jnp.exp(-jnp.abs(z)))) / GLA_GATE_NORMALIZER

    (la,) = _rowwise(f_gk, [blk(p, "gk")], [w["gla_gk_w"], w["gla_gk_b"]], [(GLA_HEADS * GLA_DK, F32)],
                     tile=512, name="gla_gate")
    o_gla, states = _gla_fwd(p, la, lay)

    def f_gla_out(ov, pg, g):
        parts = []
        for hh in range(GLA_HEADS):
            oh = ov[:, hh * GLA_DV:(hh + 1) * GLA_DV]
            ph = pg[:, hh * GLA_DV:(hh + 1) * GLA_DV]
            parts.append(oh * _rstd(oh) * g * (ph * _sigmoid(ph)))
        return jnp.concatenate(parts, axis=1)

    (o_n,) = _rowwise(f_gla_out, [full(o_gla), blk(p, "g")], [w["gla_onorm_g"]], [(d, BF16)], tile=256,
                      name="gla_out_norm")
    y_gla = _mm(o_n, w["gla_wo"], name="mm_gla_wo")

    def f_mla_prep(cq, ckv, kr, tb, gq, gkv):
        return cq * _rstd(cq) * gq, ckv * _rstd(ckv) * gkv, _rope(kr, tb, 1.0)

    cqn, ckvn, krr = _rowwise(f_mla_prep, [blk(p, "cq"), blk(p, "ckv"), blk(p, "kr"), full(tab)],
                              [w["mla_q_norm_g"], w["mla_kv_norm_g"]],
                              [(pw["cq"], BF16), (pw["ckv"], BF16), (LANE, F32)], tile=512, name="mla_prep")
    qlat = _mm(cqn, w["mla_wuq"], name="mm_mla_wuq")
    kvl = _mm(ckvn, w["mla_wukv"], name="mm_mla_wukv")
    hv = MLA_HEADS * MLA_V

    def f_qkv(ql, kn, vv, kr, tb):
        qs, ks = [], []
        for hh in range(MLA_HEADS):
            o0 = hh * MLA_QK_PAD
            qs += [ql[:, o0:o0 + LANE], _rope(ql[:, o0 + LANE:o0 + 2 * LANE], tb, 1.0)]
            ks += [kn[:, hh * LANE:(hh + 1) * LANE], kr]
        return jnp.concatenate(qs, axis=1), jnp.concatenate(ks, axis=1), vv

    qa, ka, va = _rowwise(f_qkv, [full(qlat), (kvl, hv, 0), (kvl, hv, 1), full(krr), full(tab)], [],
                          [(MLA_HEADS * MLA_QK_PAD, BF16), (MLA_HEADS * MLA_QK_PAD, BF16), (hv, BF16)],
                          tile=256, name="mla_qkv_build")
    o_mla, lse = _flash_fwd(qa, ka, va)
    y_mla = _mm(o_mla, w["mla_wo"], name="mm_mla_wo")

    def f_merge(yg, ym, ga, gb):
        return _sigmoid(ga) * yg + _sigmoid(gb) * ym

    (merged,) = _rowwise(f_merge, [full(y_gla), full(y_mla), blk(p, "ga"), blk(p, "gb")], [], [(d, BF16)],
                         tile=256, name="merge")
    mix = _mm(merged, w["w_out"], name="mm_w_out")

    def f_res_ln2(xv, mx, mod, g):
        x2v = xv + mod[2:3] * mx
        return x2v, (x2v * _rstd(x2v) * g) * (1.0 + mod[4:5]) + mod[3:4]

    x2, h2 = _rowwise(f_res_ln2, [full(x), full(mix)], [mod8, g2], [(d, F32), (d, BF16)], tile=256,
                      name="res_ln2_modulate")
    gu = _mm(h2, w["ffn_w_in"], name="mm_ffn_in")

    def f_swiglu(gv, uv):
        return gv * _sigmoid(gv) * uv

    (act,) = _rowwise(f_swiglu, [(gu, ffn, 0), (gu, ffn, 1)], [], [(ffn, BF16)], tile=128, name="swiglu")
    f_out = _mm(act, w["ffn_w_down"], name="mm_ffn_down")

    def f_head(x2v, fv, tg, mod, g):
        x3 = x2v + mod[5:6] * fv
        r = _rstd(x3)
        xh = x3 * r
        e = xh * g - tg
        loss_rows = 0.5 * jnp.mean(e * e, axis=-1, keepdims=True)
        dy = e * (1.0 / d)
        dx3 = _rms_bwd(dy * g, xh, r)
        loss = jnp.broadcast_to(jnp.sum(loss_rows, axis=0, keepdims=True), (1, LANE))
        return (dx3, dx3 * mod[5:6], loss, jnp.sum(dy * xh, axis=0, keepdims=True),
                jnp.sum(dx3 * fv, axis=0, keepdims=True))

    dx3, df, loss_v, dg3, dgate_f = _rowwise(f_head, [full(x2), full(f_out), full(target)], [mod8, g3],
                                             [(d, F32), (d, BF16)], [LANE, d, d], tile=256, name="loss_head")
    da = _mm(df, w["ffn_w_down"], tb=True, name="mm_ffn_down_dx")
    g_ffn_down = _mm(act, df, ta=True, name="mm_ffn_down_dw")

    def f_swiglu_bwd(gv, uv, dav):
        sg = _sigmoid(gv)
        return jnp.concatenate([dav * uv * (sg * (1.0 + gv * (1.0 - sg))), dav * (gv * sg)], axis=1)

    (dgu,) = _rowwise(f_swiglu_bwd, [(gu, ffn, 0), (gu, ffn, 1), full(da)], [], [(2 * ffn, BF16)], tile=128,
                      name="swiglu_bwd")
    dh2 = _mm(dgu, w["ffn_w_in"], tb=True, name="mm_ffn_in_dx")
    g_ffn_in = _mm(h2, dgu, ta=True, name="mm_ffn_in_dw")

    def f_ln2_bwd(x2v, dh, dx3v, mx, mod, g):
        r = _rstd(x2v)
        xh = x2v * r
        dn = dh * (1.0 + mod[4:5])
        dx2 = dx3v + _rms_bwd(dn * g, xh, r)
        return (dx2, dx2 * mod[2:3],
                jnp.sum(dh * (xh * g), axis=0, keepdims=True), jnp.sum(dh, axis=0, keepdims=True),
                jnp.sum(dn * xh, axis=0, keepdims=True), jnp.sum(dx2 * mx, axis=0, keepdims=True))

    dx2, dmix, dscale_f, dshift_f, dg2, dgate_m = _rowwise(
        f_ln2_bwd, [full(x2), full(dh2), full(dx3), full(mix)], [mod8, g2], [(d, F32), (d, BF16)],
        [d, d, d, d], tile=256, name="ln2_bwd")
    dmerged = _mm(dmix, w["w_out"], tb=True, name="mm_w_out_dx")
    g_w_out = _mm(merged, dmix, ta=True, name="mm_w_out_dw")

    def f_merge_bwd(dm, yg, ym, ga, gb):
        sa, sb = _sigmoid(ga), _sigmoid(gb)
        return dm * sa, dm * sb, dm * yg * sa * (1.0 - sa), dm * ym * sb * (1.0 - sb)

    dy_gla, dy_mla, dp_ga, dp_gb = _rowwise(
        f_merge_bwd, [full(dmerged), full(y_gla), full(y_mla), blk(p, "ga"), blk(p, "gb")], [],
        [(d, BF16)] * 4, tile=256, name="merge_bwd")
    do_n = _mm(dy_gla, w["gla_wo"], tb=True, name="mm_gla_wo_dx")
    g_gla_wo = _mm(o_n, dy_gla, ta=True, name="mm_gla_wo_dw")
    do_m = _mm(dy_mla, w["mla_wo"], tb=True, out_dtype=BF16, name="mm_mla_wo_dx")
    g_mla_wo = _mm(o_mla, dy_mla, ta=True, name="mm_mla_wo_dw")

    def f_gla_out_bwd(don, ov, pg, g):
        dos, dpgs = [], []
        dg = jnp.zeros((1, GLA_DV), F32)
        for hh in range(GLA_HEADS):
            sl = slice(hh * GLA_DV, (hh + 1) * GLA_DV)
            oh, ph, dn = ov[:, sl], pg[:, sl], don[:, sl]
            r = _rstd(oh)
            xh = oh * r
            sg = _sigmoid(ph)
            dpre = dn * (ph * sg)
            dg = dg + jnp.sum(dpre * xh, axis=0, keepdims=True)
            dos.append(_rms_bwd(dpre * g, xh, r))
            dpgs.append(dn * (xh * g) * (sg * (1.0 + ph * (1.0 - sg))))
        return jnp.concatenate(dos, axis=1), jnp.concatenate(dpgs, axis=1), dg

    do_gla, dp_g, dg_on = _rowwise(f_gla_out_bwd, [full(do_n), full(o_gla), blk(p, "g")], [w["gla_onorm_g"]],
                                   [(d, F32), (d, BF16)], [GLA_DV], tile=256, name="gla_out_norm_bwd")
    dp_q, dp_k, dp_v, dla = _gla_bwd(p, la, do_gla, states, lay)

    def f_gk_bwd(dlav, pgk, gkw, gkb):
        z = _dot(pgk.astype(BF16), gkw.astype(BF16)) + gkb
        dz = dlav * (1.0 / GLA_GATE_NORMALIZER) * _sigmoid(-z)
        return dz, _dot(dz.astype(BF16), gkw.astype(BF16), 1, 1), jnp.sum(dz, axis=0, keepdims=True)

    dz, dp_gk, dgk_b = _rowwise(f_gk_bwd, [full(dla), blk(p, "gk")], [w["gla_gk_w"], w["gla_gk_b"]],
                                [(GLA_HEADS * GLA_DK, BF16), (LANE, BF16)], [GLA_HEADS * GLA_DK], tile=512,
                                name="gla_gate_bwd")
    p_gk = lax.slice_in_dim(p, lay["gk"], lay["gk"] + LANE, axis=1)
    g_gk_w = _mm(p_gk, dz, ta=True, name="mm_gla_gk_dw")[:GLA_GATE_RANK]

    dqa, delta = _flash_bwd_dq(qa, ka, va, do_m, o_mla, lse)
    dka, dva = _flash_bwd_dkv(qa, ka, va, do_m, lse, delta)

    def f_qkv_bwd(dq, dk, dvv, tb):
        dqs, dkn = [], []
        dkr = jnp.zeros((dq.shape[0], LANE), F32)
        for hh in range(MLA_HEADS):
            o0 = hh * MLA_QK_PAD
            dqs += [dq[:, o0:o0 + LANE], _rope(dq[:, o0 + LANE:o0 + 2 * LANE], tb, -1.0)]
            dkn.append(dk[:, o0:o0 + LANE])
            dkr = dkr + dk[:, o0 + LANE:o0 + 2 * LANE]
        return jnp.concatenate(dqs, axis=1), jnp.concatenate(dkn + [dvv], axis=1), dkr

    dqlat, dkvl, dkrr = _rowwise(f_qkv_bwd, [full(dqa), full(dka), full(dva), full(tab)], [],
                                 [(MLA_HEADS * MLA_QK_PAD, BF16), (2 * hv, BF16), (LANE, F32)], tile=256,
                                 name="mla_qkv_build_bwd")
    dcqn = _mm(dqlat, w["mla_wuq"], tb=True, name="mm_mla_wuq_dx")
    g_wuq = _mm(cqn, dqlat, ta=True, name="mm_mla_wuq_dw")
    dckvn = _mm(dkvl, w["mla_wukv"], tb=True, name="mm_mla_wukv_dx")
    g_wukv = _mm(ckvn, dkvl, ta=True, name="mm_mla_wukv_dw")

    def f_mla_prep_bwd(dq, dkv, dkr, cq, ckv, tb, gq, gkv):
        rq, rk = _rstd(cq), _rstd(ckv)
        xq, xk = cq * rq, ckv * rk
        return (_rms_bwd(dq * gq, xq, rq), _rms_bwd(dkv * gkv, xk, rk), _rope(dkr, tb, -1.0),
                jnp.sum(dq * xq, axis=0, keepdims=True), jnp.sum(dkv * xk, axis=0, keepdims=True))

    dp_cq, dp_ckv, dp_kr, dg_q, dg_kv = _rowwise(
        f_mla_prep_bwd, [full(dcqn), full(dckvn), full(dkrr), blk(p, "cq"), blk(p, "ckv"), full(tab)],
        [w["mla_q_norm_g"], w["mla_kv_norm_g"]], [(pw["cq"], BF16), (pw["ckv"], BF16), (LANE, BF16)],
        [pw["cq"], pw["ckv"]], tile=512, name="mla_prep_bwd")

    pieces = dict(v=dp_v, g=dp_g, ga=dp_ga, gb=dp_gb, q=dp_q, k=dp_k, cq=dp_cq, ckv=dp_ckv, gk=dp_gk, kr=dp_kr)
    dp = jnp.concatenate([pieces[n] for n in MY_ORDER], axis=1)
    dh = _mm(dp, w["w_in"], tb=True, name="mm_in_proj_dx")
    g_w_in = _mm(h, dp, ta=True, name="mm_in_proj_dw")

    def f_ln1_bwd(xv, dhv, dx2v, mod, g):
        r = _rstd(xv)
        xh = xv * r
        dn = dhv * (1.0 + mod[1:2])
        return (dx2v + _rms_bwd(dn * g, xh, r),
                jnp.sum(dhv * (xh * g), axis=0, keepdims=True), jnp.sum(dhv, axis=0, keepdims=True),
                jnp.sum(dn * xh, axis=0, keepdims=True))

    grad_x, dscale_m, dshift_m, dg1 = _rowwise(f_ln1_bwd, [full(x), full(dh), full(dx2)], [mod8, g1],
                                               [(d, F32)], [d, d, d], tile=256, name="ln1_bwd")

    dmod = jnp.concatenate([dshift_m, dscale_m, dgate_m, dshift_f, dscale_f, dgate_f], axis=1)
    big = dict(w_in=g_w_in, gla_gk_w=g_gk_w, gla_wo=g_gla_wo, mla_wuq=g_wuq, mla_wukv=g_wukv,
               mla_wo=g_mla_wo, w_out=g_w_out, ffn_w_in=g_ffn_in, ffn_w_down=g_ffn_down)
    small = dict(ada_b=dmod, norm_mix_g=dg1, gla_gk_b=dgk_b, gla_onorm_g=dg_on, mla_q_norm_g=dg_q,
                 mla_kv_norm_g=dg_kv, norm_ffn_g=dg2, final_norm_g=dg3)
    return loss_v[0, 0], grad_x, big, small


def _exchange(arrs, *, scatter, name):
    n = len(arrs)
    n_peer = N_DEV - 1

    def body(*refs):
        ins, outs = refs[:n], refs[n:2 * n]
        send_sems, recv_sems, local_sems = refs[2 * n:]
        x, y, c = lax.axis_index("x"), lax.axis_index("y"), lax.axis_index("c")
        me = 4 * x + 2 * y + c
        peers = []
        for rel in range(1, N_DEV):
            px = 1 - x if rel & 4 else x
            py = 1 - y if rel & 2 else y
            pc = 1 - c if rel & 1 else c
            peers.append(((px, py, pc), 4 * px + 2 * py + pc))

        def remote(a, k, src_slot, dst_slot):
            src = ins[a].at[src_slot] if scatter else ins[a]
            return pltpu.make_async_remote_copy(
                src_ref=src, dst_ref=outs[a].at[dst_slot], send_sem=send_sems.at[a * n_peer + k],
                recv_sem=recv_sems.at[a * n_peer + k], device_id=peers[k][0], device_id_type=pl.DeviceIdType.MESH)

        local, sends = [], []
        for a in range(n):
            src = ins[a].at[me] if scatter else ins[a]
            cp = pltpu.make_async_copy(src, outs[a].at[me], local_sems.at[a])
            cp.start()
            local.append(cp)
            for k in range(n_peer):
                cp = remote(a, k, peers[k][1], me)
                cp.start()
                sends.append(cp)
        for a in range(n):
            for k in range(n_peer):
                remote(a, k, peers[k][1], peers[k][1]).wait_recv()
        for cp in sends:
            cp.wait_send()
        for cp in local:
            cp.wait()

    hbm = pl.BlockSpec(memory_space=pltpu.HBM)
    out_shape = [jax.ShapeDtypeStruct(a.shape if scatter else (N_DEV,) + a.shape, a.dtype) for a in arrs]
    return pl.pallas_call(
        body, name=name, in_specs=[hbm] * n, out_specs=[hbm] * n, out_shape=out_shape,
        scratch_shapes=[pltpu.SemaphoreType.DMA((n * n_peer,)), pltpu.SemaphoreType.DMA((n * n_peer,)),
                        pltpu.SemaphoreType.DMA((n,))],
    )(*arrs)


def _adamw_math(w, g, m, v):
    m_new = ADAM_B1 * m + (1.0 - ADAM_B1) * g
    v_new = ADAM_B2 * v + (1.0 - ADAM_B2) * (g * g)
    m_hat = m_new / (1.0 - ADAM_B1 ** ADAM_STEP)
    v_hat = v_new / (1.0 - ADAM_B2 ** ADAM_STEP)
    delta = -ADAM_LR * (m_hat / (jnp.sqrt(v_hat) + ADAM_EPS) + ADAM_WD * w)
    return delta, m_new, v_new


def _adamw(w, g, m, v, *, name):
    r, c = w.shape
    slots = g.ndim == 3
    tr = r
    for cand in (128, 64, 32, 16):
        if r % cand == 0 and r > cand:
            tr = cand
            break

    def body(w_ref, g_ref, m_ref, v_ref, go_ref, d_ref, mo_ref, vo_ref):
        if slots:
            gv = g_ref[0].astype(F32)
            for s in range(1, N_DEV):
                gv = gv + g_ref[s].astype(F32)
        else:
            gv = g_ref[...]
        delta, m_new, v_new = _adamw_math(w_ref[...], gv, m_ref[...], v_ref[...])
        go_ref[...] = gv
        d_ref[...] = delta
        mo_ref[...] = m_new
        vo_ref[...] = v_new

    spec = pl.BlockSpec((tr, c), lambda i: (i, 0))
    g_spec = pl.BlockSpec((N_DEV, tr, c), lambda i: (0, i, 0)) if slots else spec
    return pl.pallas_call(
        body, name=name, grid=(r // tr,), in_specs=[spec, g_spec, spec, spec], out_specs=[spec] * 4,
        out_shape=[jax.ShapeDtypeStruct((r, c), F32)] * 4, compiler_params=_cparams(("parallel",)),
    )(w, g, m, v)


def _unshard_cols(g):
    return jnp.transpose(g, (1, 0, 2)).reshape(g.shape[1], -1)

def _shard_cols(full):
    r = full.shape[0]
    return jnp.transpose(full.reshape(r, N_DEV, -1), (1, 0, 2))


def _w_in_to_mine(w_ref_layout, d):
    wd, pw, ref_off, _, _ = _in_layout(d)
    cols = []
    for n in MY_ORDER:
        piece = lax.slice_in_dim(w_ref_layout, ref_off[n], ref_off[n] + wd[n], axis=1)
        if pw[n] != wd[n]:
            piece = jnp.pad(piece, ((0, 0), (0, pw[n] - wd[n])))
        cols.append(piece)
    return jnp.concatenate(cols, axis=1)


def _w_in_from_mine(g_mine, d):
    wd, _, _, my_off, _ = _in_layout(d)
    return jnp.concatenate([lax.slice_in_dim(g_mine, my_off[n], my_off[n] + wd[n], axis=1) for n in IN_NAMES],
                           axis=1)


def _wuq_to_mine(wq):
    r = wq.shape[0]
    w3 = wq.reshape(r, MLA_HEADS, MLA_NOPE + MLA_ROPE)
    w3 = jnp.pad(w3, ((0, 0), (0, 0), (0, MLA_QK_PAD - MLA_NOPE - MLA_ROPE)))
    return w3.reshape(r, MLA_HEADS * MLA_QK_PAD)


def _wuq_from_mine(g):
    r = g.shape[0]
    return g.reshape(r, MLA_HEADS, MLA_QK_PAD)[:, :, :MLA_NOPE + MLA_ROPE].reshape(r, -1)


def _wukv_to_mine(wkv):
    r = wkv.shape[0]
    w3 = wkv.reshape(r, MLA_HEADS, MLA_NOPE + MLA_V)
    return jnp.concatenate([w3[:, :, :MLA_NOPE].reshape(r, -1), w3[:, :, MLA_NOPE:].reshape(r, -1)], axis=1)


def _wukv_from_mine(g):
    r = g.shape[0]
    kn = g[:, :MLA_HEADS * MLA_NOPE].reshape(r, MLA_HEADS, MLA_NOPE)
    vv = g[:, MLA_HEADS * MLA_NOPE:].reshape(r, MLA_HEADS, MLA_V)
    return jnp.concatenate([kn, vv], axis=2).reshape(r, -1)


COL_SHARDED = ("w_in", "gla_gk_w", "mla_wuq", "mla_wukv", "ffn_w_in")


def kernel(x, c, positions, ada_w, ada_b, norm_mix_g, w_in, gla_gk_w, gla_gk_b, gla_onorm_g, gla_wo, mla_q_norm_g, mla_wuq, mla_kv_norm_g, mla_wukv, mla_wo, w_out, norm_ffn_g, ffn_w_in, ffn_w_down, final_norm_g, loss_target, m_ada_w, m_ada_b, m_norm_mix_g, m_w_in, m_gla_gk_w, m_gla_gk_b, m_gla_onorm_g, m_gla_wo, m_mla_q_norm_g, m_mla_wuq, m_mla_kv_norm_g, m_mla_wukv, m_mla_wo, m_w_out, m_norm_ffn_g, m_ffn_w_in, m_ffn_w_down, m_final_norm_g, v_ada_w, v_ada_b, v_norm_mix_g, v_w_in, v_gla_gk_w, v_gla_gk_b, v_gla_onorm_g, v_gla_wo, v_mla_q_norm_g, v_mla_wuq, v_mla_kv_norm_g, v_mla_wukv, v_mla_wo, v_w_out, v_norm_ffn_g, v_ffn_w_in, v_ffn_w_down, v_final_norm_g):
    wts = dict(ada_w=ada_w, ada_b=ada_b, norm_mix_g=norm_mix_g, w_in=w_in, gla_gk_w=gla_gk_w, gla_gk_b=gla_gk_b,
               gla_onorm_g=gla_onorm_g, gla_wo=gla_wo, mla_q_norm_g=mla_q_norm_g, mla_wuq=mla_wuq,
               mla_kv_norm_g=mla_kv_norm_g, mla_wukv=mla_wukv, mla_wo=mla_wo, w_out=w_out, norm_ffn_g=norm_ffn_g,
               ffn_w_in=ffn_w_in, ffn_w_down=ffn_w_down, final_norm_g=final_norm_g)
    mom_m = dict(zip(WEIGHTS, (m_ada_w, m_ada_b, m_norm_mix_g, m_w_in, m_gla_gk_w, m_gla_gk_b, m_gla_onorm_g,
                               m_gla_wo, m_mla_q_norm_g, m_mla_wuq, m_mla_kv_norm_g, m_mla_wukv, m_mla_wo, m_w_out,
                               m_norm_ffn_g, m_ffn_w_in, m_ffn_w_down, m_final_norm_g)))
    mom_v = dict(zip(WEIGHTS, (v_ada_w, v_ada_b, v_norm_mix_g, v_w_in, v_gla_gk_w, v_gla_gk_b, v_gla_onorm_g,
                               v_gla_wo, v_mla_q_norm_g, v_mla_wuq, v_mla_kv_norm_g, v_mla_wukv, v_mla_wo, v_w_out,
                               v_norm_ffn_g, v_ffn_w_in, v_ffn_w_down, v_final_norm_g)))
    seq, d = x.shape[1], x.shape[2]
    me = 4 * lax.axis_index("x") + 2 * lax.axis_index("y") + lax.axis_index("c")

    def two_d(a):
        return a.reshape(a.shape[-2], a.shape[-1]) if a.ndim >= 2 else a.reshape(1, -1)

    shard = {n: two_d(wts[n]) for n in BIG}
    send = [shard[n].astype(F32 if n == "gla_gk_w" else BF16) for n in BIG] + [two_d(c)]
    got = _exchange(send, scatter=False, name="comm_all_gather_weights")
    gathered = dict(zip(BIG, got[:-1]))
    c_all = got[-1].reshape(N_DEV, d)
    fullw = {}
    for n in BIG:
        g = gathered[n]
        fullw[n] = _unshard_cols(g) if n in COL_SHARDED else g.reshape(-1, g.shape[-1])
    w = dict(
        w_in=_w_in_to_mine(fullw["w_in"], d),
        gla_gk_w=jnp.pad(fullw["gla_gk_w"], ((0, LANE - GLA_GATE_RANK), (0, 0))),
        gla_wo=fullw["gla_wo"], mla_wuq=_wuq_to_mine(fullw["mla_wuq"]), mla_wukv=_wukv_to_mine(fullw["mla_wukv"]),
        mla_wo=fullw["mla_wo"], w_out=fullw["w_out"], ffn_w_in=fullw["ffn_w_in"], ffn_w_down=fullw["ffn_w_down"],
        gla_gk_b=two_d(gla_gk_b), gla_onorm_g=two_d(gla_onorm_g), mla_q_norm_g=two_d(mla_q_norm_g),
        mla_kv_norm_g=two_d(mla_kv_norm_g), norm_mix_g=two_d(norm_mix_g), norm_ffn_g=two_d(norm_ffn_g),
        final_norm_g=two_d(final_norm_g))

    c_pad = jnp.pad(c_all, ((0, 16 - N_DEV), (0, 0)))
    (c_act,) = _rowwise(lambda cv: cv * _sigmoid(cv), [(c_pad, d, 0)], [], [(d, F32)], tile=16, name="silu_c")
    ada_w2 = two_d(ada_w)
    mod_part = _mm(c_act, ada_w2, name="mm_ada")[:N_DEV]
    (mod_all,) = _exchange([mod_part], scatter=False, name="comm_all_gather_mod")
    mod_mine = lax.dynamic_index_in_dim(mod_all, me, axis=1, keepdims=False).reshape(1, -1) + two_d(ada_b)
    mod8 = jnp.pad(mod_mine.reshape(6, d), ((0, 2), (0, 0)))

    inv_freq = ROPE_THETA ** (-jnp.arange(0, MLA_ROPE, 2, dtype=F32) / MLA_ROPE)
    ang = positions.reshape(seq, 1).astype(F32) * inv_freq[None, :]
    cos, sin, z32 = jnp.cos(ang), jnp.sin(ang), jnp.zeros((seq, 32), F32)
    tab = jnp.concatenate([cos, cos, z32, z32, -sin, z32, z32, z32, z32, sin, z32, z32], axis=1)
    loss_local, grad_x, big, small = _local_step(x.reshape(seq, d), loss_target.reshape(seq, d), tab, mod8, w)

    ref_layout = dict(big)
    ref_layout["w_in"] = _w_in_from_mine(big["w_in"], d)
    ref_layout["mla_wuq"] = _wuq_from_mine(big["mla_wuq"])
    ref_layout["mla_wukv"] = _wukv_from_mine(big["mla_wukv"])
    slabs = []
    for n in BIG:
        g = ref_layout[n]
        s = _shard_cols(g) if n in COL_SHARDED else g.reshape(N_DEV, -1, g.shape[-1])
        slabs.append(s.astype(BF16))
    pack = jnp.concatenate([small[n] for n in SMALL], axis=1)
    recv = dict(zip(BIG, _exchange(slabs, scatter=True, name="comm_reduce_scatter_grads")))
    (pack_all,) = _exchange([pack], scatter=False, name="comm_all_gather_small")
    pack_all = pack_all.reshape(N_DEV, -1)

    res = {}
    for n in BIG:
        res[n] = _adamw(shard[n], recv[n], two_d(mom_m[n]), two_d(mom_v[n]), name="adamw_" + n)
    n_ada = ada_w2.shape[1]
    dmod_cols = lax.dynamic_slice_in_dim(pack_all[:, :6 * d], me * n_ada, n_ada, axis=1)

    def f_outer(cat, dm):
        acc = cat[:, 0:1] * dm[0:1]
        for b in range(1, N_DEV):
            acc = acc + cat[:, b:b + 1] * dm[b:b + 1]
        return acc

    (g_ada_w,) = _rowwise(f_outer, [(jnp.transpose(c_act[:N_DEV]), N_DEV, 0)], [dmod_cols], [(n_ada, F32)],
                          tile=256, name="ada_w_grad")
    res["ada_w"] = _adamw(ada_w2, g_ada_w, two_d(m_ada_w), two_d(v_ada_w), name="adamw_ada_w")
    w_small = jnp.concatenate([two_d(wts[n]) for n in SMALL], axis=1)
    m_small = jnp.concatenate([two_d(mom_m[n]) for n in SMALL], axis=1)
    v_small = jnp.concatenate([two_d(mom_v[n]) for n in SMALL], axis=1)
    small_res = _adamw(w_small, pack_all.reshape(N_DEV, 1, -1), m_small, v_small, name="adamw_small")
    off = 0
    for n in SMALL:
        width = wts[n].size
        res[n] = tuple(lax.slice_in_dim(a, off, off + width, axis=1) for a in small_res)
        off += width

    loss = lax.psum(loss_local, ("x", "y", "c"))
    outs = [loss, grad_x.reshape(x.shape)]
    for kind in range(4):
        outs += [res[n][kind].reshape(wts[n].shape) for n in WEIGHTS]
    return tuple(outs)
```

```python
import functools

import jax
import jax.numpy as jnp
from jax import lax
from jax.experimental import pallas as pl
from jax.experimental.pallas import tpu as pltpu

F32 = jnp.float32
BF16 = jnp.bfloat16

N_DEV = 8
GLA_HEADS = 4
GLA_DK = 256
GLA_DV = 512
GLA_GATE_RANK = 16
GLA_GATE_NORMALIZER = 16.0
GLA_CHUNK = 64
MLA_HEADS = 16
MLA_NOPE = 128
MLA_ROPE = 64
MLA_V = 128
MLA_QK_PAD = 256
ROPE_THETA = 10000.0
NORM_EPS = 1e-6
ATT_SCALE = (MLA_NOPE + MLA_ROPE) ** -0.5
GLA_QSCALE = GLA_DK ** -0.5

ADAM_LR = 0.001
ADAM_B1 = 0.9
ADAM_B2 = 0.999
ADAM_EPS = 1e-08
ADAM_WD = 0.01
ADAM_STEP = 10

LANE = 128
VMEM_LIMIT = 48 * 1024 * 1024
MM_TILE_BYTES = 4 * 1024 * 1024
LOG2E = 1.4426950408889634
LN2 = 0.6931471805599453
NEG = -1e30

IN_NAMES = ("q", "k", "v", "g", "gk", "cq", "ckv", "kr", "ga", "gb")
MY_ORDER = ("v", "g", "ga", "gb", "q", "k", "cq", "ckv", "gk", "kr")

WEIGHTS = ("ada_w", "ada_b", "norm_mix_g", "w_in", "gla_gk_w", "gla_gk_b", "gla_onorm_g", "gla_wo",
           "mla_q_norm_g", "mla_wuq", "mla_kv_norm_g", "mla_wukv", "mla_wo", "w_out", "norm_ffn_g",
           "ffn_w_in", "ffn_w_down", "final_norm_g")
BIG = ("w_in", "gla_gk_w", "gla_wo", "mla_wuq", "mla_wukv", "mla_wo", "w_out", "ffn_w_in", "ffn_w_down")
SMALL = ("ada_b", "norm_mix_g", "gla_gk_b", "gla_onorm_g", "mla_q_norm_g", "mla_kv_norm_g", "norm_ffn_g",
         "final_norm_g")


def _in_layout(d):
    w = dict(q=d // 2, k=d // 2, v=d, g=d, gk=GLA_GATE_RANK, cq=d // 4, ckv=512, kr=MLA_ROPE, ga=d, gb=d)
    pw = {n: -(-w[n] // LANE) * LANE for n in w}
    ref_off, o = {}, 0
    for n in IN_NAMES:
        ref_off[n] = o
        o += w[n]
    my_off, o = {}, 0
    for n in MY_ORDER:
        assert o % pw[n] == 0
        my_off[n] = o
        o += pw[n]
    return w, pw, ref_off, my_off, o


def _cparams(sem=None):
    return pltpu.CompilerParams(dimension_semantics=sem, vmem_limit_bytes=VMEM_LIMIT)


def _dot(a, b, ca=1, cb=0):
    return lax.dot_general(a, b, (((ca,), (cb,)), ((), ())), preferred_element_type=F32)


def _tile(n, cap):
    if n <= cap:
        return n
    t = (cap // LANE) * LANE
    while t >= LANE:
        if n % t == 0:
            return t
        t -= LANE
    return n


def _mm(a, b, *, ta=False, tb=False, out_dtype=F32, name):
    m, k = (a.shape[1], a.shape[0]) if ta else a.shape
    n = b.shape[0] if tb else b.shape[1]
    assert k == (b.shape[1] if tb else b.shape[0])
    wide = max(a.dtype.itemsize, b.dtype.itemsize) > 2
    tm, tn, tk = _tile(m, 1024), _tile(n, 1024), _tile(k, MM_TILE_BYTES // (1024 * (4 if wide else 2)))
    nk = k // tk

    def product(a_ref, b_ref):
        return _dot(a_ref[...].astype(BF16), b_ref[...].astype(BF16), 0 if ta else 1, 1 if tb else 0)

    def body_one(a_ref, b_ref, o_ref):
        o_ref[...] = product(a_ref, b_ref).astype(o_ref.dtype)

    def body_acc(a_ref, b_ref, o_ref, acc_ref):
        kk = pl.program_id(2)

        @pl.when(kk == 0)
        def _():
            acc_ref[...] = jnp.zeros_like(acc_ref)

        acc_ref[...] += product(a_ref, b_ref)

        @pl.when(kk == nk - 1)
        def _():
            o_ref[...] = acc_ref[...].astype(o_ref.dtype)

    a_spec = (pl.BlockSpec((tk, tm), lambda i, j, kk: (kk, i)) if ta
              else pl.BlockSpec((tm, tk), lambda i, j, kk: (i, kk)))
    b_spec = (pl.BlockSpec((tn, tk), lambda i, j, kk: (j, kk)) if tb
              else pl.BlockSpec((tk, tn), lambda i, j, kk: (kk, j)))
    return pl.pallas_call(
        body_one if nk == 1 else body_acc, name=name, grid=(m // tm, n // tn, nk), in_specs=[a_spec, b_spec],
        out_specs=pl.BlockSpec((tm, tn), lambda i, j, kk: (i, j)),
        out_shape=jax.ShapeDtypeStruct((m, n), out_dtype),
        scratch_shapes=[] if nk == 1 else [pltpu.VMEM((tm, tn), F32)],
        compiler_params=_cparams(("parallel", "parallel", "arbitrary")),
    )(a, b)


def _rowwise(fn, rows, vecs, outs, sums=(), *, tile, name):
    t = rows[0][0].shape[0]
    tile = min(tile, t)
    assert t % tile == 0
    n_rows, n_vecs, n_outs = len(rows), len(vecs), len(outs)

    def body(*refs):
        ins = [r[...] for r in refs[:n_rows + n_vecs]]
        res = fn(*ins)
        if not isinstance(res, (tuple, list)):
            res = (res,)
        out_refs = refs[n_rows + n_vecs:]
        for r, val in zip(out_refs[:n_outs], res[:n_outs]):
            r[...] = val.astype(r.dtype)
        if sums:
            first = pl.program_id(0) == 0
            for r, val in zip(out_refs[n_outs:], res[n_outs:]):
                @pl.when(first)
                def _(r=r):
                    r[...] = jnp.zeros_like(r)
                r[...] += val

    in_specs = [pl.BlockSpec((tile, w), lambda i, cb=cb: (i, cb)) for (_, w, cb) in rows]
    in_specs += [pl.BlockSpec(v.shape, lambda i: (0, 0)) for v in vecs]
    out_specs = [pl.BlockSpec((tile, w), lambda i: (i, 0)) for (w, _) in outs]
    out_specs += [pl.BlockSpec((1, w), lambda i: (0, 0)) for w in sums]
    out_shape = [jax.ShapeDtypeStruct((t, w), dt) for (w, dt) in outs]
    out_shape += [jax.ShapeDtypeStruct((1, w), F32) for w in sums]
    res = pl.pallas_call(
        body, name=name, grid=(t // tile,), in_specs=in_specs, out_specs=out_specs, out_shape=out_shape,
        compiler_params=_cparams(("arbitrary",)),
    )(*[r[0] for r in rows], *vecs)
    return res


def _rstd(x):
    return lax.rsqrt(jnp.mean(x * x, axis=-1, keepdims=True) + NORM_EPS)


def _sigmoid(x):
    return 1.0 / (1.0 + jnp.exp(-x))


def _rms_bwd(dxh, xh, r):
    return r * (dxh - xh * jnp.mean(dxh * xh, axis=-1, keepdims=True))


def _rope(t, tab, sign):
    cosf, sin_a, sin_b = tab[:, :LANE], tab[:, LANE:2 * LANE], tab[:, 2 * LANE:]
    return t * cosf + sign * (pltpu.roll(t, 96, 1) * sin_a + pltpu.roll(t, 32, 1) * sin_b)


def _split3(x):
    hi = x.astype(BF16)
    r1 = x - hi.astype(F32)
    mid = r1.astype(BF16)
    lo = (r1 - mid.astype(F32)).astype(BF16)
    return hi, mid, lo


def _tri_sum(tri_bf16, x):
    hi, mid, lo = _split3(x)
    return _dot(tri_bf16, hi) + _dot(tri_bf16, mid) + _dot(tri_bf16, lo)


def _dot_nt2(a, b):
    a_hi = a.astype(BF16)
    a_lo = (a - a_hi.astype(F32)).astype(BF16)
    b_hi = b.astype(BF16)
    b_lo = (b - b_hi.astype(F32)).astype(BF16)
    return _dot(a_hi, b_hi, 1, 1) + _dot(a_hi, b_lo, 1, 1) + _dot(a_lo, b_hi, 1, 1)


def _gla_specs(t, rows, lay, reverse):
    nb = t // rows
    blk = (lambda i: nb - 1 - i) if reverse else (lambda i: i)
    qb, kb = lay["q"] // GLA_DK, lay["k"] // GLA_DK
    vb = lay["v"] // GLA_DV
    return [
        pl.BlockSpec((rows, GLA_DK), lambda h, i: (blk(i), qb + h)),
        pl.BlockSpec((rows, GLA_DK), lambda h, i: (blk(i), kb + h)),
        pl.BlockSpec((rows, GLA_DV), lambda h, i: (blk(i), vb + h)),
        pl.BlockSpec((rows, GLA_DK), lambda h, i: (blk(i), h)),
    ], blk


def _gla_fwd(p, la, lay):
    t = p.shape[0]
    rows = min(512, t)
    nb, nc = t // rows, rows // GLA_CHUNK
    c64 = GLA_CHUNK

    def body(q_ref, k_ref, v_ref, la_ref, o_ref, st_ref, s_ref):
        @pl.when(pl.program_id(1) == 0)
        def _():
            s_ref[...] = jnp.zeros_like(s_ref)

        r = lax.broadcasted_iota(jnp.int32, (c64, c64), 0)
        cc = lax.broadcasted_iota(jnp.int32, (c64, c64), 1)
        tril = cc <= r
        tril_b = tril.astype(BF16)
        for c in range(nc):
            sl = pl.ds(c * c64, c64)
            b = _tri_sum(tril_b, la_ref[sl, :])
            b_last = b[c64 - 1:c64, :]
            q = q_ref[sl, :] * GLA_QSCALE
            k = k_ref[sl, :]
            v = v_ref[sl, :].astype(BF16)
            qt_f = q * jnp.exp(b)
            qt = qt_f.astype(BF16)
            kh = (k * jnp.exp(b_last - b)).astype(BF16)
            s_prev = s_ref[...]
            st_ref[0, c] = s_prev
            att = jnp.where(tril, _dot_nt2(qt_f, k * jnp.exp(-b)), 0.0)
            o_ref[sl, :] = _dot(qt, s_prev.astype(BF16), 1, 1) + _dot(att.astype(BF16), v)
            s_ref[...] = s_prev * jnp.exp(b_last) + _dot(v, kh, 0, 0)

    in_specs, _ = _gla_specs(t, rows, lay, False)
    return pl.pallas_call(
        body, name="gla_fwd", grid=(GLA_HEADS, nb), in_specs=in_specs,
        out_specs=[pl.BlockSpec((rows, GLA_DV), lambda h, i: (i, h)),
                   pl.BlockSpec((1, nc, GLA_DV, GLA_DK), lambda h, i: (h, i, 0, 0))],
        out_shape=[jax.ShapeDtypeStruct((t, GLA_HEADS * GLA_DV), F32),
                   jax.ShapeDtypeStruct((GLA_HEADS, t // c64, GLA_DV, GLA_DK), F32)],
        scratch_shapes=[pltpu.VMEM((GLA_DV, GLA_DK), F32)],
        compiler_params=_cparams(("parallel", "arbitrary")),
    )(p, p, p, la)


def _gla_bwd(p, la, do, states, lay):
    t = p.shape[0]
    rows = min(512, t)
    nb, nc = t // rows, rows // GLA_CHUNK
    c64 = GLA_CHUNK

    def body(q_ref, k_ref, v_ref, la_ref, do_ref, st_ref, dq_ref, dk_ref, dv_ref, dla_ref, ds_ref):
        @pl.when(pl.program_id(1) == 0)
        def _():
            ds_ref[...] = jnp.zeros_like(ds_ref)

        r = lax.broadcasted_iota(jnp.int32, (c64, c64), 0)
        cc = lax.broadcasted_iota(jnp.int32, (c64, c64), 1)
        tril = cc <= r
        tril_b = tril.astype(BF16)
        triu_b = (cc >= r).astype(BF16)
        for c in reversed(range(nc)):
            sl = pl.ds(c * c64, c64)
            b = _tri_sum(tril_b, la_ref[sl, :])
            b_last = b[c64 - 1:c64, :]
            eb, enb, ebl_b, ebl = jnp.exp(b), jnp.exp(-b), jnp.exp(b_last - b), jnp.exp(b_last)
            k = k_ref[sl, :]
            qt_f = q_ref[sl, :] * GLA_QSCALE * eb
            kt_f = k * enb
            kh_f = k * ebl_b
            qt, kt, kh = qt_f.astype(BF16), kt_f.astype(BF16), kh_f.astype(BF16)
            v_f = v_ref[sl, :]
            dout_f = do_ref[sl, :]
            v, dout = v_f.astype(BF16), dout_f.astype(BF16)
            s_prev = st_ref[0, c]
            ds_next = ds_ref[...]
            ds_next_b = ds_next.astype(BF16)
            att = jnp.where(tril, _dot_nt2(qt_f, kt_f), 0.0).astype(BF16)
            datt = jnp.where(tril, _dot_nt2(dout_f, v_f), 0.0).astype(BF16)
            dqt = _dot(dout, s_prev.astype(BF16)) + _dot(datt, kt)
            dkt = _dot(datt, qt, 0, 0)
            dv = _dot(att, dout, 0, 0) + _dot(kh, ds_next_b, 1, 1)
            dkh = _dot(v, ds_next_b)
            d_ebl = jnp.sum(ds_next * s_prev, axis=0, keepdims=True)
            ds_ref[...] = ds_next * ebl + _dot(dout, qt, 0, 0)
            db = dqt * qt_f - dkt * kt_f - dkh * kh_f
            db_last = ebl * d_ebl + jnp.sum(dkh * kh_f, axis=0, keepdims=True)
            dq_ref[sl, :] = (dqt * eb * GLA_QSCALE).astype(dq_ref.dtype)
            dk_ref[sl, :] = (dkt * enb + dkh * ebl_b).astype(dk_ref.dtype)
            dv_ref[sl, :] = dv.astype(dv_ref.dtype)
            dla_ref[sl, :] = _tri_sum(triu_b, db) + db_last

    in_specs, blk = _gla_specs(t, rows, lay, True)
    in_specs += [pl.BlockSpec((rows, GLA_DV), lambda h, i: (blk(i), h)),
                 pl.BlockSpec((1, nc, GLA_DV, GLA_DK), lambda h, i: (h, blk(i), 0, 0))]
    dk_spec = pl.BlockSpec((rows, GLA_DK), lambda h, i: (blk(i), h))
    return pl.pallas_call(
        body, name="gla_bwd", grid=(GLA_HEADS, nb), in_specs=in_specs,
        out_specs=[dk_spec, dk_spec, pl.BlockSpec((rows, GLA_DV), lambda h, i: (blk(i), h)), dk_spec],
        out_shape=[jax.ShapeDtypeStruct((t, GLA_HEADS * GLA_DK), BF16),
                   jax.ShapeDtypeStruct((t, GLA_HEADS * GLA_DK), BF16),
                   jax.ShapeDtypeStruct((t, GLA_HEADS * GLA_DV), BF16),
                   jax.ShapeDtypeStruct((t, GLA_HEADS * GLA_DK), F32)],
        scratch_shapes=[pltpu.VMEM((GLA_DV, GLA_DK), F32)],
        compiler_params=_cparams(("parallel", "arbitrary")),
    )(p, p, p, la, do, states)


def _diag_mask(rows, cols, row0):
    row = row0 + lax.broadcasted_iota(jnp.int32, (rows, cols), 0)
    col = lax.broadcasted_iota(jnp.int32, (rows, cols), 1)
    return col <= row


def _flash_tiles(t):
    tq = min(512, t)
    halves = 2 if tq % 32 == 0 else 1
    return tq, t // tq, halves, tq // halves


def _flash_fwd(q, k, vx):
    t = q.shape[0]
    tq, nq, halves, hr = _flash_tiles(t)
    dqk, dv = MLA_QK_PAD, MLA_V

    def body(q_ref, k_ref, v_ref, o_ref, lse_ref, m_ref, acc_ref):
        i = pl.program_id(1)
        m_ref[...] = jnp.full_like(m_ref, NEG)
        acc_ref[...] = jnp.zeros_like(acc_ref)

        def step(j, masked):
            off = pl.multiple_of(j * tq, tq)
            kb = k_ref[pl.ds(off, tq), :]
            vb = v_ref[pl.ds(off, tq), :]
            for hh in range(halves):
                rs = pl.ds(hh * hr, hr)
                s = _dot(q_ref[rs, :], kb, 1, 1)
                if masked:
                    s = jnp.where(_diag_mask(hr, tq, hh * hr), s, NEG)
                m_old = m_ref[rs, :]
                m_new = jnp.maximum(m_old, jnp.max(s, axis=1, keepdims=True))
                pr = jnp.exp2(s - m_new)
                acc_ref[rs, :] = jnp.exp2(m_old - m_new) * acc_ref[rs, :] + _dot(pr.astype(BF16), vb)
                m_ref[rs, :] = m_new

        def loop_body(j, carry):
            step(j, False)
            return carry

        lax.fori_loop(0, i, loop_body, 0)
        step(i, True)
        acc = acc_ref[...]
        l = acc[:, dv:dv + 1]
        o_ref[...] = acc[:, :dv] / l
        lse_ref[0] = m_ref[...] + jnp.log(l) * LOG2E

    return pl.pallas_call(
        body, name="mla_flash_fwd", grid=(MLA_HEADS, nq),
        in_specs=[pl.BlockSpec((tq, dqk), lambda h, i: (i, h)),
                  pl.BlockSpec((t, dqk), lambda h, i: (0, h)),
                  pl.BlockSpec((t, 2 * dv), lambda h, i: (0, h))],
        out_specs=[pl.BlockSpec((tq, dv), lambda h, i: (i, h)),
                   pl.BlockSpec((1, tq, 1), lambda h, i: (h, i, 0))],
        out_shape=[jax.ShapeDtypeStruct((t, MLA_HEADS * dv), F32),
                   jax.ShapeDtypeStruct((MLA_HEADS, t, 1), F32)],
        scratch_shapes=[pltpu.VMEM((tq, 1), F32), pltpu.VMEM((tq, 2 * dv), F32)],
        compiler_params=_cparams(("parallel", "arbitrary")),
    )(q, k, vx)


def _flash_bwd_dq(q, k, v, do, o, lse):
    t = q.shape[0]
    tq, nq, halves, hr = _flash_tiles(t)
    dqk, dv = MLA_QK_PAD, MLA_V

    def body(q_ref, k_ref, v_ref, do_ref, o_ref, lse_ref, dq_ref, dl_ref, acc_ref):
        i = pl.program_id(1)
        dl_ref[0] = jnp.sum(do_ref[...].astype(F32) * o_ref[...], axis=1, keepdims=True)
        acc_ref[...] = jnp.zeros_like(acc_ref)

        def step(j, masked):
            off = pl.multiple_of(j * tq, tq)
            kb = k_ref[pl.ds(off, tq), :]
            vb = v_ref[pl.ds(off, tq), :]
            for hh in range(halves):
                rs = pl.ds(hh * hr, hr)
                s = _dot(q_ref[rs, :], kb, 1, 1)
                if masked:
                    s = jnp.where(_diag_mask(hr, tq, hh * hr), s, NEG)
                pr = jnp.exp2(s - lse_ref[0, rs, :])
                dp = _dot(do_ref[rs, :], vb, 1, 1)
                ds = pr * (dp - dl_ref[0, rs, :])
                acc_ref[rs, :] += _dot(ds.astype(BF16), kb)

        def loop_body(j, carry):
            step(j, False)
            return carry

        lax.fori_loop(0, i, loop_body, 0)
        step(i, True)
        dq_ref[...] = acc_ref[...] * ATT_SCALE

    return pl.pallas_call(
        body, name="mla_flash_bwd_dq", grid=(MLA_HEADS, nq),
        in_specs=[pl.BlockSpec((tq, dqk), lambda h, i: (i, h)),
                  pl.BlockSpec((t, dqk), lambda h, i: (0, h)),
                  pl.BlockSpec((t, dv), lambda h, i: (0, h)),
                  pl.BlockSpec((tq, dv), lambda h, i: (i, h)),
                  pl.BlockSpec((tq, dv), lambda h, i: (i, h)),
                  pl.BlockSpec((1, tq, 1), lambda h, i: (h, i, 0))],
        out_specs=[pl.BlockSpec((tq, dqk), lambda h, i: (i, h)),
                   pl.BlockSpec((1, tq, 1), lambda h, i: (h, i, 0))],
        out_shape=[jax.ShapeDtypeStruct((t, MLA_HEADS * dqk), F32),
                   jax.ShapeDtypeStruct((MLA_HEADS, t, 1), F32)],
        scratch_shapes=[pltpu.VMEM((tq, dqk), F32)],
        compiler_params=_cparams(("parallel", "arbitrary")),
    )(q, k, v, do, o, lse)


def _flash_bwd_dkv(q, k, v, do, lse, delta):
    t = q.shape[0]
    tq, nq, halves, hr = _flash_tiles(t)
    dqk, dv = MLA_QK_PAD, MLA_V

    def body(k_ref, v_ref, q_ref, do_ref, lse_ref, dl_ref, dk_ref, dv_ref, dk_acc, dv_acc):
        j = pl.program_id(1)
        kb = k_ref[...]
        vb = v_ref[...]
        dk_acc[...] = jnp.zeros_like(dk_acc)
        dv_acc[...] = jnp.zeros_like(dv_acc)

        def step(i, masked):
            for hh in range(halves):
                rs = pl.ds(pl.multiple_of(i * tq + hh * hr, hr), hr)
                qb = q_ref[rs, :]
                dout = do_ref[rs, :]
                s = _dot(qb, kb, 1, 1)
                if masked:
                    s = jnp.where(_diag_mask(hr, tq, hh * hr), s, NEG)
                pr = jnp.exp2(s - lse_ref[0, rs, :])
                dv_acc[...] += _dot(pr.astype(BF16), dout, 0, 0)
                dp = _dot(dout, vb, 1, 1)
                ds = pr * (dp - dl_ref[0, rs, :])
                dk_acc[...] += _dot(ds.astype(BF16), qb, 0, 0)

        def loop_body(i, carry):
            step(i, False)
            return carry

        step(j, True)
        lax.fori_loop(j + 1, nq, loop_body, 0)
        dk_ref[...] = dk_acc[...] * LN2
        dv_ref[...] = dv_acc[...]

    return pl.pallas_call(
        body, name="mla_flash_bwd_dkv", grid=(MLA_HEADS, nq),
        in_specs=[pl.BlockSpec((tq, dqk), lambda h, j: (j, h)),
                  pl.BlockSpec((tq, dv), lambda h, j: (j, h)),
                  pl.BlockSpec((t, dqk), lambda h, j: (0, h)),
                  pl.BlockSpec((t, dv), lambda h, j: (0, h)),
                  pl.BlockSpec((1, t, 1), lambda h, j: (h, 0, 0)),
                  pl.BlockSpec((1, t, 1), lambda h, j: (h, 0, 0))],
        out_specs=[pl.BlockSpec((tq, dqk), lambda h, j: (j, h)),
                   pl.BlockSpec((tq, dv), lambda h, j: (j, h))],
        out_shape=[jax.ShapeDtypeStruct((t, MLA_HEADS * dqk), F32),
                   jax.ShapeDtypeStruct((t, MLA_HEADS * dv), F32)],
        scratch_shapes=[pltpu.VMEM((tq, dqk), F32), pltpu.VMEM((tq, dv), F32)],
        compiler_params=_cparams(("parallel", "arbitrary")),
    )(k, v, q, do, lse, delta)


def _local_step(x, target, tab, mod8, w):
    t, d = x.shape
    _, pw, _, lay, _ = _in_layout(d)
    ffn = w["ffn_w_down"].shape[0]

    def blk(arr, name):
        return (arr, pw[name], lay[name] // pw[name])

    def full(arr):
        return (arr, arr.shape[1], 0)

    g1, g2, g3 = w["norm_mix_g"], w["norm_ffn_g"], w["final_norm_g"]

    def f_ln1(xv, mod, g):
        return (xv * _rstd(xv) * g) * (1.0 + mod[1:2]) + mod[0:1]

    (h,) = _rowwise(f_ln1, [full(x)], [mod8, g1], [(d, BF16)], tile=256, name="ln1_modulate")
    p = _mm(h, w["w_in"], name="mm_in_proj")

    def f_gk(pgk, gkw, gkb):
        z = _dot(pgk.astype(BF16), gkw.astype(BF16)) + gkb
        return (jnp.minimum(z, 0.0) - jnp.log(1.0 + jnp.exp(-jnp.abs(z)))) / GLA_GATE_NORMALIZER

    (la,) = _rowwise(f_gk, [blk(p, "gk")], [w["gla_gk_w"], w["gla_gk_b"]], [(GLA_HEADS * GLA_DK, F32)],
                     tile=512, name="gla_gate")
    o_gla, states = _gla_fwd(p, la, lay)

    def f_gla_out(ov, pg, g):
        parts = []
        for hh in range(GLA_HEADS):
            oh = ov[:, hh * GLA_DV:(hh + 1) * GLA_DV]
            ph = pg[:, hh * GLA_DV:(hh + 1) * GLA_DV]
            parts.append(oh * _rstd(oh) * g * (ph * _sigmoid(ph)))
        return jnp.concatenate(parts, axis=1)

    (o_n,) = _rowwise(f_gla_out, [full(o_gla), blk(p, "g")], [w["gla_onorm_g"]], [(d, BF16)], tile=256,
                      name="gla_out_norm")
    y_gla = _mm(o_n, w["gla_wo"], name="mm_gla_wo")

    def f_mla_prep(cq, ckv, kr, tb, gq, gkv):
        return cq * _rstd(cq) * gq, ckv * _rstd(ckv) * gkv, _rope(kr, tb, 1.0)

    cqn, ckvn, krr = _rowwise(f_mla_prep, [blk(p, "cq"), blk(p, "ckv"), blk(p, "kr"), full(tab)],
                              [w["mla_q_norm_g"], w["mla_kv_norm_g"]],
                              [(pw["cq"], BF16), (pw["ckv"], BF16), (LANE, F32)], tile=512, name="mla_prep")
    qlat = _mm(cqn, w["mla_wuq"], name="mm_mla_wuq")
    kvl = _mm(ckvn, w["mla_wukv"], name="mm_mla_wukv")
    hv = MLA_HEADS * MLA_V

    def f_qkv(ql, kn, vv, kr, tb):
        qs, ks, vx = [], [], []
        one_col = (lax.broadcasted_iota(jnp.int32, (ql.shape[0], LANE), 1) == 0).astype(F32)
        for hh in range(MLA_HEADS):
            o0 = hh * MLA_QK_PAD
            qs += [ql[:, o0:o0 + LANE], _rope(ql[:, o0 + LANE:o0 + 2 * LANE], tb, 1.0)]
            ks += [kn[:, hh * LANE:(hh + 1) * LANE], kr]
            vx += [vv[:, hh * MLA_V:(hh + 1) * MLA_V], one_col]
        return (jnp.concatenate(qs, axis=1) * (ATT_SCALE * LOG2E), jnp.concatenate(ks, axis=1), vv,
                jnp.concatenate(vx, axis=1))

    qa, ka, va, vxa = _rowwise(f_qkv, [full(qlat), (kvl, hv, 0), (kvl, hv, 1), full(krr), full(tab)], [],
                               [(MLA_HEADS * MLA_QK_PAD, BF16), (MLA_HEADS * MLA_QK_PAD, BF16), (hv, BF16),
                                (2 * hv, BF16)], tile=256, name="mla_qkv_build")
    o_mla, lse = _flash_fwd(qa, ka, vxa)
    y_mla = _mm(o_mla, w["mla_wo"], name="mm_mla_wo")

    def f_merge(yg, ym, ga, gb):
        return _sigmoid(ga) * yg + _sigmoid(gb) * ym

    (merged,) = _rowwise(f_merge, [full(y_gla), full(y_mla), blk(p, "ga"), blk(p, "gb")], [], [(d, BF16)],
                         tile=256, name="merge")
    mix = _mm(merged, w["w_out"], name="mm_w_out")

    def f_res_ln2(xv, mx, mod, g):
        x2v = xv + mod[2:3] * mx
        return x2v, (x2v * _rstd(x2v) * g) * (1.0 + mod[4:5]) + mod[3:4]

    x2, h2 = _rowwise(f_res_ln2, [full(x), full(mix)], [mod8, g2], [(d, F32), (d, BF16)], tile=256,
                      name="res_ln2_modulate")
    gu = _mm(h2, w["ffn_w_in"], name="mm_ffn_in")

    def f_swiglu(gv, uv):
        return gv * _sigmoid(gv) * uv

    (act,) = _rowwise(f_swiglu, [(gu, ffn, 0), (gu, ffn, 1)], [], [(ffn, BF16)], tile=128, name="swiglu")
    f_out = _mm(act, w["ffn_w_down"], name="mm_ffn_down")

    def f_head(x2v, fv, tg, mod, g):
        x3 = x2v + mod[5:6] * fv
        r = _rstd(x3)
        xh = x3 * r
        e = xh * g - tg
        loss_rows = 0.5 * jnp.mean(e * e, axis=-1, keepdims=True)
        dy = e * (1.0 / d)
        dx3 = _rms_bwd(dy * g, xh, r)
        loss = jnp.broadcast_to(jnp.sum(loss_rows, axis=0, keepdims=True), (1, LANE))
        return (dx3, dx3 * mod[5:6], loss, jnp.sum(dy * xh, axis=0, keepdims=True),
                jnp.sum(dx3 * fv, axis=0, keepdims=True))

    dx3, df, loss_v, dg3, dgate_f = _rowwise(f_head, [full(x2), full(f_out), full(target)], [mod8, g3],
                                             [(d, F32), (d, BF16)], [LANE, d, d], tile=256, name="loss_head")
    da = _mm(df, w["ffn_w_down"], tb=True, name="mm_ffn_down_dx")
    g_ffn_down = _mm(act, df, ta=True, name="mm_ffn_down_dw")

    def f_swiglu_bwd(gv, uv, dav):
        sg = _sigmoid(gv)
        return jnp.concatenate([dav * uv * (sg * (1.0 + gv * (1.0 - sg))), dav * (gv * sg)], axis=1)

    (dgu,) = _rowwise(f_swiglu_bwd, [(gu, ffn, 0), (gu, ffn, 1), full(da)], [], [(2 * ffn, BF16)], tile=128,
                      name="swiglu_bwd")
    dh2 = _mm(dgu, w["ffn_w_in"], tb=True, name="mm_ffn_in_dx")
    g_ffn_in = _mm(h2, dgu, ta=True, name="mm_ffn_in_dw")

    def f_ln2_bwd(x2v, dh, dx3v, mx, mod, g):
        r = _rstd(x2v)
        xh = x2v * r
        dn = dh * (1.0 + mod[4:5])
        dx2 = dx3v + _rms_bwd(dn * g, xh, r)
        return (dx2, dx2 * mod[2:3],
                jnp.sum(dh * (xh * g), axis=0, keepdims=True), jnp.sum(dh, axis=0, keepdims=True),
                jnp.sum(dn * xh, axis=0, keepdims=True), jnp.sum(dx2 * mx, axis=0, keepdims=True))

    dx2, dmix, dscale_f, dshift_f, dg2, dgate_m = _rowwise(
        f_ln2_bwd, [full(x2), full(dh2), full(dx3), full(mix)], [mod8, g2], [(d, F32), (d, BF16)],
        [d, d, d, d], tile=256, name="ln2_bwd")
    dmerged = _mm(dmix, w["w_out"], tb=True, name="mm_w_out_dx")
    g_w_out = _mm(merged, dmix, ta=True, name="mm_w_out_dw")

    def f_merge_bwd(dm, yg, ym, ga, gb):
        sa, sb = _sigmoid(ga), _sigmoid(gb)
        return dm * sa, dm * sb, dm * yg * sa * (1.0 - sa), dm * ym * sb * (1.0 - sb)

    dy_gla, dy_mla, dp_ga, dp_gb = _rowwise(
        f_merge_bwd, [full(dmerged), full(y_gla), full(y_mla), blk(p, "ga"), blk(p, "gb")], [],
        [(d, BF16)] * 4, tile=256, name="merge_bwd")
    do_n = _mm(dy_gla, w["gla_wo"], tb=True, name="mm_gla_wo_dx")
    g_gla_wo = _mm(o_n, dy_gla, ta=True, name="mm_gla_wo_dw")
    do_m = _mm(dy_mla, w["mla_wo"], tb=True, out_dtype=BF16, name="mm_mla_wo_dx")
    g_mla_wo = _mm(o_mla, dy_mla, ta=True, name="mm_mla_wo_dw")

    def f_gla_out_bwd(don, ov, pg, g):
        dos, dpgs = [], []
        dg = jnp.zeros((1, GLA_DV), F32)
        for hh in range(GLA_HEADS):
            sl = slice(hh * GLA_DV, (hh + 1) * GLA_DV)
            oh, ph, dn = ov[:, sl], pg[:, sl], don[:, sl]
            r = _rstd(oh)
            xh = oh * r
            sg = _sigmoid(ph)
            dpre = dn * (ph * sg)
            dg = dg + jnp.sum(dpre * xh, axis=0, keepdims=True)
            dos.append(_rms_bwd(dpre * g, xh, r))
            dpgs.append(dn * (xh * g) * (sg * (1.0 + ph * (1.0 - sg))))
        return jnp.concatenate(dos, axis=1), jnp.concatenate(dpgs, axis=1), dg

    do_gla, dp_g, dg_on = _rowwise(f_gla_out_bwd, [full(do_n), full(o_gla), blk(p, "g")], [w["gla_onorm_g"]],
                                   [(d, F32), (d, BF16)], [GLA_DV], tile=256, name="gla_out_norm_bwd")
    dp_q, dp_k, dp_v, dla = _gla_bwd(p, la, do_gla, states, lay)

    def f_gk_bwd(dlav, pgk, gkw, gkb):
        z = _dot(pgk.astype(BF16), gkw.astype(BF16)) + gkb
        dz = dlav * (1.0 / GLA_GATE_NORMALIZER) * _sigmoid(-z)
        return dz, _dot(dz.astype(BF16), gkw.astype(BF16), 1, 1), jnp.sum(dz, axis=0, keepdims=True)

    dz, dp_gk, dgk_b = _rowwise(f_gk_bwd, [full(dla), blk(p, "gk")], [w["gla_gk_w"], w["gla_gk_b"]],
                                [(GLA_HEADS * GLA_DK, BF16), (LANE, BF16)], [GLA_HEADS * GLA_DK], tile=512,
                                name="gla_gate_bwd")
    p_gk = lax.slice_in_dim(p, lay["gk"], lay["gk"] + LANE, axis=1)
    g_gk_w = _mm(p_gk, dz, ta=True, name="mm_gla_gk_dw")[:GLA_GATE_RANK]

    dqa, delta = _flash_bwd_dq(qa, ka, va, do_m, o_mla, lse)
    dka, dva = _flash_bwd_dkv(qa, ka, va, do_m, lse, delta)

    def f_qkv_bwd(dq, dk, dvv, tb):
        dqs, dkn = [], []
        dkr = jnp.zeros((dq.shape[0], LANE), F32)
        for hh in range(MLA_HEADS):
            o0 = hh * MLA_QK_PAD
            dqs += [dq[:, o0:o0 + LANE], _rope(dq[:, o0 + LANE:o0 + 2 * LANE], tb, -1.0)]
            dkn.append(dk[:, o0:o0 + LANE])
            dkr = dkr + dk[:, o0 + LANE:o0 + 2 * LANE]
        return jnp.concatenate(dqs, axis=1), jnp.concatenate(dkn + [dvv], axis=1), dkr

    dqlat, dkvl, dkrr = _rowwise(f_qkv_bwd, [full(dqa), full(dka), full(dva), full(tab)], [],
                                 [(MLA_HEADS * MLA_QK_PAD, BF16), (2 * hv, BF16), (LANE, F32)], tile=256,
                                 name="mla_qkv_build_bwd")
    dcqn = _mm(dqlat, w["mla_wuq"], tb=True, name="mm_mla_wuq_dx")
    g_wuq = _mm(cqn, dqlat, ta=True, name="mm_mla_wuq_dw")
    dckvn = _mm(dkvl, w["mla_wukv"], tb=True, name="mm_mla_wukv_dx")
    g_wukv = _mm(ckvn, dkvl, ta=True, name="mm_mla_wukv_dw")

    def f_mla_prep_bwd(dq, dkv, dkr, cq, ckv, tb, gq, gkv):
        rq, rk = _rstd(cq), _rstd(ckv)
        xq, xk = cq * rq, ckv * rk
        return (_rms_bwd(dq * gq, xq, rq), _rms_bwd(dkv * gkv, xk, rk), _rope(dkr, tb, -1.0),
                jnp.sum(dq * xq, axis=0, keepdims=True), jnp.sum(dkv * xk, axis=0, keepdims=True))

    dp_cq, dp_ckv, dp_kr, dg_q, dg_kv = _rowwise(
        f_mla_prep_bwd, [full(dcqn), full(dckvn), full(dkrr), blk(p, "cq"), blk(p, "ckv"), full(tab)],
        [w["mla_q_norm_g"], w["mla_kv_norm_g"]], [(pw["cq"], BF16), (pw["ckv"], BF16), (LANE, BF16)],
        [pw["cq"], pw["ckv"]], tile=512, name="mla_prep_bwd")

    pieces = dict(v=dp_v, g=dp_g, ga=dp_ga, gb=dp_gb, q=dp_q, k=dp_k, cq=dp_cq, ckv=dp_ckv, gk=dp_gk, kr=dp_kr)
    dp = jnp.concatenate([pieces[n] for n in MY_ORDER], axis=1)
    dh = _mm(dp, w["w_in"], tb=True, name="mm_in_proj_dx")
    g_w_in = _mm(h, dp, ta=True, name="mm_in_proj_dw")

    def f_ln1_bwd(xv, dhv, dx2v, mod, g):
        r = _rstd(xv)
        xh = xv * r
        dn = dhv * (1.0 + mod[1:2])
        return (dx2v + _rms_bwd(dn * g, xh, r),
                jnp.sum(dhv * (xh * g), axis=0, keepdims=True), jnp.sum(dhv, axis=0, keepdims=True),
                jnp.sum(dn * xh, axis=0, keepdims=True))

    grad_x, dscale_m, dshift_m, dg1 = _rowwise(f_ln1_bwd, [full(x), full(dh), full(dx2)], [mod8, g1],
                                               [(d, F32)], [d, d, d], tile=256, name="ln1_bwd")

    dmod = jnp.concatenate([dshift_m, dscale_m, dgate_m, dshift_f, dscale_f, dgate_f], axis=1)
    big = dict(w_in=g_w_in, gla_gk_w=g_gk_w, gla_wo=g_gla_wo, mla_wuq=g_wuq, mla_wukv=g_wukv,
               mla_wo=g_mla_wo, w_out=g_w_out, ffn_w_in=g_ffn_in, ffn_w_down=g_ffn_down)
    small = dict(ada_b=dmod, norm_mix_g=dg1, gla_gk_b=dgk_b, gla_onorm_g=dg_on, mla_q_norm_g=dg_q,
                 mla_kv_norm_g=dg_kv, norm_ffn_g=dg2, final_norm_g=dg3)
    return loss_v[0, 0], grad_x, big, small


def _exchange(arrs, *, scatter, name):
    n = len(arrs)
    n_peer = N_DEV - 1

    def body(*refs):
        ins, outs = refs[:n], refs[n:2 * n]
        send_sems, recv_sems, local_sems = refs[2 * n:]
        x, y, c = lax.axis_index("x"), lax.axis_index("y"), lax.axis_index("c")
        me = 4 * x + 2 * y + c
        peers = []
        for rel in range(1, N_DEV):
            px = 1 - x if rel & 4 else x
            py = 1 - y if rel & 2 else y
            pc = 1 - c if rel & 1 else c
            peers.append(((px, py, pc), 4 * px + 2 * py + pc))

        def remote(a, k, src_slot, dst_slot):
            src = ins[a].at[src_slot] if scatter else ins[a]
            return pltpu.make_async_remote_copy(
                src_ref=src, dst_ref=outs[a].at[dst_slot], send_sem=send_sems.at[a * n_peer + k],
                recv_sem=recv_sems.at[a * n_peer + k], device_id=peers[k][0], device_id_type=pl.DeviceIdType.MESH)

        local, sends = [], []
        for a in range(n):
            src = ins[a].at[me] if scatter else ins[a]
            cp = pltpu.make_async_copy(src, outs[a].at[me], local_sems.at[a])
            cp.start()
            local.append(cp)
            for k in range(n_peer):
                cp = remote(a, k, peers[k][1], me)
                cp.start()
                sends.append(cp)
        for a in range(n):
            for k in range(n_peer):
                remote(a, k, peers[k][1], peers[k][1]).wait_recv()
        for cp in sends:
            cp.wait_send()
        for cp in local:
            cp.wait()

    hbm = pl.BlockSpec(memory_space=pltpu.HBM)
    out_shape = [jax.ShapeDtypeStruct(a.shape if scatter else (N_DEV,) + a.shape, a.dtype) for a in arrs]
    return pl.pallas_call(
        body, name=name, in_specs=[hbm] * n, out_specs=[hbm] * n, out_shape=out_shape,
        scratch_shapes=[pltpu.SemaphoreType.DMA((n * n_peer,)), pltpu.SemaphoreType.DMA((n * n_peer,)),
                        pltpu.SemaphoreType.DMA((n,))],
    )(*arrs)


def _adamw_math(w, g, m, v):
    m_new = ADAM_B1 * m + (1.0 - ADAM_B1) * g
    v_new = ADAM_B2 * v + (1.0 - ADAM_B2) * (g * g)
    m_hat = m_new / (1.0 - ADAM_B1 ** ADAM_STEP)
    v_hat = v_new / (1.0 - ADAM_B2 ** ADAM_STEP)
    delta = -ADAM_LR * (m_hat / (jnp.sqrt(v_hat) + ADAM_EPS) + ADAM_WD * w)
    return delta, m_new, v_new


def _adamw(w, g, m, v, *, name):
    r, c = w.shape
    slots = g.ndim == 3
    tr = r
    for cand in (128, 64, 32, 16):
        if r % cand == 0 and r > cand:
            tr = cand
            break

    def body(w_ref, g_ref, m_ref, v_ref, go_ref, d_ref, mo_ref, vo_ref):
        if slots:
            gv = g_ref[0].astype(F32)
            for s in range(1, N_DEV):
                gv = gv + g_ref[s].astype(F32)
        else:
            gv = g_ref[...]
        delta, m_new, v_new = _adamw_math(w_ref[...], gv, m_ref[...], v_ref[...])
        go_ref[...] = gv
        d_ref[...] = delta
        mo_ref[...] = m_new
        vo_ref[...] = v_new

    spec = pl.BlockSpec((tr, c), lambda i: (i, 0))
    g_spec = pl.BlockSpec((N_DEV, tr, c), lambda i: (0, i, 0)) if slots else spec
    return pl.pallas_call(
        body, name=name, grid=(r // tr,), in_specs=[spec, g_spec, spec, spec], out_specs=[spec] * 4,
        out_shape=[jax.ShapeDtypeStruct((r, c), F32)] * 4, compiler_params=_cparams(("parallel",)),
    )(w, g, m, v)


def _unshard_cols(g):
    return jnp.transpose(g, (1, 0, 2)).reshape(g.shape[1], -1)

def _shard_cols(full):
    r = full.shape[0]
    return jnp.transpose(full.reshape(r, N_DEV, -1), (1, 0, 2))


def _w_in_to_mine(w_ref_layout, d):
    wd, pw, ref_off, _, _ = _in_layout(d)
    cols = []
    for n in MY_ORDER:
        piece = lax.slice_in_dim(w_ref_layout, ref_off[n], ref_off[n] + wd[n], axis=1)
        if pw[n] != wd[n]:
            piece = jnp.pad(piece, ((0, 0), (0, pw[n] - wd[n])))
        cols.append(piece)
    return jnp.concatenate(cols, axis=1)


def _w_in_from_mine(g_mine, d):
    wd, _, _, my_off, _ = _in_layout(d)
    return jnp.concatenate([lax.slice_in_dim(g_mine, my_off[n], my_off[n] + wd[n], axis=1) for n in IN_NAMES],
                           axis=1)


def _wuq_to_mine(wq):
    r = wq.shape[0]
    w3 = wq.reshape(r, MLA_HEADS, MLA_NOPE + MLA_ROPE)
    w3 = jnp.pad(w3, ((0, 0), (0, 0), (0, MLA_QK_PAD - MLA_NOPE - MLA_ROPE)))
    return w3.reshape(r, MLA_HEADS * MLA_QK_PAD)


def _wuq_from_mine(g):
    r = g.shape[0]
    return g.reshape(r, MLA_HEADS, MLA_QK_PAD)[:, :, :MLA_NOPE + MLA_ROPE].reshape(r, -1)


def _wukv_to_mine(wkv):
    r = wkv.shape[0]
    w3 = wkv.reshape(r, MLA_HEADS, MLA_NOPE + MLA_V)
    return jnp.concatenate([w3[:, :, :MLA_NOPE].reshape(r, -1), w3[:, :, MLA_NOPE:].reshape(r, -1)], axis=1)


def _wukv_from_mine(g):
    r = g.shape[0]
    kn = g[:, :MLA_HEADS * MLA_NOPE].reshape(r, MLA_HEADS, MLA_NOPE)
    vv = g[:, MLA_HEADS * MLA_NOPE:].reshape(r, MLA_HEADS, MLA_V)
    return jnp.concatenate([kn, vv], axis=2).reshape(r, -1)


COL_SHARDED = ("w_in", "gla_gk_w", "mla_wuq", "mla_wukv", "ffn_w_in")


def kernel(x, c, positions, ada_w, ada_b, norm_mix_g, w_in, gla_gk_w, gla_gk_b, gla_onorm_g, gla_wo, mla_q_norm_g, mla_wuq, mla_kv_norm_g, mla_wukv, mla_wo, w_out, norm_ffn_g, ffn_w_in, ffn_w_down, final_norm_g, loss_target, m_ada_w, m_ada_b, m_norm_mix_g, m_w_in, m_gla_gk_w, m_gla_gk_b, m_gla_onorm_g, m_gla_wo, m_mla_q_norm_g, m_mla_wuq, m_mla_kv_norm_g, m_mla_wukv, m_mla_wo, m_w_out, m_norm_ffn_g, m_ffn_w_in, m_ffn_w_down, m_final_norm_g, v_ada_w, v_ada_b, v_norm_mix_g, v_w_in, v_gla_gk_w, v_gla_gk_b, v_gla_onorm_g, v_gla_wo, v_mla_q_norm_g, v_mla_wuq, v_mla_kv_norm_g, v_mla_wukv, v_mla_wo, v_w_out, v_norm_ffn_g, v_ffn_w_in, v_ffn_w_down, v_final_norm_g):
    wts = dict(ada_w=ada_w, ada_b=ada_b, norm_mix_g=norm_mix_g, w_in=w_in, gla_gk_w=gla_gk_w, gla_gk_b=gla_gk_b,
               gla_onorm_g=gla_onorm_g, gla_wo=gla_wo, mla_q_norm_g=mla_q_norm_g, mla_wuq=mla_wuq,
               mla_kv_norm_g=mla_kv_norm_g, mla_wukv=mla_wukv, mla_wo=mla_wo, w_out=w_out, norm_ffn_g=norm_ffn_g,
               ffn_w_in=ffn_w_in, ffn_w_down=ffn_w_down, final_norm_g=final_norm_g)
    mom_m = dict(zip(WEIGHTS, (m_ada_w, m_ada_b, m_norm_mix_g, m_w_in, m_gla_gk_w, m_gla_gk_b, m_gla_onorm_g,
                               m_gla_wo, m_mla_q_norm_g, m_mla_wuq, m_mla_kv_norm_g, m_mla_wukv, m_mla_wo, m_w_out,
                               m_norm_ffn_g, m_ffn_w_in, m_ffn_w_down, m_final_norm_g)))
    mom_v = dict(zip(WEIGHTS, (v_ada_w, v_ada_b, v_norm_mix_g, v_w_in, v_gla_gk_w, v_gla_gk_b, v_gla_onorm_g,
                               v_gla_wo, v_mla_q_norm_g, v_mla_wuq, v_mla_kv_norm_g, v_mla_wukv, v_mla_wo, v_w_out,
                               v_norm_ffn_g, v_ffn_w_in, v_ffn_w_down, v_final_norm_g)))
    seq, d = x.shape[1], x.shape[2]
    me = 4 * lax.axis_index("x") + 2 * lax.axis_index("y") + lax.axis_index("c")

    def two_d(a):
        return a.reshape(a.shape[-2], a.shape[-1]) if a.ndim >= 2 else a.reshape(1, -1)

    shard = {n: two_d(wts[n]) for n in BIG}
    send = [shard[n].astype(F32 if n == "gla_gk_w" else BF16) for n in BIG] + [two_d(c)]
    got = _exchange(send, scatter=False, name="comm_all_gather_weights")
    gathered = dict(zip(BIG, got[:-1]))
    c_all = got[-1].reshape(N_DEV, d)
    fullw = {}
    for n in BIG:
        g = gathered[n]
        fullw[n] = _unshard_cols(g) if n in COL_SHARDED else g.reshape(-1, g.shape[-1])
    w = dict(
        w_in=_w_in_to_mine(fullw["w_in"], d),
        gla_gk_w=jnp.pad(fullw["gla_gk_w"], ((0, LANE - GLA_GATE_RANK), (0, 0))),
        gla_wo=fullw["gla_wo"], mla_wuq=_wuq_to_mine(fullw["mla_wuq"]), mla_wukv=_wukv_to_mine(fullw["mla_wukv"]),
        mla_wo=fullw["mla_wo"], w_out=fullw["w_out"], ffn_w_in=fullw["ffn_w_in"], ffn_w_down=fullw["ffn_w_down"],
        gla_gk_b=two_d(gla_gk_b), gla_onorm_g=two_d(gla_onorm_g), mla_q_norm_g=two_d(mla_q_norm_g),
        mla_kv_norm_g=two_d(mla_kv_norm_g), norm_mix_g=two_d(norm_mix_g), norm_ffn_g=two_d(norm_ffn_g),
        final_norm_g=two_d(final_norm_g))

    c_pad = jnp.pad(c_all, ((0, 16 - N_DEV), (0, 0)))
    (c_act,) = _rowwise(lambda cv: cv * _sigmoid(cv), [(c_pad, d, 0)], [], [(d, F32)], tile=16, name="silu_c")
    ada_w2 = two_d(ada_w)
    mod_part = _mm(c_act, ada_w2, name="mm_ada")[:N_DEV]
    (mod_all,) = _exchange([mod_part], scatter=False, name="comm_all_gather_mod")
    mod_mine = lax.dynamic_index_in_dim(mod_all, me, axis=1, keepdims=False).reshape(1, -1) + two_d(ada_b)
    mod8 = jnp.pad(mod_mine.reshape(6, d), ((0, 2), (0, 0)))

    inv_freq = ROPE_THETA ** (-jnp.arange(0, MLA_ROPE, 2, dtype=F32) / MLA_ROPE)
    ang = positions.reshape(seq, 1).astype(F32) * inv_freq[None, :]
    cos, sin, z32 = jnp.cos(ang), jnp.sin(ang), jnp.zeros((seq, 32), F32)
    tab = jnp.concatenate([cos, cos, z32, z32, -sin, z32, z32, z32, z32, sin, z32, z32], axis=1)
    loss_local, grad_x, big, small = _local_step(x.reshape(seq, d), loss_target.reshape(seq, d), tab, mod8, w)

    ref_layout = dict(big)
    ref_layout["w_in"] = _w_in_from_mine(big["w_in"], d)
    ref_layout["mla_wuq"] = _wuq_from_mine(big["mla_wuq"])
    ref_layout["mla_wukv"] = _wukv_from_mine(big["mla_wukv"])
    slabs = []
    for n in BIG:
        g = ref_layout[n]
        s = _shard_cols(g) if n in COL_SHARDED else g.reshape(N_DEV, -1, g.shape[-1])
        slabs.append(s.astype(BF16))
    pack = jnp.concatenate([small[n] for n in SMALL], axis=1)
    recv = dict(zip(BIG, _exchange(slabs, scatter=True, name="comm_reduce_scatter_grads")))
    (pack_all,) = _exchange([pack], scatter=False, name="comm_all_gather_small")
    pack_all = pack_all.reshape(N_DEV, -1)

    res = {}
    for n in BIG:
        res[n] = _adamw(shard[n], recv[n], two_d(mom_m[n]), two_d(mom_v[n]), name="adamw_" + n)
    n_ada = ada_w2.shape[1]
    dmod_cols = lax.dynamic_slice_in_dim(pack_all[:, :6 * d], me * n_ada, n_ada, axis=1)

    def f_outer(cat, dm):
        acc = cat[:, 0:1] * dm[0:1]
        for b in range(1, N_DEV):
            acc = acc + cat[:, b:b + 1] * dm[b:b + 1]
        return acc

    (g_ada_w,) = _rowwise(f_outer, [(jnp.transpose(c_act[:N_DEV]), N_DEV, 0)], [dmod_cols], [(n_ada, F32)],
                          tile=256, name="ada_w_grad")
    res["ada_w"] = _adamw(ada_w2, g_ada_w, two_d(m_ada_w), two_d(v_ada_w), name="adamw_ada_w")
    w_small = jnp.concatenate([two_d(wts[n]) for n in SMALL], axis=1)
    m_small = jnp.concatenate([two_d(mom_m[n]) for n in SMALL], axis=1)
    v_small = jnp.concatenate([two_d(mom_v[n]) for n in SMALL], axis=1)
    small_res = _adamw(w_small, pack_all.reshape(N_DEV, 1, -1), m_small, v_small, name="adamw_small")
    off = 0
    for n in SMALL:
        width = wts[n].size
        res[n] = tuple(lax.slice_in_dim(a, off, off + width, axis=1) for a in small_res)
        off += width

    loss = lax.psum(loss_local, ("x", "y", "c"))
    outs = [loss, grad_x.reshape(x.shape)]
    for kind in range(4):
        outs += [res[n][kind].reshape(wts[n].shape) for n in WEIGHTS]
    return tuple(outs)
```

```python
import functools

import jax
import jax.numpy as jnp
from jax import lax
from jax.experimental import pallas as pl
from jax.experimental.pallas import tpu as pltpu

F32 = jnp.float32
BF16 = jnp.bfloat16

N_DEV = 8
GLA_HEADS = 4
GLA_DK = 256
GLA_DV = 512
GLA_GATE_RANK = 16
GLA_GATE_NORMALIZER = 16.0
GLA_CHUNK = 64
MLA_HEADS = 16
MLA_NOPE = 128
MLA_ROPE = 64
MLA_V = 128
MLA_QK_PAD = 256
ROPE_THETA = 10000.0
NORM_EPS = 1e-6
ATT_SCALE = (MLA_NOPE + MLA_ROPE) ** -0.5
GLA_QSCALE = GLA_DK ** -0.5

ADAM_LR = 0.001
ADAM_B1 = 0.9
ADAM_B2 = 0.999
ADAM_EPS = 1e-08
ADAM_WD = 0.01
ADAM_STEP = 10

LANE = 128
VMEM_LIMIT = 48 * 1024 * 1024
MM_TILE_BYTES = 4 * 1024 * 1024
LOG2E = 1.4426950408889634
LN2 = 0.6931471805599453
NEG = -1e30

IN_NAMES = ("q", "k", "v", "g", "gk", "cq", "ckv", "kr", "ga", "gb")
MY_ORDER = ("v", "g", "ga", "gb", "q", "k", "cq", "ckv", "gk", "kr")

WEIGHTS = ("ada_w", "ada_b", "norm_mix_g", "w_in", "gla_gk_w", "gla_gk_b", "gla_onorm_g", "gla_wo",
           "mla_q_norm_g", "mla_wuq", "mla_kv_norm_g", "mla_wukv", "mla_wo", "w_out", "norm_ffn_g",
           "ffn_w_in", "ffn_w_down", "final_norm_g")
BIG = ("w_in", "gla_gk_w", "gla_wo", "mla_wuq", "mla_wukv", "mla_wo", "w_out", "ffn_w_in", "ffn_w_down")
SMALL = ("ada_b", "norm_mix_g", "gla_gk_b", "gla_onorm_g", "mla_q_norm_g", "mla_kv_norm_g", "norm_ffn_g",
         "final_norm_g")


def _in_layout(d):
    w = dict(q=d // 2, k=d // 2, v=d, g=d, gk=GLA_GATE_RANK, cq=d // 4, ckv=512, kr=MLA_ROPE, ga=d, gb=d)
    pw = {n: -(-w[n] // LANE) * LANE for n in w}
    ref_off, o = {}, 0
    for n in IN_NAMES:
        ref_off[n] = o
        o += w[n]
    my_off, o = {}, 0
    for n in MY_ORDER:
        assert o % pw[n] == 0
        my_off[n] = o
        o += pw[n]
    return w, pw, ref_off, my_off, o


def _cparams(sem=None):
    return pltpu.CompilerParams(dimension_semantics=sem, vmem_limit_bytes=VMEM_LIMIT)


def _dot(a, b, ca=1, cb=0):
    return lax.dot_general(a, b, (((ca,), (cb,)), ((), ())), preferred_element_type=F32)


def _tile(n, cap):
    if n <= cap:
        return n
    t = (cap // LANE) * LANE
    while t >= LANE:
        if n % t == 0:
            return t
        t -= LANE
    return n


def _mm(a, b, *, ta=False, tb=False, out_dtype=F32, name):
    m, k = (a.shape[1], a.shape[0]) if ta else a.shape
    n = b.shape[0] if tb else b.shape[1]
    assert k == (b.shape[1] if tb else b.shape[0])
    wide = max(a.dtype.itemsize, b.dtype.itemsize) > 2
    tm, tn, tk = _tile(m, 1024), _tile(n, 1024), _tile(k, MM_TILE_BYTES // (1024 * (4 if wide else 2)))
    nk = k // tk

    def product(a_ref, b_ref):
        return _dot(a_ref[...].astype(BF16), b_ref[...].astype(BF16), 0 if ta else 1, 1 if tb else 0)

    def body_one(a_ref, b_ref, o_ref):
        o_ref[...] = product(a_ref, b_ref).astype(o_ref.dtype)

    def body_acc(a_ref, b_ref, o_ref, acc_ref):
        kk = pl.program_id(2)

        @pl.when(kk == 0)
        def _():
            acc_ref[...] = jnp.zeros_like(acc_ref)

        acc_ref[...] += product(a_ref, b_ref)

        @pl.when(kk == nk - 1)
        def _():
            o_ref[...] = acc_ref[...].astype(o_ref.dtype)

    a_spec = (pl.BlockSpec((tk, tm), lambda i, j, kk: (kk, i)) if ta
              else pl.BlockSpec((tm, tk), lambda i, j, kk: (i, kk)))
    b_spec = (pl.BlockSpec((tn, tk), lambda i, j, kk: (j, kk)) if tb
              else pl.BlockSpec((tk, tn), lambda i, j, kk: (kk, j)))
    return pl.pallas_call(
        body_one if nk == 1 else body_acc, name=name, grid=(m // tm, n // tn, nk), in_specs=[a_spec, b_spec],
        out_specs=pl.BlockSpec((tm, tn), lambda i, j, kk: (i, j)),
        out_shape=jax.ShapeDtypeStruct((m, n), out_dtype),
        scratch_shapes=[] if nk == 1 else [pltpu.VMEM((tm, tn), F32)],
        compiler_params=_cparams(("parallel", "parallel", "arbitrary")),
    )(a, b)


def _rowwise(fn, rows, vecs, outs, sums=(), *, tile, name):
    t = rows[0][0].shape[0]
    tile = min(tile, t)
    assert t % tile == 0
    n_rows, n_vecs, n_outs = len(rows), len(vecs), len(outs)

    def body(*refs):
        ins = [r[...] for r in refs[:n_rows + n_vecs]]
        res = fn(*ins)
        if not isinstance(res, (tuple, list)):
            res = (res,)
        out_refs = refs[n_rows + n_vecs:]
        for r, val in zip(out_refs[:n_outs], res[:n_outs]):
            r[...] = val.astype(r.dtype)
        if sums:
            first = pl.program_id(0) == 0
            for r, val in zip(out_refs[n_outs:], res[n_outs:]):
                @pl.when(first)
                def _(r=r):
                    r[...] = jnp.zeros_like(r)
                r[...] += val

    in_specs = [pl.BlockSpec((tile, w), lambda i, cb=cb: (i, cb)) for (_, w, cb) in rows]
    in_specs += [pl.BlockSpec(v.shape, lambda i: (0, 0)) for v in vecs]
    out_specs = [pl.BlockSpec((tile, w), lambda i: (i, 0)) for (w, _) in outs]
    out_specs += [pl.BlockSpec((1, w), lambda i: (0, 0)) for w in sums]
    out_shape = [jax.ShapeDtypeStruct((t, w), dt) for (w, dt) in outs]
    out_shape += [jax.ShapeDtypeStruct((1, w), F32) for w in sums]
    res = pl.pallas_call(
        body, name=name, grid=(t // tile,), in_specs=in_specs, out_specs=out_specs, out_shape=out_shape,
        compiler_params=_cparams(("arbitrary",)),
    )(*[r[0] for r in rows], *vecs)
    return res


def _rstd(x):
    return lax.rsqrt(jnp.mean(x * x, axis=-1, keepdims=True) + NORM_EPS)


def _sigmoid(x):
    return 1.0 / (1.0 + jnp.exp(-x))


def _rms_bwd(dxh, xh, r):
    return r * (dxh - xh * jnp.mean(dxh * xh, axis=-1, keepdims=True))


def _rope(t, tab, sign):
    cosf, sin_a, sin_b = tab[:, :LANE], tab[:, LANE:2 * LANE], tab[:, 2 * LANE:]
    return t * cosf + sign * (pltpu.roll(t, 96, 1) * sin_a + pltpu.roll(t, 32, 1) * sin_b)


def _split3(x):
    hi = x.astype(BF16)
    r1 = x - hi.astype(F32)
    mid = r1.astype(BF16)
    lo = (r1 - mid.astype(F32)).astype(BF16)
    return hi, mid, lo


def _tri_sum(tri_bf16, x):
    hi, mid, lo = _split3(x)
    return _dot(tri_bf16, hi) + _dot(tri_bf16, mid) + _dot(tri_bf16, lo)


def _dot_nt2(a, b):
    a_hi = a.astype(BF16)
    a_lo = (a - a_hi.astype(F32)).astype(BF16)
    b_hi = b.astype(BF16)
    b_lo = (b - b_hi.astype(F32)).astype(BF16)
    return _dot(a_hi, b_hi, 1, 1) + _dot(a_hi, b_lo, 1, 1) + _dot(a_lo, b_hi, 1, 1)


def _gla_specs(t, rows, lay, reverse):
    nb = t // rows
    blk = (lambda i: nb - 1 - i) if reverse else (lambda i: i)
    qb, kb = lay["q"] // GLA_DK, lay["k"] // GLA_DK
    vb = lay["v"] // GLA_DV
    return [
        pl.BlockSpec((rows, GLA_DK), lambda h, i: (blk(i), qb + h)),
        pl.BlockSpec((rows, GLA_DK), lambda h, i: (blk(i), kb + h)),
        pl.BlockSpec((rows, GLA_DV), lambda h, i: (blk(i), vb + h)),
        pl.BlockSpec((rows, GLA_DK), lambda h, i: (blk(i), h)),
    ], blk


def _gla_fwd(p, la, lay):
    t = p.shape[0]
    rows = min(512, t)
    nb, nc = t // rows, rows // GLA_CHUNK
    c64 = GLA_CHUNK

    def body(q_ref, k_ref, v_ref, la_ref, o_ref, st_ref, s_ref):
        @pl.when(pl.program_id(1) == 0)
        def _():
            s_ref[...] = jnp.zeros_like(s_ref)

        r = lax.broadcasted_iota(jnp.int32, (c64, c64), 0)
        cc = lax.broadcasted_iota(jnp.int32, (c64, c64), 1)
        tril = cc <= r
        tril_b = tril.astype(BF16)
        for c in range(nc):
            sl = pl.ds(c * c64, c64)
            b = _tri_sum(tril_b, la_ref[sl, :])
            b_last = b[c64 - 1:c64, :]
            q = q_ref[sl, :] * GLA_QSCALE
            k = k_ref[sl, :]
            v = v_ref[sl, :].astype(BF16)
            qt_f = q * jnp.exp(b)
            qt = qt_f.astype(BF16)
            kh = (k * jnp.exp(b_last - b)).astype(BF16)
            s_prev = s_ref[...]
            st_ref[0, c] = s_prev
            att = jnp.where(tril, _dot_nt2(qt_f, k * jnp.exp(-b)), 0.0)
            o_ref[sl, :] = _dot(qt, s_prev.astype(BF16), 1, 1) + _dot(att.astype(BF16), v)
            s_ref[...] = s_prev * jnp.exp(b_last) + _dot(v, kh, 0, 0)

    in_specs, _ = _gla_specs(t, rows, lay, False)
    return pl.pallas_call(
        body, name="gla_fwd", grid=(GLA_HEADS, nb), in_specs=in_specs,
        out_specs=[pl.BlockSpec((rows, GLA_DV), lambda h, i: (i, h)),
                   pl.BlockSpec((1, nc, GLA_DV, GLA_DK), lambda h, i: (h, i, 0, 0))],
        out_shape=[jax.ShapeDtypeStruct((t, GLA_HEADS * GLA_DV), F32),
                   jax.ShapeDtypeStruct((GLA_HEADS, t // c64, GLA_DV, GLA_DK), F32)],
        scratch_shapes=[pltpu.VMEM((GLA_DV, GLA_DK), F32)],
        compiler_params=_cparams(("parallel", "arbitrary")),
    )(p, p, p, la)


def _gla_bwd(p, la, do, states, lay):
    t = p.shape[0]
    rows = min(512, t)
    nb, nc = t // rows, rows // GLA_CHUNK
    c64 = GLA_CHUNK

    def body(q_ref, k_ref, v_ref, la_ref, do_ref, st_ref, dq_ref, dk_ref, dv_ref, dla_ref, ds_ref):
        @pl.when(pl.program_id(1) == 0)
        def _():
            ds_ref[...] = jnp.zeros_like(ds_ref)

        r = lax.broadcasted_iota(jnp.int32, (c64, c64), 0)
        cc = lax.broadcasted_iota(jnp.int32, (c64, c64), 1)
        tril = cc <= r
        tril_b = tril.astype(BF16)
        triu_b = (cc >= r).astype(BF16)
        for c in reversed(range(nc)):
            sl = pl.ds(c * c64, c64)
            b = _tri_sum(tril_b, la_ref[sl, :])
            b_last = b[c64 - 1:c64, :]
            eb, enb, ebl_b, ebl = jnp.exp(b), jnp.exp(-b), jnp.exp(b_last - b), jnp.exp(b_last)
            k = k_ref[sl, :]
            qt_f = q_ref[sl, :] * GLA_QSCALE * eb
            kt_f = k * enb
            kh_f = k * ebl_b
            qt, kt, kh = qt_f.astype(BF16), kt_f.astype(BF16), kh_f.astype(BF16)
            v_f = v_ref[sl, :]
            dout_f = do_ref[sl, :]
            v, dout = v_f.astype(BF16), dout_f.astype(BF16)
            s_prev = st_ref[0, c]
            ds_next = ds_ref[...]
            ds_next_b = ds_next.astype(BF16)
            att = jnp.where(tril, _dot_nt2(qt_f, kt_f), 0.0).astype(BF16)
            datt = jnp.where(tril, _dot_nt2(dout_f, v_f), 0.0).astype(BF16)
            dqt = _dot(dout, s_prev.astype(BF16)) + _dot(datt, kt)
            dkt = _dot(datt, qt, 0, 0)
            dv = _dot(att, dout, 0, 0) + _dot(kh, ds_next_b, 1, 1)
            dkh = _dot(v, ds_next_b)
            d_ebl = jnp.sum(ds_next * s_prev, axis=0, keepdims=True)
            ds_ref[...] = ds_next * ebl + _dot(dout, qt, 0, 0)
            db = dqt * qt_f - dkt * kt_f - dkh * kh_f
            db_last = ebl * d_ebl + jnp.sum(dkh * kh_f, axis=0, keepdims=True)
            dq_ref[sl, :] = (dqt * eb * GLA_QSCALE).astype(dq_ref.dtype)
            dk_ref[sl, :] = (dkt * enb + dkh * ebl_b).astype(dk_ref.dtype)
            dv_ref[sl, :] = dv.astype(dv_ref.dtype)
            dla_ref[sl, :] = _tri_sum(triu_b, db) + db_last

    in_specs, blk = _gla_specs(t, rows, lay, True)
    in_specs += [pl.BlockSpec((rows, GLA_DV), lambda h, i: (blk(i), h)),
                 pl.BlockSpec((1, nc, GLA_DV, GLA_DK), lambda h, i: (h, blk(i), 0, 0))]
    dk_spec = pl.BlockSpec((rows, GLA_DK), lambda h, i: (blk(i), h))
    return pl.pallas_call(
        body, name="gla_bwd", grid=(GLA_HEADS, nb), in_specs=in_specs,
        out_specs=[dk_spec, dk_spec, pl.BlockSpec((rows, GLA_DV), lambda h, i: (blk(i), h)), dk_spec],
        out_shape=[jax.ShapeDtypeStruct((t, GLA_HEADS * GLA_DK), BF16),
                   jax.ShapeDtypeStruct((t, GLA_HEADS * GLA_DK), BF16),
                   jax.ShapeDtypeStruct((t, GLA_HEADS * GLA_DV), BF16),
                   jax.ShapeDtypeStruct((t, GLA_HEADS * GLA_DK), F32)],
        scratch_shapes=[pltpu.VMEM((GLA_DV, GLA_DK), F32)],
        compiler_params=_cparams(("parallel", "arbitrary")),
    )(p, p, p, la, do, states)


def _diag_mask(rows, cols, row0):
    row = row0 + lax.broadcasted_iota(jnp.int32, (rows, cols), 0)
    col = lax.broadcasted_iota(jnp.int32, (rows, cols), 1)
    return col <= row


def _flash_tiles(t):
    tq = min(1024, t)
    halves = 2 if tq % 32 == 0 else 1
    return tq, t // tq, halves, tq // halves


def _flash_fwd(q, k, vx):
    t = q.shape[0]
    tq, nq, halves, hr = _flash_tiles(t)
    dqk, dv = MLA_QK_PAD, MLA_V

    def body(q_ref, k_ref, v_ref, o_ref, lse_ref, m_ref, acc_ref):
        i = pl.program_id(1)
        m_ref[...] = jnp.full_like(m_ref, NEG)
        acc_ref[...] = jnp.zeros_like(acc_ref)

        def step(j, masked):
            off = pl.multiple_of(j * tq, tq)
            kb = k_ref[pl.ds(off, tq), :]
            vb = v_ref[pl.ds(off, tq), :]
            for hh in range(halves):
                rs = pl.ds(hh * hr, hr)
                s = _dot(q_ref[rs, :], kb, 1, 1)
                if masked:
                    s = jnp.where(_diag_mask(hr, tq, hh * hr), s, NEG)
                m_old = m_ref[rs, :]
                m_new = jnp.maximum(m_old, jnp.max(s, axis=1, keepdims=True))
                pr = jnp.exp2(s - m_new)
                acc_ref[rs, :] = jnp.exp2(m_old - m_new) * acc_ref[rs, :] + _dot(pr.astype(BF16), vb)
                m_ref[rs, :] = m_new

        def loop_body(j, carry):
            step(j, False)
            return carry

        lax.fori_loop(0, i, loop_body, 0)
        step(i, True)
        acc = acc_ref[...]
        l = acc[:, dv:dv + 1]
        o_ref[...] = acc[:, :dv] / l
        lse_ref[0] = m_ref[...] + jnp.log(l) * LOG2E

    return pl.pallas_call(
        body, name="mla_flash_fwd", grid=(MLA_HEADS, nq),
        in_specs=[pl.BlockSpec((tq, dqk), lambda h, i: (i, h)),
                  pl.BlockSpec((t, dqk), lambda h, i: (0, h)),
                  pl.BlockSpec((t, 2 * dv), lambda h, i: (0, h))],
        out_specs=[pl.BlockSpec((tq, dv), lambda h, i: (i, h)),
                   pl.BlockSpec((1, tq, 1), lambda h, i: (h, i, 0))],
        out_shape=[jax.ShapeDtypeStruct((t, MLA_HEADS * dv), F32),
                   jax.ShapeDtypeStruct((MLA_HEADS, t, 1), F32)],
        scratch_shapes=[pltpu.VMEM((tq, 1), F32), pltpu.VMEM((tq, 2 * dv), F32)],
        compiler_params=_cparams(("parallel", "arbitrary")),
    )(q, k, vx)


def _flash_bwd_dq(q, k, v, do, o, lse):
    t = q.shape[0]
    tq, nq, halves, hr = _flash_tiles(t)
    dqk, dv = MLA_QK_PAD, MLA_V

    def body(q_ref, k_ref, v_ref, do_ref, o_ref, lse_ref, dq_ref, dl_ref, acc_ref):
        i = pl.program_id(1)
        dl_ref[0] = jnp.sum(do_ref[...].astype(F32) * o_ref[...], axis=1, keepdims=True)
        acc_ref[...] = jnp.zeros_like(acc_ref)

        def step(j, masked):
            off = pl.multiple_of(j * tq, tq)
            kb = k_ref[pl.ds(off, tq), :]
            vb = v_ref[pl.ds(off, tq), :]
            for hh in range(halves):
                rs = pl.ds(hh * hr, hr)
                s = _dot(q_ref[rs, :], kb, 1, 1)
                if masked:
                    s = jnp.where(_diag_mask(hr, tq, hh * hr), s, NEG)
                pr = jnp.exp2(s - lse_ref[0, rs, :])
                dp = _dot(do_ref[rs, :], vb, 1, 1)
                ds = pr * (dp - dl_ref[0, rs, :])
                acc_ref[rs, :] += _dot(ds.astype(BF16), kb)

        def loop_body(j, carry):
            step(j, False)
            return carry

        lax.fori_loop(0, i, loop_body, 0)
        step(i, True)
        dq_ref[...] = acc_ref[...] * ATT_SCALE

    return pl.pallas_call(
        body, name="mla_flash_bwd_dq", grid=(MLA_HEADS, nq),
        in_specs=[pl.BlockSpec((tq, dqk), lambda h, i: (i, h)),
                  pl.BlockSpec((t, dqk), lambda h, i: (0, h)),
                  pl.BlockSpec((t, dv), lambda h, i: (0, h)),
                  pl.BlockSpec((tq, dv), lambda h, i: (i, h)),
                  pl.BlockSpec((tq, dv), lambda h, i: (i, h)),
                  pl.BlockSpec((1, tq, 1), lambda h, i: (h, i, 0))],
        out_specs=[pl.BlockSpec((tq, dqk), lambda h, i: (i, h)),
                   pl.BlockSpec((1, tq, 1), lambda h, i: (h, i, 0))],
        out_shape=[jax.ShapeDtypeStruct((t, MLA_HEADS * dqk), F32),
                   jax.ShapeDtypeStruct((MLA_HEADS, t, 1), F32)],
        scratch_shapes=[pltpu.VMEM((tq, dqk), F32)],
        compiler_params=_cparams(("parallel", "arbitrary")),
    )(q, k, v, do, o, lse)


def _flash_bwd_dkv(q, k, v, do, lse, delta):
    t = q.shape[0]
    tq, nq, halves, hr = _flash_tiles(t)
    dqk, dv = MLA_QK_PAD, MLA_V

    def body(k_ref, v_ref, q_ref, do_ref, lse_ref, dl_ref, dk_ref, dv_ref, dk_acc, dv_acc):
        j = pl.program_id(1)
        kb = k_ref[...]
        vb = v_ref[...]
        dk_acc[...] = jnp.zeros_like(dk_acc)
        dv_acc[...] = jnp.zeros_like(dv_acc)

        def step(i, masked):
            for hh in range(halves):
                rs = pl.ds(pl.multiple_of(i * tq + hh * hr, hr), hr)
                qb = q_ref[rs, :]
                dout = do_ref[rs, :]
                s = _dot(qb, kb, 1, 1)
                if masked:
                    s = jnp.where(_diag_mask(hr, tq, hh * hr), s, NEG)
                pr = jnp.exp2(s - lse_ref[0, rs, :])
                dv_acc[...] += _dot(pr.astype(BF16), dout, 0, 0)
                dp = _dot(dout, vb, 1, 1)
                ds = pr * (dp - dl_ref[0, rs, :])
                dk_acc[...] += _dot(ds.astype(BF16), qb, 0, 0)

        def loop_body(i, carry):
            step(i, False)
            return carry

        step(j, True)
        lax.fori_loop(j + 1, nq, loop_body, 0)
        dk_ref[...] = dk_acc[...] * LN2
        dv_ref[...] = dv_acc[...]

    return pl.pallas_call(
        body, name="mla_flash_bwd_dkv", grid=(MLA_HEADS, nq),
        in_specs=[pl.BlockSpec((tq, dqk), lambda h, j: (j, h)),
                  pl.BlockSpec((tq, dv), lambda h, j: (j, h)),
                  pl.BlockSpec((t, dqk), lambda h, j: (0, h)),
                  pl.BlockSpec((t, dv), lambda h, j: (0, h)),
                  pl.BlockSpec((1, t, 1), lambda h, j: (h, 0, 0)),
                  pl.BlockSpec((1, t, 1), lambda h, j: (h, 0, 0))],
        out_specs=[pl.BlockSpec((tq, dqk), lambda h, j: (j, h)),
                   pl.BlockSpec((tq, dv), lambda h, j: (j, h))],
        out_shape=[jax.ShapeDtypeStruct((t, MLA_HEADS * dqk), F32),
                   jax.ShapeDtypeStruct((t, MLA_HEADS * dv), F32)],
        scratch_shapes=[pltpu.VMEM((tq, dqk), F32), pltpu.VMEM((tq, dv), F32)],
        compiler_params=_cparams(("parallel", "arbitrary")),
    )(k, v, q, do, lse, delta)


def _local_step(x, target, tab, mod8, w):
    t, d = x.shape
    _, pw, _, lay, _ = _in_layout(d)
    ffn = w["ffn_w_down"].shape[0]

    def blk(arr, name):
        return (arr, pw[name], lay[name] // pw[name])

    def full(arr):
        return (arr, arr.shape[1], 0)

    g1, g2, g3 = w["norm_mix_g"], w["norm_ffn_g"], w["final_norm_g"]

    def f_ln1(xv, mod, g):
        return (xv * _rstd(xv) * g) * (1.0 + mod[1:2]) + mod[0:1]

    (h,) = _rowwise(f_ln1, [full(x)], [mod8, g1], [(d, BF16)], tile=256, name="ln1_modulate")
    p = _mm(h, w["w_in"], name="mm_in_proj")

    def f_gk(pgk, gkw, gkb):
        z = _dot(pgk.astype(BF16), gkw.astype(BF16)) + gkb
        return (jnp.minimum(z, 0.0) - jnp.log(1.0 + jnp.exp(-jnp.abs(z)))) / GLA_GATE_NORMALIZER

    (la,) = _rowwise(f_gk, [blk(p, "gk")], [w["gla_gk_w"], w["gla_gk_b"]], [(GLA_HEADS * GLA_DK, F32)],
                     tile=512, name="gla_gate")
    o_gla, states = _gla_fwd(p, la, lay)

    def f_gla_out(ov, pg, g):
        parts = []
        for hh in range(GLA_HEADS):
            oh = ov[:, hh * GLA_DV:(hh + 1) * GLA_DV]
            ph = pg[:, hh * GLA_DV:(hh + 1) * GLA_DV]
            parts.append(oh * _rstd(oh) * g * (ph * _sigmoid(ph)))
        return jnp.concatenate(parts, axis=1)

    (o_n,) = _rowwise(f_gla_out, [full(o_gla), blk(p, "g")], [w["gla_onorm_g"]], [(d, BF16)], tile=256,
                      name="gla_out_norm")
    y_gla = _mm(o_n, w["gla_wo"], name="mm_gla_wo")

    def f_mla_prep(cq, ckv, kr, tb, gq, gkv):
        return cq * _rstd(cq) * gq, ckv * _rstd(ckv) * gkv, _rope(kr, tb, 1.0)

    cqn, ckvn, krr = _rowwise(f_mla_prep, [blk(p, "cq"), blk(p, "ckv"), blk(p, "kr"), full(tab)],
                              [w["mla_q_norm_g"], w["mla_kv_norm_g"]],
                              [(pw["cq"], BF16), (pw["ckv"], BF16), (LANE, F32)], tile=512, name="mla_prep")
    qlat = _mm(cqn, w["mla_wuq"], name="mm_mla_wuq")
    kvl = _mm(ckvn, w["mla_wukv"], name="mm_mla_wukv")
    hv = MLA_HEADS * MLA_V

    def f_qkv(ql, kn, vv, kr, tb):
        qs, ks, vx = [], [], []
        one_col = (lax.broadcasted_iota(jnp.int32, (ql.shape[0], LANE), 1) == 0).astype(F32)
        for hh in range(MLA_HEADS):
            o0 = hh * MLA_QK_PAD
            qs += [ql[:, o0:o0 + LANE], _rope(ql[:, o0 + LANE:o0 + 2 * LANE], tb, 1.0)]
            ks += [kn[:, hh * LANE:(hh + 1) * LANE], kr]
            vx += [vv[:, hh * MLA_V:(hh + 1) * MLA_V], one_col]
        return (jnp.concatenate(qs, axis=1) * (ATT_SCALE * LOG2E), jnp.concatenate(ks, axis=1), vv,
                jnp.concatenate(vx, axis=1))

    qa, ka, va, vxa = _rowwise(f_qkv, [full(qlat), (kvl, hv, 0), (kvl, hv, 1), full(krr), full(tab)], [],
                               [(MLA_HEADS * MLA_QK_PAD, BF16), (MLA_HEADS * MLA_QK_PAD, BF16), (hv, BF16),
                                (2 * hv, BF16)], tile=256, name="mla_qkv_build")
    o_mla, lse = _flash_fwd(qa, ka, vxa)
    y_mla = _mm(o_mla, w["mla_wo"], name="mm_mla_wo")

    def f_merge(yg, ym, ga, gb):
        return _sigmoid(ga) * yg + _sigmoid(gb) * ym

    (merged,) = _rowwise(f_merge, [full(y_gla), full(y_mla), blk(p, "ga"), blk(p, "gb")], [], [(d, BF16)],
                         tile=256, name="merge")
    mix = _mm(merged, w["w_out"], name="mm_w_out")

    def f_res_ln2(xv, mx, mod, g):
        x2v = xv + mod[2:3] * mx
        return x2v, (x2v * _rstd(x2v) * g) * (1.0 + mod[4:5]) + mod[3:4]

    x2, h2 = _rowwise(f_res_ln2, [full(x), full(mix)], [mod8, g2], [(d, F32), (d, BF16)], tile=256,
                      name="res_ln2_modulate")
    gu = _mm(h2, w["ffn_w_in"], name="mm_ffn_in")

    def f_swiglu(gv, uv):
        return gv * _sigmoid(gv) * uv

    (act,) = _rowwise(f_swiglu, [(gu, ffn, 0), (gu, ffn, 1)], [], [(ffn, BF16)], tile=128, name="swiglu")
    f_out = _mm(act, w["ffn_w_down"], name="mm_ffn_down")

    def f_head(x2v, fv, tg, mod, g):
        x3 = x2v + mod[5:6] * fv
        r = _rstd(x3)
        xh = x3 * r
        e = xh * g - tg
        loss_rows = 0.5 * jnp.mean(e * e, axis=-1, keepdims=True)
        dy = e * (1.0 / d)
        dx3 = _rms_bwd(dy * g, xh, r)
        loss = jnp.broadcast_to(jnp.sum(loss_rows, axis=0, keepdims=True), (1, LANE))
        return (dx3, dx3 * mod[5:6], loss, jnp.sum(dy * xh, axis=0, keepdims=True),
                jnp.sum(dx3 * fv, axis=0, keepdims=True))

    dx3, df, loss_v, dg3, dgate_f = _rowwise(f_head, [full(x2), full(f_out), full(target)], [mod8, g3],
                                             [(d, F32), (d, BF16)], [LANE, d, d], tile=256, name="loss_head")
    da = _mm(df, w["ffn_w_down"], tb=True, name="mm_ffn_down_dx")
    g_ffn_down = _mm(act, df, ta=True, name="mm_ffn_down_dw")

    def f_swiglu_bwd(gv, uv, dav):
        sg = _sigmoid(gv)
        return jnp.concatenate([dav * uv * (sg * (1.0 + gv * (1.0 - sg))), dav * (gv * sg)], axis=1)

    (dgu,) = _rowwise(f_swiglu_bwd, [(gu, ffn, 0), (gu, ffn, 1), full(da)], [], [(2 * ffn, BF16)], tile=128,
                      name="swiglu_bwd")
    dh2 = _mm(dgu, w["ffn_w_in"], tb=True, name="mm_ffn_in_dx")
    g_ffn_in = _mm(h2, dgu, ta=True, name="mm_ffn_in_dw")

    def f_ln2_bwd(x2v, dh, dx3v, mx, mod, g):
        r = _rstd(x2v)
        xh = x2v * r
        dn = dh * (1.0 + mod[4:5])
        dx2 = dx3v + _rms_bwd(dn * g, xh, r)
        return (dx2, dx2 * mod[2:3],
                jnp.sum(dh * (xh * g), axis=0, keepdims=True), jnp.sum(dh, axis=0, keepdims=True),
                jnp.sum(dn * xh, axis=0, keepdims=True), jnp.sum(dx2 * mx, axis=0, keepdims=True))

    dx2, dmix, dscale_f, dshift_f, dg2, dgate_m = _rowwise(
        f_ln2_bwd, [full(x2), full(dh2), full(dx3), full(mix)], [mod8, g2], [(d, F32), (d, BF16)],
        [d, d, d, d], tile=256, name="ln2_bwd")
    dmerged = _mm(dmix, w["w_out"], tb=True, name="mm_w_out_dx")
    g_w_out = _mm(merged, dmix, ta=True, name="mm_w_out_dw")

    def f_merge_bwd(dm, yg, ym, ga, gb):
        sa, sb = _sigmoid(ga), _sigmoid(gb)
        return dm * sa, dm * sb, dm * yg * sa * (1.0 - sa), dm * ym * sb * (1.0 - sb)

    dy_gla, dy_mla, dp_ga, dp_gb = _rowwise(
        f_merge_bwd, [full(dmerged), full(y_gla), full(y_mla), blk(p, "ga"), blk(p, "gb")], [],
        [(d, BF16)] * 4, tile=256, name="merge_bwd")
    do_n = _mm(dy_gla, w["gla_wo"], tb=True, name="mm_gla_wo_dx")
    g_gla_wo = _mm(o_n, dy_gla, ta=True, name="mm_gla_wo_dw")
    do_m = _mm(dy_mla, w["mla_wo"], tb=True, out_dtype=BF16, name="mm_mla_wo_dx")
    g_mla_wo = _mm(o_mla, dy_mla, ta=True, name="mm_mla_wo_dw")

    def f_gla_out_bwd(don, ov, pg, g):
        dos, dpgs = [], []
        dg = jnp.zeros((1, GLA_DV), F32)
        for hh in range(GLA_HEADS):
            sl = slice(hh * GLA_DV, (hh + 1) * GLA_DV)
            oh, ph, dn = ov[:, sl], pg[:, sl], don[:, sl]
            r = _rstd(oh)
            xh = oh * r
            sg = _sigmoid(ph)
            dpre = dn * (ph * sg)
            dg = dg + jnp.sum(dpre * xh, axis=0, keepdims=True)
            dos.append(_rms_bwd(dpre * g, xh, r))
            dpgs.append(dn * (xh * g) * (sg * (1.0 + ph * (1.0 - sg))))
        return jnp.concatenate(dos, axis=1), jnp.concatenate(dpgs, axis=1), dg

    do_gla, dp_g, dg_on = _rowwise(f_gla_out_bwd, [full(do_n), full(o_gla), blk(p, "g")], [w["gla_onorm_g"]],
                                   [(d, F32), (d, BF16)], [GLA_DV], tile=256, name="gla_out_norm_bwd")
    dp_q, dp_k, dp_v, dla = _gla_bwd(p, la, do_gla, states, lay)

    def f_gk_bwd(dlav, pgk, gkw, gkb):
        z = _dot(pgk.astype(BF16), gkw.astype(BF16)) + gkb
        dz = dlav * (1.0 / GLA_GATE_NORMALIZER) * _sigmoid(-z)
        return dz, _dot(dz.astype(BF16), gkw.astype(BF16), 1, 1), jnp.sum(dz, axis=0, keepdims=True)

    dz, dp_gk, dgk_b = _rowwise(f_gk_bwd, [full(dla), blk(p, "gk")], [w["gla_gk_w"], w["gla_gk_b"]],
                                [(GLA_HEADS * GLA_DK, BF16), (LANE, BF16)], [GLA_HEADS * GLA_DK], tile=512,
                                name="gla_gate_bwd")
    p_gk = lax.slice_in_dim(p, lay["gk"], lay["gk"] + LANE, axis=1)
    g_gk_w = _mm(p_gk, dz, ta=True, name="mm_gla_gk_dw")[:GLA_GATE_RANK]

    dqa, delta = _flash_bwd_dq(qa, ka, va, do_m, o_mla, lse)
    dka, dva = _flash_bwd_dkv(qa, ka, va, do_m, lse, delta)

    def f_qkv_bwd(dq, dk, dvv, tb):
        dqs, dkn = [], []
        dkr = jnp.zeros((dq.shape[0], LANE), F32)
        for hh in range(MLA_HEADS):
            o0 = hh * MLA_QK_PAD
            dqs += [dq[:, o0:o0 + LANE], _rope(dq[:, o0 + LANE:o0 + 2 * LANE], tb, -1.0)]
            dkn.append(dk[:, o0:o0 + LANE])
            dkr = dkr + dk[:, o0 + LANE:o0 + 2 * LANE]
        return jnp.concatenate(dqs, axis=1), jnp.concatenate(dkn + [dvv], axis=1), dkr

    dqlat, dkvl, dkrr = _rowwise(f_qkv_bwd, [full(dqa), full(dka), full(dva), full(tab)], [],
                                 [(MLA_HEADS * MLA_QK_PAD, BF16), (2 * hv, BF16), (LANE, F32)], tile=256,
                                 name="mla_qkv_build_bwd")
    dcqn = _mm(dqlat, w["mla_wuq"], tb=True, name="mm_mla_wuq_dx")
    g_wuq = _mm(cqn, dqlat, ta=True, name="mm_mla_wuq_dw")
    dckvn = _mm(dkvl, w["mla_wukv"], tb=True, name="mm_mla_wukv_dx")
    g_wukv = _mm(ckvn, dkvl, ta=True, name="mm_mla_wukv_dw")

    def f_mla_prep_bwd(dq, dkv, dkr, cq, ckv, tb, gq, gkv):
        rq, rk = _rstd(cq), _rstd(ckv)
        xq, xk = cq * rq, ckv * rk
        return (_rms_bwd(dq * gq, xq, rq), _rms_bwd(dkv * gkv, xk, rk), _rope(dkr, tb, -1.0),
                jnp.sum(dq * xq, axis=0, keepdims=True), jnp.sum(dkv * xk, axis=0, keepdims=True))

    dp_cq, dp_ckv, dp_kr, dg_q, dg_kv = _rowwise(
        f_mla_prep_bwd, [full(dcqn), full(dckvn), full(dkrr), blk(p, "cq"), blk(p, "ckv"), full(tab)],
        [w["mla_q_norm_g"], w["mla_kv_norm_g"]], [(pw["cq"], BF16), (pw["ckv"], BF16), (LANE, BF16)],
        [pw["cq"], pw["ckv"]], tile=512, name="mla_prep_bwd")

    pieces = dict(v=dp_v, g=dp_g, ga=dp_ga, gb=dp_gb, q=dp_q, k=dp_k, cq=dp_cq, ckv=dp_ckv, gk=dp_gk, kr=dp_kr)
    dp = jnp.concatenate([pieces[n] for n in MY_ORDER], axis=1)
    dh = _mm(dp, w["w_in"], tb=True, name="mm_in_proj_dx")
    g_w_in = _mm(h, dp, ta=True, name="mm_in_proj_dw")

    def f_ln1_bwd(xv, dhv, dx2v, mod, g):
        r = _rstd(xv)
        xh = xv * r
        dn = dhv * (1.0 + mod[1:2])
        return (dx2v + _rms_bwd(dn * g, xh, r),
                jnp.sum(dhv * (xh * g), axis=0, keepdims=True), jnp.sum(dhv, axis=0, keepdims=True),
                jnp.sum(dn * xh, axis=0, keepdims=True))

    grad_x, dscale_m, dshift_m, dg1 = _rowwise(f_ln1_bwd, [full(x), full(dh), full(dx2)], [mod8, g1],
                                               [(d, F32)], [d, d, d], tile=256, name="ln1_bwd")

    dmod = jnp.concatenate([dshift_m, dscale_m, dgate_m, dshift_f, dscale_f, dgate_f], axis=1)
    big = dict(w_in=g_w_in, gla_gk_w=g_gk_w, gla_wo=g_gla_wo, mla_wuq=g_wuq, mla_wukv=g_wukv,
               mla_wo=g_mla_wo, w_out=g_w_out, ffn_w_in=g_ffn_in, ffn_w_down=g_ffn_down)
    small = dict(ada_b=dmod, norm_mix_g=dg1, gla_gk_b=dgk_b, gla_onorm_g=dg_on, mla_q_norm_g=dg_q,
                 mla_kv_norm_g=dg_kv, norm_ffn_g=dg2, final_norm_g=dg3)
    return loss_v[0, 0], grad_x, big, small


def _exchange(arrs, *, scatter, name):
    n = len(arrs)
    n_peer = N_DEV - 1

    def body(*refs):
        ins, outs = refs[:n], refs[n:2 * n]
        send_sems, recv_sems, local_sems = refs[2 * n:]
        x, y, c = lax.axis_index("x"), lax.axis_index("y"), lax.axis_index("c")
        me = 4 * x + 2 * y + c
        peers = []
        for rel in range(1, N_DEV):
            px = 1 - x if rel & 4 else x
            py = 1 - y if rel & 2 else y
            pc = 1 - c if rel & 1 else c
            peers.append(((px, py, pc), 4 * px + 2 * py + pc))

        def remote(a, k, src_slot, dst_slot):
            src = ins[a].at[src_slot] if scatter else ins[a]
            return pltpu.make_async_remote_copy(
                src_ref=src, dst_ref=outs[a].at[dst_slot], send_sem=send_sems.at[a * n_peer + k],
                recv_sem=recv_sems.at[a * n_peer + k], device_id=peers[k][0], device_id_type=pl.DeviceIdType.MESH)

        local, sends = [], []
        for a in range(n):
            src = ins[a].at[me] if scatter else ins[a]
            cp = pltpu.make_async_copy(src, outs[a].at[me], local_sems.at[a])
            cp.start()
            local.append(cp)
            for k in range(n_peer):
                cp = remote(a, k, peers[k][1], me)
                cp.start()
                sends.append(cp)
        for a in range(n):
            for k in range(n_peer):
                remote(a, k, peers[k][1], peers[k][1]).wait_recv()
        for cp in sends:
            cp.wait_send()
        for cp in local:
            cp.wait()

    hbm = pl.BlockSpec(memory_space=pltpu.HBM)
    out_shape = [jax.ShapeDtypeStruct(a.shape if scatter else (N_DEV,) + a.shape, a.dtype) for a in arrs]
    return pl.pallas_call(
        body, name=name, in_specs=[hbm] * n, out_specs=[hbm] * n, out_shape=out_shape,
        scratch_shapes=[pltpu.SemaphoreType.DMA((n * n_peer,)), pltpu.SemaphoreType.DMA((n * n_peer,)),
                        pltpu.SemaphoreType.DMA((n,))],
    )(*arrs)


def _adamw_math(w, g, m, v):
    m_new = ADAM_B1 * m + (1.0 - ADAM_B1) * g
    v_new = ADAM_B2 * v + (1.0 - ADAM_B2) * (g * g)
    m_hat = m_new / (1.0 - ADAM_B1 ** ADAM_STEP)
    v_hat = v_new / (1.0 - ADAM_B2 ** ADAM_STEP)
    delta = -ADAM_LR * (m_hat / (jnp.sqrt(v_hat) + ADAM_EPS) + ADAM_WD * w)
    return delta, m_new, v_new


def _adamw(w, g, m, v, *, name):
    r, c = w.shape
    slots = g.ndim == 3
    tr = r
    for cand in (128, 64, 32, 16):
        if r % cand == 0 and r > cand:
            tr = cand
            break

    def body(w_ref, g_ref, m_ref, v_ref, go_ref, d_ref, mo_ref, vo_ref):
        if slots:
            gv = g_ref[0].astype(F32)
            for s in range(1, N_DEV):
                gv = gv + g_ref[s].astype(F32)
        else:
            gv = g_ref[...]
        delta, m_new, v_new = _adamw_math(w_ref[...], gv, m_ref[...], v_ref[...])
        go_ref[...] = gv
        d_ref[...] = delta
        mo_ref[...] = m_new
        vo_ref[...] = v_new

    spec = pl.BlockSpec((tr, c), lambda i: (i, 0))
    g_spec = pl.BlockSpec((N_DEV, tr, c), lambda i: (0, i, 0)) if slots else spec
    return pl.pallas_call(
        body, name=name, grid=(r // tr,), in_specs=[spec, g_spec, spec, spec], out_specs=[spec] * 4,
        out_shape=[jax.ShapeDtypeStruct((r, c), F32)] * 4, compiler_params=_cparams(("parallel",)),
    )(w, g, m, v)


def _unshard_cols(g):
    return jnp.transpose(g, (1, 0, 2)).reshape(g.shape[1], -1)

def _shard_cols(full):
    r = full.shape[0]
    return jnp.transpose(full.reshape(r, N_DEV, -1), (1, 0, 2))


def _w_in_to_mine(w_ref_layout, d):
    wd, pw, ref_off, _, _ = _in_layout(d)
    cols = []
    for n in MY_ORDER:
        piece = lax.slice_in_dim(w_ref_layout, ref_off[n], ref_off[n] + wd[n], axis=1)
        if pw[n] != wd[n]:
            piece = jnp.pad(piece, ((0, 0), (0, pw[n] - wd[n])))
        cols.append(piece)
    return jnp.concatenate(cols, axis=1)


def _w_in_from_mine(g_mine, d):
    wd, _, _, my_off, _ = _in_layout(d)
    return jnp.concatenate([lax.slice_in_dim(g_mine, my_off[n], my_off[n] + wd[n], axis=1) for n in IN_NAMES],
                           axis=1)


def _wuq_to_mine(wq):
    r = wq.shape[0]
    w3 = wq.reshape(r, MLA_HEADS, MLA_NOPE + MLA_ROPE)
    w3 = jnp.pad(w3, ((0, 0), (0, 0), (0, MLA_QK_PAD - MLA_NOPE - MLA_ROPE)))
    return w3.reshape(r, MLA_HEADS * MLA_QK_PAD)


def _wuq_from_mine(g):
    r = g.shape[0]
    return g.reshape(r, MLA_HEADS, MLA_QK_PAD)[:, :, :MLA_NOPE + MLA_ROPE].reshape(r, -1)


def _wukv_to_mine(wkv):
    r = wkv.shape[0]
    w3 = wkv.reshape(r, MLA_HEADS, MLA_NOPE + MLA_V)
    return jnp.concatenate([w3[:, :, :MLA_NOPE].reshape(r, -1), w3[:, :, MLA_NOPE:].reshape(r, -1)], axis=1)


def _wukv_from_mine(g):
    r = g.shape[0]
    kn = g[:, :MLA_HEADS * MLA_NOPE].reshape(r, MLA_HEADS, MLA_NOPE)
    vv = g[:, MLA_HEADS * MLA_NOPE:].reshape(r, MLA_HEADS, MLA_V)
    return jnp.concatenate([kn, vv], axis=2).reshape(r, -1)


COL_SHARDED = ("w_in", "gla_gk_w", "mla_wuq", "mla_wukv", "ffn_w_in")


def kernel(x, c, positions, ada_w, ada_b, norm_mix_g, w_in, gla_gk_w, gla_gk_b, gla_onorm_g, gla_wo, mla_q_norm_g, mla_wuq, mla_kv_norm_g, mla_wukv, mla_wo, w_out, norm_ffn_g, ffn_w_in, ffn_w_down, final_norm_g, loss_target, m_ada_w, m_ada_b, m_norm_mix_g, m_w_in, m_gla_gk_w, m_gla_gk_b, m_gla_onorm_g, m_gla_wo, m_mla_q_norm_g, m_mla_wuq, m_mla_kv_norm_g, m_mla_wukv, m_mla_wo, m_w_out, m_norm_ffn_g, m_ffn_w_in, m_ffn_w_down, m_final_norm_g, v_ada_w, v_ada_b, v_norm_mix_g, v_w_in, v_gla_gk_w, v_gla_gk_b, v_gla_onorm_g, v_gla_wo, v_mla_q_norm_g, v_mla_wuq, v_mla_kv_norm_g, v_mla_wukv, v_mla_wo, v_w_out, v_norm_ffn_g, v_ffn_w_in, v_ffn_w_down, v_final_norm_g):
    wts = dict(ada_w=ada_w, ada_b=ada_b, norm_mix_g=norm_mix_g, w_in=w_in, gla_gk_w=gla_gk_w, gla_gk_b=gla_gk_b,
               gla_onorm_g=gla_onorm_g, gla_wo=gla_wo, mla_q_norm_g=mla_q_norm_g, mla_wuq=mla_wuq,
               mla_kv_norm_g=mla_kv_norm_g, mla_wukv=mla_wukv, mla_wo=mla_wo, w_out=w_out, norm_ffn_g=norm_ffn_g,
               ffn_w_in=ffn_w_in, ffn_w_down=ffn_w_down, final_norm_g=final_norm_g)
    mom_m = dict(zip(WEIGHTS, (m_ada_w, m_ada_b, m_norm_mix_g, m_w_in, m_gla_gk_w, m_gla_gk_b, m_gla_onorm_g,
                               m_gla_wo, m_mla_q_norm_g, m_mla_wuq, m_mla_kv_norm_g, m_mla_wukv, m_mla_wo, m_w_out,
                               m_norm_ffn_g, m_ffn_w_in, m_ffn_w_down, m_final_norm_g)))
    mom_v = dict(zip(WEIGHTS, (v_ada_w, v_ada_b, v_norm_mix_g, v_w_in, v_gla_gk_w, v_gla_gk_b, v_gla_onorm_g,
                               v_gla_wo, v_mla_q_norm_g, v_mla_wuq, v_mla_kv_norm_g, v_mla_wukv, v_mla_wo, v_w_out,
                               v_norm_ffn_g, v_ffn_w_in, v_ffn_w_down, v_final_norm_g)))
    seq, d = x.shape[1], x.shape[2]
    me = 4 * lax.axis_index("x") + 2 * lax.axis_index("y") + lax.axis_index("c")

    def two_d(a):
        return a.reshape(a.shape[-2], a.shape[-1]) if a.ndim >= 2 else a.reshape(1, -1)

    shard = {n: two_d(wts[n]) for n in BIG}
    send = [shard[n].astype(F32 if n == "gla_gk_w" else BF16) for n in BIG] + [two_d(c)]
    got = _exchange(send, scatter=False, name="comm_all_gather_weights")
    gathered = dict(zip(BIG, got[:-1]))
    c_all = got[-1].reshape(N_DEV, d)
    fullw = {}
    for n in BIG:
        g = gathered[n]
        fullw[n] = _unshard_cols(g) if n in COL_SHARDED else g.reshape(-1, g.shape[-1])
    w = dict(
        w_in=_w_in_to_mine(fullw["w_in"], d),
        gla_gk_w=jnp.pad(fullw["gla_gk_w"], ((0, LANE - GLA_GATE_RANK), (0, 0))),
        gla_wo=fullw["gla_wo"], mla_wuq=_wuq_to_mine(fullw["mla_wuq"]), mla_wukv=_wukv_to_mine(fullw["mla_wukv"]),
        mla_wo=fullw["mla_wo"], w_out=fullw["w_out"], ffn_w_in=fullw["ffn_w_in"], ffn_w_down=fullw["ffn_w_down"],
        gla_gk_b=two_d(gla_gk_b), gla_onorm_g=two_d(gla_onorm_g), mla_q_norm_g=two_d(mla_q_norm_g),
        mla_kv_norm_g=two_d(mla_kv_norm_g), norm_mix_g=two_d(norm_mix_g), norm_ffn_g=two_d(norm_ffn_g),
        final_norm_g=two_d(final_norm_g))

    c_pad = jnp.pad(c_all, ((0, 16 - N_DEV), (0, 0)))
    (c_act,) = _rowwise(lambda cv: cv * _sigmoid(cv), [(c_pad, d, 0)], [], [(d, F32)], tile=16, name="silu_c")
    ada_w2 = two_d(ada_w)
    mod_part = _mm(c_act, ada_w2, name="mm_ada")[:N_DEV]
    (mod_all,) = _exchange([mod_part], scatter=False, name="comm_all_gather_mod")
    mod_mine = lax.dynamic_index_in_dim(mod_all, me, axis=1, keepdims=False).reshape(1, -1) + two_d(ada_b)
    mod8 = jnp.pad(mod_mine.reshape(6, d), ((0, 2), (0, 0)))

    inv_freq = ROPE_THETA ** (-jnp.arange(0, MLA_ROPE, 2, dtype=F32) / MLA_ROPE)
    ang = positions.reshape(seq, 1).astype(F32) * inv_freq[None, :]
    cos, sin, z32 = jnp.cos(ang), jnp.sin(ang), jnp.zeros((seq, 32), F32)
    tab = jnp.concatenate([cos, cos, z32, z32, -sin, z32, z32, z32, z32, sin, z32, z32], axis=1)
    loss_local, grad_x, big, small = _local_step(x.reshape(seq, d), loss_target.reshape(seq, d), tab, mod8, w)

    ref_layout = dict(big)
    ref_layout["w_in"] = _w_in_from_mine(big["w_in"], d)
    ref_layout["mla_wuq"] = _wuq_from_mine(big["mla_wuq"])
    ref_layout["mla_wukv"] = _wukv_from_mine(big["mla_wukv"])
    slabs = []
    for n in BIG:
        g = ref_layout[n]
        s = _shard_cols(g) if n in COL_SHARDED else g.reshape(N_DEV, -1, g.shape[-1])
        slabs.append(s.astype(BF16))
    pack = jnp.concatenate([small[n] for n in SMALL], axis=1)
    recv = dict(zip(BIG, _exchange(slabs, scatter=True, name="comm_reduce_scatter_grads")))
    (pack_all,) = _exchange([pack], scatter=False, name="comm_all_gather_small")
    pack_all = pack_all.reshape(N_DEV, -1)

    res = {}
    for n in BIG:
        res[n] = _adamw(shard[n], recv[n], two_d(mom_m[n]), two_d(mom_v[n]), name="adamw_" + n)
    n_ada = ada_w2.shape[1]
    dmod_cols = lax.dynamic_slice_in_dim(pack_all[:, :6 * d], me * n_ada, n_ada, axis=1)

    def f_outer(cat, dm):
        acc = cat[:, 0:1] * dm[0:1]
        for b in range(1, N_DEV):
            acc = acc + cat[:, b:b + 1] * dm[b:b + 1]
        return acc

    (g_ada_w,) = _rowwise(f_outer, [(jnp.transpose(c_act[:N_DEV]), N_DEV, 0)], [dmod_cols], [(n_ada, F32)],
                          tile=256, name="ada_w_grad")
    res["ada_w"] = _adamw(ada_w2, g_ada_w, two_d(m_ada_w), two_d(v_ada_w), name="adamw_ada_w")
    w_small = jnp.concatenate([two_d(wts[n]) for n in SMALL], axis=1)
    m_small = jnp.concatenate([two_d(mom_m[n]) for n in SMALL], axis=1)
    v_small = jnp.concatenate([two_d(mom_v[n]) for n in SMALL], axis=1)
    small_res = _adamw(w_small, pack_all.reshape(N_DEV, 1, -1), m_small, v_small, name="adamw_small")
    off = 0
    for n in SMALL:
        width = wts[n].size
        res[n] = tuple(lax.slice_in_dim(a, off, off + width, axis=1) for a in small_res)
        off += width

    loss = lax.psum(loss_local, ("x", "y", "c"))
    outs = [loss, grad_x.reshape(x.shape)]
    for kind in range(4):
        outs += [res[n][kind].reshape(wts[n].shape) for n in WEIGHTS]
    return tuple(outs)
```

```python
import functools

import jax
import jax.numpy as jnp
from jax import lax
from jax.experimental import pallas as pl
from jax.experimental.pallas import tpu as pltpu

F32 = jnp.float32
BF16 = jnp.bfloat16

N_DEV = 8
GLA_HEADS = 4
GLA_DK = 256
GLA_DV = 512
GLA_GATE_RANK = 16
GLA_GATE_NORMALIZER = 16.0
GLA_CHUNK = 64
MLA_HEADS = 16
MLA_NOPE = 128
MLA_ROPE = 64
MLA_V = 128
MLA_QK_PAD = 256
ROPE_THETA = 10000.0
NORM_EPS = 1e-6
ATT_SCALE = (MLA_NOPE + MLA_ROPE) ** -0.5
GLA_QSCALE = GLA_DK ** -0.5

ADAM_LR = 0.001
ADAM_B1 = 0.9
ADAM_B2 = 0.999
ADAM_EPS = 1e-08
ADAM_WD = 0.01
ADAM_STEP = 10

LANE = 128
VMEM_LIMIT = 48 * 1024 * 1024
MM_TILE_BYTES = 4 * 1024 * 1024
LOG2E = 1.4426950408889634
LN2 = 0.6931471805599453
NEG = -1e30

IN_NAMES = ("q", "k", "v", "g", "gk", "cq", "ckv", "kr", "ga", "gb")
MY_ORDER = ("v", "g", "ga", "gb", "q", "k", "cq", "ckv", "gk", "kr")

WEIGHTS = ("ada_w", "ada_b", "norm_mix_g", "w_in", "gla_gk_w", "gla_gk_b", "gla_onorm_g", "gla_wo",
           "mla_q_norm_g", "mla_wuq", "mla_kv_norm_g", "mla_wukv", "mla_wo", "w_out", "norm_ffn_g",
           "ffn_w_in", "ffn_w_down", "final_norm_g")
BIG = ("w_in", "gla_gk_w", "gla_wo", "mla_wuq", "mla_wukv", "mla_wo", "w_out", "ffn_w_in", "ffn_w_down")
SMALL = ("ada_b", "norm_mix_g", "gla_gk_b", "gla_onorm_g", "mla_q_norm_g", "mla_kv_norm_g", "norm_ffn_g",
         "final_norm_g")


def _in_layout(d):
    w = dict(q=d // 2, k=d // 2, v=d, g=d, gk=GLA_GATE_RANK, cq=d // 4, ckv=512, kr=MLA_ROPE, ga=d, gb=d)
    pw = {n: -(-w[n] // LANE) * LANE for n in w}
    ref_off, o = {}, 0
    for n in IN_NAMES:
        ref_off[n] = o
        o += w[n]
    my_off, o = {}, 0
    for n in MY_ORDER:
        assert o % pw[n] == 0
        my_off[n] = o
        o += pw[n]
    return w, pw, ref_off, my_off, o


def _cparams(sem=None):
    return pltpu.CompilerParams(dimension_semantics=sem, vmem_limit_bytes=VMEM_LIMIT)


def _dot(a, b, ca=1, cb=0):
    return lax.dot_general(a, b, (((ca,), (cb,)), ((), ())), preferred_element_type=F32)


def _tile(n, cap):
    if n <= cap:
        return n
    t = (cap // LANE) * LANE
    while t >= LANE:
        if n % t == 0:
            return t
        t -= LANE
    return n


def _mm(a, b, *, ta=False, tb=False, out_dtype=F32, name, ride=None):
    m, k = (a.shape[1], a.shape[0]) if ta else a.shape
    n = b.shape[0] if tb else b.shape[1]
    assert k == (b.shape[1] if tb else b.shape[0])
    wide = max(a.dtype.itemsize, b.dtype.itemsize) > 2
    tm, tn, tk = _tile(m, 1024), _tile(n, 1024), _tile(k, MM_TILE_BYTES // (1024 * (4 if wide else 2)))
    nk = k // tk

    def product(a_ref, b_ref):
        return _dot(a_ref[...].astype(BF16), b_ref[...].astype(BF16), 0 if ta else 1, 1 if tb else 0)

    def body_one(a_ref, b_ref, o_ref):
        o_ref[...] = product(a_ref, b_ref).astype(o_ref.dtype)

    def body_acc(a_ref, b_ref, o_ref, acc_ref):
        kk = pl.program_id(2)

        @pl.when(kk == 0)
        def _():
            acc_ref[...] = jnp.zeros_like(acc_ref)

        acc_ref[...] += product(a_ref, b_ref)

        @pl.when(kk == nk - 1)
        def _():
            o_ref[...] = acc_ref[...].astype(o_ref.dtype)

    a_spec = (pl.BlockSpec((tk, tm), lambda i, j, kk: (kk, i)) if ta
              else pl.BlockSpec((tm, tk), lambda i, j, kk: (i, kk)))
    b_spec = (pl.BlockSpec((tn, tk), lambda i, j, kk: (j, kk)) if tb
              else pl.BlockSpec((tk, tn), lambda i, j, kk: (kk, j)))
    (out,), rode = _call(
        body_one if nk == 1 else body_acc, name=name, grid=(m // tm, n // tn, nk), in_specs=[a_spec, b_spec],
        out_specs=[pl.BlockSpec((tm, tn), lambda i, j, kk: (i, j))],
        out_shape=[jax.ShapeDtypeStruct((m, n), out_dtype)],
        scratch_shapes=[] if nk == 1 else [pltpu.VMEM((tm, tn), F32)],
        sem=("parallel", "parallel", "arbitrary"), args=(a, b), ride=ride)
    return out if ride is None else (out, rode)


def _rowwise(fn, rows, vecs, outs, sums=(), *, tile, name):
    t = rows[0][0].shape[0]
    tile = min(tile, t)
    assert t % tile == 0
    n_rows, n_vecs, n_outs = len(rows), len(vecs), len(outs)

    def body(*refs):
        ins = [r[...] for r in refs[:n_rows + n_vecs]]
        res = fn(*ins)
        if not isinstance(res, (tuple, list)):
            res = (res,)
        out_refs = refs[n_rows + n_vecs:]
        for r, val in zip(out_refs[:n_outs], res[:n_outs]):
            r[...] = val.astype(r.dtype)
        if sums:
            first = pl.program_id(0) == 0
            for r, val in zip(out_refs[n_outs:], res[n_outs:]):
                @pl.when(first)
                def _(r=r):
                    r[...] = jnp.zeros_like(r)
                r[...] += val

    in_specs = [pl.BlockSpec((tile, w), lambda i, cb=cb: (i, cb)) for (_, w, cb) in rows]
    in_specs += [pl.BlockSpec(v.shape, lambda i: (0, 0)) for v in vecs]
    out_specs = [pl.BlockSpec((tile, w), lambda i: (i, 0)) for (w, _) in outs]
    out_specs += [pl.BlockSpec((1, w), lambda i: (0, 0)) for w in sums]
    out_shape = [jax.ShapeDtypeStruct((t, w), dt) for (w, dt) in outs]
    out_shape += [jax.ShapeDtypeStruct((1, w), F32) for w in sums]
    res = pl.pallas_call(
        body, name=name, grid=(t // tile,), in_specs=in_specs, out_specs=out_specs, out_shape=out_shape,
        compiler_params=_cparams(("arbitrary",)),
    )(*[r[0] for r in rows], *vecs)
    return res


def _rstd(x):
    return lax.rsqrt(jnp.mean(x * x, axis=-1, keepdims=True) + NORM_EPS)


def _sigmoid(x):
    return 1.0 / (1.0 + jnp.exp(-x))


def _rms_bwd(dxh, xh, r):
    return r * (dxh - xh * jnp.mean(dxh * xh, axis=-1, keepdims=True))


def _rope(t, tab, sign):
    cosf, sin_a, sin_b = tab[:, :LANE], tab[:, LANE:2 * LANE], tab[:, 2 * LANE:]
    return t * cosf + sign * (pltpu.roll(t, 96, 1) * sin_a + pltpu.roll(t, 32, 1) * sin_b)


def _split3(x):
    hi = x.astype(BF16)
    r1 = x - hi.astype(F32)
    mid = r1.astype(BF16)
    lo = (r1 - mid.astype(F32)).astype(BF16)
    return hi, mid, lo


def _tri_sum(tri_bf16, x):
    hi, mid, lo = _split3(x)
    return _dot(tri_bf16, hi) + _dot(tri_bf16, mid) + _dot(tri_bf16, lo)


def _dot_nt2(a, b):
    a_hi = a.astype(BF16)
    a_lo = (a - a_hi.astype(F32)).astype(BF16)
    b_hi = b.astype(BF16)
    b_lo = (b - b_hi.astype(F32)).astype(BF16)
    return _dot(a_hi, b_hi, 1, 1) + _dot(a_hi, b_lo, 1, 1) + _dot(a_lo, b_hi, 1, 1)


def _gla_specs(t, rows, lay, reverse):
    nb = t // rows
    blk = (lambda i: nb - 1 - i) if reverse else (lambda i: i)
    qb, kb = lay["q"] // GLA_DK, lay["k"] // GLA_DK
    vb = lay["v"] // GLA_DV
    return [
        pl.BlockSpec((rows, GLA_DK), lambda h, i: (blk(i), qb + h)),
        pl.BlockSpec((rows, GLA_DK), lambda h, i: (blk(i), kb + h)),
        pl.BlockSpec((rows, GLA_DV), lambda h, i: (blk(i), vb + h)),
        pl.BlockSpec((rows, GLA_DK), lambda h, i: (blk(i), h)),
    ], blk


def _gla_fwd(p, la, lay):
    t = p.shape[0]
    rows = min(512, t)
    nb, nc = t // rows, rows // GLA_CHUNK
    c64 = GLA_CHUNK

    def body(q_ref, k_ref, v_ref, la_ref, o_ref, st_ref, s_ref):
        @pl.when(pl.program_id(1) == 0)
        def _():
            s_ref[...] = jnp.zeros_like(s_ref)

        r = lax.broadcasted_iota(jnp.int32, (c64, c64), 0)
        cc = lax.broadcasted_iota(jnp.int32, (c64, c64), 1)
        tril = cc <= r
        tril_b = tril.astype(BF16)
        for c in range(nc):
            sl = pl.ds(c * c64, c64)
            b = _tri_sum(tril_b, la_ref[sl, :])
            b_last = b[c64 - 1:c64, :]
            q = q_ref[sl, :] * GLA_QSCALE
            k = k_ref[sl, :]
            v = v_ref[sl, :].astype(BF16)
            qt_f = q * jnp.exp(b)
            qt = qt_f.astype(BF16)
            kh = (k * jnp.exp(b_last - b)).astype(BF16)
            s_prev = s_ref[...]
            st_ref[0, c] = s_prev
            att = jnp.where(tril, _dot_nt2(qt_f, k * jnp.exp(-b)), 0.0)
            o_ref[sl, :] = _dot(qt, s_prev.astype(BF16), 1, 1) + _dot(att.astype(BF16), v)
            s_ref[...] = s_prev * jnp.exp(b_last) + _dot(v, kh, 0, 0)

    in_specs, _ = _gla_specs(t, rows, lay, False)
    return pl.pallas_call(
        body, name="gla_fwd", grid=(GLA_HEADS, nb), in_specs=in_specs,
        out_specs=[pl.BlockSpec((rows, GLA_DV), lambda h, i: (i, h)),
                   pl.BlockSpec((1, nc, GLA_DV, GLA_DK), lambda h, i: (h, i, 0, 0))],
        out_shape=[jax.ShapeDtypeStruct((t, GLA_HEADS * GLA_DV), F32),
                   jax.ShapeDtypeStruct((GLA_HEADS, t // c64, GLA_DV, GLA_DK), F32)],
        scratch_shapes=[pltpu.VMEM((GLA_DV, GLA_DK), F32)],
        compiler_params=_cparams(("parallel", "arbitrary")),
    )(p, p, p, la)


def _gla_bwd(p, la, do, states, lay):
    t = p.shape[0]
    rows = min(512, t)
    nb, nc = t // rows, rows // GLA_CHUNK
    c64 = GLA_CHUNK

    def body(q_ref, k_ref, v_ref, la_ref, do_ref, st_ref, dq_ref, dk_ref, dv_ref, dla_ref, ds_ref):
        @pl.when(pl.program_id(1) == 0)
        def _():
            ds_ref[...] = jnp.zeros_like(ds_ref)

        r = lax.broadcasted_iota(jnp.int32, (c64, c64), 0)
        cc = lax.broadcasted_iota(jnp.int32, (c64, c64), 1)
        tril = cc <= r
        tril_b = tril.astype(BF16)
        triu_b = (cc >= r).astype(BF16)
        for c in reversed(range(nc)):
            sl = pl.ds(c * c64, c64)
            b = _tri_sum(tril_b, la_ref[sl, :])
            b_last = b[c64 - 1:c64, :]
            eb, enb, ebl_b, ebl = jnp.exp(b), jnp.exp(-b), jnp.exp(b_last - b), jnp.exp(b_last)
            k = k_ref[sl, :]
            qt_f = q_ref[sl, :] * GLA_QSCALE * eb
            kt_f = k * enb
            kh_f = k * ebl_b
            qt, kt, kh = qt_f.astype(BF16), kt_f.astype(BF16), kh_f.astype(BF16)
            v_f = v_ref[sl, :]
            dout_f = do_ref[sl, :]
            v, dout = v_f.astype(BF16), dout_f.astype(BF16)
            s_prev = st_ref[0, c]
            ds_next = ds_ref[...]
            ds_next_b = ds_next.astype(BF16)
            att = jnp.where(tril, _dot_nt2(qt_f, kt_f), 0.0).astype(BF16)
            datt = jnp.where(tril, _dot_nt2(dout_f, v_f), 0.0).astype(BF16)
            dqt = _dot(dout, s_prev.astype(BF16)) + _dot(datt, kt)
            dkt = _dot(datt, qt, 0, 0)
            dv = _dot(att, dout, 0, 0) + _dot(kh, ds_next_b, 1, 1)
            dkh = _dot(v, ds_next_b)
            d_ebl = jnp.sum(ds_next * s_prev, axis=0, keepdims=True)
            ds_ref[...] = ds_next * ebl + _dot(dout, qt, 0, 0)
            db = dqt * qt_f - dkt * kt_f - dkh * kh_f
            db_last = ebl * d_ebl + jnp.sum(dkh * kh_f, axis=0, keepdims=True)
            dq_ref[sl, :] = (dqt * eb * GLA_QSCALE).astype(dq_ref.dtype)
            dk_ref[sl, :] = (dkt * enb + dkh * ebl_b).astype(dk_ref.dtype)
            dv_ref[sl, :] = dv.astype(dv_ref.dtype)
            dla_ref[sl, :] = _tri_sum(triu_b, db) + db_last

    in_specs, blk = _gla_specs(t, rows, lay, True)
    in_specs += [pl.BlockSpec((rows, GLA_DV), lambda h, i: (blk(i), h)),
                 pl.BlockSpec((1, nc, GLA_DV, GLA_DK), lambda h, i: (h, blk(i), 0, 0))]
    dk_spec = pl.BlockSpec((rows, GLA_DK), lambda h, i: (blk(i), h))
    return pl.pallas_call(
        body, name="gla_bwd", grid=(GLA_HEADS, nb), in_specs=in_specs,
        out_specs=[dk_spec, dk_spec, pl.BlockSpec((rows, GLA_DV), lambda h, i: (blk(i), h)), dk_spec],
        out_shape=[jax.ShapeDtypeStruct((t, GLA_HEADS * GLA_DK), BF16),
                   jax.ShapeDtypeStruct((t, GLA_HEADS * GLA_DK), BF16),
                   jax.ShapeDtypeStruct((t, GLA_HEADS * GLA_DV), BF16),
                   jax.ShapeDtypeStruct((t, GLA_HEADS * GLA_DK), F32)],
        scratch_shapes=[pltpu.VMEM((GLA_DV, GLA_DK), F32)],
        compiler_params=_cparams(("parallel", "arbitrary")),
    )(p, p, p, la, do, states)


def _diag_mask(rows, cols, row0):
    row = row0 + lax.broadcasted_iota(jnp.int32, (rows, cols), 0)
    col = lax.broadcasted_iota(jnp.int32, (rows, cols), 1)
    return col <= row


def _flash_tiles(t):
    tq = min(1024, t)
    halves = 2 if tq % 32 == 0 else 1
    return tq, t // tq, halves, tq // halves


def _flash_fwd(q, k, vx, ride=None):
    t = q.shape[0]
    tq, nq, halves, hr = _flash_tiles(t)
    dqk, dv = MLA_QK_PAD, MLA_V

    def body(q_ref, k_ref, v_ref, o_ref, lse_ref, m_ref, acc_ref):
        i = pl.program_id(1)
        m_ref[...] = jnp.full_like(m_ref, NEG)
        acc_ref[...] = jnp.zeros_like(acc_ref)

        def step(j, masked):
            off = pl.multiple_of(j * tq, tq)
            kb = k_ref[pl.ds(off, tq), :]
            vb = v_ref[pl.ds(off, tq), :]
            for hh in range(halves):
                rs = pl.ds(hh * hr, hr)
                s = _dot(q_ref[rs, :], kb, 1, 1)
                if masked:
                    s = jnp.where(_diag_mask(hr, tq, hh * hr), s, NEG)
                m_old = m_ref[rs, :]
                m_new = jnp.maximum(m_old, jnp.max(s, axis=1, keepdims=True))
                pr = jnp.exp2(s - m_new)
                acc_ref[rs, :] = jnp.exp2(m_old - m_new) * acc_ref[rs, :] + _dot(pr.astype(BF16), vb)
                m_ref[rs, :] = m_new

        def loop_body(j, carry):
            step(j, False)
            return carry

        lax.fori_loop(0, i, loop_body, 0)
        step(i, True)
        acc = acc_ref[...]
        l = acc[:, dv:dv + 1]
        o_ref[...] = acc[:, :dv] / l
        lse_ref[0] = m_ref[...] + jnp.log(l) * LOG2E

    outs, rode = _call(
        body, name="mla_flash_fwd", grid=(MLA_HEADS, nq),
        in_specs=[pl.BlockSpec((tq, dqk), lambda h, i: (i, h)),
                  pl.BlockSpec((t, dqk), lambda h, i: (0, h)),
                  pl.BlockSpec((t, 2 * dv), lambda h, i: (0, h))],
        out_specs=[pl.BlockSpec((tq, dv), lambda h, i: (i, h)),
                   pl.BlockSpec((1, tq, 1), lambda h, i: (h, i, 0))],
        out_shape=[jax.ShapeDtypeStruct((t, MLA_HEADS * dv), F32),
                   jax.ShapeDtypeStruct((MLA_HEADS, t, 1), F32)],
        scratch_shapes=[pltpu.VMEM((tq, 1), F32), pltpu.VMEM((tq, 2 * dv), F32)],
        sem=("parallel", "arbitrary"), args=(q, k, vx), ride=ride)
    return tuple(outs) if ride is None else (tuple(outs), rode)


def _flash_bwd_dq(q, k, v, do, o, lse, ride=None):
    t = q.shape[0]
    tq, nq, halves, hr = _flash_tiles(t)
    dqk, dv = MLA_QK_PAD, MLA_V

    def body(q_ref, k_ref, v_ref, do_ref, o_ref, lse_ref, dq_ref, dl_ref, acc_ref):
        i = pl.program_id(1)
        dl_ref[0] = jnp.sum(do_ref[...].astype(F32) * o_ref[...], axis=1, keepdims=True)
        acc_ref[...] = jnp.zeros_like(acc_ref)

        def step(j, masked):
            off = pl.multiple_of(j * tq, tq)
            kb = k_ref[pl.ds(off, tq), :]
            vb = v_ref[pl.ds(off, tq), :]
            for hh in range(halves):
                rs = pl.ds(hh * hr, hr)
                s = _dot(q_ref[rs, :], kb, 1, 1)
                if masked:
                    s = jnp.where(_diag_mask(hr, tq, hh * hr), s, NEG)
                pr = jnp.exp2(s - lse_ref[0, rs, :])
                dp = _dot(do_ref[rs, :], vb, 1, 1)
                ds = pr * (dp - dl_ref[0, rs, :])
                acc_ref[rs, :] += _dot(ds.astype(BF16), kb)

        def loop_body(j, carry):
            step(j, False)
            return carry

        lax.fori_loop(0, i, loop_body, 0)
        step(i, True)
        dq_ref[...] = acc_ref[...] * ATT_SCALE

    outs, rode = _call(
        body, name="mla_flash_bwd_dq", grid=(MLA_HEADS, nq),
        in_specs=[pl.BlockSpec((tq, dqk), lambda h, i: (i, h)),
                  pl.BlockSpec((t, dqk), lambda h, i: (0, h)),
                  pl.BlockSpec((t, dv), lambda h, i: (0, h)),
                  pl.BlockSpec((tq, dv), lambda h, i: (i, h)),
                  pl.BlockSpec((tq, dv), lambda h, i: (i, h)),
                  pl.BlockSpec((1, tq, 1), lambda h, i: (h, i, 0))],
        out_specs=[pl.BlockSpec((tq, dqk), lambda h, i: (i, h)),
                   pl.BlockSpec((1, tq, 1), lambda h, i: (h, i, 0))],
        out_shape=[jax.ShapeDtypeStruct((t, MLA_HEADS * dqk), F32),
                   jax.ShapeDtypeStruct((MLA_HEADS, t, 1), F32)],
        scratch_shapes=[pltpu.VMEM((tq, dqk), F32)],
        sem=("parallel", "arbitrary"), args=(q, k, v, do, o, lse), ride=ride)
    return tuple(outs) if ride is None else (tuple(outs), rode)


def _flash_bwd_dkv(q, k, v, do, lse, delta):
    t = q.shape[0]
    tq, nq, halves, hr = _flash_tiles(t)
    dqk, dv = MLA_QK_PAD, MLA_V

    def body(k_ref, v_ref, q_ref, do_ref, lse_ref, dl_ref, dk_ref, dv_ref, dk_acc, dv_acc):
        j = pl.program_id(1)
        kb = k_ref[...]
        vb = v_ref[...]
        dk_acc[...] = jnp.zeros_like(dk_acc)
        dv_acc[...] = jnp.zeros_like(dv_acc)

        def step(i, masked):
            for hh in range(halves):
                rs = pl.ds(pl.multiple_of(i * tq + hh * hr, hr), hr)
                qb = q_ref[rs, :]
                dout = do_ref[rs, :]
                s = _dot(qb, kb, 1, 1)
                if masked:
                    s = jnp.where(_diag_mask(hr, tq, hh * hr), s, NEG)
                pr = jnp.exp2(s - lse_ref[0, rs, :])
                dv_acc[...] += _dot(pr.astype(BF16), dout, 0, 0)
                dp = _dot(dout, vb, 1, 1)
                ds = pr * (dp - dl_ref[0, rs, :])
                dk_acc[...] += _dot(ds.astype(BF16), qb, 0, 0)

        def loop_body(i, carry):
            step(i, False)
            return carry

        step(j, True)
        lax.fori_loop(j + 1, nq, loop_body, 0)
        dk_ref[...] = dk_acc[...] * LN2
        dv_ref[...] = dv_acc[...]

    return pl.pallas_call(
        body, name="mla_flash_bwd_dkv", grid=(MLA_HEADS, nq),
        in_specs=[pl.BlockSpec((tq, dqk), lambda h, j: (j, h)),
                  pl.BlockSpec((tq, dv), lambda h, j: (j, h)),
                  pl.BlockSpec((t, dqk), lambda h, j: (0, h)),
                  pl.BlockSpec((t, dv), lambda h, j: (0, h)),
                  pl.BlockSpec((1, t, 1), lambda h, j: (h, 0, 0)),
                  pl.BlockSpec((1, t, 1), lambda h, j: (h, 0, 0))],
        out_specs=[pl.BlockSpec((tq, dqk), lambda h, j: (j, h)),
                   pl.BlockSpec((tq, dv), lambda h, j: (j, h))],
        out_shape=[jax.ShapeDtypeStruct((t, MLA_HEADS * dqk), F32),
                   jax.ShapeDtypeStruct((t, MLA_HEADS * dv), F32)],
        scratch_shapes=[pltpu.VMEM((tq, dqk), F32), pltpu.VMEM((tq, dv), F32)],
        compiler_params=_cparams(("parallel", "arbitrary")),
    )(k, v, q, do, lse, delta)


class _NoRides:
    def ride(self, stage, grads):
        return None

    def done(self, stage, rode, w):
        pass


def _local_step(x, target, tab, mod8, w, rides=None):
    t, d = x.shape
    rides = rides or _NoRides()
    big = {}

    def riding(stage, fn):
        r = rides.ride(stage, big)
        res = fn(r)
        if r is None:
            return res
        rides.done(stage, res[1], w)
        return res[0]
    _, pw, _, lay, _ = _in_layout(d)
    ffn = ((8 * d // 3 + 255) // 256) * 256

    def blk(arr, name):
        return (arr, pw[name], lay[name] // pw[name])

    def full(arr):
        return (arr, arr.shape[1], 0)

    g1, g2, g3 = w["norm_mix_g"], w["norm_ffn_g"], w["final_norm_g"]

    def f_ln1(xv, mod, g):
        return (xv * _rstd(xv) * g) * (1.0 + mod[1:2]) + mod[0:1]

    (h,) = _rowwise(f_ln1, [full(x)], [mod8, g1], [(d, BF16)], tile=256, name="ln1_modulate")
    p = riding("in_proj", lambda r: _mm(h, w["w_in"], name="mm_in_proj", ride=r))

    def f_gk(pgk, gkw, gkb):
        z = _dot(pgk.astype(BF16), gkw.astype(BF16)) + gkb
        return (jnp.minimum(z, 0.0) - jnp.log(1.0 + jnp.exp(-jnp.abs(z)))) / GLA_GATE_NORMALIZER

    (la,) = _rowwise(f_gk, [blk(p, "gk")], [w["gla_gk_w"], w["gla_gk_b"]], [(GLA_HEADS * GLA_DK, F32)],
                     tile=512, name="gla_gate")
    o_gla, states = _gla_fwd(p, la, lay)

    def f_gla_out(ov, pg, g):
        parts = []
        for hh in range(GLA_HEADS):
            oh = ov[:, hh * GLA_DV:(hh + 1) * GLA_DV]
            ph = pg[:, hh * GLA_DV:(hh + 1) * GLA_DV]
            parts.append(oh * _rstd(oh) * g * (ph * _sigmoid(ph)))
        return jnp.concatenate(parts, axis=1)

    (o_n,) = _rowwise(f_gla_out, [full(o_gla), blk(p, "g")], [w["gla_onorm_g"]], [(d, BF16)], tile=256,
                      name="gla_out_norm")
    y_gla = _mm(o_n, w["gla_wo"], name="mm_gla_wo")

    def f_mla_prep(cq, ckv, kr, tb, gq, gkv):
        return cq * _rstd(cq) * gq, ckv * _rstd(ckv) * gkv, _rope(kr, tb, 1.0)

    cqn, ckvn, krr = _rowwise(f_mla_prep, [blk(p, "cq"), blk(p, "ckv"), blk(p, "kr"), full(tab)],
                              [w["mla_q_norm_g"], w["mla_kv_norm_g"]],
                              [(pw["cq"], BF16), (pw["ckv"], BF16), (LANE, F32)], tile=512, name="mla_prep")
    qlat = _mm(cqn, w["mla_wuq"], name="mm_mla_wuq")
    kvl = _mm(ckvn, w["mla_wukv"], name="mm_mla_wukv")
    hv = MLA_HEADS * MLA_V

    def f_qkv(ql, kn, vv, kr, tb):
        qs, ks, vx = [], [], []
        one_col = (lax.broadcasted_iota(jnp.int32, (ql.shape[0], LANE), 1) == 0).astype(F32)
        for hh in range(MLA_HEADS):
            o0 = hh * MLA_QK_PAD
            qs += [ql[:, o0:o0 + LANE], _rope(ql[:, o0 + LANE:o0 + 2 * LANE], tb, 1.0)]
            ks += [kn[:, hh * LANE:(hh + 1) * LANE], kr]
            vx += [vv[:, hh * MLA_V:(hh + 1) * MLA_V], one_col]
        return (jnp.concatenate(qs, axis=1) * (ATT_SCALE * LOG2E), jnp.concatenate(ks, axis=1), vv,
                jnp.concatenate(vx, axis=1))

    qa, ka, va, vxa = _rowwise(f_qkv, [full(qlat), (kvl, hv, 0), (kvl, hv, 1), full(krr), full(tab)], [],
                               [(MLA_HEADS * MLA_QK_PAD, BF16), (MLA_HEADS * MLA_QK_PAD, BF16), (hv, BF16),
                                (2 * hv, BF16)], tile=256, name="mla_qkv_build")
    o_mla, lse = riding("flash_fwd", lambda r: _flash_fwd(qa, ka, vxa, ride=r))
    y_mla = _mm(o_mla, w["mla_wo"], name="mm_mla_wo")

    def f_merge(yg, ym, ga, gb):
        return _sigmoid(ga) * yg + _sigmoid(gb) * ym

    (merged,) = _rowwise(f_merge, [full(y_gla), full(y_mla), blk(p, "ga"), blk(p, "gb")], [], [(d, BF16)],
                         tile=256, name="merge")
    mix = _mm(merged, w["w_out"], name="mm_w_out")

    def f_res_ln2(xv, mx, mod, g):
        x2v = xv + mod[2:3] * mx
        return x2v, (x2v * _rstd(x2v) * g) * (1.0 + mod[4:5]) + mod[3:4]

    x2, h2 = _rowwise(f_res_ln2, [full(x), full(mix)], [mod8, g2], [(d, F32), (d, BF16)], tile=256,
                      name="res_ln2_modulate")
    gu = _mm(h2, w["ffn_w_in"], name="mm_ffn_in")

    def f_swiglu(gv, uv):
        return gv * _sigmoid(gv) * uv

    (act,) = _rowwise(f_swiglu, [(gu, ffn, 0), (gu, ffn, 1)], [], [(ffn, BF16)], tile=128, name="swiglu")
    f_out = _mm(act, w["ffn_w_down"], name="mm_ffn_down")

    def f_head(x2v, fv, tg, mod, g):
        x3 = x2v + mod[5:6] * fv
        r = _rstd(x3)
        xh = x3 * r
        e = xh * g - tg
        loss_rows = 0.5 * jnp.mean(e * e, axis=-1, keepdims=True)
        dy = e * (1.0 / d)
        dx3 = _rms_bwd(dy * g, xh, r)
        loss = jnp.broadcast_to(jnp.sum(loss_rows, axis=0, keepdims=True), (1, LANE))
        return (dx3, dx3 * mod[5:6], loss, jnp.sum(dy * xh, axis=0, keepdims=True),
                jnp.sum(dx3 * fv, axis=0, keepdims=True))

    dx3, df, loss_v, dg3, dgate_f = _rowwise(f_head, [full(x2), full(f_out), full(target)], [mod8, g3],
                                             [(d, F32), (d, BF16)], [LANE, d, d], tile=256, name="loss_head")
    da = _mm(df, w["ffn_w_down"], tb=True, name="mm_ffn_down_dx")
    big["ffn_w_down"] = _mm(act, df, ta=True, name="mm_ffn_down_dw")

    def f_swiglu_bwd(gv, uv, dav):
        sg = _sigmoid(gv)
        return jnp.concatenate([dav * uv * (sg * (1.0 + gv * (1.0 - sg))), dav * (gv * sg)], axis=1)

    (dgu,) = _rowwise(f_swiglu_bwd, [(gu, ffn, 0), (gu, ffn, 1), full(da)], [], [(2 * ffn, BF16)], tile=128,
                      name="swiglu_bwd")
    dh2 = _mm(dgu, w["ffn_w_in"], tb=True, name="mm_ffn_in_dx")
    big["ffn_w_in"] = _mm(h2, dgu, ta=True, name="mm_ffn_in_dw")

    def f_ln2_bwd(x2v, dh, dx3v, mx, mod, g):
        r = _rstd(x2v)
        xh = x2v * r
        dn = dh * (1.0 + mod[4:5])
        dx2 = dx3v + _rms_bwd(dn * g, xh, r)
        return (dx2, dx2 * mod[2:3],
                jnp.sum(dh * (xh * g), axis=0, keepdims=True), jnp.sum(dh, axis=0, keepdims=True),
                jnp.sum(dn * xh, axis=0, keepdims=True), jnp.sum(dx2 * mx, axis=0, keepdims=True))

    dx2, dmix, dscale_f, dshift_f, dg2, dgate_m = _rowwise(
        f_ln2_bwd, [full(x2), full(dh2), full(dx3), full(mix)], [mod8, g2], [(d, F32), (d, BF16)],
        [d, d, d, d], tile=256, name="ln2_bwd")
    dmerged = _mm(dmix, w["w_out"], tb=True, name="mm_w_out_dx")
    big["w_out"] = _mm(merged, dmix, ta=True, name="mm_w_out_dw")

    def f_merge_bwd(dm, yg, ym, ga, gb):
        sa, sb = _sigmoid(ga), _sigmoid(gb)
        return dm * sa, dm * sb, dm * yg * sa * (1.0 - sa), dm * ym * sb * (1.0 - sb)

    dy_gla, dy_mla, dp_ga, dp_gb = _rowwise(
        f_merge_bwd, [full(dmerged), full(y_gla), full(y_mla), blk(p, "ga"), blk(p, "gb")], [],
        [(d, BF16)] * 4, tile=256, name="merge_bwd")
    do_n = _mm(dy_gla, w["gla_wo"], tb=True, name="mm_gla_wo_dx")
    big["gla_wo"] = _mm(o_n, dy_gla, ta=True, name="mm_gla_wo_dw")
    do_m = _mm(dy_mla, w["mla_wo"], tb=True, out_dtype=BF16, name="mm_mla_wo_dx")
    big["mla_wo"] = _mm(o_mla, dy_mla, ta=True, name="mm_mla_wo_dw")

    def f_gla_out_bwd(don, ov, pg, g):
        dos, dpgs = [], []
        dg = jnp.zeros((1, GLA_DV), F32)
        for hh in range(GLA_HEADS):
            sl = slice(hh * GLA_DV, (hh + 1) * GLA_DV)
            oh, ph, dn = ov[:, sl], pg[:, sl], don[:, sl]
            r = _rstd(oh)
            xh = oh * r
            sg = _sigmoid(ph)
            dpre = dn * (ph * sg)
            dg = dg + jnp.sum(dpre * xh, axis=0, keepdims=True)
            dos.append(_rms_bwd(dpre * g, xh, r))
            dpgs.append(dn * (xh * g) * (sg * (1.0 + ph * (1.0 - sg))))
        return jnp.concatenate(dos, axis=1), jnp.concatenate(dpgs, axis=1), dg

    do_gla, dp_g, dg_on = _rowwise(f_gla_out_bwd, [full(do_n), full(o_gla), blk(p, "g")], [w["gla_onorm_g"]],
                                   [(d, F32), (d, BF16)], [GLA_DV], tile=256, name="gla_out_norm_bwd")
    dp_q, dp_k, dp_v, dla = _gla_bwd(p, la, do_gla, states, lay)

    def f_gk_bwd(dlav, pgk, gkw, gkb):
        z = _dot(pgk.astype(BF16), gkw.astype(BF16)) + gkb
        dz = dlav * (1.0 / GLA_GATE_NORMALIZER) * _sigmoid(-z)
        return dz, _dot(dz.astype(BF16), gkw.astype(BF16), 1, 1), jnp.sum(dz, axis=0, keepdims=True)

    dz, dp_gk, dgk_b = _rowwise(f_gk_bwd, [full(dla), blk(p, "gk")], [w["gla_gk_w"], w["gla_gk_b"]],
                                [(GLA_HEADS * GLA_DK, BF16), (LANE, BF16)], [GLA_HEADS * GLA_DK], tile=512,
                                name="gla_gate_bwd")
    p_gk = lax.slice_in_dim(p, lay["gk"], lay["gk"] + LANE, axis=1)
    big["gla_gk_w"] = _mm(p_gk, dz, ta=True, name="mm_gla_gk_dw")[:GLA_GATE_RANK]

    dqa, delta = riding("flash_bwd_dq", lambda r: _flash_bwd_dq(qa, ka, va, do_m, o_mla, lse, ride=r))
    dka, dva = _flash_bwd_dkv(qa, ka, va, do_m, lse, delta)

    def f_qkv_bwd(dq, dk, dvv, tb):
        dqs, dkn = [], []
        dkr = jnp.zeros((dq.shape[0], LANE), F32)
        for hh in range(MLA_HEADS):
            o0 = hh * MLA_QK_PAD
            dqs += [dq[:, o0:o0 + LANE], _rope(dq[:, o0 + LANE:o0 + 2 * LANE], tb, -1.0)]
            dkn.append(dk[:, o0:o0 + LANE])
            dkr = dkr + dk[:, o0 + LANE:o0 + 2 * LANE]
        return jnp.concatenate(dqs, axis=1), jnp.concatenate(dkn + [dvv], axis=1), dkr

    dqlat, dkvl, dkrr = _rowwise(f_qkv_bwd, [full(dqa), full(dka), full(dva), full(tab)], [],
                                 [(MLA_HEADS * MLA_QK_PAD, BF16), (2 * hv, BF16), (LANE, F32)], tile=256,
                                 name="mla_qkv_build_bwd")
    dcqn = _mm(dqlat, w["mla_wuq"], tb=True, name="mm_mla_wuq_dx")
    big["mla_wuq"] = _mm(cqn, dqlat, ta=True, name="mm_mla_wuq_dw")
    dckvn = _mm(dkvl, w["mla_wukv"], tb=True, name="mm_mla_wukv_dx")
    big["mla_wukv"] = _mm(ckvn, dkvl, ta=True, name="mm_mla_wukv_dw")

    def f_mla_prep_bwd(dq, dkv, dkr, cq, ckv, tb, gq, gkv):
        rq, rk = _rstd(cq), _rstd(ckv)
        xq, xk = cq * rq, ckv * rk
        return (_rms_bwd(dq * gq, xq, rq), _rms_bwd(dkv * gkv, xk, rk), _rope(dkr, tb, -1.0),
                jnp.sum(dq * xq, axis=0, keepdims=True), jnp.sum(dkv * xk, axis=0, keepdims=True))

    dp_cq, dp_ckv, dp_kr, dg_q, dg_kv = _rowwise(
        f_mla_prep_bwd, [full(dcqn), full(dckvn), full(dkrr), blk(p, "cq"), blk(p, "ckv"), full(tab)],
        [w["mla_q_norm_g"], w["mla_kv_norm_g"]], [(pw["cq"], BF16), (pw["ckv"], BF16), (LANE, BF16)],
        [pw["cq"], pw["ckv"]], tile=512, name="mla_prep_bwd")

    pieces = dict(v=dp_v, g=dp_g, ga=dp_ga, gb=dp_gb, q=dp_q, k=dp_k, cq=dp_cq, ckv=dp_ckv, gk=dp_gk, kr=dp_kr)
    dp = jnp.concatenate([pieces[n] for n in MY_ORDER], axis=1)
    big["w_in"] = riding("in_proj_dw", lambda r: _mm(h, dp, ta=True, name="mm_in_proj_dw", ride=r))
    dh = riding("in_proj_dx", lambda r: _mm(dp, w["w_in"], tb=True, name="mm_in_proj_dx", ride=r))

    def f_ln1_bwd(xv, dhv, dx2v, mod, g):
        r = _rstd(xv)
        xh = xv * r
        dn = dhv * (1.0 + mod[1:2])
        return (dx2v + _rms_bwd(dn * g, xh, r),
                jnp.sum(dhv * (xh * g), axis=0, keepdims=True), jnp.sum(dhv, axis=0, keepdims=True),
                jnp.sum(dn * xh, axis=0, keepdims=True))

    grad_x, dscale_m, dshift_m, dg1 = _rowwise(f_ln1_bwd, [full(x), full(dh), full(dx2)], [mod8, g1],
                                               [(d, F32)], [d, d, d], tile=256, name="ln1_bwd")

    dmod = jnp.concatenate([dshift_m, dscale_m, dgate_m, dshift_f, dscale_f, dgate_f], axis=1)
    small = dict(ada_b=dmod, norm_mix_g=dg1, gla_gk_b=dgk_b, gla_onorm_g=dg_on, mla_q_norm_g=dg_q,
                 mla_kv_norm_g=dg_kv, norm_ffn_g=dg2, final_norm_g=dg3)
    return loss_v[0, 0], grad_x, big, small


N_PEER = N_DEV - 1


def _exchange_copies(ins, outs, sems, scatter):
    send_sems, recv_sems, local_sems = sems
    x, y, c = lax.axis_index("x"), lax.axis_index("y"), lax.axis_index("c")
    me = 4 * x + 2 * y + c
    peers = []
    for rel in range(1, N_DEV):
        px = 1 - x if rel & 4 else x
        py = 1 - y if rel & 2 else y
        pc = 1 - c if rel & 1 else c
        peers.append(((px, py, pc), 4 * px + 2 * py + pc))

    def remote(a, k, src_slot, dst_slot):
        src = ins[a].at[src_slot] if scatter else ins[a]
        return pltpu.make_async_remote_copy(
            src_ref=src, dst_ref=outs[a].at[dst_slot], send_sem=send_sems.at[a * N_PEER + k],
            recv_sem=recv_sems.at[a * N_PEER + k], device_id=peers[k][0], device_id_type=pl.DeviceIdType.MESH)

    local, sends, recvs = [], [], []
    for a in range(len(ins)):
        src = ins[a].at[me] if scatter else ins[a]
        local.append(pltpu.make_async_copy(src, outs[a].at[me], local_sems.at[a]))
        for k in range(N_PEER):
            sends.append(remote(a, k, peers[k][1], me))
            recvs.append(remote(a, k, peers[k][1], peers[k][1]))
    return local, sends, recvs


def _exchange_start(ins, outs, sems, scatter):
    local, sends, _ = _exchange_copies(ins, outs, sems, scatter)
    for cp in local + sends:
        cp.start()


def _exchange_wait(ins, outs, sems, scatter):
    local, sends, recvs = _exchange_copies(ins, outs, sems, scatter)
    for cp in recvs:
        cp.wait_recv()
    for cp in sends:
        cp.wait_send()
    for cp in local:
        cp.wait()


def _exchange_shapes(arrs, scatter):
    n = len(arrs)
    out_shape = [jax.ShapeDtypeStruct(a.shape if scatter else (N_DEV,) + a.shape, a.dtype) for a in arrs]
    sems = [pltpu.SemaphoreType.DMA((n * N_PEER,)), pltpu.SemaphoreType.DMA((n * N_PEER,)),
            pltpu.SemaphoreType.DMA((n,))]
    return out_shape, sems


def _exchange(arrs, *, scatter, name):
    n = len(arrs)

    def body(*refs):
        ins, outs, sems = refs[:n], refs[n:2 * n], refs[2 * n:]
        _exchange_start(ins, outs, sems, scatter)
        _exchange_wait(ins, outs, sems, scatter)

    hbm = pl.BlockSpec(memory_space=pltpu.HBM)
    out_shape, sems = _exchange_shapes(arrs, scatter)
    return pl.pallas_call(body, name=name, in_specs=[hbm] * n, out_specs=[hbm] * n, out_shape=out_shape,
                          scratch_shapes=sems)(*arrs)


def _call(body, *, name, grid, in_specs, out_specs, out_shape, scratch_shapes, sem, args, ride=None):
    if ride is None:
        res = pl.pallas_call(body, name=name, grid=grid, in_specs=in_specs, out_specs=out_specs, out_shape=out_shape,
                             scratch_shapes=scratch_shapes, compiler_params=_cparams(sem))(*args)
        return res, None
    arrs, scatter = ride
    n, n_in, n_out, n_scr = len(arrs), len(in_specs), len(out_specs), len(scratch_shapes)
    x_shape, x_sems = _exchange_shapes(arrs, scatter)

    def hosted(*refs):
        c_in, x_in = refs[:n_in], refs[n_in:n_in + n]
        c_out, x_out = refs[n_in + n:n_in + n + n_out], refs[n_in + n + n_out:n_in + 2 * n + n_out]
        scr = refs[n_in + 2 * n + n_out:]
        c_scr, sems = scr[:n_scr], scr[n_scr:]
        first = functools.reduce(jnp.logical_and, [pl.program_id(a) == 0 for a in range(len(grid))])
        last = functools.reduce(jnp.logical_and, [pl.program_id(a) == grid[a] - 1 for a in range(len(grid))])

        @pl.when(first)
        def _():
            _exchange_start(x_in, x_out, sems, scatter)

        body(*c_in, *c_out, *c_scr)

        @pl.when(last)
        def _():
            _exchange_wait(x_in, x_out, sems, scatter)

    hbm = pl.BlockSpec(memory_space=pltpu.HBM)
    res = pl.pallas_call(
        hosted, name=name, grid=grid, in_specs=list(in_specs) + [hbm] * n, out_specs=list(out_specs) + [hbm] * n,
        out_shape=list(out_shape) + x_shape, scratch_shapes=list(scratch_shapes) + x_sems,
        compiler_params=_cparams(("arbitrary",) * len(grid)))(*args, *arrs)
    return res[:n_out], res[n_out:]


def _adamw_math(w, g, m, v):
    m_new = ADAM_B1 * m + (1.0 - ADAM_B1) * g
    v_new = ADAM_B2 * v + (1.0 - ADAM_B2) * (g * g)
    m_hat = m_new / (1.0 - ADAM_B1 ** ADAM_STEP)
    v_hat = v_new / (1.0 - ADAM_B2 ** ADAM_STEP)
    delta = -ADAM_LR * (m_hat / (jnp.sqrt(v_hat) + ADAM_EPS) + ADAM_WD * w)
    return delta, m_new, v_new


def _adamw(w, g, m, v, *, name):
    r, c = w.shape
    slots = g.ndim == 3
    tr = r
    for cand in (128, 64, 32, 16):
        if r % cand == 0 and r > cand:
            tr = cand
            break

    def body(w_ref, g_ref, m_ref, v_ref, go_ref, d_ref, mo_ref, vo_ref):
        if slots:
            gv = g_ref[0].astype(F32)
            for s in range(1, N_DEV):
                gv = gv + g_ref[s].astype(F32)
        else:
            gv = g_ref[...]
        delta, m_new, v_new = _adamw_math(w_ref[...], gv, m_ref[...], v_ref[...])
        go_ref[...] = gv
        d_ref[...] = delta
        mo_ref[...] = m_new
        vo_ref[...] = v_new

    spec = pl.BlockSpec((tr, c), lambda i: (i, 0))
    g_spec = pl.BlockSpec((N_DEV, tr, c), lambda i: (0, i, 0)) if slots else spec
    return pl.pallas_call(
        body, name=name, grid=(r // tr,), in_specs=[spec, g_spec, spec, spec], out_specs=[spec] * 4,
        out_shape=[jax.ShapeDtypeStruct((r, c), F32)] * 4, compiler_params=_cparams(("parallel",)),
    )(w, g, m, v)


def _unshard_cols(g):
    return jnp.transpose(g, (1, 0, 2)).reshape(g.shape[1], -1)

def _shard_cols(full):
    r = full.shape[0]
    return jnp.transpose(full.reshape(r, N_DEV, -1), (1, 0, 2))


def _w_in_to_mine(w_ref_layout, d):
    wd, pw, ref_off, _, _ = _in_layout(d)
    cols = []
    for n in MY_ORDER:
        piece = lax.slice_in_dim(w_ref_layout, ref_off[n], ref_off[n] + wd[n], axis=1)
        if pw[n] != wd[n]:
            piece = jnp.pad(piece, ((0, 0), (0, pw[n] - wd[n])))
        cols.append(piece)
    return jnp.concatenate(cols, axis=1)


def _w_in_from_mine(g_mine, d):
    wd, _, _, my_off, _ = _in_layout(d)
    return jnp.concatenate([lax.slice_in_dim(g_mine, my_off[n], my_off[n] + wd[n], axis=1) for n in IN_NAMES],
                           axis=1)


def _wuq_to_mine(wq):
    r = wq.shape[0]
    w3 = wq.reshape(r, MLA_HEADS, MLA_NOPE + MLA_ROPE)
    w3 = jnp.pad(w3, ((0, 0), (0, 0), (0, MLA_QK_PAD - MLA_NOPE - MLA_ROPE)))
    return w3.reshape(r, MLA_HEADS * MLA_QK_PAD)


def _wuq_from_mine(g):
    r = g.shape[0]
    return g.reshape(r, MLA_HEADS, MLA_QK_PAD)[:, :, :MLA_NOPE + MLA_ROPE].reshape(r, -1)


def _wukv_to_mine(wkv):
    r = wkv.shape[0]
    w3 = wkv.reshape(r, MLA_HEADS, MLA_NOPE + MLA_V)
    return jnp.concatenate([w3[:, :, :MLA_NOPE].reshape(r, -1), w3[:, :, MLA_NOPE:].reshape(r, -1)], axis=1)


def _wukv_from_mine(g):
    r = g.shape[0]
    kn = g[:, :MLA_HEADS * MLA_NOPE].reshape(r, MLA_HEADS, MLA_NOPE)
    vv = g[:, MLA_HEADS * MLA_NOPE:].reshape(r, MLA_HEADS, MLA_V)
    return jnp.concatenate([kn, vv], axis=2).reshape(r, -1)


COL_SHARDED = ("w_in", "gla_gk_w", "mla_wuq", "mla_wukv", "ffn_w_in")


def _gathered_to_mine(name, g, d):
    full = _unshard_cols(g) if name in COL_SHARDED else g.reshape(-1, g.shape[-1])
    if name == "w_in":
        return _w_in_to_mine(full, d)
    if name == "gla_gk_w":
        return jnp.pad(full, ((0, LANE - GLA_GATE_RANK), (0, 0)))
    if name == "mla_wuq":
        return _wuq_to_mine(full)
    if name == "mla_wukv":
        return _wukv_to_mine(full)
    return full


def _grad_to_slabs(name, g, d):
    if name == "w_in":
        g = _w_in_from_mine(g, d)
    elif name == "mla_wuq":
        g = _wuq_from_mine(g)
    elif name == "mla_wukv":
        g = _wukv_from_mine(g)
    s = _shard_cols(g) if name in COL_SHARDED else g.reshape(N_DEV, -1, g.shape[-1])
    return s.astype(BF16)


class _Rides:
    GATHER = {"in_proj": ("gla_wo", "mla_wuq", "mla_wukv", "mla_wo", "w_out"),
              "flash_fwd": ("ffn_w_in", "ffn_w_down")}
    SCATTER = {"flash_bwd_dq": ("ffn_w_in", "ffn_w_down", "w_out", "gla_wo", "mla_wo", "gla_gk_w"),
               "in_proj_dw": ("mla_wuq", "mla_wukv"),
               "in_proj_dx": ("w_in",)}

    def __init__(self, send, d):
        self.send, self.d, self.recv = send, d, {}

    def ride(self, stage, grads):
        if stage in self.GATHER:
            return [self.send[n] for n in self.GATHER[stage]], False
        return [_grad_to_slabs(n, grads[n], self.d) for n in self.SCATTER[stage]], True

    def done(self, stage, rode, w):
        if stage in self.GATHER:
            for n, g in zip(self.GATHER[stage], rode):
                w[n] = _gathered_to_mine(n, g, self.d)
        else:
            self.recv.update(zip(self.SCATTER[stage], rode))


def kernel(x, c, positions, ada_w, ada_b, norm_mix_g, w_in, gla_gk_w, gla_gk_b, gla_onorm_g, gla_wo, mla_q_norm_g, mla_wuq, mla_kv_norm_g, mla_wukv, mla_wo, w_out, norm_ffn_g, ffn_w_in, ffn_w_down, final_norm_g, loss_target, m_ada_w, m_ada_b, m_norm_mix_g, m_w_in, m_gla_gk_w, m_gla_gk_b, m_gla_onorm_g, m_gla_wo, m_mla_q_norm_g, m_mla_wuq, m_mla_kv_norm_g, m_mla_wukv, m_mla_wo, m_w_out, m_norm_ffn_g, m_ffn_w_in, m_ffn_w_down, m_final_norm_g, v_ada_w, v_ada_b, v_norm_mix_g, v_w_in, v_gla_gk_w, v_gla_gk_b, v_gla_onorm_g, v_gla_wo, v_mla_q_norm_g, v_mla_wuq, v_mla_kv_norm_g, v_mla_wukv, v_mla_wo, v_w_out, v_norm_ffn_g, v_ffn_w_in, v_ffn_w_down, v_final_norm_g):
    wts = dict(ada_w=ada_w, ada_b=ada_b, norm_mix_g=norm_mix_g, w_in=w_in, gla_gk_w=gla_gk_w, gla_gk_b=gla_gk_b,
               gla_onorm_g=gla_onorm_g, gla_wo=gla_wo, mla_q_norm_g=mla_q_norm_g, mla_wuq=mla_wuq,
               mla_kv_norm_g=mla_kv_norm_g, mla_wukv=mla_wukv, mla_wo=mla_wo, w_out=w_out, norm_ffn_g=norm_ffn_g,
               ffn_w_in=ffn_w_in, ffn_w_down=ffn_w_down, final_norm_g=final_norm_g)
    mom_m = dict(zip(WEIGHTS, (m_ada_w, m_ada_b, m_norm_mix_g, m_w_in, m_gla_gk_w, m_gla_gk_b, m_gla_onorm_g,
                               m_gla_wo, m_mla_q_norm_g, m_mla_wuq, m_mla_kv_norm_g, m_mla_wukv, m_mla_wo, m_w_out,
                               m_norm_ffn_g, m_ffn_w_in, m_ffn_w_down, m_final_norm_g)))
    mom_v = dict(zip(WEIGHTS, (v_ada_w, v_ada_b, v_norm_mix_g, v_w_in, v_gla_gk_w, v_gla_gk_b, v_gla_onorm_g,
                               v_gla_wo, v_mla_q_norm_g, v_mla_wuq, v_mla_kv_norm_g, v_mla_wukv, v_mla_wo, v_w_out,
                               v_norm_ffn_g, v_ffn_w_in, v_ffn_w_down, v_final_norm_g)))
    seq, d = x.shape[1], x.shape[2]
    me = 4 * lax.axis_index("x") + 2 * lax.axis_index("y") + lax.axis_index("c")

    def two_d(a):
        return a.reshape(a.shape[-2], a.shape[-1]) if a.ndim >= 2 else a.reshape(1, -1)

    shard = {n: two_d(wts[n]) for n in BIG}
    send = {n: shard[n].astype(F32 if n == "gla_gk_w" else BF16) for n in BIG}
    got = _exchange([send["w_in"], send["gla_gk_w"], two_d(c)], scatter=False, name="comm_all_gather_first")
    c_all = got[2].reshape(N_DEV, d)
    w = dict(
        w_in=_gathered_to_mine("w_in", got[0], d), gla_gk_w=_gathered_to_mine("gla_gk_w", got[1], d),
        gla_gk_b=two_d(gla_gk_b), gla_onorm_g=two_d(gla_onorm_g), mla_q_norm_g=two_d(mla_q_norm_g),
        mla_kv_norm_g=two_d(mla_kv_norm_g), norm_mix_g=two_d(norm_mix_g), norm_ffn_g=two_d(norm_ffn_g),
        final_norm_g=two_d(final_norm_g))
    rides = _Rides(send, d)

    c_pad = jnp.pad(c_all, ((0, 16 - N_DEV), (0, 0)))
    (c_act,) = _rowwise(lambda cv: cv * _sigmoid(cv), [(c_pad, d, 0)], [], [(d, F32)], tile=16, name="silu_c")
    ada_w2 = two_d(ada_w)
    mod_part = _mm(c_act, ada_w2, name="mm_ada")[:N_DEV]
    (mod_all,) = _exchange([mod_part], scatter=False, name="comm_all_gather_mod")
    mod_mine = lax.dynamic_index_in_dim(mod_all, me, axis=1, keepdims=False).reshape(1, -1) + two_d(ada_b)
    mod8 = jnp.pad(mod_mine.reshape(6, d), ((0, 2), (0, 0)))

    inv_freq = ROPE_THETA ** (-jnp.arange(0, MLA_ROPE, 2, dtype=F32) / MLA_ROPE)
    ang = positions.reshape(seq, 1).astype(F32) * inv_freq[None, :]
    cos, sin, z32 = jnp.cos(ang), jnp.sin(ang), jnp.zeros((seq, 32), F32)
    tab = jnp.concatenate([cos, cos, z32, z32, -sin, z32, z32, z32, z32, sin, z32, z32], axis=1)
    loss_local, grad_x, _, small = _local_step(x.reshape(seq, d), loss_target.reshape(seq, d), tab, mod8, w, rides)

    recv = rides.recv
    pack = jnp.concatenate([small[n] for n in SMALL], axis=1)
    (pack_all,) = _exchange([pack], scatter=False, name="comm_all_gather_small")
    pack_all = pack_all.reshape(N_DEV, -1)

    res = {}
    for n in BIG:
        res[n] = _adamw(shard[n], recv[n], two_d(mom_m[n]), two_d(mom_v[n]), name="adamw_" + n)
    n_ada = ada_w2.shape[1]
    dmod_cols = lax.dynamic_slice_in_dim(pack_all[:, :6 * d], me * n_ada, n_ada, axis=1)

    def f_outer(cat, dm):
        acc = cat[:, 0:1] * dm[0:1]
        for b in range(1, N_DEV):
            acc = acc + cat[:, b:b + 1] * dm[b:b + 1]
        return acc

    (g_ada_w,) = _rowwise(f_outer, [(jnp.transpose(c_act[:N_DEV]), N_DEV, 0)], [dmod_cols], [(n_ada, F32)],
                          tile=256, name="ada_w_grad")
    res["ada_w"] = _adamw(ada_w2, g_ada_w, two_d(m_ada_w), two_d(v_ada_w), name="adamw_ada_w")
    w_small = jnp.concatenate([two_d(wts[n]) for n in SMALL], axis=1)
    m_small = jnp.concatenate([two_d(mom_m[n]) for n in SMALL], axis=1)
    v_small = jnp.concatenate([two_d(mom_v[n]) for n in SMALL], axis=1)
    small_res = _adamw(w_small, pack_all.reshape(N_DEV, 1, -1), m_small, v_small, name="adamw_small")
    off = 0
    for n in SMALL:
        width = wts[n].size
        res[n] = tuple(lax.slice_in_dim(a, off, off + width, axis=1) for a in small_res)
        off += width

    loss = lax.psum(loss_local, ("x", "y", "c"))
    outs = [loss, grad_x.reshape(x.shape)]
    for kind in range(4):
        outs += [res[n][kind].reshape(wts[n].shape) for n in WEIGHTS]
    return tuple(outs)
```

```python
import functools

import jax
import jax.numpy as jnp
from jax import lax
from jax.experimental import pallas as pl
from jax.experimental.pallas import tpu as pltpu

F32 = jnp.float32
BF16 = jnp.bfloat16

N_DEV = 8
GLA_HEADS = 4
GLA_DK = 256
GLA_DV = 512
GLA_GATE_RANK = 16
GLA_GATE_NORMALIZER = 16.0
GLA_CHUNK = 64
MLA_HEADS = 16
MLA_NOPE = 128
MLA_ROPE = 64
MLA_V = 128
MLA_QK_PAD = 256
ROPE_THETA = 10000.0
NORM_EPS = 1e-6
ATT_SCALE = (MLA_NOPE + MLA_ROPE) ** -0.5
GLA_QSCALE = GLA_DK ** -0.5

ADAM_LR = 0.001
ADAM_B1 = 0.9
ADAM_B2 = 0.999
ADAM_EPS = 1e-08
ADAM_WD = 0.01
ADAM_STEP = 10

LANE = 128
VMEM_LIMIT = 48 * 1024 * 1024
MM_TILE_BYTES = 4 * 1024 * 1024
LOG2E = 1.4426950408889634
LN2 = 0.6931471805599453
NEG = -1e30

IN_NAMES = ("q", "k", "v", "g", "gk", "cq", "ckv", "kr", "ga", "gb")
MY_ORDER = ("v", "g", "ga", "gb", "q", "k", "cq", "ckv", "gk", "kr")

WEIGHTS = ("ada_w", "ada_b", "norm_mix_g", "w_in", "gla_gk_w", "gla_gk_b", "gla_onorm_g", "gla_wo",
           "mla_q_norm_g", "mla_wuq", "mla_kv_norm_g", "mla_wukv", "mla_wo", "w_out", "norm_ffn_g",
           "ffn_w_in", "ffn_w_down", "final_norm_g")
BIG = ("w_in", "gla_gk_w", "gla_wo", "mla_wuq", "mla_wukv", "mla_wo", "w_out", "ffn_w_in", "ffn_w_down")
SMALL = ("ada_b", "norm_mix_g", "gla_gk_b", "gla_onorm_g", "mla_q_norm_g", "mla_kv_norm_g", "norm_ffn_g",
         "final_norm_g")


def _in_layout(d):
    w = dict(q=d // 2, k=d // 2, v=d, g=d, gk=GLA_GATE_RANK, cq=d // 4, ckv=512, kr=MLA_ROPE, ga=d, gb=d)
    pw = {n: -(-w[n] // LANE) * LANE for n in w}
    ref_off, o = {}, 0
    for n in IN_NAMES:
        ref_off[n] = o
        o += w[n]
    my_off, o = {}, 0
    for n in MY_ORDER:
        assert o % pw[n] == 0
        my_off[n] = o
        o += pw[n]
    return w, pw, ref_off, my_off, o


def _cparams(sem=None):
    return pltpu.CompilerParams(dimension_semantics=sem, vmem_limit_bytes=VMEM_LIMIT)


def _dot(a, b, ca=1, cb=0):
    return lax.dot_general(a, b, (((ca,), (cb,)), ((), ())), preferred_element_type=F32)


def _tile(n, cap):
    if n <= cap:
        return n
    t = (cap // LANE) * LANE
    while t >= LANE:
        if n % t == 0:
            return t
        t -= LANE
    return n


def _mm(a, b, *, ta=False, tb=False, out_dtype=F32, name, ride=None, slabs=False):
    m, k = (a.shape[1], a.shape[0]) if ta else a.shape
    n = b.shape[0] if tb else b.shape[1]
    assert k == (b.shape[1] if tb else b.shape[0])
    wide = max(a.dtype.itemsize, b.dtype.itemsize) > 2
    tm, tn, tk = _tile(m, 1024), _tile(n, 1024), _tile(k, MM_TILE_BYTES // (1024 * (4 if wide else 2)))
    if slabs:
        tn = n // N_DEV
        assert tn % LANE == 0
    nk = k // tk

    def product(a_ref, b_ref):
        return _dot(a_ref[...].astype(BF16), b_ref[...].astype(BF16), 0 if ta else 1, 1 if tb else 0)

    def store(o_ref, val):
        if slabs:
            o_ref[0] = val.astype(o_ref.dtype)
        else:
            o_ref[...] = val.astype(o_ref.dtype)

    def body_one(a_ref, b_ref, o_ref):
        store(o_ref, product(a_ref, b_ref))

    def body_acc(a_ref, b_ref, o_ref, acc_ref):
        kk = pl.program_id(2)

        @pl.when(kk == 0)
        def _():
            acc_ref[...] = jnp.zeros_like(acc_ref)

        acc_ref[...] += product(a_ref, b_ref)

        @pl.when(kk == nk - 1)
        def _():
            store(o_ref, acc_ref[...])

    a_spec = (pl.BlockSpec((tk, tm), lambda i, j, kk: (kk, i)) if ta
              else pl.BlockSpec((tm, tk), lambda i, j, kk: (i, kk)))
    b_spec = (pl.BlockSpec((tn, tk), lambda i, j, kk: (j, kk)) if tb
              else pl.BlockSpec((tk, tn), lambda i, j, kk: (kk, j)))
    (out,), rode = _call(
        body_one if nk == 1 else body_acc, name=name, grid=(m // tm, n // tn, nk), in_specs=[a_spec, b_spec],
        out_specs=[pl.BlockSpec((1, tm, tn), lambda i, j, kk: (j, i, 0)) if slabs
                   else pl.BlockSpec((tm, tn), lambda i, j, kk: (i, j))],
        out_shape=[jax.ShapeDtypeStruct((N_DEV, m, tn) if slabs else (m, n), out_dtype)],
        scratch_shapes=[] if nk == 1 else [pltpu.VMEM((tm, tn), F32)],
        sem=("parallel", "parallel", "arbitrary"), args=(a, b), ride=ride)
    return out if ride is None else (out, rode)


def _rowwise(fn, rows, vecs, outs, sums=(), *, tile, name):
    t = rows[0][0].shape[0]
    tile = min(tile, t)
    assert t % tile == 0
    n_rows, n_vecs, n_outs = len(rows), len(vecs), len(outs)

    def body(*refs):
        ins = [r[...].astype(F32) for r in refs[:n_rows + n_vecs]]
        res = fn(*ins)
        if not isinstance(res, (tuple, list)):
            res = (res,)
        out_refs = refs[n_rows + n_vecs:]
        for r, val in zip(out_refs[:n_outs], res[:n_outs]):
            r[...] = val.astype(r.dtype)
        if sums:
            first = pl.program_id(0) == 0
            for r, val in zip(out_refs[n_outs:], res[n_outs:]):
                @pl.when(first)
                def _(r=r):
                    r[...] = jnp.zeros_like(r)
                r[...] += val

    in_specs = [pl.BlockSpec((tile, w), lambda i, cb=cb: (i, cb)) for (_, w, cb) in rows]
    in_specs += [pl.BlockSpec(v.shape, lambda i: (0, 0)) for v in vecs]
    out_specs = [pl.BlockSpec((tile, w), lambda i: (i, 0)) for (w, _) in outs]
    out_specs += [pl.BlockSpec((1, w), lambda i: (0, 0)) for w in sums]
    out_shape = [jax.ShapeDtypeStruct((t, w), dt) for (w, dt) in outs]
    out_shape += [jax.ShapeDtypeStruct((1, w), F32) for w in sums]
    res = pl.pallas_call(
        body, name=name, grid=(t // tile,), in_specs=in_specs, out_specs=out_specs, out_shape=out_shape,
        compiler_params=_cparams(("arbitrary",)),
    )(*[r[0] for r in rows], *vecs)
    return res


def _rstd(x):
    return lax.rsqrt(jnp.mean(x * x, axis=-1, keepdims=True) + NORM_EPS)


def _sigmoid(x):
    return 1.0 / (1.0 + jnp.exp(-x))


def _rms_bwd(dxh, xh, r):
    return r * (dxh - xh * jnp.mean(dxh * xh, axis=-1, keepdims=True))


def _rope(t, tab, sign):
    cosf, sin_a, sin_b = tab[:, :LANE], tab[:, LANE:2 * LANE], tab[:, 2 * LANE:]
    return t * cosf + sign * (pltpu.roll(t, 96, 1) * sin_a + pltpu.roll(t, 32, 1) * sin_b)


def _split3(x):
    hi = x.astype(BF16)
    r1 = x - hi.astype(F32)
    mid = r1.astype(BF16)
    lo = (r1 - mid.astype(F32)).astype(BF16)
    return hi, mid, lo


def _tri_sum(tri_bf16, x):
    hi, mid, lo = _split3(x)
    return _dot(tri_bf16, hi) + _dot(tri_bf16, mid) + _dot(tri_bf16, lo)


def _dot_nt2(a, b):
    a_hi = a.astype(BF16)
    a_lo = (a - a_hi.astype(F32)).astype(BF16)
    b_hi = b.astype(BF16)
    b_lo = (b - b_hi.astype(F32)).astype(BF16)
    return _dot(a_hi, b_hi, 1, 1) + _dot(a_hi, b_lo, 1, 1) + _dot(a_lo, b_hi, 1, 1)


def _gla_specs(t, rows, lay, reverse):
    nb = t // rows
    blk = (lambda i: nb - 1 - i) if reverse else (lambda i: i)
    qb, kb = lay["q"] // GLA_DK, lay["k"] // GLA_DK
    vb = lay["v"] // GLA_DV
    return [
        pl.BlockSpec((rows, GLA_DK), lambda h, i: (blk(i), qb + h)),
        pl.BlockSpec((rows, GLA_DK), lambda h, i: (blk(i), kb + h)),
        pl.BlockSpec((rows, GLA_DV), lambda h, i: (blk(i), vb + h)),
        pl.BlockSpec((rows, GLA_DK), lambda h, i: (blk(i), h)),
    ], blk


def _gla_fwd(p, la, lay):
    t = p.shape[0]
    rows = min(512, t)
    nb, nc = t // rows, rows // GLA_CHUNK
    c64 = GLA_CHUNK

    def body(q_ref, k_ref, v_ref, la_ref, o_ref, st_ref, s_ref):
        @pl.when(pl.program_id(1) == 0)
        def _():
            s_ref[...] = jnp.zeros_like(s_ref)

        r = lax.broadcasted_iota(jnp.int32, (c64, c64), 0)
        cc = lax.broadcasted_iota(jnp.int32, (c64, c64), 1)
        tril = cc <= r
        tril_b = tril.astype(BF16)
        for c in range(nc):
            sl = pl.ds(c * c64, c64)
            b = _tri_sum(tril_b, la_ref[sl, :])
            b_last = b[c64 - 1:c64, :]
            q = q_ref[sl, :] * GLA_QSCALE
            k = k_ref[sl, :]
            v = v_ref[sl, :].astype(BF16)
            qt_f = q * jnp.exp(b)
            qt = qt_f.astype(BF16)
            kh = (k * jnp.exp(b_last - b)).astype(BF16)
            s_prev = s_ref[...]
            st_ref[0, c] = s_prev
            att = jnp.where(tril, _dot_nt2(qt_f, k * jnp.exp(-b)), 0.0)
            o_ref[sl, :] = _dot(qt, s_prev.astype(BF16), 1, 1) + _dot(att.astype(BF16), v)
            s_ref[...] = s_prev * jnp.exp(b_last) + _dot(v, kh, 0, 0)

    in_specs, _ = _gla_specs(t, rows, lay, False)
    return pl.pallas_call(
        body, name="gla_fwd", grid=(GLA_HEADS, nb), in_specs=in_specs,
        out_specs=[pl.BlockSpec((rows, GLA_DV), lambda h, i: (i, h)),
                   pl.BlockSpec((1, nc, GLA_DV, GLA_DK), lambda h, i: (h, i, 0, 0))],
        out_shape=[jax.ShapeDtypeStruct((t, GLA_HEADS * GLA_DV), F32),
                   jax.ShapeDtypeStruct((GLA_HEADS, t // c64, GLA_DV, GLA_DK), F32)],
        scratch_shapes=[pltpu.VMEM((GLA_DV, GLA_DK), F32)],
        compiler_params=_cparams(("parallel", "arbitrary")),
    )(p, p, p, la)


def _gla_bwd(p, la, do, states, lay):
    t = p.shape[0]
    rows = min(512, t)
    nb, nc = t // rows, rows // GLA_CHUNK
    c64 = GLA_CHUNK

    def body(q_ref, k_ref, v_ref, la_ref, do_ref, st_ref, dq_ref, dk_ref, dv_ref, dla_ref, ds_ref):
        @pl.when(pl.program_id(1) == 0)
        def _():
            ds_ref[...] = jnp.zeros_like(ds_ref)

        r = lax.broadcasted_iota(jnp.int32, (c64, c64), 0)
        cc = lax.broadcasted_iota(jnp.int32, (c64, c64), 1)
        tril = cc <= r
        tril_b = tril.astype(BF16)
        triu_b = (cc >= r).astype(BF16)
        for c in reversed(range(nc)):
            sl = pl.ds(c * c64, c64)
            b = _tri_sum(tril_b, la_ref[sl, :])
            b_last = b[c64 - 1:c64, :]
            eb, enb, ebl_b, ebl = jnp.exp(b), jnp.exp(-b), jnp.exp(b_last - b), jnp.exp(b_last)
            k = k_ref[sl, :]
            qt_f = q_ref[sl, :] * GLA_QSCALE * eb
            kt_f = k * enb
            kh_f = k * ebl_b
            qt, kt, kh = qt_f.astype(BF16), kt_f.astype(BF16), kh_f.astype(BF16)
            v_f = v_ref[sl, :]
            dout_f = do_ref[sl, :]
            v, dout = v_f.astype(BF16), dout_f.astype(BF16)
            s_prev = st_ref[0, c]
            ds_next = ds_ref[...]
            ds_next_b = ds_next.astype(BF16)
            att = jnp.where(tril, _dot_nt2(qt_f, kt_f), 0.0).astype(BF16)
            datt = jnp.where(tril, _dot_nt2(dout_f, v_f), 0.0).astype(BF16)
            dqt = _dot(dout, s_prev.astype(BF16)) + _dot(datt, kt)
            dkt = _dot(datt, qt, 0, 0)
            dv = _dot(att, dout, 0, 0) + _dot(kh, ds_next_b, 1, 1)
            dkh = _dot(v, ds_next_b)
            d_ebl = jnp.sum(ds_next * s_prev, axis=0, keepdims=True)
            ds_ref[...] = ds_next * ebl + _dot(dout, qt, 0, 0)
            db = dqt * qt_f - dkt * kt_f - dkh * kh_f
            db_last = ebl * d_ebl + jnp.sum(dkh * kh_f, axis=0, keepdims=True)
            dq_ref[sl, :] = (dqt * eb * GLA_QSCALE).astype(dq_ref.dtype)
            dk_ref[sl, :] = (dkt * enb + dkh * ebl_b).astype(dk_ref.dtype)
            dv_ref[sl, :] = dv.astype(dv_ref.dtype)
            dla_ref[sl, :] = _tri_sum(triu_b, db) + db_last

    in_specs, blk = _gla_specs(t, rows, lay, True)
    in_specs += [pl.BlockSpec((rows, GLA_DV), lambda h, i: (blk(i), h)),
                 pl.BlockSpec((1, nc, GLA_DV, GLA_DK), lambda h, i: (h, blk(i), 0, 0))]
    dk_spec = pl.BlockSpec((rows, GLA_DK), lambda h, i: (blk(i), h))
    return pl.pallas_call(
        body, name="gla_bwd", grid=(GLA_HEADS, nb), in_specs=in_specs,
        out_specs=[dk_spec, dk_spec, pl.BlockSpec((rows, GLA_DV), lambda h, i: (blk(i), h)), dk_spec],
        out_shape=[jax.ShapeDtypeStruct((t, GLA_HEADS * GLA_DK), BF16),
                   jax.ShapeDtypeStruct((t, GLA_HEADS * GLA_DK), BF16),
                   jax.ShapeDtypeStruct((t, GLA_HEADS * GLA_DV), BF16),
                   jax.ShapeDtypeStruct((t, GLA_HEADS * GLA_DK), F32)],
        scratch_shapes=[pltpu.VMEM((GLA_DV, GLA_DK), F32)],
        compiler_params=_cparams(("parallel", "arbitrary")),
    )(p, p, p, la, do, states)


def _diag_mask(rows, cols, row0):
    row = row0 + lax.broadcasted_iota(jnp.int32, (rows, cols), 0)
    col = lax.broadcasted_iota(jnp.int32, (rows, cols), 1)
    return col <= row


def _flash_tiles(t):
    tq = min(1024, t)
    halves = 2 if tq % 32 == 0 else 1
    return tq, t // tq, halves, tq // halves


def _flash_fwd(q, k, vx, ride=None):
    t = q.shape[0]
    tq, nq, halves, hr = _flash_tiles(t)
    dqk, dv = MLA_QK_PAD, MLA_V

    def body(q_ref, k_ref, v_ref, o_ref, lse_ref, m_ref, acc_ref):
        i = pl.program_id(1)
        m_ref[...] = jnp.full_like(m_ref, NEG)
        acc_ref[...] = jnp.zeros_like(acc_ref)

        def scores(j):
            kb = k_ref[pl.ds(pl.multiple_of(j * tq, tq), tq), :]
            return tuple(_dot(q_ref[pl.ds(hh * hr, hr), :], kb, 1, 1) for hh in range(halves))

        def consume(j, s_all, masked):
            vb = v_ref[pl.ds(pl.multiple_of(j * tq, tq), tq), :]
            for hh in range(halves):
                rs = pl.ds(hh * hr, hr)
                s = s_all[hh]
                if masked:
                    s = jnp.where(_diag_mask(hr, tq, hh * hr), s, NEG)
                m_old = m_ref[rs, :]
                m_new = jnp.maximum(m_old, jnp.max(s, axis=1, keepdims=True))
                pr = jnp.exp2(s - m_new)
                acc_ref[rs, :] = jnp.exp2(m_old - m_new) * acc_ref[rs, :] + _dot(pr.astype(BF16), vb)
                m_ref[rs, :] = m_new

        def loop_body(j, s_cur):
            s_next = scores(j + 1)
            consume(j, s_cur, False)
            return s_next

        consume(i, lax.fori_loop(0, i, loop_body, scores(0)), True)
        acc = acc_ref[...]
        l = acc[:, dv:dv + 1]
        o_ref[...] = acc[:, :dv] / l
        lse_ref[0] = m_ref[...] + jnp.log(l) * LOG2E

    outs, rode = _call(
        body, name="mla_flash_fwd", grid=(MLA_HEADS, nq),
        in_specs=[pl.BlockSpec((tq, dqk), lambda h, i: (i, h)),
                  pl.BlockSpec((t, dqk), lambda h, i: (0, h)),
                  pl.BlockSpec((t, 2 * dv), lambda h, i: (0, h))],
        out_specs=[pl.BlockSpec((tq, dv), lambda h, i: (i, h)),
                   pl.BlockSpec((1, tq, 1), lambda h, i: (h, i, 0))],
        out_shape=[jax.ShapeDtypeStruct((t, MLA_HEADS * dv), F32),
                   jax.ShapeDtypeStruct((MLA_HEADS, t, 1), F32)],
        scratch_shapes=[pltpu.VMEM((tq, 1), F32), pltpu.VMEM((tq, 2 * dv), F32)],
        sem=("parallel", "arbitrary"), args=(q, k, vx), ride=ride)
    return tuple(outs) if ride is None else (tuple(outs), rode)


def _flash_bwd_dq(q, k, v, do, o, lse, ride=None):
    t = q.shape[0]
    tq, nq, halves, hr = _flash_tiles(t)
    dqk, dv = MLA_QK_PAD, MLA_V

    def body(q_ref, k_ref, v_ref, do_ref, o_ref, lse_ref, dq_ref, dl_ref, acc_ref):
        i = pl.program_id(1)
        dl_ref[0] = jnp.sum(do_ref[...].astype(F32) * o_ref[...], axis=1, keepdims=True)
        acc_ref[...] = jnp.zeros_like(acc_ref)

        def step(j, masked):
            off = pl.multiple_of(j * tq, tq)
            kb = k_ref[pl.ds(off, tq), :]
            vb = v_ref[pl.ds(off, tq), :]
            for hh in range(halves):
                rs = pl.ds(hh * hr, hr)
                nc = (hh + 1) * hr if masked else tq
                kc, vc = kb[:nc], vb[:nc]
                s = _dot(q_ref[rs, :], kc, 1, 1)
                if masked:
                    s = jnp.where(_diag_mask(hr, nc, hh * hr), s, NEG)
                pr = jnp.exp2(s - lse_ref[0, rs, :])
                dp = _dot(do_ref[rs, :], vc, 1, 1)
                ds = pr * (dp - dl_ref[0, rs, :])
                acc_ref[rs, :] += _dot(ds.astype(BF16), kc)

        def loop_body(j, carry):
            step(j, False)
            return carry

        lax.fori_loop(0, i, loop_body, 0)
        step(i, True)
        dq_ref[...] = (acc_ref[...] * ATT_SCALE).astype(dq_ref.dtype)

    outs, rode = _call(
        body, name="mla_flash_bwd_dq", grid=(MLA_HEADS, nq),
        in_specs=[pl.BlockSpec((tq, dqk), lambda h, i: (i, h)),
                  pl.BlockSpec((t, dqk), lambda h, i: (0, h)),
                  pl.BlockSpec((t, dv), lambda h, i: (0, h)),
                  pl.BlockSpec((tq, dv), lambda h, i: (i, h)),
                  pl.BlockSpec((tq, dv), lambda h, i: (i, h)),
                  pl.BlockSpec((1, tq, 1), lambda h, i: (h, i, 0))],
        out_specs=[pl.BlockSpec((tq, dqk), lambda h, i: (i, h)),
                   pl.BlockSpec((1, tq, 1), lambda h, i: (h, i, 0))],
        out_shape=[jax.ShapeDtypeStruct((t, MLA_HEADS * dqk), BF16),
                   jax.ShapeDtypeStruct((MLA_HEADS, t, 1), F32)],
        scratch_shapes=[pltpu.VMEM((tq, dqk), F32)],
        sem=("parallel", "arbitrary"), args=(q, k, v, do, o, lse), ride=ride)
    return tuple(outs) if ride is None else (tuple(outs), rode)


def _flash_bwd_dkv(q, k, v, do, lse, delta):
    t = q.shape[0]
    tq, nq, halves, hr = _flash_tiles(t)
    dqk, dv = MLA_QK_PAD, MLA_V

    def body(k_ref, v_ref, q_ref, do_ref, lse_ref, dl_ref, dk_ref, dv_ref, dk_acc, dv_acc):
        j = pl.program_id(1)
        kb = k_ref[...]
        vb = v_ref[...]
        dk_acc[...] = jnp.zeros_like(dk_acc)
        dv_acc[...] = jnp.zeros_like(dv_acc)

        def step(i, masked):
            for hh in range(halves):
                rs = pl.ds(pl.multiple_of(i * tq + hh * hr, hr), hr)
                qb = q_ref[rs, :]
                dout = do_ref[rs, :]
                nc = (hh + 1) * hr if masked else tq
                s = _dot(qb, kb[:nc], 1, 1)
                if masked:
                    s = jnp.where(_diag_mask(hr, nc, hh * hr), s, NEG)
                pr = jnp.exp2(s - lse_ref[0, rs, :])
                dv_acc[pl.ds(0, nc), :] += _dot(pr.astype(BF16), dout, 0, 0)
                dp = _dot(dout, vb[:nc], 1, 1)
                ds = pr * (dp - dl_ref[0, rs, :])
                dk_acc[pl.ds(0, nc), :] += _dot(ds.astype(BF16), qb, 0, 0)

        def loop_body(i, carry):
            step(i, False)
            return carry

        step(j, True)
        lax.fori_loop(j + 1, nq, loop_body, 0)
        dk_ref[...] = (dk_acc[...] * LN2).astype(dk_ref.dtype)
        dv_ref[...] = dv_acc[...].astype(dv_ref.dtype)

    return pl.pallas_call(
        body, name="mla_flash_bwd_dkv", grid=(MLA_HEADS, nq),
        in_specs=[pl.BlockSpec((tq, dqk), lambda h, j: (j, h)),
                  pl.BlockSpec((tq, dv), lambda h, j: (j, h)),
                  pl.BlockSpec((t, dqk), lambda h, j: (0, h)),
                  pl.BlockSpec((t, dv), lambda h, j: (0, h)),
                  pl.BlockSpec((1, t, 1), lambda h, j: (h, 0, 0)),
                  pl.BlockSpec((1, t, 1), lambda h, j: (h, 0, 0))],
        out_specs=[pl.BlockSpec((tq, dqk), lambda h, j: (j, h)),
                   pl.BlockSpec((tq, dv), lambda h, j: (j, h))],
        out_shape=[jax.ShapeDtypeStruct((t, MLA_HEADS * dqk), BF16),
                   jax.ShapeDtypeStruct((t, MLA_HEADS * dv), BF16)],
        scratch_shapes=[pltpu.VMEM((tq, dqk), F32), pltpu.VMEM((tq, dv), F32)],
        compiler_params=_cparams(("parallel", "arbitrary")),
    )(k, v, q, do, lse, delta)


class _NoRides:
    def ride(self, stage, grads):
        return None

    def done(self, stage, rode, w):
        pass


def _local_step(x, target, tab, mod8, w, rides=None):
    t, d = x.shape
    rides = rides or _NoRides()
    big = {}

    def riding(stage, fn):
        r = rides.ride(stage, big)
        res = fn(r)
        if r is None:
            return res
        rides.done(stage, res[1], w)
        return res[0]
    _, pw, _, lay, _ = _in_layout(d)
    ffn = ((8 * d // 3 + 255) // 256) * 256

    def blk(arr, name):
        return (arr, pw[name], lay[name] // pw[name])

    def full(arr):
        return (arr, arr.shape[1], 0)

    g1, g2, g3 = w["norm_mix_g"], w["norm_ffn_g"], w["final_norm_g"]

    def f_ln1(xv, mod, g):
        return (xv * _rstd(xv) * g) * (1.0 + mod[1:2]) + mod[0:1]

    (h,) = _rowwise(f_ln1, [full(x)], [mod8, g1], [(d, BF16)], tile=256, name="ln1_modulate")
    p = riding("in_proj", lambda r: _mm(h, w["w_in"], name="mm_in_proj", ride=r))

    def f_gk(pgk, gkw, gkb):
        z = _dot(pgk.astype(BF16), gkw.astype(BF16)) + gkb
        return (jnp.minimum(z, 0.0) - jnp.log(1.0 + jnp.exp(-jnp.abs(z)))) / GLA_GATE_NORMALIZER

    (la,) = _rowwise(f_gk, [blk(p, "gk")], [w["gla_gk_w"], w["gla_gk_b"]], [(GLA_HEADS * GLA_DK, F32)],
                     tile=512, name="gla_gate")
    o_gla, states = _gla_fwd(p, la, lay)

    def f_gla_out(ov, pg, g):
        parts = []
        for hh in range(GLA_HEADS):
            oh = ov[:, hh * GLA_DV:(hh + 1) * GLA_DV]
            ph = pg[:, hh * GLA_DV:(hh + 1) * GLA_DV]
            parts.append(oh * _rstd(oh) * g * (ph * _sigmoid(ph)))
        return jnp.concatenate(parts, axis=1)

    (o_n,) = _rowwise(f_gla_out, [full(o_gla), blk(p, "g")], [w["gla_onorm_g"]], [(d, BF16)], tile=256,
                      name="gla_out_norm")
    y_gla = _mm(o_n, w["gla_wo"], out_dtype=BF16, name="mm_gla_wo")

    def f_mla_prep(cq, ckv, kr, tb, gq, gkv):
        return cq * _rstd(cq) * gq, ckv * _rstd(ckv) * gkv, _rope(kr, tb, 1.0)

    cqn, ckvn, krr = _rowwise(f_mla_prep, [blk(p, "cq"), blk(p, "ckv"), blk(p, "kr"), full(tab)],
                              [w["mla_q_norm_g"], w["mla_kv_norm_g"]],
                              [(pw["cq"], BF16), (pw["ckv"], BF16), (LANE, F32)], tile=512, name="mla_prep")
    qlat = _mm(cqn, w["mla_wuq"], out_dtype=BF16, name="mm_mla_wuq")
    kvl = _mm(ckvn, w["mla_wukv"], out_dtype=BF16, name="mm_mla_wukv")
    hv = MLA_HEADS * MLA_V

    def f_qkv(ql, kn, vv, kr, tb):
        qs, ks, vx = [], [], []
        one_col = (lax.broadcasted_iota(jnp.int32, (ql.shape[0], LANE), 1) == 0).astype(F32)
        for hh in range(MLA_HEADS):
            o0 = hh * MLA_QK_PAD
            qs += [ql[:, o0:o0 + LANE], _rope(ql[:, o0 + LANE:o0 + 2 * LANE], tb, 1.0)]
            ks += [kn[:, hh * LANE:(hh + 1) * LANE], kr]
            vx += [vv[:, hh * MLA_V:(hh + 1) * MLA_V], one_col]
        return (jnp.concatenate(qs, axis=1) * (ATT_SCALE * LOG2E), jnp.concatenate(ks, axis=1), vv,
                jnp.concatenate(vx, axis=1))

    qa, ka, va, vxa = _rowwise(f_qkv, [full(qlat), (kvl, hv, 0), (kvl, hv, 1), full(krr), full(tab)], [],
                               [(MLA_HEADS * MLA_QK_PAD, BF16), (MLA_HEADS * MLA_QK_PAD, BF16), (hv, BF16),
                                (2 * hv, BF16)], tile=256, name="mla_qkv_build")
    o_mla, lse = riding("flash_fwd", lambda r: _flash_fwd(qa, ka, vxa, ride=r))
    y_mla = _mm(o_mla, w["mla_wo"], out_dtype=BF16, name="mm_mla_wo")

    def f_merge(yg, ym, ga, gb):
        return _sigmoid(ga) * yg + _sigmoid(gb) * ym

    (merged,) = _rowwise(f_merge, [full(y_gla), full(y_mla), blk(p, "ga"), blk(p, "gb")], [], [(d, BF16)],
                         tile=256, name="merge")
    mix = _mm(merged, w["w_out"], name="mm_w_out")

    def f_res_ln2(xv, mx, mod, g):
        x2v = xv + mod[2:3] * mx
        return x2v, (x2v * _rstd(x2v) * g) * (1.0 + mod[4:5]) + mod[3:4]

    x2, h2 = _rowwise(f_res_ln2, [full(x), full(mix)], [mod8, g2], [(d, F32), (d, BF16)], tile=256,
                      name="res_ln2_modulate")
    gu = _mm(h2, w["ffn_w_in"], out_dtype=BF16, name="mm_ffn_in")

    def f_swiglu(gv, uv):
        return gv * _sigmoid(gv) * uv

    (act,) = _rowwise(f_swiglu, [(gu, ffn, 0), (gu, ffn, 1)], [], [(ffn, BF16)], tile=128, name="swiglu")
    f_out = _mm(act, w["ffn_w_down"], name="mm_ffn_down")

    def f_head(x2v, fv, tg, mod, g):
        x3 = x2v + mod[5:6] * fv
        r = _rstd(x3)
        xh = x3 * r
        e = xh * g - tg
        loss_rows = 0.5 * jnp.mean(e * e, axis=-1, keepdims=True)
        dy = e * (1.0 / d)
        dx3 = _rms_bwd(dy * g, xh, r)
        loss = jnp.broadcast_to(jnp.sum(loss_rows, axis=0, keepdims=True), (1, LANE))
        return (dx3, dx3 * mod[5:6], loss, jnp.sum(dy * xh, axis=0, keepdims=True),
                jnp.sum(dx3 * fv, axis=0, keepdims=True))

    dx3, df, loss_v, dg3, dgate_f = _rowwise(f_head, [full(x2), full(f_out), full(target)], [mod8, g3],
                                             [(d, F32), (d, BF16)], [LANE, d, d], tile=256, name="loss_head")
    da = _mm(df, w["ffn_w_down"], tb=True, out_dtype=BF16, name="mm_ffn_down_dx")
    big["ffn_w_down"] = _mm(act, df, ta=True, out_dtype=BF16, name="mm_ffn_down_dw")

    def f_swiglu_bwd(gv, uv, dav):
        sg = _sigmoid(gv)
        return jnp.concatenate([dav * uv * (sg * (1.0 + gv * (1.0 - sg))), dav * (gv * sg)], axis=1)

    (dgu,) = _rowwise(f_swiglu_bwd, [(gu, ffn, 0), (gu, ffn, 1), full(da)], [], [(2 * ffn, BF16)], tile=128,
                      name="swiglu_bwd")
    dh2 = _mm(dgu, w["ffn_w_in"], tb=True, name="mm_ffn_in_dx")
    big["ffn_w_in"] = _mm(h2, dgu, ta=True, out_dtype=BF16, slabs=True, name="mm_ffn_in_dw")

    def f_ln2_bwd(x2v, dh, dx3v, mx, mod, g):
        r = _rstd(x2v)
        xh = x2v * r
        dn = dh * (1.0 + mod[4:5])
        dx2 = dx3v + _rms_bwd(dn * g, xh, r)
        return (dx2, dx2 * mod[2:3],
                jnp.sum(dh * (xh * g), axis=0, keepdims=True), jnp.sum(dh, axis=0, keepdims=True),
                jnp.sum(dn * xh, axis=0, keepdims=True), jnp.sum(dx2 * mx, axis=0, keepdims=True))

    dx2, dmix, dscale_f, dshift_f, dg2, dgate_m = _rowwise(
        f_ln2_bwd, [full(x2), full(dh2), full(dx3), full(mix)], [mod8, g2], [(d, F32), (d, BF16)],
        [d, d, d, d], tile=256, name="ln2_bwd")
    dmerged = _mm(dmix, w["w_out"], tb=True, out_dtype=BF16, name="mm_w_out_dx")
    big["w_out"] = _mm(merged, dmix, ta=True, out_dtype=BF16, name="mm_w_out_dw")

    def f_merge_bwd(dm, yg, ym, ga, gb):
        sa, sb = _sigmoid(ga), _sigmoid(gb)
        return dm * sa, dm * sb, dm * yg * sa * (1.0 - sa), dm * ym * sb * (1.0 - sb)

    dy_gla, dy_mla, dp_ga, dp_gb = _rowwise(
        f_merge_bwd, [full(dmerged), full(y_gla), full(y_mla), blk(p, "ga"), blk(p, "gb")], [],
        [(d, BF16)] * 4, tile=256, name="merge_bwd")
    do_n = _mm(dy_gla, w["gla_wo"], tb=True, name="mm_gla_wo_dx")
    big["gla_wo"] = _mm(o_n, dy_gla, ta=True, out_dtype=BF16, name="mm_gla_wo_dw")
    do_m = _mm(dy_mla, w["mla_wo"], tb=True, out_dtype=BF16, name="mm_mla_wo_dx")
    big["mla_wo"] = _mm(o_mla, dy_mla, ta=True, out_dtype=BF16, name="mm_mla_wo_dw")

    def f_gla_out_bwd(don, ov, pg, g):
        dos, dpgs = [], []
        dg = jnp.zeros((1, GLA_DV), F32)
        for hh in range(GLA_HEADS):
            sl = slice(hh * GLA_DV, (hh + 1) * GLA_DV)
            oh, ph, dn = ov[:, sl], pg[:, sl], don[:, sl]
            r = _rstd(oh)
            xh = oh * r
            sg = _sigmoid(ph)
            dpre = dn * (ph * sg)
            dg = dg + jnp.sum(dpre * xh, axis=0, keepdims=True)
            dos.append(_rms_bwd(dpre * g, xh, r))
            dpgs.append(dn * (xh * g) * (sg * (1.0 + ph * (1.0 - sg))))
        return jnp.concatenate(dos, axis=1), jnp.concatenate(dpgs, axis=1), dg

    do_gla, dp_g, dg_on = _rowwise(f_gla_out_bwd, [full(do_n), full(o_gla), blk(p, "g")], [w["gla_onorm_g"]],
                                   [(d, F32), (d, BF16)], [GLA_DV], tile=256, name="gla_out_norm_bwd")
    dp_q, dp_k, dp_v, dla = _gla_bwd(p, la, do_gla, states, lay)

    def f_gk_bwd(dlav, pgk, gkw, gkb):
        z = _dot(pgk.astype(BF16), gkw.astype(BF16)) + gkb
        dz = dlav * (1.0 / GLA_GATE_NORMALIZER) * _sigmoid(-z)
        return dz, _dot(dz.astype(BF16), gkw.astype(BF16), 1, 1), jnp.sum(dz, axis=0, keepdims=True)

    dz, dp_gk, dgk_b = _rowwise(f_gk_bwd, [full(dla), blk(p, "gk")], [w["gla_gk_w"], w["gla_gk_b"]],
                                [(GLA_HEADS * GLA_DK, BF16), (LANE, BF16)], [GLA_HEADS * GLA_DK], tile=512,
                                name="gla_gate_bwd")
    p_gk = lax.slice_in_dim(p, lay["gk"], lay["gk"] + LANE, axis=1)
    big["gla_gk_w"] = _mm(p_gk, dz, ta=True, name="mm_gla_gk_dw")[:GLA_GATE_RANK]

    dqa, delta = riding("flash_bwd_dq", lambda r: _flash_bwd_dq(qa, ka, va, do_m, o_mla, lse, ride=r))
    dka, dva = _flash_bwd_dkv(qa, ka, va, do_m, lse, delta)

    def f_qkv_bwd(dq, dk, dvv, tb):
        dqs, dkn = [], []
        dkr = jnp.zeros((dq.shape[0], LANE), F32)
        for hh in range(MLA_HEADS):
            o0 = hh * MLA_QK_PAD
            dqs += [dq[:, o0:o0 + LANE], _rope(dq[:, o0 + LANE:o0 + 2 * LANE], tb, -1.0)]
            dkn.append(dk[:, o0:o0 + LANE])
            dkr = dkr + dk[:, o0 + LANE:o0 + 2 * LANE]
        return jnp.concatenate(dqs, axis=1), jnp.concatenate(dkn + [dvv], axis=1), dkr

    dqlat, dkvl, dkrr = _rowwise(f_qkv_bwd, [full(dqa), full(dka), full(dva), full(tab)], [],
                                 [(MLA_HEADS * MLA_QK_PAD, BF16), (2 * hv, BF16), (LANE, F32)], tile=256,
                                 name="mla_qkv_build_bwd")
    dcqn = _mm(dqlat, w["mla_wuq"], tb=True, name="mm_mla_wuq_dx")
    big["mla_wuq"] = _mm(cqn, dqlat, ta=True, out_dtype=BF16, name="mm_mla_wuq_dw")
    dckvn = _mm(dkvl, w["mla_wukv"], tb=True, name="mm_mla_wukv_dx")
    big["mla_wukv"] = _mm(ckvn, dkvl, ta=True, out_dtype=BF16, name="mm_mla_wukv_dw")

    def f_mla_prep_bwd(dq, dkv, dkr, cq, ckv, tb, gq, gkv):
        rq, rk = _rstd(cq), _rstd(ckv)
        xq, xk = cq * rq, ckv * rk
        return (_rms_bwd(dq * gq, xq, rq), _rms_bwd(dkv * gkv, xk, rk), _rope(dkr, tb, -1.0),
                jnp.sum(dq * xq, axis=0, keepdims=True), jnp.sum(dkv * xk, axis=0, keepdims=True))

    dp_cq, dp_ckv, dp_kr, dg_q, dg_kv = _rowwise(
        f_mla_prep_bwd, [full(dcqn), full(dckvn), full(dkrr), blk(p, "cq"), blk(p, "ckv"), full(tab)],
        [w["mla_q_norm_g"], w["mla_kv_norm_g"]], [(pw["cq"], BF16), (pw["ckv"], BF16), (LANE, BF16)],
        [pw["cq"], pw["ckv"]], tile=512, name="mla_prep_bwd")

    pieces = dict(v=dp_v, g=dp_g, ga=dp_ga, gb=dp_gb, q=dp_q, k=dp_k, cq=dp_cq, ckv=dp_ckv, gk=dp_gk, kr=dp_kr)
    dp = jnp.concatenate([pieces[n] for n in MY_ORDER], axis=1)
    big["w_in"] = riding("in_proj_dw", lambda r: _mm(h, dp, ta=True, out_dtype=BF16, name="mm_in_proj_dw", ride=r))
    dh = riding("in_proj_dx", lambda r: _mm(dp, w["w_in"], tb=True, name="mm_in_proj_dx", ride=r))

    def f_ln1_bwd(xv, dhv, dx2v, mod, g):
        r = _rstd(xv)
        xh = xv * r
        dn = dhv * (1.0 + mod[1:2])
        return (dx2v + _rms_bwd(dn * g, xh, r),
                jnp.sum(dhv * (xh * g), axis=0, keepdims=True), jnp.sum(dhv, axis=0, keepdims=True),
                jnp.sum(dn * xh, axis=0, keepdims=True))

    grad_x, dscale_m, dshift_m, dg1 = _rowwise(f_ln1_bwd, [full(x), full(dh), full(dx2)], [mod8, g1],
                                               [(d, F32)], [d, d, d], tile=256, name="ln1_bwd")

    dmod = jnp.concatenate([dshift_m, dscale_m, dgate_m, dshift_f, dscale_f, dgate_f], axis=1)
    small = dict(ada_b=dmod, norm_mix_g=dg1, gla_gk_b=dgk_b, gla_onorm_g=dg_on, mla_q_norm_g=dg_q,
                 mla_kv_norm_g=dg_kv, norm_ffn_g=dg2, final_norm_g=dg3)
    return loss_v[0, 0], grad_x, big, small


N_PEER = N_DEV - 1


def _exchange_copies(ins, outs, sems, scatter):
    send_sems, recv_sems, local_sems = sems
    x, y, c = lax.axis_index("x"), lax.axis_index("y"), lax.axis_index("c")
    me = 4 * x + 2 * y + c
    peers = []
    for rel in range(1, N_DEV):
        px = 1 - x if rel & 4 else x
        py = 1 - y if rel & 2 else y
        pc = 1 - c if rel & 1 else c
        peers.append(((px, py, pc), 4 * px + 2 * py + pc))

    def remote(a, k, src_slot, dst_slot):
        src = ins[a].at[src_slot] if scatter else ins[a]
        return pltpu.make_async_remote_copy(
            src_ref=src, dst_ref=outs[a].at[dst_slot], send_sem=send_sems.at[a * N_PEER + k],
            recv_sem=recv_sems.at[a * N_PEER + k], device_id=peers[k][0], device_id_type=pl.DeviceIdType.MESH)

    local, sends, recvs = [], [], []
    for a in range(len(ins)):
        src = ins[a].at[me] if scatter else ins[a]
        local.append(pltpu.make_async_copy(src, outs[a].at[me], local_sems.at[a]))
        for k in range(N_PEER):
            sends.append(remote(a, k, peers[k][1], me))
            recvs.append(remote(a, k, peers[k][1], peers[k][1]))
    return local, sends, recvs


def _exchange_start(ins, outs, sems, scatter):
    local, sends, _ = _exchange_copies(ins, outs, sems, scatter)
    for cp in local + sends:
        cp.start()


def _exchange_wait(ins, outs, sems, scatter):
    local, sends, recvs = _exchange_copies(ins, outs, sems, scatter)
    for cp in recvs:
        cp.wait_recv()
    for cp in sends:
        cp.wait_send()
    for cp in local:
        cp.wait()


def _exchange_shapes(arrs, scatter):
    n = len(arrs)
    out_shape = [jax.ShapeDtypeStruct(a.shape if scatter else (N_DEV,) + a.shape, a.dtype) for a in arrs]
    sems = [pltpu.SemaphoreType.DMA((n * N_PEER,)), pltpu.SemaphoreType.DMA((n * N_PEER,)),
            pltpu.SemaphoreType.DMA((n,))]
    return out_shape, sems


def _exchange(arrs, *, scatter, name):
    n = len(arrs)

    def body(*refs):
        ins, outs, sems = refs[:n], refs[n:2 * n], refs[2 * n:]
        _exchange_start(ins, outs, sems, scatter)
        _exchange_wait(ins, outs, sems, scatter)

    hbm = pl.BlockSpec(memory_space=pltpu.HBM)
    out_shape, sems = _exchange_shapes(arrs, scatter)
    return pl.pallas_call(body, name=name, in_specs=[hbm] * n, out_specs=[hbm] * n, out_shape=out_shape,
                          scratch_shapes=sems)(*arrs)


def _call(body, *, name, grid, in_specs, out_specs, out_shape, scratch_shapes, sem, args, ride=None):
    if ride is None:
        res = pl.pallas_call(body, name=name, grid=grid, in_specs=in_specs, out_specs=out_specs, out_shape=out_shape,
                             scratch_shapes=scratch_shapes, compiler_params=_cparams(sem))(*args)
        return res, None
    arrs, scatter = ride
    n, n_in, n_out, n_scr = len(arrs), len(in_specs), len(out_specs), len(scratch_shapes)
    x_shape, x_sems = _exchange_shapes(arrs, scatter)

    def hosted(*refs):
        c_in, x_in = refs[:n_in], refs[n_in:n_in + n]
        c_out, x_out = refs[n_in + n:n_in + n + n_out], refs[n_in + n + n_out:n_in + 2 * n + n_out]
        scr = refs[n_in + 2 * n + n_out:]
        c_scr, sems = scr[:n_scr], scr[n_scr:]
        first = functools.reduce(jnp.logical_and, [pl.program_id(a) == 0 for a in range(len(grid))])
        last = functools.reduce(jnp.logical_and, [pl.program_id(a) == grid[a] - 1 for a in range(len(grid))])

        @pl.when(first)
        def _():
            _exchange_start(x_in, x_out, sems, scatter)

        body(*c_in, *c_out, *c_scr)

        @pl.when(last)
        def _():
            _exchange_wait(x_in, x_out, sems, scatter)

    hbm = pl.BlockSpec(memory_space=pltpu.HBM)
    res = pl.pallas_call(
        hosted, name=name, grid=grid, in_specs=list(in_specs) + [hbm] * n, out_specs=list(out_specs) + [hbm] * n,
        out_shape=list(out_shape) + x_shape, scratch_shapes=list(scratch_shapes) + x_sems,
        compiler_params=_cparams(("arbitrary",) * len(grid)))(*args, *arrs)
    return res[:n_out], res[n_out:]


def _adamw_math(w, g, m, v):
    m_new = ADAM_B1 * m + (1.0 - ADAM_B1) * g
    v_new = ADAM_B2 * v + (1.0 - ADAM_B2) * (g * g)
    m_hat = m_new / (1.0 - ADAM_B1 ** ADAM_STEP)
    v_hat = v_new / (1.0 - ADAM_B2 ** ADAM_STEP)
    delta = -ADAM_LR * (m_hat / (jnp.sqrt(v_hat) + ADAM_EPS) + ADAM_WD * w)
    return delta, m_new, v_new


def _adamw(w, g, m, v, *, name):
    r, c = w.shape
    slots = g.ndim == 3
    tr = r
    for cand in (128, 64, 32, 16):
        if r % cand == 0 and r > cand:
            tr = cand
            break

    def body(w_ref, g_ref, m_ref, v_ref, go_ref, d_ref, mo_ref, vo_ref):
        if slots:
            gv = g_ref[0].astype(F32)
            for s in range(1, N_DEV):
                gv = gv + g_ref[s].astype(F32)
        else:
            gv = g_ref[...]
        delta, m_new, v_new = _adamw_math(w_ref[...], gv, m_ref[...], v_ref[...])
        go_ref[...] = gv
        d_ref[...] = delta
        mo_ref[...] = m_new
        vo_ref[...] = v_new

    spec = pl.BlockSpec((tr, c), lambda i: (i, 0))
    g_spec = pl.BlockSpec((N_DEV, tr, c), lambda i: (0, i, 0)) if slots else spec
    return pl.pallas_call(
        body, name=name, grid=(r // tr,), in_specs=[spec, g_spec, spec, spec], out_specs=[spec] * 4,
        out_shape=[jax.ShapeDtypeStruct((r, c), F32)] * 4, compiler_params=_cparams(("parallel",)),
    )(w, g, m, v)


def _unshard_cols(g):
    return jnp.transpose(g, (1, 0, 2)).reshape(g.shape[1], -1)

def _shard_cols(full):
    r = full.shape[0]
    return jnp.transpose(full.reshape(r, N_DEV, -1), (1, 0, 2))


def _w_in_to_mine(w_ref_layout, d):
    wd, pw, ref_off, _, _ = _in_layout(d)
    cols = []
    for n in MY_ORDER:
        piece = lax.slice_in_dim(w_ref_layout, ref_off[n], ref_off[n] + wd[n], axis=1)
        if pw[n] != wd[n]:
            piece = jnp.pad(piece, ((0, 0), (0, pw[n] - wd[n])))
        cols.append(piece)
    return jnp.concatenate(cols, axis=1)


def _w_in_from_mine(g_mine, d):
    wd, _, _, my_off, _ = _in_layout(d)
    return jnp.concatenate([lax.slice_in_dim(g_mine, my_off[n], my_off[n] + wd[n], axis=1) for n in IN_NAMES],
                           axis=1)


def _wuq_to_mine(wq):
    r = wq.shape[0]
    w3 = wq.reshape(r, MLA_HEADS, MLA_NOPE + MLA_ROPE)
    w3 = jnp.pad(w3, ((0, 0), (0, 0), (0, MLA_QK_PAD - MLA_NOPE - MLA_ROPE)))
    return w3.reshape(r, MLA_HEADS * MLA_QK_PAD)


def _wuq_from_mine(g):
    r = g.shape[0]
    return g.reshape(r, MLA_HEADS, MLA_QK_PAD)[:, :, :MLA_NOPE + MLA_ROPE].reshape(r, -1)


def _wukv_to_mine(wkv):
    r = wkv.shape[0]
    w3 = wkv.reshape(r, MLA_HEADS, MLA_NOPE + MLA_V)
    return jnp.concatenate([w3[:, :, :MLA_NOPE].reshape(r, -1), w3[:, :, MLA_NOPE:].reshape(r, -1)], axis=1)


def _wukv_from_mine(g):
    r = g.shape[0]
    kn = g[:, :MLA_HEADS * MLA_NOPE].reshape(r, MLA_HEADS, MLA_NOPE)
    vv = g[:, MLA_HEADS * MLA_NOPE:].reshape(r, MLA_HEADS, MLA_V)
    return jnp.concatenate([kn, vv], axis=2).reshape(r, -1)


COL_SHARDED = ("w_in", "gla_gk_w", "mla_wuq", "mla_wukv", "ffn_w_in")


def _gathered_to_mine(name, g, d):
    full = _unshard_cols(g) if name in COL_SHARDED else g.reshape(-1, g.shape[-1])
    if name == "w_in":
        return _w_in_to_mine(full, d)
    if name == "gla_gk_w":
        return jnp.pad(full, ((0, LANE - GLA_GATE_RANK), (0, 0)))
    if name == "mla_wuq":
        return _wuq_to_mine(full)
    if name == "mla_wukv":
        return _wukv_to_mine(full)
    return full


def _grad_to_slabs(name, g, d):
    if g.ndim == 3:
        return g.astype(BF16)
    if name == "w_in":
        g = _w_in_from_mine(g, d)
    elif name == "mla_wuq":
        g = _wuq_from_mine(g)
    elif name == "mla_wukv":
        g = _wukv_from_mine(g)
    s = _shard_cols(g) if name in COL_SHARDED else g.reshape(N_DEV, -1, g.shape[-1])
    return s.astype(BF16)


class _Rides:
    GATHER = {"in_proj": ("gla_wo", "mla_wuq", "mla_wukv", "mla_wo", "w_out"),
              "flash_fwd": ("ffn_w_in", "ffn_w_down")}
    SCATTER = {"flash_bwd_dq": ("ffn_w_in", "ffn_w_down", "w_out", "gla_wo", "mla_wo", "gla_gk_w"),
               "in_proj_dw": ("mla_wuq", "mla_wukv"),
               "in_proj_dx": ("w_in",)}

    def __init__(self, send, d):
        self.send, self.d, self.recv = send, d, {}

    def ride(self, stage, grads):
        if stage in self.GATHER:
            return [self.send[n] for n in self.GATHER[stage]], False
        return [_grad_to_slabs(n, grads[n], self.d) for n in self.SCATTER[stage]], True

    def done(self, stage, rode, w):
        if stage in self.GATHER:
            for n, g in zip(self.GATHER[stage], rode):
                w[n] = _gathered_to_mine(n, g, self.d)
        else:
            self.recv.update(zip(self.SCATTER[stage], rode))


def kernel(x, c, positions, ada_w, ada_b, norm_mix_g, w_in, gla_gk_w, gla_gk_b, gla_onorm_g, gla_wo, mla_q_norm_g, mla_wuq, mla_kv_norm_g, mla_wukv, mla_wo, w_out, norm_ffn_g, ffn_w_in, ffn_w_down, final_norm_g, loss_target, m_ada_w, m_ada_b, m_norm_mix_g, m_w_in, m_gla_gk_w, m_gla_gk_b, m_gla_onorm_g, m_gla_wo, m_mla_q_norm_g, m_mla_wuq, m_mla_kv_norm_g, m_mla_wukv, m_mla_wo, m_w_out, m_norm_ffn_g, m_ffn_w_in, m_ffn_w_down, m_final_norm_g, v_ada_w, v_ada_b, v_norm_mix_g, v_w_in, v_gla_gk_w, v_gla_gk_b, v_gla_onorm_g, v_gla_wo, v_mla_q_norm_g, v_mla_wuq, v_mla_kv_norm_g, v_mla_wukv, v_mla_wo, v_w_out, v_norm_ffn_g, v_ffn_w_in, v_ffn_w_down, v_final_norm_g):
    wts = dict(ada_w=ada_w, ada_b=ada_b, norm_mix_g=norm_mix_g, w_in=w_in, gla_gk_w=gla_gk_w, gla_gk_b=gla_gk_b,
               gla_onorm_g=gla_onorm_g, gla_wo=gla_wo, mla_q_norm_g=mla_q_norm_g, mla_wuq=mla_wuq,
               mla_kv_norm_g=mla_kv_norm_g, mla_wukv=mla_wukv, mla_wo=mla_wo, w_out=w_out, norm_ffn_g=norm_ffn_g,
               ffn_w_in=ffn_w_in, ffn_w_down=ffn_w_down, final_norm_g=final_norm_g)
    mom_m = dict(zip(WEIGHTS, (m_ada_w, m_ada_b, m_norm_mix_g, m_w_in, m_gla_gk_w, m_gla_gk_b, m_gla_onorm_g,
                               m_gla_wo, m_mla_q_norm_g, m_mla_wuq, m_mla_kv_norm_g, m_mla_wukv, m_mla_wo, m_w_out,
                               m_norm_ffn_g, m_ffn_w_in, m_ffn_w_down, m_final_norm_g)))
    mom_v = dict(zip(WEIGHTS, (v_ada_w, v_ada_b, v_norm_mix_g, v_w_in, v_gla_gk_w, v_gla_gk_b, v_gla_onorm_g,
                               v_gla_wo, v_mla_q_norm_g, v_mla_wuq, v_mla_kv_norm_g, v_mla_wukv, v_mla_wo, v_w_out,
                               v_norm_ffn_g, v_ffn_w_in, v_ffn_w_down, v_final_norm_g)))
    seq, d = x.shape[1], x.shape[2]
    me = 4 * lax.axis_index("x") + 2 * lax.axis_index("y") + lax.axis_index("c")

    def two_d(a):
        return a.reshape(a.shape[-2], a.shape[-1]) if a.ndim >= 2 else a.reshape(1, -1)

    shard = {n: two_d(wts[n]) for n in BIG}
    send = {n: shard[n].astype(F32 if n == "gla_gk_w" else BF16) for n in BIG}
    got = _exchange([send["w_in"], send["gla_gk_w"], two_d(c)], scatter=False, name="comm_all_gather_first")
    c_all = got[2].reshape(N_DEV, d)
    w = dict(
        w_in=_gathered_to_mine("w_in", got[0], d), gla_gk_w=_gathered_to_mine("gla_gk_w", got[1], d),
        gla_gk_b=two_d(gla_gk_b), gla_onorm_g=two_d(gla_onorm_g), mla_q_norm_g=two_d(mla_q_norm_g),
        mla_kv_norm_g=two_d(mla_kv_norm_g), norm_mix_g=two_d(norm_mix_g), norm_ffn_g=two_d(norm_ffn_g),
        final_norm_g=two_d(final_norm_g))
    rides = _Rides(send, d)

    c_pad = jnp.pad(c_all, ((0, 16 - N_DEV), (0, 0)))
    (c_act,) = _rowwise(lambda cv: cv * _sigmoid(cv), [(c_pad, d, 0)], [], [(d, F32)], tile=16, name="silu_c")
    ada_w2 = two_d(ada_w)
    mod_part = _mm(c_act, ada_w2, name="mm_ada")[:N_DEV]
    (mod_all,) = _exchange([mod_part], scatter=False, name="comm_all_gather_mod")
    mod_mine = lax.dynamic_index_in_dim(mod_all, me, axis=1, keepdims=False).reshape(1, -1) + two_d(ada_b)
    mod8 = jnp.pad(mod_mine.reshape(6, d), ((0, 2), (0, 0)))

    inv_freq = ROPE_THETA ** (-jnp.arange(0, MLA_ROPE, 2, dtype=F32) / MLA_ROPE)
    ang = positions.reshape(seq, 1).astype(F32) * inv_freq[None, :]
    cos, sin, z32 = jnp.cos(ang), jnp.sin(ang), jnp.zeros((seq, 32), F32)
    tab = jnp.concatenate([cos, cos, z32, z32, -sin, z32, z32, z32, z32, sin, z32, z32], axis=1)
    loss_local, grad_x, _, small = _local_step(x.reshape(seq, d), loss_target.reshape(seq, d), tab, mod8, w, rides)

    recv = rides.recv
    pack = jnp.concatenate([small[n] for n in SMALL], axis=1)
    (pack_all,) = _exchange([pack], scatter=False, name="comm_all_gather_small")
    pack_all = pack_all.reshape(N_DEV, -1)

    res = {}
    for n in BIG:
        res[n] = _adamw(shard[n], recv[n], two_d(mom_m[n]), two_d(mom_v[n]), name="adamw_" + n)
    n_ada = ada_w2.shape[1]
    dmod_cols = lax.dynamic_slice_in_dim(pack_all[:, :6 * d], me * n_ada, n_ada, axis=1)

    def f_outer(cat, dm):
        acc = cat[:, 0:1] * dm[0:1]
        for b in range(1, N_DEV):
            acc = acc + cat[:, b:b + 1] * dm[b:b + 1]
        return acc

    (g_ada_w,) = _rowwise(f_outer, [(jnp.transpose(c_act[:N_DEV]), N_DEV, 0)], [dmod_cols], [(n_ada, F32)],
                          tile=256, name="ada_w_grad")
    res["ada_w"] = _adamw(ada_w2, g_ada_w, two_d(m_ada_w), two_d(v_ada_w), name="adamw_ada_w")
    w_small = jnp.concatenate([two_d(wts[n]) for n in SMALL], axis=1)
    m_small = jnp.concatenate([two_d(mom_m[n]) for n in SMALL], axis=1)
    v_small = jnp.concatenate([two_d(mom_v[n]) for n in SMALL], axis=1)
    small_res = _adamw(w_small, pack_all.reshape(N_DEV, 1, -1), m_small, v_small, name="adamw_small")
    off = 0
    for n in SMALL:
        width = wts[n].size
        res[n] = tuple(lax.slice_in_dim(a, off, off + width, axis=1) for a in small_res)
        off += width

    loss = lax.psum(loss_local, ("x", "y", "c"))
    outs = [loss, grad_x.reshape(x.shape)]
    for kind in range(4):
        outs += [res[n][kind].reshape(wts[n].shape) for n in WEIGHTS]
    return tuple(outs)
```

```python
import functools

import jax
import jax.numpy as jnp
from jax import lax
from jax.experimental import pallas as pl
from jax.experimental.pallas import tpu as pltpu

F32 = jnp.float32
BF16 = jnp.bfloat16

N_DEV = 8
GLA_HEADS = 4
GLA_DK = 256
GLA_DV = 512
GLA_GATE_RANK = 16
GLA_GATE_NORMALIZER = 16.0
GLA_CHUNK = 64
MLA_HEADS = 16
MLA_NOPE = 128
MLA_ROPE = 64
MLA_V = 128
MLA_QK_PAD = 256
ROPE_THETA = 10000.0
NORM_EPS = 1e-6
ATT_SCALE = (MLA_NOPE + MLA_ROPE) ** -0.5
GLA_QSCALE = GLA_DK ** -0.5

ADAM_LR = 0.001
ADAM_B1 = 0.9
ADAM_B2 = 0.999
ADAM_EPS = 1e-08
ADAM_WD = 0.01
ADAM_STEP = 10

LANE = 128
VMEM_LIMIT = 48 * 1024 * 1024
FLASH_BWD_VMEM = 58 * 1024 * 1024
MM_TILE_BYTES = 4 * 1024 * 1024
LOG2E = 1.4426950408889634
LN2 = 0.6931471805599453
NEG = -1e30

IN_NAMES = ("q", "k", "v", "g", "gk", "cq", "ckv", "kr", "ga", "gb")
MY_ORDER = ("v", "g", "ga", "gb", "q", "k", "cq", "ckv", "gk", "kr")

WEIGHTS = ("ada_w", "ada_b", "norm_mix_g", "w_in", "gla_gk_w", "gla_gk_b", "gla_onorm_g", "gla_wo",
           "mla_q_norm_g", "mla_wuq", "mla_kv_norm_g", "mla_wukv", "mla_wo", "w_out", "norm_ffn_g",
           "ffn_w_in", "ffn_w_down", "final_norm_g")
BIG = ("w_in", "gla_gk_w", "gla_wo", "mla_wuq", "mla_wukv", "mla_wo", "w_out", "ffn_w_in", "ffn_w_down")
SMALL = ("ada_b", "norm_mix_g", "gla_gk_b", "gla_onorm_g", "mla_q_norm_g", "mla_kv_norm_g", "norm_ffn_g",
         "final_norm_g")


def _in_layout(d):
    w = dict(q=d // 2, k=d // 2, v=d, g=d, gk=GLA_GATE_RANK, cq=d // 4, ckv=512, kr=MLA_ROPE, ga=d, gb=d)
    pw = {n: -(-w[n] // LANE) * LANE for n in w}
    ref_off, o = {}, 0
    for n in IN_NAMES:
        ref_off[n] = o
        o += w[n]
    my_off, o = {}, 0
    for n in MY_ORDER:
        assert o % pw[n] == 0
        my_off[n] = o
        o += pw[n]
    return w, pw, ref_off, my_off, o


def _cparams(sem=None, vmem_limit=VMEM_LIMIT):
    return pltpu.CompilerParams(dimension_semantics=sem, vmem_limit_bytes=vmem_limit)


def _dot(a, b, ca=1, cb=0):
    return lax.dot_general(a, b, (((ca,), (cb,)), ((), ())), preferred_element_type=F32)


def _tile(n, cap):
    if n <= cap:
        return n
    t = (cap // LANE) * LANE
    while t >= LANE:
        if n % t == 0:
            return t
        t -= LANE
    return n


def _mm(a, b, *, ta=False, tb=False, out_dtype=F32, name, ride=None, slabs=False):
    m, k = (a.shape[1], a.shape[0]) if ta else a.shape
    n = b.shape[0] if tb else b.shape[1]
    assert k == (b.shape[1] if tb else b.shape[0])
    wide = max(a.dtype.itemsize, b.dtype.itemsize) > 2
    tm, tn, tk = _tile(m, 1024), _tile(n, 1024), _tile(k, MM_TILE_BYTES // (1024 * (4 if wide else 2)))
    if slabs:
        tn = n // N_DEV
        assert tn % LANE == 0
    nk = k // tk

    def product(a_ref, b_ref):
        return _dot(a_ref[...].astype(BF16), b_ref[...].astype(BF16), 0 if ta else 1, 1 if tb else 0)

    def store(o_ref, val):
        if slabs:
            o_ref[0] = val.astype(o_ref.dtype)
        else:
            o_ref[...] = val.astype(o_ref.dtype)

    def body_one(a_ref, b_ref, o_ref):
        store(o_ref, product(a_ref, b_ref))

    def body_acc(a_ref, b_ref, o_ref, acc_ref):
        kk = pl.program_id(2)

        @pl.when(kk == 0)
        def _():
            acc_ref[...] = jnp.zeros_like(acc_ref)

        acc_ref[...] += product(a_ref, b_ref)

        @pl.when(kk == nk - 1)
        def _():
            store(o_ref, acc_ref[...])

    a_spec = (pl.BlockSpec((tk, tm), lambda i, j, kk: (kk, i)) if ta
              else pl.BlockSpec((tm, tk), lambda i, j, kk: (i, kk)))
    b_spec = (pl.BlockSpec((tn, tk), lambda i, j, kk: (j, kk)) if tb
              else pl.BlockSpec((tk, tn), lambda i, j, kk: (kk, j)))
    (out,), rode = _call(
        body_one if nk == 1 else body_acc, name=name, grid=(m // tm, n // tn, nk), in_specs=[a_spec, b_spec],
        out_specs=[pl.BlockSpec((1, tm, tn), lambda i, j, kk: (j, i, 0)) if slabs
                   else pl.BlockSpec((tm, tn), lambda i, j, kk: (i, j))],
        out_shape=[jax.ShapeDtypeStruct((N_DEV, m, tn) if slabs else (m, n), out_dtype)],
        scratch_shapes=[] if nk == 1 else [pltpu.VMEM((tm, tn), F32)],
        sem=("parallel", "parallel", "arbitrary"), args=(a, b), ride=ride)
    return out if ride is None else (out, rode)


def _rowwise(fn, rows, vecs, outs, sums=(), *, tile, name):
    t = rows[0][0].shape[0]
    tile = min(tile, t)
    assert t % tile == 0
    n_rows, n_vecs, n_outs = len(rows), len(vecs), len(outs)

    def body(*refs):
        ins = [r[...].astype(F32) for r in refs[:n_rows + n_vecs]]
        res = fn(*ins)
        if not isinstance(res, (tuple, list)):
            res = (res,)
        out_refs = refs[n_rows + n_vecs:]
        for r, val in zip(out_refs[:n_outs], res[:n_outs]):
            r[...] = val.astype(r.dtype)
        if sums:
            first = pl.program_id(0) == 0
            for r, val in zip(out_refs[n_outs:], res[n_outs:]):
                @pl.when(first)
                def _(r=r):
                    r[...] = jnp.zeros_like(r)
                r[...] += val

    in_specs = [pl.BlockSpec((tile, w), lambda i, cb=cb: (i, cb)) for (_, w, cb) in rows]
    in_specs += [pl.BlockSpec(v.shape, lambda i: (0, 0)) for v in vecs]
    out_specs = [pl.BlockSpec((tile, w), lambda i: (i, 0)) for (w, _) in outs]
    out_specs += [pl.BlockSpec((1, w), lambda i: (0, 0)) for w in sums]
    out_shape = [jax.ShapeDtypeStruct((t, w), dt) for (w, dt) in outs]
    out_shape += [jax.ShapeDtypeStruct((1, w), F32) for w in sums]
    res = pl.pallas_call(
        body, name=name, grid=(t // tile,), in_specs=in_specs, out_specs=out_specs, out_shape=out_shape,
        compiler_params=_cparams(("arbitrary",)),
    )(*[r[0] for r in rows], *vecs)
    return res


def _rstd(x):
    return lax.rsqrt(jnp.mean(x * x, axis=-1, keepdims=True) + NORM_EPS)


def _sigmoid(x):
    return 1.0 / (1.0 + jnp.exp(-x))


def _rms_bwd(dxh, xh, r):
    return r * (dxh - xh * jnp.mean(dxh * xh, axis=-1, keepdims=True))


def _rope(t, tab, sign):
    cosf, sin_a, sin_b = tab[:, :LANE], tab[:, LANE:2 * LANE], tab[:, 2 * LANE:]
    return t * cosf + sign * (pltpu.roll(t, 96, 1) * sin_a + pltpu.roll(t, 32, 1) * sin_b)


def _split3(x):
    hi = x.astype(BF16)
    r1 = x - hi.astype(F32)
    mid = r1.astype(BF16)
    lo = (r1 - mid.astype(F32)).astype(BF16)
    return hi, mid, lo


def _tri_sum(tri_bf16, x):
    hi, mid, lo = _split3(x)
    return _dot(tri_bf16, hi) + _dot(tri_bf16, mid) + _dot(tri_bf16, lo)


def _dot_nt2(a, b):
    a_hi = a.astype(BF16)
    a_lo = (a - a_hi.astype(F32)).astype(BF16)
    b_hi = b.astype(BF16)
    b_lo = (b - b_hi.astype(F32)).astype(BF16)
    return _dot(a_hi, b_hi, 1, 1) + _dot(a_hi, b_lo, 1, 1) + _dot(a_lo, b_hi, 1, 1)


def _gla_specs(t, rows, lay, reverse):
    nb = t // rows
    blk = (lambda i: nb - 1 - i) if reverse else (lambda i: i)
    qb, kb = lay["q"] // GLA_DK, lay["k"] // GLA_DK
    vb = lay["v"] // GLA_DV
    return [
        pl.BlockSpec((rows, GLA_DK), lambda h, i: (blk(i), qb + h)),
        pl.BlockSpec((rows, GLA_DK), lambda h, i: (blk(i), kb + h)),
        pl.BlockSpec((rows, GLA_DV), lambda h, i: (blk(i), vb + h)),
        pl.BlockSpec((rows, GLA_DK), lambda h, i: (blk(i), h)),
    ], blk


def _gla_fwd(p, la, lay):
    t = p.shape[0]
    rows = min(512, t)
    nb, nc = t // rows, rows // GLA_CHUNK
    c64 = GLA_CHUNK

    def body(q_ref, k_ref, v_ref, la_ref, o_ref, st_ref, s_ref):
        @pl.when(pl.program_id(1) == 0)
        def _():
            s_ref[...] = jnp.zeros_like(s_ref)

        r = lax.broadcasted_iota(jnp.int32, (c64, c64), 0)
        cc = lax.broadcasted_iota(jnp.int32, (c64, c64), 1)
        tril = cc <= r
        tril_b = tril.astype(BF16)
        for c in range(nc):
            sl = pl.ds(c * c64, c64)
            b = _tri_sum(tril_b, la_ref[sl, :])
            b_last = b[c64 - 1:c64, :]
            q = q_ref[sl, :] * GLA_QSCALE
            k = k_ref[sl, :]
            v = v_ref[sl, :].astype(BF16)
            qt_f = q * jnp.exp(b)
            qt = qt_f.astype(BF16)
            kh = (k * jnp.exp(b_last - b)).astype(BF16)
            s_prev = s_ref[...]
            st_ref[0, c] = s_prev
            att = jnp.where(tril, _dot_nt2(qt_f, k * jnp.exp(-b)), 0.0)
            o_ref[sl, :] = _dot(qt, s_prev.astype(BF16), 1, 1) + _dot(att.astype(BF16), v)
            s_ref[...] = s_prev * jnp.exp(b_last) + _dot(v, kh, 0, 0)

    in_specs, _ = _gla_specs(t, rows, lay, False)
    return pl.pallas_call(
        body, name="gla_fwd", grid=(GLA_HEADS, nb), in_specs=in_specs,
        out_specs=[pl.BlockSpec((rows, GLA_DV), lambda h, i: (i, h)),
                   pl.BlockSpec((1, nc, GLA_DV, GLA_DK), lambda h, i: (h, i, 0, 0))],
        out_shape=[jax.ShapeDtypeStruct((t, GLA_HEADS * GLA_DV), F32),
                   jax.ShapeDtypeStruct((GLA_HEADS, t // c64, GLA_DV, GLA_DK), F32)],
        scratch_shapes=[pltpu.VMEM((GLA_DV, GLA_DK), F32)],
        compiler_params=_cparams(("parallel", "arbitrary")),
    )(p, p, p, la)


def _gla_bwd(p, la, do, states, lay):
    t = p.shape[0]
    rows = min(512, t)
    nb, nc = t // rows, rows // GLA_CHUNK
    c64 = GLA_CHUNK

    def body(q_ref, k_ref, v_ref, la_ref, do_ref, st_ref, dq_ref, dk_ref, dv_ref, dla_ref, ds_ref):
        @pl.when(pl.program_id(1) == 0)
        def _():
            ds_ref[...] = jnp.zeros_like(ds_ref)

        r = lax.broadcasted_iota(jnp.int32, (c64, c64), 0)
        cc = lax.broadcasted_iota(jnp.int32, (c64, c64), 1)
        tril = cc <= r
        tril_b = tril.astype(BF16)
        triu_b = (cc >= r).astype(BF16)
        for c in reversed(range(nc)):
            sl = pl.ds(c * c64, c64)
            b = _tri_sum(tril_b, la_ref[sl, :])
            b_last = b[c64 - 1:c64, :]
            eb, enb, ebl_b, ebl = jnp.exp(b), jnp.exp(-b), jnp.exp(b_last - b), jnp.exp(b_last)
            k = k_ref[sl, :]
            qt_f = q_ref[sl, :] * GLA_QSCALE * eb
            kt_f = k * enb
            kh_f = k * ebl_b
            qt, kt, kh = qt_f.astype(BF16), kt_f.astype(BF16), kh_f.astype(BF16)
            v_f = v_ref[sl, :]
            dout_f = do_ref[sl, :]
            v, dout = v_f.astype(BF16), dout_f.astype(BF16)
            s_prev = st_ref[0, c]
            ds_next = ds_ref[...]
            ds_next_b = ds_next.astype(BF16)
            att = jnp.where(tril, _dot_nt2(qt_f, kt_f), 0.0).astype(BF16)
            datt = jnp.where(tril, _dot_nt2(dout_f, v_f), 0.0).astype(BF16)
            dqt = _dot(dout, s_prev.astype(BF16)) + _dot(datt, kt)
            dkt = _dot(datt, qt, 0, 0)
            dv = _dot(att, dout, 0, 0) + _dot(kh, ds_next_b, 1, 1)
            dkh = _dot(v, ds_next_b)
            d_ebl = jnp.sum(ds_next * s_prev, axis=0, keepdims=True)
            ds_ref[...] = ds_next * ebl + _dot(dout, qt, 0, 0)
            db = dqt * qt_f - dkt * kt_f - dkh * kh_f
            db_last = ebl * d_ebl + jnp.sum(dkh * kh_f, axis=0, keepdims=True)
            dq_ref[sl, :] = (dqt * eb * GLA_QSCALE).astype(dq_ref.dtype)
            dk_ref[sl, :] = (dkt * enb + dkh * ebl_b).astype(dk_ref.dtype)
            dv_ref[sl, :] = dv.astype(dv_ref.dtype)
            dla_ref[sl, :] = _tri_sum(triu_b, db) + db_last

    in_specs, blk = _gla_specs(t, rows, lay, True)
    in_specs += [pl.BlockSpec((rows, GLA_DV), lambda h, i: (blk(i), h)),
                 pl.BlockSpec((1, nc, GLA_DV, GLA_DK), lambda h, i: (h, blk(i), 0, 0))]
    dk_spec = pl.BlockSpec((rows, GLA_DK), lambda h, i: (blk(i), h))
    return pl.pallas_call(
        body, name="gla_bwd", grid=(GLA_HEADS, nb), in_specs=in_specs,
        out_specs=[dk_spec, dk_spec, pl.BlockSpec((rows, GLA_DV), lambda h, i: (blk(i), h)), dk_spec],
        out_shape=[jax.ShapeDtypeStruct((t, GLA_HEADS * GLA_DK), BF16),
                   jax.ShapeDtypeStruct((t, GLA_HEADS * GLA_DK), BF16),
                   jax.ShapeDtypeStruct((t, GLA_HEADS * GLA_DV), BF16),
                   jax.ShapeDtypeStruct((t, GLA_HEADS * GLA_DK), F32)],
        scratch_shapes=[pltpu.VMEM((GLA_DV, GLA_DK), F32)],
        compiler_params=_cparams(("parallel", "arbitrary")),
    )(p, p, p, la, do, states)


def _diag_mask(rows, cols, row0):
    row = row0 + lax.broadcasted_iota(jnp.int32, (rows, cols), 0)
    col = lax.broadcasted_iota(jnp.int32, (rows, cols), 1)
    return col <= row


QK_SPARE = MLA_NOPE + MLA_ROPE
N_SPARE = 3


def _with_spare(x, col, lane0):
    lane = lax.broadcasted_iota(jnp.int32, x.shape, 1)
    for n, term in enumerate(_split3(col)):
        x = jnp.where(lane == lane0 + n, term, x)
    return x


def _spare_ones(shape, lane0):
    lane = lax.broadcasted_iota(jnp.int32, shape, 1)
    return ((lane >= lane0) & (lane < lane0 + N_SPARE)).astype(F32)


def _flash_tiles(t):
    tq = min(1024, t)
    halves = 2 if tq % 32 == 0 else 1
    return tq, t // tq, halves, tq // halves


def _flash_fwd(q, k, vx, ride=None):
    t = q.shape[0]
    tq, nq, halves, hr = _flash_tiles(t)
    dqk, dv = MLA_QK_PAD, MLA_V

    def body(q_ref, k_ref, v_ref, o_ref, qa_ref, m_ref, acc_ref):
        i = pl.program_id(1)
        m_ref[...] = jnp.full_like(m_ref, NEG)
        acc_ref[...] = jnp.zeros_like(acc_ref)

        def scores(j):
            kb = k_ref[pl.ds(pl.multiple_of(j * tq, tq), tq), :]
            return tuple(_dot(q_ref[pl.ds(hh * hr, hr), :], kb, 1, 1) for hh in range(halves))

        def consume(j, s_all, masked):
            vb = v_ref[pl.ds(pl.multiple_of(j * tq, tq), tq), :]
            for hh in range(halves):
                rs = pl.ds(hh * hr, hr)
                s = s_all[hh]
                if masked:
                    s = jnp.where(_diag_mask(hr, tq, hh * hr), s, NEG)
                m_old = m_ref[rs, :]
                m_new = jnp.maximum(m_old, jnp.max(s, axis=1, keepdims=True))
                pr = jnp.exp2(s - m_new)
                acc_ref[rs, :] = jnp.exp2(m_old - m_new) * acc_ref[rs, :] + _dot(pr.astype(BF16), vb)
                m_ref[rs, :] = m_new

        def loop_body(j, s_cur):
            s_next = scores(j + 1)
            consume(j, s_cur, False)
            return s_next

        consume(i, lax.fori_loop(0, i, loop_body, scores(0)), True)
        acc = acc_ref[...]
        l = acc[:, dv:dv + 1]
        o_ref[...] = acc[:, :dv] / l
        qa_ref[...] = _with_spare(q_ref[...], -(m_ref[...] + jnp.log(l) * LOG2E), QK_SPARE)

    outs, rode = _call(
        body, name="mla_flash_fwd", grid=(MLA_HEADS, nq),
        in_specs=[pl.BlockSpec((tq, dqk), lambda h, i: (i, h)),
                  pl.BlockSpec((t, dqk), lambda h, i: (0, h)),
                  pl.BlockSpec((t, 2 * dv), lambda h, i: (0, h))],
        out_specs=[pl.BlockSpec((tq, dv), lambda h, i: (i, h)),
                   pl.BlockSpec((tq, dqk), lambda h, i: (i, h))],
        out_shape=[jax.ShapeDtypeStruct((t, MLA_HEADS * dv), F32),
                   jax.ShapeDtypeStruct((t, MLA_HEADS * dqk), BF16)],
        scratch_shapes=[pltpu.VMEM((tq, 1), F32), pltpu.VMEM((tq, 2 * dv), F32)],
        sem=("parallel", "arbitrary"), args=(q, k, vx), ride=ride)
    return tuple(outs) if ride is None else (tuple(outs), rode)


def _flash_bwd(qa, k, vx, doa, ride=None):
    t = qa.shape[0]
    tq, nq, halves, hr = _flash_tiles(t)
    dqk, dv = MLA_QK_PAD, MLA_V

    def body(k_ref, v_ref, q_ref, do_ref, dq_ref, dk_ref, dv_ref, dq_acc, dk_acc, dv_acc):
        j = pl.program_id(1)

        @pl.when(j == 0)
        def _():
            dq_acc[...] = jnp.zeros_like(dq_acc)

        kb = k_ref[...]
        vb = v_ref[...]
        dk_acc[...] = jnp.zeros_like(dk_acc)
        dv_acc[...] = jnp.zeros_like(dv_acc)

        def step(i, masked):
            for hh in range(halves):
                rs = pl.ds(pl.multiple_of(i * tq + hh * hr, hr), hr)
                qb = q_ref[rs, :]
                dout = do_ref[rs, :]
                nc = (hh + 1) * hr if masked else tq
                s = _dot(qb, kb[:nc], 1, 1)
                if masked:
                    s = jnp.where(_diag_mask(hr, nc, hh * hr), s, NEG)
                pr = jnp.exp2(s)
                ds = (pr * _dot(dout, vb[:nc], 1, 1)).astype(BF16)
                dv_acc[pl.ds(0, nc), :] += _dot(pr.astype(BF16), dout, 0, 0)
                dk_acc[pl.ds(0, nc), :] += _dot(ds, qb, 0, 0)
                dq_acc[rs, :] += _dot(ds, kb[:nc])

        def loop_body(i, carry):
            step(i, False)
            return carry

        step(j, True)
        lax.fori_loop(j + 1, nq, loop_body, 0)
        dk_ref[...] = (dk_acc[...] * LN2).astype(dk_ref.dtype)
        dv_ref[...] = dv_acc[:, :dv].astype(dv_ref.dtype)

        @pl.when(j == nq - 1)
        def _():
            dq_ref[...] = (dq_acc[...] * ATT_SCALE).astype(dq_ref.dtype)

    outs, rode = _call(
        body, name="mla_flash_bwd", grid=(MLA_HEADS, nq),
        in_specs=[pl.BlockSpec((tq, dqk), lambda h, j: (j, h)),
                  pl.BlockSpec((tq, 2 * dv), lambda h, j: (j, h)),
                  pl.BlockSpec((t, dqk), lambda h, j: (0, h)),
                  pl.BlockSpec((t, 2 * dv), lambda h, j: (0, h))],
        out_specs=[pl.BlockSpec((t, dqk), lambda h, j: (0, h)),
                   pl.BlockSpec((tq, dqk), lambda h, j: (j, h)),
                   pl.BlockSpec((tq, dv), lambda h, j: (j, h))],
        out_shape=[jax.ShapeDtypeStruct((t, MLA_HEADS * dqk), BF16),
                   jax.ShapeDtypeStruct((t, MLA_HEADS * dqk), BF16),
                   jax.ShapeDtypeStruct((t, MLA_HEADS * dv), BF16)],
        scratch_shapes=[pltpu.VMEM((t, dqk), F32), pltpu.VMEM((tq, dqk), F32), pltpu.VMEM((tq, 2 * dv), F32)],
        sem=("parallel", "arbitrary"), args=(k, vx, qa, doa), ride=ride, vmem_limit=FLASH_BWD_VMEM)
    return tuple(outs) if ride is None else (tuple(outs), rode)


class _NoRides:
    def ride(self, stage, grads):
        return None

    def done(self, stage, rode, w):
        pass


def _local_step(x, target, tab, mod8, w, rides=None):
    t, d = x.shape
    rides = rides or _NoRides()
    big = {}

    def riding(stage, fn):
        r = rides.ride(stage, big)
        res = fn(r)
        if r is None:
            return res
        rides.done(stage, res[1], w)
        return res[0]
    _, pw, _, lay, _ = _in_layout(d)
    ffn = ((8 * d // 3 + 255) // 256) * 256

    def blk(arr, name):
        return (arr, pw[name], lay[name] // pw[name])

    def full(arr):
        return (arr, arr.shape[1], 0)

    g1, g2, g3 = w["norm_mix_g"], w["norm_ffn_g"], w["final_norm_g"]

    def f_ln1(xv, mod, g):
        return (xv * _rstd(xv) * g) * (1.0 + mod[1:2]) + mod[0:1]

    (h,) = _rowwise(f_ln1, [full(x)], [mod8, g1], [(d, BF16)], tile=256, name="ln1_modulate")
    p = riding("in_proj", lambda r: _mm(h, w["w_in"], name="mm_in_proj", ride=r))

    def f_gk(pgk, gkw, gkb):
        z = _dot(pgk.astype(BF16), gkw.astype(BF16)) + gkb
        return (jnp.minimum(z, 0.0) - jnp.log(1.0 + jnp.exp(-jnp.abs(z)))) / GLA_GATE_NORMALIZER

    (la,) = _rowwise(f_gk, [blk(p, "gk")], [w["gla_gk_w"], w["gla_gk_b"]], [(GLA_HEADS * GLA_DK, F32)],
                     tile=512, name="gla_gate")
    o_gla, states = _gla_fwd(p, la, lay)

    def f_gla_out(ov, pg, g):
        parts = []
        for hh in range(GLA_HEADS):
            oh = ov[:, hh * GLA_DV:(hh + 1) * GLA_DV]
            ph = pg[:, hh * GLA_DV:(hh + 1) * GLA_DV]
            parts.append(oh * _rstd(oh) * g * (ph * _sigmoid(ph)))
        return jnp.concatenate(parts, axis=1)

    (o_n,) = _rowwise(f_gla_out, [full(o_gla), blk(p, "g")], [w["gla_onorm_g"]], [(d, BF16)], tile=256,
                      name="gla_out_norm")
    y_gla = _mm(o_n, w["gla_wo"], out_dtype=BF16, name="mm_gla_wo")

    def f_mla_prep(cq, ckv, kr, tb, gq, gkv):
        return cq * _rstd(cq) * gq, ckv * _rstd(ckv) * gkv, _rope(kr, tb, 1.0)

    cqn, ckvn, krr = _rowwise(f_mla_prep, [blk(p, "cq"), blk(p, "ckv"), blk(p, "kr"), full(tab)],
                              [w["mla_q_norm_g"], w["mla_kv_norm_g"]],
                              [(pw["cq"], BF16), (pw["ckv"], BF16), (LANE, F32)], tile=512, name="mla_prep")
    qlat = _mm(cqn, w["mla_wuq"], out_dtype=BF16, name="mm_mla_wuq")
    kvl = _mm(ckvn, w["mla_wukv"], out_dtype=BF16, name="mm_mla_wukv")
    hv = MLA_HEADS * MLA_V

    def f_qkv(ql, kn, vv, kr, tb):
        qs, ks, vx = [], [], []
        kr1 = kr + _spare_ones(kr.shape, MLA_ROPE)
        ones = _spare_ones(kr.shape, 0)
        for hh in range(MLA_HEADS):
            o0 = hh * MLA_QK_PAD
            qs += [ql[:, o0:o0 + LANE], _rope(ql[:, o0 + LANE:o0 + 2 * LANE], tb, 1.0)]
            ks += [kn[:, hh * LANE:(hh + 1) * LANE], kr1]
            vx += [vv[:, hh * MLA_V:(hh + 1) * MLA_V], ones]
        return (jnp.concatenate(qs, axis=1) * (ATT_SCALE * LOG2E), jnp.concatenate(ks, axis=1),
                jnp.concatenate(vx, axis=1))

    qa, ka, vxa = _rowwise(f_qkv, [full(qlat), (kvl, hv, 0), (kvl, hv, 1), full(krr), full(tab)], [],
                           [(MLA_HEADS * MLA_QK_PAD, BF16), (MLA_HEADS * MLA_QK_PAD, BF16), (2 * hv, BF16)],
                           tile=256, name="mla_qkv_build")
    o_mla, qa_lse = riding("flash_fwd", lambda r: _flash_fwd(qa, ka, vxa, ride=r))
    y_mla = _mm(o_mla, w["mla_wo"], out_dtype=BF16, name="mm_mla_wo")

    def f_merge(yg, ym, ga, gb):
        return _sigmoid(ga) * yg + _sigmoid(gb) * ym

    (merged,) = _rowwise(f_merge, [full(y_gla), full(y_mla), blk(p, "ga"), blk(p, "gb")], [], [(d, BF16)],
                         tile=256, name="merge")
    mix = _mm(merged, w["w_out"], name="mm_w_out")

    def f_res_ln2(xv, mx, mod, g):
        x2v = xv + mod[2:3] * mx
        return x2v, (x2v * _rstd(x2v) * g) * (1.0 + mod[4:5]) + mod[3:4]

    x2, h2 = _rowwise(f_res_ln2, [full(x), full(mix)], [mod8, g2], [(d, F32), (d, BF16)], tile=256,
                      name="res_ln2_modulate")
    gu = _mm(h2, w["ffn_w_in"], out_dtype=BF16, name="mm_ffn_in")

    def f_swiglu(gv, uv):
        return gv * _sigmoid(gv) * uv

    (act,) = _rowwise(f_swiglu, [(gu, ffn, 0), (gu, ffn, 1)], [], [(ffn, BF16)], tile=128, name="swiglu")
    f_out = _mm(act, w["ffn_w_down"], name="mm_ffn_down")

    def f_head(x2v, fv, tg, mod, g):
        x3 = x2v + mod[5:6] * fv
        r = _rstd(x3)
        xh = x3 * r
        e = xh * g - tg
        loss_rows = 0.5 * jnp.mean(e * e, axis=-1, keepdims=True)
        dy = e * (1.0 / d)
        dx3 = _rms_bwd(dy * g, xh, r)
        loss = jnp.broadcast_to(jnp.sum(loss_rows, axis=0, keepdims=True), (1, LANE))
        return (dx3, dx3 * mod[5:6], loss, jnp.sum(dy * xh, axis=0, keepdims=True),
                jnp.sum(dx3 * fv, axis=0, keepdims=True))

    dx3, df, loss_v, dg3, dgate_f = _rowwise(f_head, [full(x2), full(f_out), full(target)], [mod8, g3],
                                             [(d, F32), (d, BF16)], [LANE, d, d], tile=256, name="loss_head")
    da = _mm(df, w["ffn_w_down"], tb=True, out_dtype=BF16, name="mm_ffn_down_dx")
    big["ffn_w_down"] = _mm(act, df, ta=True, out_dtype=BF16, name="mm_ffn_down_dw")

    def f_swiglu_bwd(gv, uv, dav):
        sg = _sigmoid(gv)
        return jnp.concatenate([dav * uv * (sg * (1.0 + gv * (1.0 - sg))), dav * (gv * sg)], axis=1)

    (dgu,) = _rowwise(f_swiglu_bwd, [(gu, ffn, 0), (gu, ffn, 1), full(da)], [], [(2 * ffn, BF16)], tile=128,
                      name="swiglu_bwd")
    dh2 = _mm(dgu, w["ffn_w_in"], tb=True, name="mm_ffn_in_dx")
    big["ffn_w_in"] = _mm(h2, dgu, ta=True, out_dtype=BF16, slabs=True, name="mm_ffn_in_dw")

    def f_ln2_bwd(x2v, dh, dx3v, mx, mod, g):
        r = _rstd(x2v)
        xh = x2v * r
        dn = dh * (1.0 + mod[4:5])
        dx2 = dx3v + _rms_bwd(dn * g, xh, r)
        return (dx2, dx2 * mod[2:3],
                jnp.sum(dh * (xh * g), axis=0, keepdims=True), jnp.sum(dh, axis=0, keepdims=True),
                jnp.sum(dn * xh, axis=0, keepdims=True), jnp.sum(dx2 * mx, axis=0, keepdims=True))

    dx2, dmix, dscale_f, dshift_f, dg2, dgate_m = _rowwise(
        f_ln2_bwd, [full(x2), full(dh2), full(dx3), full(mix)], [mod8, g2], [(d, F32), (d, BF16)],
        [d, d, d, d], tile=256, name="ln2_bwd")
    dmerged = _mm(dmix, w["w_out"], tb=True, out_dtype=BF16, name="mm_w_out_dx")
    big["w_out"] = _mm(merged, dmix, ta=True, out_dtype=BF16, name="mm_w_out_dw")

    def f_merge_bwd(dm, yg, ym, ga, gb):
        sa, sb = _sigmoid(ga), _sigmoid(gb)
        return dm * sa, dm * sb, dm * yg * sa * (1.0 - sa), dm * ym * sb * (1.0 - sb)

    dy_gla, dy_mla, dp_ga, dp_gb = _rowwise(
        f_merge_bwd, [full(dmerged), full(y_gla), full(y_mla), blk(p, "ga"), blk(p, "gb")], [],
        [(d, BF16)] * 4, tile=256, name="merge_bwd")
    do_n = _mm(dy_gla, w["gla_wo"], tb=True, name="mm_gla_wo_dx")
    big["gla_wo"] = _mm(o_n, dy_gla, ta=True, out_dtype=BF16, name="mm_gla_wo_dw")
    do_m = _mm(dy_mla, w["mla_wo"], tb=True, out_dtype=BF16, name="mm_mla_wo_dx")
    big["mla_wo"] = _mm(o_mla, dy_mla, ta=True, out_dtype=BF16, name="mm_mla_wo_dw")

    def f_gla_out_bwd(don, ov, pg, g):
        dos, dpgs = [], []
        dg = jnp.zeros((1, GLA_DV), F32)
        for hh in range(GLA_HEADS):
            sl = slice(hh * GLA_DV, (hh + 1) * GLA_DV)
            oh, ph, dn = ov[:, sl], pg[:, sl], don[:, sl]
            r = _rstd(oh)
            xh = oh * r
            sg = _sigmoid(ph)
            dpre = dn * (ph * sg)
            dg = dg + jnp.sum(dpre * xh, axis=0, keepdims=True)
            dos.append(_rms_bwd(dpre * g, xh, r))
            dpgs.append(dn * (xh * g) * (sg * (1.0 + ph * (1.0 - sg))))
        return jnp.concatenate(dos, axis=1), jnp.concatenate(dpgs, axis=1), dg

    do_gla, dp_g, dg_on = _rowwise(f_gla_out_bwd, [full(do_n), full(o_gla), blk(p, "g")], [w["gla_onorm_g"]],
                                   [(d, F32), (d, BF16)], [GLA_DV], tile=256, name="gla_out_norm_bwd")
    dp_q, dp_k, dp_v, dla = _gla_bwd(p, la, do_gla, states, lay)

    def f_gk_bwd(dlav, pgk, gkw, gkb):
        z = _dot(pgk.astype(BF16), gkw.astype(BF16)) + gkb
        dz = dlav * (1.0 / GLA_GATE_NORMALIZER) * _sigmoid(-z)
        return dz, _dot(dz.astype(BF16), gkw.astype(BF16), 1, 1), jnp.sum(dz, axis=0, keepdims=True)

    dz, dp_gk, dgk_b = _rowwise(f_gk_bwd, [full(dla), blk(p, "gk")], [w["gla_gk_w"], w["gla_gk_b"]],
                                [(GLA_HEADS * GLA_DK, BF16), (LANE, BF16)], [GLA_HEADS * GLA_DK], tile=512,
                                name="gla_gate_bwd")
    p_gk = lax.slice_in_dim(p, lay["gk"], lay["gk"] + LANE, axis=1)
    big["gla_gk_w"] = _mm(p_gk, dz, ta=True, name="mm_gla_gk_dw")[:GLA_GATE_RANK]

    def f_do_aug(dom, om):
        parts = []
        for hh in range(MLA_HEADS):
            dh_ = dom[:, hh * MLA_V:(hh + 1) * MLA_V]
            delta = jnp.sum(dh_ * om[:, hh * MLA_V:(hh + 1) * MLA_V], axis=1, keepdims=True)
            parts += [dh_.astype(BF16), _with_spare(jnp.zeros(dh_.shape, BF16), -delta, 0)]
        return jnp.concatenate(parts, axis=1)

    (doa,) = _rowwise(f_do_aug, [full(do_m), full(o_mla)], [], [(2 * hv, BF16)], tile=256, name="mla_do_delta")
    dqa, dka, dva = riding("flash_bwd", lambda r: _flash_bwd(qa_lse, ka, vxa, doa, ride=r))

    def f_qkv_bwd(dq, dk, dvv, tb):
        dqs, dkn = [], []
        dkr = jnp.zeros((dq.shape[0], LANE), F32)
        for hh in range(MLA_HEADS):
            o0 = hh * MLA_QK_PAD
            dqs += [dq[:, o0:o0 + LANE], _rope(dq[:, o0 + LANE:o0 + 2 * LANE], tb, -1.0)]
            dkn.append(dk[:, o0:o0 + LANE])
            dkr = dkr + dk[:, o0 + LANE:o0 + 2 * LANE]
        return jnp.concatenate(dqs, axis=1), jnp.concatenate(dkn + [dvv], axis=1), dkr

    dqlat, dkvl, dkrr = _rowwise(f_qkv_bwd, [full(dqa), full(dka), full(dva), full(tab)], [],
                                 [(MLA_HEADS * MLA_QK_PAD, BF16), (2 * hv, BF16), (LANE, F32)], tile=256,
                                 name="mla_qkv_build_bwd")
    dcqn = _mm(dqlat, w["mla_wuq"], tb=True, name="mm_mla_wuq_dx")
    big["mla_wuq"] = _mm(cqn, dqlat, ta=True, out_dtype=BF16, name="mm_mla_wuq_dw")
    dckvn = _mm(dkvl, w["mla_wukv"], tb=True, name="mm_mla_wukv_dx")
    big["mla_wukv"] = _mm(ckvn, dkvl, ta=True, out_dtype=BF16, name="mm_mla_wukv_dw")

    def f_mla_prep_bwd(dq, dkv, dkr, cq, ckv, tb, gq, gkv):
        rq, rk = _rstd(cq), _rstd(ckv)
        xq, xk = cq * rq, ckv * rk
        return (_rms_bwd(dq * gq, xq, rq), _rms_bwd(dkv * gkv, xk, rk), _rope(dkr, tb, -1.0),
                jnp.sum(dq * xq, axis=0, keepdims=True), jnp.sum(dkv * xk, axis=0, keepdims=True))

    dp_cq, dp_ckv, dp_kr, dg_q, dg_kv = _rowwise(
        f_mla_prep_bwd, [full(dcqn), full(dckvn), full(dkrr), blk(p, "cq"), blk(p, "ckv"), full(tab)],
        [w["mla_q_norm_g"], w["mla_kv_norm_g"]], [(pw["cq"], BF16), (pw["ckv"], BF16), (LANE, BF16)],
        [pw["cq"], pw["ckv"]], tile=512, name="mla_prep_bwd")

    pieces = dict(v=dp_v, g=dp_g, ga=dp_ga, gb=dp_gb, q=dp_q, k=dp_k, cq=dp_cq, ckv=dp_ckv, gk=dp_gk, kr=dp_kr)
    dp = jnp.concatenate([pieces[n] for n in MY_ORDER], axis=1)
    big["w_in"] = riding("in_proj_dw", lambda r: _mm(h, dp, ta=True, out_dtype=BF16, name="mm_in_proj_dw", ride=r))
    dh = riding("in_proj_dx", lambda r: _mm(dp, w["w_in"], tb=True, name="mm_in_proj_dx", ride=r))

    def f_ln1_bwd(xv, dhv, dx2v, mod, g):
        r = _rstd(xv)
        xh = xv * r
        dn = dhv * (1.0 + mod[1:2])
        return (dx2v + _rms_bwd(dn * g, xh, r),
                jnp.sum(dhv * (xh * g), axis=0, keepdims=True), jnp.sum(dhv, axis=0, keepdims=True),
                jnp.sum(dn * xh, axis=0, keepdims=True))

    grad_x, dscale_m, dshift_m, dg1 = _rowwise(f_ln1_bwd, [full(x), full(dh), full(dx2)], [mod8, g1],
                                               [(d, F32)], [d, d, d], tile=256, name="ln1_bwd")

    dmod = jnp.concatenate([dshift_m, dscale_m, dgate_m, dshift_f, dscale_f, dgate_f], axis=1)
    small = dict(ada_b=dmod, norm_mix_g=dg1, gla_gk_b=dgk_b, gla_onorm_g=dg_on, mla_q_norm_g=dg_q,
                 mla_kv_norm_g=dg_kv, norm_ffn_g=dg2, final_norm_g=dg3)
    return loss_v[0, 0], grad_x, big, small


N_PEER = N_DEV - 1


def _exchange_copies(ins, outs, sems, scatter):
    send_sems, recv_sems, local_sems = sems
    x, y, c = lax.axis_index("x"), lax.axis_index("y"), lax.axis_index("c")
    me = 4 * x + 2 * y + c
    peers = []
    for rel in range(1, N_DEV):
        px = 1 - x if rel & 4 else x
        py = 1 - y if rel & 2 else y
        pc = 1 - c if rel & 1 else c
        peers.append(((px, py, pc), 4 * px + 2 * py + pc))

    def remote(a, k, src_slot, dst_slot):
        src = ins[a].at[src_slot] if scatter else ins[a]
        return pltpu.make_async_remote_copy(
            src_ref=src, dst_ref=outs[a].at[dst_slot], send_sem=send_sems.at[a * N_PEER + k],
            recv_sem=recv_sems.at[a * N_PEER + k], device_id=peers[k][0], device_id_type=pl.DeviceIdType.MESH)

    local, sends, recvs = [], [], []
    for a in range(len(ins)):
        src = ins[a].at[me] if scatter else ins[a]
        local.append(pltpu.make_async_copy(src, outs[a].at[me], local_sems.at[a]))
        for k in range(N_PEER):
            sends.append(remote(a, k, peers[k][1], me))
            recvs.append(remote(a, k, peers[k][1], peers[k][1]))
    return local, sends, recvs


def _exchange_start(ins, outs, sems, scatter):
    local, sends, _ = _exchange_copies(ins, outs, sems, scatter)
    for cp in local + sends:
        cp.start()


def _exchange_wait(ins, outs, sems, scatter):
    local, sends, recvs = _exchange_copies(ins, outs, sems, scatter)
    for cp in recvs:
        cp.wait_recv()
    for cp in sends:
        cp.wait_send()
    for cp in local:
        cp.wait()


def _exchange_shapes(arrs, scatter):
    n = len(arrs)
    out_shape = [jax.ShapeDtypeStruct(a.shape if scatter else (N_DEV,) + a.shape, a.dtype) for a in arrs]
    sems = [pltpu.SemaphoreType.DMA((n * N_PEER,)), pltpu.SemaphoreType.DMA((n * N_PEER,)),
            pltpu.SemaphoreType.DMA((n,))]
    return out_shape, sems


def _exchange(arrs, *, scatter, name):
    n = len(arrs)

    def body(*refs):
        ins, outs, sems = refs[:n], refs[n:2 * n], refs[2 * n:]
        _exchange_start(ins, outs, sems, scatter)
        _exchange_wait(ins, outs, sems, scatter)

    hbm = pl.BlockSpec(memory_space=pltpu.HBM)
    out_shape, sems = _exchange_shapes(arrs, scatter)
    return pl.pallas_call(body, name=name, in_specs=[hbm] * n, out_specs=[hbm] * n, out_shape=out_shape,
                          scratch_shapes=sems)(*arrs)


def _call(body, *, name, grid, in_specs, out_specs, out_shape, scratch_shapes, sem, args, ride=None,
          vmem_limit=VMEM_LIMIT):
    if ride is None:
        res = pl.pallas_call(body, name=name, grid=grid, in_specs=in_specs, out_specs=out_specs, out_shape=out_shape,
                             scratch_shapes=scratch_shapes, compiler_params=_cparams(sem, vmem_limit))(*args)
        return res, None
    arrs, scatter = ride
    n, n_in, n_out, n_scr = len(arrs), len(in_specs), len(out_specs), len(scratch_shapes)
    x_shape, x_sems = _exchange_shapes(arrs, scatter)

    def hosted(*refs):
        c_in, x_in = refs[:n_in], refs[n_in:n_in + n]
        c_out, x_out = refs[n_in + n:n_in + n + n_out], refs[n_in + n + n_out:n_in + 2 * n + n_out]
        scr = refs[n_in + 2 * n + n_out:]
        c_scr, sems = scr[:n_scr], scr[n_scr:]
        first = functools.reduce(jnp.logical_and, [pl.program_id(a) == 0 for a in range(len(grid))])
        last = functools.reduce(jnp.logical_and, [pl.program_id(a) == grid[a] - 1 for a in range(len(grid))])

        @pl.when(first)
        def _():
            _exchange_start(x_in, x_out, sems, scatter)

        body(*c_in, *c_out, *c_scr)

        @pl.when(last)
        def _():
            _exchange_wait(x_in, x_out, sems, scatter)

    hbm = pl.BlockSpec(memory_space=pltpu.HBM)
    res = pl.pallas_call(
        hosted, name=name, grid=grid, in_specs=list(in_specs) + [hbm] * n, out_specs=list(out_specs) + [hbm] * n,
        out_shape=list(out_shape) + x_shape, scratch_shapes=list(scratch_shapes) + x_sems,
        compiler_params=_cparams(("arbitrary",) * len(grid), vmem_limit))(*args, *arrs)
    return res[:n_out], res[n_out:]


def _adamw_math(w, g, m, v):
    m_new = ADAM_B1 * m + (1.0 - ADAM_B1) * g
    v_new = ADAM_B2 * v + (1.0 - ADAM_B2) * (g * g)
    m_hat = m_new / (1.0 - ADAM_B1 ** ADAM_STEP)
    v_hat = v_new / (1.0 - ADAM_B2 ** ADAM_STEP)
    delta = -ADAM_LR * (m_hat / (jnp.sqrt(v_hat) + ADAM_EPS) + ADAM_WD * w)
    return delta, m_new, v_new


def _adamw(w, g, m, v, *, name):
    r, c = w.shape
    slots = g.ndim == 3
    tr = r
    for cand in (128, 64, 32, 16):
        if r % cand == 0 and r > cand:
            tr = cand
            break

    def body(w_ref, g_ref, m_ref, v_ref, go_ref, d_ref, mo_ref, vo_ref):
        if slots:
            gv = g_ref[0].astype(F32)
            for s in range(1, N_DEV):
                gv = gv + g_ref[s].astype(F32)
        else:
            gv = g_ref[...]
        delta, m_new, v_new = _adamw_math(w_ref[...], gv, m_ref[...], v_ref[...])
        go_ref[...] = gv
        d_ref[...] = delta
        mo_ref[...] = m_new
        vo_ref[...] = v_new

    spec = pl.BlockSpec((tr, c), lambda i: (i, 0))
    g_spec = pl.BlockSpec((N_DEV, tr, c), lambda i: (0, i, 0)) if slots else spec
    return pl.pallas_call(
        body, name=name, grid=(r // tr,), in_specs=[spec, g_spec, spec, spec], out_specs=[spec] * 4,
        out_shape=[jax.ShapeDtypeStruct((r, c), F32)] * 4, compiler_params=_cparams(("parallel",)),
    )(w, g, m, v)


def _unshard_cols(g):
    return jnp.transpose(g, (1, 0, 2)).reshape(g.shape[1], -1)

def _shard_cols(full):
    r = full.shape[0]
    return jnp.transpose(full.reshape(r, N_DEV, -1), (1, 0, 2))


def _w_in_to_mine(w_ref_layout, d):
    wd, pw, ref_off, _, _ = _in_layout(d)
    cols = []
    for n in MY_ORDER:
        piece = lax.slice_in_dim(w_ref_layout, ref_off[n], ref_off[n] + wd[n], axis=1)
        if pw[n] != wd[n]:
            piece = jnp.pad(piece, ((0, 0), (0, pw[n] - wd[n])))
        cols.append(piece)
    return jnp.concatenate(cols, axis=1)


def _w_in_from_mine(g_mine, d):
    wd, _, _, my_off, _ = _in_layout(d)
    return jnp.concatenate([lax.slice_in_dim(g_mine, my_off[n], my_off[n] + wd[n], axis=1) for n in IN_NAMES],
                           axis=1)


def _wuq_to_mine(wq):
    r = wq.shape[0]
    w3 = wq.reshape(r, MLA_HEADS, MLA_NOPE + MLA_ROPE)
    w3 = jnp.pad(w3, ((0, 0), (0, 0), (0, MLA_QK_PAD - MLA_NOPE - MLA_ROPE)))
    return w3.reshape(r, MLA_HEADS * MLA_QK_PAD)


def _wuq_from_mine(g):
    r = g.shape[0]
    return g.reshape(r, MLA_HEADS, MLA_QK_PAD)[:, :, :MLA_NOPE + MLA_ROPE].reshape(r, -1)


def _wukv_to_mine(wkv):
    r = wkv.shape[0]
    w3 = wkv.reshape(r, MLA_HEADS, MLA_NOPE + MLA_V)
    return jnp.concatenate([w3[:, :, :MLA_NOPE].reshape(r, -1), w3[:, :, MLA_NOPE:].reshape(r, -1)], axis=1)


def _wukv_from_mine(g):
    r = g.shape[0]
    kn = g[:, :MLA_HEADS * MLA_NOPE].reshape(r, MLA_HEADS, MLA_NOPE)
    vv = g[:, MLA_HEADS * MLA_NOPE:].reshape(r, MLA_HEADS, MLA_V)
    return jnp.concatenate([kn, vv], axis=2).reshape(r, -1)


COL_SHARDED = ("w_in", "gla_gk_w", "mla_wuq", "mla_wukv", "ffn_w_in")


def _gathered_to_mine(name, g, d):
    full = _unshard_cols(g) if name in COL_SHARDED else g.reshape(-1, g.shape[-1])
    if name == "w_in":
        return _w_in_to_mine(full, d)
    if name == "gla_gk_w":
        return jnp.pad(full, ((0, LANE - GLA_GATE_RANK), (0, 0)))
    if name == "mla_wuq":
        return _wuq_to_mine(full)
    if name == "mla_wukv":
        return _wukv_to_mine(full)
    return full


def _grad_to_slabs(name, g, d):
    if g.ndim == 3:
        return g.astype(BF16)
    if name == "w_in":
        g = _w_in_from_mine(g, d)
    elif name == "mla_wuq":
        g = _wuq_from_mine(g)
    elif name == "mla_wukv":
        g = _wukv_from_mine(g)
    s = _shard_cols(g) if name in COL_SHARDED else g.reshape(N_DEV, -1, g.shape[-1])
    return s.astype(BF16)


class _Rides:
    GATHER = {"in_proj": ("gla_wo", "mla_wuq", "mla_wukv", "mla_wo", "w_out"),
              "flash_fwd": ("ffn_w_in", "ffn_w_down")}
    SCATTER = {"flash_bwd": ("ffn_w_in", "ffn_w_down", "w_out", "gla_wo", "mla_wo", "gla_gk_w"),
               "in_proj_dw": ("mla_wuq", "mla_wukv"),
               "in_proj_dx": ("w_in",)}

    def __init__(self, send, d):
        self.send, self.d, self.recv = send, d, {}

    def ride(self, stage, grads):
        if stage in self.GATHER:
            return [self.send[n] for n in self.GATHER[stage]], False
        return [_grad_to_slabs(n, grads[n], self.d) for n in self.SCATTER[stage]], True

    def done(self, stage, rode, w):
        if stage in self.GATHER:
            for n, g in zip(self.GATHER[stage], rode):
                w[n] = _gathered_to_mine(n, g, self.d)
        else:
            self.recv.update(zip(self.SCATTER[stage], rode))


def kernel(x, c, positions, ada_w, ada_b, norm_mix_g, w_in, gla_gk_w, gla_gk_b, gla_onorm_g, gla_wo, mla_q_norm_g, mla_wuq, mla_kv_norm_g, mla_wukv, mla_wo, w_out, norm_ffn_g, ffn_w_in, ffn_w_down, final_norm_g, loss_target, m_ada_w, m_ada_b, m_norm_mix_g, m_w_in, m_gla_gk_w, m_gla_gk_b, m_gla_onorm_g, m_gla_wo, m_mla_q_norm_g, m_mla_wuq, m_mla_kv_norm_g, m_mla_wukv, m_mla_wo, m_w_out, m_norm_ffn_g, m_ffn_w_in, m_ffn_w_down, m_final_norm_g, v_ada_w, v_ada_b, v_norm_mix_g, v_w_in, v_gla_gk_w, v_gla_gk_b, v_gla_onorm_g, v_gla_wo, v_mla_q_norm_g, v_mla_wuq, v_mla_kv_norm_g, v_mla_wukv, v_mla_wo, v_w_out, v_norm_ffn_g, v_ffn_w_in, v_ffn_w_down, v_final_norm_g):
    wts = dict(ada_w=ada_w, ada_b=ada_b, norm_mix_g=norm_mix_g, w_in=w_in, gla_gk_w=gla_gk_w, gla_gk_b=gla_gk_b,
               gla_onorm_g=gla_onorm_g, gla_wo=gla_wo, mla_q_norm_g=mla_q_norm_g, mla_wuq=mla_wuq,
               mla_kv_norm_g=mla_kv_norm_g, mla_wukv=mla_wukv, mla_wo=mla_wo, w_out=w_out, norm_ffn_g=norm_ffn_g,
               ffn_w_in=ffn_w_in, ffn_w_down=ffn_w_down, final_norm_g=final_norm_g)
    mom_m = dict(zip(WEIGHTS, (m_ada_w, m_ada_b, m_norm_mix_g, m_w_in, m_gla_gk_w, m_gla_gk_b, m_gla_onorm_g,
                               m_gla_wo, m_mla_q_norm_g, m_mla_wuq, m_mla_kv_norm_g, m_mla_wukv, m_mla_wo, m_w_out,
                               m_norm_ffn_g, m_ffn_w_in, m_ffn_w_down, m_final_norm_g)))
    mom_v = dict(zip(WEIGHTS, (v_ada_w, v_ada_b, v_norm_mix_g, v_w_in, v_gla_gk_w, v_gla_gk_b, v_gla_onorm_g,
                               v_gla_wo, v_mla_q_norm_g, v_mla_wuq, v_mla_kv_norm_g, v_mla_wukv, v_mla_wo, v_w_out,
                               v_norm_ffn_g, v_ffn_w_in, v_ffn_w_down, v_final_norm_g)))
    seq, d = x.shape[1], x.shape[2]
    me = 4 * lax.axis_index("x") + 2 * lax.axis_index("y") + lax.axis_index("c")

    def two_d(a):
        return a.reshape(a.shape[-2], a.shape[-1]) if a.ndim >= 2 else a.reshape(1, -1)

    shard = {n: two_d(wts[n]) for n in BIG}
    send = {n: shard[n].astype(F32 if n == "gla_gk_w" else BF16) for n in BIG}
    got = _exchange([send["w_in"], send["gla_gk_w"], two_d(c)], scatter=False, name="comm_all_gather_first")
    c_all = got[2].reshape(N_DEV, d)
    w = dict(
        w_in=_gathered_to_mine("w_in", got[0], d), gla_gk_w=_gathered_to_mine("gla_gk_w", got[1], d),
        gla_gk_b=two_d(gla_gk_b), gla_onorm_g=two_d(gla_onorm_g), mla_q_norm_g=two_d(mla_q_norm_g),
        mla_kv_norm_g=two_d(mla_kv_norm_g), norm_mix_g=two_d(norm_mix_g), norm_ffn_g=two_d(norm_ffn_g),
        final_norm_g=two_d(final_norm_g))
    rides = _Rides(send, d)

    c_pad = jnp.pad(c_all, ((0, 16 - N_DEV), (0, 0)))
    (c_act,) = _rowwise(lambda cv: cv * _sigmoid(cv), [(c_pad, d, 0)], [], [(d, F32)], tile=16, name="silu_c")
    ada_w2 = two_d(ada_w)
    mod_part = _mm(c_act, ada_w2, name="mm_ada")[:N_DEV]
    (mod_all,) = _exchange([mod_part], scatter=False, name="comm_all_gather_mod")
    mod_mine = lax.dynamic_index_in_dim(mod_all, me, axis=1, keepdims=False).reshape(1, -1) + two_d(ada_b)
    mod8 = jnp.pad(mod_mine.reshape(6, d), ((0, 2), (0, 0)))

    inv_freq = ROPE_THETA ** (-jnp.arange(0, MLA_ROPE, 2, dtype=F32) / MLA_ROPE)
    ang = positions.reshape(seq, 1).astype(F32) * inv_freq[None, :]
    cos, sin, z32 = jnp.cos(ang), jnp.sin(ang), jnp.zeros((seq, 32), F32)
    tab = jnp.concatenate([cos, cos, z32, z32, -sin, z32, z32, z32, z32, sin, z32, z32], axis=1)
    loss_local, grad_x, _, small = _local_step(x.reshape(seq, d), loss_target.reshape(seq, d), tab, mod8, w, rides)

    recv = rides.recv
    pack = jnp.concatenate([small[n] for n in SMALL], axis=1)
    (pack_all,) = _exchange([pack], scatter=False, name="comm_all_gather_small")
    pack_all = pack_all.reshape(N_DEV, -1)

    res = {}
    for n in BIG:
        res[n] = _adamw(shard[n], recv[n], two_d(mom_m[n]), two_d(mom_v[n]), name="adamw_" + n)
    n_ada = ada_w2.shape[1]
    dmod_cols = lax.dynamic_slice_in_dim(pack_all[:, :6 * d], me * n_ada, n_ada, axis=1)

    def f_outer(cat, dm):
        acc = cat[:, 0:1] * dm[0:1]
        for b in range(1, N_DEV):
            acc = acc + cat[:, b:b + 1] * dm[b:b + 1]
        return acc

    (g_ada_w,) = _rowwise(f_outer, [(jnp.transpose(c_act[:N_DEV]), N_DEV, 0)], [dmod_cols], [(n_ada, F32)],
                          tile=256, name="ada_w_grad")
    res["ada_w"] = _adamw(ada_w2, g_ada_w, two_d(m_ada_w), two_d(v_ada_w), name="adamw_ada_w")
    w_small = jnp.concatenate([two_d(wts[n]) for n in SMALL], axis=1)
    m_small = jnp.concatenate([two_d(mom_m[n]) for n in SMALL], axis=1)
    v_small = jnp.concatenate([two_d(mom_v[n]) for n in SMALL], axis=1)
    small_res = _adamw(w_small, pack_all.reshape(N_DEV, 1, -1), m_small, v_small, name="adamw_small")
    off = 0
    for n in SMALL:
        width = wts[n].size
        res[n] = tuple(lax.slice_in_dim(a, off, off + width, axis=1) for a in small_res)
        off += width

    loss = lax.psum(loss_local, ("x", "y", "c"))
    outs = [loss, grad_x.reshape(x.shape)]
    for kind in range(4):
        outs += [res[n][kind].reshape(wts[n].shape) for n in WEIGHTS]
    return tuple(outs)
```

```python
import functools

import jax
import jax.numpy as jnp
from jax import lax
from jax.experimental import pallas as pl
from jax.experimental.pallas import tpu as pltpu

F32 = jnp.float32
BF16 = jnp.bfloat16

N_DEV = 8
GLA_HEADS = 4
GLA_DK = 256
GLA_DV = 512
GLA_GATE_RANK = 16
GLA_GATE_NORMALIZER = 16.0
GLA_CHUNK = 64
MLA_HEADS = 16
MLA_NOPE = 128
MLA_ROPE = 64
MLA_V = 128
MLA_QK_PAD = 256
ROPE_THETA = 10000.0
NORM_EPS = 1e-6
ATT_SCALE = (MLA_NOPE + MLA_ROPE) ** -0.5
GLA_QSCALE = GLA_DK ** -0.5

ADAM_LR = 0.001
ADAM_B1 = 0.9
ADAM_B2 = 0.999
ADAM_EPS = 1e-08
ADAM_WD = 0.01
ADAM_STEP = 10

LANE = 128
VMEM_LIMIT = 48 * 1024 * 1024
FLASH_BWD_VMEM = 58 * 1024 * 1024
MM_TILE_BYTES = 4 * 1024 * 1024
LOG2E = 1.4426950408889634
LN2 = 0.6931471805599453
NEG = -1e30

IN_NAMES = ("q", "k", "v", "g", "gk", "cq", "ckv", "kr", "ga", "gb")
MY_ORDER = ("v", "g", "ga", "gb", "q", "k", "cq", "ckv", "gk", "kr")

WEIGHTS = ("ada_w", "ada_b", "norm_mix_g", "w_in", "gla_gk_w", "gla_gk_b", "gla_onorm_g", "gla_wo",
           "mla_q_norm_g", "mla_wuq", "mla_kv_norm_g", "mla_wukv", "mla_wo", "w_out", "norm_ffn_g",
           "ffn_w_in", "ffn_w_down", "final_norm_g")
BIG = ("w_in", "gla_gk_w", "gla_wo", "mla_wuq", "mla_wukv", "mla_wo", "w_out", "ffn_w_in", "ffn_w_down")
SMALL = ("ada_b", "norm_mix_g", "gla_gk_b", "gla_onorm_g", "mla_q_norm_g", "mla_kv_norm_g", "norm_ffn_g",
         "final_norm_g")


def _in_layout(d):
    w = dict(q=d // 2, k=d // 2, v=d, g=d, gk=GLA_GATE_RANK, cq=d // 4, ckv=512, kr=MLA_ROPE, ga=d, gb=d)
    pw = {n: -(-w[n] // LANE) * LANE for n in w}
    ref_off, o = {}, 0
    for n in IN_NAMES:
        ref_off[n] = o
        o += w[n]
    my_off, o = {}, 0
    for n in MY_ORDER:
        assert o % pw[n] == 0
        my_off[n] = o
        o += pw[n]
    return w, pw, ref_off, my_off, o


def _cparams(sem=None, vmem_limit=VMEM_LIMIT):
    return pltpu.CompilerParams(dimension_semantics=sem, vmem_limit_bytes=vmem_limit)


def _dot(a, b, ca=1, cb=0):
    return lax.dot_general(a, b, (((ca,), (cb,)), ((), ())), preferred_element_type=F32)


def _tile(n, cap):
    if n <= cap:
        return n
    t = (cap // LANE) * LANE
    while t >= LANE:
        if n % t == 0:
            return t
        t -= LANE
    return n


def _mm(a, b, *, ta=False, tb=False, out_dtype=F32, name, ride=None, slabs=False):
    m, k = (a.shape[1], a.shape[0]) if ta else a.shape
    n = b.shape[0] if tb else b.shape[1]
    assert k == (b.shape[1] if tb else b.shape[0])
    wide = max(a.dtype.itemsize, b.dtype.itemsize) > 2
    tm, tn, tk = _tile(m, 1024), _tile(n, 1024), _tile(k, MM_TILE_BYTES // (1024 * (4 if wide else 2)))
    if slabs:
        tn = n // N_DEV
        assert tn % LANE == 0
    nk = k // tk

    def product(a_ref, b_ref):
        return _dot(a_ref[...].astype(BF16), b_ref[...].astype(BF16), 0 if ta else 1, 1 if tb else 0)

    def store(o_ref, val):
        if slabs:
            o_ref[0] = val.astype(o_ref.dtype)
        else:
            o_ref[...] = val.astype(o_ref.dtype)

    def body_one(a_ref, b_ref, o_ref):
        store(o_ref, product(a_ref, b_ref))

    def body_acc(a_ref, b_ref, o_ref, acc_ref):
        kk = pl.program_id(2)

        @pl.when(kk == 0)
        def _():
            acc_ref[...] = jnp.zeros_like(acc_ref)

        acc_ref[...] += product(a_ref, b_ref)

        @pl.when(kk == nk - 1)
        def _():
            store(o_ref, acc_ref[...])

    a_spec = (pl.BlockSpec((tk, tm), lambda i, j, kk: (kk, i)) if ta
              else pl.BlockSpec((tm, tk), lambda i, j, kk: (i, kk)))
    b_spec = (pl.BlockSpec((tn, tk), lambda i, j, kk: (j, kk)) if tb
              else pl.BlockSpec((tk, tn), lambda i, j, kk: (kk, j)))
    (out,), rode = _call(
        body_one if nk == 1 else body_acc, name=name, grid=(m // tm, n // tn, nk), in_specs=[a_spec, b_spec],
        out_specs=[pl.BlockSpec((1, tm, tn), lambda i, j, kk: (j, i, 0)) if slabs
                   else pl.BlockSpec((tm, tn), lambda i, j, kk: (i, j))],
        out_shape=[jax.ShapeDtypeStruct((N_DEV, m, tn) if slabs else (m, n), out_dtype)],
        scratch_shapes=[] if nk == 1 else [pltpu.VMEM((tm, tn), F32)],
        sem=("parallel", "parallel", "arbitrary"), args=(a, b), ride=ride)
    return out if ride is None else (out, rode)


def _rowwise(fn, rows, vecs, outs, sums=(), *, tile, name):
    t = rows[0][0].shape[0]
    tile = min(tile, t)
    assert t % tile == 0
    n_rows, n_vecs, n_outs = len(rows), len(vecs), len(outs)

    def body(*refs):
        ins = [r[...].astype(F32) for r in refs[:n_rows + n_vecs]]
        res = fn(*ins)
        if not isinstance(res, (tuple, list)):
            res = (res,)
        out_refs = refs[n_rows + n_vecs:]
        for r, val in zip(out_refs[:n_outs], res[:n_outs]):
            r[...] = val.astype(r.dtype)
        if sums:
            first = pl.program_id(0) == 0
            for r, val in zip(out_refs[n_outs:], res[n_outs:]):
                @pl.when(first)
                def _(r=r):
                    r[...] = jnp.zeros_like(r)
                r[...] += val

    in_specs = [pl.BlockSpec((tile, w), lambda i, cb=cb: (i, cb)) for (_, w, cb) in rows]
    in_specs += [pl.BlockSpec(v.shape, lambda i: (0, 0)) for v in vecs]
    out_specs = [pl.BlockSpec((tile, w), lambda i: (i, 0)) for (w, _) in outs]
    out_specs += [pl.BlockSpec((1, w), lambda i: (0, 0)) for w in sums]
    out_shape = [jax.ShapeDtypeStruct((t, w), dt) for (w, dt) in outs]
    out_shape += [jax.ShapeDtypeStruct((1, w), F32) for w in sums]
    res = pl.pallas_call(
        body, name=name, grid=(t // tile,), in_specs=in_specs, out_specs=out_specs, out_shape=out_shape,
        compiler_params=_cparams(("arbitrary",)),
    )(*[r[0] for r in rows], *vecs)
    return res


def _rstd(x):
    return lax.rsqrt(jnp.mean(x * x, axis=-1, keepdims=True) + NORM_EPS)


def _sigmoid(x):
    return 1.0 / (1.0 + jnp.exp(-x))


def _rms_bwd(dxh, xh, r):
    return r * (dxh - xh * jnp.mean(dxh * xh, axis=-1, keepdims=True))


def _rope(t, tab, sign):
    cosf, sin_a, sin_b = tab[:, :LANE], tab[:, LANE:2 * LANE], tab[:, 2 * LANE:]
    return t * cosf + sign * (pltpu.roll(t, 96, 1) * sin_a + pltpu.roll(t, 32, 1) * sin_b)


def _split3(x):
    hi = x.astype(BF16)
    r1 = x - hi.astype(F32)
    mid = r1.astype(BF16)
    lo = (r1 - mid.astype(F32)).astype(BF16)
    return hi, mid, lo


def _tri_sum(tri_bf16, x):
    hi, mid, lo = _split3(x)
    return _dot(tri_bf16, hi) + _dot(tri_bf16, mid) + _dot(tri_bf16, lo)


def _dot_nt2(a, b):
    a_hi = a.astype(BF16)
    a_lo = (a - a_hi.astype(F32)).astype(BF16)
    b_hi = b.astype(BF16)
    b_lo = (b - b_hi.astype(F32)).astype(BF16)
    return _dot(a_hi, b_hi, 1, 1) + _dot(a_hi, b_lo, 1, 1) + _dot(a_lo, b_hi, 1, 1)


def _gla_specs(t, rows, lay, reverse):
    nb = t // rows
    blk = (lambda i: nb - 1 - i) if reverse else (lambda i: i)
    qb, kb = lay["q"] // GLA_DK, lay["k"] // GLA_DK
    vb = lay["v"] // GLA_DV
    return [
        pl.BlockSpec((rows, GLA_DK), lambda h, i: (blk(i), qb + h)),
        pl.BlockSpec((rows, GLA_DK), lambda h, i: (blk(i), kb + h)),
        pl.BlockSpec((rows, GLA_DV), lambda h, i: (blk(i), vb + h)),
        pl.BlockSpec((rows, GLA_DK), lambda h, i: (blk(i), h)),
    ], blk


def _gla_fwd(p, la, lay):
    t = p.shape[0]
    rows = min(512, t)
    nb, nc = t // rows, rows // GLA_CHUNK
    c64 = GLA_CHUNK

    def body(q_ref, k_ref, v_ref, la_ref, o_ref, st_ref, s_ref):
        @pl.when(pl.program_id(1) == 0)
        def _():
            s_ref[...] = jnp.zeros_like(s_ref)

        r = lax.broadcasted_iota(jnp.int32, (c64, c64), 0)
        cc = lax.broadcasted_iota(jnp.int32, (c64, c64), 1)
        tril = cc <= r
        tril_b = tril.astype(BF16)
        for c in range(nc):
            sl = pl.ds(c * c64, c64)
            b = _tri_sum(tril_b, la_ref[sl, :])
            b_last = b[c64 - 1:c64, :]
            q = q_ref[sl, :].astype(F32) * GLA_QSCALE
            k = k_ref[sl, :].astype(F32)
            v = v_ref[sl, :].astype(BF16)
            qt_f = q * jnp.exp(b)
            qt = qt_f.astype(BF16)
            kh = (k * jnp.exp(b_last - b)).astype(BF16)
            s_prev = s_ref[...]
            st_ref[0, c] = s_prev
            att = jnp.where(tril, _dot_nt2(qt_f, k * jnp.exp(-b)), 0.0)
            o_ref[sl, :] = _dot(qt, s_prev.astype(BF16), 1, 1) + _dot(att.astype(BF16), v)
            s_ref[...] = s_prev * jnp.exp(b_last) + _dot(v, kh, 0, 0)

    in_specs, _ = _gla_specs(t, rows, lay, False)
    return pl.pallas_call(
        body, name="gla_fwd", grid=(GLA_HEADS, nb), in_specs=in_specs,
        out_specs=[pl.BlockSpec((rows, GLA_DV), lambda h, i: (i, h)),
                   pl.BlockSpec((1, nc, GLA_DV, GLA_DK), lambda h, i: (h, i, 0, 0))],
        out_shape=[jax.ShapeDtypeStruct((t, GLA_HEADS * GLA_DV), F32),
                   jax.ShapeDtypeStruct((GLA_HEADS, t // c64, GLA_DV, GLA_DK), F32)],
        scratch_shapes=[pltpu.VMEM((GLA_DV, GLA_DK), F32)],
        compiler_params=_cparams(("parallel", "arbitrary")),
    )(p, p, p, la)


def _gla_bwd(p, la, do, states, lay):
    t = p.shape[0]
    rows = min(512, t)
    nb, nc = t // rows, rows // GLA_CHUNK
    c64 = GLA_CHUNK

    def body(q_ref, k_ref, v_ref, la_ref, do_ref, st_ref, dq_ref, dk_ref, dv_ref, dla_ref, ds_ref):
        @pl.when(pl.program_id(1) == 0)
        def _():
            ds_ref[...] = jnp.zeros_like(ds_ref)

        r = lax.broadcasted_iota(jnp.int32, (c64, c64), 0)
        cc = lax.broadcasted_iota(jnp.int32, (c64, c64), 1)
        tril = cc <= r
        tril_b = tril.astype(BF16)
        triu_b = (cc >= r).astype(BF16)
        for c in reversed(range(nc)):
            sl = pl.ds(c * c64, c64)
            b = _tri_sum(tril_b, la_ref[sl, :])
            b_last = b[c64 - 1:c64, :]
            eb, enb, ebl_b, ebl = jnp.exp(b), jnp.exp(-b), jnp.exp(b_last - b), jnp.exp(b_last)
            k = k_ref[sl, :].astype(F32)
            qt_f = q_ref[sl, :].astype(F32) * GLA_QSCALE * eb
            kt_f = k * enb
            kh_f = k * ebl_b
            qt, kt, kh = qt_f.astype(BF16), kt_f.astype(BF16), kh_f.astype(BF16)
            v_f = v_ref[sl, :].astype(F32)
            dout_f = do_ref[sl, :]
            v, dout = v_f.astype(BF16), dout_f.astype(BF16)
            s_prev = st_ref[0, c]
            ds_next = ds_ref[...]
            ds_next_b = ds_next.astype(BF16)
            att = jnp.where(tril, _dot_nt2(qt_f, kt_f), 0.0).astype(BF16)
            datt = jnp.where(tril, _dot_nt2(dout_f, v_f), 0.0).astype(BF16)
            dqt = _dot(dout, s_prev.astype(BF16)) + _dot(datt, kt)
            dkt = _dot(datt, qt, 0, 0)
            dv = _dot(att, dout, 0, 0) + _dot(kh, ds_next_b, 1, 1)
            dkh = _dot(v, ds_next_b)
            d_ebl = jnp.sum(ds_next * s_prev, axis=0, keepdims=True)
            ds_ref[...] = ds_next * ebl + _dot(dout, qt, 0, 0)
            db = dqt * qt_f - dkt * kt_f - dkh * kh_f
            db_last = ebl * d_ebl + jnp.sum(dkh * kh_f, axis=0, keepdims=True)
            dq_ref[sl, :] = (dqt * eb * GLA_QSCALE).astype(dq_ref.dtype)
            dk_ref[sl, :] = (dkt * enb + dkh * ebl_b).astype(dk_ref.dtype)
            dv_ref[sl, :] = dv.astype(dv_ref.dtype)
            dla_ref[sl, :] = _tri_sum(triu_b, db) + db_last

    in_specs, blk = _gla_specs(t, rows, lay, True)
    in_specs += [pl.BlockSpec((rows, GLA_DV), lambda h, i: (blk(i), h)),
                 pl.BlockSpec((1, nc, GLA_DV, GLA_DK), lambda h, i: (h, blk(i), 0, 0))]
    dk_spec = pl.BlockSpec((rows, GLA_DK), lambda h, i: (blk(i), h))
    return pl.pallas_call(
        body, name="gla_bwd", grid=(GLA_HEADS, nb), in_specs=in_specs,
        out_specs=[dk_spec, dk_spec, pl.BlockSpec((rows, GLA_DV), lambda h, i: (blk(i), h)), dk_spec],
        out_shape=[jax.ShapeDtypeStruct((t, GLA_HEADS * GLA_DK), BF16),
                   jax.ShapeDtypeStruct((t, GLA_HEADS * GLA_DK), BF16),
                   jax.ShapeDtypeStruct((t, GLA_HEADS * GLA_DV), BF16),
                   jax.ShapeDtypeStruct((t, GLA_HEADS * GLA_DK), F32)],
        scratch_shapes=[pltpu.VMEM((GLA_DV, GLA_DK), F32)],
        compiler_params=_cparams(("parallel", "arbitrary")),
    )(p, p, p, la, do, states)


def _diag_mask(rows, cols, row0):
    row = row0 + lax.broadcasted_iota(jnp.int32, (rows, cols), 0)
    col = lax.broadcasted_iota(jnp.int32, (rows, cols), 1)
    return col <= row


QK_SPARE = MLA_NOPE + MLA_ROPE
N_SPARE = 3


def _with_spare(x, col, lane0):
    lane = lax.broadcasted_iota(jnp.int32, x.shape, 1)
    for n, term in enumerate(_split3(col)):
        x = jnp.where(lane == lane0 + n, term, x)
    return x


def _spare_ones(shape, lane0):
    lane = lax.broadcasted_iota(jnp.int32, shape, 1)
    return ((lane >= lane0) & (lane < lane0 + N_SPARE)).astype(F32)


def _flash_tiles(t):
    tq = min(1024, t)
    halves = 2 if tq % 32 == 0 else 1
    return tq, t // tq, halves, tq // halves


def _flash_fwd(q, k, vx, ride=None):
    t = q.shape[0]
    tq, nq, halves, hr = _flash_tiles(t)
    dqk, dv = MLA_QK_PAD, MLA_V

    def body(q_ref, k_ref, v_ref, o_ref, qa_ref, m_ref, acc_ref):
        i = pl.program_id(1)
        m_ref[...] = jnp.full_like(m_ref, NEG)
        acc_ref[...] = jnp.zeros_like(acc_ref)

        def scores(j):
            kb = k_ref[pl.ds(pl.multiple_of(j * tq, tq), tq), :]
            return tuple(_dot(q_ref[pl.ds(hh * hr, hr), :], kb, 1, 1) for hh in range(halves))

        def consume(j, s_all, masked):
            vb = v_ref[pl.ds(pl.multiple_of(j * tq, tq), tq), :]
            for hh in range(halves):
                rs = pl.ds(hh * hr, hr)
                s = s_all[hh]
                if masked:
                    s = jnp.where(_diag_mask(hr, tq, hh * hr), s, NEG)
                m_old = m_ref[rs, :]
                m_new = jnp.maximum(m_old, jnp.max(s, axis=1, keepdims=True))
                pr = jnp.exp2(s - m_new)
                acc_ref[rs, :] = jnp.exp2(m_old - m_new) * acc_ref[rs, :] + _dot(pr.astype(BF16), vb)
                m_ref[rs, :] = m_new

        def loop_body(j, s_cur):
            s_next = scores(j + 1)
            consume(j, s_cur, False)
            return s_next

        consume(i, lax.fori_loop(0, i, loop_body, scores(0)), True)
        acc = acc_ref[...]
        l = acc[:, dv:dv + 1]
        o_ref[...] = acc[:, :dv] / l
        qa_ref[...] = _with_spare(q_ref[...], -(m_ref[...] + jnp.log(l) * LOG2E), QK_SPARE)

    outs, rode = _call(
        body, name="mla_flash_fwd", grid=(MLA_HEADS, nq),
        in_specs=[pl.BlockSpec((tq, dqk), lambda h, i: (i, h)),
                  pl.BlockSpec((t, dqk), lambda h, i: (0, h)),
                  pl.BlockSpec((t, 2 * dv), lambda h, i: (0, h))],
        out_specs=[pl.BlockSpec((tq, dv), lambda h, i: (i, h)),
                   pl.BlockSpec((tq, dqk), lambda h, i: (i, h))],
        out_shape=[jax.ShapeDtypeStruct((t, MLA_HEADS * dv), F32),
                   jax.ShapeDtypeStruct((t, MLA_HEADS * dqk), BF16)],
        scratch_shapes=[pltpu.VMEM((tq, 1), F32), pltpu.VMEM((tq, 2 * dv), F32)],
        sem=("parallel", "arbitrary"), args=(q, k, vx), ride=ride)
    return tuple(outs) if ride is None else (tuple(outs), rode)


def _flash_bwd(qa, k, vx, doa, ride=None):
    t = qa.shape[0]
    tq, nq, halves, hr = _flash_tiles(t)
    dqk, dv = MLA_QK_PAD, MLA_V

    def body(k_ref, v_ref, q_ref, do_ref, dq_ref, dk_ref, dv_ref, dq_acc, dk_acc, dv_acc):
        j = pl.program_id(1)

        @pl.when(j == 0)
        def _():
            dq_acc[...] = jnp.zeros_like(dq_acc)

        kb = k_ref[...]
        vb = v_ref[...]
        dk_acc[...] = jnp.zeros_like(dk_acc)
        dv_acc[...] = jnp.zeros_like(dv_acc)

        def step(i, masked):
            for hh in range(halves):
                rs = pl.ds(pl.multiple_of(i * tq + hh * hr, hr), hr)
                qb = q_ref[rs, :]
                dout = do_ref[rs, :]
                nc = (hh + 1) * hr if masked else tq
                s = _dot(qb, kb[:nc], 1, 1)
                if masked:
                    s = jnp.where(_diag_mask(hr, nc, hh * hr), s, NEG)
                pr = jnp.exp2(s)
                ds = (pr * _dot(dout, vb[:nc], 1, 1)).astype(BF16)
                dv_acc[pl.ds(0, nc), :] += _dot(pr.astype(BF16), dout, 0, 0)
                dk_acc[pl.ds(0, nc), :] += _dot(ds, qb, 0, 0)
                dq_acc[rs, :] += _dot(ds, kb[:nc])

        def loop_body(i, carry):
            step(i, False)
            return carry

        step(j, True)
        lax.fori_loop(j + 1, nq, loop_body, 0)
        dk_ref[...] = (dk_acc[...] * LN2).astype(dk_ref.dtype)
        dv_ref[...] = dv_acc[:, :dv].astype(dv_ref.dtype)

        @pl.when(j == nq - 1)
        def _():
            dq_ref[...] = (dq_acc[...] * ATT_SCALE).astype(dq_ref.dtype)

    outs, rode = _call(
        body, name="mla_flash_bwd", grid=(MLA_HEADS, nq),
        in_specs=[pl.BlockSpec((tq, dqk), lambda h, j: (j, h)),
                  pl.BlockSpec((tq, 2 * dv), lambda h, j: (j, h)),
                  pl.BlockSpec((t, dqk), lambda h, j: (0, h)),
                  pl.BlockSpec((t, 2 * dv), lambda h, j: (0, h))],
        out_specs=[pl.BlockSpec((t, dqk), lambda h, j: (0, h)),
                   pl.BlockSpec((tq, dqk), lambda h, j: (j, h)),
                   pl.BlockSpec((tq, dv), lambda h, j: (j, h))],
        out_shape=[jax.ShapeDtypeStruct((t, MLA_HEADS * dqk), BF16),
                   jax.ShapeDtypeStruct((t, MLA_HEADS * dqk), BF16),
                   jax.ShapeDtypeStruct((t, MLA_HEADS * dv), BF16)],
        scratch_shapes=[pltpu.VMEM((t, dqk), F32), pltpu.VMEM((tq, dqk), F32), pltpu.VMEM((tq, 2 * dv), F32)],
        sem=("parallel", "arbitrary"), args=(k, vx, qa, doa), ride=ride, vmem_limit=FLASH_BWD_VMEM)
    return tuple(outs) if ride is None else (tuple(outs), rode)


class _NoRides:
    def ride(self, stage, grads):
        return None

    def done(self, stage, rode, w):
        pass


def _local_step(x, target, tab, mod8, w, rides=None):
    t, d = x.shape
    rides = rides or _NoRides()
    big = {}

    def riding(stage, fn):
        r = rides.ride(stage, big)
        res = fn(r)
        if r is None:
            return res
        rides.done(stage, res[1], w)
        return res[0]
    _, pw, _, lay, _ = _in_layout(d)
    ffn = ((8 * d // 3 + 255) // 256) * 256

    def blk(arr, name):
        return (arr, pw[name], lay[name] // pw[name])

    def full(arr):
        return (arr, arr.shape[1], 0)

    g1, g2, g3 = w["norm_mix_g"], w["norm_ffn_g"], w["final_norm_g"]

    def f_ln1(xv, mod, g):
        return (xv * _rstd(xv) * g) * (1.0 + mod[1:2]) + mod[0:1]

    (h,) = _rowwise(f_ln1, [full(x)], [mod8, g1], [(d, BF16)], tile=256, name="ln1_modulate")
    p = riding("in_proj", lambda r: _mm(h, w["w_in"], name="mm_in_proj", ride=r))

    def f_gk(pgk, gkw, gkb):
        z = _dot(pgk.astype(BF16), gkw.astype(BF16)) + gkb
        return (jnp.minimum(z, 0.0) - jnp.log(1.0 + jnp.exp(-jnp.abs(z)))) / GLA_GATE_NORMALIZER

    (la,) = _rowwise(f_gk, [blk(p, "gk")], [w["gla_gk_w"], w["gla_gk_b"]], [(GLA_HEADS * GLA_DK, F32)],
                     tile=512, name="gla_gate")
    o_gla, states = _gla_fwd(p, la, lay)

    def f_gla_out(ov, pg, g):
        parts = []
        for hh in range(GLA_HEADS):
            oh = ov[:, hh * GLA_DV:(hh + 1) * GLA_DV]
            ph = pg[:, hh * GLA_DV:(hh + 1) * GLA_DV]
            parts.append(oh * _rstd(oh) * g * (ph * _sigmoid(ph)))
        return jnp.concatenate(parts, axis=1)

    (o_n,) = _rowwise(f_gla_out, [full(o_gla), blk(p, "g")], [w["gla_onorm_g"]], [(d, BF16)], tile=256,
                      name="gla_out_norm")
    y_gla = _mm(o_n, w["gla_wo"], out_dtype=BF16, name="mm_gla_wo")

    def f_mla_prep(cq, ckv, kr, tb, gq, gkv):
        return cq * _rstd(cq) * gq, ckv * _rstd(ckv) * gkv, _rope(kr, tb, 1.0)

    cqn, ckvn, krr = _rowwise(f_mla_prep, [blk(p, "cq"), blk(p, "ckv"), blk(p, "kr"), full(tab)],
                              [w["mla_q_norm_g"], w["mla_kv_norm_g"]],
                              [(pw["cq"], BF16), (pw["ckv"], BF16), (LANE, F32)], tile=512, name="mla_prep")
    qlat = _mm(cqn, w["mla_wuq"], out_dtype=BF16, name="mm_mla_wuq")
    kvl = _mm(ckvn, w["mla_wukv"], out_dtype=BF16, name="mm_mla_wukv")
    hv = MLA_HEADS * MLA_V

    def f_qkv(ql, kn, vv, kr, tb):
        qs, ks, vx = [], [], []
        kr1 = kr + _spare_ones(kr.shape, MLA_ROPE)
        ones = _spare_ones(kr.shape, 0)
        for hh in range(MLA_HEADS):
            o0 = hh * MLA_QK_PAD
            qs += [ql[:, o0:o0 + LANE], _rope(ql[:, o0 + LANE:o0 + 2 * LANE], tb, 1.0)]
            ks += [kn[:, hh * LANE:(hh + 1) * LANE], kr1]
            vx += [vv[:, hh * MLA_V:(hh + 1) * MLA_V], ones]
        return (jnp.concatenate(qs, axis=1) * (ATT_SCALE * LOG2E), jnp.concatenate(ks, axis=1),
                jnp.concatenate(vx, axis=1))

    qa, ka, vxa = _rowwise(f_qkv, [full(qlat), (kvl, hv, 0), (kvl, hv, 1), full(krr), full(tab)], [],
                           [(MLA_HEADS * MLA_QK_PAD, BF16), (MLA_HEADS * MLA_QK_PAD, BF16), (2 * hv, BF16)],
                           tile=256, name="mla_qkv_build")
    o_mla, qa_lse = riding("flash_fwd", lambda r: _flash_fwd(qa, ka, vxa, ride=r))
    y_mla = _mm(o_mla, w["mla_wo"], out_dtype=BF16, name="mm_mla_wo")

    def f_merge(yg, ym, ga, gb):
        return _sigmoid(ga) * yg + _sigmoid(gb) * ym

    (merged,) = _rowwise(f_merge, [full(y_gla), full(y_mla), blk(p, "ga"), blk(p, "gb")], [], [(d, BF16)],
                         tile=256, name="merge")
    mix = _mm(merged, w["w_out"], name="mm_w_out")

    def f_res_ln2(xv, mx, mod, g):
        x2v = xv + mod[2:3] * mx
        return x2v, (x2v * _rstd(x2v) * g) * (1.0 + mod[4:5]) + mod[3:4]

    x2, h2 = _rowwise(f_res_ln2, [full(x), full(mix)], [mod8, g2], [(d, F32), (d, BF16)], tile=256,
                      name="res_ln2_modulate")
    gu = _mm(h2, w["ffn_w_in"], out_dtype=BF16, name="mm_ffn_in")

    def f_swiglu(gv, uv):
        return gv * _sigmoid(gv) * uv

    (act,) = _rowwise(f_swiglu, [(gu, ffn, 0), (gu, ffn, 1)], [], [(ffn, BF16)], tile=128, name="swiglu")
    f_out = _mm(act, w["ffn_w_down"], name="mm_ffn_down")

    def f_head(x2v, fv, tg, mod, g):
        x3 = x2v + mod[5:6] * fv
        r = _rstd(x3)
        xh = x3 * r
        e = xh * g - tg
        loss_rows = 0.5 * jnp.mean(e * e, axis=-1, keepdims=True)
        dy = e * (1.0 / d)
        dx3 = _rms_bwd(dy * g, xh, r)
        loss = jnp.broadcast_to(jnp.sum(loss_rows, axis=0, keepdims=True), (1, LANE))
        return (dx3, dx3 * mod[5:6], loss, jnp.sum(dy * xh, axis=0, keepdims=True),
                jnp.sum(dx3 * fv, axis=0, keepdims=True))

    dx3, df, loss_v, dg3, dgate_f = _rowwise(f_head, [full(x2), full(f_out), full(target)], [mod8, g3],
                                             [(d, F32), (d, BF16)], [LANE, d, d], tile=256, name="loss_head")
    da = _mm(df, w["ffn_w_down"], tb=True, out_dtype=BF16, name="mm_ffn_down_dx")
    big["ffn_w_down"] = _mm(act, df, ta=True, out_dtype=BF16, name="mm_ffn_down_dw")

    def f_swiglu_bwd(gv, uv, dav):
        sg = _sigmoid(gv)
        return jnp.concatenate([dav * uv * (sg * (1.0 + gv * (1.0 - sg))), dav * (gv * sg)], axis=1)

    (dgu,) = _rowwise(f_swiglu_bwd, [(gu, ffn, 0), (gu, ffn, 1), full(da)], [], [(2 * ffn, BF16)], tile=128,
                      name="swiglu_bwd")
    dh2 = _mm(dgu, w["ffn_w_in"], tb=True, name="mm_ffn_in_dx")
    big["ffn_w_in"] = _mm(h2, dgu, ta=True, out_dtype=BF16, slabs=True, name="mm_ffn_in_dw")

    def f_ln2_bwd(x2v, dh, dx3v, mx, mod, g):
        r = _rstd(x2v)
        xh = x2v * r
        dn = dh * (1.0 + mod[4:5])
        dx2 = dx3v + _rms_bwd(dn * g, xh, r)
        return (dx2, dx2 * mod[2:3],
                jnp.sum(dh * (xh * g), axis=0, keepdims=True), jnp.sum(dh, axis=0, keepdims=True),
                jnp.sum(dn * xh, axis=0, keepdims=True), jnp.sum(dx2 * mx, axis=0, keepdims=True))

    dx2, dmix, dscale_f, dshift_f, dg2, dgate_m = _rowwise(
        f_ln2_bwd, [full(x2), full(dh2), full(dx3), full(mix)], [mod8, g2], [(d, F32), (d, BF16)],
        [d, d, d, d], tile=256, name="ln2_bwd")
    dmerged = _mm(dmix, w["w_out"], tb=True, out_dtype=BF16, name="mm_w_out_dx")
    big["w_out"] = _mm(merged, dmix, ta=True, out_dtype=BF16, name="mm_w_out_dw")

    def f_merge_bwd(dm, yg, ym, ga, gb):
        sa, sb = _sigmoid(ga), _sigmoid(gb)
        return dm * sa, dm * sb, dm * yg * sa * (1.0 - sa), dm * ym * sb * (1.0 - sb)

    dy_gla, dy_mla, dp_ga, dp_gb = _rowwise(
        f_merge_bwd, [full(dmerged), full(y_gla), full(y_mla), blk(p, "ga"), blk(p, "gb")], [],
        [(d, BF16)] * 4, tile=256, name="merge_bwd")
    do_n = _mm(dy_gla, w["gla_wo"], tb=True, name="mm_gla_wo_dx")
    big["gla_wo"] = _mm(o_n, dy_gla, ta=True, out_dtype=BF16, name="mm_gla_wo_dw")
    do_m = _mm(dy_mla, w["mla_wo"], tb=True, out_dtype=BF16, name="mm_mla_wo_dx")
    big["mla_wo"] = _mm(o_mla, dy_mla, ta=True, out_dtype=BF16, name="mm_mla_wo_dw")

    def f_gla_out_bwd(don, ov, pg, g):
        dos, dpgs = [], []
        dg = jnp.zeros((1, GLA_DV), F32)
        for hh in range(GLA_HEADS):
            sl = slice(hh * GLA_DV, (hh + 1) * GLA_DV)
            oh, ph, dn = ov[:, sl], pg[:, sl], don[:, sl]
            r = _rstd(oh)
            xh = oh * r
            sg = _sigmoid(ph)
            dpre = dn * (ph * sg)
            dg = dg + jnp.sum(dpre * xh, axis=0, keepdims=True)
            dos.append(_rms_bwd(dpre * g, xh, r))
            dpgs.append(dn * (xh * g) * (sg * (1.0 + ph * (1.0 - sg))))
        return jnp.concatenate(dos, axis=1), jnp.concatenate(dpgs, axis=1), dg

    do_gla, dp_g, dg_on = _rowwise(f_gla_out_bwd, [full(do_n), full(o_gla), blk(p, "g")], [w["gla_onorm_g"]],
                                   [(d, F32), (d, BF16)], [GLA_DV], tile=256, name="gla_out_norm_bwd")
    dp_q, dp_k, dp_v, dla = _gla_bwd(p, la, do_gla, states, lay)

    def f_gk_bwd(dlav, pgk, gkw, gkb):
        z = _dot(pgk.astype(BF16), gkw.astype(BF16)) + gkb
        dz = dlav * (1.0 / GLA_GATE_NORMALIZER) * _sigmoid(-z)
        return dz, _dot(dz.astype(BF16), gkw.astype(BF16), 1, 1), jnp.sum(dz, axis=0, keepdims=True)

    dz, dp_gk, dgk_b = _rowwise(f_gk_bwd, [full(dla), blk(p, "gk")], [w["gla_gk_w"], w["gla_gk_b"]],
                                [(GLA_HEADS * GLA_DK, BF16), (LANE, BF16)], [GLA_HEADS * GLA_DK], tile=512,
                                name="gla_gate_bwd")
    p_gk = lax.slice_in_dim(p, lay["gk"], lay["gk"] + LANE, axis=1)
    big["gla_gk_w"] = _mm(p_gk, dz, ta=True, name="mm_gla_gk_dw")[:GLA_GATE_RANK]

    def f_do_aug(dom, om):
        parts = []
        for hh in range(MLA_HEADS):
            dh_ = dom[:, hh * MLA_V:(hh + 1) * MLA_V]
            delta = jnp.sum(dh_ * om[:, hh * MLA_V:(hh + 1) * MLA_V], axis=1, keepdims=True)
            parts += [dh_.astype(BF16), _with_spare(jnp.zeros(dh_.shape, BF16), -delta, 0)]
        return jnp.concatenate(parts, axis=1)

    (doa,) = _rowwise(f_do_aug, [full(do_m), full(o_mla)], [], [(2 * hv, BF16)], tile=256, name="mla_do_delta")
    dqa, dka, dva = riding("flash_bwd", lambda r: _flash_bwd(qa_lse, ka, vxa, doa, ride=r))

    def f_qkv_bwd(dq, dk, dvv, tb):
        dqs, dkn = [], []
        dkr = jnp.zeros((dq.shape[0], LANE), F32)
        for hh in range(MLA_HEADS):
            o0 = hh * MLA_QK_PAD
            dqs += [dq[:, o0:o0 + LANE], _rope(dq[:, o0 + LANE:o0 + 2 * LANE], tb, -1.0)]
            dkn.append(dk[:, o0:o0 + LANE])
            dkr = dkr + dk[:, o0 + LANE:o0 + 2 * LANE]
        return jnp.concatenate(dqs, axis=1), jnp.concatenate(dkn + [dvv], axis=1), dkr

    dqlat, dkvl, dkrr = _rowwise(f_qkv_bwd, [full(dqa), full(dka), full(dva), full(tab)], [],
                                 [(MLA_HEADS * MLA_QK_PAD, BF16), (2 * hv, BF16), (LANE, F32)], tile=256,
                                 name="mla_qkv_build_bwd")
    dcqn = _mm(dqlat, w["mla_wuq"], tb=True, name="mm_mla_wuq_dx")
    big["mla_wuq"] = _mm(cqn, dqlat, ta=True, out_dtype=BF16, name="mm_mla_wuq_dw")
    dckvn = _mm(dkvl, w["mla_wukv"], tb=True, name="mm_mla_wukv_dx")
    big["mla_wukv"] = _mm(ckvn, dkvl, ta=True, out_dtype=BF16, name="mm_mla_wukv_dw")

    def f_mla_prep_bwd(dq, dkv, dkr, cq, ckv, tb, gq, gkv):
        rq, rk = _rstd(cq), _rstd(ckv)
        xq, xk = cq * rq, ckv * rk
        return (_rms_bwd(dq * gq, xq, rq), _rms_bwd(dkv * gkv, xk, rk), _rope(dkr, tb, -1.0),
                jnp.sum(dq * xq, axis=0, keepdims=True), jnp.sum(dkv * xk, axis=0, keepdims=True))

    dp_cq, dp_ckv, dp_kr, dg_q, dg_kv = _rowwise(
        f_mla_prep_bwd, [full(dcqn), full(dckvn), full(dkrr), blk(p, "cq"), blk(p, "ckv"), full(tab)],
        [w["mla_q_norm_g"], w["mla_kv_norm_g"]], [(pw["cq"], BF16), (pw["ckv"], BF16), (LANE, BF16)],
        [pw["cq"], pw["ckv"]], tile=512, name="mla_prep_bwd")

    pieces = dict(v=dp_v, g=dp_g, ga=dp_ga, gb=dp_gb, q=dp_q, k=dp_k, cq=dp_cq, ckv=dp_ckv, gk=dp_gk, kr=dp_kr)
    dp = jnp.concatenate([pieces[n] for n in MY_ORDER], axis=1)
    big["w_in"] = riding("in_proj_dw", lambda r: _mm(h, dp, ta=True, out_dtype=BF16, name="mm_in_proj_dw", ride=r))
    dh = riding("in_proj_dx", lambda r: _mm(dp, w["w_in"], tb=True, name="mm_in_proj_dx", ride=r))

    def f_ln1_bwd(xv, dhv, dx2v, mod, g):
        r = _rstd(xv)
        xh = xv * r
        dn = dhv * (1.0 + mod[1:2])
        return (dx2v + _rms_bwd(dn * g, xh, r),
                jnp.sum(dhv * (xh * g), axis=0, keepdims=True), jnp.sum(dhv, axis=0, keepdims=True),
                jnp.sum(dn * xh, axis=0, keepdims=True))

    grad_x, dscale_m, dshift_m, dg1 = _rowwise(f_ln1_bwd, [full(x), full(dh), full(dx2)], [mod8, g1],
                                               [(d, F32)], [d, d, d], tile=256, name="ln1_bwd")

    dmod = jnp.concatenate([dshift_m, dscale_m, dgate_m, dshift_f, dscale_f, dgate_f], axis=1)
    small = dict(ada_b=dmod, norm_mix_g=dg1, gla_gk_b=dgk_b, gla_onorm_g=dg_on, mla_q_norm_g=dg_q,
                 mla_kv_norm_g=dg_kv, norm_ffn_g=dg2, final_norm_g=dg3)
    return loss_v[0, 0], grad_x, big, small


N_PEER = N_DEV - 1


def _exchange_copies(ins, outs, sems, scatter):
    send_sems, recv_sems, local_sems = sems
    x, y, c = lax.axis_index("x"), lax.axis_index("y"), lax.axis_index("c")
    me = 4 * x + 2 * y + c
    peers = []
    for rel in range(1, N_DEV):
        px = 1 - x if rel & 4 else x
        py = 1 - y if rel & 2 else y
        pc = 1 - c if rel & 1 else c
        peers.append(((px, py, pc), 4 * px + 2 * py + pc))

    def remote(a, k, src_slot, dst_slot):
        src = ins[a].at[src_slot] if scatter else ins[a]
        return pltpu.make_async_remote_copy(
            src_ref=src, dst_ref=outs[a].at[dst_slot], send_sem=send_sems.at[a * N_PEER + k],
            recv_sem=recv_sems.at[a * N_PEER + k], device_id=peers[k][0], device_id_type=pl.DeviceIdType.MESH)

    local, sends, recvs = [], [], []
    for a in range(len(ins)):
        src = ins[a].at[me] if scatter else ins[a]
        local.append(pltpu.make_async_copy(src, outs[a].at[me], local_sems.at[a]))
        for k in range(N_PEER):
            sends.append(remote(a, k, peers[k][1], me))
            recvs.append(remote(a, k, peers[k][1], peers[k][1]))
    return local, sends, recvs


def _exchange_start(ins, outs, sems, scatter):
    local, sends, _ = _exchange_copies(ins, outs, sems, scatter)
    for cp in local + sends:
        cp.start()


def _exchange_wait(ins, outs, sems, scatter):
    local, sends, recvs = _exchange_copies(ins, outs, sems, scatter)
    for cp in recvs:
        cp.wait_recv()
    for cp in sends:
        cp.wait_send()
    for cp in local:
        cp.wait()


def _exchange_shapes(arrs, scatter):
    n = len(arrs)
    out_shape = [jax.ShapeDtypeStruct(a.shape if scatter else (N_DEV,) + a.shape, a.dtype) for a in arrs]
    sems = [pltpu.SemaphoreType.DMA((n * N_PEER,)), pltpu.SemaphoreType.DMA((n * N_PEER,)),
            pltpu.SemaphoreType.DMA((n,))]
    return out_shape, sems


def _exchange(arrs, *, scatter, name):
    n = len(arrs)

    def body(*refs):
        ins, outs, sems = refs[:n], refs[n:2 * n], refs[2 * n:]
        _exchange_start(ins, outs, sems, scatter)
        _exchange_wait(ins, outs, sems, scatter)

    hbm = pl.BlockSpec(memory_space=pltpu.HBM)
    out_shape, sems = _exchange_shapes(arrs, scatter)
    return pl.pallas_call(body, name=name, in_specs=[hbm] * n, out_specs=[hbm] * n, out_shape=out_shape,
                          scratch_shapes=sems)(*arrs)


def _gather_once_per_chip(arrs, *, name):
    n = len(arrs)

    def body(*refs):
        ins, outs = refs[:n], refs[n:2 * n]
        send_sems, recv_sems, local_sems = refs[2 * n:]
        x, y, c = lax.axis_index("x"), lax.axis_index("y"), lax.axis_index("c")
        sibling = (x, y, 1 - c)
        chips = [(1 - x, y), (x, 1 - y), (1 - x, 1 - y)]

        def slot(px, py, pc):
            return 4 * px + 2 * py + pc

        def copy(a, k, block, to, src=None):
            dst = outs[a].at[slot(*block)]
            return pltpu.make_async_remote_copy(
                src_ref=dst if src is None else src, dst_ref=dst, send_sem=send_sems.at[a * N_PEER + k],
                recv_sem=recv_sems.at[a * N_PEER + k], device_id=to, device_id_type=pl.DeviceIdType.MESH)

        local, sends = [], []
        for a in range(n):
            local.append(pltpu.make_async_copy(ins[a], outs[a].at[slot(x, y, c)], local_sems.at[a]))
            sends.append(copy(a, 0, (x, y, c), sibling, src=ins[a]))
            sends += [copy(a, 1 + j, (x, y, c), (*chip, c), src=ins[a]) for j, chip in enumerate(chips)]
        for cp in local + sends:
            cp.start()
        for a in range(n):
            for j, chip in enumerate(chips):
                copy(a, 1 + j, (*chip, c), (x, y, c)).wait_recv()
                sends.append(copy(a, 4 + j, (*chip, c), sibling))
                sends[-1].start()
        for a in range(n):
            copy(a, 0, sibling, (x, y, c)).wait_recv()
            for j, chip in enumerate(chips):
                copy(a, 4 + j, (*chip, 1 - c), (x, y, c)).wait_recv()
        for cp in sends:
            cp.wait_send()
        for cp in local:
            cp.wait()

    hbm = pl.BlockSpec(memory_space=pltpu.HBM)
    out_shape, sems = _exchange_shapes(arrs, False)
    return pl.pallas_call(body, name=name, in_specs=[hbm] * n, out_specs=[hbm] * n, out_shape=out_shape,
                          scratch_shapes=sems)(*arrs)


def _call(body, *, name, grid, in_specs, out_specs, out_shape, scratch_shapes, sem, args, ride=None,
          vmem_limit=VMEM_LIMIT):
    if ride is None:
        res = pl.pallas_call(body, name=name, grid=grid, in_specs=in_specs, out_specs=out_specs, out_shape=out_shape,
                             scratch_shapes=scratch_shapes, compiler_params=_cparams(sem, vmem_limit))(*args)
        return res, None
    arrs, scatter = ride
    n, n_in, n_out, n_scr = len(arrs), len(in_specs), len(out_specs), len(scratch_shapes)
    x_shape, x_sems = _exchange_shapes(arrs, scatter)

    def hosted(*refs):
        c_in, x_in = refs[:n_in], refs[n_in:n_in + n]
        c_out, x_out = refs[n_in + n:n_in + n + n_out], refs[n_in + n + n_out:n_in + 2 * n + n_out]
        scr = refs[n_in + 2 * n + n_out:]
        c_scr, sems = scr[:n_scr], scr[n_scr:]
        first = functools.reduce(jnp.logical_and, [pl.program_id(a) == 0 for a in range(len(grid))])
        last = functools.reduce(jnp.logical_and, [pl.program_id(a) == grid[a] - 1 for a in range(len(grid))])

        @pl.when(first)
        def _():
            _exchange_start(x_in, x_out, sems, scatter)

        body(*c_in, *c_out, *c_scr)

        @pl.when(last)
        def _():
            _exchange_wait(x_in, x_out, sems, scatter)

    hbm = pl.BlockSpec(memory_space=pltpu.HBM)
    res = pl.pallas_call(
        hosted, name=name, grid=grid, in_specs=list(in_specs) + [hbm] * n, out_specs=list(out_specs) + [hbm] * n,
        out_shape=list(out_shape) + x_shape, scratch_shapes=list(scratch_shapes) + x_sems,
        compiler_params=_cparams(("arbitrary",) * len(grid), vmem_limit))(*args, *arrs)
    return res[:n_out], res[n_out:]


def _adamw_math(w, g, m, v):
    m_new = ADAM_B1 * m + (1.0 - ADAM_B1) * g
    v_new = ADAM_B2 * v + (1.0 - ADAM_B2) * (g * g)
    m_hat = m_new / (1.0 - ADAM_B1 ** ADAM_STEP)
    v_hat = v_new / (1.0 - ADAM_B2 ** ADAM_STEP)
    delta = -ADAM_LR * (m_hat / (jnp.sqrt(v_hat) + ADAM_EPS) + ADAM_WD * w)
    return delta, m_new, v_new


def _adamw(w, g, m, v, *, name):
    r, c = w.shape
    slots = g.ndim == 3
    tr = r
    for cand in (128, 64, 32, 16):
        if r % cand == 0 and r > cand:
            tr = cand
            break

    def body(w_ref, g_ref, m_ref, v_ref, go_ref, d_ref, mo_ref, vo_ref):
        if slots:
            gv = g_ref[0].astype(F32)
            for s in range(1, N_DEV):
                gv = gv + g_ref[s].astype(F32)
        else:
            gv = g_ref[...]
        delta, m_new, v_new = _adamw_math(w_ref[...], gv, m_ref[...], v_ref[...])
        go_ref[...] = gv
        d_ref[...] = delta
        mo_ref[...] = m_new
        vo_ref[...] = v_new

    spec = pl.BlockSpec((tr, c), lambda i: (i, 0))
    g_spec = pl.BlockSpec((N_DEV, tr, c), lambda i: (0, i, 0)) if slots else spec
    return pl.pallas_call(
        body, name=name, grid=(r // tr,), in_specs=[spec, g_spec, spec, spec], out_specs=[spec] * 4,
        out_shape=[jax.ShapeDtypeStruct((r, c), F32)] * 4, compiler_params=_cparams(("parallel",)),
    )(w, g, m, v)


def _unshard_cols(g):
    return jnp.transpose(g, (1, 0, 2)).reshape(g.shape[1], -1)

def _shard_cols(full):
    r = full.shape[0]
    return jnp.transpose(full.reshape(r, N_DEV, -1), (1, 0, 2))


def _w_in_to_mine(w_ref_layout, d):
    wd, pw, ref_off, _, _ = _in_layout(d)
    cols = []
    for n in MY_ORDER:
        piece = lax.slice_in_dim(w_ref_layout, ref_off[n], ref_off[n] + wd[n], axis=1)
        if pw[n] != wd[n]:
            piece = jnp.pad(piece, ((0, 0), (0, pw[n] - wd[n])))
        cols.append(piece)
    return jnp.concatenate(cols, axis=1)


def _w_in_from_mine(g_mine, d):
    wd, _, _, my_off, _ = _in_layout(d)
    return jnp.concatenate([lax.slice_in_dim(g_mine, my_off[n], my_off[n] + wd[n], axis=1) for n in IN_NAMES],
                           axis=1)


def _wuq_to_mine(wq):
    r = wq.shape[0]
    w3 = wq.reshape(r, MLA_HEADS, MLA_NOPE + MLA_ROPE)
    w3 = jnp.pad(w3, ((0, 0), (0, 0), (0, MLA_QK_PAD - MLA_NOPE - MLA_ROPE)))
    return w3.reshape(r, MLA_HEADS * MLA_QK_PAD)


def _wuq_from_mine(g):
    r = g.shape[0]
    return g.reshape(r, MLA_HEADS, MLA_QK_PAD)[:, :, :MLA_NOPE + MLA_ROPE].reshape(r, -1)


def _wukv_to_mine(wkv):
    r = wkv.shape[0]
    w3 = wkv.reshape(r, MLA_HEADS, MLA_NOPE + MLA_V)
    return jnp.concatenate([w3[:, :, :MLA_NOPE].reshape(r, -1), w3[:, :, MLA_NOPE:].reshape(r, -1)], axis=1)


def _wukv_from_mine(g):
    r = g.shape[0]
    kn = g[:, :MLA_HEADS * MLA_NOPE].reshape(r, MLA_HEADS, MLA_NOPE)
    vv = g[:, MLA_HEADS * MLA_NOPE:].reshape(r, MLA_HEADS, MLA_V)
    return jnp.concatenate([kn, vv], axis=2).reshape(r, -1)


COL_SHARDED = ("w_in", "gla_gk_w", "mla_wuq", "mla_wukv", "ffn_w_in")


def _gathered_to_mine(name, g, d):
    full = _unshard_cols(g) if name in COL_SHARDED else g.reshape(-1, g.shape[-1])
    if name == "w_in":
        return _w_in_to_mine(full, d)
    if name == "gla_gk_w":
        return jnp.pad(full, ((0, LANE - GLA_GATE_RANK), (0, 0)))
    if name == "mla_wuq":
        return _wuq_to_mine(full)
    if name == "mla_wukv":
        return _wukv_to_mine(full)
    return full


def _grad_to_slabs(name, g, d):
    if g.ndim == 3:
        return g.astype(BF16)
    if name == "w_in":
        g = _w_in_from_mine(g, d)
    elif name == "mla_wuq":
        g = _wuq_from_mine(g)
    elif name == "mla_wukv":
        g = _wukv_from_mine(g)
    s = _shard_cols(g) if name in COL_SHARDED else g.reshape(N_DEV, -1, g.shape[-1])
    return s.astype(BF16)


class _Rides:
    GATHER = {"in_proj": ("gla_wo", "mla_wuq", "mla_wukv", "mla_wo", "w_out"),
              "flash_fwd": ("ffn_w_in", "ffn_w_down")}
    SCATTER = {"flash_bwd": ("ffn_w_in", "ffn_w_down", "w_out", "gla_wo", "mla_wo", "gla_gk_w"),
               "in_proj_dw": ("mla_wuq", "mla_wukv"),
               "in_proj_dx": ("w_in",)}

    def __init__(self, send, d):
        self.send, self.d, self.recv = send, d, {}

    def ride(self, stage, grads):
        if stage in self.GATHER:
            return [self.send[n] for n in self.GATHER[stage]], False
        return [_grad_to_slabs(n, grads[n], self.d) for n in self.SCATTER[stage]], True

    def done(self, stage, rode, w):
        if stage in self.GATHER:
            for n, g in zip(self.GATHER[stage], rode):
                w[n] = _gathered_to_mine(n, g, self.d)
        else:
            self.recv.update(zip(self.SCATTER[stage], rode))


def kernel(x, c, positions, ada_w, ada_b, norm_mix_g, w_in, gla_gk_w, gla_gk_b, gla_onorm_g, gla_wo, mla_q_norm_g, mla_wuq, mla_kv_norm_g, mla_wukv, mla_wo, w_out, norm_ffn_g, ffn_w_in, ffn_w_down, final_norm_g, loss_target, m_ada_w, m_ada_b, m_norm_mix_g, m_w_in, m_gla_gk_w, m_gla_gk_b, m_gla_onorm_g, m_gla_wo, m_mla_q_norm_g, m_mla_wuq, m_mla_kv_norm_g, m_mla_wukv, m_mla_wo, m_w_out, m_norm_ffn_g, m_ffn_w_in, m_ffn_w_down, m_final_norm_g, v_ada_w, v_ada_b, v_norm_mix_g, v_w_in, v_gla_gk_w, v_gla_gk_b, v_gla_onorm_g, v_gla_wo, v_mla_q_norm_g, v_mla_wuq, v_mla_kv_norm_g, v_mla_wukv, v_mla_wo, v_w_out, v_norm_ffn_g, v_ffn_w_in, v_ffn_w_down, v_final_norm_g):
    wts = dict(ada_w=ada_w, ada_b=ada_b, norm_mix_g=norm_mix_g, w_in=w_in, gla_gk_w=gla_gk_w, gla_gk_b=gla_gk_b,
               gla_onorm_g=gla_onorm_g, gla_wo=gla_wo, mla_q_norm_g=mla_q_norm_g, mla_wuq=mla_wuq,
               mla_kv_norm_g=mla_kv_norm_g, mla_wukv=mla_wukv, mla_wo=mla_wo, w_out=w_out, norm_ffn_g=norm_ffn_g,
               ffn_w_in=ffn_w_in, ffn_w_down=ffn_w_down, final_norm_g=final_norm_g)
    mom_m = dict(zip(WEIGHTS, (m_ada_w, m_ada_b, m_norm_mix_g, m_w_in, m_gla_gk_w, m_gla_gk_b, m_gla_onorm_g,
                               m_gla_wo, m_mla_q_norm_g, m_mla_wuq, m_mla_kv_norm_g, m_mla_wukv, m_mla_wo, m_w_out,
                               m_norm_ffn_g, m_ffn_w_in, m_ffn_w_down, m_final_norm_g)))
    mom_v = dict(zip(WEIGHTS, (v_ada_w, v_ada_b, v_norm_mix_g, v_w_in, v_gla_gk_w, v_gla_gk_b, v_gla_onorm_g,
                               v_gla_wo, v_mla_q_norm_g, v_mla_wuq, v_mla_kv_norm_g, v_mla_wukv, v_mla_wo, v_w_out,
                               v_norm_ffn_g, v_ffn_w_in, v_ffn_w_down, v_final_norm_g)))
    seq, d = x.shape[1], x.shape[2]
    me = 4 * lax.axis_index("x") + 2 * lax.axis_index("y") + lax.axis_index("c")

    def two_d(a):
        return a.reshape(a.shape[-2], a.shape[-1]) if a.ndim >= 2 else a.reshape(1, -1)

    shard = {n: two_d(wts[n]) for n in BIG}
    send = {n: shard[n].astype(F32 if n == "gla_gk_w" else BF16) for n in BIG}
    got = _gather_once_per_chip([send["w_in"], send["gla_gk_w"], two_d(c)], name="comm_all_gather_first")
    c_all = got[2].reshape(N_DEV, d)
    w = dict(
        w_in=_gathered_to_mine("w_in", got[0], d), gla_gk_w=_gathered_to_mine("gla_gk_w", got[1], d),
        gla_gk_b=two_d(gla_gk_b), gla_onorm_g=two_d(gla_onorm_g), mla_q_norm_g=two_d(mla_q_norm_g),
        mla_kv_norm_g=two_d(mla_kv_norm_g), norm_mix_g=two_d(norm_mix_g), norm_ffn_g=two_d(norm_ffn_g),
        final_norm_g=two_d(final_norm_g))
    rides = _Rides(send, d)

    c_pad = jnp.pad(c_all, ((0, 16 - N_DEV), (0, 0)))
    (c_act,) = _rowwise(lambda cv: cv * _sigmoid(cv), [(c_pad, d, 0)], [], [(d, F32)], tile=16, name="silu_c")
    ada_w2 = two_d(ada_w)
    mod_part = _mm(c_act, ada_w2, name="mm_ada")[:N_DEV]
    (mod_all,) = _exchange([mod_part], scatter=False, name="comm_all_gather_mod")
    mod_mine = lax.dynamic_index_in_dim(mod_all, me, axis=1, keepdims=False).reshape(1, -1) + two_d(ada_b)
    mod8 = jnp.pad(mod_mine.reshape(6, d), ((0, 2), (0, 0)))

    inv_freq = ROPE_THETA ** (-jnp.arange(0, MLA_ROPE, 2, dtype=F32) / MLA_ROPE)
    ang = positions.reshape(seq, 1).astype(F32) * inv_freq[None, :]
    cos, sin, z32 = jnp.cos(ang), jnp.sin(ang), jnp.zeros((seq, 32), F32)
    tab = jnp.concatenate([cos, cos, z32, z32, -sin, z32, z32, z32, z32, sin, z32, z32], axis=1)
    loss_local, grad_x, _, small = _local_step(x.reshape(seq, d), loss_target.reshape(seq, d), tab, mod8, w, rides)

    recv = rides.recv
    pack = jnp.concatenate([small[n] for n in SMALL], axis=1)
    (pack_all,) = _exchange([pack], scatter=False, name="comm_all_gather_small")
    pack_all = pack_all.reshape(N_DEV, -1)

    res = {}
    for n in BIG:
        res[n] = _adamw(shard[n], recv[n], two_d(mom_m[n]), two_d(mom_v[n]), name="adamw_" + n)
    n_ada = ada_w2.shape[1]
    dmod_cols = lax.dynamic_slice_in_dim(pack_all[:, :6 * d], me * n_ada, n_ada, axis=1)

    def f_outer(cat, dm):
        acc = cat[:, 0:1] * dm[0:1]
        for b in range(1, N_DEV):
            acc = acc + cat[:, b:b + 1] * dm[b:b + 1]
        return acc

    (g_ada_w,) = _rowwise(f_outer, [(jnp.transpose(c_act[:N_DEV]), N_DEV, 0)], [dmod_cols], [(n_ada, F32)],
                          tile=256, name="ada_w_grad")
    res["ada_w"] = _adamw(ada_w2, g_ada_w, two_d(m_ada_w), two_d(v_ada_w), name="adamw_ada_w")
    w_small = jnp.concatenate([two_d(wts[n]) for n in SMALL], axis=1)
    m_small = jnp.concatenate([two_d(mom_m[n]) for n in SMALL], axis=1)
    v_small = jnp.concatenate([two_d(mom_v[n]) for n in SMALL], axis=1)
    small_res = _adamw(w_small, pack_all.reshape(N_DEV, 1, -1), m_small, v_small, name="adamw_small")
    off = 0
    for n in SMALL:
        width = wts[n].size
        res[n] = tuple(lax.slice_in_dim(a, off, off + width, axis=1) for a in small_res)
        off += width

    loss = lax.psum(loss_local, ("x", "y", "c"))
    outs = [loss, grad_x.reshape(x.shape)]
    for kind in range(4):
        outs += [res[n][kind].reshape(wts[n].shape) for n in WEIGHTS]
    return tuple(outs)
```

```python
import functools

import jax
import jax.numpy as jnp
from jax import lax
from jax.experimental import pallas as pl
from jax.experimental.pallas import tpu as pltpu

F32 = jnp.float32
BF16 = jnp.bfloat16

N_DEV = 8
GLA_HEADS = 4
GLA_DK = 256
GLA_DV = 512
GLA_GATE_RANK = 16
GLA_GATE_NORMALIZER = 16.0
GLA_CHUNK = 64
MLA_HEADS = 16
MLA_NOPE = 128
MLA_ROPE = 64
MLA_V = 128
MLA_QK_PAD = 256
ROPE_THETA = 10000.0
NORM_EPS = 1e-6
ATT_SCALE = (MLA_NOPE + MLA_ROPE) ** -0.5
GLA_QSCALE = GLA_DK ** -0.5

ADAM_LR = 0.001
ADAM_B1 = 0.9
ADAM_B2 = 0.999
ADAM_EPS = 1e-08
ADAM_WD = 0.01
ADAM_STEP = 10

LANE = 128
VMEM_LIMIT = 48 * 1024 * 1024
FLASH_BWD_VMEM = 58 * 1024 * 1024
MM_TILE_BYTES = 6 * 1024 * 1024
LOG2E = 1.4426950408889634
LN2 = 0.6931471805599453
NEG = -1e30

IN_NAMES = ("q", "k", "v", "g", "gk", "cq", "ckv", "kr", "ga", "gb")
MY_ORDER = ("v", "g", "ga", "gb", "q", "k", "cq", "ckv", "gk", "kr")

WEIGHTS = ("ada_w", "ada_b", "norm_mix_g", "w_in", "gla_gk_w", "gla_gk_b", "gla_onorm_g", "gla_wo",
           "mla_q_norm_g", "mla_wuq", "mla_kv_norm_g", "mla_wukv", "mla_wo", "w_out", "norm_ffn_g",
           "ffn_w_in", "ffn_w_down", "final_norm_g")
BIG = ("w_in", "gla_gk_w", "gla_wo", "mla_wuq", "mla_wukv", "mla_wo", "w_out", "ffn_w_in", "ffn_w_down")
SMALL = ("ada_b", "norm_mix_g", "gla_gk_b", "gla_onorm_g", "mla_q_norm_g", "mla_kv_norm_g", "norm_ffn_g",
         "final_norm_g")


def _in_layout(d):
    w = dict(q=d // 2, k=d // 2, v=d, g=d, gk=GLA_GATE_RANK, cq=d // 4, ckv=512, kr=MLA_ROPE, ga=d, gb=d)
    pw = {n: -(-w[n] // LANE) * LANE for n in w}
    ref_off, o = {}, 0
    for n in IN_NAMES:
        ref_off[n] = o
        o += w[n]
    my_off, o = {}, 0
    for n in MY_ORDER:
        assert o % pw[n] == 0
        my_off[n] = o
        o += pw[n]
    return w, pw, ref_off, my_off, o


def _cparams(sem=None, vmem_limit=VMEM_LIMIT):
    return pltpu.CompilerParams(dimension_semantics=sem, vmem_limit_bytes=vmem_limit)


def _dot(a, b, ca=1, cb=0):
    return lax.dot_general(a, b, (((ca,), (cb,)), ((), ())), preferred_element_type=F32)


def _tile(n, cap):
    if n <= cap:
        return n
    t = (cap // LANE) * LANE
    while t >= LANE:
        if n % t == 0:
            return t
        t -= LANE
    return n


def _mm(a, b, *, ta=False, tb=False, out_dtype=F32, name, ride=None, slabs=False):
    m, k = (a.shape[1], a.shape[0]) if ta else a.shape
    n = b.shape[0] if tb else b.shape[1]
    assert k == (b.shape[1] if tb else b.shape[0])
    wide = max(a.dtype.itemsize, b.dtype.itemsize) > 2
    tm, tn, tk = _tile(m, 1024), _tile(n, 1024), _tile(k, MM_TILE_BYTES // (1024 * (4 if wide else 2)))
    if slabs:
        tn = n // N_DEV
        assert tn % LANE == 0
    nk = k // tk

    def product(a_ref, b_ref):
        return _dot(a_ref[...].astype(BF16), b_ref[...].astype(BF16), 0 if ta else 1, 1 if tb else 0)

    def store(o_ref, val):
        if slabs:
            o_ref[0] = val.astype(o_ref.dtype)
        else:
            o_ref[...] = val.astype(o_ref.dtype)

    def body_one(a_ref, b_ref, o_ref):
        store(o_ref, product(a_ref, b_ref))

    def body_acc(a_ref, b_ref, o_ref, acc_ref):
        kk = pl.program_id(2)

        @pl.when(kk == 0)
        def _():
            acc_ref[...] = jnp.zeros_like(acc_ref)

        acc_ref[...] += product(a_ref, b_ref)

        @pl.when(kk == nk - 1)
        def _():
            store(o_ref, acc_ref[...])

    a_spec = (pl.BlockSpec((tk, tm), lambda i, j, kk: (kk, i)) if ta
              else pl.BlockSpec((tm, tk), lambda i, j, kk: (i, kk)))
    b_spec = (pl.BlockSpec((tn, tk), lambda i, j, kk: (j, kk)) if tb
              else pl.BlockSpec((tk, tn), lambda i, j, kk: (kk, j)))
    (out,), rode = _call(
        body_one if nk == 1 else body_acc, name=name, grid=(m // tm, n // tn, nk), in_specs=[a_spec, b_spec],
        out_specs=[pl.BlockSpec((1, tm, tn), lambda i, j, kk: (j, i, 0)) if slabs
                   else pl.BlockSpec((tm, tn), lambda i, j, kk: (i, j))],
        out_shape=[jax.ShapeDtypeStruct((N_DEV, m, tn) if slabs else (m, n), out_dtype)],
        scratch_shapes=[] if nk == 1 else [pltpu.VMEM((tm, tn), F32)],
        sem=("parallel", "parallel", "arbitrary"), args=(a, b), ride=ride)
    return out if ride is None else (out, rode)


def _rowwise(fn, rows, vecs, outs, sums=(), *, tile, name):
    t = rows[0][0].shape[0]
    tile = min(tile, t)
    assert t % tile == 0
    n_rows, n_vecs, n_outs = len(rows), len(vecs), len(outs)

    def body(*refs):
        ins = [r[...].astype(F32) for r in refs[:n_rows + n_vecs]]
        res = fn(*ins)
        if not isinstance(res, (tuple, list)):
            res = (res,)
        out_refs = refs[n_rows + n_vecs:]
        for r, val in zip(out_refs[:n_outs], res[:n_outs]):
            r[...] = val.astype(r.dtype)
        if sums:
            first = pl.program_id(0) == 0
            for r, val in zip(out_refs[n_outs:], res[n_outs:]):
                @pl.when(first)
                def _(r=r):
                    r[...] = jnp.zeros_like(r)
                r[...] += val

    in_specs = [pl.BlockSpec((tile, w), lambda i, cb=cb: (i, cb)) for (_, w, cb) in rows]
    in_specs += [pl.BlockSpec(v.shape, lambda i: (0, 0)) for v in vecs]
    out_specs = [pl.BlockSpec((tile, w), lambda i: (i, 0)) for (w, _) in outs]
    out_specs += [pl.BlockSpec((1, w), lambda i: (0, 0)) for w in sums]
    out_shape = [jax.ShapeDtypeStruct((t, w), dt) for (w, dt) in outs]
    out_shape += [jax.ShapeDtypeStruct((1, w), F32) for w in sums]
    res = pl.pallas_call(
        body, name=name, grid=(t // tile,), in_specs=in_specs, out_specs=out_specs, out_shape=out_shape,
        compiler_params=_cparams(("arbitrary",)),
    )(*[r[0] for r in rows], *vecs)
    return res


def _rstd(x):
    return lax.rsqrt(jnp.mean(x * x, axis=-1, keepdims=True) + NORM_EPS)


def _sigmoid(x):
    return 1.0 / (1.0 + jnp.exp(-x))


def _rms_bwd(dxh, xh, r):
    return r * (dxh - xh * jnp.mean(dxh * xh, axis=-1, keepdims=True))


def _rope(t, tab, sign):
    cosf, sin_a, sin_b = tab[:, :LANE], tab[:, LANE:2 * LANE], tab[:, 2 * LANE:]
    return t * cosf + sign * (pltpu.roll(t, 96, 1) * sin_a + pltpu.roll(t, 32, 1) * sin_b)


def _split3(x):
    hi = x.astype(BF16)
    r1 = x - hi.astype(F32)
    mid = r1.astype(BF16)
    lo = (r1 - mid.astype(F32)).astype(BF16)
    return hi, mid, lo


def _tri_sum(tri_bf16, x):
    hi, mid, lo = _split3(x)
    return _dot(tri_bf16, hi) + _dot(tri_bf16, mid) + _dot(tri_bf16, lo)


def _dot_nt2(a, b):
    a_hi = a.astype(BF16)
    a_lo = (a - a_hi.astype(F32)).astype(BF16)
    b_hi = b.astype(BF16)
    b_lo = (b - b_hi.astype(F32)).astype(BF16)
    return _dot(a_hi, b_hi, 1, 1) + _dot(a_hi, b_lo, 1, 1) + _dot(a_lo, b_hi, 1, 1)


def _gla_specs(t, rows, lay, reverse):
    nb = t // rows
    blk = (lambda i: nb - 1 - i) if reverse else (lambda i: i)
    qb, kb = lay["q"] // GLA_DK, lay["k"] // GLA_DK
    vb = lay["v"] // GLA_DV
    return [
        pl.BlockSpec((rows, GLA_DK), lambda h, i: (blk(i), qb + h)),
        pl.BlockSpec((rows, GLA_DK), lambda h, i: (blk(i), kb + h)),
        pl.BlockSpec((rows, GLA_DV), lambda h, i: (blk(i), vb + h)),
        pl.BlockSpec((rows, GLA_DK), lambda h, i: (blk(i), h)),
    ], blk


def _gla_fwd(p, la, lay):
    t = p.shape[0]
    rows = min(512, t)
    nb, nc = t // rows, rows // GLA_CHUNK
    c64 = GLA_CHUNK

    def body(q_ref, k_ref, v_ref, la_ref, o_ref, st_ref, s_ref):
        @pl.when(pl.program_id(1) == 0)
        def _():
            s_ref[...] = jnp.zeros_like(s_ref)

        r = lax.broadcasted_iota(jnp.int32, (c64, c64), 0)
        cc = lax.broadcasted_iota(jnp.int32, (c64, c64), 1)
        tril = cc <= r
        tril_b = tril.astype(BF16)
        for c in range(nc):
            sl = pl.ds(c * c64, c64)
            b = _tri_sum(tril_b, la_ref[sl, :])
            b_last = b[c64 - 1:c64, :]
            q = q_ref[sl, :].astype(F32) * GLA_QSCALE
            k = k_ref[sl, :].astype(F32)
            v = v_ref[sl, :].astype(BF16)
            qt_f = q * jnp.exp(b)
            qt = qt_f.astype(BF16)
            kh = (k * jnp.exp(b_last - b)).astype(BF16)
            s_prev = s_ref[...]
            st_ref[0, c] = s_prev
            att = jnp.where(tril, _dot_nt2(qt_f, k * jnp.exp(-b)), 0.0)
            o_ref[sl, :] = _dot(qt, s_prev.astype(BF16), 1, 1) + _dot(att.astype(BF16), v)
            s_ref[...] = s_prev * jnp.exp(b_last) + _dot(v, kh, 0, 0)

    in_specs, _ = _gla_specs(t, rows, lay, False)
    return pl.pallas_call(
        body, name="gla_fwd", grid=(GLA_HEADS, nb), in_specs=in_specs,
        out_specs=[pl.BlockSpec((rows, GLA_DV), lambda h, i: (i, h)),
                   pl.BlockSpec((1, nc, GLA_DV, GLA_DK), lambda h, i: (h, i, 0, 0))],
        out_shape=[jax.ShapeDtypeStruct((t, GLA_HEADS * GLA_DV), F32),
                   jax.ShapeDtypeStruct((GLA_HEADS, t // c64, GLA_DV, GLA_DK), F32)],
        scratch_shapes=[pltpu.VMEM((GLA_DV, GLA_DK), F32)],
        compiler_params=_cparams(("parallel", "arbitrary")),
    )(p, p, p, la)


def _gla_bwd(p, la, do, states, lay):
    t = p.shape[0]
    rows = min(512, t)
    nb, nc = t // rows, rows // GLA_CHUNK
    c64 = GLA_CHUNK

    def body(q_ref, k_ref, v_ref, la_ref, do_ref, st_ref, dq_ref, dk_ref, dv_ref, dla_ref, ds_ref):
        @pl.when(pl.program_id(1) == 0)
        def _():
            ds_ref[...] = jnp.zeros_like(ds_ref)

        r = lax.broadcasted_iota(jnp.int32, (c64, c64), 0)
        cc = lax.broadcasted_iota(jnp.int32, (c64, c64), 1)
        tril = cc <= r
        tril_b = tril.astype(BF16)
        triu_b = (cc >= r).astype(BF16)
        for c in reversed(range(nc)):
            sl = pl.ds(c * c64, c64)
            b = _tri_sum(tril_b, la_ref[sl, :])
            b_last = b[c64 - 1:c64, :]
            eb, enb, ebl_b, ebl = jnp.exp(b), jnp.exp(-b), jnp.exp(b_last - b), jnp.exp(b_last)
            k = k_ref[sl, :].astype(F32)
            qt_f = q_ref[sl, :].astype(F32) * GLA_QSCALE * eb
            kt_f = k * enb
            kh_f = k * ebl_b
            qt, kt, kh = qt_f.astype(BF16), kt_f.astype(BF16), kh_f.astype(BF16)
            v_f = v_ref[sl, :].astype(F32)
            dout_f = do_ref[sl, :]
            v, dout = v_f.astype(BF16), dout_f.astype(BF16)
            s_prev = st_ref[0, c]
            ds_next = ds_ref[...]
            ds_next_b = ds_next.astype(BF16)
            att = jnp.where(tril, _dot_nt2(qt_f, kt_f), 0.0).astype(BF16)
            datt = jnp.where(tril, _dot_nt2(dout_f, v_f), 0.0).astype(BF16)
            dqt = _dot(dout, s_prev.astype(BF16)) + _dot(datt, kt)
            dkt = _dot(datt, qt, 0, 0)
            dv = _dot(att, dout, 0, 0) + _dot(kh, ds_next_b, 1, 1)
            dkh = _dot(v, ds_next_b)
            d_ebl = jnp.sum(ds_next * s_prev, axis=0, keepdims=True)
            ds_ref[...] = ds_next * ebl + _dot(dout, qt, 0, 0)
            db = dqt * qt_f - dkt * kt_f - dkh * kh_f
            db_last = ebl * d_ebl + jnp.sum(dkh * kh_f, axis=0, keepdims=True)
            dq_ref[sl, :] = (dqt * eb * GLA_QSCALE).astype(dq_ref.dtype)
            dk_ref[sl, :] = (dkt * enb + dkh * ebl_b).astype(dk_ref.dtype)
            dv_ref[sl, :] = dv.astype(dv_ref.dtype)
            dla_ref[sl, :] = _tri_sum(triu_b, db) + db_last

    in_specs, blk = _gla_specs(t, rows, lay, True)
    in_specs += [pl.BlockSpec((rows, GLA_DV), lambda h, i: (blk(i), h)),
                 pl.BlockSpec((1, nc, GLA_DV, GLA_DK), lambda h, i: (h, blk(i), 0, 0))]
    dk_spec = pl.BlockSpec((rows, GLA_DK), lambda h, i: (blk(i), h))
    return pl.pallas_call(
        body, name="gla_bwd", grid=(GLA_HEADS, nb), in_specs=in_specs,
        out_specs=[dk_spec, dk_spec, pl.BlockSpec((rows, GLA_DV), lambda h, i: (blk(i), h)), dk_spec],
        out_shape=[jax.ShapeDtypeStruct((t, GLA_HEADS * GLA_DK), BF16),
                   jax.ShapeDtypeStruct((t, GLA_HEADS * GLA_DK), BF16),
                   jax.ShapeDtypeStruct((t, GLA_HEADS * GLA_DV), BF16),
                   jax.ShapeDtypeStruct((t, GLA_HEADS * GLA_DK), F32)],
        scratch_shapes=[pltpu.VMEM((GLA_DV, GLA_DK), F32)],
        compiler_params=_cparams(("parallel", "arbitrary")),
    )(p, p, p, la, do, states)


def _diag_mask(rows, cols, row0):
    row = row0 + lax.broadcasted_iota(jnp.int32, (rows, cols), 0)
    col = lax.broadcasted_iota(jnp.int32, (rows, cols), 1)
    return col <= row


QK_SPARE = MLA_NOPE + MLA_ROPE
N_SPARE = 3


def _with_spare(x, col, lane0):
    lane = lax.broadcasted_iota(jnp.int32, x.shape, 1)
    for n, term in enumerate(_split3(col)):
        x = jnp.where(lane == lane0 + n, term, x)
    return x


def _spare_ones(shape, lane0):
    lane = lax.broadcasted_iota(jnp.int32, shape, 1)
    return ((lane >= lane0) & (lane < lane0 + N_SPARE)).astype(F32)


def _flash_tiles(t):
    tq = min(1024, t)
    halves = 2 if tq % 32 == 0 else 1
    return tq, t // tq, halves, tq // halves


def _flash_fwd(q, k, vx, ride=None):
    t = q.shape[0]
    tq, nq, halves, hr = _flash_tiles(t)
    dqk, dv = MLA_QK_PAD, MLA_V

    def body(q_ref, k_ref, v_ref, o_ref, qa_ref, m_ref, acc_ref, s_ref):
        i = pl.program_id(1)
        m_ref[...] = jnp.full_like(m_ref, NEG)
        acc_ref[...] = jnp.zeros_like(acc_ref)

        def scores(j, slot):
            kb = k_ref[pl.ds(pl.multiple_of(j * tq, tq), tq), :]
            for hh in range(halves):
                s_ref[slot, pl.ds(hh * hr, hr), :] = _dot(q_ref[pl.ds(hh * hr, hr), :], kb, 1, 1)

        def consume(j, slot, masked):
            vb = v_ref[pl.ds(pl.multiple_of(j * tq, tq), tq), :]
            for hh in range(halves):
                rs = pl.ds(hh * hr, hr)
                s = s_ref[slot, rs, :]
                if masked:
                    s = jnp.where(_diag_mask(hr, tq, hh * hr), s, NEG)
                m_old = m_ref[rs, :]
                m_new = jnp.maximum(m_old, jnp.max(s, axis=1, keepdims=True))
                pr = jnp.exp2(s - m_new)
                acc_ref[rs, :] = jnp.exp2(m_old - m_new) * acc_ref[rs, :] + _dot(pr.astype(BF16), vb)
                m_ref[rs, :] = m_new

        def two_blocks(jj, carry):
            scores(2 * jj + 1, 1)
            consume(2 * jj, 0, False)
            scores(2 * jj + 2, 0)
            consume(2 * jj + 1, 1, False)
            return carry

        scores(0, 0)
        lax.fori_loop(0, i // 2, two_blocks, 0)

        @pl.when(i % 2 == 0)
        def _():
            consume(i, 0, True)

        @pl.when(i % 2 == 1)
        def _():
            scores(i, 1)
            consume(i - 1, 0, False)
            consume(i, 1, True)

        acc = acc_ref[...]
        l = acc[:, dv:dv + 1]
        o_ref[...] = acc[:, :dv] / l
        qa_ref[...] = _with_spare(q_ref[...], -(m_ref[...] + jnp.log(l) * LOG2E), QK_SPARE)

    outs, rode = _call(
        body, name="mla_flash_fwd", grid=(MLA_HEADS, nq),
        in_specs=[pl.BlockSpec((tq, dqk), lambda h, i: (i, h)),
                  pl.BlockSpec((t, dqk), lambda h, i: (0, h)),
                  pl.BlockSpec((t, 2 * dv), lambda h, i: (0, h))],
        out_specs=[pl.BlockSpec((tq, dv), lambda h, i: (i, h)),
                   pl.BlockSpec((tq, dqk), lambda h, i: (i, h))],
        out_shape=[jax.ShapeDtypeStruct((t, MLA_HEADS * dv), F32),
                   jax.ShapeDtypeStruct((t, MLA_HEADS * dqk), BF16)],
        scratch_shapes=[pltpu.VMEM((tq, 1), F32), pltpu.VMEM((tq, 2 * dv), F32), pltpu.VMEM((2, tq, tq), F32)],
        sem=("parallel", "arbitrary"), args=(q, k, vx), ride=ride)
    return tuple(outs) if ride is None else (tuple(outs), rode)


def _flash_bwd(qa, k, vx, doa, ride=None):
    t = qa.shape[0]
    tq, nq, halves, hr = _flash_tiles(t)
    dqk, dv = MLA_QK_PAD, MLA_V

    def body(k_ref, v_ref, q_ref, do_ref, dq_ref, dk_ref, dv_ref, dq_acc, dk_acc, dv_acc):
        j = pl.program_id(1)

        @pl.when(j == 0)
        def _():
            dq_acc[...] = jnp.zeros_like(dq_acc)

        kb = k_ref[...]
        vb = v_ref[...]
        dk_acc[...] = jnp.zeros_like(dk_acc)
        dv_acc[...] = jnp.zeros_like(dv_acc)

        def step(i, masked):
            for hh in range(halves):
                rs = pl.ds(pl.multiple_of(i * tq + hh * hr, hr), hr)
                qb = q_ref[rs, :]
                dout = do_ref[rs, :]
                nc = (hh + 1) * hr if masked else tq
                s = _dot(qb, kb[:nc], 1, 1)
                if masked:
                    s = jnp.where(_diag_mask(hr, nc, hh * hr), s, NEG)
                pr = jnp.exp2(s)
                ds = (pr * _dot(dout, vb[:nc], 1, 1)).astype(BF16)
                dv_acc[pl.ds(0, nc), :] += _dot(pr.astype(BF16), dout, 0, 0)
                dk_acc[pl.ds(0, nc), :] += _dot(ds, qb, 0, 0)
                dq_acc[rs, :] += _dot(ds, kb[:nc])

        def loop_body(i, carry):
            step(i, False)
            return carry

        step(j, True)
        lax.fori_loop(j + 1, nq, loop_body, 0)
        dk_ref[...] = (dk_acc[...] * LN2).astype(dk_ref.dtype)
        dv_ref[...] = dv_acc[:, :dv].astype(dv_ref.dtype)

        @pl.when(j == nq - 1)
        def _():
            dq_ref[...] = (dq_acc[...] * ATT_SCALE).astype(dq_ref.dtype)

    outs, rode = _call(
        body, name="mla_flash_bwd", grid=(MLA_HEADS, nq),
        in_specs=[pl.BlockSpec((tq, dqk), lambda h, j: (j, h)),
                  pl.BlockSpec((tq, 2 * dv), lambda h, j: (j, h)),
                  pl.BlockSpec((t, dqk), lambda h, j: (0, h)),
                  pl.BlockSpec((t, 2 * dv), lambda h, j: (0, h))],
        out_specs=[pl.BlockSpec((t, dqk), lambda h, j: (0, h)),
                   pl.BlockSpec((tq, dqk), lambda h, j: (j, h)),
                   pl.BlockSpec((tq, dv), lambda h, j: (j, h))],
        out_shape=[jax.ShapeDtypeStruct((t, MLA_HEADS * dqk), BF16),
                   jax.ShapeDtypeStruct((t, MLA_HEADS * dqk), BF16),
                   jax.ShapeDtypeStruct((t, MLA_HEADS * dv), BF16)],
        scratch_shapes=[pltpu.VMEM((t, dqk), F32), pltpu.VMEM((tq, dqk), F32), pltpu.VMEM((tq, 2 * dv), F32)],
        sem=("parallel", "arbitrary"), args=(k, vx, qa, doa), ride=ride, vmem_limit=FLASH_BWD_VMEM)
    return tuple(outs) if ride is None else (tuple(outs), rode)


class _NoRides:
    def ride(self, stage, grads):
        return None

    def done(self, stage, rode, w):
        pass


def _local_step(x, target, tab, mod8, w, rides=None):
    t, d = x.shape
    rides = rides or _NoRides()
    big = {}

    def riding(stage, fn):
        r = rides.ride(stage, big)
        res = fn(r)
        if r is None:
            return res
        rides.done(stage, res[1], w)
        return res[0]
    _, pw, _, lay, _ = _in_layout(d)
    ffn = ((8 * d // 3 + 255) // 256) * 256

    def blk(arr, name):
        return (arr, pw[name], lay[name] // pw[name])

    def full(arr):
        return (arr, arr.shape[1], 0)

    g1, g2, g3 = w["norm_mix_g"], w["norm_ffn_g"], w["final_norm_g"]

    def f_ln1(xv, mod, g):
        return (xv * _rstd(xv) * g) * (1.0 + mod[1:2]) + mod[0:1]

    (h,) = _rowwise(f_ln1, [full(x)], [mod8, g1], [(d, BF16)], tile=256, name="ln1_modulate")
    p = riding("in_proj", lambda r: _mm(h, w["w_in"], name="mm_in_proj", ride=r))

    def f_gk(pgk, gkw, gkb):
        z = _dot(pgk.astype(BF16), gkw.astype(BF16)) + gkb
        return (jnp.minimum(z, 0.0) - jnp.log(1.0 + jnp.exp(-jnp.abs(z)))) / GLA_GATE_NORMALIZER

    (la,) = _rowwise(f_gk, [blk(p, "gk")], [w["gla_gk_w"], w["gla_gk_b"]], [(GLA_HEADS * GLA_DK, F32)],
                     tile=512, name="gla_gate")
    o_gla, states = _gla_fwd(p, la, lay)

    def f_gla_out(ov, pg, g):
        parts = []
        for hh in range(GLA_HEADS):
            oh = ov[:, hh * GLA_DV:(hh + 1) * GLA_DV]
            ph = pg[:, hh * GLA_DV:(hh + 1) * GLA_DV]
            parts.append(oh * _rstd(oh) * g * (ph * _sigmoid(ph)))
        return jnp.concatenate(parts, axis=1)

    (o_n,) = _rowwise(f_gla_out, [full(o_gla), blk(p, "g")], [w["gla_onorm_g"]], [(d, BF16)], tile=256,
                      name="gla_out_norm")
    y_gla = _mm(o_n, w["gla_wo"], out_dtype=BF16, name="mm_gla_wo")

    def f_mla_prep(cq, ckv, kr, tb, gq, gkv):
        return cq * _rstd(cq) * gq, ckv * _rstd(ckv) * gkv, _rope(kr, tb, 1.0)

    cqn, ckvn, krr = _rowwise(f_mla_prep, [blk(p, "cq"), blk(p, "ckv"), blk(p, "kr"), full(tab)],
                              [w["mla_q_norm_g"], w["mla_kv_norm_g"]],
                              [(pw["cq"], BF16), (pw["ckv"], BF16), (LANE, F32)], tile=512, name="mla_prep")
    qlat = _mm(cqn, w["mla_wuq"], out_dtype=BF16, name="mm_mla_wuq")
    kvl = _mm(ckvn, w["mla_wukv"], out_dtype=BF16, name="mm_mla_wukv")
    hv = MLA_HEADS * MLA_V

    def f_qkv(ql, kn, vv, kr, tb):
        qs, ks, vx = [], [], []
        kr1 = kr + _spare_ones(kr.shape, MLA_ROPE)
        ones = _spare_ones(kr.shape, 0)
        for hh in range(MLA_HEADS):
            o0 = hh * MLA_QK_PAD
            qs += [ql[:, o0:o0 + LANE], _rope(ql[:, o0 + LANE:o0 + 2 * LANE], tb, 1.0)]
            ks += [kn[:, hh * LANE:(hh + 1) * LANE], kr1]
            vx += [vv[:, hh * MLA_V:(hh + 1) * MLA_V], ones]
        return (jnp.concatenate(qs, axis=1) * (ATT_SCALE * LOG2E), jnp.concatenate(ks, axis=1),
                jnp.concatenate(vx, axis=1))

    qa, ka, vxa = _rowwise(f_qkv, [full(qlat), (kvl, hv, 0), (kvl, hv, 1), full(krr), full(tab)], [],
                           [(MLA_HEADS * MLA_QK_PAD, BF16), (MLA_HEADS * MLA_QK_PAD, BF16), (2 * hv, BF16)],
                           tile=256, name="mla_qkv_build")
    o_mla, qa_lse = riding("flash_fwd", lambda r: _flash_fwd(qa, ka, vxa, ride=r))
    y_mla = _mm(o_mla, w["mla_wo"], out_dtype=BF16, name="mm_mla_wo")

    def f_merge(yg, ym, ga, gb):
        return _sigmoid(ga) * yg + _sigmoid(gb) * ym

    (merged,) = _rowwise(f_merge, [full(y_gla), full(y_mla), blk(p, "ga"), blk(p, "gb")], [], [(d, BF16)],
                         tile=256, name="merge")
    mix = _mm(merged, w["w_out"], name="mm_w_out")

    def f_res_ln2(xv, mx, mod, g):
        x2v = xv + mod[2:3] * mx
        return x2v, (x2v * _rstd(x2v) * g) * (1.0 + mod[4:5]) + mod[3:4]

    x2, h2 = _rowwise(f_res_ln2, [full(x), full(mix)], [mod8, g2], [(d, F32), (d, BF16)], tile=256,
                      name="res_ln2_modulate")
    gu = _mm(h2, w["ffn_w_in"], out_dtype=BF16, name="mm_ffn_in")

    def f_swiglu(gv, uv):
        return gv * _sigmoid(gv) * uv

    (act,) = _rowwise(f_swiglu, [(gu, ffn, 0), (gu, ffn, 1)], [], [(ffn, BF16)], tile=128, name="swiglu")
    f_out = _mm(act, w["ffn_w_down"], name="mm_ffn_down")

    def f_head(x2v, fv, tg, mod, g):
        x3 = x2v + mod[5:6] * fv
        r = _rstd(x3)
        xh = x3 * r
        e = xh * g - tg
        loss_rows = 0.5 * jnp.mean(e * e, axis=-1, keepdims=True)
        dy = e * (1.0 / d)
        dx3 = _rms_bwd(dy * g, xh, r)
        loss = jnp.broadcast_to(jnp.sum(loss_rows, axis=0, keepdims=True), (1, LANE))
        return (dx3, dx3 * mod[5:6], loss, jnp.sum(dy * xh, axis=0, keepdims=True),
                jnp.sum(dx3 * fv, axis=0, keepdims=True))

    dx3, df, loss_v, dg3, dgate_f = _rowwise(f_head, [full(x2), full(f_out), full(target)], [mod8, g3],
                                             [(d, F32), (d, BF16)], [LANE, d, d], tile=256, name="loss_head")
    da = _mm(df, w["ffn_w_down"], tb=True, out_dtype=BF16, name="mm_ffn_down_dx")
    big["ffn_w_down"] = _mm(act, df, ta=True, out_dtype=BF16, name="mm_ffn_down_dw")

    def f_swiglu_bwd(gv, uv, dav):
        sg = _sigmoid(gv)
        return jnp.concatenate([dav * uv * (sg * (1.0 + gv * (1.0 - sg))), dav * (gv * sg)], axis=1)

    (dgu,) = _rowwise(f_swiglu_bwd, [(gu, ffn, 0), (gu, ffn, 1), full(da)], [], [(2 * ffn, BF16)], tile=128,
                      name="swiglu_bwd")
    dh2 = _mm(dgu, w["ffn_w_in"], tb=True, name="mm_ffn_in_dx")
    big["ffn_w_in"] = _mm(h2, dgu, ta=True, out_dtype=BF16, slabs=True, name="mm_ffn_in_dw")

    def f_ln2_bwd(x2v, dh, dx3v, mx, mod, g):
        r = _rstd(x2v)
        xh = x2v * r
        dn = dh * (1.0 + mod[4:5])
        dx2 = dx3v + _rms_bwd(dn * g, xh, r)
        return (dx2, dx2 * mod[2:3],
                jnp.sum(dh * (xh * g), axis=0, keepdims=True), jnp.sum(dh, axis=0, keepdims=True),
                jnp.sum(dn * xh, axis=0, keepdims=True), jnp.sum(dx2 * mx, axis=0, keepdims=True))

    dx2, dmix, dscale_f, dshift_f, dg2, dgate_m = _rowwise(
        f_ln2_bwd, [full(x2), full(dh2), full(dx3), full(mix)], [mod8, g2], [(d, F32), (d, BF16)],
        [d, d, d, d], tile=256, name="ln2_bwd")
    dmerged = _mm(dmix, w["w_out"], tb=True, out_dtype=BF16, name="mm_w_out_dx")
    big["w_out"] = _mm(merged, dmix, ta=True, out_dtype=BF16, name="mm_w_out_dw")

    def f_merge_bwd(dm, yg, ym, ga, gb):
        sa, sb = _sigmoid(ga), _sigmoid(gb)
        return dm * sa, dm * sb, dm * yg * sa * (1.0 - sa), dm * ym * sb * (1.0 - sb)

    dy_gla, dy_mla, dp_ga, dp_gb = _rowwise(
        f_merge_bwd, [full(dmerged), full(y_gla), full(y_mla), blk(p, "ga"), blk(p, "gb")], [],
        [(d, BF16)] * 4, tile=256, name="merge_bwd")
    do_n = _mm(dy_gla, w["gla_wo"], tb=True, name="mm_gla_wo_dx")
    big["gla_wo"] = _mm(o_n, dy_gla, ta=True, out_dtype=BF16, name="mm_gla_wo_dw")
    do_m = _mm(dy_mla, w["mla_wo"], tb=True, out_dtype=BF16, name="mm_mla_wo_dx")
    big["mla_wo"] = _mm(o_mla, dy_mla, ta=True, out_dtype=BF16, name="mm_mla_wo_dw")

    def f_gla_out_bwd(don, ov, pg, g):
        dos, dpgs = [], []
        dg = jnp.zeros((1, GLA_DV), F32)
        for hh in range(GLA_HEADS):
            sl = slice(hh * GLA_DV, (hh + 1) * GLA_DV)
            oh, ph, dn = ov[:, sl], pg[:, sl], don[:, sl]
            r = _rstd(oh)
            xh = oh * r
            sg = _sigmoid(ph)
            dpre = dn * (ph * sg)
            dg = dg + jnp.sum(dpre * xh, axis=0, keepdims=True)
            dos.append(_rms_bwd(dpre * g, xh, r))
            dpgs.append(dn * (xh * g) * (sg * (1.0 + ph * (1.0 - sg))))
        return jnp.concatenate(dos, axis=1), jnp.concatenate(dpgs, axis=1), dg

    do_gla, dp_g, dg_on = _rowwise(f_gla_out_bwd, [full(do_n), full(o_gla), blk(p, "g")], [w["gla_onorm_g"]],
                                   [(d, F32), (d, BF16)], [GLA_DV], tile=256, name="gla_out_norm_bwd")
    dp_q, dp_k, dp_v, dla = _gla_bwd(p, la, do_gla, states, lay)

    def f_gk_bwd(dlav, pgk, gkw, gkb):
        z = _dot(pgk.astype(BF16), gkw.astype(BF16)) + gkb
        dz = dlav * (1.0 / GLA_GATE_NORMALIZER) * _sigmoid(-z)
        return dz, _dot(dz.astype(BF16), gkw.astype(BF16), 1, 1), jnp.sum(dz, axis=0, keepdims=True)

    dz, dp_gk, dgk_b = _rowwise(f_gk_bwd, [full(dla), blk(p, "gk")], [w["gla_gk_w"], w["gla_gk_b"]],
                                [(GLA_HEADS * GLA_DK, BF16), (LANE, BF16)], [GLA_HEADS * GLA_DK], tile=512,
                                name="gla_gate_bwd")
    p_gk = lax.slice_in_dim(p, lay["gk"], lay["gk"] + LANE, axis=1)
    big["gla_gk_w"] = _mm(p_gk, dz, ta=True, name="mm_gla_gk_dw")[:GLA_GATE_RANK]

    def f_do_aug(dom, om):
        parts = []
        for hh in range(MLA_HEADS):
            dh_ = dom[:, hh * MLA_V:(hh + 1) * MLA_V]
            delta = jnp.sum(dh_ * om[:, hh * MLA_V:(hh + 1) * MLA_V], axis=1, keepdims=True)
            parts += [dh_.astype(BF16), _with_spare(jnp.zeros(dh_.shape, BF16), -delta, 0)]
        return jnp.concatenate(parts, axis=1)

    (doa,) = _rowwise(f_do_aug, [full(do_m), full(o_mla)], [], [(2 * hv, BF16)], tile=256, name="mla_do_delta")
    dqa, dka, dva = riding("flash_bwd", lambda r: _flash_bwd(qa_lse, ka, vxa, doa, ride=r))

    def f_qkv_bwd(dq, dk, dvv, tb):
        dqs, dkn = [], []
        dkr = jnp.zeros((dq.shape[0], LANE), F32)
        for hh in range(MLA_HEADS):
            o0 = hh * MLA_QK_PAD
            dqs += [dq[:, o0:o0 + LANE], _rope(dq[:, o0 + LANE:o0 + 2 * LANE], tb, -1.0)]
            dkn.append(dk[:, o0:o0 + LANE])
            dkr = dkr + dk[:, o0 + LANE:o0 + 2 * LANE]
        return jnp.concatenate(dqs, axis=1), jnp.concatenate(dkn + [dvv], axis=1), dkr

    dqlat, dkvl, dkrr = _rowwise(f_qkv_bwd, [full(dqa), full(dka), full(dva), full(tab)], [],
                                 [(MLA_HEADS * MLA_QK_PAD, BF16), (2 * hv, BF16), (LANE, F32)], tile=256,
                                 name="mla_qkv_build_bwd")
    dcqn = _mm(dqlat, w["mla_wuq"], tb=True, name="mm_mla_wuq_dx")
    big["mla_wuq"] = _mm(cqn, dqlat, ta=True, out_dtype=BF16, name="mm_mla_wuq_dw")
    dckvn = _mm(dkvl, w["mla_wukv"], tb=True, name="mm_mla_wukv_dx")
    big["mla_wukv"] = _mm(ckvn, dkvl, ta=True, out_dtype=BF16, name="mm_mla_wukv_dw")

    def f_mla_prep_bwd(dq, dkv, dkr, cq, ckv, tb, gq, gkv):
        rq, rk = _rstd(cq), _rstd(ckv)
        xq, xk = cq * rq, ckv * rk
        return (_rms_bwd(dq * gq, xq, rq), _rms_bwd(dkv * gkv, xk, rk), _rope(dkr, tb, -1.0),
                jnp.sum(dq * xq, axis=0, keepdims=True), jnp.sum(dkv * xk, axis=0, keepdims=True))

    dp_cq, dp_ckv, dp_kr, dg_q, dg_kv = _rowwise(
        f_mla_prep_bwd, [full(dcqn), full(dckvn), full(dkrr), blk(p, "cq"), blk(p, "ckv"), full(tab)],
        [w["mla_q_norm_g"], w["mla_kv_norm_g"]], [(pw["cq"], BF16), (pw["ckv"], BF16), (LANE, BF16)],
        [pw["cq"], pw["ckv"]], tile=512, name="mla_prep_bwd")

    pieces = dict(v=dp_v, g=dp_g, ga=dp_ga, gb=dp_gb, q=dp_q, k=dp_k, cq=dp_cq, ckv=dp_ckv, gk=dp_gk, kr=dp_kr)
    dp = jnp.concatenate([pieces[n] for n in MY_ORDER], axis=1)
    big["w_in"] = riding("in_proj_dw", lambda r: _mm(h, dp, ta=True, out_dtype=BF16, name="mm_in_proj_dw", ride=r))
    dh = riding("in_proj_dx", lambda r: _mm(dp, w["w_in"], tb=True, name="mm_in_proj_dx", ride=r))

    def f_ln1_bwd(xv, dhv, dx2v, mod, g):
        r = _rstd(xv)
        xh = xv * r
        dn = dhv * (1.0 + mod[1:2])
        return (dx2v + _rms_bwd(dn * g, xh, r),
                jnp.sum(dhv * (xh * g), axis=0, keepdims=True), jnp.sum(dhv, axis=0, keepdims=True),
                jnp.sum(dn * xh, axis=0, keepdims=True))

    grad_x, dscale_m, dshift_m, dg1 = _rowwise(f_ln1_bwd, [full(x), full(dh), full(dx2)], [mod8, g1],
                                               [(d, F32)], [d, d, d], tile=256, name="ln1_bwd")

    dmod = jnp.concatenate([dshift_m, dscale_m, dgate_m, dshift_f, dscale_f, dgate_f], axis=1)
    small = dict(ada_b=dmod, norm_mix_g=dg1, gla_gk_b=dgk_b, gla_onorm_g=dg_on, mla_q_norm_g=dg_q,
                 mla_kv_norm_g=dg_kv, norm_ffn_g=dg2, final_norm_g=dg3)
    return loss_v[0, 0], grad_x, big, small


N_PEER = N_DEV - 1


def _exchange_copies(ins, outs, sems, scatter):
    send_sems, recv_sems, local_sems = sems
    x, y, c = lax.axis_index("x"), lax.axis_index("y"), lax.axis_index("c")
    me = 4 * x + 2 * y + c
    peers = []
    for rel in range(1, N_DEV):
        px = 1 - x if rel & 4 else x
        py = 1 - y if rel & 2 else y
        pc = 1 - c if rel & 1 else c
        peers.append(((px, py, pc), 4 * px + 2 * py + pc))

    def remote(a, k, src_slot, dst_slot):
        src = ins[a].at[src_slot] if scatter else ins[a]
        return pltpu.make_async_remote_copy(
            src_ref=src, dst_ref=outs[a].at[dst_slot], send_sem=send_sems.at[a * N_PEER + k],
            recv_sem=recv_sems.at[a * N_PEER + k], device_id=peers[k][0], device_id_type=pl.DeviceIdType.MESH)

    local, sends, recvs = [], [], []
    for a in range(len(ins)):
        src = ins[a].at[me] if scatter else ins[a]
        local.append(pltpu.make_async_copy(src, outs[a].at[me], local_sems.at[a]))
        for k in range(N_PEER):
            sends.append(remote(a, k, peers[k][1], me))
            recvs.append(remote(a, k, peers[k][1], peers[k][1]))
    return local, sends, recvs


def _exchange_start(ins, outs, sems, scatter):
    local, sends, _ = _exchange_copies(ins, outs, sems, scatter)
    for cp in local + sends:
        cp.start()


def _exchange_wait(ins, outs, sems, scatter):
    local, sends, recvs = _exchange_copies(ins, outs, sems, scatter)
    for cp in recvs:
        cp.wait_recv()
    for cp in sends:
        cp.wait_send()
    for cp in local:
        cp.wait()


def _exchange_shapes(arrs, scatter):
    n = len(arrs)
    out_shape = [jax.ShapeDtypeStruct(a.shape if scatter else (N_DEV,) + a.shape, a.dtype) for a in arrs]
    sems = [pltpu.SemaphoreType.DMA((n * N_PEER,)), pltpu.SemaphoreType.DMA((n * N_PEER,)),
            pltpu.SemaphoreType.DMA((n,))]
    return out_shape, sems


def _exchange(arrs, *, scatter, name):
    n = len(arrs)

    def body(*refs):
        ins, outs, sems = refs[:n], refs[n:2 * n], refs[2 * n:]
        _exchange_start(ins, outs, sems, scatter)
        _exchange_wait(ins, outs, sems, scatter)

    hbm = pl.BlockSpec(memory_space=pltpu.HBM)
    out_shape, sems = _exchange_shapes(arrs, scatter)
    return pl.pallas_call(body, name=name, in_specs=[hbm] * n, out_specs=[hbm] * n, out_shape=out_shape,
                          scratch_shapes=sems)(*arrs)


def _gather_once_per_chip(arrs, *, name):
    n = len(arrs)

    def body(*refs):
        ins, outs = refs[:n], refs[n:2 * n]
        send_sems, recv_sems, local_sems = refs[2 * n:]
        x, y, c = lax.axis_index("x"), lax.axis_index("y"), lax.axis_index("c")
        sibling = (x, y, 1 - c)
        chips = [(1 - x, y), (x, 1 - y), (1 - x, 1 - y)]

        def slot(px, py, pc):
            return 4 * px + 2 * py + pc

        def copy(a, k, block, to, src=None):
            dst = outs[a].at[slot(*block)]
            return pltpu.make_async_remote_copy(
                src_ref=dst if src is None else src, dst_ref=dst, send_sem=send_sems.at[a * N_PEER + k],
                recv_sem=recv_sems.at[a * N_PEER + k], device_id=to, device_id_type=pl.DeviceIdType.MESH)

        local, sends = [], []
        for a in range(n):
            local.append(pltpu.make_async_copy(ins[a], outs[a].at[slot(x, y, c)], local_sems.at[a]))
            sends.append(copy(a, 0, (x, y, c), sibling, src=ins[a]))
            sends += [copy(a, 1 + j, (x, y, c), (*chip, c), src=ins[a]) for j, chip in enumerate(chips)]
        for cp in local + sends:
            cp.start()
        for a in range(n):
            for j, chip in enumerate(chips):
                copy(a, 1 + j, (*chip, c), (x, y, c)).wait_recv()
                sends.append(copy(a, 4 + j, (*chip, c), sibling))
                sends[-1].start()
        for a in range(n):
            copy(a, 0, sibling, (x, y, c)).wait_recv()
            for j, chip in enumerate(chips):
                copy(a, 4 + j, (*chip, 1 - c), (x, y, c)).wait_recv()
        for cp in sends:
            cp.wait_send()
        for cp in local:
            cp.wait()

    hbm = pl.BlockSpec(memory_space=pltpu.HBM)
    out_shape, sems = _exchange_shapes(arrs, False)
    return pl.pallas_call(body, name=name, in_specs=[hbm] * n, out_specs=[hbm] * n, out_shape=out_shape,
                          scratch_shapes=sems)(*arrs)


def _call(body, *, name, grid, in_specs, out_specs, out_shape, scratch_shapes, sem, args, ride=None,
          vmem_limit=VMEM_LIMIT):
    if ride is None:
        res = pl.pallas_call(body, name=name, grid=grid, in_specs=in_specs, out_specs=out_specs, out_shape=out_shape,
                             scratch_shapes=scratch_shapes, compiler_params=_cparams(sem, vmem_limit))(*args)
        return res, None
    arrs, scatter = ride
    n, n_in, n_out, n_scr = len(arrs), len(in_specs), len(out_specs), len(scratch_shapes)
    x_shape, x_sems = _exchange_shapes(arrs, scatter)

    def hosted(*refs):
        c_in, x_in = refs[:n_in], refs[n_in:n_in + n]
        c_out, x_out = refs[n_in + n:n_in + n + n_out], refs[n_in + n + n_out:n_in + 2 * n + n_out]
        scr = refs[n_in + 2 * n + n_out:]
        c_scr, sems = scr[:n_scr], scr[n_scr:]
        first = functools.reduce(jnp.logical_and, [pl.program_id(a) == 0 for a in range(len(grid))])
        last = functools.reduce(jnp.logical_and, [pl.program_id(a) == grid[a] - 1 for a in range(len(grid))])

        @pl.when(first)
        def _():
            _exchange_start(x_in, x_out, sems, scatter)

        body(*c_in, *c_out, *c_scr)

        @pl.when(last)
        def _():
            _exchange_wait(x_in, x_out, sems, scatter)

    hbm = pl.BlockSpec(memory_space=pltpu.HBM)
    res = pl.pallas_call(
        hosted, name=name, grid=grid, in_specs=list(in_specs) + [hbm] * n, out_specs=list(out_specs) + [hbm] * n,
        out_shape=list(out_shape) + x_shape, scratch_shapes=list(scratch_shapes) + x_sems,
        compiler_params=_cparams(("arbitrary",) * len(grid), vmem_limit))(*args, *arrs)
    return res[:n_out], res[n_out:]


def _adamw_math(w, g, m, v):
    m_new = ADAM_B1 * m + (1.0 - ADAM_B1) * g
    v_new = ADAM_B2 * v + (1.0 - ADAM_B2) * (g * g)
    m_hat = m_new / (1.0 - ADAM_B1 ** ADAM_STEP)
    v_hat = v_new / (1.0 - ADAM_B2 ** ADAM_STEP)
    delta = -ADAM_LR * (m_hat / (jnp.sqrt(v_hat) + ADAM_EPS) + ADAM_WD * w)
    return delta, m_new, v_new


def _adamw(w, g, m, v, *, name):
    r, c = w.shape
    slots = g.ndim == 3
    tr = r
    for cand in (128, 64, 32, 16):
        if r % cand == 0 and r > cand:
            tr = cand
            break

    def body(w_ref, g_ref, m_ref, v_ref, go_ref, d_ref, mo_ref, vo_ref):
        if slots:
            gv = g_ref[0].astype(F32)
            for s in range(1, N_DEV):
                gv = gv + g_ref[s].astype(F32)
        else:
            gv = g_ref[...]
        delta, m_new, v_new = _adamw_math(w_ref[...], gv, m_ref[...], v_ref[...])
        go_ref[...] = gv
        d_ref[...] = delta
        mo_ref[...] = m_new
        vo_ref[...] = v_new

    spec = pl.BlockSpec((tr, c), lambda i: (i, 0))
    g_spec = pl.BlockSpec((N_DEV, tr, c), lambda i: (0, i, 0)) if slots else spec
    return pl.pallas_call(
        body, name=name, grid=(r // tr,), in_specs=[spec, g_spec, spec, spec], out_specs=[spec] * 4,
        out_shape=[jax.ShapeDtypeStruct((r, c), F32)] * 4, compiler_params=_cparams(("parallel",)),
    )(w, g, m, v)


def _unshard_cols(g):
    return jnp.transpose(g, (1, 0, 2)).reshape(g.shape[1], -1)

def _shard_cols(full):
    r = full.shape[0]
    return jnp.transpose(full.reshape(r, N_DEV, -1), (1, 0, 2))


def _w_in_to_mine(w_ref_layout, d):
    wd, pw, ref_off, _, _ = _in_layout(d)
    cols = []
    for n in MY_ORDER:
        piece = lax.slice_in_dim(w_ref_layout, ref_off[n], ref_off[n] + wd[n], axis=1)
        if pw[n] != wd[n]:
            piece = jnp.pad(piece, ((0, 0), (0, pw[n] - wd[n])))
        cols.append(piece)
    return jnp.concatenate(cols, axis=1)


def _w_in_from_mine(g_mine, d):
    wd, _, _, my_off, _ = _in_layout(d)
    return jnp.concatenate([lax.slice_in_dim(g_mine, my_off[n], my_off[n] + wd[n], axis=1) for n in IN_NAMES],
                           axis=1)


def _wuq_to_mine(wq):
    r = wq.shape[0]
    w3 = wq.reshape(r, MLA_HEADS, MLA_NOPE + MLA_ROPE)
    w3 = jnp.pad(w3, ((0, 0), (0, 0), (0, MLA_QK_PAD - MLA_NOPE - MLA_ROPE)))
    return w3.reshape(r, MLA_HEADS * MLA_QK_PAD)


def _wuq_from_mine(g):
    r = g.shape[0]
    return g.reshape(r, MLA_HEADS, MLA_QK_PAD)[:, :, :MLA_NOPE + MLA_ROPE].reshape(r, -1)


def _wukv_to_mine(wkv):
    r = wkv.shape[0]
    w3 = wkv.reshape(r, MLA_HEADS, MLA_NOPE + MLA_V)
    return jnp.concatenate([w3[:, :, :MLA_NOPE].reshape(r, -1), w3[:, :, MLA_NOPE:].reshape(r, -1)], axis=1)


def _wukv_from_mine(g):
    r = g.shape[0]
    kn = g[:, :MLA_HEADS * MLA_NOPE].reshape(r, MLA_HEADS, MLA_NOPE)
    vv = g[:, MLA_HEADS * MLA_NOPE:].reshape(r, MLA_HEADS, MLA_V)
    return jnp.concatenate([kn, vv], axis=2).reshape(r, -1)


COL_SHARDED = ("w_in", "gla_gk_w", "mla_wuq", "mla_wukv", "ffn_w_in")


def _gathered_to_mine(name, g, d):
    full = _unshard_cols(g) if name in COL_SHARDED else g.reshape(-1, g.shape[-1])
    if name == "w_in":
        return _w_in_to_mine(full, d)
    if name == "gla_gk_w":
        return jnp.pad(full, ((0, LANE - GLA_GATE_RANK), (0, 0)))
    if name == "mla_wuq":
        return _wuq_to_mine(full)
    if name == "mla_wukv":
        return _wukv_to_mine(full)
    return full


def _grad_to_slabs(name, g, d):
    if g.ndim == 3:
        return g.astype(BF16)
    if name == "w_in":
        g = _w_in_from_mine(g, d)
    elif name == "mla_wuq":
        g = _wuq_from_mine(g)
    elif name == "mla_wukv":
        g = _wukv_from_mine(g)
    s = _shard_cols(g) if name in COL_SHARDED else g.reshape(N_DEV, -1, g.shape[-1])
    return s.astype(BF16)


class _Rides:
    GATHER = {"in_proj": ("gla_wo", "mla_wuq", "mla_wukv", "mla_wo", "w_out"),
              "flash_fwd": ("ffn_w_in", "ffn_w_down")}
    SCATTER = {"flash_bwd": ("ffn_w_in", "ffn_w_down", "w_out", "gla_wo", "mla_wo", "gla_gk_w"),
               "in_proj_dw": ("mla_wuq", "mla_wukv"),
               "in_proj_dx": ("w_in",)}

    def __init__(self, send, d):
        self.send, self.d, self.recv = send, d, {}

    def ride(self, stage, grads):
        if stage in self.GATHER:
            return [self.send[n] for n in self.GATHER[stage]], False
        return [_grad_to_slabs(n, grads[n], self.d) for n in self.SCATTER[stage]], True

    def done(self, stage, rode, w):
        if stage in self.GATHER:
            for n, g in zip(self.GATHER[stage], rode):
                w[n] = _gathered_to_mine(n, g, self.d)
        else:
            self.recv.update(zip(self.SCATTER[stage], rode))


def kernel(x, c, positions, ada_w, ada_b, norm_mix_g, w_in, gla_gk_w, gla_gk_b, gla_onorm_g, gla_wo, mla_q_norm_g, mla_wuq, mla_kv_norm_g, mla_wukv, mla_wo, w_out, norm_ffn_g, ffn_w_in, ffn_w_down, final_norm_g, loss_target, m_ada_w, m_ada_b, m_norm_mix_g, m_w_in, m_gla_gk_w, m_gla_gk_b, m_gla_onorm_g, m_gla_wo, m_mla_q_norm_g, m_mla_wuq, m_mla_kv_norm_g, m_mla_wukv, m_mla_wo, m_w_out, m_norm_ffn_g, m_ffn_w_in, m_ffn_w_down, m_final_norm_g, v_ada_w, v_ada_b, v_norm_mix_g, v_w_in, v_gla_gk_w, v_gla_gk_b, v_gla_onorm_g, v_gla_wo, v_mla_q_norm_g, v_mla_wuq, v_mla_kv_norm_g, v_mla_wukv, v_mla_wo, v_w_out, v_norm_ffn_g, v_ffn_w_in, v_ffn_w_down, v_final_norm_g):
    wts = dict(ada_w=ada_w, ada_b=ada_b, norm_mix_g=norm_mix_g, w_in=w_in, gla_gk_w=gla_gk_w, gla_gk_b=gla_gk_b,
               gla_onorm_g=gla_onorm_g, gla_wo=gla_wo, mla_q_norm_g=mla_q_norm_g, mla_wuq=mla_wuq,
               mla_kv_norm_g=mla_kv_norm_g, mla_wukv=mla_wukv, mla_wo=mla_wo, w_out=w_out, norm_ffn_g=norm_ffn_g,
               ffn_w_in=ffn_w_in, ffn_w_down=ffn_w_down, final_norm_g=final_norm_g)
    mom_m = dict(zip(WEIGHTS, (m_ada_w, m_ada_b, m_norm_mix_g, m_w_in, m_gla_gk_w, m_gla_gk_b, m_gla_onorm_g,
                               m_gla_wo, m_mla_q_norm_g, m_mla_wuq, m_mla_kv_norm_g, m_mla_wukv, m_mla_wo, m_w_out,
                               m_norm_ffn_g, m_ffn_w_in, m_ffn_w_down, m_final_norm_g)))
    mom_v = dict(zip(WEIGHTS, (v_ada_w, v_ada_b, v_norm_mix_g, v_w_in, v_gla_gk_w, v_gla_gk_b, v_gla_onorm_g,
                               v_gla_wo, v_mla_q_norm_g, v_mla_wuq, v_mla_kv_norm_g, v_mla_wukv, v_mla_wo, v_w_out,
                               v_norm_ffn_g, v_ffn_w_in, v_ffn_w_down, v_final_norm_g)))
    seq, d = x.shape[1], x.shape[2]
    me = 4 * lax.axis_index("x") + 2 * lax.axis_index("y") + lax.axis_index("c")

    def two_d(a):
        return a.reshape(a.shape[-2], a.shape[-1]) if a.ndim >= 2 else a.reshape(1, -1)

    shard = {n: two_d(wts[n]) for n in BIG}
    send = {n: shard[n].astype(F32 if n == "gla_gk_w" else BF16) for n in BIG}
    got = _gather_once_per_chip([send["w_in"], send["gla_gk_w"], two_d(c)], name="comm_all_gather_first")
    c_all = got[2].reshape(N_DEV, d)
    w = dict(
        w_in=_gathered_to_mine("w_in", got[0], d), gla_gk_w=_gathered_to_mine("gla_gk_w", got[1], d),
        gla_gk_b=two_d(gla_gk_b), gla_onorm_g=two_d(gla_onorm_g), mla_q_norm_g=two_d(mla_q_norm_g),
        mla_kv_norm_g=two_d(mla_kv_norm_g), norm_mix_g=two_d(norm_mix_g), norm_ffn_g=two_d(norm_ffn_g),
        final_norm_g=two_d(final_norm_g))
    rides = _Rides(send, d)

    c_pad = jnp.pad(c_all, ((0, 16 - N_DEV), (0, 0)))
    (c_act,) = _rowwise(lambda cv: cv * _sigmoid(cv), [(c_pad, d, 0)], [], [(d, F32)], tile=16, name="silu_c")
    ada_w2 = two_d(ada_w)
    mod_part = _mm(c_act, ada_w2, name="mm_ada")[:N_DEV]
    (mod_all,) = _exchange([mod_part], scatter=False, name="comm_all_gather_mod")
    mod_mine = lax.dynamic_index_in_dim(mod_all, me, axis=1, keepdims=False).reshape(1, -1) + two_d(ada_b)
    mod8 = jnp.pad(mod_mine.reshape(6, d), ((0, 2), (0, 0)))

    inv_freq = ROPE_THETA ** (-jnp.arange(0, MLA_ROPE, 2, dtype=F32) / MLA_ROPE)
    ang = positions.reshape(seq, 1).astype(F32) * inv_freq[None, :]
    cos, sin, z32 = jnp.cos(ang), jnp.sin(ang), jnp.zeros((seq, 32), F32)
    tab = jnp.concatenate([cos, cos, z32, z32, -sin, z32, z32, z32, z32, sin, z32, z32], axis=1)
    loss_local, grad_x, _, small = _local_step(x.reshape(seq, d), loss_target.reshape(seq, d), tab, mod8, w, rides)

    recv = rides.recv
    pack = jnp.concatenate([small[n] for n in SMALL], axis=1)
    (pack_all,) = _exchange([pack], scatter=False, name="comm_all_gather_small")
    pack_all = pack_all.reshape(N_DEV, -1)

    res = {}
    for n in BIG:
        res[n] = _adamw(shard[n], recv[n], two_d(mom_m[n]), two_d(mom_v[n]), name="adamw_" + n)
    n_ada = ada_w2.shape[1]
    dmod_cols = lax.dynamic_slice_in_dim(pack_all[:, :6 * d], me * n_ada, n_ada, axis=1)

    def f_outer(cat, dm):
        acc = cat[:, 0:1] * dm[0:1]
        for b in range(1, N_DEV):
            acc = acc + cat[:, b:b + 1] * dm[b:b + 1]
        return acc

    (g_ada_w,) = _rowwise(f_outer, [(jnp.transpose(c_act[:N_DEV]), N_DEV, 0)], [dmod_cols], [(n_ada, F32)],
                          tile=256, name="ada_w_grad")
    res["ada_w"] = _adamw(ada_w2, g_ada_w, two_d(m_ada_w), two_d(v_ada_w), name="adamw_ada_w")
    w_small = jnp.concatenate([two_d(wts[n]) for n in SMALL], axis=1)
    m_small = jnp.concatenate([two_d(mom_m[n]) for n in SMALL], axis=1)
    v_small = jnp.concatenate([two_d(mom_v[n]) for n in SMALL], axis=1)
    small_res = _adamw(w_small, pack_all.reshape(N_DEV, 1, -1), m_small, v_small, name="adamw_small")
    off = 0
    for n in SMALL:
        width = wts[n].size
        res[n] = tuple(lax.slice_in_dim(a, off, off + width, axis=1) for a in small_res)
        off += width

    loss = lax.psum(loss_local, ("x", "y", "c"))
    outs = [loss, grad_x.reshape(x.shape)]
    for kind in range(4):
        outs += [res[n][kind].reshape(wts[n].shape) for n in WEIGHTS]
    return tuple(outs)
```

```python
import functools

import jax
import jax.numpy as jnp
from jax import lax
from jax.experimental import pallas as pl
from jax.experimental.pallas import tpu as pltpu

F32 = jnp.float32
BF16 = jnp.bfloat16

N_DEV = 8
GLA_HEADS = 4
GLA_DK = 256
GLA_DV = 512
GLA_GATE_RANK = 16
GLA_GATE_NORMALIZER = 16.0
GLA_CHUNK = 64
MLA_HEADS = 16
MLA_NOPE = 128
MLA_ROPE = 64
MLA_V = 128
MLA_QK_PAD = 256
ROPE_THETA = 10000.0
NORM_EPS = 1e-6
ATT_SCALE = (MLA_NOPE + MLA_ROPE) ** -0.5
GLA_QSCALE = GLA_DK ** -0.5

ADAM_LR = 0.001
ADAM_B1 = 0.9
ADAM_B2 = 0.999
ADAM_EPS = 1e-08
ADAM_WD = 0.01
ADAM_STEP = 10

LANE = 128
VMEM_LIMIT = 48 * 1024 * 1024
FLASH_BWD_VMEM = 58 * 1024 * 1024
MM_TILE_BYTES = 6 * 1024 * 1024
LOG2E = 1.4426950408889634
LN2 = 0.6931471805599453
NEG = -1e30

IN_NAMES = ("q", "k", "v", "g", "gk", "cq", "ckv", "kr", "ga", "gb")
MY_ORDER = ("v", "g", "ga", "gb", "q", "k", "cq", "ckv", "gk", "kr")

WEIGHTS = ("ada_w", "ada_b", "norm_mix_g", "w_in", "gla_gk_w", "gla_gk_b", "gla_onorm_g", "gla_wo",
           "mla_q_norm_g", "mla_wuq", "mla_kv_norm_g", "mla_wukv", "mla_wo", "w_out", "norm_ffn_g",
           "ffn_w_in", "ffn_w_down", "final_norm_g")
BIG = ("w_in", "gla_gk_w", "gla_wo", "mla_wuq", "mla_wukv", "mla_wo", "w_out", "ffn_w_in", "ffn_w_down")
SMALL = ("ada_b", "norm_mix_g", "gla_gk_b", "gla_onorm_g", "mla_q_norm_g", "mla_kv_norm_g", "norm_ffn_g",
         "final_norm_g")


def _in_layout(d):
    w = dict(q=d // 2, k=d // 2, v=d, g=d, gk=GLA_GATE_RANK, cq=d // 4, ckv=512, kr=MLA_ROPE, ga=d, gb=d)
    pw = {n: -(-w[n] // LANE) * LANE for n in w}
    ref_off, o = {}, 0
    for n in IN_NAMES:
        ref_off[n] = o
        o += w[n]
    my_off, o = {}, 0
    for n in MY_ORDER:
        assert o % pw[n] == 0
        my_off[n] = o
        o += pw[n]
    return w, pw, ref_off, my_off, o


def _cparams(sem=None, vmem_limit=VMEM_LIMIT):
    return pltpu.CompilerParams(dimension_semantics=sem, vmem_limit_bytes=vmem_limit)


def _dot(a, b, ca=1, cb=0):
    return lax.dot_general(a, b, (((ca,), (cb,)), ((), ())), preferred_element_type=F32)


def _tile(n, cap):
    if n <= cap:
        return n
    t = (cap // LANE) * LANE
    while t >= LANE:
        if n % t == 0:
            return t
        t -= LANE
    return n


def _mm(a, b, *, ta=False, tb=False, out_dtype=F32, name, ride=None, slabs=False):
    m, k = (a.shape[1], a.shape[0]) if ta else a.shape
    n = b.shape[0] if tb else b.shape[1]
    assert k == (b.shape[1] if tb else b.shape[0])
    wide = max(a.dtype.itemsize, b.dtype.itemsize) > 2
    tm, tn, tk = _tile(m, 1024), _tile(n, 1024), _tile(k, MM_TILE_BYTES // (1024 * (4 if wide else 2)))
    if slabs:
        tn = n // N_DEV
        assert tn % LANE == 0
    nk = k // tk

    def product(a_ref, b_ref):
        return _dot(a_ref[...].astype(BF16), b_ref[...].astype(BF16), 0 if ta else 1, 1 if tb else 0)

    def store(o_ref, val):
        if slabs:
            o_ref[0] = val.astype(o_ref.dtype)
        else:
            o_ref[...] = val.astype(o_ref.dtype)

    def body_one(a_ref, b_ref, o_ref):
        store(o_ref, product(a_ref, b_ref))

    def body_acc(a_ref, b_ref, o_ref, acc_ref):
        kk = pl.program_id(2)

        @pl.when(kk == 0)
        def _():
            acc_ref[...] = jnp.zeros_like(acc_ref)

        acc_ref[...] += product(a_ref, b_ref)

        @pl.when(kk == nk - 1)
        def _():
            store(o_ref, acc_ref[...])

    a_spec = (pl.BlockSpec((tk, tm), lambda i, j, kk: (kk, i)) if ta
              else pl.BlockSpec((tm, tk), lambda i, j, kk: (i, kk)))
    b_spec = (pl.BlockSpec((tn, tk), lambda i, j, kk: (j, kk)) if tb
              else pl.BlockSpec((tk, tn), lambda i, j, kk: (kk, j)))
    (out,), rode = _call(
        body_one if nk == 1 else body_acc, name=name, grid=(m // tm, n // tn, nk), in_specs=[a_spec, b_spec],
        out_specs=[pl.BlockSpec((1, tm, tn), lambda i, j, kk: (j, i, 0)) if slabs
                   else pl.BlockSpec((tm, tn), lambda i, j, kk: (i, j))],
        out_shape=[jax.ShapeDtypeStruct((N_DEV, m, tn) if slabs else (m, n), out_dtype)],
        scratch_shapes=[] if nk == 1 else [pltpu.VMEM((tm, tn), F32)],
        sem=("parallel", "parallel", "arbitrary"), args=(a, b), ride=ride)
    return out if ride is None else (out, rode)


def _rowwise(fn, rows, vecs, outs, sums=(), *, tile, name):
    t = rows[0][0].shape[0]
    tile = min(tile, t)
    assert t % tile == 0
    n_rows, n_vecs, n_outs = len(rows), len(vecs), len(outs)

    def body(*refs):
        ins = [r[...].astype(F32) for r in refs[:n_rows + n_vecs]]
        res = fn(*ins)
        if not isinstance(res, (tuple, list)):
            res = (res,)
        out_refs = refs[n_rows + n_vecs:]
        for r, val in zip(out_refs[:n_outs], res[:n_outs]):
            r[...] = val.astype(r.dtype)
        if sums:
            first = pl.program_id(0) == 0
            for r, val in zip(out_refs[n_outs:], res[n_outs:]):
                @pl.when(first)
                def _(r=r):
                    r[...] = jnp.zeros_like(r)
                r[...] += val

    in_specs = [pl.BlockSpec((tile, w), lambda i, cb=cb: (i, cb)) for (_, w, cb) in rows]
    in_specs += [pl.BlockSpec(v.shape, lambda i: (0, 0)) for v in vecs]
    out_specs = [pl.BlockSpec((tile, w), lambda i: (i, 0)) for (w, _) in outs]
    out_specs += [pl.BlockSpec((1, w), lambda i: (0, 0)) for w in sums]
    out_shape = [jax.ShapeDtypeStruct((t, w), dt) for (w, dt) in outs]
    out_shape += [jax.ShapeDtypeStruct((1, w), F32) for w in sums]
    res = pl.pallas_call(
        body, name=name, grid=(t // tile,), in_specs=in_specs, out_specs=out_specs, out_shape=out_shape,
        compiler_params=_cparams(("arbitrary",)),
    )(*[r[0] for r in rows], *vecs)
    return res


def _rstd(x):
    return lax.rsqrt(jnp.mean(x * x, axis=-1, keepdims=True) + NORM_EPS)


def _sigmoid(x):
    return 1.0 / (1.0 + jnp.exp(-x))


def _rms_bwd(dxh, xh, r):
    return r * (dxh - xh * jnp.mean(dxh * xh, axis=-1, keepdims=True))


def _rope(t, tab, sign):
    cosf, sin_a, sin_b = tab[:, :LANE], tab[:, LANE:2 * LANE], tab[:, 2 * LANE:]
    return t * cosf + sign * (pltpu.roll(t, 96, 1) * sin_a + pltpu.roll(t, 32, 1) * sin_b)


def _split3(x):
    hi = x.astype(BF16)
    r1 = x - hi.astype(F32)
    mid = r1.astype(BF16)
    lo = (r1 - mid.astype(F32)).astype(BF16)
    return hi, mid, lo


def _tri_sum(tri_bf16, x):
    hi, mid, lo = _split3(x)
    return _dot(tri_bf16, hi) + _dot(tri_bf16, mid) + _dot(tri_bf16, lo)


def _dot_nt2(a, b):
    a_hi = a.astype(BF16)
    a_lo = (a - a_hi.astype(F32)).astype(BF16)
    b_hi = b.astype(BF16)
    b_lo = (b - b_hi.astype(F32)).astype(BF16)
    return _dot(a_hi, b_hi, 1, 1) + _dot(a_hi, b_lo, 1, 1) + _dot(a_lo, b_hi, 1, 1)


def _gla_specs(t, rows, lay, reverse):
    nb = t // rows
    blk = (lambda i: nb - 1 - i) if reverse else (lambda i: i)
    qb, kb = lay["q"] // GLA_DK, lay["k"] // GLA_DK
    vb = lay["v"] // GLA_DV
    return [
        pl.BlockSpec((rows, GLA_DK), lambda h, i: (blk(i), qb + h)),
        pl.BlockSpec((rows, GLA_DK), lambda h, i: (blk(i), kb + h)),
        pl.BlockSpec((rows, GLA_DV), lambda h, i: (blk(i), vb + h)),
        pl.BlockSpec((rows, GLA_DK), lambda h, i: (blk(i), h)),
    ], blk


def _gla_fwd(p, la, lay):
    t = p.shape[0]
    rows = min(512, t)
    nb, nc = t // rows, rows // GLA_CHUNK
    c64 = GLA_CHUNK

    def body(q_ref, k_ref, v_ref, la_ref, o_ref, st_ref, s_ref):
        @pl.when(pl.program_id(1) == 0)
        def _():
            s_ref[...] = jnp.zeros_like(s_ref)

        r = lax.broadcasted_iota(jnp.int32, (c64, c64), 0)
        cc = lax.broadcasted_iota(jnp.int32, (c64, c64), 1)
        tril = cc <= r
        tril_b = tril.astype(BF16)
        for c in range(nc):
            sl = pl.ds(c * c64, c64)
            b = _tri_sum(tril_b, la_ref[sl, :])
            b_last = b[c64 - 1:c64, :]
            q = q_ref[sl, :].astype(F32) * GLA_QSCALE
            k = k_ref[sl, :].astype(F32)
            v = v_ref[sl, :].astype(BF16)
            qt_f = q * jnp.exp(b)
            qt = qt_f.astype(BF16)
            kh = (k * jnp.exp(b_last - b)).astype(BF16)
            s_prev = s_ref[...]
            st_ref[0, c] = s_prev
            att = jnp.where(tril, _dot_nt2(qt_f, k * jnp.exp(-b)), 0.0)
            o_ref[sl, :] = _dot(qt, s_prev.astype(BF16), 1, 1) + _dot(att.astype(BF16), v)
            s_ref[...] = s_prev * jnp.exp(b_last) + _dot(v, kh, 0, 0)

    in_specs, _ = _gla_specs(t, rows, lay, False)
    return pl.pallas_call(
        body, name="gla_fwd", grid=(GLA_HEADS, nb), in_specs=in_specs,
        out_specs=[pl.BlockSpec((rows, GLA_DV), lambda h, i: (i, h)),
                   pl.BlockSpec((1, nc, GLA_DV, GLA_DK), lambda h, i: (h, i, 0, 0))],
        out_shape=[jax.ShapeDtypeStruct((t, GLA_HEADS * GLA_DV), F32),
                   jax.ShapeDtypeStruct((GLA_HEADS, t // c64, GLA_DV, GLA_DK), F32)],
        scratch_shapes=[pltpu.VMEM((GLA_DV, GLA_DK), F32)],
        compiler_params=_cparams(("parallel", "arbitrary")),
    )(p, p, p, la)


def _gla_bwd(p, la, do, states, lay):
    t = p.shape[0]
    rows = min(512, t)
    nb, nc = t // rows, rows // GLA_CHUNK
    c64 = GLA_CHUNK

    def body(q_ref, k_ref, v_ref, la_ref, do_ref, st_ref, dq_ref, dk_ref, dv_ref, dla_ref, ds_ref):
        @pl.when(pl.program_id(1) == 0)
        def _():
            ds_ref[...] = jnp.zeros_like(ds_ref)

        r = lax.broadcasted_iota(jnp.int32, (c64, c64), 0)
        cc = lax.broadcasted_iota(jnp.int32, (c64, c64), 1)
        tril = cc <= r
        tril_b = tril.astype(BF16)
        triu_b = (cc >= r).astype(BF16)
        for c in reversed(range(nc)):
            sl = pl.ds(c * c64, c64)
            b = _tri_sum(tril_b, la_ref[sl, :])
            b_last = b[c64 - 1:c64, :]
            eb, enb, ebl_b, ebl = jnp.exp(b), jnp.exp(-b), jnp.exp(b_last - b), jnp.exp(b_last)
            k = k_ref[sl, :].astype(F32)
            qt_f = q_ref[sl, :].astype(F32) * GLA_QSCALE * eb
            kt_f = k * enb
            kh_f = k * ebl_b
            qt, kt, kh = qt_f.astype(BF16), kt_f.astype(BF16), kh_f.astype(BF16)
            v_f = v_ref[sl, :].astype(F32)
            dout_f = do_ref[sl, :]
            v, dout = v_f.astype(BF16), dout_f.astype(BF16)
            s_prev = st_ref[0, c]
            ds_next = ds_ref[...]
            ds_next_b = ds_next.astype(BF16)
            att = jnp.where(tril, _dot_nt2(qt_f, kt_f), 0.0).astype(BF16)
            datt = jnp.where(tril, _dot_nt2(dout_f, v_f), 0.0).astype(BF16)
            dqt = _dot(dout, s_prev.astype(BF16)) + _dot(datt, kt)
            dkt = _dot(datt, qt, 0, 0)
            dv = _dot(att, dout, 0, 0) + _dot(kh, ds_next_b, 1, 1)
            dkh = _dot(v, ds_next_b)
            d_ebl = jnp.sum(ds_next * s_prev, axis=0, keepdims=True)
            ds_ref[...] = ds_next * ebl + _dot(dout, qt, 0, 0)
            db = dqt * qt_f - dkt * kt_f - dkh * kh_f
            db_last = ebl * d_ebl + jnp.sum(dkh * kh_f, axis=0, keepdims=True)
            dq_ref[sl, :] = (dqt * eb * GLA_QSCALE).astype(dq_ref.dtype)
            dk_ref[sl, :] = (dkt * enb + dkh * ebl_b).astype(dk_ref.dtype)
            dv_ref[sl, :] = dv.astype(dv_ref.dtype)
            dla_ref[sl, :] = _tri_sum(triu_b, db) + db_last

    in_specs, blk = _gla_specs(t, rows, lay, True)
    in_specs += [pl.BlockSpec((rows, GLA_DV), lambda h, i: (blk(i), h)),
                 pl.BlockSpec((1, nc, GLA_DV, GLA_DK), lambda h, i: (h, blk(i), 0, 0))]
    dk_spec = pl.BlockSpec((rows, GLA_DK), lambda h, i: (blk(i), h))
    return pl.pallas_call(
        body, name="gla_bwd", grid=(GLA_HEADS, nb), in_specs=in_specs,
        out_specs=[dk_spec, dk_spec, pl.BlockSpec((rows, GLA_DV), lambda h, i: (blk(i), h)), dk_spec],
        out_shape=[jax.ShapeDtypeStruct((t, GLA_HEADS * GLA_DK), BF16),
                   jax.ShapeDtypeStruct((t, GLA_HEADS * GLA_DK), BF16),
                   jax.ShapeDtypeStruct((t, GLA_HEADS * GLA_DV), BF16),
                   jax.ShapeDtypeStruct((t, GLA_HEADS * GLA_DK), F32)],
        scratch_shapes=[pltpu.VMEM((GLA_DV, GLA_DK), F32)],
        compiler_params=_cparams(("parallel", "arbitrary")),
    )(p, p, p, la, do, states)


def _diag_mask(rows, cols, row0):
    row = row0 + lax.broadcasted_iota(jnp.int32, (rows, cols), 0)
    col = lax.broadcasted_iota(jnp.int32, (rows, cols), 1)
    return col <= row


QK_SPARE = MLA_NOPE + MLA_ROPE
N_SPARE = 3


def _with_spare(x, col, lane0):
    lane = lax.broadcasted_iota(jnp.int32, x.shape, 1)
    for n, term in enumerate(_split3(col)):
        x = jnp.where(lane == lane0 + n, term, x)
    return x


def _spare_ones(shape, lane0):
    lane = lax.broadcasted_iota(jnp.int32, shape, 1)
    return ((lane >= lane0) & (lane < lane0 + N_SPARE)).astype(F32)


def _flash_tiles(t):
    tq = min(1024, t)
    halves = 2 if tq % 32 == 0 else 1
    return tq, t // tq, halves, tq // halves


def _flash_fwd(q, k, vx, ride=None):
    t = q.shape[0]
    tq, nq, halves, hr = _flash_tiles(t)
    dqk, dv = MLA_QK_PAD, MLA_V

    def body(q_ref, k_ref, v_ref, o_ref, qa_ref, m_ref, acc_ref, s_ref):
        i = pl.program_id(1)
        m_ref[...] = jnp.full_like(m_ref, NEG)
        acc_ref[...] = jnp.zeros_like(acc_ref)

        def scores(j, slot):
            kb = k_ref[pl.ds(pl.multiple_of(j * tq, tq), tq), :]
            for hh in range(halves):
                s_ref[slot, pl.ds(hh * hr, hr), :] = _dot(q_ref[pl.ds(hh * hr, hr), :], kb, 1, 1)

        def consume(j, slot, masked):
            vb = v_ref[pl.ds(pl.multiple_of(j * tq, tq), tq), :]
            for hh in range(halves):
                rs = pl.ds(hh * hr, hr)
                s = s_ref[slot, rs, :]
                if masked:
                    s = jnp.where(_diag_mask(hr, tq, hh * hr), s, NEG)
                m_old = m_ref[rs, :]
                m_new = jnp.maximum(m_old, jnp.max(s, axis=1, keepdims=True))
                pr = jnp.exp2(s - m_new)
                acc_ref[rs, :] = jnp.exp2(m_old - m_new) * acc_ref[rs, :] + _dot(pr.astype(BF16), vb)
                m_ref[rs, :] = m_new

        def two_blocks(jj, carry):
            scores(2 * jj + 1, 1)
            consume(2 * jj, 0, False)
            scores(2 * jj + 2, 0)
            consume(2 * jj + 1, 1, False)
            return carry

        scores(0, 0)
        lax.fori_loop(0, i // 2, two_blocks, 0)

        @pl.when(i % 2 == 0)
        def _():
            consume(i, 0, True)

        @pl.when(i % 2 == 1)
        def _():
            scores(i, 1)
            consume(i - 1, 0, False)
            consume(i, 1, True)

        acc = acc_ref[...]
        l = acc[:, dv:dv + 1]
        o_ref[...] = acc[:, :dv] / l
        qa_ref[...] = _with_spare(q_ref[...], -(m_ref[...] + jnp.log(l) * LOG2E), QK_SPARE)

    outs, rode = _call(
        body, name="mla_flash_fwd", grid=(MLA_HEADS, nq),
        in_specs=[pl.BlockSpec((tq, dqk), lambda h, i: (i, h)),
                  pl.BlockSpec((t, dqk), lambda h, i: (0, h)),
                  pl.BlockSpec((t, 2 * dv), lambda h, i: (0, h))],
        out_specs=[pl.BlockSpec((tq, dv), lambda h, i: (i, h)),
                   pl.BlockSpec((tq, dqk), lambda h, i: (i, h))],
        out_shape=[jax.ShapeDtypeStruct((t, MLA_HEADS * dv), F32),
                   jax.ShapeDtypeStruct((t, MLA_HEADS * dqk), BF16)],
        scratch_shapes=[pltpu.VMEM((tq, 1), F32), pltpu.VMEM((tq, 2 * dv), F32), pltpu.VMEM((2, tq, tq), F32)],
        sem=("parallel", "arbitrary"), args=(q, k, vx), ride=ride)
    return tuple(outs) if ride is None else (tuple(outs), rode)


def _flash_bwd(qa, k, vx, doa, ride=None):
    t = qa.shape[0]
    tq, nq, halves, hr = _flash_tiles(t)
    dqk, dv = MLA_QK_PAD, MLA_V

    def body(k_ref, v_ref, q_ref, do_ref, dq_ref, dk_ref, dv_ref, dq_acc, dk_acc, dv_acc):
        j = pl.program_id(1)

        @pl.when(j == 0)
        def _():
            dq_acc[...] = jnp.zeros_like(dq_acc)

        kb = k_ref[...]
        vb = v_ref[...]
        dk_acc[...] = jnp.zeros_like(dk_acc)
        dv_acc[...] = jnp.zeros_like(dv_acc)

        def step(i, masked):
            for hh in range(halves):
                rs = pl.ds(pl.multiple_of(i * tq + hh * hr, hr), hr)
                qb = q_ref[rs, :]
                dout = do_ref[rs, :]
                nc = (hh + 1) * hr if masked else tq
                s = _dot(qb, kb[:nc], 1, 1)
                if masked:
                    s = jnp.where(_diag_mask(hr, nc, hh * hr), s, NEG)
                pr = jnp.exp2(s)
                ds = (pr * _dot(dout, vb[:nc], 1, 1)).astype(BF16)
                dv_acc[pl.ds(0, nc), :] += _dot(pr.astype(BF16), dout, 0, 0)
                dk_acc[pl.ds(0, nc), :] += _dot(ds, qb, 0, 0)
                dq_acc[rs, :] += _dot(ds, kb[:nc])

        def loop_body(i, carry):
            step(i, False)
            return carry

        step(j, True)
        lax.fori_loop(j + 1, nq, loop_body, 0)
        dk_ref[...] = (dk_acc[...] * LN2).astype(dk_ref.dtype)
        dv_ref[...] = dv_acc[:, :dv].astype(dv_ref.dtype)

        @pl.when(j == nq - 1)
        def _():
            dq_ref[...] = (dq_acc[...] * ATT_SCALE).astype(dq_ref.dtype)

    outs, rode = _call(
        body, name="mla_flash_bwd", grid=(MLA_HEADS, nq),
        in_specs=[pl.BlockSpec((tq, dqk), lambda h, j: (j, h)),
                  pl.BlockSpec((tq, 2 * dv), lambda h, j: (j, h)),
                  pl.BlockSpec((t, dqk), lambda h, j: (0, h)),
                  pl.BlockSpec((t, 2 * dv), lambda h, j: (0, h))],
        out_specs=[pl.BlockSpec((t, dqk), lambda h, j: (0, h)),
                   pl.BlockSpec((tq, dqk), lambda h, j: (j, h)),
                   pl.BlockSpec((tq, dv), lambda h, j: (j, h))],
        out_shape=[jax.ShapeDtypeStruct((t, MLA_HEADS * dqk), BF16),
                   jax.ShapeDtypeStruct((t, MLA_HEADS * dqk), BF16),
                   jax.ShapeDtypeStruct((t, MLA_HEADS * dv), BF16)],
        scratch_shapes=[pltpu.VMEM((t, dqk), F32), pltpu.VMEM((tq, dqk), F32), pltpu.VMEM((tq, 2 * dv), F32)],
        sem=("parallel", "arbitrary"), args=(k, vx, qa, doa), ride=ride, vmem_limit=FLASH_BWD_VMEM)
    return tuple(outs) if ride is None else (tuple(outs), rode)


class _NoRides:
    def ride(self, stage, grads):
        return None

    def done(self, stage, rode, w):
        pass


def _local_step(x, target, tab, mod8, w, rides=None):
    t, d = x.shape
    rides = rides or _NoRides()
    big = {}

    def riding(stage, fn):
        r = rides.ride(stage, big)
        res = fn(r)
        if r is None:
            return res
        rides.done(stage, res[1], w)
        return res[0]
    _, pw, _, lay, _ = _in_layout(d)
    ffn = ((8 * d // 3 + 255) // 256) * 256

    def blk(arr, name):
        return (arr, pw[name], lay[name] // pw[name])

    def full(arr):
        return (arr, arr.shape[1], 0)

    g1, g2, g3 = w["norm_mix_g"], w["norm_ffn_g"], w["final_norm_g"]

    def f_ln1(xv, mod, g):
        return (xv * _rstd(xv) * g) * (1.0 + mod[1:2]) + mod[0:1]

    (h,) = _rowwise(f_ln1, [full(x)], [mod8, g1], [(d, BF16)], tile=256, name="ln1_modulate")
    p = riding("in_proj", lambda r: _mm(h, w["w_in"], name="mm_in_proj", ride=r))

    def f_gk(pgk, gkw, gkb):
        z = _dot(pgk.astype(BF16), gkw.astype(BF16)) + gkb
        return (jnp.minimum(z, 0.0) - jnp.log(1.0 + jnp.exp(-jnp.abs(z)))) / GLA_GATE_NORMALIZER

    (la,) = _rowwise(f_gk, [blk(p, "gk")], [w["gla_gk_w"], w["gla_gk_b"]], [(GLA_HEADS * GLA_DK, F32)],
                     tile=512, name="gla_gate")
    o_gla, states = _gla_fwd(p, la, lay)

    def f_gla_out(ov, pg, g):
        parts = []
        for hh in range(GLA_HEADS):
            oh = ov[:, hh * GLA_DV:(hh + 1) * GLA_DV]
            ph = pg[:, hh * GLA_DV:(hh + 1) * GLA_DV]
            parts.append(oh * _rstd(oh) * g * (ph * _sigmoid(ph)))
        return jnp.concatenate(parts, axis=1)

    (o_n,) = _rowwise(f_gla_out, [full(o_gla), blk(p, "g")], [w["gla_onorm_g"]], [(d, BF16)], tile=256,
                      name="gla_out_norm")
    y_gla = _mm(o_n, w["gla_wo"], out_dtype=BF16, name="mm_gla_wo")

    def f_mla_prep(cq, ckv, kr, tb, gq, gkv):
        return cq * _rstd(cq) * gq, ckv * _rstd(ckv) * gkv, _rope(kr, tb, 1.0)

    cqn, ckvn, krr = _rowwise(f_mla_prep, [blk(p, "cq"), blk(p, "ckv"), blk(p, "kr"), full(tab)],
                              [w["mla_q_norm_g"], w["mla_kv_norm_g"]],
                              [(pw["cq"], BF16), (pw["ckv"], BF16), (LANE, F32)], tile=512, name="mla_prep")
    qlat = _mm(cqn, w["mla_wuq"], out_dtype=BF16, name="mm_mla_wuq")
    kvl = _mm(ckvn, w["mla_wukv"], out_dtype=BF16, name="mm_mla_wukv")
    hv = MLA_HEADS * MLA_V

    def f_qkv(ql, kn, vv, kr, tb):
        qs, ks, vx = [], [], []
        kr1 = kr + _spare_ones(kr.shape, MLA_ROPE)
        ones = _spare_ones(kr.shape, 0)
        for hh in range(MLA_HEADS):
            o0 = hh * MLA_QK_PAD
            qs += [ql[:, o0:o0 + LANE], _rope(ql[:, o0 + LANE:o0 + 2 * LANE], tb, 1.0)]
            ks += [kn[:, hh * LANE:(hh + 1) * LANE], kr1]
            vx += [vv[:, hh * MLA_V:(hh + 1) * MLA_V], ones]
        return (jnp.concatenate(qs, axis=1) * (ATT_SCALE * LOG2E), jnp.concatenate(ks, axis=1),
                jnp.concatenate(vx, axis=1))

    qa, ka, vxa = _rowwise(f_qkv, [full(qlat), (kvl, hv, 0), (kvl, hv, 1), full(krr), full(tab)], [],
                           [(MLA_HEADS * MLA_QK_PAD, BF16), (MLA_HEADS * MLA_QK_PAD, BF16), (2 * hv, BF16)],
                           tile=256, name="mla_qkv_build")
    o_mla, qa_lse = riding("flash_fwd", lambda r: _flash_fwd(qa, ka, vxa, ride=r))
    y_mla = _mm(o_mla, w["mla_wo"], out_dtype=BF16, name="mm_mla_wo")

    def f_merge(yg, ym, ga, gb):
        return _sigmoid(ga) * yg + _sigmoid(gb) * ym

    (merged,) = _rowwise(f_merge, [full(y_gla), full(y_mla), blk(p, "ga"), blk(p, "gb")], [], [(d, BF16)],
                         tile=256, name="merge")
    mix = _mm(merged, w["w_out"], name="mm_w_out")

    def f_res_ln2(xv, mx, mod, g):
        x2v = xv + mod[2:3] * mx
        return x2v, (x2v * _rstd(x2v) * g) * (1.0 + mod[4:5]) + mod[3:4]

    x2, h2 = _rowwise(f_res_ln2, [full(x), full(mix)], [mod8, g2], [(d, F32), (d, BF16)], tile=256,
                      name="res_ln2_modulate")
    gu = _mm(h2, w["ffn_w_in"], out_dtype=BF16, name="mm_ffn_in")

    def f_swiglu(gv, uv):
        return gv * _sigmoid(gv) * uv

    (act,) = _rowwise(f_swiglu, [(gu, ffn, 0), (gu, ffn, 1)], [], [(ffn, BF16)], tile=128, name="swiglu")
    f_out = _mm(act, w["ffn_w_down"], name="mm_ffn_down")

    def f_head(x2v, fv, tg, mod, g):
        x3 = x2v + mod[5:6] * fv
        r = _rstd(x3)
        xh = x3 * r
        e = xh * g - tg
        loss_rows = 0.5 * jnp.mean(e * e, axis=-1, keepdims=True)
        dy = e * (1.0 / d)
        dx3 = _rms_bwd(dy * g, xh, r)
        loss = jnp.broadcast_to(jnp.sum(loss_rows, axis=0, keepdims=True), (1, LANE))
        return (dx3, dx3 * mod[5:6], loss, jnp.sum(dy * xh, axis=0, keepdims=True),
                jnp.sum(dx3 * fv, axis=0, keepdims=True))

    dx3, df, loss_v, dg3, dgate_f = _rowwise(f_head, [full(x2), full(f_out), full(target)], [mod8, g3],
                                             [(d, F32), (d, BF16)], [LANE, d, d], tile=256, name="loss_head")
    da = _mm(df, w["ffn_w_down"], tb=True, out_dtype=BF16, name="mm_ffn_down_dx")
    big["ffn_w_down"] = _mm(act, df, ta=True, out_dtype=BF16, name="mm_ffn_down_dw")

    def f_swiglu_bwd(gv, uv, dav):
        sg = _sigmoid(gv)
        return jnp.concatenate([dav * uv * (sg * (1.0 + gv * (1.0 - sg))), dav * (gv * sg)], axis=1)

    (dgu,) = _rowwise(f_swiglu_bwd, [(gu, ffn, 0), (gu, ffn, 1), full(da)], [], [(2 * ffn, BF16)], tile=128,
                      name="swiglu_bwd")
    dh2 = _mm(dgu, w["ffn_w_in"], tb=True, name="mm_ffn_in_dx")
    big["ffn_w_in"] = _mm(h2, dgu, ta=True, out_dtype=BF16, slabs=True, name="mm_ffn_in_dw")

    def f_ln2_bwd(x2v, dh, dx3v, mx, mod, g):
        r = _rstd(x2v)
        xh = x2v * r
        dn = dh * (1.0 + mod[4:5])
        dx2 = dx3v + _rms_bwd(dn * g, xh, r)
        return (dx2, dx2 * mod[2:3],
                jnp.sum(dh * (xh * g), axis=0, keepdims=True), jnp.sum(dh, axis=0, keepdims=True),
                jnp.sum(dn * xh, axis=0, keepdims=True), jnp.sum(dx2 * mx, axis=0, keepdims=True))

    dx2, dmix, dscale_f, dshift_f, dg2, dgate_m = _rowwise(
        f_ln2_bwd, [full(x2), full(dh2), full(dx3), full(mix)], [mod8, g2], [(d, F32), (d, BF16)],
        [d, d, d, d], tile=256, name="ln2_bwd")
    dmerged = _mm(dmix, w["w_out"], tb=True, out_dtype=BF16, name="mm_w_out_dx")
    big["w_out"] = _mm(merged, dmix, ta=True, out_dtype=BF16, name="mm_w_out_dw")

    def f_merge_bwd(dm, yg, ym, ga, gb):
        sa, sb = _sigmoid(ga), _sigmoid(gb)
        return dm * sa, dm * sb, dm * yg * sa * (1.0 - sa), dm * ym * sb * (1.0 - sb)

    dy_gla, dy_mla, dp_ga, dp_gb = _rowwise(
        f_merge_bwd, [full(dmerged), full(y_gla), full(y_mla), blk(p, "ga"), blk(p, "gb")], [],
        [(d, BF16)] * 4, tile=256, name="merge_bwd")
    do_n = _mm(dy_gla, w["gla_wo"], tb=True, name="mm_gla_wo_dx")
    big["gla_wo"] = _mm(o_n, dy_gla, ta=True, out_dtype=BF16, name="mm_gla_wo_dw")
    do_m = _mm(dy_mla, w["mla_wo"], tb=True, out_dtype=BF16, name="mm_mla_wo_dx")
    big["mla_wo"] = _mm(o_mla, dy_mla, ta=True, out_dtype=BF16, name="mm_mla_wo_dw")

    def f_gla_out_bwd(don, ov, pg, g):
        dos, dpgs = [], []
        dg = jnp.zeros((1, GLA_DV), F32)
        for hh in range(GLA_HEADS):
            sl = slice(hh * GLA_DV, (hh + 1) * GLA_DV)
            oh, ph, dn = ov[:, sl], pg[:, sl], don[:, sl]
            r = _rstd(oh)
            xh = oh * r
            sg = _sigmoid(ph)
            dpre = dn * (ph * sg)
            dg = dg + jnp.sum(dpre * xh, axis=0, keepdims=True)
            dos.append(_rms_bwd(dpre * g, xh, r))
            dpgs.append(dn * (xh * g) * (sg * (1.0 + ph * (1.0 - sg))))
        return jnp.concatenate(dos, axis=1), jnp.concatenate(dpgs, axis=1), dg

    do_gla, dp_g, dg_on = _rowwise(f_gla_out_bwd, [full(do_n), full(o_gla), blk(p, "g")], [w["gla_onorm_g"]],
                                   [(d, F32), (d, BF16)], [GLA_DV], tile=256, name="gla_out_norm_bwd")
    dp_q, dp_k, dp_v, dla = _gla_bwd(p, la, do_gla, states, lay)

    def f_gk_bwd(dlav, pgk, gkw, gkb):
        z = _dot(pgk.astype(BF16), gkw.astype(BF16)) + gkb
        dz = dlav * (1.0 / GLA_GATE_NORMALIZER) * _sigmoid(-z)
        return dz, _dot(dz.astype(BF16), gkw.astype(BF16), 1, 1), jnp.sum(dz, axis=0, keepdims=True)

    dz, dp_gk, dgk_b = _rowwise(f_gk_bwd, [full(dla), blk(p, "gk")], [w["gla_gk_w"], w["gla_gk_b"]],
                                [(GLA_HEADS * GLA_DK, BF16), (LANE, BF16)], [GLA_HEADS * GLA_DK], tile=512,
                                name="gla_gate_bwd")
    p_gk = lax.slice_in_dim(p, lay["gk"], lay["gk"] + LANE, axis=1)
    big["gla_gk_w"] = _mm(p_gk, dz, ta=True, name="mm_gla_gk_dw")[:GLA_GATE_RANK]

    def f_do_aug(dom, om):
        parts = []
        for hh in range(MLA_HEADS):
            dh_ = dom[:, hh * MLA_V:(hh + 1) * MLA_V]
            delta = jnp.sum(dh_ * om[:, hh * MLA_V:(hh + 1) * MLA_V], axis=1, keepdims=True)
            parts += [dh_.astype(BF16), _with_spare(jnp.zeros(dh_.shape, BF16), -delta, 0)]
        return jnp.concatenate(parts, axis=1)

    (doa,) = _rowwise(f_do_aug, [full(do_m), full(o_mla)], [], [(2 * hv, BF16)], tile=256, name="mla_do_delta")
    dqa, dka, dva = riding("flash_bwd", lambda r: _flash_bwd(qa_lse, ka, vxa, doa, ride=r))

    def f_qkv_bwd(dq, dk, dvv, tb):
        dqs, dkn = [], []
        dkr = jnp.zeros((dq.shape[0], LANE), F32)
        for hh in range(MLA_HEADS):
            o0 = hh * MLA_QK_PAD
            dqs += [dq[:, o0:o0 + LANE], _rope(dq[:, o0 + LANE:o0 + 2 * LANE], tb, -1.0)]
            dkn.append(dk[:, o0:o0 + LANE])
            dkr = dkr + dk[:, o0 + LANE:o0 + 2 * LANE]
        return jnp.concatenate(dqs, axis=1), jnp.concatenate(dkn + [dvv], axis=1), dkr

    dqlat, dkvl, dkrr = _rowwise(f_qkv_bwd, [full(dqa), full(dka), full(dva), full(tab)], [],
                                 [(MLA_HEADS * MLA_QK_PAD, BF16), (2 * hv, BF16), (LANE, F32)], tile=256,
                                 name="mla_qkv_build_bwd")
    dcqn = _mm(dqlat, w["mla_wuq"], tb=True, name="mm_mla_wuq_dx")
    big["mla_wuq"] = _mm(cqn, dqlat, ta=True, out_dtype=BF16, name="mm_mla_wuq_dw")
    dckvn = _mm(dkvl, w["mla_wukv"], tb=True, name="mm_mla_wukv_dx")
    big["mla_wukv"] = _mm(ckvn, dkvl, ta=True, out_dtype=BF16, name="mm_mla_wukv_dw")

    def f_mla_prep_bwd(dq, dkv, dkr, cq, ckv, tb, gq, gkv):
        rq, rk = _rstd(cq), _rstd(ckv)
        xq, xk = cq * rq, ckv * rk
        return (_rms_bwd(dq * gq, xq, rq), _rms_bwd(dkv * gkv, xk, rk), _rope(dkr, tb, -1.0),
                jnp.sum(dq * xq, axis=0, keepdims=True), jnp.sum(dkv * xk, axis=0, keepdims=True))

    dp_cq, dp_ckv, dp_kr, dg_q, dg_kv = _rowwise(
        f_mla_prep_bwd, [full(dcqn), full(dckvn), full(dkrr), blk(p, "cq"), blk(p, "ckv"), full(tab)],
        [w["mla_q_norm_g"], w["mla_kv_norm_g"]], [(pw["cq"], BF16), (pw["ckv"], BF16), (LANE, BF16)],
        [pw["cq"], pw["ckv"]], tile=512, name="mla_prep_bwd")

    pieces = dict(v=dp_v, g=dp_g, ga=dp_ga, gb=dp_gb, q=dp_q, k=dp_k, cq=dp_cq, ckv=dp_ckv, gk=dp_gk, kr=dp_kr)
    dp = jnp.concatenate([pieces[n] for n in MY_ORDER], axis=1)
    big["w_in"] = riding("in_proj_dw", lambda r: _mm(h, dp, ta=True, out_dtype=BF16, name="mm_in_proj_dw", ride=r))
    dh = riding("in_proj_dx", lambda r: _mm(dp, w["w_in"], tb=True, name="mm_in_proj_dx", ride=r))

    def f_ln1_bwd(xv, dhv, dx2v, mod, g):
        r = _rstd(xv)
        xh = xv * r
        dn = dhv * (1.0 + mod[1:2])
        return (dx2v + _rms_bwd(dn * g, xh, r),
                jnp.sum(dhv * (xh * g), axis=0, keepdims=True), jnp.sum(dhv, axis=0, keepdims=True),
                jnp.sum(dn * xh, axis=0, keepdims=True))

    grad_x, dscale_m, dshift_m, dg1 = _rowwise(f_ln1_bwd, [full(x), full(dh), full(dx2)], [mod8, g1],
                                               [(d, F32)], [d, d, d], tile=256, name="ln1_bwd")

    dmod = jnp.concatenate([dshift_m, dscale_m, dgate_m, dshift_f, dscale_f, dgate_f], axis=1)
    small = dict(ada_b=dmod, norm_mix_g=dg1, gla_gk_b=dgk_b, gla_onorm_g=dg_on, mla_q_norm_g=dg_q,
                 mla_kv_norm_g=dg_kv, norm_ffn_g=dg2, final_norm_g=dg3)
    return loss_v[0, 0], grad_x, big, small


N_PEER = N_DEV - 1


def _exchange_copies(ins, outs, sems, scatter):
    send_sems, recv_sems, local_sems = sems
    x, y, c = lax.axis_index("x"), lax.axis_index("y"), lax.axis_index("c")
    me = 4 * x + 2 * y + c
    peers = []
    for rel in range(1, N_DEV):
        px = 1 - x if rel & 4 else x
        py = 1 - y if rel & 2 else y
        pc = 1 - c if rel & 1 else c
        peers.append(((px, py, pc), 4 * px + 2 * py + pc))

    def remote(a, k, src_slot, dst_slot):
        src = ins[a].at[src_slot] if scatter else ins[a]
        return pltpu.make_async_remote_copy(
            src_ref=src, dst_ref=outs[a].at[dst_slot], send_sem=send_sems.at[a * N_PEER + k],
            recv_sem=recv_sems.at[a * N_PEER + k], device_id=peers[k][0], device_id_type=pl.DeviceIdType.MESH)

    local, sends, recvs = [], [], []
    for a in range(len(ins)):
        src = ins[a].at[me] if scatter else ins[a]
        local.append(pltpu.make_async_copy(src, outs[a].at[me], local_sems.at[a]))
        for k in range(N_PEER):
            sends.append(remote(a, k, peers[k][1], me))
            recvs.append(remote(a, k, peers[k][1], peers[k][1]))
    return local, sends, recvs


def _exchange_start(ins, outs, sems, scatter):
    local, sends, _ = _exchange_copies(ins, outs, sems, scatter)
    for cp in local + sends:
        cp.start()


def _exchange_wait(ins, outs, sems, scatter):
    local, sends, recvs = _exchange_copies(ins, outs, sems, scatter)
    for cp in recvs:
        cp.wait_recv()
    for cp in sends:
        cp.wait_send()
    for cp in local:
        cp.wait()


def _exchange_shapes(arrs, scatter):
    n = len(arrs)
    out_shape = [jax.ShapeDtypeStruct(a.shape if scatter else (N_DEV,) + a.shape, a.dtype) for a in arrs]
    sems = [pltpu.SemaphoreType.DMA((n * N_PEER,)), pltpu.SemaphoreType.DMA((n * N_PEER,)),
            pltpu.SemaphoreType.DMA((n,))]
    return out_shape, sems


def _exchange(arrs, *, scatter, name):
    n = len(arrs)

    def body(*refs):
        ins, outs, sems = refs[:n], refs[n:2 * n], refs[2 * n:]
        _exchange_start(ins, outs, sems, scatter)
        _exchange_wait(ins, outs, sems, scatter)

    hbm = pl.BlockSpec(memory_space=pltpu.HBM)
    out_shape, sems = _exchange_shapes(arrs, scatter)
    return pl.pallas_call(body, name=name, in_specs=[hbm] * n, out_specs=[hbm] * n, out_shape=out_shape,
                          scratch_shapes=sems)(*arrs)


def _gather_once_per_chip(arrs, *, name):
    n = len(arrs)

    def body(*refs):
        ins, outs = refs[:n], refs[n:2 * n]
        send_sems, recv_sems, local_sems = refs[2 * n:]
        x, y, c = lax.axis_index("x"), lax.axis_index("y"), lax.axis_index("c")
        sibling = (x, y, 1 - c)
        chips = [(1 - x, y), (x, 1 - y), (1 - x, 1 - y)]

        def slot(px, py, pc):
            return 4 * px + 2 * py + pc

        def copy(a, k, block, to, src=None):
            dst = outs[a].at[slot(*block)]
            return pltpu.make_async_remote_copy(
                src_ref=dst if src is None else src, dst_ref=dst, send_sem=send_sems.at[a * N_PEER + k],
                recv_sem=recv_sems.at[a * N_PEER + k], device_id=to, device_id_type=pl.DeviceIdType.MESH)

        local, sends = [], []
        for a in range(n):
            local.append(pltpu.make_async_copy(ins[a], outs[a].at[slot(x, y, c)], local_sems.at[a]))
            sends.append(copy(a, 0, (x, y, c), sibling, src=ins[a]))
            sends += [copy(a, 1 + j, (x, y, c), (*chip, c), src=ins[a]) for j, chip in enumerate(chips)]
        for cp in local + sends:
            cp.start()
        for a in range(n):
            for j, chip in enumerate(chips):
                copy(a, 1 + j, (*chip, c), (x, y, c)).wait_recv()
                sends.append(copy(a, 4 + j, (*chip, c), sibling))
                sends[-1].start()
        for a in range(n):
            copy(a, 0, sibling, (x, y, c)).wait_recv()
            for j, chip in enumerate(chips):
                copy(a, 4 + j, (*chip, 1 - c), (x, y, c)).wait_recv()
        for cp in sends:
            cp.wait_send()
        for cp in local:
            cp.wait()

    hbm = pl.BlockSpec(memory_space=pltpu.HBM)
    out_shape, sems = _exchange_shapes(arrs, False)
    return pl.pallas_call(body, name=name, in_specs=[hbm] * n, out_specs=[hbm] * n, out_shape=out_shape,
                          scratch_shapes=sems)(*arrs)


def _call(body, *, name, grid, in_specs, out_specs, out_shape, scratch_shapes, sem, args, ride=None,
          vmem_limit=VMEM_LIMIT):
    if ride is None:
        res = pl.pallas_call(body, name=name, grid=grid, in_specs=in_specs, out_specs=out_specs, out_shape=out_shape,
                             scratch_shapes=scratch_shapes, compiler_params=_cparams(sem, vmem_limit))(*args)
        return res, None
    arrs, scatter = ride
    n, n_in, n_out, n_scr = len(arrs), len(in_specs), len(out_specs), len(scratch_shapes)
    x_shape, x_sems = _exchange_shapes(arrs, scatter)

    def hosted(*refs):
        c_in, x_in = refs[:n_in], refs[n_in:n_in + n]
        c_out, x_out = refs[n_in + n:n_in + n + n_out], refs[n_in + n + n_out:n_in + 2 * n + n_out]
        scr = refs[n_in + 2 * n + n_out:]
        c_scr, sems = scr[:n_scr], scr[n_scr:]
        first = functools.reduce(jnp.logical_and, [pl.program_id(a) == 0 for a in range(len(grid))])
        last = functools.reduce(jnp.logical_and, [pl.program_id(a) == grid[a] - 1 for a in range(len(grid))])

        @pl.when(first)
        def _():
            _exchange_start(x_in, x_out, sems, scatter)

        body(*c_in, *c_out, *c_scr)

        @pl.when(last)
        def _():
            _exchange_wait(x_in, x_out, sems, scatter)

    hbm = pl.BlockSpec(memory_space=pltpu.HBM)
    res = pl.pallas_call(
        hosted, name=name, grid=grid, in_specs=list(in_specs) + [hbm] * n, out_specs=list(out_specs) + [hbm] * n,
        out_shape=list(out_shape) + x_shape, scratch_shapes=list(scratch_shapes) + x_sems,
        compiler_params=_cparams(("arbitrary",) * len(grid), vmem_limit))(*args, *arrs)
    return res[:n_out], res[n_out:]


def _adamw_math(w, g, m, v):
    m_new = ADAM_B1 * m + (1.0 - ADAM_B1) * g
    v_new = ADAM_B2 * v + (1.0 - ADAM_B2) * (g * g)
    m_hat = m_new / (1.0 - ADAM_B1 ** ADAM_STEP)
    v_hat = v_new / (1.0 - ADAM_B2 ** ADAM_STEP)
    delta = -ADAM_LR * (m_hat / (jnp.sqrt(v_hat) + ADAM_EPS) + ADAM_WD * w)
    return delta, m_new, v_new


def _adamw(w, g, m, v, *, name):
    _, r, c = w.shape
    slots = g.ndim == 3
    tr = r
    for cand in (128, 64, 32, 16):
        if r % cand == 0 and r > cand:
            tr = cand
            break

    def body(w_ref, g_ref, m_ref, v_ref, go_ref, d_ref, mo_ref, vo_ref):
        if slots:
            gv = g_ref[0].astype(F32)
            for s in range(1, N_DEV):
                gv = gv + g_ref[s].astype(F32)
        else:
            gv = g_ref[...]
        delta, m_new, v_new = _adamw_math(w_ref[0], gv, m_ref[0], v_ref[0])
        go_ref[0] = gv
        d_ref[0] = delta
        mo_ref[0] = m_new
        vo_ref[0] = v_new

    spec = pl.BlockSpec((1, tr, c), lambda i: (0, i, 0))
    g_spec = pl.BlockSpec((N_DEV, tr, c), lambda i: (0, i, 0)) if slots else pl.BlockSpec((tr, c), lambda i: (i, 0))
    return pl.pallas_call(
        body, name=name, grid=(r // tr,), in_specs=[spec, g_spec, spec, spec], out_specs=[spec] * 4,
        out_shape=[jax.ShapeDtypeStruct((1, r, c), F32)] * 4, compiler_params=_cparams(("parallel",)),
    )(w, g, m, v)


def _unshard_cols(g):
    return jnp.transpose(g, (1, 0, 2)).reshape(g.shape[1], -1)

def _shard_cols(full):
    r = full.shape[0]
    return jnp.transpose(full.reshape(r, N_DEV, -1), (1, 0, 2))


def _w_in_to_mine(w_ref_layout, d):
    wd, pw, ref_off, _, _ = _in_layout(d)
    cols = []
    for n in MY_ORDER:
        piece = lax.slice_in_dim(w_ref_layout, ref_off[n], ref_off[n] + wd[n], axis=1)
        if pw[n] != wd[n]:
            piece = jnp.pad(piece, ((0, 0), (0, pw[n] - wd[n])))
        cols.append(piece)
    return jnp.concatenate(cols, axis=1)


def _w_in_from_mine(g_mine, d):
    wd, _, _, my_off, _ = _in_layout(d)
    return jnp.concatenate([lax.slice_in_dim(g_mine, my_off[n], my_off[n] + wd[n], axis=1) for n in IN_NAMES],
                           axis=1)


def _w_in_pieces(d):
    wd, _, ref_off, _, total = _in_layout(d)
    shard = sum(wd.values()) // N_DEV
    out = []
    for n in IN_NAMES:
        r0, r1 = ref_off[n], ref_off[n] + wd[n]
        for s in range(N_DEV):
            lo, hi = max(r0, s * shard), min(r1, (s + 1) * shard)
            if lo < hi:
                out.append((n, s, lo - s * shard, lo - r0, hi - lo))
    return out


def _w_in_gathered_to_mine(g, d):
    wd, pw, _, _, _ = _in_layout(d)
    pieces = _w_in_pieces(d)
    cols = []
    for n in MY_ORDER:
        cols += [lax.slice_in_dim(g[s], a, a + wdt, axis=1) for (m, s, a, _, wdt) in pieces if m == n]
        if pw[n] != wd[n]:
            cols.append(jnp.zeros((g.shape[1], pw[n] - wd[n]), g.dtype))
    return jnp.concatenate(cols, axis=1)


def _w_in_mine_to_slabs(g_mine, d):
    _, _, _, my_off, _ = _in_layout(d)
    pieces = _w_in_pieces(d)
    slabs = []
    for s in range(N_DEV):
        parts = [lax.slice_in_dim(g_mine, my_off[m] + b, my_off[m] + b + wdt, axis=1)
                 for (m, s2, _, b, wdt) in pieces if s2 == s]
        slabs.append(jnp.concatenate(parts, axis=1))
    return jnp.stack(slabs)


def _wuq_to_mine(wq):
    r = wq.shape[0]
    w3 = wq.reshape(r, MLA_HEADS, MLA_NOPE + MLA_ROPE)
    w3 = jnp.pad(w3, ((0, 0), (0, 0), (0, MLA_QK_PAD - MLA_NOPE - MLA_ROPE)))
    return w3.reshape(r, MLA_HEADS * MLA_QK_PAD)


def _wuq_from_mine(g):
    r = g.shape[0]
    return g.reshape(r, MLA_HEADS, MLA_QK_PAD)[:, :, :MLA_NOPE + MLA_ROPE].reshape(r, -1)


def _wukv_to_mine(wkv):
    r = wkv.shape[0]
    w3 = wkv.reshape(r, MLA_HEADS, MLA_NOPE + MLA_V)
    return jnp.concatenate([w3[:, :, :MLA_NOPE].reshape(r, -1), w3[:, :, MLA_NOPE:].reshape(r, -1)], axis=1)


def _wukv_from_mine(g):
    r = g.shape[0]
    kn = g[:, :MLA_HEADS * MLA_NOPE].reshape(r, MLA_HEADS, MLA_NOPE)
    vv = g[:, MLA_HEADS * MLA_NOPE:].reshape(r, MLA_HEADS, MLA_V)
    return jnp.concatenate([kn, vv], axis=2).reshape(r, -1)


COL_SHARDED = ("w_in", "gla_gk_w", "mla_wuq", "mla_wukv", "ffn_w_in")


def _gathered_to_mine(name, g, d):
    if name == "w_in":
        return _w_in_gathered_to_mine(g, d)
    full = _unshard_cols(g) if name in COL_SHARDED else g.reshape(-1, g.shape[-1])
    if name == "gla_gk_w":
        return jnp.pad(full, ((0, LANE - GLA_GATE_RANK), (0, 0)))
    if name == "mla_wuq":
        return _wuq_to_mine(full)
    if name == "mla_wukv":
        return _wukv_to_mine(full)
    return full


def _grad_to_slabs(name, g, d):
    if g.ndim == 3:
        return g.astype(BF16)
    if name == "w_in":
        return _w_in_mine_to_slabs(g, d).astype(BF16)
    if name == "mla_wuq":
        g = _wuq_from_mine(g)
    elif name == "mla_wukv":
        g = _wukv_from_mine(g)
    s = _shard_cols(g) if name in COL_SHARDED else g.reshape(N_DEV, -1, g.shape[-1])
    return s.astype(BF16)


class _Rides:
    GATHER = {"in_proj": ("gla_wo", "mla_wuq", "mla_wukv", "mla_wo", "w_out"),
              "flash_fwd": ("ffn_w_in", "ffn_w_down")}
    SCATTER = {"flash_bwd": ("ffn_w_in", "ffn_w_down", "w_out", "gla_wo", "mla_wo", "gla_gk_w"),
               "in_proj_dw": ("mla_wuq", "mla_wukv"),
               "in_proj_dx": ("w_in",)}

    def __init__(self, send, d):
        self.send, self.d, self.recv = send, d, {}

    def ride(self, stage, grads):
        if stage in self.GATHER:
            return [self.send[n] for n in self.GATHER[stage]], False
        return [_grad_to_slabs(n, grads[n], self.d) for n in self.SCATTER[stage]], True

    def done(self, stage, rode, w):
        if stage in self.GATHER:
            for n, g in zip(self.GATHER[stage], rode):
                w[n] = _gathered_to_mine(n, g, self.d)
        else:
            self.recv.update(zip(self.SCATTER[stage], rode))


def kernel(x, c, positions, ada_w, ada_b, norm_mix_g, w_in, gla_gk_w, gla_gk_b, gla_onorm_g, gla_wo, mla_q_norm_g, mla_wuq, mla_kv_norm_g, mla_wukv, mla_wo, w_out, norm_ffn_g, ffn_w_in, ffn_w_down, final_norm_g, loss_target, m_ada_w, m_ada_b, m_norm_mix_g, m_w_in, m_gla_gk_w, m_gla_gk_b, m_gla_onorm_g, m_gla_wo, m_mla_q_norm_g, m_mla_wuq, m_mla_kv_norm_g, m_mla_wukv, m_mla_wo, m_w_out, m_norm_ffn_g, m_ffn_w_in, m_ffn_w_down, m_final_norm_g, v_ada_w, v_ada_b, v_norm_mix_g, v_w_in, v_gla_gk_w, v_gla_gk_b, v_gla_onorm_g, v_gla_wo, v_mla_q_norm_g, v_mla_wuq, v_mla_kv_norm_g, v_mla_wukv, v_mla_wo, v_w_out, v_norm_ffn_g, v_ffn_w_in, v_ffn_w_down, v_final_norm_g):
    wts = dict(ada_w=ada_w, ada_b=ada_b, norm_mix_g=norm_mix_g, w_in=w_in, gla_gk_w=gla_gk_w, gla_gk_b=gla_gk_b,
               gla_onorm_g=gla_onorm_g, gla_wo=gla_wo, mla_q_norm_g=mla_q_norm_g, mla_wuq=mla_wuq,
               mla_kv_norm_g=mla_kv_norm_g, mla_wukv=mla_wukv, mla_wo=mla_wo, w_out=w_out, norm_ffn_g=norm_ffn_g,
               ffn_w_in=ffn_w_in, ffn_w_down=ffn_w_down, final_norm_g=final_norm_g)
    mom_m = dict(zip(WEIGHTS, (m_ada_w, m_ada_b, m_norm_mix_g, m_w_in, m_gla_gk_w, m_gla_gk_b, m_gla_onorm_g,
                               m_gla_wo, m_mla_q_norm_g, m_mla_wuq, m_mla_kv_norm_g, m_mla_wukv, m_mla_wo, m_w_out,
                               m_norm_ffn_g, m_ffn_w_in, m_ffn_w_down, m_final_norm_g)))
    mom_v = dict(zip(WEIGHTS, (v_ada_w, v_ada_b, v_norm_mix_g, v_w_in, v_gla_gk_w, v_gla_gk_b, v_gla_onorm_g,
                               v_gla_wo, v_mla_q_norm_g, v_mla_wuq, v_mla_kv_norm_g, v_mla_wukv, v_mla_wo, v_w_out,
                               v_norm_ffn_g, v_ffn_w_in, v_ffn_w_down, v_final_norm_g)))
    seq, d = x.shape[1], x.shape[2]
    me = 4 * lax.axis_index("x") + 2 * lax.axis_index("y") + lax.axis_index("c")

    def two_d(a):
        return a.reshape(a.shape[-2], a.shape[-1]) if a.ndim >= 2 else a.reshape(1, -1)

    shard = {n: two_d(wts[n]) for n in BIG}
    send = {n: shard[n].astype(F32 if n == "gla_gk_w" else BF16) for n in BIG}
    got = _gather_once_per_chip([send["w_in"], send["gla_gk_w"], two_d(c)], name="comm_all_gather_first")
    c_all = got[2].reshape(N_DEV, d)
    w = dict(
        w_in=_gathered_to_mine("w_in", got[0], d), gla_gk_w=_gathered_to_mine("gla_gk_w", got[1], d),
        gla_gk_b=two_d(gla_gk_b), gla_onorm_g=two_d(gla_onorm_g), mla_q_norm_g=two_d(mla_q_norm_g),
        mla_kv_norm_g=two_d(mla_kv_norm_g), norm_mix_g=two_d(norm_mix_g), norm_ffn_g=two_d(norm_ffn_g),
        final_norm_g=two_d(final_norm_g))
    rides = _Rides(send, d)

    c_pad = jnp.pad(c_all, ((0, 16 - N_DEV), (0, 0)))
    (c_act,) = _rowwise(lambda cv: cv * _sigmoid(cv), [(c_pad, d, 0)], [], [(d, F32)], tile=16, name="silu_c")
    ada_w2 = two_d(ada_w)
    mod_part = _mm(c_act, ada_w2, name="mm_ada")[:N_DEV]
    (mod_all,) = _exchange([mod_part], scatter=False, name="comm_all_gather_mod")
    mod_mine = lax.dynamic_index_in_dim(mod_all, me, axis=1, keepdims=False).reshape(1, -1) + two_d(ada_b)
    mod8 = jnp.pad(mod_mine.reshape(6, d), ((0, 2), (0, 0)))

    inv_freq = ROPE_THETA ** (-jnp.arange(0, MLA_ROPE, 2, dtype=F32) / MLA_ROPE)
    ang = positions.reshape(seq, 1).astype(F32) * inv_freq[None, :]
    cos, sin, z32 = jnp.cos(ang), jnp.sin(ang), jnp.zeros((seq, 32), F32)
    tab = jnp.concatenate([cos, cos, z32, z32, -sin, z32, z32, z32, z32, sin, z32, z32], axis=1)
    loss_local, grad_x, _, small = _local_step(x.reshape(seq, d), loss_target.reshape(seq, d), tab, mod8, w, rides)

    recv = rides.recv
    pack = jnp.concatenate([small[n] for n in SMALL], axis=1)
    (pack_all,) = _exchange([pack], scatter=False, name="comm_all_gather_small")
    pack_all = pack_all.reshape(N_DEV, -1)

    res = {}
    for n in BIG:
        res[n] = _adamw(wts[n], recv[n], mom_m[n], mom_v[n], name="adamw_" + n)
    n_ada = ada_w2.shape[1]
    dmod_cols = lax.dynamic_slice_in_dim(pack_all[:, :6 * d], me * n_ada, n_ada, axis=1)

    def f_outer(cat, dm):
        acc = cat[:, 0:1] * dm[0:1]
        for b in range(1, N_DEV):
            acc = acc + cat[:, b:b + 1] * dm[b:b + 1]
        return acc

    (g_ada_w,) = _rowwise(f_outer, [(jnp.transpose(c_act[:N_DEV]), N_DEV, 0)], [dmod_cols], [(n_ada, F32)],
                          tile=256, name="ada_w_grad")
    res["ada_w"] = _adamw(ada_w, g_ada_w, m_ada_w, v_ada_w, name="adamw_ada_w")
    w_small = jnp.concatenate([two_d(wts[n]) for n in SMALL], axis=1)[None]
    m_small = jnp.concatenate([two_d(mom_m[n]) for n in SMALL], axis=1)[None]
    v_small = jnp.concatenate([two_d(mom_v[n]) for n in SMALL], axis=1)[None]
    small_res = _adamw(w_small, pack_all.reshape(N_DEV, 1, -1), m_small, v_small, name="adamw_small")
    off = 0
    for n in SMALL:
        width = wts[n].size
        res[n] = tuple(lax.slice_in_dim(a, off, off + width, axis=2) for a in small_res)
        off += width

    loss = lax.psum(loss_local, ("x", "y", "c"))
    outs = [loss, grad_x.reshape(x.shape)]
    for kind in range(4):
        outs += [res[n][kind].reshape(wts[n].shape) for n in WEIGHTS]
    return tuple(outs)
```

```python
import functools

import jax
import jax.numpy as jnp
from jax import lax
from jax.experimental import pallas as pl
from jax.experimental.pallas import tpu as pltpu

F32 = jnp.float32
BF16 = jnp.bfloat16

N_DEV = 8
GLA_HEADS = 4
GLA_DK = 256
GLA_DV = 512
GLA_GATE_RANK = 16
GLA_GATE_NORMALIZER = 16.0
GLA_CHUNK = 64
MLA_HEADS = 16
MLA_NOPE = 128
MLA_ROPE = 64
MLA_V = 128
MLA_QK_PAD = 256
ROPE_THETA = 10000.0
NORM_EPS = 1e-6
ATT_SCALE = (MLA_NOPE + MLA_ROPE) ** -0.5
GLA_QSCALE = GLA_DK ** -0.5

ADAM_LR = 0.001
ADAM_B1 = 0.9
ADAM_B2 = 0.999
ADAM_EPS = 1e-08
ADAM_WD = 0.01
ADAM_STEP = 10

LANE = 128
VMEM_LIMIT = 48 * 1024 * 1024
FLASH_BWD_VMEM = 58 * 1024 * 1024
MM_TILE_BYTES = 6 * 1024 * 1024
LOG2E = 1.4426950408889634
LN2 = 0.6931471805599453
NEG = -1e30

IN_NAMES = ("q", "k", "v", "g", "gk", "cq", "ckv", "kr", "ga", "gb")
MY_ORDER = ("v", "g", "ga", "gb", "q", "k", "cq", "ckv", "gk", "kr")

WEIGHTS = ("ada_w", "ada_b", "norm_mix_g", "w_in", "gla_gk_w", "gla_gk_b", "gla_onorm_g", "gla_wo",
           "mla_q_norm_g", "mla_wuq", "mla_kv_norm_g", "mla_wukv", "mla_wo", "w_out", "norm_ffn_g",
           "ffn_w_in", "ffn_w_down", "final_norm_g")
BIG = ("w_in", "gla_gk_w", "gla_wo", "mla_wuq", "mla_wukv", "mla_wo", "w_out", "ffn_w_in", "ffn_w_down")
SMALL = ("ada_b", "norm_mix_g", "gla_gk_b", "gla_onorm_g", "mla_q_norm_g", "mla_kv_norm_g", "norm_ffn_g",
         "final_norm_g")


def _in_layout(d):
    w = dict(q=d // 2, k=d // 2, v=d, g=d, gk=GLA_GATE_RANK, cq=d // 4, ckv=512, kr=MLA_ROPE, ga=d, gb=d)
    pw = {n: -(-w[n] // LANE) * LANE for n in w}
    ref_off, o = {}, 0
    for n in IN_NAMES:
        ref_off[n] = o
        o += w[n]
    my_off, o = {}, 0
    for n in MY_ORDER:
        assert o % pw[n] == 0
        my_off[n] = o
        o += pw[n]
    return w, pw, ref_off, my_off, o


def _cparams(sem=None, vmem_limit=VMEM_LIMIT):
    return pltpu.CompilerParams(dimension_semantics=sem, vmem_limit_bytes=vmem_limit)


def _dot(a, b, ca=1, cb=0):
    return lax.dot_general(a, b, (((ca,), (cb,)), ((), ())), preferred_element_type=F32)


def _tile(n, cap):
    if n <= cap:
        return n
    t = (cap // LANE) * LANE
    while t >= LANE:
        if n % t == 0:
            return t
        t -= LANE
    return n


def _mm(a, b, *, ta=False, tb=False, out_dtype=F32, name, ride=None, slabs=False):
    m, k = (a.shape[1], a.shape[0]) if ta else a.shape
    n = b.shape[0] if tb else b.shape[1]
    assert k == (b.shape[1] if tb else b.shape[0])
    wide = max(a.dtype.itemsize, b.dtype.itemsize) > 2
    tm, tn, tk = _tile(m, 1024), _tile(n, 1024), _tile(k, MM_TILE_BYTES // (1024 * (4 if wide else 2)))
    if slabs:
        tn = n // N_DEV
        assert tn % LANE == 0
    nk = k // tk

    def product(a_ref, b_ref):
        return _dot(a_ref[...].astype(BF16), b_ref[...].astype(BF16), 0 if ta else 1, 1 if tb else 0)

    def store(o_ref, val):
        if slabs:
            o_ref[0] = val.astype(o_ref.dtype)
        else:
            o_ref[...] = val.astype(o_ref.dtype)

    def body_one(a_ref, b_ref, o_ref):
        store(o_ref, product(a_ref, b_ref))

    def body_acc(a_ref, b_ref, o_ref, acc_ref):
        kk = pl.program_id(2)

        @pl.when(kk == 0)
        def _():
            acc_ref[...] = jnp.zeros_like(acc_ref)

        acc_ref[...] += product(a_ref, b_ref)

        @pl.when(kk == nk - 1)
        def _():
            store(o_ref, acc_ref[...])

    a_spec = (pl.BlockSpec((tk, tm), lambda i, j, kk: (kk, i)) if ta
              else pl.BlockSpec((tm, tk), lambda i, j, kk: (i, kk)))
    b_spec = (pl.BlockSpec((tn, tk), lambda i, j, kk: (j, kk)) if tb
              else pl.BlockSpec((tk, tn), lambda i, j, kk: (kk, j)))
    (out,), rode = _call(
        body_one if nk == 1 else body_acc, name=name, grid=(m // tm, n // tn, nk), in_specs=[a_spec, b_spec],
        out_specs=[pl.BlockSpec((1, tm, tn), lambda i, j, kk: (j, i, 0)) if slabs
                   else pl.BlockSpec((tm, tn), lambda i, j, kk: (i, j))],
        out_shape=[jax.ShapeDtypeStruct((N_DEV, m, tn) if slabs else (m, n), out_dtype)],
        scratch_shapes=[] if nk == 1 else [pltpu.VMEM((tm, tn), F32)],
        sem=("parallel", "parallel", "arbitrary"), args=(a, b), ride=ride)
    return out if ride is None else (out, rode)


def _rowwise(fn, rows, vecs, outs, sums=(), *, tile, name):
    t = rows[0][0].shape[0]
    tile = min(tile, t)
    assert t % tile == 0
    n_rows, n_vecs, n_outs = len(rows), len(vecs), len(outs)

    def body(*refs):
        ins = [r[...].astype(F32) for r in refs[:n_rows + n_vecs]]
        res = fn(*ins)
        if not isinstance(res, (tuple, list)):
            res = (res,)
        out_refs = refs[n_rows + n_vecs:]
        for r, val in zip(out_refs[:n_outs], res[:n_outs]):
            r[...] = val.astype(r.dtype)
        if sums:
            first = pl.program_id(0) == 0
            for r, val in zip(out_refs[n_outs:], res[n_outs:]):
                @pl.when(first)
                def _(r=r):
                    r[...] = jnp.zeros_like(r)
                r[...] += val

    in_specs = [pl.BlockSpec((tile, w), lambda i, cb=cb: (i, cb)) for (_, w, cb) in rows]
    in_specs += [pl.BlockSpec(v.shape, lambda i: (0, 0)) for v in vecs]
    out_specs = [pl.BlockSpec((tile, w), lambda i: (i, 0)) for (w, _) in outs]
    out_specs += [pl.BlockSpec((1, w), lambda i: (0, 0)) for w in sums]
    out_shape = [jax.ShapeDtypeStruct((t, w), dt) for (w, dt) in outs]
    out_shape += [jax.ShapeDtypeStruct((1, w), F32) for w in sums]
    res = pl.pallas_call(
        body, name=name, grid=(t // tile,), in_specs=in_specs, out_specs=out_specs, out_shape=out_shape,
        compiler_params=_cparams(("arbitrary",)),
    )(*[r[0] for r in rows], *vecs)
    return res


def _rstd(x):
    return lax.rsqrt(jnp.mean(x * x, axis=-1, keepdims=True) + NORM_EPS)


def _sigmoid(x):
    return 1.0 / (1.0 + jnp.exp(-x))


def _rms_bwd(dxh, xh, r):
    return r * (dxh - xh * jnp.mean(dxh * xh, axis=-1, keepdims=True))


def _rope(t, tab, sign):
    cosf, sin_a, sin_b = tab[:, :LANE], tab[:, LANE:2 * LANE], tab[:, 2 * LANE:]
    return t * cosf + sign * (pltpu.roll(t, 96, 1) * sin_a + pltpu.roll(t, 32, 1) * sin_b)


def _split3(x):
    hi = x.astype(BF16)
    r1 = x - hi.astype(F32)
    mid = r1.astype(BF16)
    lo = (r1 - mid.astype(F32)).astype(BF16)
    return hi, mid, lo


def _tri_sum(tri_bf16, x):
    hi, mid, lo = _split3(x)
    return _dot(tri_bf16, hi) + _dot(tri_bf16, mid) + _dot(tri_bf16, lo)


def _dot_nt2(a, b):
    a_hi = a.astype(BF16)
    a_lo = (a - a_hi.astype(F32)).astype(BF16)
    b_hi = b.astype(BF16)
    b_lo = (b - b_hi.astype(F32)).astype(BF16)
    return _dot(a_hi, b_hi, 1, 1) + _dot(a_hi, b_lo, 1, 1) + _dot(a_lo, b_hi, 1, 1)


def _gla_specs(t, rows, lay, reverse):
    nb = t // rows
    blk = (lambda i: nb - 1 - i) if reverse else (lambda i: i)
    qb, kb = lay["q"] // GLA_DK, lay["k"] // GLA_DK
    vb = lay["v"] // GLA_DV
    return [
        pl.BlockSpec((rows, GLA_DK), lambda h, i: (blk(i), qb + h)),
        pl.BlockSpec((rows, GLA_DK), lambda h, i: (blk(i), kb + h)),
        pl.BlockSpec((rows, GLA_DV), lambda h, i: (blk(i), vb + h)),
        pl.BlockSpec((rows, GLA_DK), lambda h, i: (blk(i), h)),
    ], blk


def _gla_fwd(p, la, lay):
    t = p.shape[0]
    rows = min(512, t)
    nb, nc = t // rows, rows // GLA_CHUNK
    c64 = GLA_CHUNK

    def body(q_ref, k_ref, v_ref, la_ref, o_ref, st_ref, s_ref):
        @pl.when(pl.program_id(1) == 0)
        def _():
            s_ref[...] = jnp.zeros_like(s_ref)

        r = lax.broadcasted_iota(jnp.int32, (c64, c64), 0)
        cc = lax.broadcasted_iota(jnp.int32, (c64, c64), 1)
        tril = cc <= r
        tril_b = tril.astype(BF16)
        for c in range(nc):
            sl = pl.ds(c * c64, c64)
            b = _tri_sum(tril_b, la_ref[sl, :])
            b_last = b[c64 - 1:c64, :]
            q = q_ref[sl, :].astype(F32) * GLA_QSCALE
            k = k_ref[sl, :].astype(F32)
            v = v_ref[sl, :].astype(BF16)
            qt_f = q * jnp.exp(b)
            qt = qt_f.astype(BF16)
            kh = (k * jnp.exp(b_last - b)).astype(BF16)
            s_prev = s_ref[...]
            st_ref[0, c] = s_prev
            att = jnp.where(tril, _dot_nt2(qt_f, k * jnp.exp(-b)), 0.0)
            o_ref[sl, :] = _dot(qt, s_prev.astype(BF16), 1, 1) + _dot(att.astype(BF16), v)
            s_ref[...] = s_prev * jnp.exp(b_last) + _dot(v, kh, 0, 0)

    in_specs, _ = _gla_specs(t, rows, lay, False)
    return pl.pallas_call(
        body, name="gla_fwd", grid=(GLA_HEADS, nb), in_specs=in_specs,
        out_specs=[pl.BlockSpec((rows, GLA_DV), lambda h, i: (i, h)),
                   pl.BlockSpec((1, nc, GLA_DV, GLA_DK), lambda h, i: (h, i, 0, 0))],
        out_shape=[jax.ShapeDtypeStruct((t, GLA_HEADS * GLA_DV), F32),
                   jax.ShapeDtypeStruct((GLA_HEADS, t // c64, GLA_DV, GLA_DK), F32)],
        scratch_shapes=[pltpu.VMEM((GLA_DV, GLA_DK), F32)],
        compiler_params=_cparams(("parallel", "arbitrary")),
    )(p, p, p, la)


def _gla_bwd(p, la, do, states, lay):
    t = p.shape[0]
    rows = min(512, t)
    nb, nc = t // rows, rows // GLA_CHUNK
    c64 = GLA_CHUNK

    def body(q_ref, k_ref, v_ref, la_ref, do_ref, st_ref, dq_ref, dk_ref, dv_ref, dla_ref, ds_ref):
        @pl.when(pl.program_id(1) == 0)
        def _():
            ds_ref[...] = jnp.zeros_like(ds_ref)

        r = lax.broadcasted_iota(jnp.int32, (c64, c64), 0)
        cc = lax.broadcasted_iota(jnp.int32, (c64, c64), 1)
        tril = cc <= r
        tril_b = tril.astype(BF16)
        triu_b = (cc >= r).astype(BF16)
        for c in reversed(range(nc)):
            sl = pl.ds(c * c64, c64)
            b = _tri_sum(tril_b, la_ref[sl, :])
            b_last = b[c64 - 1:c64, :]
            eb, enb, ebl_b, ebl = jnp.exp(b), jnp.exp(-b), jnp.exp(b_last - b), jnp.exp(b_last)
            k = k_ref[sl, :].astype(F32)
            qt_f = q_ref[sl, :].astype(F32) * GLA_QSCALE * eb
            kt_f = k * enb
            kh_f = k * ebl_b
            qt, kt, kh = qt_f.astype(BF16), kt_f.astype(BF16), kh_f.astype(BF16)
            v_f = v_ref[sl, :].astype(F32)
            dout_f = do_ref[sl, :]
            v, dout = v_f.astype(BF16), dout_f.astype(BF16)
            s_prev = st_ref[0, c]
            ds_next = ds_ref[...]
            ds_next_b = ds_next.astype(BF16)
            att = jnp.where(tril, _dot_nt2(qt_f, kt_f), 0.0).astype(BF16)
            datt = jnp.where(tril, _dot_nt2(dout_f, v_f), 0.0).astype(BF16)
            dqt = _dot(dout, s_prev.astype(BF16)) + _dot(datt, kt)
            dkt = _dot(datt, qt, 0, 0)
            dv = _dot(att, dout, 0, 0) + _dot(kh, ds_next_b, 1, 1)
            dkh = _dot(v, ds_next_b)
            d_ebl = jnp.sum(ds_next * s_prev, axis=0, keepdims=True)
            ds_ref[...] = ds_next * ebl + _dot(dout, qt, 0, 0)
            db = dqt * qt_f - dkt * kt_f - dkh * kh_f
            db_last = ebl * d_ebl + jnp.sum(dkh * kh_f, axis=0, keepdims=True)
            dq_ref[sl, :] = (dqt * eb * GLA_QSCALE).astype(dq_ref.dtype)
            dk_ref[sl, :] = (dkt * enb + dkh * ebl_b).astype(dk_ref.dtype)
            dv_ref[sl, :] = dv.astype(dv_ref.dtype)
            dla_ref[sl, :] = _tri_sum(triu_b, db) + db_last

    in_specs, blk = _gla_specs(t, rows, lay, True)
    in_specs += [pl.BlockSpec((rows, GLA_DV), lambda h, i: (blk(i), h)),
                 pl.BlockSpec((1, nc, GLA_DV, GLA_DK), lambda h, i: (h, blk(i), 0, 0))]
    dk_spec = pl.BlockSpec((rows, GLA_DK), lambda h, i: (blk(i), h))
    return pl.pallas_call(
        body, name="gla_bwd", grid=(GLA_HEADS, nb), in_specs=in_specs,
        out_specs=[dk_spec, dk_spec, pl.BlockSpec((rows, GLA_DV), lambda h, i: (blk(i), h)), dk_spec],
        out_shape=[jax.ShapeDtypeStruct((t, GLA_HEADS * GLA_DK), BF16),
                   jax.ShapeDtypeStruct((t, GLA_HEADS * GLA_DK), BF16),
                   jax.ShapeDtypeStruct((t, GLA_HEADS * GLA_DV), BF16),
                   jax.ShapeDtypeStruct((t, GLA_HEADS * GLA_DK), F32)],
        scratch_shapes=[pltpu.VMEM((GLA_DV, GLA_DK), F32)],
        compiler_params=_cparams(("parallel", "arbitrary")),
    )(p, p, p, la, do, states)


def _diag_mask(rows, cols, row0):
    row = row0 + lax.broadcasted_iota(jnp.int32, (rows, cols), 0)
    col = lax.broadcasted_iota(jnp.int32, (rows, cols), 1)
    return col <= row


QK_SPARE = MLA_NOPE + MLA_ROPE
N_SPARE = 3


def _with_spare(x, col, lane0):
    lane = lax.broadcasted_iota(jnp.int32, x.shape, 1)
    for n, term in enumerate(_split3(col)):
        x = jnp.where(lane == lane0 + n, term, x)
    return x


def _spare_ones(shape, lane0):
    lane = lax.broadcasted_iota(jnp.int32, shape, 1)
    return ((lane >= lane0) & (lane < lane0 + N_SPARE)).astype(F32)


def _flash_tiles(t):
    tq = min(1024, t)
    halves = 2 if tq % 32 == 0 else 1
    return tq, t // tq, halves, tq // halves


def _flash_fwd(q, k, vx, ride=None):
    t = q.shape[0]
    tq, nq, halves, hr = _flash_tiles(t)
    dqk, dv = MLA_QK_PAD, MLA_V

    def body(q_ref, k_ref, v_ref, o_ref, qa_ref, m_ref, acc_ref, s_ref):
        i = pl.program_id(1)
        m_ref[...] = jnp.full_like(m_ref, NEG)
        acc_ref[...] = jnp.zeros_like(acc_ref)

        def scores(j, slot):
            kb = k_ref[pl.ds(pl.multiple_of(j * tq, tq), tq), :]
            for hh in range(halves):
                s_ref[slot, pl.ds(hh * hr, hr), :] = _dot(q_ref[pl.ds(hh * hr, hr), :], kb, 1, 1)

        def consume(j, slot, masked):
            vb = v_ref[pl.ds(pl.multiple_of(j * tq, tq), tq), :]
            for hh in range(halves):
                rs = pl.ds(hh * hr, hr)
                s = s_ref[slot, rs, :]
                if masked:
                    s = jnp.where(_diag_mask(hr, tq, hh * hr), s, NEG)
                m_old = m_ref[rs, :]
                m_new = jnp.maximum(m_old, jnp.max(s, axis=1, keepdims=True))
                pr = jnp.exp2(s - m_new)
                acc_ref[rs, :] = jnp.exp2(m_old - m_new) * acc_ref[rs, :] + _dot(pr.astype(BF16), vb)
                m_ref[rs, :] = m_new

        def two_blocks(jj, carry):
            scores(2 * jj + 1, 1)
            consume(2 * jj, 0, False)
            scores(2 * jj + 2, 0)
            consume(2 * jj + 1, 1, False)
            return carry

        scores(0, 0)
        lax.fori_loop(0, i // 2, two_blocks, 0)

        @pl.when(i % 2 == 0)
        def _():
            consume(i, 0, True)

        @pl.when(i % 2 == 1)
        def _():
            scores(i, 1)
            consume(i - 1, 0, False)
            consume(i, 1, True)

        acc = acc_ref[...]
        l = acc[:, dv:dv + 1]
        o_ref[...] = acc[:, :dv] / l
        qa_ref[...] = _with_spare(q_ref[...], -(m_ref[...] + jnp.log(l) * LOG2E), QK_SPARE)

    outs, rode = _call(
        body, name="mla_flash_fwd", grid=(MLA_HEADS, nq),
        in_specs=[pl.BlockSpec((tq, dqk), lambda h, i: (i, h)),
                  pl.BlockSpec((t, dqk), lambda h, i: (0, h)),
                  pl.BlockSpec((t, 2 * dv), lambda h, i: (0, h))],
        out_specs=[pl.BlockSpec((tq, dv), lambda h, i: (i, h)),
                   pl.BlockSpec((tq, dqk), lambda h, i: (i, h))],
        out_shape=[jax.ShapeDtypeStruct((t, MLA_HEADS * dv), F32),
                   jax.ShapeDtypeStruct((t, MLA_HEADS * dqk), BF16)],
        scratch_shapes=[pltpu.VMEM((tq, 1), F32), pltpu.VMEM((tq, 2 * dv), F32), pltpu.VMEM((2, tq, tq), F32)],
        sem=("parallel", "arbitrary"), args=(q, k, vx), ride=ride)
    return tuple(outs) if ride is None else (tuple(outs), rode)


def _flash_bwd(qa, k, vx, doa, ride=None):
    t = qa.shape[0]
    tq, nq, halves, hr = _flash_tiles(t)
    dqk, dv = MLA_QK_PAD, MLA_V

    def body(k_ref, v_ref, q_ref, do_ref, dq_ref, dk_ref, dv_ref, dq_acc, dk_acc, dv_acc):
        j = pl.program_id(1)

        @pl.when(j == 0)
        def _():
            dq_acc[...] = jnp.zeros_like(dq_acc)

        kb = k_ref[...]
        vb = v_ref[...]
        dk_acc[...] = jnp.zeros_like(dk_acc)
        dv_acc[...] = jnp.zeros_like(dv_acc)

        def step(i, masked):
            for hh in range(halves):
                rs = pl.ds(pl.multiple_of(i * tq + hh * hr, hr), hr)
                qb = q_ref[rs, :]
                dout = do_ref[rs, :]
                nc = (hh + 1) * hr if masked else tq
                s = _dot(qb, kb[:nc], 1, 1)
                if masked:
                    s = jnp.where(_diag_mask(hr, nc, hh * hr), s, NEG)
                pr = jnp.exp2(s)
                ds = (pr * _dot(dout, vb[:nc], 1, 1)).astype(BF16)
                dv_acc[pl.ds(0, nc), :] += _dot(pr.astype(BF16), dout, 0, 0)
                dk_acc[pl.ds(0, nc), :] += _dot(ds, qb, 0, 0)
                dq_acc[rs, :] += _dot(ds, kb[:nc])

        def loop_body(i, carry):
            step(i, False)
            return carry

        step(j, True)
        lax.fori_loop(j + 1, nq, loop_body, 0)
        dk_ref[...] = (dk_acc[...] * LN2).astype(dk_ref.dtype)
        dv_ref[...] = dv_acc[:, :dv].astype(dv_ref.dtype)

        @pl.when(j == nq - 1)
        def _():
            dq_ref[...] = (dq_acc[...] * ATT_SCALE).astype(dq_ref.dtype)

    outs, rode = _call(
        body, name="mla_flash_bwd", grid=(MLA_HEADS, nq),
        in_specs=[pl.BlockSpec((tq, dqk), lambda h, j: (j, h)),
                  pl.BlockSpec((tq, 2 * dv), lambda h, j: (j, h)),
                  pl.BlockSpec((t, dqk), lambda h, j: (0, h)),
                  pl.BlockSpec((t, 2 * dv), lambda h, j: (0, h))],
        out_specs=[pl.BlockSpec((t, dqk), lambda h, j: (0, h)),
                   pl.BlockSpec((tq, dqk), lambda h, j: (j, h)),
                   pl.BlockSpec((tq, dv), lambda h, j: (j, h))],
        out_shape=[jax.ShapeDtypeStruct((t, MLA_HEADS * dqk), BF16),
                   jax.ShapeDtypeStruct((t, MLA_HEADS * dqk), BF16),
                   jax.ShapeDtypeStruct((t, MLA_HEADS * dv), BF16)],
        scratch_shapes=[pltpu.VMEM((t, dqk), F32), pltpu.VMEM((tq, dqk), F32), pltpu.VMEM((tq, 2 * dv), F32)],
        sem=("parallel", "arbitrary"), args=(k, vx, qa, doa), ride=ride, vmem_limit=FLASH_BWD_VMEM)
    return tuple(outs) if ride is None else (tuple(outs), rode)


class _NoRides:
    def ride(self, stage, grads):
        return None

    def done(self, stage, rode, w):
        pass


def _local_step(x, target, tab, mod8, w, rides=None):
    t, d = x.shape
    rides = rides or _NoRides()
    big = {}

    def riding(stage, fn):
        r = rides.ride(stage, big)
        res = fn(r)
        if r is None:
            return res
        rides.done(stage, res[1], w)
        return res[0]
    _, pw, _, lay, _ = _in_layout(d)
    ffn = ((8 * d // 3 + 255) // 256) * 256

    def blk(arr, name):
        return (arr, pw[name], lay[name] // pw[name])

    def full(arr):
        return (arr, arr.shape[1], 0)

    g1, g2, g3 = w["norm_mix_g"], w["norm_ffn_g"], w["final_norm_g"]

    def f_ln1(xv, mod, g):
        return (xv * _rstd(xv) * g) * (1.0 + mod[1:2]) + mod[0:1]

    (h,) = _rowwise(f_ln1, [full(x)], [mod8, g1], [(d, BF16)], tile=256, name="ln1_modulate")
    p = riding("in_proj", lambda r: _mm(h, w["w_in"], tb=True, name="mm_in_proj", ride=r))

    def f_gk(pgk, gkw, gkb):
        z = _dot(pgk.astype(BF16), gkw.astype(BF16)) + gkb
        return (jnp.minimum(z, 0.0) - jnp.log(1.0 + jnp.exp(-jnp.abs(z)))) / GLA_GATE_NORMALIZER

    (la,) = _rowwise(f_gk, [blk(p, "gk")], [w["gla_gk_w"], w["gla_gk_b"]], [(GLA_HEADS * GLA_DK, F32)],
                     tile=512, name="gla_gate")
    o_gla, states = _gla_fwd(p, la, lay)

    def f_gla_out(ov, pg, g):
        parts = []
        for hh in range(GLA_HEADS):
            oh = ov[:, hh * GLA_DV:(hh + 1) * GLA_DV]
            ph = pg[:, hh * GLA_DV:(hh + 1) * GLA_DV]
            parts.append(oh * _rstd(oh) * g * (ph * _sigmoid(ph)))
        return jnp.concatenate(parts, axis=1)

    (o_n,) = _rowwise(f_gla_out, [full(o_gla), blk(p, "g")], [w["gla_onorm_g"]], [(d, BF16)], tile=256,
                      name="gla_out_norm")
    y_gla = _mm(o_n, w["gla_wo"], out_dtype=BF16, name="mm_gla_wo")

    def f_mla_prep(cq, ckv, kr, tb, gq, gkv):
        return cq * _rstd(cq) * gq, ckv * _rstd(ckv) * gkv, _rope(kr, tb, 1.0)

    cqn, ckvn, krr = _rowwise(f_mla_prep, [blk(p, "cq"), blk(p, "ckv"), blk(p, "kr"), full(tab)],
                              [w["mla_q_norm_g"], w["mla_kv_norm_g"]],
                              [(pw["cq"], BF16), (pw["ckv"], BF16), (LANE, F32)], tile=512, name="mla_prep")
    qlat = _mm(cqn, w["mla_wuq"], out_dtype=BF16, name="mm_mla_wuq")
    kvl = _mm(ckvn, w["mla_wukv"], out_dtype=BF16, name="mm_mla_wukv")
    hv = MLA_HEADS * MLA_V

    def f_qkv(ql, kn, vv, kr, tb):
        qs, ks, vx = [], [], []
        kr1 = kr + _spare_ones(kr.shape, MLA_ROPE)
        ones = _spare_ones(kr.shape, 0)
        for hh in range(MLA_HEADS):
            o0 = hh * MLA_QK_PAD
            qs += [ql[:, o0:o0 + LANE], _rope(ql[:, o0 + LANE:o0 + 2 * LANE], tb, 1.0)]
            ks += [kn[:, hh * LANE:(hh + 1) * LANE], kr1]
            vx += [vv[:, hh * MLA_V:(hh + 1) * MLA_V], ones]
        return (jnp.concatenate(qs, axis=1) * (ATT_SCALE * LOG2E), jnp.concatenate(ks, axis=1),
                jnp.concatenate(vx, axis=1))

    qa, ka, vxa = _rowwise(f_qkv, [full(qlat), (kvl, hv, 0), (kvl, hv, 1), full(krr), full(tab)], [],
                           [(MLA_HEADS * MLA_QK_PAD, BF16), (MLA_HEADS * MLA_QK_PAD, BF16), (2 * hv, BF16)],
                           tile=256, name="mla_qkv_build")
    o_mla, qa_lse = riding("flash_fwd", lambda r: _flash_fwd(qa, ka, vxa, ride=r))
    y_mla = _mm(o_mla, w["mla_wo"], out_dtype=BF16, name="mm_mla_wo")

    def f_merge(yg, ym, ga, gb):
        return _sigmoid(ga) * yg + _sigmoid(gb) * ym

    (merged,) = _rowwise(f_merge, [full(y_gla), full(y_mla), blk(p, "ga"), blk(p, "gb")], [], [(d, BF16)],
                         tile=256, name="merge")
    mix = _mm(merged, w["w_out"], name="mm_w_out")

    def f_res_ln2(xv, mx, mod, g):
        x2v = xv + mod[2:3] * mx
        return x2v, (x2v * _rstd(x2v) * g) * (1.0 + mod[4:5]) + mod[3:4]

    x2, h2 = _rowwise(f_res_ln2, [full(x), full(mix)], [mod8, g2], [(d, F32), (d, BF16)], tile=256,
                      name="res_ln2_modulate")
    gu = _mm(h2, w["ffn_w_in"], out_dtype=BF16, name="mm_ffn_in")

    def f_swiglu(gv, uv):
        return gv * _sigmoid(gv) * uv

    (act,) = _rowwise(f_swiglu, [(gu, ffn, 0), (gu, ffn, 1)], [], [(ffn, BF16)], tile=128, name="swiglu")
    f_out = _mm(act, w["ffn_w_down"], name="mm_ffn_down")

    def f_head(x2v, fv, tg, mod, g):
        x3 = x2v + mod[5:6] * fv
        r = _rstd(x3)
        xh = x3 * r
        e = xh * g - tg
        loss_rows = 0.5 * jnp.mean(e * e, axis=-1, keepdims=True)
        dy = e * (1.0 / d)
        dx3 = _rms_bwd(dy * g, xh, r)
        loss = jnp.broadcast_to(jnp.sum(loss_rows, axis=0, keepdims=True), (1, LANE))
        return (dx3, dx3 * mod[5:6], loss, jnp.sum(dy * xh, axis=0, keepdims=True),
                jnp.sum(dx3 * fv, axis=0, keepdims=True))

    dx3, df, loss_v, dg3, dgate_f = _rowwise(f_head, [full(x2), full(f_out), full(target)], [mod8, g3],
                                             [(d, F32), (d, BF16)], [LANE, d, d], tile=256, name="loss_head")
    da = _mm(df, w["ffn_w_down"], tb=True, out_dtype=BF16, name="mm_ffn_down_dx")
    big["ffn_w_down"] = _mm(act, df, ta=True, out_dtype=BF16, name="mm_ffn_down_dw")

    def f_swiglu_bwd(gv, uv, dav):
        sg = _sigmoid(gv)
        return jnp.concatenate([dav * uv * (sg * (1.0 + gv * (1.0 - sg))), dav * (gv * sg)], axis=1)

    (dgu,) = _rowwise(f_swiglu_bwd, [(gu, ffn, 0), (gu, ffn, 1), full(da)], [], [(2 * ffn, BF16)], tile=128,
                      name="swiglu_bwd")
    dh2 = _mm(dgu, w["ffn_w_in"], tb=True, name="mm_ffn_in_dx")
    big["ffn_w_in"] = _mm(h2, dgu, ta=True, out_dtype=BF16, slabs=True, name="mm_ffn_in_dw")

    def f_ln2_bwd(x2v, dh, dx3v, mx, mod, g):
        r = _rstd(x2v)
        xh = x2v * r
        dn = dh * (1.0 + mod[4:5])
        dx2 = dx3v + _rms_bwd(dn * g, xh, r)
        return (dx2, dx2 * mod[2:3],
                jnp.sum(dh * (xh * g), axis=0, keepdims=True), jnp.sum(dh, axis=0, keepdims=True),
                jnp.sum(dn * xh, axis=0, keepdims=True), jnp.sum(dx2 * mx, axis=0, keepdims=True))

    dx2, dmix, dscale_f, dshift_f, dg2, dgate_m = _rowwise(
        f_ln2_bwd, [full(x2), full(dh2), full(dx3), full(mix)], [mod8, g2], [(d, F32), (d, BF16)],
        [d, d, d, d], tile=256, name="ln2_bwd")
    dmerged = _mm(dmix, w["w_out"], tb=True, out_dtype=BF16, name="mm_w_out_dx")
    big["w_out"] = _mm(merged, dmix, ta=True, out_dtype=BF16, name="mm_w_out_dw")

    def f_merge_bwd(dm, yg, ym, ga, gb):
        sa, sb = _sigmoid(ga), _sigmoid(gb)
        return dm * sa, dm * sb, dm * yg * sa * (1.0 - sa), dm * ym * sb * (1.0 - sb)

    dy_gla, dy_mla, dp_ga, dp_gb = _rowwise(
        f_merge_bwd, [full(dmerged), full(y_gla), full(y_mla), blk(p, "ga"), blk(p, "gb")], [],
        [(d, BF16)] * 4, tile=256, name="merge_bwd")
    do_n = _mm(dy_gla, w["gla_wo"], tb=True, name="mm_gla_wo_dx")
    big["gla_wo"] = _mm(o_n, dy_gla, ta=True, out_dtype=BF16, name="mm_gla_wo_dw")
    do_m = _mm(dy_mla, w["mla_wo"], tb=True, out_dtype=BF16, name="mm_mla_wo_dx")
    big["mla_wo"] = _mm(o_mla, dy_mla, ta=True, out_dtype=BF16, name="mm_mla_wo_dw")

    def f_gla_out_bwd(don, ov, pg, g):
        dos, dpgs = [], []
        dg = jnp.zeros((1, GLA_DV), F32)
        for hh in range(GLA_HEADS):
            sl = slice(hh * GLA_DV, (hh + 1) * GLA_DV)
            oh, ph, dn = ov[:, sl], pg[:, sl], don[:, sl]
            r = _rstd(oh)
            xh = oh * r
            sg = _sigmoid(ph)
            dpre = dn * (ph * sg)
            dg = dg + jnp.sum(dpre * xh, axis=0, keepdims=True)
            dos.append(_rms_bwd(dpre * g, xh, r))
            dpgs.append(dn * (xh * g) * (sg * (1.0 + ph * (1.0 - sg))))
        return jnp.concatenate(dos, axis=1), jnp.concatenate(dpgs, axis=1), dg

    do_gla, dp_g, dg_on = _rowwise(f_gla_out_bwd, [full(do_n), full(o_gla), blk(p, "g")], [w["gla_onorm_g"]],
                                   [(d, F32), (d, BF16)], [GLA_DV], tile=256, name="gla_out_norm_bwd")
    dp_q, dp_k, dp_v, dla = _gla_bwd(p, la, do_gla, states, lay)

    def f_gk_bwd(dlav, pgk, gkw, gkb):
        z = _dot(pgk.astype(BF16), gkw.astype(BF16)) + gkb
        dz = dlav * (1.0 / GLA_GATE_NORMALIZER) * _sigmoid(-z)
        return dz, _dot(dz.astype(BF16), gkw.astype(BF16), 1, 1), jnp.sum(dz, axis=0, keepdims=True)

    dz, dp_gk, dgk_b = _rowwise(f_gk_bwd, [full(dla), blk(p, "gk")], [w["gla_gk_w"], w["gla_gk_b"]],
                                [(GLA_HEADS * GLA_DK, BF16), (LANE, BF16)], [GLA_HEADS * GLA_DK], tile=512,
                                name="gla_gate_bwd")
    p_gk = lax.slice_in_dim(p, lay["gk"], lay["gk"] + LANE, axis=1)
    big["gla_gk_w"] = _mm(p_gk, dz, ta=True, name="mm_gla_gk_dw")[:GLA_GATE_RANK]

    def f_do_aug(dom, om):
        parts = []
        for hh in range(MLA_HEADS):
            dh_ = dom[:, hh * MLA_V:(hh + 1) * MLA_V]
            delta = jnp.sum(dh_ * om[:, hh * MLA_V:(hh + 1) * MLA_V], axis=1, keepdims=True)
            parts += [dh_.astype(BF16), _with_spare(jnp.zeros(dh_.shape, BF16), -delta, 0)]
        return jnp.concatenate(parts, axis=1)

    (doa,) = _rowwise(f_do_aug, [full(do_m), full(o_mla)], [], [(2 * hv, BF16)], tile=256, name="mla_do_delta")
    dqa, dka, dva = riding("flash_bwd", lambda r: _flash_bwd(qa_lse, ka, vxa, doa, ride=r))

    def f_qkv_bwd(dq, dk, dvv, tb):
        dqs, dkn = [], []
        dkr = jnp.zeros((dq.shape[0], LANE), F32)
        for hh in range(MLA_HEADS):
            o0 = hh * MLA_QK_PAD
            dqs += [dq[:, o0:o0 + LANE], _rope(dq[:, o0 + LANE:o0 + 2 * LANE], tb, -1.0)]
            dkn.append(dk[:, o0:o0 + LANE])
            dkr = dkr + dk[:, o0 + LANE:o0 + 2 * LANE]
        return jnp.concatenate(dqs, axis=1), jnp.concatenate(dkn + [dvv], axis=1), dkr

    dqlat, dkvl, dkrr = _rowwise(f_qkv_bwd, [full(dqa), full(dka), full(dva), full(tab)], [],
                                 [(MLA_HEADS * MLA_QK_PAD, BF16), (2 * hv, BF16), (LANE, F32)], tile=256,
                                 name="mla_qkv_build_bwd")
    dcqn = _mm(dqlat, w["mla_wuq"], tb=True, name="mm_mla_wuq_dx")
    big["mla_wuq"] = _mm(cqn, dqlat, ta=True, out_dtype=BF16, name="mm_mla_wuq_dw")
    dckvn = _mm(dkvl, w["mla_wukv"], tb=True, name="mm_mla_wukv_dx")
    big["mla_wukv"] = _mm(ckvn, dkvl, ta=True, out_dtype=BF16, name="mm_mla_wukv_dw")

    def f_mla_prep_bwd(dq, dkv, dkr, cq, ckv, tb, gq, gkv):
        rq, rk = _rstd(cq), _rstd(ckv)
        xq, xk = cq * rq, ckv * rk
        return (_rms_bwd(dq * gq, xq, rq), _rms_bwd(dkv * gkv, xk, rk), _rope(dkr, tb, -1.0),
                jnp.sum(dq * xq, axis=0, keepdims=True), jnp.sum(dkv * xk, axis=0, keepdims=True))

    dp_cq, dp_ckv, dp_kr, dg_q, dg_kv = _rowwise(
        f_mla_prep_bwd, [full(dcqn), full(dckvn), full(dkrr), blk(p, "cq"), blk(p, "ckv"), full(tab)],
        [w["mla_q_norm_g"], w["mla_kv_norm_g"]], [(pw["cq"], BF16), (pw["ckv"], BF16), (LANE, BF16)],
        [pw["cq"], pw["ckv"]], tile=512, name="mla_prep_bwd")

    pieces = dict(v=dp_v, g=dp_g, ga=dp_ga, gb=dp_gb, q=dp_q, k=dp_k, cq=dp_cq, ckv=dp_ckv, gk=dp_gk, kr=dp_kr)
    dp = jnp.concatenate([pieces[n] for n in MY_ORDER], axis=1)
    big["w_in"] = riding("in_proj_dw", lambda r: _mm(dp, h, ta=True, out_dtype=BF16, name="mm_in_proj_dw", ride=r))
    dh = riding("in_proj_dx", lambda r: _mm(dp, w["w_in"], name="mm_in_proj_dx", ride=r))

    def f_ln1_bwd(xv, dhv, dx2v, mod, g):
        r = _rstd(xv)
        xh = xv * r
        dn = dhv * (1.0 + mod[1:2])
        return (dx2v + _rms_bwd(dn * g, xh, r),
                jnp.sum(dhv * (xh * g), axis=0, keepdims=True), jnp.sum(dhv, axis=0, keepdims=True),
                jnp.sum(dn * xh, axis=0, keepdims=True))

    grad_x, dscale_m, dshift_m, dg1 = _rowwise(f_ln1_bwd, [full(x), full(dh), full(dx2)], [mod8, g1],
                                               [(d, F32)], [d, d, d], tile=256, name="ln1_bwd")

    dmod = jnp.concatenate([dshift_m, dscale_m, dgate_m, dshift_f, dscale_f, dgate_f], axis=1)
    small = dict(ada_b=dmod, norm_mix_g=dg1, gla_gk_b=dgk_b, gla_onorm_g=dg_on, mla_q_norm_g=dg_q,
                 mla_kv_norm_g=dg_kv, norm_ffn_g=dg2, final_norm_g=dg3)
    return loss_v[0, 0], grad_x, big, small


N_PEER = N_DEV - 1


def _exchange_copies(ins, outs, sems, scatter):
    send_sems, recv_sems, local_sems = sems
    x, y, c = lax.axis_index("x"), lax.axis_index("y"), lax.axis_index("c")
    me = 4 * x + 2 * y + c
    peers = []
    for rel in range(1, N_DEV):
        px = 1 - x if rel & 4 else x
        py = 1 - y if rel & 2 else y
        pc = 1 - c if rel & 1 else c
        peers.append(((px, py, pc), 4 * px + 2 * py + pc))

    def remote(a, k, src_slot, dst_slot):
        src = ins[a].at[src_slot] if scatter else ins[a]
        return pltpu.make_async_remote_copy(
            src_ref=src, dst_ref=outs[a].at[dst_slot], send_sem=send_sems.at[a * N_PEER + k],
            recv_sem=recv_sems.at[a * N_PEER + k], device_id=peers[k][0], device_id_type=pl.DeviceIdType.MESH)

    local, sends, recvs = [], [], []
    for a in range(len(ins)):
        src = ins[a].at[me] if scatter else ins[a]
        local.append(pltpu.make_async_copy(src, outs[a].at[me], local_sems.at[a]))
        for k in range(N_PEER):
            sends.append(remote(a, k, peers[k][1], me))
            recvs.append(remote(a, k, peers[k][1], peers[k][1]))
    return local, sends, recvs


def _exchange_start(ins, outs, sems, scatter):
    local, sends, _ = _exchange_copies(ins, outs, sems, scatter)
    for cp in local + sends:
        cp.start()


def _exchange_wait(ins, outs, sems, scatter):
    local, sends, recvs = _exchange_copies(ins, outs, sems, scatter)
    for cp in recvs:
        cp.wait_recv()
    for cp in sends:
        cp.wait_send()
    for cp in local:
        cp.wait()


def _exchange_shapes(arrs, scatter):
    n = len(arrs)
    out_shape = [jax.ShapeDtypeStruct(a.shape if scatter else (N_DEV,) + a.shape, a.dtype) for a in arrs]
    sems = [pltpu.SemaphoreType.DMA((n * N_PEER,)), pltpu.SemaphoreType.DMA((n * N_PEER,)),
            pltpu.SemaphoreType.DMA((n,))]
    return out_shape, sems


def _exchange(arrs, *, scatter, name):
    n = len(arrs)

    def body(*refs):
        ins, outs, sems = refs[:n], refs[n:2 * n], refs[2 * n:]
        _exchange_start(ins, outs, sems, scatter)
        _exchange_wait(ins, outs, sems, scatter)

    hbm = pl.BlockSpec(memory_space=pltpu.HBM)
    out_shape, sems = _exchange_shapes(arrs, scatter)
    return pl.pallas_call(body, name=name, in_specs=[hbm] * n, out_specs=[hbm] * n, out_shape=out_shape,
                          scratch_shapes=sems)(*arrs)


def _gather_once_per_chip(arrs, *, name):
    n = len(arrs)

    def body(*refs):
        ins, outs = refs[:n], refs[n:2 * n]
        send_sems, recv_sems, local_sems = refs[2 * n:]
        x, y, c = lax.axis_index("x"), lax.axis_index("y"), lax.axis_index("c")
        sibling = (x, y, 1 - c)
        chips = [(1 - x, y), (x, 1 - y), (1 - x, 1 - y)]

        def slot(px, py, pc):
            return 4 * px + 2 * py + pc

        def copy(a, k, block, to, src=None):
            dst = outs[a].at[slot(*block)]
            return pltpu.make_async_remote_copy(
                src_ref=dst if src is None else src, dst_ref=dst, send_sem=send_sems.at[a * N_PEER + k],
                recv_sem=recv_sems.at[a * N_PEER + k], device_id=to, device_id_type=pl.DeviceIdType.MESH)

        local, sends = [], []
        for a in range(n):
            local.append(pltpu.make_async_copy(ins[a], outs[a].at[slot(x, y, c)], local_sems.at[a]))
            sends.append(copy(a, 0, (x, y, c), sibling, src=ins[a]))
            sends += [copy(a, 1 + j, (x, y, c), (*chip, c), src=ins[a]) for j, chip in enumerate(chips)]
        for cp in local + sends:
            cp.start()
        for a in range(n):
            for j, chip in enumerate(chips):
                copy(a, 1 + j, (*chip, c), (x, y, c)).wait_recv()
                sends.append(copy(a, 4 + j, (*chip, c), sibling))
                sends[-1].start()
        for a in range(n):
            copy(a, 0, sibling, (x, y, c)).wait_recv()
            for j, chip in enumerate(chips):
                copy(a, 4 + j, (*chip, 1 - c), (x, y, c)).wait_recv()
        for cp in sends:
            cp.wait_send()
        for cp in local:
            cp.wait()

    hbm = pl.BlockSpec(memory_space=pltpu.HBM)
    out_shape, sems = _exchange_shapes(arrs, False)
    return pl.pallas_call(body, name=name, in_specs=[hbm] * n, out_specs=[hbm] * n, out_shape=out_shape,
                          scratch_shapes=sems)(*arrs)


def _call(body, *, name, grid, in_specs, out_specs, out_shape, scratch_shapes, sem, args, ride=None,
          vmem_limit=VMEM_LIMIT):
    if ride is None:
        res = pl.pallas_call(body, name=name, grid=grid, in_specs=in_specs, out_specs=out_specs, out_shape=out_shape,
                             scratch_shapes=scratch_shapes, compiler_params=_cparams(sem, vmem_limit))(*args)
        return res, None
    arrs, scatter = ride
    n, n_in, n_out, n_scr = len(arrs), len(in_specs), len(out_specs), len(scratch_shapes)
    x_shape, x_sems = _exchange_shapes(arrs, scatter)

    def hosted(*refs):
        c_in, x_in = refs[:n_in], refs[n_in:n_in + n]
        c_out, x_out = refs[n_in + n:n_in + n + n_out], refs[n_in + n + n_out:n_in + 2 * n + n_out]
        scr = refs[n_in + 2 * n + n_out:]
        c_scr, sems = scr[:n_scr], scr[n_scr:]
        first = functools.reduce(jnp.logical_and, [pl.program_id(a) == 0 for a in range(len(grid))])
        last = functools.reduce(jnp.logical_and, [pl.program_id(a) == grid[a] - 1 for a in range(len(grid))])

        @pl.when(first)
        def _():
            _exchange_start(x_in, x_out, sems, scatter)

        body(*c_in, *c_out, *c_scr)

        @pl.when(last)
        def _():
            _exchange_wait(x_in, x_out, sems, scatter)

    hbm = pl.BlockSpec(memory_space=pltpu.HBM)
    res = pl.pallas_call(
        hosted, name=name, grid=grid, in_specs=list(in_specs) + [hbm] * n, out_specs=list(out_specs) + [hbm] * n,
        out_shape=list(out_shape) + x_shape, scratch_shapes=list(scratch_shapes) + x_sems,
        compiler_params=_cparams(("arbitrary",) * len(grid), vmem_limit))(*args, *arrs)
    return res[:n_out], res[n_out:]


def _adamw_math(w, g, m, v):
    m_new = ADAM_B1 * m + (1.0 - ADAM_B1) * g
    v_new = ADAM_B2 * v + (1.0 - ADAM_B2) * (g * g)
    m_hat = m_new / (1.0 - ADAM_B1 ** ADAM_STEP)
    v_hat = v_new / (1.0 - ADAM_B2 ** ADAM_STEP)
    delta = -ADAM_LR * (m_hat / (jnp.sqrt(v_hat) + ADAM_EPS) + ADAM_WD * w)
    return delta, m_new, v_new


def _adamw_by_columns(w, g, m, v, *, name):
    _, r, c = w.shape
    tc = 256

    def body(w_ref, g_ref, m_ref, v_ref, go_ref, d_ref, mo_ref, vo_ref):
        gv = g_ref[0].astype(F32)
        for s in range(1, N_DEV):
            gv = gv + g_ref[s].astype(F32)
        delta, m_new, v_new = _adamw_math(w_ref[0], gv, m_ref[0], v_ref[0])
        go_ref[0] = gv
        d_ref[0] = delta
        mo_ref[0] = m_new
        vo_ref[0] = v_new

    spec = pl.BlockSpec((1, r, tc), lambda i: (0, 0, i))
    return pl.pallas_call(
        body, name=name, grid=(c // tc,), in_specs=[spec, pl.BlockSpec((N_DEV, r, tc), lambda i: (0, 0, i)), spec, spec],
        out_specs=[spec] * 4, out_shape=[jax.ShapeDtypeStruct((1, r, c), F32)] * 4,
        compiler_params=_cparams(("parallel",)),
    )(w, g, m, v)


def _adamw(w, g, m, v, *, name):
    _, r, c = w.shape
    slots = g.ndim == 3
    if r > 16 and r % 16:
        return _adamw_by_columns(w, g, m, v, name=name)
    tr = r
    for cand in (128, 64, 32, 16):
        if r % cand == 0 and r > cand:
            tr = cand
            break

    def body(w_ref, g_ref, m_ref, v_ref, go_ref, d_ref, mo_ref, vo_ref):
        if slots:
            gv = g_ref[0].astype(F32)
            for s in range(1, N_DEV):
                gv = gv + g_ref[s].astype(F32)
        else:
            gv = g_ref[...]
        delta, m_new, v_new = _adamw_math(w_ref[0], gv, m_ref[0], v_ref[0])
        go_ref[0] = gv
        d_ref[0] = delta
        mo_ref[0] = m_new
        vo_ref[0] = v_new

    spec = pl.BlockSpec((1, tr, c), lambda i: (0, i, 0))
    g_spec = pl.BlockSpec((N_DEV, tr, c), lambda i: (0, i, 0)) if slots else pl.BlockSpec((tr, c), lambda i: (i, 0))
    return pl.pallas_call(
        body, name=name, grid=(r // tr,), in_specs=[spec, g_spec, spec, spec], out_specs=[spec] * 4,
        out_shape=[jax.ShapeDtypeStruct((1, r, c), F32)] * 4, compiler_params=_cparams(("parallel",)),
    )(w, g, m, v)


def _unshard_cols(g):
    return jnp.transpose(g, (1, 0, 2)).reshape(g.shape[1], -1)

def _shard_cols(full):
    r = full.shape[0]
    return jnp.transpose(full.reshape(r, N_DEV, -1), (1, 0, 2))


def _w_in_to_mine(w_ref_layout, d):
    wd, pw, ref_off, _, _ = _in_layout(d)
    cols = []
    for n in MY_ORDER:
        piece = lax.slice_in_dim(w_ref_layout, ref_off[n], ref_off[n] + wd[n], axis=1)
        if pw[n] != wd[n]:
            piece = jnp.pad(piece, ((0, 0), (0, pw[n] - wd[n])))
        cols.append(piece)
    return jnp.concatenate(cols, axis=1)


def _w_in_from_mine(g_mine, d):
    wd, _, _, my_off, _ = _in_layout(d)
    return jnp.concatenate([lax.slice_in_dim(g_mine, my_off[n], my_off[n] + wd[n], axis=1) for n in IN_NAMES],
                           axis=1)


def _w_in_pieces(d):
    wd, _, ref_off, _, total = _in_layout(d)
    shard = sum(wd.values()) // N_DEV
    out = []
    for n in IN_NAMES:
        r0, r1 = ref_off[n], ref_off[n] + wd[n]
        for s in range(N_DEV):
            lo, hi = max(r0, s * shard), min(r1, (s + 1) * shard)
            if lo < hi:
                out.append((n, s, lo - s * shard, lo - r0, hi - lo))
    return out


def _w_in_gathered_to_mine(g, d):
    wd, pw, _, _, _ = _in_layout(d)
    pieces = _w_in_pieces(d)
    rows = []
    for n in MY_ORDER:
        rows += [lax.slice_in_dim(g[s], a, a + wdt, axis=0) for (m, s, a, _, wdt) in pieces if m == n]
        if pw[n] != wd[n]:
            rows.append(jnp.zeros((pw[n] - wd[n], g.shape[2]), g.dtype))
    return jnp.concatenate(rows, axis=0)


def _w_in_mine_to_slabs(g_mine, d):
    _, _, _, my_off, _ = _in_layout(d)
    pieces = _w_in_pieces(d)
    slabs = []
    for s in range(N_DEV):
        parts = [lax.slice_in_dim(g_mine, my_off[m] + b, my_off[m] + b + wdt, axis=0)
                 for (m, s2, _, b, wdt) in pieces if s2 == s]
        slabs.append(jnp.concatenate(parts, axis=0))
    return jnp.stack(slabs)


def _wuq_to_mine(wq):
    r = wq.shape[0]
    w3 = wq.reshape(r, MLA_HEADS, MLA_NOPE + MLA_ROPE)
    w3 = jnp.pad(w3, ((0, 0), (0, 0), (0, MLA_QK_PAD - MLA_NOPE - MLA_ROPE)))
    return w3.reshape(r, MLA_HEADS * MLA_QK_PAD)


def _wuq_from_mine(g):
    r = g.shape[0]
    return g.reshape(r, MLA_HEADS, MLA_QK_PAD)[:, :, :MLA_NOPE + MLA_ROPE].reshape(r, -1)


def _wukv_to_mine(wkv):
    r = wkv.shape[0]
    w3 = wkv.reshape(r, MLA_HEADS, MLA_NOPE + MLA_V)
    return jnp.concatenate([w3[:, :, :MLA_NOPE].reshape(r, -1), w3[:, :, MLA_NOPE:].reshape(r, -1)], axis=1)


def _wukv_from_mine(g):
    r = g.shape[0]
    kn = g[:, :MLA_HEADS * MLA_NOPE].reshape(r, MLA_HEADS, MLA_NOPE)
    vv = g[:, MLA_HEADS * MLA_NOPE:].reshape(r, MLA_HEADS, MLA_V)
    return jnp.concatenate([kn, vv], axis=2).reshape(r, -1)


COL_SHARDED = ("w_in", "gla_gk_w", "mla_wuq", "mla_wukv", "ffn_w_in")


def _gathered_to_mine(name, g, d):
    if name == "w_in":
        return _w_in_gathered_to_mine(g, d)
    full = _unshard_cols(g) if name in COL_SHARDED else g.reshape(-1, g.shape[-1])
    if name == "gla_gk_w":
        return jnp.pad(full, ((0, LANE - GLA_GATE_RANK), (0, 0)))
    if name == "mla_wuq":
        return _wuq_to_mine(full)
    if name == "mla_wukv":
        return _wukv_to_mine(full)
    return full


def _grad_to_slabs(name, g, d):
    if g.ndim == 3:
        return g.astype(BF16)
    if name == "w_in":
        return _w_in_mine_to_slabs(g, d).astype(BF16)
    if name == "mla_wuq":
        g = _wuq_from_mine(g)
    elif name == "mla_wukv":
        g = _wukv_from_mine(g)
    s = _shard_cols(g) if name in COL_SHARDED else g.reshape(N_DEV, -1, g.shape[-1])
    return s.astype(BF16)


class _Rides:
    GATHER = {"in_proj": ("gla_wo", "mla_wuq", "mla_wukv", "mla_wo", "w_out"),
              "flash_fwd": ("ffn_w_in", "ffn_w_down")}
    SCATTER = {"flash_bwd": ("ffn_w_in", "ffn_w_down", "w_out", "gla_wo", "mla_wo", "gla_gk_w"),
               "in_proj_dw": ("mla_wuq", "mla_wukv"),
               "in_proj_dx": ("w_in",)}

    def __init__(self, send, d):
        self.send, self.d, self.recv = send, d, {}

    def ride(self, stage, grads):
        if stage in self.GATHER:
            return [self.send[n] for n in self.GATHER[stage]], False
        return [_grad_to_slabs(n, grads[n], self.d) for n in self.SCATTER[stage]], True

    def done(self, stage, rode, w):
        if stage in self.GATHER:
            for n, g in zip(self.GATHER[stage], rode):
                w[n] = _gathered_to_mine(n, g, self.d)
        else:
            self.recv.update(zip(self.SCATTER[stage], rode))


def kernel(x, c, positions, ada_w, ada_b, norm_mix_g, w_in, gla_gk_w, gla_gk_b, gla_onorm_g, gla_wo, mla_q_norm_g, mla_wuq, mla_kv_norm_g, mla_wukv, mla_wo, w_out, norm_ffn_g, ffn_w_in, ffn_w_down, final_norm_g, loss_target, m_ada_w, m_ada_b, m_norm_mix_g, m_w_in, m_gla_gk_w, m_gla_gk_b, m_gla_onorm_g, m_gla_wo, m_mla_q_norm_g, m_mla_wuq, m_mla_kv_norm_g, m_mla_wukv, m_mla_wo, m_w_out, m_norm_ffn_g, m_ffn_w_in, m_ffn_w_down, m_final_norm_g, v_ada_w, v_ada_b, v_norm_mix_g, v_w_in, v_gla_gk_w, v_gla_gk_b, v_gla_onorm_g, v_gla_wo, v_mla_q_norm_g, v_mla_wuq, v_mla_kv_norm_g, v_mla_wukv, v_mla_wo, v_w_out, v_norm_ffn_g, v_ffn_w_in, v_ffn_w_down, v_final_norm_g):
    wts = dict(ada_w=ada_w, ada_b=ada_b, norm_mix_g=norm_mix_g, w_in=w_in, gla_gk_w=gla_gk_w, gla_gk_b=gla_gk_b,
               gla_onorm_g=gla_onorm_g, gla_wo=gla_wo, mla_q_norm_g=mla_q_norm_g, mla_wuq=mla_wuq,
               mla_kv_norm_g=mla_kv_norm_g, mla_wukv=mla_wukv, mla_wo=mla_wo, w_out=w_out, norm_ffn_g=norm_ffn_g,
               ffn_w_in=ffn_w_in, ffn_w_down=ffn_w_down, final_norm_g=final_norm_g)
    mom_m = dict(zip(WEIGHTS, (m_ada_w, m_ada_b, m_norm_mix_g, m_w_in, m_gla_gk_w, m_gla_gk_b, m_gla_onorm_g,
                               m_gla_wo, m_mla_q_norm_g, m_mla_wuq, m_mla_kv_norm_g, m_mla_wukv, m_mla_wo, m_w_out,
                               m_norm_ffn_g, m_ffn_w_in, m_ffn_w_down, m_final_norm_g)))
    mom_v = dict(zip(WEIGHTS, (v_ada_w, v_ada_b, v_norm_mix_g, v_w_in, v_gla_gk_w, v_gla_gk_b, v_gla_onorm_g,
                               v_gla_wo, v_mla_q_norm_g, v_mla_wuq, v_mla_kv_norm_g, v_mla_wukv, v_mla_wo, v_w_out,
                               v_norm_ffn_g, v_ffn_w_in, v_ffn_w_down, v_final_norm_g)))
    seq, d = x.shape[1], x.shape[2]
    me = 4 * lax.axis_index("x") + 2 * lax.axis_index("y") + lax.axis_index("c")

    def two_d(a):
        return a.reshape(a.shape[-2], a.shape[-1]) if a.ndim >= 2 else a.reshape(1, -1)

    shard = {n: two_d(wts[n]) for n in BIG}
    shard["w_in"] = shard["w_in"].T
    send = {n: shard[n].astype(F32 if n == "gla_gk_w" else BF16) for n in BIG}
    got = _gather_once_per_chip([send["w_in"], send["gla_gk_w"], two_d(c)], name="comm_all_gather_first")
    c_all = got[2].reshape(N_DEV, d)
    w = dict(
        w_in=_gathered_to_mine("w_in", got[0], d), gla_gk_w=_gathered_to_mine("gla_gk_w", got[1], d),
        gla_gk_b=two_d(gla_gk_b), gla_onorm_g=two_d(gla_onorm_g), mla_q_norm_g=two_d(mla_q_norm_g),
        mla_kv_norm_g=two_d(mla_kv_norm_g), norm_mix_g=two_d(norm_mix_g), norm_ffn_g=two_d(norm_ffn_g),
        final_norm_g=two_d(final_norm_g))
    rides = _Rides(send, d)

    c_pad = jnp.pad(c_all, ((0, 16 - N_DEV), (0, 0)))
    (c_act,) = _rowwise(lambda cv: cv * _sigmoid(cv), [(c_pad, d, 0)], [], [(d, F32)], tile=16, name="silu_c")
    ada_w2 = two_d(ada_w)
    mod_part = _mm(c_act, ada_w2, name="mm_ada")[:N_DEV]
    (mod_all,) = _exchange([mod_part], scatter=False, name="comm_all_gather_mod")
    mod_mine = lax.dynamic_index_in_dim(mod_all, me, axis=1, keepdims=False).reshape(1, -1) + two_d(ada_b)
    mod8 = jnp.pad(mod_mine.reshape(6, d), ((0, 2), (0, 0)))

    inv_freq = ROPE_THETA ** (-jnp.arange(0, MLA_ROPE, 2, dtype=F32) / MLA_ROPE)
    ang = positions.reshape(seq, 1).astype(F32) * inv_freq[None, :]
    cos, sin, z32 = jnp.cos(ang), jnp.sin(ang), jnp.zeros((seq, 32), F32)
    tab = jnp.concatenate([cos, cos, z32, z32, -sin, z32, z32, z32, z32, sin, z32, z32], axis=1)
    loss_local, grad_x, _, small = _local_step(x.reshape(seq, d), loss_target.reshape(seq, d), tab, mod8, w, rides)

    recv = rides.recv
    pack = jnp.concatenate([small[n] for n in SMALL], axis=1)
    (pack_all,) = _exchange([pack], scatter=False, name="comm_all_gather_small")
    pack_all = pack_all.reshape(N_DEV, -1)

    res = {}
    for n in BIG:
        if n == "w_in":
            t_res = _adamw(jnp.swapaxes(wts[n], 1, 2), recv[n], jnp.swapaxes(mom_m[n], 1, 2),
                           jnp.swapaxes(mom_v[n], 1, 2), name="adamw_" + n)
            res[n] = tuple(jnp.swapaxes(a, 1, 2) for a in t_res)
        else:
            res[n] = _adamw(wts[n], recv[n], mom_m[n], mom_v[n], name="adamw_" + n)
    n_ada = ada_w2.shape[1]
    dmod_cols = lax.dynamic_slice_in_dim(pack_all[:, :6 * d], me * n_ada, n_ada, axis=1)

    def f_outer(cat, dm):
        acc = cat[:, 0:1] * dm[0:1]
        for b in range(1, N_DEV):
            acc = acc + cat[:, b:b + 1] * dm[b:b + 1]
        return acc

    (g_ada_w,) = _rowwise(f_outer, [(jnp.transpose(c_act[:N_DEV]), N_DEV, 0)], [dmod_cols], [(n_ada, F32)],
                          tile=256, name="ada_w_grad")
    res["ada_w"] = _adamw(ada_w, g_ada_w, m_ada_w, v_ada_w, name="adamw_ada_w")
    w_small = jnp.concatenate([two_d(wts[n]) for n in SMALL], axis=1)[None]
    m_small = jnp.concatenate([two_d(mom_m[n]) for n in SMALL], axis=1)[None]
    v_small = jnp.concatenate([two_d(mom_v[n]) for n in SMALL], axis=1)[None]
    small_res = _adamw(w_small, pack_all.reshape(N_DEV, 1, -1), m_small, v_small, name="adamw_small")
    off = 0
    for n in SMALL:
        width = wts[n].size
        res[n] = tuple(lax.slice_in_dim(a, off, off + width, axis=2) for a in small_res)
        off += width

    loss = lax.psum(loss_local, ("x", "y", "c"))
    outs = [loss, grad_x.reshape(x.shape)]
    for kind in range(4):
        outs += [res[n][kind].reshape(wts[n].shape) for n in WEIGHTS]
    return tuple(outs)
```

```python
import functools

import jax
import jax.numpy as jnp
from jax import lax
from jax.experimental import pallas as pl
from jax.experimental.pallas import tpu as pltpu

F32 = jnp.float32
BF16 = jnp.bfloat16

N_DEV = 8
GLA_HEADS = 4
GLA_DK = 256
GLA_DV = 512
GLA_GATE_RANK = 16
GLA_GATE_NORMALIZER = 16.0
GLA_CHUNK = 64
MLA_HEADS = 16
MLA_NOPE = 128
MLA_ROPE = 64
MLA_V = 128
MLA_QK_PAD = 256
ROPE_THETA = 10000.0
NORM_EPS = 1e-6
ATT_SCALE = (MLA_NOPE + MLA_ROPE) ** -0.5
GLA_QSCALE = GLA_DK ** -0.5

ADAM_LR = 0.001
ADAM_B1 = 0.9
ADAM_B2 = 0.999
ADAM_EPS = 1e-08
ADAM_WD = 0.01
ADAM_STEP = 10

LANE = 128
VMEM_LIMIT = 48 * 1024 * 1024
FLASH_BWD_VMEM = 58 * 1024 * 1024
MM_TILE_BYTES = 6 * 1024 * 1024
LOG2E = 1.4426950408889634
LN2 = 0.6931471805599453
NEG = -1e30

IN_NAMES = ("q", "k", "v", "g", "gk", "cq", "ckv", "kr", "ga", "gb")
MY_ORDER = ("v", "g", "ga", "gb", "q", "k", "cq", "ckv", "gk", "kr")

WEIGHTS = ("ada_w", "ada_b", "norm_mix_g", "w_in", "gla_gk_w", "gla_gk_b", "gla_onorm_g", "gla_wo",
           "mla_q_norm_g", "mla_wuq", "mla_kv_norm_g", "mla_wukv", "mla_wo", "w_out", "norm_ffn_g",
           "ffn_w_in", "ffn_w_down", "final_norm_g")
BIG = ("w_in", "gla_gk_w", "gla_wo", "mla_wuq", "mla_wukv", "mla_wo", "w_out", "ffn_w_in", "ffn_w_down")
SMALL = ("ada_b", "norm_mix_g", "gla_gk_b", "gla_onorm_g", "mla_q_norm_g", "mla_kv_norm_g", "norm_ffn_g",
         "final_norm_g")


def _in_layout(d):
    w = dict(q=d // 2, k=d // 2, v=d, g=d, gk=GLA_GATE_RANK, cq=d // 4, ckv=512, kr=MLA_ROPE, ga=d, gb=d)
    pw = {n: -(-w[n] // LANE) * LANE for n in w}
    ref_off, o = {}, 0
    for n in IN_NAMES:
        ref_off[n] = o
        o += w[n]
    my_off, o = {}, 0
    for n in MY_ORDER:
        assert o % pw[n] == 0
        my_off[n] = o
        o += pw[n]
    return w, pw, ref_off, my_off, o


def _cparams(sem=None, vmem_limit=VMEM_LIMIT):
    return pltpu.CompilerParams(dimension_semantics=sem, vmem_limit_bytes=vmem_limit)


def _dot(a, b, ca=1, cb=0):
    return lax.dot_general(a, b, (((ca,), (cb,)), ((), ())), preferred_element_type=F32)


def _tile(n, cap):
    if n <= cap:
        return n
    t = (cap // LANE) * LANE
    while t >= LANE:
        if n % t == 0:
            return t
        t -= LANE
    return n


def _mm(a, b, *, ta=False, tb=False, out_dtype=F32, name, ride=None, slabs=False):
    m, k = (a.shape[1], a.shape[0]) if ta else a.shape
    n = b.shape[0] if tb else b.shape[1]
    assert k == (b.shape[1] if tb else b.shape[0])
    wide = max(a.dtype.itemsize, b.dtype.itemsize) > 2
    tm, tn, tk = _tile(m, 1024), _tile(n, 1024), _tile(k, MM_TILE_BYTES // (1024 * (4 if wide else 2)))
    if slabs:
        tn = n // N_DEV
        assert tn % LANE == 0
    nk = k // tk

    def product(a_ref, b_ref):
        return _dot(a_ref[...].astype(BF16), b_ref[...].astype(BF16), 0 if ta else 1, 1 if tb else 0)

    def store(o_ref, val):
        if slabs:
            o_ref[0] = val.astype(o_ref.dtype)
        else:
            o_ref[...] = val.astype(o_ref.dtype)

    def body_one(a_ref, b_ref, o_ref):
        store(o_ref, product(a_ref, b_ref))

    def body_acc(a_ref, b_ref, o_ref, acc_ref):
        kk = pl.program_id(2)

        @pl.when(kk == 0)
        def _():
            acc_ref[...] = jnp.zeros_like(acc_ref)

        acc_ref[...] += product(a_ref, b_ref)

        @pl.when(kk == nk - 1)
        def _():
            store(o_ref, acc_ref[...])

    a_spec = (pl.BlockSpec((tk, tm), lambda i, j, kk: (kk, i)) if ta
              else pl.BlockSpec((tm, tk), lambda i, j, kk: (i, kk)))
    b_spec = (pl.BlockSpec((tn, tk), lambda i, j, kk: (j, kk)) if tb
              else pl.BlockSpec((tk, tn), lambda i, j, kk: (kk, j)))
    (out,), rode = _call(
        body_one if nk == 1 else body_acc, name=name, grid=(m // tm, n // tn, nk), in_specs=[a_spec, b_spec],
        out_specs=[pl.BlockSpec((1, tm, tn), lambda i, j, kk: (j, i, 0)) if slabs
                   else pl.BlockSpec((tm, tn), lambda i, j, kk: (i, j))],
        out_shape=[jax.ShapeDtypeStruct((N_DEV, m, tn) if slabs else (m, n), out_dtype)],
        scratch_shapes=[] if nk == 1 else [pltpu.VMEM((tm, tn), F32)],
        sem=("parallel", "parallel", "arbitrary"), args=(a, b), ride=ride)
    return out if ride is None else (out, rode)


def _rowwise(fn, rows, vecs, outs, sums=(), *, tile, name):
    t = rows[0][0].shape[0]
    tile = min(tile, t)
    assert t % tile == 0
    n_rows, n_vecs, n_outs = len(rows), len(vecs), len(outs)

    def body(*refs):
        ins = [r[...].astype(F32) for r in refs[:n_rows + n_vecs]]
        res = fn(*ins)
        if not isinstance(res, (tuple, list)):
            res = (res,)
        out_refs = refs[n_rows + n_vecs:]
        for r, val in zip(out_refs[:n_outs], res[:n_outs]):
            r[...] = val.astype(r.dtype)
        if sums:
            first = pl.program_id(0) == 0
            for r, val in zip(out_refs[n_outs:], res[n_outs:]):
                @pl.when(first)
                def _(r=r):
                    r[...] = jnp.zeros_like(r)
                r[...] += val

    in_specs = [pl.BlockSpec((tile, w), lambda i, cb=cb: (i, cb)) for (_, w, cb) in rows]
    in_specs += [pl.BlockSpec(v.shape, lambda i: (0, 0)) for v in vecs]
    out_specs = [pl.BlockSpec((tile, w), lambda i: (i, 0)) for (w, _) in outs]
    out_specs += [pl.BlockSpec((1, w), lambda i: (0, 0)) for w in sums]
    out_shape = [jax.ShapeDtypeStruct((t, w), dt) for (w, dt) in outs]
    out_shape += [jax.ShapeDtypeStruct((1, w), F32) for w in sums]
    res = pl.pallas_call(
        body, name=name, grid=(t // tile,), in_specs=in_specs, out_specs=out_specs, out_shape=out_shape,
        compiler_params=_cparams(("arbitrary",)),
    )(*[r[0] for r in rows], *vecs)
    return res


def _rstd(x):
    return lax.rsqrt(jnp.mean(x * x, axis=-1, keepdims=True) + NORM_EPS)


def _sigmoid(x):
    return 1.0 / (1.0 + jnp.exp(-x))


def _rms_bwd(dxh, xh, r):
    return r * (dxh - xh * jnp.mean(dxh * xh, axis=-1, keepdims=True))


def _rope(t, tab, sign):
    cosf, sin_a, sin_b = tab[:, :LANE], tab[:, LANE:2 * LANE], tab[:, 2 * LANE:]
    return t * cosf + sign * (pltpu.roll(t, 96, 1) * sin_a + pltpu.roll(t, 32, 1) * sin_b)


def _split3(x):
    hi = x.astype(BF16)
    r1 = x - hi.astype(F32)
    mid = r1.astype(BF16)
    lo = (r1 - mid.astype(F32)).astype(BF16)
    return hi, mid, lo


def _tri_sum(tri_bf16, x):
    hi, mid, lo = _split3(x)
    return _dot(tri_bf16, hi) + _dot(tri_bf16, mid) + _dot(tri_bf16, lo)


def _dot_nt2(a, b):
    a_hi = a.astype(BF16)
    a_lo = (a - a_hi.astype(F32)).astype(BF16)
    b_hi = b.astype(BF16)
    b_lo = (b - b_hi.astype(F32)).astype(BF16)
    return _dot(a_hi, b_hi, 1, 1) + _dot(a_hi, b_lo, 1, 1) + _dot(a_lo, b_hi, 1, 1)


GLA_PAIR = 2


def _gla_specs(t, rows, lay, reverse):
    nb = t // rows
    blk = (lambda i: nb - 1 - i) if reverse else (lambda i: i)
    wk, wv = GLA_PAIR * GLA_DK, GLA_PAIR * GLA_DV
    qb, kb, vb = lay["q"] // wk, lay["k"] // wk, lay["v"] // wv
    return [
        pl.BlockSpec((rows, wk), lambda g, i: (blk(i), qb + g)),
        pl.BlockSpec((rows, wk), lambda g, i: (blk(i), kb + g)),
        pl.BlockSpec((rows, wv), lambda g, i: (blk(i), vb + g)),
        pl.BlockSpec((rows, wk), lambda g, i: (blk(i), g)),
    ], blk


def _gla_fwd(p, la, lay):
    t = p.shape[0]
    rows = min(512, t)
    nb, nc = t // rows, rows // GLA_CHUNK
    c64 = GLA_CHUNK

    def body(q_ref, k_ref, v_ref, la_ref, o_ref, st_ref, s_ref):
        @pl.when(pl.program_id(1) == 0)
        def _():
            s_ref[...] = jnp.zeros_like(s_ref)

        r = lax.broadcasted_iota(jnp.int32, (c64, c64), 0)
        cc = lax.broadcasted_iota(jnp.int32, (c64, c64), 1)
        tril = cc <= r
        tril_b = tril.astype(BF16)
        for c in range(nc):
            sl = pl.ds(c * c64, c64)
            for hh in range(GLA_PAIR):
                lk, lv = pl.ds(hh * GLA_DK, GLA_DK), pl.ds(hh * GLA_DV, GLA_DV)
                b = _tri_sum(tril_b, la_ref[sl, lk])
                b_last = b[c64 - 1:c64, :]
                q = q_ref[sl, lk].astype(F32) * GLA_QSCALE
                k = k_ref[sl, lk].astype(F32)
                v = v_ref[sl, lv].astype(BF16)
                qt_f = q * jnp.exp(b)
                qt = qt_f.astype(BF16)
                kh = (k * jnp.exp(b_last - b)).astype(BF16)
                s_prev = s_ref[hh]
                st_ref[hh, c] = s_prev
                att = jnp.where(tril, _dot_nt2(qt_f, k * jnp.exp(-b)), 0.0)
                o_ref[sl, lv] = _dot(qt, s_prev.astype(BF16), 1, 1) + _dot(att.astype(BF16), v)
                s_ref[hh] = s_prev * jnp.exp(b_last) + _dot(v, kh, 0, 0)

    in_specs, _ = _gla_specs(t, rows, lay, False)
    return pl.pallas_call(
        body, name="gla_fwd", grid=(GLA_HEADS // GLA_PAIR, nb), in_specs=in_specs,
        out_specs=[pl.BlockSpec((rows, GLA_PAIR * GLA_DV), lambda g, i: (i, g)),
                   pl.BlockSpec((GLA_PAIR, nc, GLA_DV, GLA_DK), lambda g, i: (g, i, 0, 0))],
        out_shape=[jax.ShapeDtypeStruct((t, GLA_HEADS * GLA_DV), F32),
                   jax.ShapeDtypeStruct((GLA_HEADS, t // c64, GLA_DV, GLA_DK), F32)],
        scratch_shapes=[pltpu.VMEM((GLA_PAIR, GLA_DV, GLA_DK), F32)],
        compiler_params=_cparams(("parallel", "arbitrary")),
    )(p, p, p, la)


def _gla_bwd(p, la, do, states, lay):
    t = p.shape[0]
    rows = min(512, t)
    nb, nc = t // rows, rows // GLA_CHUNK
    c64 = GLA_CHUNK

    def body(q_ref, k_ref, v_ref, la_ref, do_ref, st_ref, dq_ref, dk_ref, dv_ref, dla_ref, ds_ref):
        @pl.when(pl.program_id(1) == 0)
        def _():
            ds_ref[...] = jnp.zeros_like(ds_ref)

        r = lax.broadcasted_iota(jnp.int32, (c64, c64), 0)
        cc = lax.broadcasted_iota(jnp.int32, (c64, c64), 1)
        tril = cc <= r
        tril_b = tril.astype(BF16)
        triu_b = (cc >= r).astype(BF16)
        for c in reversed(range(nc)):
            sl = pl.ds(c * c64, c64)
            for hh in range(GLA_PAIR):
                lk, lv = pl.ds(hh * GLA_DK, GLA_DK), pl.ds(hh * GLA_DV, GLA_DV)
                b = _tri_sum(tril_b, la_ref[sl, lk])
                b_last = b[c64 - 1:c64, :]
                eb, enb, ebl_b, ebl = jnp.exp(b), jnp.exp(-b), jnp.exp(b_last - b), jnp.exp(b_last)
                k = k_ref[sl, lk].astype(F32)
                qt_f = q_ref[sl, lk].astype(F32) * GLA_QSCALE * eb
                kt_f = k * enb
                kh_f = k * ebl_b
                qt, kt, kh = qt_f.astype(BF16), kt_f.astype(BF16), kh_f.astype(BF16)
                v_f = v_ref[sl, lv].astype(F32)
                dout_f = do_ref[sl, lv]
                v, dout = v_f.astype(BF16), dout_f.astype(BF16)
                s_prev = st_ref[hh, c]
                ds_next = ds_ref[hh]
                ds_next_b = ds_next.astype(BF16)
                att = jnp.where(tril, _dot_nt2(qt_f, kt_f), 0.0).astype(BF16)
                datt = jnp.where(tril, _dot_nt2(dout_f, v_f), 0.0).astype(BF16)
                dqt = _dot(dout, s_prev.astype(BF16)) + _dot(datt, kt)
                dkt = _dot(datt, qt, 0, 0)
                dv = _dot(att, dout, 0, 0) + _dot(kh, ds_next_b, 1, 1)
                dkh = _dot(v, ds_next_b)
                d_ebl = jnp.sum(ds_next * s_prev, axis=0, keepdims=True)
                ds_ref[hh] = ds_next * ebl + _dot(dout, qt, 0, 0)
                db = dqt * qt_f - dkt * kt_f - dkh * kh_f
                db_last = ebl * d_ebl + jnp.sum(dkh * kh_f, axis=0, keepdims=True)
                dq_ref[sl, lk] = (dqt * eb * GLA_QSCALE).astype(dq_ref.dtype)
                dk_ref[sl, lk] = (dkt * enb + dkh * ebl_b).astype(dk_ref.dtype)
                dv_ref[sl, lv] = dv.astype(dv_ref.dtype)
                dla_ref[sl, lk] = _tri_sum(triu_b, db) + db_last

    in_specs, blk = _gla_specs(t, rows, lay, True)
    wk, wv = GLA_PAIR * GLA_DK, GLA_PAIR * GLA_DV
    in_specs += [pl.BlockSpec((rows, wv), lambda g, i: (blk(i), g)),
                 pl.BlockSpec((GLA_PAIR, nc, GLA_DV, GLA_DK), lambda g, i: (g, blk(i), 0, 0))]
    dk_spec = pl.BlockSpec((rows, wk), lambda g, i: (blk(i), g))
    return pl.pallas_call(
        body, name="gla_bwd", grid=(GLA_HEADS // GLA_PAIR, nb), in_specs=in_specs,
        out_specs=[dk_spec, dk_spec, pl.BlockSpec((rows, wv), lambda g, i: (blk(i), g)), dk_spec],
        out_shape=[jax.ShapeDtypeStruct((t, GLA_HEADS * GLA_DK), BF16),
                   jax.ShapeDtypeStruct((t, GLA_HEADS * GLA_DK), BF16),
                   jax.ShapeDtypeStruct((t, GLA_HEADS * GLA_DV), BF16),
                   jax.ShapeDtypeStruct((t, GLA_HEADS * GLA_DK), F32)],
        scratch_shapes=[pltpu.VMEM((GLA_PAIR, GLA_DV, GLA_DK), F32)],
        compiler_params=_cparams(("parallel", "arbitrary")),
    )(p, p, p, la, do, states)


def _diag_mask(rows, cols, row0):
    row = row0 + lax.broadcasted_iota(jnp.int32, (rows, cols), 0)
    col = lax.broadcasted_iota(jnp.int32, (rows, cols), 1)
    return col <= row


QK_SPARE = MLA_NOPE + MLA_ROPE
N_SPARE = 3


def _with_spare(x, col, lane0):
    lane = lax.broadcasted_iota(jnp.int32, x.shape, 1)
    for n, term in enumerate(_split3(col)):
        x = jnp.where(lane == lane0 + n, term, x)
    return x


def _spare_ones(shape, lane0):
    lane = lax.broadcasted_iota(jnp.int32, shape, 1)
    return ((lane >= lane0) & (lane < lane0 + N_SPARE)).astype(F32)


def _flash_tiles(t):
    tq = min(1024, t)
    halves = 2 if tq % 32 == 0 else 1
    return tq, t // tq, halves, tq // halves


def _flash_fwd(q, k, vx, ride=None):
    t = q.shape[0]
    tq, nq, halves, hr = _flash_tiles(t)
    dqk, dv = MLA_QK_PAD, MLA_V

    def body(q_ref, k_ref, v_ref, o_ref, qa_ref, m_ref, acc_ref, s_ref):
        i = pl.program_id(1)
        m_ref[...] = jnp.full_like(m_ref, NEG)
        acc_ref[...] = jnp.zeros_like(acc_ref)

        def scores(j, slot):
            kb = k_ref[pl.ds(pl.multiple_of(j * tq, tq), tq), :]
            for hh in range(halves):
                s_ref[slot, pl.ds(hh * hr, hr), :] = _dot(q_ref[pl.ds(hh * hr, hr), :], kb, 1, 1)

        def consume(j, slot, masked):
            vb = v_ref[pl.ds(pl.multiple_of(j * tq, tq), tq), :]
            for hh in range(halves):
                rs = pl.ds(hh * hr, hr)
                s = s_ref[slot, rs, :]
                if masked:
                    s = jnp.where(_diag_mask(hr, tq, hh * hr), s, NEG)
                m_old = m_ref[rs, :]
                m_new = jnp.maximum(m_old, jnp.max(s, axis=1, keepdims=True))
                pr = jnp.exp2(s - m_new)
                acc_ref[rs, :] = jnp.exp2(m_old - m_new) * acc_ref[rs, :] + _dot(pr.astype(BF16), vb)
                m_ref[rs, :] = m_new

        def two_blocks(jj, carry):
            scores(2 * jj + 1, 1)
            consume(2 * jj, 0, False)
            scores(2 * jj + 2, 0)
            consume(2 * jj + 1, 1, False)
            return carry

        scores(0, 0)
        lax.fori_loop(0, i // 2, two_blocks, 0)

        @pl.when(i % 2 == 0)
        def _():
            consume(i, 0, True)

        @pl.when(i % 2 == 1)
        def _():
            scores(i, 1)
            consume(i - 1, 0, False)
            consume(i, 1, True)

        acc = acc_ref[...]
        l = acc[:, dv:dv + 1]
        o_ref[...] = acc[:, :dv] / l
        qa_ref[...] = _with_spare(q_ref[...], -(m_ref[...] + jnp.log(l) * LOG2E), QK_SPARE)

    outs, rode = _call(
        body, name="mla_flash_fwd", grid=(MLA_HEADS, nq),
        in_specs=[pl.BlockSpec((tq, dqk), lambda h, i: (i, h)),
                  pl.BlockSpec((t, dqk), lambda h, i: (0, h)),
                  pl.BlockSpec((t, 2 * dv), lambda h, i: (0, h))],
        out_specs=[pl.BlockSpec((tq, dv), lambda h, i: (i, h)),
                   pl.BlockSpec((tq, dqk), lambda h, i: (i, h))],
        out_shape=[jax.ShapeDtypeStruct((t, MLA_HEADS * dv), F32),
                   jax.ShapeDtypeStruct((t, MLA_HEADS * dqk), BF16)],
        scratch_shapes=[pltpu.VMEM((tq, 1), F32), pltpu.VMEM((tq, 2 * dv), F32), pltpu.VMEM((2, tq, tq), F32)],
        sem=("parallel", "arbitrary"), args=(q, k, vx), ride=ride)
    return tuple(outs) if ride is None else (tuple(outs), rode)


def _flash_bwd(qa, k, vx, doa, ride=None):
    t = qa.shape[0]
    tq, nq, halves, hr = _flash_tiles(t)
    dqk, dv = MLA_QK_PAD, MLA_V

    def body(k_ref, v_ref, q_ref, do_ref, dq_ref, dk_ref, dv_ref, dq_acc, dk_acc, dv_acc):
        j = pl.program_id(1)

        @pl.when(j == 0)
        def _():
            dq_acc[...] = jnp.zeros_like(dq_acc)

        kb = k_ref[...]
        vb = v_ref[...]
        dk_acc[...] = jnp.zeros_like(dk_acc)
        dv_acc[...] = jnp.zeros_like(dv_acc)

        def step(i, masked):
            for hh in range(halves):
                rs = pl.ds(pl.multiple_of(i * tq + hh * hr, hr), hr)
                qb = q_ref[rs, :]
                dout = do_ref[rs, :]
                nc = (hh + 1) * hr if masked else tq
                s = _dot(qb, kb[:nc], 1, 1)
                if masked:
                    s = jnp.where(_diag_mask(hr, nc, hh * hr), s, NEG)
                pr = jnp.exp2(s)
                ds = (pr * _dot(dout, vb[:nc], 1, 1)).astype(BF16)
                dv_acc[pl.ds(0, nc), :] += _dot(pr.astype(BF16), dout, 0, 0)
                dk_acc[pl.ds(0, nc), :] += _dot(ds, qb, 0, 0)
                dq_acc[rs, :] += _dot(ds, kb[:nc])

        def loop_body(i, carry):
            step(i, False)
            return carry

        step(j, True)
        lax.fori_loop(j + 1, nq, loop_body, 0)
        dk_ref[...] = (dk_acc[...] * LN2).astype(dk_ref.dtype)
        dv_ref[...] = dv_acc[:, :dv].astype(dv_ref.dtype)

        @pl.when(j == nq - 1)
        def _():
            dq_ref[...] = (dq_acc[...] * ATT_SCALE).astype(dq_ref.dtype)

    outs, rode = _call(
        body, name="mla_flash_bwd", grid=(MLA_HEADS, nq),
        in_specs=[pl.BlockSpec((tq, dqk), lambda h, j: (j, h)),
                  pl.BlockSpec((tq, 2 * dv), lambda h, j: (j, h)),
                  pl.BlockSpec((t, dqk), lambda h, j: (0, h)),
                  pl.BlockSpec((t, 2 * dv), lambda h, j: (0, h))],
        out_specs=[pl.BlockSpec((t, dqk), lambda h, j: (0, h)),
                   pl.BlockSpec((tq, dqk), lambda h, j: (j, h)),
                   pl.BlockSpec((tq, dv), lambda h, j: (j, h))],
        out_shape=[jax.ShapeDtypeStruct((t, MLA_HEADS * dqk), BF16),
                   jax.ShapeDtypeStruct((t, MLA_HEADS * dqk), BF16),
                   jax.ShapeDtypeStruct((t, MLA_HEADS * dv), BF16)],
        scratch_shapes=[pltpu.VMEM((t, dqk), F32), pltpu.VMEM((tq, dqk), F32), pltpu.VMEM((tq, 2 * dv), F32)],
        sem=("parallel", "arbitrary"), args=(k, vx, qa, doa), ride=ride, vmem_limit=FLASH_BWD_VMEM)
    return tuple(outs) if ride is None else (tuple(outs), rode)


class _NoRides:
    def ride(self, stage, grads):
        return None

    def done(self, stage, rode, w):
        pass


def _local_step(x, target, tab, mod8, w, rides=None):
    t, d = x.shape
    rides = rides or _NoRides()
    big = {}

    def riding(stage, fn):
        r = rides.ride(stage, big)
        res = fn(r)
        if r is None:
            return res
        rides.done(stage, res[1], w)
        return res[0]
    _, pw, _, lay, _ = _in_layout(d)
    ffn = ((8 * d // 3 + 255) // 256) * 256

    def blk(arr, name):
        return (arr, pw[name], lay[name] // pw[name])

    def full(arr):
        return (arr, arr.shape[1], 0)

    g1, g2, g3 = w["norm_mix_g"], w["norm_ffn_g"], w["final_norm_g"]

    def f_ln1(xv, mod, g):
        return (xv * _rstd(xv) * g) * (1.0 + mod[1:2]) + mod[0:1]

    (h,) = _rowwise(f_ln1, [full(x)], [mod8, g1], [(d, BF16)], tile=256, name="ln1_modulate")
    p = riding("in_proj", lambda r: _mm(h, w["w_in"], name="mm_in_proj", ride=r))

    def f_gk(pgk, gkw, gkb):
        z = _dot(pgk.astype(BF16), gkw.astype(BF16)) + gkb
        return (jnp.minimum(z, 0.0) - jnp.log(1.0 + jnp.exp(-jnp.abs(z)))) / GLA_GATE_NORMALIZER

    (la,) = _rowwise(f_gk, [blk(p, "gk")], [w["gla_gk_w"], w["gla_gk_b"]], [(GLA_HEADS * GLA_DK, F32)],
                     tile=512, name="gla_gate")
    o_gla, states = _gla_fwd(p, la, lay)

    def f_gla_out(ov, pg, g):
        parts = []
        for hh in range(GLA_HEADS):
            oh = ov[:, hh * GLA_DV:(hh + 1) * GLA_DV]
            ph = pg[:, hh * GLA_DV:(hh + 1) * GLA_DV]
            parts.append(oh * _rstd(oh) * g * (ph * _sigmoid(ph)))
        return jnp.concatenate(parts, axis=1)

    (o_n,) = _rowwise(f_gla_out, [full(o_gla), blk(p, "g")], [w["gla_onorm_g"]], [(d, BF16)], tile=256,
                      name="gla_out_norm")
    y_gla = _mm(o_n, w["gla_wo"], out_dtype=BF16, name="mm_gla_wo")

    def f_mla_prep(cq, ckv, kr, tb, gq, gkv):
        return cq * _rstd(cq) * gq, ckv * _rstd(ckv) * gkv, _rope(kr, tb, 1.0)

    cqn, ckvn, krr = _rowwise(f_mla_prep, [blk(p, "cq"), blk(p, "ckv"), blk(p, "kr"), full(tab)],
                              [w["mla_q_norm_g"], w["mla_kv_norm_g"]],
                              [(pw["cq"], BF16), (pw["ckv"], BF16), (LANE, F32)], tile=512, name="mla_prep")
    qlat = _mm(cqn, w["mla_wuq"], out_dtype=BF16, name="mm_mla_wuq")
    kvl = _mm(ckvn, w["mla_wukv"], out_dtype=BF16, name="mm_mla_wukv")
    hv = MLA_HEADS * MLA_V

    def f_qkv(ql, kn, vv, kr, tb):
        qs, ks, vx = [], [], []
        kr1 = kr + _spare_ones(kr.shape, MLA_ROPE)
        ones = _spare_ones(kr.shape, 0)
        for hh in range(MLA_HEADS):
            o0 = hh * MLA_QK_PAD
            qs += [ql[:, o0:o0 + LANE], _rope(ql[:, o0 + LANE:o0 + 2 * LANE], tb, 1.0)]
            ks += [kn[:, hh * LANE:(hh + 1) * LANE], kr1]
            vx += [vv[:, hh * MLA_V:(hh + 1) * MLA_V], ones]
        return (jnp.concatenate(qs, axis=1) * (ATT_SCALE * LOG2E), jnp.concatenate(ks, axis=1),
                jnp.concatenate(vx, axis=1))

    qa, ka, vxa = _rowwise(f_qkv, [full(qlat), (kvl, hv, 0), (kvl, hv, 1), full(krr), full(tab)], [],
                           [(MLA_HEADS * MLA_QK_PAD, BF16), (MLA_HEADS * MLA_QK_PAD, BF16), (2 * hv, BF16)],
                           tile=256, name="mla_qkv_build")
    o_mla, qa_lse = riding("flash_fwd", lambda r: _flash_fwd(qa, ka, vxa, ride=r))
    y_mla = _mm(o_mla, w["mla_wo"], out_dtype=BF16, name="mm_mla_wo")

    def f_merge(yg, ym, ga, gb):
        return _sigmoid(ga) * yg + _sigmoid(gb) * ym

    (merged,) = _rowwise(f_merge, [full(y_gla), full(y_mla), blk(p, "ga"), blk(p, "gb")], [], [(d, BF16)],
                         tile=256, name="merge")
    mix = _mm(merged, w["w_out"], name="mm_w_out")

    def f_res_ln2(xv, mx, mod, g):
        x2v = xv + mod[2:3] * mx
        return x2v, (x2v * _rstd(x2v) * g) * (1.0 + mod[4:5]) + mod[3:4]

    x2, h2 = _rowwise(f_res_ln2, [full(x), full(mix)], [mod8, g2], [(d, F32), (d, BF16)], tile=256,
                      name="res_ln2_modulate")
    gu = _mm(h2, w["ffn_w_in"], out_dtype=BF16, name="mm_ffn_in")

    def f_swiglu(gv, uv):
        return gv * _sigmoid(gv) * uv

    (act,) = _rowwise(f_swiglu, [(gu, ffn, 0), (gu, ffn, 1)], [], [(ffn, BF16)], tile=128, name="swiglu")
    f_out = _mm(act, w["ffn_w_down"], name="mm_ffn_down")

    def f_head(x2v, fv, tg, mod, g):
        x3 = x2v + mod[5:6] * fv
        r = _rstd(x3)
        xh = x3 * r
        e = xh * g - tg
        loss_rows = 0.5 * jnp.mean(e * e, axis=-1, keepdims=True)
        dy = e * (1.0 / d)
        dx3 = _rms_bwd(dy * g, xh, r)
        loss = jnp.broadcast_to(jnp.sum(loss_rows, axis=0, keepdims=True), (1, LANE))
        return (dx3, dx3 * mod[5:6], loss, jnp.sum(dy * xh, axis=0, keepdims=True),
                jnp.sum(dx3 * fv, axis=0, keepdims=True))

    dx3, df, loss_v, dg3, dgate_f = _rowwise(f_head, [full(x2), full(f_out), full(target)], [mod8, g3],
                                             [(d, F32), (d, BF16)], [LANE, d, d], tile=256, name="loss_head")
    da = _mm(df, w["ffn_w_down"], tb=True, out_dtype=BF16, name="mm_ffn_down_dx")
    big["ffn_w_down"] = _mm(act, df, ta=True, out_dtype=BF16, name="mm_ffn_down_dw")

    def f_swiglu_bwd(gv, uv, dav):
        sg = _sigmoid(gv)
        return jnp.concatenate([dav * uv * (sg * (1.0 + gv * (1.0 - sg))), dav * (gv * sg)], axis=1)

    (dgu,) = _rowwise(f_swiglu_bwd, [(gu, ffn, 0), (gu, ffn, 1), full(da)], [], [(2 * ffn, BF16)], tile=128,
                      name="swiglu_bwd")
    dh2 = _mm(dgu, w["ffn_w_in"], tb=True, name="mm_ffn_in_dx")
    big["ffn_w_in"] = _mm(h2, dgu, ta=True, out_dtype=BF16, slabs=True, name="mm_ffn_in_dw")

    def f_ln2_bwd(x2v, dh, dx3v, mx, mod, g):
        r = _rstd(x2v)
        xh = x2v * r
        dn = dh * (1.0 + mod[4:5])
        dx2 = dx3v + _rms_bwd(dn * g, xh, r)
        return (dx2, dx2 * mod[2:3],
                jnp.sum(dh * (xh * g), axis=0, keepdims=True), jnp.sum(dh, axis=0, keepdims=True),
                jnp.sum(dn * xh, axis=0, keepdims=True), jnp.sum(dx2 * mx, axis=0, keepdims=True))

    dx2, dmix, dscale_f, dshift_f, dg2, dgate_m = _rowwise(
        f_ln2_bwd, [full(x2), full(dh2), full(dx3), full(mix)], [mod8, g2], [(d, F32), (d, BF16)],
        [d, d, d, d], tile=256, name="ln2_bwd")
    dmerged = _mm(dmix, w["w_out"], tb=True, out_dtype=BF16, name="mm_w_out_dx")
    big["w_out"] = _mm(merged, dmix, ta=True, out_dtype=BF16, name="mm_w_out_dw")

    def f_merge_bwd(dm, yg, ym, ga, gb):
        sa, sb = _sigmoid(ga), _sigmoid(gb)
        return dm * sa, dm * sb, dm * yg * sa * (1.0 - sa), dm * ym * sb * (1.0 - sb)

    dy_gla, dy_mla, dp_ga, dp_gb = _rowwise(
        f_merge_bwd, [full(dmerged), full(y_gla), full(y_mla), blk(p, "ga"), blk(p, "gb")], [],
        [(d, BF16)] * 4, tile=256, name="merge_bwd")
    do_n = _mm(dy_gla, w["gla_wo"], tb=True, name="mm_gla_wo_dx")
    big["gla_wo"] = _mm(o_n, dy_gla, ta=True, out_dtype=BF16, name="mm_gla_wo_dw")
    do_m = _mm(dy_mla, w["mla_wo"], tb=True, out_dtype=BF16, name="mm_mla_wo_dx")
    big["mla_wo"] = _mm(o_mla, dy_mla, ta=True, out_dtype=BF16, name="mm_mla_wo_dw")

    def f_gla_out_bwd(don, ov, pg, g):
        dos, dpgs = [], []
        dg = jnp.zeros((1, GLA_DV), F32)
        for hh in range(GLA_HEADS):
            sl = slice(hh * GLA_DV, (hh + 1) * GLA_DV)
            oh, ph, dn = ov[:, sl], pg[:, sl], don[:, sl]
            r = _rstd(oh)
            xh = oh * r
            sg = _sigmoid(ph)
            dpre = dn * (ph * sg)
            dg = dg + jnp.sum(dpre * xh, axis=0, keepdims=True)
            dos.append(_rms_bwd(dpre * g, xh, r))
            dpgs.append(dn * (xh * g) * (sg * (1.0 + ph * (1.0 - sg))))
        return jnp.concatenate(dos, axis=1), jnp.concatenate(dpgs, axis=1), dg

    do_gla, dp_g, dg_on = _rowwise(f_gla_out_bwd, [full(do_n), full(o_gla), blk(p, "g")], [w["gla_onorm_g"]],
                                   [(d, F32), (d, BF16)], [GLA_DV], tile=256, name="gla_out_norm_bwd")
    dp_q, dp_k, dp_v, dla = _gla_bwd(p, la, do_gla, states, lay)

    def f_gk_bwd(dlav, pgk, gkw, gkb):
        z = _dot(pgk.astype(BF16), gkw.astype(BF16)) + gkb
        dz = dlav * (1.0 / GLA_GATE_NORMALIZER) * _sigmoid(-z)
        return dz, _dot(dz.astype(BF16), gkw.astype(BF16), 1, 1), jnp.sum(dz, axis=0, keepdims=True)

    dz, dp_gk, dgk_b = _rowwise(f_gk_bwd, [full(dla), blk(p, "gk")], [w["gla_gk_w"], w["gla_gk_b"]],
                                [(GLA_HEADS * GLA_DK, BF16), (LANE, BF16)], [GLA_HEADS * GLA_DK], tile=512,
                                name="gla_gate_bwd")
    p_gk = lax.slice_in_dim(p, lay["gk"], lay["gk"] + LANE, axis=1)
    big["gla_gk_w"] = _mm(p_gk, dz, ta=True, name="mm_gla_gk_dw")[:GLA_GATE_RANK]

    def f_do_aug(dom, om):
        parts = []
        for hh in range(MLA_HEADS):
            dh_ = dom[:, hh * MLA_V:(hh + 1) * MLA_V]
            delta = jnp.sum(dh_ * om[:, hh * MLA_V:(hh + 1) * MLA_V], axis=1, keepdims=True)
            parts += [dh_.astype(BF16), _with_spare(jnp.zeros(dh_.shape, BF16), -delta, 0)]
        return jnp.concatenate(parts, axis=1)

    (doa,) = _rowwise(f_do_aug, [full(do_m), full(o_mla)], [], [(2 * hv, BF16)], tile=256, name="mla_do_delta")
    dqa, dka, dva = riding("flash_bwd", lambda r: _flash_bwd(qa_lse, ka, vxa, doa, ride=r))

    def f_qkv_bwd(dq, dk, dvv, tb):
        dqs, dkn = [], []
        dkr = jnp.zeros((dq.shape[0], LANE), F32)
        for hh in range(MLA_HEADS):
            o0 = hh * MLA_QK_PAD
            dqs += [dq[:, o0:o0 + LANE], _rope(dq[:, o0 + LANE:o0 + 2 * LANE], tb, -1.0)]
            dkn.append(dk[:, o0:o0 + LANE])
            dkr = dkr + dk[:, o0 + LANE:o0 + 2 * LANE]
        return jnp.concatenate(dqs, axis=1), jnp.concatenate(dkn + [dvv], axis=1), dkr

    dqlat, dkvl, dkrr = _rowwise(f_qkv_bwd, [full(dqa), full(dka), full(dva), full(tab)], [],
                                 [(MLA_HEADS * MLA_QK_PAD, BF16), (2 * hv, BF16), (LANE, F32)], tile=256,
                                 name="mla_qkv_build_bwd")
    dcqn = _mm(dqlat, w["mla_wuq"], tb=True, name="mm_mla_wuq_dx")
    big["mla_wuq"] = _mm(cqn, dqlat, ta=True, out_dtype=BF16, name="mm_mla_wuq_dw")
    dckvn = _mm(dkvl, w["mla_wukv"], tb=True, name="mm_mla_wukv_dx")
    big["mla_wukv"] = _mm(ckvn, dkvl, ta=True, out_dtype=BF16, name="mm_mla_wukv_dw")

    def f_mla_prep_bwd(dq, dkv, dkr, cq, ckv, tb, gq, gkv):
        rq, rk = _rstd(cq), _rstd(ckv)
        xq, xk = cq * rq, ckv * rk
        return (_rms_bwd(dq * gq, xq, rq), _rms_bwd(dkv * gkv, xk, rk), _rope(dkr, tb, -1.0),
                jnp.sum(dq * xq, axis=0, keepdims=True), jnp.sum(dkv * xk, axis=0, keepdims=True))

    dp_cq, dp_ckv, dp_kr, dg_q, dg_kv = _rowwise(
        f_mla_prep_bwd, [full(dcqn), full(dckvn), full(dkrr), blk(p, "cq"), blk(p, "ckv"), full(tab)],
        [w["mla_q_norm_g"], w["mla_kv_norm_g"]], [(pw["cq"], BF16), (pw["ckv"], BF16), (LANE, BF16)],
        [pw["cq"], pw["ckv"]], tile=512, name="mla_prep_bwd")

    pieces = dict(v=dp_v, g=dp_g, ga=dp_ga, gb=dp_gb, q=dp_q, k=dp_k, cq=dp_cq, ckv=dp_ckv, gk=dp_gk, kr=dp_kr)
    dp = jnp.concatenate([pieces[n] for n in MY_ORDER], axis=1)
    big["w_in"] = riding("in_proj_dw", lambda r: _mm(dp, h, ta=True, out_dtype=BF16, name="mm_in_proj_dw", ride=r))
    dh = riding("in_proj_dx", lambda r: _mm(dp, w["w_in"], tb=True, name="mm_in_proj_dx", ride=r))

    def f_ln1_bwd(xv, dhv, dx2v, mod, g):
        r = _rstd(xv)
        xh = xv * r
        dn = dhv * (1.0 + mod[1:2])
        return (dx2v + _rms_bwd(dn * g, xh, r),
                jnp.sum(dhv * (xh * g), axis=0, keepdims=True), jnp.sum(dhv, axis=0, keepdims=True),
                jnp.sum(dn * xh, axis=0, keepdims=True))

    grad_x, dscale_m, dshift_m, dg1 = _rowwise(f_ln1_bwd, [full(x), full(dh), full(dx2)], [mod8, g1],
                                               [(d, F32)], [d, d, d], tile=256, name="ln1_bwd")

    dmod = jnp.concatenate([dshift_m, dscale_m, dgate_m, dshift_f, dscale_f, dgate_f], axis=1)
    small = dict(ada_b=dmod, norm_mix_g=dg1, gla_gk_b=dgk_b, gla_onorm_g=dg_on, mla_q_norm_g=dg_q,
                 mla_kv_norm_g=dg_kv, norm_ffn_g=dg2, final_norm_g=dg3)
    return loss_v[0, 0], grad_x, big, small


N_PEER = N_DEV - 1


def _exchange_copies(ins, outs, sems, scatter):
    send_sems, recv_sems, local_sems = sems
    x, y, c = lax.axis_index("x"), lax.axis_index("y"), lax.axis_index("c")
    me = 4 * x + 2 * y + c
    peers = []
    for rel in range(1, N_DEV):
        px = 1 - x if rel & 4 else x
        py = 1 - y if rel & 2 else y
        pc = 1 - c if rel & 1 else c
        peers.append(((px, py, pc), 4 * px + 2 * py + pc))

    def remote(a, k, src_slot, dst_slot):
        src = ins[a].at[src_slot] if scatter else ins[a]
        return pltpu.make_async_remote_copy(
            src_ref=src, dst_ref=outs[a].at[dst_slot], send_sem=send_sems.at[a * N_PEER + k],
            recv_sem=recv_sems.at[a * N_PEER + k], device_id=peers[k][0], device_id_type=pl.DeviceIdType.MESH)

    local, sends, recvs = [], [], []
    for a in range(len(ins)):
        src = ins[a].at[me] if scatter else ins[a]
        local.append(pltpu.make_async_copy(src, outs[a].at[me], local_sems.at[a]))
        for k in range(N_PEER):
            sends.append(remote(a, k, peers[k][1], me))
            recvs.append(remote(a, k, peers[k][1], peers[k][1]))
    return local, sends, recvs


def _exchange_start(ins, outs, sems, scatter):
    local, sends, _ = _exchange_copies(ins, outs, sems, scatter)
    for cp in local + sends:
        cp.start()


def _exchange_wait(ins, outs, sems, scatter):
    local, sends, recvs = _exchange_copies(ins, outs, sems, scatter)
    for cp in recvs:
        cp.wait_recv()
    for cp in sends:
        cp.wait_send()
    for cp in local:
        cp.wait()


def _exchange_shapes(arrs, scatter):
    n = len(arrs)
    out_shape = [jax.ShapeDtypeStruct(a.shape if scatter else (N_DEV,) + a.shape, a.dtype) for a in arrs]
    sems = [pltpu.SemaphoreType.DMA((n * N_PEER,)), pltpu.SemaphoreType.DMA((n * N_PEER,)),
            pltpu.SemaphoreType.DMA((n,))]
    return out_shape, sems


def _exchange(arrs, *, scatter, name):
    n = len(arrs)

    def body(*refs):
        ins, outs, sems = refs[:n], refs[n:2 * n], refs[2 * n:]
        _exchange_start(ins, outs, sems, scatter)
        _exchange_wait(ins, outs, sems, scatter)

    hbm = pl.BlockSpec(memory_space=pltpu.HBM)
    out_shape, sems = _exchange_shapes(arrs, scatter)
    return pl.pallas_call(body, name=name, in_specs=[hbm] * n, out_specs=[hbm] * n, out_shape=out_shape,
                          scratch_shapes=sems)(*arrs)


def _gather_once_per_chip(arrs, *, name):
    n = len(arrs)

    def body(*refs):
        ins, outs = refs[:n], refs[n:2 * n]
        send_sems, recv_sems, local_sems = refs[2 * n:]
        x, y, c = lax.axis_index("x"), lax.axis_index("y"), lax.axis_index("c")
        sibling = (x, y, 1 - c)
        chips = [(1 - x, y), (x, 1 - y), (1 - x, 1 - y)]

        def slot(px, py, pc):
            return 4 * px + 2 * py + pc

        def copy(a, k, block, to, src=None):
            dst = outs[a].at[slot(*block)]
            return pltpu.make_async_remote_copy(
                src_ref=dst if src is None else src, dst_ref=dst, send_sem=send_sems.at[a * N_PEER + k],
                recv_sem=recv_sems.at[a * N_PEER + k], device_id=to, device_id_type=pl.DeviceIdType.MESH)

        local, sends = [], []
        for a in range(n):
            local.append(pltpu.make_async_copy(ins[a], outs[a].at[slot(x, y, c)], local_sems.at[a]))
            sends.append(copy(a, 0, (x, y, c), sibling, src=ins[a]))
            sends += [copy(a, 1 + j, (x, y, c), (*chip, c), src=ins[a]) for j, chip in enumerate(chips)]
        for cp in local + sends:
            cp.start()
        for a in range(n):
            for j, chip in enumerate(chips):
                copy(a, 1 + j, (*chip, c), (x, y, c)).wait_recv()
                sends.append(copy(a, 4 + j, (*chip, c), sibling))
                sends[-1].start()
        for a in range(n):
            copy(a, 0, sibling, (x, y, c)).wait_recv()
            for j, chip in enumerate(chips):
                copy(a, 4 + j, (*chip, 1 - c), (x, y, c)).wait_recv()
        for cp in sends:
            cp.wait_send()
        for cp in local:
            cp.wait()

    hbm = pl.BlockSpec(memory_space=pltpu.HBM)
    out_shape, sems = _exchange_shapes(arrs, False)
    return pl.pallas_call(body, name=name, in_specs=[hbm] * n, out_specs=[hbm] * n, out_shape=out_shape,
                          scratch_shapes=sems)(*arrs)


def _call(body, *, name, grid, in_specs, out_specs, out_shape, scratch_shapes, sem, args, ride=None,
          vmem_limit=VMEM_LIMIT):
    if ride is None:
        res = pl.pallas_call(body, name=name, grid=grid, in_specs=in_specs, out_specs=out_specs, out_shape=out_shape,
                             scratch_shapes=scratch_shapes, compiler_params=_cparams(sem, vmem_limit))(*args)
        return res, None
    arrs, scatter = ride
    n, n_in, n_out, n_scr = len(arrs), len(in_specs), len(out_specs), len(scratch_shapes)
    x_shape, x_sems = _exchange_shapes(arrs, scatter)

    def hosted(*refs):
        c_in, x_in = refs[:n_in], refs[n_in:n_in + n]
        c_out, x_out = refs[n_in + n:n_in + n + n_out], refs[n_in + n + n_out:n_in + 2 * n + n_out]
        scr = refs[n_in + 2 * n + n_out:]
        c_scr, sems = scr[:n_scr], scr[n_scr:]
        first = functools.reduce(jnp.logical_and, [pl.program_id(a) == 0 for a in range(len(grid))])
        last = functools.reduce(jnp.logical_and, [pl.program_id(a) == grid[a] - 1 for a in range(len(grid))])

        @pl.when(first)
        def _():
            _exchange_start(x_in, x_out, sems, scatter)

        body(*c_in, *c_out, *c_scr)

        @pl.when(last)
        def _():
            _exchange_wait(x_in, x_out, sems, scatter)

    hbm = pl.BlockSpec(memory_space=pltpu.HBM)
    res = pl.pallas_call(
        hosted, name=name, grid=grid, in_specs=list(in_specs) + [hbm] * n, out_specs=list(out_specs) + [hbm] * n,
        out_shape=list(out_shape) + x_shape, scratch_shapes=list(scratch_shapes) + x_sems,
        compiler_params=_cparams(("arbitrary",) * len(grid), vmem_limit))(*args, *arrs)
    return res[:n_out], res[n_out:]


def _adamw_math(w, g, m, v):
    m_new = ADAM_B1 * m + (1.0 - ADAM_B1) * g
    v_new = ADAM_B2 * v + (1.0 - ADAM_B2) * (g * g)
    m_hat = m_new / (1.0 - ADAM_B1 ** ADAM_STEP)
    v_hat = v_new / (1.0 - ADAM_B2 ** ADAM_STEP)
    delta = -ADAM_LR * (m_hat / (jnp.sqrt(v_hat) + ADAM_EPS) + ADAM_WD * w)
    return delta, m_new, v_new


def _adamw_by_columns(w, g, m, v, *, name):
    _, r, c = w.shape
    tc = 256

    def body(w_ref, g_ref, m_ref, v_ref, go_ref, d_ref, mo_ref, vo_ref):
        gv = g_ref[0].astype(F32)
        for s in range(1, N_DEV):
            gv = gv + g_ref[s].astype(F32)
        delta, m_new, v_new = _adamw_math(w_ref[0], gv, m_ref[0], v_ref[0])
        go_ref[0] = gv
        d_ref[0] = delta
        mo_ref[0] = m_new
        vo_ref[0] = v_new

    spec = pl.BlockSpec((1, r, tc), lambda i: (0, 0, i))
    return pl.pallas_call(
        body, name=name, grid=(c // tc,), in_specs=[spec, pl.BlockSpec((N_DEV, r, tc), lambda i: (0, 0, i)), spec, spec],
        out_specs=[spec] * 4, out_shape=[jax.ShapeDtypeStruct((1, r, c), F32)] * 4,
        compiler_params=_cparams(("parallel",)),
    )(w, g, m, v)


def _adamw(w, g, m, v, *, name):
    _, r, c = w.shape
    slots = g.ndim == 3
    if r > 16 and r % 16:
        return _adamw_by_columns(w, g, m, v, name=name)
    tr = r
    for cand in (128, 64, 32, 16):
        if r % cand == 0 and r > cand:
            tr = cand
            break

    def body(w_ref, g_ref, m_ref, v_ref, go_ref, d_ref, mo_ref, vo_ref):
        if slots:
            gv = g_ref[0].astype(F32)
            for s in range(1, N_DEV):
                gv = gv + g_ref[s].astype(F32)
        else:
            gv = g_ref[...]
        delta, m_new, v_new = _adamw_math(w_ref[0], gv, m_ref[0], v_ref[0])
        go_ref[0] = gv
        d_ref[0] = delta
        mo_ref[0] = m_new
        vo_ref[0] = v_new

    spec = pl.BlockSpec((1, tr, c), lambda i: (0, i, 0))
    g_spec = pl.BlockSpec((N_DEV, tr, c), lambda i: (0, i, 0)) if slots else pl.BlockSpec((tr, c), lambda i: (i, 0))
    return pl.pallas_call(
        body, name=name, grid=(r // tr,), in_specs=[spec, g_spec, spec, spec], out_specs=[spec] * 4,
        out_shape=[jax.ShapeDtypeStruct((1, r, c), F32)] * 4, compiler_params=_cparams(("parallel",)),
    )(w, g, m, v)


def _unshard_cols(g):
    return jnp.transpose(g, (1, 0, 2)).reshape(g.shape[1], -1)

def _shard_cols(full):
    r = full.shape[0]
    return jnp.transpose(full.reshape(r, N_DEV, -1), (1, 0, 2))


def _w_in_to_mine(w_ref_layout, d):
    wd, pw, ref_off, _, _ = _in_layout(d)
    cols = []
    for n in MY_ORDER:
        piece = lax.slice_in_dim(w_ref_layout, ref_off[n], ref_off[n] + wd[n], axis=1)
        if pw[n] != wd[n]:
            piece = jnp.pad(piece, ((0, 0), (0, pw[n] - wd[n])))
        cols.append(piece)
    return jnp.concatenate(cols, axis=1)


def _w_in_from_mine(g_mine, d):
    wd, _, _, my_off, _ = _in_layout(d)
    return jnp.concatenate([lax.slice_in_dim(g_mine, my_off[n], my_off[n] + wd[n], axis=1) for n in IN_NAMES],
                           axis=1)


def _w_in_pieces(d):
    wd, _, ref_off, _, total = _in_layout(d)
    shard = sum(wd.values()) // N_DEV
    out = []
    for n in IN_NAMES:
        r0, r1 = ref_off[n], ref_off[n] + wd[n]
        for s in range(N_DEV):
            lo, hi = max(r0, s * shard), min(r1, (s + 1) * shard)
            if lo < hi:
                out.append((n, s, lo - s * shard, lo - r0, hi - lo))
    return out


def _w_in_gathered_to_mine(g, d):
    wd, pw, _, _, _ = _in_layout(d)
    pieces = _w_in_pieces(d)
    cols = []
    for n in MY_ORDER:
        cols += [lax.slice_in_dim(g[s], a, a + wdt, axis=1) for (m, s, a, _, wdt) in pieces if m == n]
        if pw[n] != wd[n]:
            cols.append(jnp.zeros((g.shape[1], pw[n] - wd[n]), g.dtype))
    return jnp.concatenate(cols, axis=1)


def _w_in_mine_to_slabs(g_mine, d):
    _, _, _, my_off, _ = _in_layout(d)
    pieces = _w_in_pieces(d)
    slabs = []
    for s in range(N_DEV):
        parts = [lax.slice_in_dim(g_mine, my_off[m] + b, my_off[m] + b + wdt, axis=0)
                 for (m, s2, _, b, wdt) in pieces if s2 == s]
        slabs.append(jnp.concatenate(parts, axis=0))
    return jnp.stack(slabs)


def _wuq_to_mine(wq):
    r = wq.shape[0]
    w3 = wq.reshape(r, MLA_HEADS, MLA_NOPE + MLA_ROPE)
    w3 = jnp.pad(w3, ((0, 0), (0, 0), (0, MLA_QK_PAD - MLA_NOPE - MLA_ROPE)))
    return w3.reshape(r, MLA_HEADS * MLA_QK_PAD)


def _wuq_from_mine(g):
    r = g.shape[0]
    return g.reshape(r, MLA_HEADS, MLA_QK_PAD)[:, :, :MLA_NOPE + MLA_ROPE].reshape(r, -1)


def _wukv_to_mine(wkv):
    r = wkv.shape[0]
    w3 = wkv.reshape(r, MLA_HEADS, MLA_NOPE + MLA_V)
    return jnp.concatenate([w3[:, :, :MLA_NOPE].reshape(r, -1), w3[:, :, MLA_NOPE:].reshape(r, -1)], axis=1)


def _wukv_from_mine(g):
    r = g.shape[0]
    kn = g[:, :MLA_HEADS * MLA_NOPE].reshape(r, MLA_HEADS, MLA_NOPE)
    vv = g[:, MLA_HEADS * MLA_NOPE:].reshape(r, MLA_HEADS, MLA_V)
    return jnp.concatenate([kn, vv], axis=2).reshape(r, -1)


COL_SHARDED = ("w_in", "gla_gk_w", "mla_wuq", "mla_wukv", "ffn_w_in")


def _gathered_to_mine(name, g, d):
    if name == "w_in":
        return _w_in_gathered_to_mine(g, d)
    full = _unshard_cols(g) if name in COL_SHARDED else g.reshape(-1, g.shape[-1])
    if name == "gla_gk_w":
        return jnp.pad(full, ((0, LANE - GLA_GATE_RANK), (0, 0)))
    if name == "mla_wuq":
        return _wuq_to_mine(full)
    if name == "mla_wukv":
        return _wukv_to_mine(full)
    return full


def _grad_to_slabs(name, g, d):
    if g.ndim == 3:
        return g.astype(BF16)
    if name == "w_in":
        return _w_in_mine_to_slabs(g, d).astype(BF16)
    if name == "mla_wuq":
        g = _wuq_from_mine(g)
    elif name == "mla_wukv":
        g = _wukv_from_mine(g)
    s = _shard_cols(g) if name in COL_SHARDED else g.reshape(N_DEV, -1, g.shape[-1])
    return s.astype(BF16)


class _Rides:
    GATHER = {"in_proj": ("gla_wo", "mla_wuq", "mla_wukv", "mla_wo", "w_out"),
              "flash_fwd": ("ffn_w_in", "ffn_w_down")}
    SCATTER = {"flash_bwd": ("ffn_w_in", "ffn_w_down", "w_out", "gla_wo", "mla_wo", "gla_gk_w"),
               "in_proj_dw": ("mla_wuq", "mla_wukv"),
               "in_proj_dx": ("w_in",)}

    def __init__(self, send, d):
        self.send, self.d, self.recv = send, d, {}

    def ride(self, stage, grads):
        if stage in self.GATHER:
            return [self.send[n] for n in self.GATHER[stage]], False
        return [_grad_to_slabs(n, grads[n], self.d) for n in self.SCATTER[stage]], True

    def done(self, stage, rode, w):
        if stage in self.GATHER:
            for n, g in zip(self.GATHER[stage], rode):
                w[n] = _gathered_to_mine(n, g, self.d)
        else:
            self.recv.update(zip(self.SCATTER[stage], rode))


def kernel(x, c, positions, ada_w, ada_b, norm_mix_g, w_in, gla_gk_w, gla_gk_b, gla_onorm_g, gla_wo, mla_q_norm_g, mla_wuq, mla_kv_norm_g, mla_wukv, mla_wo, w_out, norm_ffn_g, ffn_w_in, ffn_w_down, final_norm_g, loss_target, m_ada_w, m_ada_b, m_norm_mix_g, m_w_in, m_gla_gk_w, m_gla_gk_b, m_gla_onorm_g, m_gla_wo, m_mla_q_norm_g, m_mla_wuq, m_mla_kv_norm_g, m_mla_wukv, m_mla_wo, m_w_out, m_norm_ffn_g, m_ffn_w_in, m_ffn_w_down, m_final_norm_g, v_ada_w, v_ada_b, v_norm_mix_g, v_w_in, v_gla_gk_w, v_gla_gk_b, v_gla_onorm_g, v_gla_wo, v_mla_q_norm_g, v_mla_wuq, v_mla_kv_norm_g, v_mla_wukv, v_mla_wo, v_w_out, v_norm_ffn_g, v_ffn_w_in, v_ffn_w_down, v_final_norm_g):
    wts = dict(ada_w=ada_w, ada_b=ada_b, norm_mix_g=norm_mix_g, w_in=w_in, gla_gk_w=gla_gk_w, gla_gk_b=gla_gk_b,
               gla_onorm_g=gla_onorm_g, gla_wo=gla_wo, mla_q_norm_g=mla_q_norm_g, mla_wuq=mla_wuq,
               mla_kv_norm_g=mla_kv_norm_g, mla_wukv=mla_wukv, mla_wo=mla_wo, w_out=w_out, norm_ffn_g=norm_ffn_g,
               ffn_w_in=ffn_w_in, ffn_w_down=ffn_w_down, final_norm_g=final_norm_g)
    mom_m = dict(zip(WEIGHTS, (m_ada_w, m_ada_b, m_norm_mix_g, m_w_in, m_gla_gk_w, m_gla_gk_b, m_gla_onorm_g,
                               m_gla_wo, m_mla_q_norm_g, m_mla_wuq, m_mla_kv_norm_g, m_mla_wukv, m_mla_wo, m_w_out,
                               m_norm_ffn_g, m_ffn_w_in, m_ffn_w_down, m_final_norm_g)))
    mom_v = dict(zip(WEIGHTS, (v_ada_w, v_ada_b, v_norm_mix_g, v_w_in, v_gla_gk_w, v_gla_gk_b, v_gla_onorm_g,
                               v_gla_wo, v_mla_q_norm_g, v_mla_wuq, v_mla_kv_norm_g, v_mla_wukv, v_mla_wo, v_w_out,
                               v_norm_ffn_g, v_ffn_w_in, v_ffn_w_down, v_final_norm_g)))
    seq, d = x.shape[1], x.shape[2]
    me = 4 * lax.axis_index("x") + 2 * lax.axis_index("y") + lax.axis_index("c")

    def two_d(a):
        return a.reshape(a.shape[-2], a.shape[-1]) if a.ndim >= 2 else a.reshape(1, -1)

    shard = {n: two_d(wts[n]) for n in BIG}
    send = {n: shard[n].astype(F32 if n == "gla_gk_w" else BF16) for n in BIG}
    got = _gather_once_per_chip([send["w_in"], send["gla_gk_w"], two_d(c)], name="comm_all_gather_first")
    c_all = got[2].reshape(N_DEV, d)
    w = dict(
        w_in=_gathered_to_mine("w_in", got[0], d), gla_gk_w=_gathered_to_mine("gla_gk_w", got[1], d),
        gla_gk_b=two_d(gla_gk_b), gla_onorm_g=two_d(gla_onorm_g), mla_q_norm_g=two_d(mla_q_norm_g),
        mla_kv_norm_g=two_d(mla_kv_norm_g), norm_mix_g=two_d(norm_mix_g), norm_ffn_g=two_d(norm_ffn_g),
        final_norm_g=two_d(final_norm_g))
    rides = _Rides(send, d)

    c_pad = jnp.pad(c_all, ((0, 16 - N_DEV), (0, 0)))
    (c_act,) = _rowwise(lambda cv: cv * _sigmoid(cv), [(c_pad, d, 0)], [], [(d, F32)], tile=16, name="silu_c")
    ada_w2 = two_d(ada_w)
    mod_part = _mm(c_act, ada_w2, name="mm_ada")[:N_DEV]
    (mod_all,) = _exchange([mod_part], scatter=False, name="comm_all_gather_mod")
    mod_mine = lax.dynamic_index_in_dim(mod_all, me, axis=1, keepdims=False).reshape(1, -1) + two_d(ada_b)
    mod8 = jnp.pad(mod_mine.reshape(6, d), ((0, 2), (0, 0)))

    inv_freq = ROPE_THETA ** (-jnp.arange(0, MLA_ROPE, 2, dtype=F32) / MLA_ROPE)
    ang = positions.reshape(seq, 1).astype(F32) * inv_freq[None, :]
    cos, sin, z32 = jnp.cos(ang), jnp.sin(ang), jnp.zeros((seq, 32), F32)
    tab = jnp.concatenate([cos, cos, z32, z32, -sin, z32, z32, z32, z32, sin, z32, z32], axis=1)
    loss_local, grad_x, _, small = _local_step(x.reshape(seq, d), loss_target.reshape(seq, d), tab, mod8, w, rides)

    recv = rides.recv
    pack = jnp.concatenate([small[n] for n in SMALL], axis=1)
    (pack_all,) = _exchange([pack], scatter=False, name="comm_all_gather_small")
    pack_all = pack_all.reshape(N_DEV, -1)

    res = {}
    for n in BIG:
        if n == "w_in":
            t_res = _adamw(jnp.swapaxes(wts[n], 1, 2), recv[n], jnp.swapaxes(mom_m[n], 1, 2),
                           jnp.swapaxes(mom_v[n], 1, 2), name="adamw_" + n)
            res[n] = tuple(jnp.swapaxes(a, 1, 2) for a in t_res)
        else:
            res[n] = _adamw(wts[n], recv[n], mom_m[n], mom_v[n], name="adamw_" + n)
    n_ada = ada_w2.shape[1]
    dmod_cols = lax.dynamic_slice_in_dim(pack_all[:, :6 * d], me * n_ada, n_ada, axis=1)

    def f_outer(cat, dm):
        acc = cat[:, 0:1] * dm[0:1]
        for b in range(1, N_DEV):
            acc = acc + cat[:, b:b + 1] * dm[b:b + 1]
        return acc

    (g_ada_w,) = _rowwise(f_outer, [(jnp.transpose(c_act[:N_DEV]), N_DEV, 0)], [dmod_cols], [(n_ada, F32)],
                          tile=256, name="ada_w_grad")
    res["ada_w"] = _adamw(ada_w, g_ada_w, m_ada_w, v_ada_w, name="adamw_ada_w")
    w_small = jnp.concatenate([two_d(wts[n]) for n in SMALL], axis=1)[None]
    m_small = jnp.concatenate([two_d(mom_m[n]) for n in SMALL], axis=1)[None]
    v_small = jnp.concatenate([two_d(mom_v[n]) for n in SMALL], axis=1)[None]
    small_res = _adamw(w_small, pack_all.reshape(N_DEV, 1, -1), m_small, v_small, name="adamw_small")
    off = 0
    for n in SMALL:
        width = wts[n].size
        res[n] = tuple(lax.slice_in_dim(a, off, off + width, axis=2) for a in small_res)
        off += width

    loss = lax.psum(loss_local, ("x", "y", "c"))
    outs = [loss, grad_x.reshape(x.shape)]
    for kind in range(4):
        outs += [res[n][kind].reshape(wts[n].shape) for n in WEIGHTS]
    return tuple(outs)
```

```python
import functools

import jax
import jax.numpy as jnp
from jax import lax
from jax.experimental import pallas as pl
from jax.experimental.pallas import tpu as pltpu

F32 = jnp.float32
BF16 = jnp.bfloat16

N_DEV = 8
GLA_HEADS = 4
GLA_DK = 256
GLA_DV = 512
GLA_GATE_RANK = 16
GLA_GATE_NORMALIZER = 16.0
GLA_CHUNK = 64
MLA_HEADS = 16
MLA_NOPE = 128
MLA_ROPE = 64
MLA_V = 128
MLA_QK_PAD = 256
ROPE_THETA = 10000.0
NORM_EPS = 1e-6
ATT_SCALE = (MLA_NOPE + MLA_ROPE) ** -0.5
GLA_QSCALE = GLA_DK ** -0.5

ADAM_LR = 0.001
ADAM_B1 = 0.9
ADAM_B2 = 0.999
ADAM_EPS = 1e-08
ADAM_WD = 0.01
ADAM_STEP = 10

LANE = 128
VMEM_LIMIT = 48 * 1024 * 1024
FLASH_BWD_VMEM = 58 * 1024 * 1024
MM_TILE_BYTES = 6 * 1024 * 1024
LOG2E = 1.4426950408889634
LN2 = 0.6931471805599453
NEG = -1e30

IN_NAMES = ("q", "k", "v", "g", "gk", "cq", "ckv", "kr", "ga", "gb")
MY_ORDER = ("v", "g", "ga", "gb", "q", "k", "cq", "ckv", "gk", "kr")

WEIGHTS = ("ada_w", "ada_b", "norm_mix_g", "w_in", "gla_gk_w", "gla_gk_b", "gla_onorm_g", "gla_wo",
           "mla_q_norm_g", "mla_wuq", "mla_kv_norm_g", "mla_wukv", "mla_wo", "w_out", "norm_ffn_g",
           "ffn_w_in", "ffn_w_down", "final_norm_g")
BIG = ("w_in", "gla_gk_w", "gla_wo", "mla_wuq", "mla_wukv", "mla_wo", "w_out", "ffn_w_in", "ffn_w_down")
SMALL = ("ada_b", "norm_mix_g", "gla_gk_b", "gla_onorm_g", "mla_q_norm_g", "mla_kv_norm_g", "norm_ffn_g",
         "final_norm_g")


def _in_layout(d):
    w = dict(q=d // 2, k=d // 2, v=d, g=d, gk=GLA_GATE_RANK, cq=d // 4, ckv=512, kr=MLA_ROPE, ga=d, gb=d)
    pw = {n: -(-w[n] // LANE) * LANE for n in w}
    ref_off, o = {}, 0
    for n in IN_NAMES:
        ref_off[n] = o
        o += w[n]
    my_off, o = {}, 0
    for n in MY_ORDER:
        assert o % pw[n] == 0
        my_off[n] = o
        o += pw[n]
    return w, pw, ref_off, my_off, o


def _cparams(sem=None, vmem_limit=VMEM_LIMIT):
    return pltpu.CompilerParams(dimension_semantics=sem, vmem_limit_bytes=vmem_limit)


def _dot(a, b, ca=1, cb=0):
    return lax.dot_general(a, b, (((ca,), (cb,)), ((), ())), preferred_element_type=F32)


def _tile(n, cap):
    if n <= cap:
        return n
    t = (cap // LANE) * LANE
    while t >= LANE:
        if n % t == 0:
            return t
        t -= LANE
    return n


def _mm(a, b, *, ta=False, tb=False, out_dtype=F32, name, ride=None, slabs=False):
    m, k = (a.shape[1], a.shape[0]) if ta else a.shape
    n = b.shape[0] if tb else b.shape[1]
    assert k == (b.shape[1] if tb else b.shape[0])
    wide = max(a.dtype.itemsize, b.dtype.itemsize) > 2
    tm, tn, tk = _tile(m, 1024), _tile(n, 1024), _tile(k, MM_TILE_BYTES // (1024 * (4 if wide else 2)))
    if slabs:
        tn = n // N_DEV
        assert tn % LANE == 0
    nk = k // tk

    def product(a_ref, b_ref):
        return _dot(a_ref[...].astype(BF16), b_ref[...].astype(BF16), 0 if ta else 1, 1 if tb else 0)

    def store(o_ref, val):
        if slabs:
            o_ref[0] = val.astype(o_ref.dtype)
        else:
            o_ref[...] = val.astype(o_ref.dtype)

    def body_one(a_ref, b_ref, o_ref):
        store(o_ref, product(a_ref, b_ref))

    def body_acc(a_ref, b_ref, o_ref, acc_ref):
        kk = pl.program_id(2)

        @pl.when(kk == 0)
        def _():
            acc_ref[...] = jnp.zeros_like(acc_ref)

        acc_ref[...] += product(a_ref, b_ref)

        @pl.when(kk == nk - 1)
        def _():
            store(o_ref, acc_ref[...])

    a_spec = (pl.BlockSpec((tk, tm), lambda i, j, kk: (kk, i)) if ta
              else pl.BlockSpec((tm, tk), lambda i, j, kk: (i, kk)))
    b_spec = (pl.BlockSpec((tn, tk), lambda i, j, kk: (j, kk)) if tb
              else pl.BlockSpec((tk, tn), lambda i, j, kk: (kk, j)))
    (out,), rode = _call(
        body_one if nk == 1 else body_acc, name=name, grid=(m // tm, n // tn, nk), in_specs=[a_spec, b_spec],
        out_specs=[pl.BlockSpec((1, tm, tn), lambda i, j, kk: (j, i, 0)) if slabs
                   else pl.BlockSpec((tm, tn), lambda i, j, kk: (i, j))],
        out_shape=[jax.ShapeDtypeStruct((N_DEV, m, tn) if slabs else (m, n), out_dtype)],
        scratch_shapes=[] if nk == 1 else [pltpu.VMEM((tm, tn), F32)],
        sem=("parallel", "parallel", "arbitrary"), args=(a, b), ride=ride)
    return out if ride is None else (out, rode)


def _rowwise(fn, rows, vecs, outs, sums=(), *, tile, name):
    t = rows[0][0].shape[0]
    tile = min(tile, t)
    assert t % tile == 0
    n_rows, n_vecs, n_outs = len(rows), len(vecs), len(outs)

    def body(*refs):
        ins = [r[...].astype(F32) for r in refs[:n_rows + n_vecs]]
        res = fn(*ins)
        if not isinstance(res, (tuple, list)):
            res = (res,)
        out_refs = refs[n_rows + n_vecs:]
        for r, val in zip(out_refs[:n_outs], res[:n_outs]):
            r[...] = val.astype(r.dtype)
        if sums:
            first = pl.program_id(0) == 0
            for r, val in zip(out_refs[n_outs:], res[n_outs:]):
                @pl.when(first)
                def _(r=r):
                    r[...] = jnp.zeros_like(r)
                r[...] += val

    in_specs = [pl.BlockSpec((tile, w), lambda i, cb=cb: (i, cb)) for (_, w, cb) in rows]
    in_specs += [pl.BlockSpec(v.shape, lambda i: (0, 0)) for v in vecs]
    out_specs = [pl.BlockSpec((tile, w), lambda i: (i, 0)) for (w, _) in outs]
    out_specs += [pl.BlockSpec((1, w), lambda i: (0, 0)) for w in sums]
    out_shape = [jax.ShapeDtypeStruct((t, w), dt) for (w, dt) in outs]
    out_shape += [jax.ShapeDtypeStruct((1, w), F32) for w in sums]
    res = pl.pallas_call(
        body, name=name, grid=(t // tile,), in_specs=in_specs, out_specs=out_specs, out_shape=out_shape,
        compiler_params=_cparams(("arbitrary",)),
    )(*[r[0] for r in rows], *vecs)
    return res


def _rstd(x):
    return lax.rsqrt(jnp.mean(x * x, axis=-1, keepdims=True) + NORM_EPS)


def _sigmoid(x):
    return 1.0 / (1.0 + jnp.exp(-x))


def _rms_bwd(dxh, xh, r):
    return r * (dxh - xh * jnp.mean(dxh * xh, axis=-1, keepdims=True))


def _rope(t, tab, sign):
    cosf, sin_a, sin_b = tab[:, :LANE], tab[:, LANE:2 * LANE], tab[:, 2 * LANE:]
    return t * cosf + sign * (pltpu.roll(t, 96, 1) * sin_a + pltpu.roll(t, 32, 1) * sin_b)


def _split3(x):
    hi = x.astype(BF16)
    r1 = x - hi.astype(F32)
    mid = r1.astype(BF16)
    lo = (r1 - mid.astype(F32)).astype(BF16)
    return hi, mid, lo


def _tri_sum(tri_bf16, x):
    hi, mid, lo = _split3(x)
    return _dot(tri_bf16, hi) + _dot(tri_bf16, mid) + _dot(tri_bf16, lo)


def _dot_nt2(a, b):
    a_hi = a.astype(BF16)
    a_lo = (a - a_hi.astype(F32)).astype(BF16)
    b_hi = b.astype(BF16)
    b_lo = (b - b_hi.astype(F32)).astype(BF16)
    return _dot(a_hi, b_hi, 1, 1) + _dot(a_hi, b_lo, 1, 1) + _dot(a_lo, b_hi, 1, 1)


GLA_PAIR = 2


def _gla_specs(t, rows, lay, reverse):
    nb = t // rows
    blk = (lambda i: nb - 1 - i) if reverse else (lambda i: i)
    wk, wv = GLA_PAIR * GLA_DK, GLA_PAIR * GLA_DV
    qb, kb, vb = lay["q"] // wk, lay["k"] // wk, lay["v"] // wv
    return [
        pl.BlockSpec((rows, wk), lambda g, i: (blk(i), qb + g)),
        pl.BlockSpec((rows, wk), lambda g, i: (blk(i), kb + g)),
        pl.BlockSpec((rows, wv), lambda g, i: (blk(i), vb + g)),
        pl.BlockSpec((rows, wk), lambda g, i: (blk(i), g)),
    ], blk


def _gla_fwd(p, la, lay):
    t = p.shape[0]
    rows = min(512, t)
    nb, nc = t // rows, rows // GLA_CHUNK
    c64 = GLA_CHUNK

    def body(q_ref, k_ref, v_ref, la_ref, o_ref, st_ref, s_ref):
        @pl.when(pl.program_id(1) == 0)
        def _():
            s_ref[...] = jnp.zeros_like(s_ref)

        r = lax.broadcasted_iota(jnp.int32, (c64, c64), 0)
        cc = lax.broadcasted_iota(jnp.int32, (c64, c64), 1)
        tril = cc <= r
        tril_b = tril.astype(BF16)
        state = [s_ref[hh] for hh in range(GLA_PAIR)]
        for c in range(nc):
            sl = pl.ds(c * c64, c64)
            for hh in range(GLA_PAIR):
                lk, lv = pl.ds(hh * GLA_DK, GLA_DK), pl.ds(hh * GLA_DV, GLA_DV)
                b = _tri_sum(tril_b, la_ref[sl, lk])
                b_last = b[c64 - 1:c64, :]
                q = q_ref[sl, lk].astype(F32) * GLA_QSCALE
                k = k_ref[sl, lk].astype(F32)
                v = v_ref[sl, lv].astype(BF16)
                qt_f = q * jnp.exp(b)
                qt = qt_f.astype(BF16)
                kh = (k * jnp.exp(b_last - b)).astype(BF16)
                s_prev = state[hh]
                st_ref[hh, c] = s_prev
                att = jnp.where(tril, _dot_nt2(qt_f, k * jnp.exp(-b)), 0.0)
                o_ref[sl, lv] = _dot(qt, s_prev.astype(BF16), 1, 1) + _dot(att.astype(BF16), v)
                state[hh] = s_prev * jnp.exp(b_last) + _dot(v, kh, 0, 0)
        for hh in range(GLA_PAIR):
            s_ref[hh] = state[hh]

    in_specs, _ = _gla_specs(t, rows, lay, False)
    return pl.pallas_call(
        body, name="gla_fwd", grid=(GLA_HEADS // GLA_PAIR, nb), in_specs=in_specs,
        out_specs=[pl.BlockSpec((rows, GLA_PAIR * GLA_DV), lambda g, i: (i, g)),
                   pl.BlockSpec((GLA_PAIR, nc, GLA_DV, GLA_DK), lambda g, i: (g, i, 0, 0))],
        out_shape=[jax.ShapeDtypeStruct((t, GLA_HEADS * GLA_DV), F32),
                   jax.ShapeDtypeStruct((GLA_HEADS, t // c64, GLA_DV, GLA_DK), F32)],
        scratch_shapes=[pltpu.VMEM((GLA_PAIR, GLA_DV, GLA_DK), F32)],
        compiler_params=_cparams(("parallel", "arbitrary")),
    )(p, p, p, la)


def _gla_bwd(p, la, do, states, lay):
    t = p.shape[0]
    rows = min(512, t)
    nb, nc = t // rows, rows // GLA_CHUNK
    c64 = GLA_CHUNK

    def body(q_ref, k_ref, v_ref, la_ref, do_ref, st_ref, dq_ref, dk_ref, dv_ref, dla_ref, ds_ref):
        @pl.when(pl.program_id(1) == 0)
        def _():
            ds_ref[...] = jnp.zeros_like(ds_ref)

        r = lax.broadcasted_iota(jnp.int32, (c64, c64), 0)
        cc = lax.broadcasted_iota(jnp.int32, (c64, c64), 1)
        tril = cc <= r
        tril_b = tril.astype(BF16)
        triu_b = (cc >= r).astype(BF16)
        dstate = [ds_ref[hh] for hh in range(GLA_PAIR)]
        for c in reversed(range(nc)):
            sl = pl.ds(c * c64, c64)
            for hh in range(GLA_PAIR):
                lk, lv = pl.ds(hh * GLA_DK, GLA_DK), pl.ds(hh * GLA_DV, GLA_DV)
                b = _tri_sum(tril_b, la_ref[sl, lk])
                b_last = b[c64 - 1:c64, :]
                eb, enb, ebl_b, ebl = jnp.exp(b), jnp.exp(-b), jnp.exp(b_last - b), jnp.exp(b_last)
                k = k_ref[sl, lk].astype(F32)
                qt_f = q_ref[sl, lk].astype(F32) * GLA_QSCALE * eb
                kt_f = k * enb
                kh_f = k * ebl_b
                qt, kt, kh = qt_f.astype(BF16), kt_f.astype(BF16), kh_f.astype(BF16)
                v_f = v_ref[sl, lv].astype(F32)
                dout_f = do_ref[sl, lv]
                v, dout = v_f.astype(BF16), dout_f.astype(BF16)
                s_prev = st_ref[hh, c]
                ds_next = dstate[hh]
                ds_next_b = ds_next.astype(BF16)
                att = jnp.where(tril, _dot_nt2(qt_f, kt_f), 0.0).astype(BF16)
                datt = jnp.where(tril, _dot_nt2(dout_f, v_f), 0.0).astype(BF16)
                dqt = _dot(dout, s_prev.astype(BF16)) + _dot(datt, kt)
                dkt = _dot(datt, qt, 0, 0)
                dv = _dot(att, dout, 0, 0) + _dot(kh, ds_next_b, 1, 1)
                dkh = _dot(v, ds_next_b)
                d_ebl = jnp.sum(ds_next * s_prev, axis=0, keepdims=True)
                dstate[hh] = ds_next * ebl + _dot(dout, qt, 0, 0)
                db = dqt * qt_f - dkt * kt_f - dkh * kh_f
                db_last = ebl * d_ebl + jnp.sum(dkh * kh_f, axis=0, keepdims=True)
                dq_ref[sl, lk] = (dqt * eb * GLA_QSCALE).astype(dq_ref.dtype)
                dk_ref[sl, lk] = (dkt * enb + dkh * ebl_b).astype(dk_ref.dtype)
                dv_ref[sl, lv] = dv.astype(dv_ref.dtype)
                dla_ref[sl, lk] = _tri_sum(triu_b, db) + db_last
        for hh in range(GLA_PAIR):
            ds_ref[hh] = dstate[hh]

    in_specs, blk = _gla_specs(t, rows, lay, True)
    wk, wv = GLA_PAIR * GLA_DK, GLA_PAIR * GLA_DV
    in_specs += [pl.BlockSpec((rows, wv), lambda g, i: (blk(i), g)),
                 pl.BlockSpec((GLA_PAIR, nc, GLA_DV, GLA_DK), lambda g, i: (g, blk(i), 0, 0))]
    dk_spec = pl.BlockSpec((rows, wk), lambda g, i: (blk(i), g))
    return pl.pallas_call(
        body, name="gla_bwd", grid=(GLA_HEADS // GLA_PAIR, nb), in_specs=in_specs,
        out_specs=[dk_spec, dk_spec, pl.BlockSpec((rows, wv), lambda g, i: (blk(i), g)), dk_spec],
        out_shape=[jax.ShapeDtypeStruct((t, GLA_HEADS * GLA_DK), BF16),
                   jax.ShapeDtypeStruct((t, GLA_HEADS * GLA_DK), BF16),
                   jax.ShapeDtypeStruct((t, GLA_HEADS * GLA_DV), BF16),
                   jax.ShapeDtypeStruct((t, GLA_HEADS * GLA_DK), F32)],
        scratch_shapes=[pltpu.VMEM((GLA_PAIR, GLA_DV, GLA_DK), F32)],
        compiler_params=_cparams(("parallel", "arbitrary")),
    )(p, p, p, la, do, states)


def _diag_mask(rows, cols, row0):
    row = row0 + lax.broadcasted_iota(jnp.int32, (rows, cols), 0)
    col = lax.broadcasted_iota(jnp.int32, (rows, cols), 1)
    return col <= row


QK_SPARE = MLA_NOPE + MLA_ROPE
N_SPARE = 3


def _with_spare(x, col, lane0):
    lane = lax.broadcasted_iota(jnp.int32, x.shape, 1)
    for n, term in enumerate(_split3(col)):
        x = jnp.where(lane == lane0 + n, term, x)
    return x


def _spare_ones(shape, lane0):
    lane = lax.broadcasted_iota(jnp.int32, shape, 1)
    return ((lane >= lane0) & (lane < lane0 + N_SPARE)).astype(F32)


def _flash_tiles(t):
    tq = min(1024, t)
    halves = 2 if tq % 32 == 0 else 1
    return tq, t // tq, halves, tq // halves


def _flash_fwd(q, k, vx, ride=None):
    t = q.shape[0]
    tq, nq, halves, hr = _flash_tiles(t)
    dqk, dv = MLA_QK_PAD, MLA_V

    def body(q_ref, k_ref, v_ref, o_ref, qa_ref, m_ref, acc_ref, s_ref):
        i = pl.program_id(1)
        m_ref[...] = jnp.full_like(m_ref, NEG)
        acc_ref[...] = jnp.zeros_like(acc_ref)

        def scores(j, slot):
            kb = k_ref[pl.ds(pl.multiple_of(j * tq, tq), tq), :]
            for hh in range(halves):
                s_ref[slot, pl.ds(hh * hr, hr), :] = _dot(q_ref[pl.ds(hh * hr, hr), :], kb, 1, 1)

        def consume(j, slot, masked):
            vb = v_ref[pl.ds(pl.multiple_of(j * tq, tq), tq), :]
            for hh in range(halves):
                rs = pl.ds(hh * hr, hr)
                s = s_ref[slot, rs, :]
                if masked:
                    s = jnp.where(_diag_mask(hr, tq, hh * hr), s, NEG)
                m_old = m_ref[rs, :]
                m_new = jnp.maximum(m_old, jnp.max(s, axis=1, keepdims=True))
                pr = jnp.exp2(s - m_new)
                acc_ref[rs, :] = jnp.exp2(m_old - m_new) * acc_ref[rs, :] + _dot(pr.astype(BF16), vb)
                m_ref[rs, :] = m_new

        def two_blocks(jj, carry):
            scores(2 * jj + 1, 1)
            consume(2 * jj, 0, False)
            scores(2 * jj + 2, 0)
            consume(2 * jj + 1, 1, False)
            return carry

        scores(0, 0)
        lax.fori_loop(0, i // 2, two_blocks, 0)

        @pl.when(i % 2 == 0)
        def _():
            consume(i, 0, True)

        @pl.when(i % 2 == 1)
        def _():
            scores(i, 1)
            consume(i - 1, 0, False)
            consume(i, 1, True)

        acc = acc_ref[...]
        l = acc[:, dv:dv + 1]
        o_ref[...] = acc[:, :dv] / l
        qa_ref[...] = _with_spare(q_ref[...], -(m_ref[...] + jnp.log(l) * LOG2E), QK_SPARE)

    outs, rode = _call(
        body, name="mla_flash_fwd", grid=(MLA_HEADS, nq),
        in_specs=[pl.BlockSpec((tq, dqk), lambda h, i: (i, h)),
                  pl.BlockSpec((t, dqk), lambda h, i: (0, h)),
                  pl.BlockSpec((t, 2 * dv), lambda h, i: (0, h))],
        out_specs=[pl.BlockSpec((tq, dv), lambda h, i: (i, h)),
                   pl.BlockSpec((tq, dqk), lambda h, i: (i, h))],
        out_shape=[jax.ShapeDtypeStruct((t, MLA_HEADS * dv), F32),
                   jax.ShapeDtypeStruct((t, MLA_HEADS * dqk), BF16)],
        scratch_shapes=[pltpu.VMEM((tq, 1), F32), pltpu.VMEM((tq, 2 * dv), F32), pltpu.VMEM((2, tq, tq), F32)],
        sem=("parallel", "arbitrary"), args=(q, k, vx), ride=ride)
    return tuple(outs) if ride is None else (tuple(outs), rode)


def _flash_bwd(qa, k, vx, doa, ride=None):
    t = qa.shape[0]
    tq, nq, halves, hr = _flash_tiles(t)
    dqk, dv = MLA_QK_PAD, MLA_V

    def body(k_ref, v_ref, q_ref, do_ref, dq_ref, dk_ref, dv_ref, dq_acc, dk_acc, dv_acc):
        j = pl.program_id(1)

        @pl.when(j == 0)
        def _():
            dq_acc[...] = jnp.zeros_like(dq_acc)

        kb = k_ref[...]
        vb = v_ref[...]
        dk_acc[...] = jnp.zeros_like(dk_acc)
        dv_acc[...] = jnp.zeros_like(dv_acc)

        def step(i, masked):
            for hh in range(halves):
                rs = pl.ds(pl.multiple_of(i * tq + hh * hr, hr), hr)
                qb = q_ref[rs, :]
                dout = do_ref[rs, :]
                nc = (hh + 1) * hr if masked else tq
                s = _dot(qb, kb[:nc], 1, 1)
                if masked:
                    s = jnp.where(_diag_mask(hr, nc, hh * hr), s, NEG)
                pr = jnp.exp2(s)
                ds = (pr * _dot(dout, vb[:nc], 1, 1)).astype(BF16)
                dv_acc[pl.ds(0, nc), :] += _dot(pr.astype(BF16), dout, 0, 0)
                dk_acc[pl.ds(0, nc), :] += _dot(ds, qb, 0, 0)
                dq_acc[rs, :] += _dot(ds, kb[:nc])

        def loop_body(i, carry):
            step(i, False)
            return carry

        step(j, True)
        lax.fori_loop(j + 1, nq, loop_body, 0)
        dk_ref[...] = (dk_acc[...] * LN2).astype(dk_ref.dtype)
        dv_ref[...] = dv_acc[:, :dv].astype(dv_ref.dtype)

        @pl.when(j == nq - 1)
        def _():
            dq_ref[...] = (dq_acc[...] * ATT_SCALE).astype(dq_ref.dtype)

    outs, rode = _call(
        body, name="mla_flash_bwd", grid=(MLA_HEADS, nq),
        in_specs=[pl.BlockSpec((tq, dqk), lambda h, j: (j, h)),
                  pl.BlockSpec((tq, 2 * dv), lambda h, j: (j, h)),
                  pl.BlockSpec((t, dqk), lambda h, j: (0, h)),
                  pl.BlockSpec((t, 2 * dv), lambda h, j: (0, h))],
        out_specs=[pl.BlockSpec((t, dqk), lambda h, j: (0, h)),
                   pl.BlockSpec((tq, dqk), lambda h, j: (j, h)),
                   pl.BlockSpec((tq, dv), lambda h, j: (j, h))],
        out_shape=[jax.ShapeDtypeStruct((t, MLA_HEADS * dqk), BF16),
                   jax.ShapeDtypeStruct((t, MLA_HEADS * dqk), BF16),
                   jax.ShapeDtypeStruct((t, MLA_HEADS * dv), BF16)],
        scratch_shapes=[pltpu.VMEM((t, dqk), F32), pltpu.VMEM((tq, dqk), F32), pltpu.VMEM((tq, 2 * dv), F32)],
        sem=("parallel", "arbitrary"), args=(k, vx, qa, doa), ride=ride, vmem_limit=FLASH_BWD_VMEM)
    return tuple(outs) if ride is None else (tuple(outs), rode)


class _NoRides:
    def ride(self, stage, grads):
        return None

    def done(self, stage, rode, w):
        pass


def _local_step(x, target, tab, mod8, w, rides=None):
    t, d = x.shape
    rides = rides or _NoRides()
    big = {}

    def riding(stage, fn):
        r = rides.ride(stage, big)
        res = fn(r)
        if r is None:
            return res
        rides.done(stage, res[1], w)
        return res[0]
    _, pw, _, lay, _ = _in_layout(d)
    ffn = ((8 * d // 3 + 255) // 256) * 256

    def blk(arr, name):
        return (arr, pw[name], lay[name] // pw[name])

    def full(arr):
        return (arr, arr.shape[1], 0)

    g1, g2, g3 = w["norm_mix_g"], w["norm_ffn_g"], w["final_norm_g"]

    def f_ln1(xv, mod, g):
        return (xv * _rstd(xv) * g) * (1.0 + mod[1:2]) + mod[0:1]

    (h,) = _rowwise(f_ln1, [full(x)], [mod8, g1], [(d, BF16)], tile=256, name="ln1_modulate")
    p = riding("in_proj", lambda r: _mm(h, w["w_in"], name="mm_in_proj", ride=r))

    def f_gk(pgk, gkw, gkb):
        z = _dot(pgk.astype(BF16), gkw.astype(BF16)) + gkb
        return (jnp.minimum(z, 0.0) - jnp.log(1.0 + jnp.exp(-jnp.abs(z)))) / GLA_GATE_NORMALIZER

    (la,) = _rowwise(f_gk, [blk(p, "gk")], [w["gla_gk_w"], w["gla_gk_b"]], [(GLA_HEADS * GLA_DK, F32)],
                     tile=512, name="gla_gate")
    o_gla, states = _gla_fwd(p, la, lay)

    def f_gla_out(ov, pg, g):
        parts = []
        for hh in range(GLA_HEADS):
            oh = ov[:, hh * GLA_DV:(hh + 1) * GLA_DV]
            ph = pg[:, hh * GLA_DV:(hh + 1) * GLA_DV]
            parts.append(oh * _rstd(oh) * g * (ph * _sigmoid(ph)))
        return jnp.concatenate(parts, axis=1)

    (o_n,) = _rowwise(f_gla_out, [full(o_gla), blk(p, "g")], [w["gla_onorm_g"]], [(d, BF16)], tile=256,
                      name="gla_out_norm")
    y_gla = _mm(o_n, w["gla_wo"], out_dtype=BF16, name="mm_gla_wo")

    def f_mla_prep(cq, ckv, kr, tb, gq, gkv):
        return cq * _rstd(cq) * gq, ckv * _rstd(ckv) * gkv, _rope(kr, tb, 1.0)

    cqn, ckvn, krr = _rowwise(f_mla_prep, [blk(p, "cq"), blk(p, "ckv"), blk(p, "kr"), full(tab)],
                              [w["mla_q_norm_g"], w["mla_kv_norm_g"]],
                              [(pw["cq"], BF16), (pw["ckv"], BF16), (LANE, F32)], tile=512, name="mla_prep")
    qlat = _mm(cqn, w["mla_wuq"], out_dtype=BF16, name="mm_mla_wuq")
    kvl = _mm(ckvn, w["mla_wukv"], out_dtype=BF16, name="mm_mla_wukv")
    hv = MLA_HEADS * MLA_V

    def f_qkv(ql, kn, vv, kr, tb):
        qs, ks, vx = [], [], []
        kr1 = kr + _spare_ones(kr.shape, MLA_ROPE)
        ones = _spare_ones(kr.shape, 0)
        for hh in range(MLA_HEADS):
            o0 = hh * MLA_QK_PAD
            qs += [ql[:, o0:o0 + LANE], _rope(ql[:, o0 + LANE:o0 + 2 * LANE], tb, 1.0)]
            ks += [kn[:, hh * LANE:(hh + 1) * LANE], kr1]
            vx += [vv[:, hh * MLA_V:(hh + 1) * MLA_V], ones]
        return (jnp.concatenate(qs, axis=1) * (ATT_SCALE * LOG2E), jnp.concatenate(ks, axis=1),
                jnp.concatenate(vx, axis=1))

    qa, ka, vxa = _rowwise(f_qkv, [full(qlat), (kvl, hv, 0), (kvl, hv, 1), full(krr), full(tab)], [],
                           [(MLA_HEADS * MLA_QK_PAD, BF16), (MLA_HEADS * MLA_QK_PAD, BF16), (2 * hv, BF16)],
                           tile=256, name="mla_qkv_build")
    o_mla, qa_lse = riding("flash_fwd", lambda r: _flash_fwd(qa, ka, vxa, ride=r))
    y_mla = _mm(o_mla, w["mla_wo"], out_dtype=BF16, name="mm_mla_wo")

    def f_merge(yg, ym, ga, gb):
        return _sigmoid(ga) * yg + _sigmoid(gb) * ym

    (merged,) = _rowwise(f_merge, [full(y_gla), full(y_mla), blk(p, "ga"), blk(p, "gb")], [], [(d, BF16)],
                         tile=256, name="merge")
    mix = _mm(merged, w["w_out"], name="mm_w_out")

    def f_res_ln2(xv, mx, mod, g):
        x2v = xv + mod[2:3] * mx
        return x2v, (x2v * _rstd(x2v) * g) * (1.0 + mod[4:5]) + mod[3:4]

    x2, h2 = _rowwise(f_res_ln2, [full(x), full(mix)], [mod8, g2], [(d, F32), (d, BF16)], tile=256,
                      name="res_ln2_modulate")
    gu = _mm(h2, w["ffn_w_in"], out_dtype=BF16, name="mm_ffn_in")

    def f_swiglu(gv, uv):
        return gv * _sigmoid(gv) * uv

    (act,) = _rowwise(f_swiglu, [(gu, ffn, 0), (gu, ffn, 1)], [], [(ffn, BF16)], tile=128, name="swiglu")
    f_out = _mm(act, w["ffn_w_down"], name="mm_ffn_down")

    def f_head(x2v, fv, tg, mod, g):
        x3 = x2v + mod[5:6] * fv
        r = _rstd(x3)
        xh = x3 * r
        e = xh * g - tg
        loss_rows = 0.5 * jnp.mean(e * e, axis=-1, keepdims=True)
        dy = e * (1.0 / d)
        dx3 = _rms_bwd(dy * g, xh, r)
        loss = jnp.broadcast_to(jnp.sum(loss_rows, axis=0, keepdims=True), (1, LANE))
        return (dx3, dx3 * mod[5:6], loss, jnp.sum(dy * xh, axis=0, keepdims=True),
                jnp.sum(dx3 * fv, axis=0, keepdims=True))

    dx3, df, loss_v, dg3, dgate_f = _rowwise(f_head, [full(x2), full(f_out), full(target)], [mod8, g3],
                                             [(d, F32), (d, BF16)], [LANE, d, d], tile=256, name="loss_head")
    da = _mm(df, w["ffn_w_down"], tb=True, out_dtype=BF16, name="mm_ffn_down_dx")
    big["ffn_w_down"] = _mm(act, df, ta=True, out_dtype=BF16, name="mm_ffn_down_dw")

    def f_swiglu_bwd(gv, uv, dav):
        sg = _sigmoid(gv)
        return jnp.concatenate([dav * uv * (sg * (1.0 + gv * (1.0 - sg))), dav * (gv * sg)], axis=1)

    (dgu,) = _rowwise(f_swiglu_bwd, [(gu, ffn, 0), (gu, ffn, 1), full(da)], [], [(2 * ffn, BF16)], tile=128,
                      name="swiglu_bwd")
    dh2 = _mm(dgu, w["ffn_w_in"], tb=True, name="mm_ffn_in_dx")
    big["ffn_w_in"] = _mm(h2, dgu, ta=True, out_dtype=BF16, slabs=True, name="mm_ffn_in_dw")

    def f_ln2_bwd(x2v, dh, dx3v, mx, mod, g):
        r = _rstd(x2v)
        xh = x2v * r
        dn = dh * (1.0 + mod[4:5])
        dx2 = dx3v + _rms_bwd(dn * g, xh, r)
        return (dx2, dx2 * mod[2:3],
                jnp.sum(dh * (xh * g), axis=0, keepdims=True), jnp.sum(dh, axis=0, keepdims=True),
                jnp.sum(dn * xh, axis=0, keepdims=True), jnp.sum(dx2 * mx, axis=0, keepdims=True))

    dx2, dmix, dscale_f, dshift_f, dg2, dgate_m = _rowwise(
        f_ln2_bwd, [full(x2), full(dh2), full(dx3), full(mix)], [mod8, g2], [(d, F32), (d, BF16)],
        [d, d, d, d], tile=256, name="ln2_bwd")
    dmerged = _mm(dmix, w["w_out"], tb=True, out_dtype=BF16, name="mm_w_out_dx")
    big["w_out"] = _mm(merged, dmix, ta=True, out_dtype=BF16, name="mm_w_out_dw")

    def f_merge_bwd(dm, yg, ym, ga, gb):
        sa, sb = _sigmoid(ga), _sigmoid(gb)
        return dm * sa, dm * sb, dm * yg * sa * (1.0 - sa), dm * ym * sb * (1.0 - sb)

    dy_gla, dy_mla, dp_ga, dp_gb = _rowwise(
        f_merge_bwd, [full(dmerged), full(y_gla), full(y_mla), blk(p, "ga"), blk(p, "gb")], [],
        [(d, BF16)] * 4, tile=256, name="merge_bwd")
    do_n = _mm(dy_gla, w["gla_wo"], tb=True, name="mm_gla_wo_dx")
    big["gla_wo"] = _mm(o_n, dy_gla, ta=True, out_dtype=BF16, name="mm_gla_wo_dw")
    do_m = _mm(dy_mla, w["mla_wo"], tb=True, out_dtype=BF16, name="mm_mla_wo_dx")
    big["mla_wo"] = _mm(o_mla, dy_mla, ta=True, out_dtype=BF16, name="mm_mla_wo_dw")

    def f_gla_out_bwd(don, ov, pg, g):
        dos, dpgs = [], []
        dg = jnp.zeros((1, GLA_DV), F32)
        for hh in range(GLA_HEADS):
            sl = slice(hh * GLA_DV, (hh + 1) * GLA_DV)
            oh, ph, dn = ov[:, sl], pg[:, sl], don[:, sl]
            r = _rstd(oh)
            xh = oh * r
            sg = _sigmoid(ph)
            dpre = dn * (ph * sg)
            dg = dg + jnp.sum(dpre * xh, axis=0, keepdims=True)
            dos.append(_rms_bwd(dpre * g, xh, r))
            dpgs.append(dn * (xh * g) * (sg * (1.0 + ph * (1.0 - sg))))
        return jnp.concatenate(dos, axis=1), jnp.concatenate(dpgs, axis=1), dg

    do_gla, dp_g, dg_on = _rowwise(f_gla_out_bwd, [full(do_n), full(o_gla), blk(p, "g")], [w["gla_onorm_g"]],
                                   [(d, F32), (d, BF16)], [GLA_DV], tile=256, name="gla_out_norm_bwd")
    dp_q, dp_k, dp_v, dla = _gla_bwd(p, la, do_gla, states, lay)

    def f_gk_bwd(dlav, pgk, gkw, gkb):
        z = _dot(pgk.astype(BF16), gkw.astype(BF16)) + gkb
        dz = dlav * (1.0 / GLA_GATE_NORMALIZER) * _sigmoid(-z)
        return dz, _dot(dz.astype(BF16), gkw.astype(BF16), 1, 1), jnp.sum(dz, axis=0, keepdims=True)

    dz, dp_gk, dgk_b = _rowwise(f_gk_bwd, [full(dla), blk(p, "gk")], [w["gla_gk_w"], w["gla_gk_b"]],
                                [(GLA_HEADS * GLA_DK, BF16), (LANE, BF16)], [GLA_HEADS * GLA_DK], tile=512,
                                name="gla_gate_bwd")
    p_gk = lax.slice_in_dim(p, lay["gk"], lay["gk"] + LANE, axis=1)
    big["gla_gk_w"] = _mm(p_gk, dz, ta=True, name="mm_gla_gk_dw")[:GLA_GATE_RANK]

    def f_do_aug(dom, om):
        parts = []
        for hh in range(MLA_HEADS):
            dh_ = dom[:, hh * MLA_V:(hh + 1) * MLA_V]
            delta = jnp.sum(dh_ * om[:, hh * MLA_V:(hh + 1) * MLA_V], axis=1, keepdims=True)
            parts += [dh_.astype(BF16), _with_spare(jnp.zeros(dh_.shape, BF16), -delta, 0)]
        return jnp.concatenate(parts, axis=1)

    (doa,) = _rowwise(f_do_aug, [full(do_m), full(o_mla)], [], [(2 * hv, BF16)], tile=256, name="mla_do_delta")
    dqa, dka, dva = riding("flash_bwd", lambda r: _flash_bwd(qa_lse, ka, vxa, doa, ride=r))

    def f_qkv_bwd(dq, dk, dvv, tb):
        dqs, dkn = [], []
        dkr = jnp.zeros((dq.shape[0], LANE), F32)
        for hh in range(MLA_HEADS):
            o0 = hh * MLA_QK_PAD
            dqs += [dq[:, o0:o0 + LANE], _rope(dq[:, o0 + LANE:o0 + 2 * LANE], tb, -1.0)]
            dkn.append(dk[:, o0:o0 + LANE])
            dkr = dkr + dk[:, o0 + LANE:o0 + 2 * LANE]
        return jnp.concatenate(dqs, axis=1), jnp.concatenate(dkn + [dvv], axis=1), dkr

    dqlat, dkvl, dkrr = _rowwise(f_qkv_bwd, [full(dqa), full(dka), full(dva), full(tab)], [],
                                 [(MLA_HEADS * MLA_QK_PAD, BF16), (2 * hv, BF16), (LANE, F32)], tile=256,
                                 name="mla_qkv_build_bwd")
    dcqn = _mm(dqlat, w["mla_wuq"], tb=True, name="mm_mla_wuq_dx")
    big["mla_wuq"] = _mm(cqn, dqlat, ta=True, out_dtype=BF16, name="mm_mla_wuq_dw")
    dckvn = _mm(dkvl, w["mla_wukv"], tb=True, name="mm_mla_wukv_dx")
    big["mla_wukv"] = _mm(ckvn, dkvl, ta=True, out_dtype=BF16, name="mm_mla_wukv_dw")

    def f_mla_prep_bwd(dq, dkv, dkr, cq, ckv, tb, gq, gkv):
        rq, rk = _rstd(cq), _rstd(ckv)
        xq, xk = cq * rq, ckv * rk
        return (_rms_bwd(dq * gq, xq, rq), _rms_bwd(dkv * gkv, xk, rk), _rope(dkr, tb, -1.0),
                jnp.sum(dq * xq, axis=0, keepdims=True), jnp.sum(dkv * xk, axis=0, keepdims=True))

    dp_cq, dp_ckv, dp_kr, dg_q, dg_kv = _rowwise(
        f_mla_prep_bwd, [full(dcqn), full(dckvn), full(dkrr), blk(p, "cq"), blk(p, "ckv"), full(tab)],
        [w["mla_q_norm_g"], w["mla_kv_norm_g"]], [(pw["cq"], BF16), (pw["ckv"], BF16), (LANE, BF16)],
        [pw["cq"], pw["ckv"]], tile=512, name="mla_prep_bwd")

    pieces = dict(v=dp_v, g=dp_g, ga=dp_ga, gb=dp_gb, q=dp_q, k=dp_k, cq=dp_cq, ckv=dp_ckv, gk=dp_gk, kr=dp_kr)
    dp = jnp.concatenate([pieces[n] for n in MY_ORDER], axis=1)
    big["w_in"] = riding("in_proj_dw", lambda r: _mm(dp, h, ta=True, out_dtype=BF16, name="mm_in_proj_dw", ride=r))
    dh = riding("in_proj_dx", lambda r: _mm(dp, w["w_in"], tb=True, name="mm_in_proj_dx", ride=r))

    def f_ln1_bwd(xv, dhv, dx2v, mod, g):
        r = _rstd(xv)
        xh = xv * r
        dn = dhv * (1.0 + mod[1:2])
        return (dx2v + _rms_bwd(dn * g, xh, r),
                jnp.sum(dhv * (xh * g), axis=0, keepdims=True), jnp.sum(dhv, axis=0, keepdims=True),
                jnp.sum(dn * xh, axis=0, keepdims=True))

    grad_x, dscale_m, dshift_m, dg1 = _rowwise(f_ln1_bwd, [full(x), full(dh), full(dx2)], [mod8, g1],
                                               [(d, F32)], [d, d, d], tile=256, name="ln1_bwd")

    dmod = jnp.concatenate([dshift_m, dscale_m, dgate_m, dshift_f, dscale_f, dgate_f], axis=1)
    small = dict(ada_b=dmod, norm_mix_g=dg1, gla_gk_b=dgk_b, gla_onorm_g=dg_on, mla_q_norm_g=dg_q,
                 mla_kv_norm_g=dg_kv, norm_ffn_g=dg2, final_norm_g=dg3)
    return loss_v[0, 0], grad_x, big, small


N_PEER = N_DEV - 1


def _exchange_copies(ins, outs, sems, scatter):
    send_sems, recv_sems, local_sems = sems
    x, y, c = lax.axis_index("x"), lax.axis_index("y"), lax.axis_index("c")
    me = 4 * x + 2 * y + c
    peers = []
    for rel in range(1, N_DEV):
        px = 1 - x if rel & 4 else x
        py = 1 - y if rel & 2 else y
        pc = 1 - c if rel & 1 else c
        peers.append(((px, py, pc), 4 * px + 2 * py + pc))

    def remote(a, k, src_slot, dst_slot):
        src = ins[a].at[src_slot] if scatter else ins[a]
        return pltpu.make_async_remote_copy(
            src_ref=src, dst_ref=outs[a].at[dst_slot], send_sem=send_sems.at[a * N_PEER + k],
            recv_sem=recv_sems.at[a * N_PEER + k], device_id=peers[k][0], device_id_type=pl.DeviceIdType.MESH)

    local, sends, recvs = [], [], []
    for a in range(len(ins)):
        src = ins[a].at[me] if scatter else ins[a]
        local.append(pltpu.make_async_copy(src, outs[a].at[me], local_sems.at[a]))
        for k in range(N_PEER):
            sends.append(remote(a, k, peers[k][1], me))
            recvs.append(remote(a, k, peers[k][1], peers[k][1]))
    return local, sends, recvs


def _exchange_start(ins, outs, sems, scatter):
    local, sends, _ = _exchange_copies(ins, outs, sems, scatter)
    for cp in local + sends:
        cp.start()


def _exchange_wait(ins, outs, sems, scatter):
    local, sends, recvs = _exchange_copies(ins, outs, sems, scatter)
    for cp in recvs:
        cp.wait_recv()
    for cp in sends:
        cp.wait_send()
    for cp in local:
        cp.wait()


def _exchange_shapes(arrs, scatter):
    n = len(arrs)
    out_shape = [jax.ShapeDtypeStruct(a.shape if scatter else (N_DEV,) + a.shape, a.dtype) for a in arrs]
    sems = [pltpu.SemaphoreType.DMA((n * N_PEER,)), pltpu.SemaphoreType.DMA((n * N_PEER,)),
            pltpu.SemaphoreType.DMA((n,))]
    return out_shape, sems


def _exchange(arrs, *, scatter, name):
    n = len(arrs)

    def body(*refs):
        ins, outs, sems = refs[:n], refs[n:2 * n], refs[2 * n:]
        _exchange_start(ins, outs, sems, scatter)
        _exchange_wait(ins, outs, sems, scatter)

    hbm = pl.BlockSpec(memory_space=pltpu.HBM)
    out_shape, sems = _exchange_shapes(arrs, scatter)
    return pl.pallas_call(body, name=name, in_specs=[hbm] * n, out_specs=[hbm] * n, out_shape=out_shape,
                          scratch_shapes=sems)(*arrs)


def _gather_once_per_chip(arrs, *, name):
    n = len(arrs)

    def body(*refs):
        ins, outs = refs[:n], refs[n:2 * n]
        send_sems, recv_sems, local_sems = refs[2 * n:]
        x, y, c = lax.axis_index("x"), lax.axis_index("y"), lax.axis_index("c")
        sibling = (x, y, 1 - c)
        chips = [(1 - x, y), (x, 1 - y), (1 - x, 1 - y)]

        def slot(px, py, pc):
            return 4 * px + 2 * py + pc

        def copy(a, k, block, to, src=None):
            dst = outs[a].at[slot(*block)]
            return pltpu.make_async_remote_copy(
                src_ref=dst if src is None else src, dst_ref=dst, send_sem=send_sems.at[a * N_PEER + k],
                recv_sem=recv_sems.at[a * N_PEER + k], device_id=to, device_id_type=pl.DeviceIdType.MESH)

        local, sends = [], []
        for a in range(n):
            local.append(pltpu.make_async_copy(ins[a], outs[a].at[slot(x, y, c)], local_sems.at[a]))
            sends.append(copy(a, 0, (x, y, c), sibling, src=ins[a]))
            sends += [copy(a, 1 + j, (x, y, c), (*chip, c), src=ins[a]) for j, chip in enumerate(chips)]
        for cp in local + sends:
            cp.start()
        for a in range(n):
            for j, chip in enumerate(chips):
                copy(a, 1 + j, (*chip, c), (x, y, c)).wait_recv()
                sends.append(copy(a, 4 + j, (*chip, c), sibling))
                sends[-1].start()
        for a in range(n):
            copy(a, 0, sibling, (x, y, c)).wait_recv()
            for j, chip in enumerate(chips):
                copy(a, 4 + j, (*chip, 1 - c), (x, y, c)).wait_recv()
        for cp in sends:
            cp.wait_send()
        for cp in local:
            cp.wait()

    hbm = pl.BlockSpec(memory_space=pltpu.HBM)
    out_shape, sems = _exchange_shapes(arrs, False)
    return pl.pallas_call(body, name=name, in_specs=[hbm] * n, out_specs=[hbm] * n, out_shape=out_shape,
                          scratch_shapes=sems)(*arrs)


def _call(body, *, name, grid, in_specs, out_specs, out_shape, scratch_shapes, sem, args, ride=None,
          vmem_limit=VMEM_LIMIT):
    if ride is None:
        res = pl.pallas_call(body, name=name, grid=grid, in_specs=in_specs, out_specs=out_specs, out_shape=out_shape,
                             scratch_shapes=scratch_shapes, compiler_params=_cparams(sem, vmem_limit))(*args)
        return res, None
    arrs, scatter = ride
    n, n_in, n_out, n_scr = len(arrs), len(in_specs), len(out_specs), len(scratch_shapes)
    x_shape, x_sems = _exchange_shapes(arrs, scatter)

    def hosted(*refs):
        c_in, x_in = refs[:n_in], refs[n_in:n_in + n]
        c_out, x_out = refs[n_in + n:n_in + n + n_out], refs[n_in + n + n_out:n_in + 2 * n + n_out]
        scr = refs[n_in + 2 * n + n_out:]
        c_scr, sems = scr[:n_scr], scr[n_scr:]
        first = functools.reduce(jnp.logical_and, [pl.program_id(a) == 0 for a in range(len(grid))])
        last = functools.reduce(jnp.logical_and, [pl.program_id(a) == grid[a] - 1 for a in range(len(grid))])

        @pl.when(first)
        def _():
            _exchange_start(x_in, x_out, sems, scatter)

        body(*c_in, *c_out, *c_scr)

        @pl.when(last)
        def _():
            _exchange_wait(x_in, x_out, sems, scatter)

    hbm = pl.BlockSpec(memory_space=pltpu.HBM)
    res = pl.pallas_call(
        hosted, name=name, grid=grid, in_specs=list(in_specs) + [hbm] * n, out_specs=list(out_specs) + [hbm] * n,
        out_shape=list(out_shape) + x_shape, scratch_shapes=list(scratch_shapes) + x_sems,
        compiler_params=_cparams(("arbitrary",) * len(grid), vmem_limit))(*args, *arrs)
    return res[:n_out], res[n_out:]


def _adamw_math(w, g, m, v):
    m_new = ADAM_B1 * m + (1.0 - ADAM_B1) * g
    v_new = ADAM_B2 * v + (1.0 - ADAM_B2) * (g * g)
    m_hat = m_new / (1.0 - ADAM_B1 ** ADAM_STEP)
    v_hat = v_new / (1.0 - ADAM_B2 ** ADAM_STEP)
    delta = -ADAM_LR * (m_hat / (jnp.sqrt(v_hat) + ADAM_EPS) + ADAM_WD * w)
    return delta, m_new, v_new


def _adamw_by_columns(w, g, m, v, *, name):
    _, r, c = w.shape
    tc = 256

    def body(w_ref, g_ref, m_ref, v_ref, go_ref, d_ref, mo_ref, vo_ref):
        gv = g_ref[0].astype(F32)
        for s in range(1, N_DEV):
            gv = gv + g_ref[s].astype(F32)
        delta, m_new, v_new = _adamw_math(w_ref[0], gv, m_ref[0], v_ref[0])
        go_ref[0] = gv
        d_ref[0] = delta
        mo_ref[0] = m_new
        vo_ref[0] = v_new

    spec = pl.BlockSpec((1, r, tc), lambda i: (0, 0, i))
    return pl.pallas_call(
        body, name=name, grid=(c // tc,), in_specs=[spec, pl.BlockSpec((N_DEV, r, tc), lambda i: (0, 0, i)), spec, spec],
        out_specs=[spec] * 4, out_shape=[jax.ShapeDtypeStruct((1, r, c), F32)] * 4,
        compiler_params=_cparams(("parallel",)),
    )(w, g, m, v)


def _adamw(w, g, m, v, *, name):
    _, r, c = w.shape
    slots = g.ndim == 3
    if r > 16 and r % 16:
        return _adamw_by_columns(w, g, m, v, name=name)
    tr = r
    for cand in (128, 64, 32, 16):
        if r % cand == 0 and r > cand:
            tr = cand
            break

    def body(w_ref, g_ref, m_ref, v_ref, go_ref, d_ref, mo_ref, vo_ref):
        if slots:
            gv = g_ref[0].astype(F32)
            for s in range(1, N_DEV):
                gv = gv + g_ref[s].astype(F32)
        else:
            gv = g_ref[...]
        delta, m_new, v_new = _adamw_math(w_ref[0], gv, m_ref[0], v_ref[0])
        go_ref[0] = gv
        d_ref[0] = delta
        mo_ref[0] = m_new
        vo_ref[0] = v_new

    spec = pl.BlockSpec((1, tr, c), lambda i: (0, i, 0))
    g_spec = pl.BlockSpec((N_DEV, tr, c), lambda i: (0, i, 0)) if slots else pl.BlockSpec((tr, c), lambda i: (i, 0))
    return pl.pallas_call(
        body, name=name, grid=(r // tr,), in_specs=[spec, g_spec, spec, spec], out_specs=[spec] * 4,
        out_shape=[jax.ShapeDtypeStruct((1, r, c), F32)] * 4, compiler_params=_cparams(("parallel",)),
    )(w, g, m, v)


def _unshard_cols(g):
    return jnp.transpose(g, (1, 0, 2)).reshape(g.shape[1], -1)

def _shard_cols(full):
    r = full.shape[0]
    return jnp.transpose(full.reshape(r, N_DEV, -1), (1, 0, 2))


def _w_in_pieces(d):
    wd, _, ref_off, _, total = _in_layout(d)
    shard = sum(wd.values()) // N_DEV
    out = []
    for n in IN_NAMES:
        r0, r1 = ref_off[n], ref_off[n] + wd[n]
        for s in range(N_DEV):
            lo, hi = max(r0, s * shard), min(r1, (s + 1) * shard)
            if lo < hi:
                out.append((n, s, lo - s * shard, lo - r0, hi - lo))
    return out


def _w_in_gathered_to_mine(g, d):
    wd, pw, _, _, _ = _in_layout(d)
    pieces = _w_in_pieces(d)
    cols = []
    for n in MY_ORDER:
        cols += [lax.slice_in_dim(g[s], a, a + wdt, axis=1) for (m, s, a, _, wdt) in pieces if m == n]
        if pw[n] != wd[n]:
            cols.append(jnp.zeros((g.shape[1], pw[n] - wd[n]), g.dtype))
    return jnp.concatenate(cols, axis=1)


def _w_in_mine_to_slabs(g_mine, d):
    _, _, _, my_off, _ = _in_layout(d)
    pieces = _w_in_pieces(d)
    slabs = []
    for s in range(N_DEV):
        parts = [lax.slice_in_dim(g_mine, my_off[m] + b, my_off[m] + b + wdt, axis=0)
                 for (m, s2, _, b, wdt) in pieces if s2 == s]
        slabs.append(jnp.concatenate(parts, axis=0))
    return jnp.stack(slabs)


def _wuq_to_mine(wq):
    r = wq.shape[0]
    w3 = wq.reshape(r, MLA_HEADS, MLA_NOPE + MLA_ROPE)
    w3 = jnp.pad(w3, ((0, 0), (0, 0), (0, MLA_QK_PAD - MLA_NOPE - MLA_ROPE)))
    return w3.reshape(r, MLA_HEADS * MLA_QK_PAD)


def _wuq_from_mine(g):
    r = g.shape[0]
    return g.reshape(r, MLA_HEADS, MLA_QK_PAD)[:, :, :MLA_NOPE + MLA_ROPE].reshape(r, -1)


def _wukv_to_mine(wkv):
    r = wkv.shape[0]
    w3 = wkv.reshape(r, MLA_HEADS, MLA_NOPE + MLA_V)
    return jnp.concatenate([w3[:, :, :MLA_NOPE].reshape(r, -1), w3[:, :, MLA_NOPE:].reshape(r, -1)], axis=1)


def _wukv_from_mine(g):
    r = g.shape[0]
    kn = g[:, :MLA_HEADS * MLA_NOPE].reshape(r, MLA_HEADS, MLA_NOPE)
    vv = g[:, MLA_HEADS * MLA_NOPE:].reshape(r, MLA_HEADS, MLA_V)
    return jnp.concatenate([kn, vv], axis=2).reshape(r, -1)


COL_SHARDED = ("w_in", "gla_gk_w", "mla_wuq", "mla_wukv", "ffn_w_in")


def _gathered_to_mine(name, g, d):
    if name == "w_in":
        return _w_in_gathered_to_mine(g, d)
    full = _unshard_cols(g) if name in COL_SHARDED else g.reshape(-1, g.shape[-1])
    if name == "gla_gk_w":
        return jnp.pad(full, ((0, LANE - GLA_GATE_RANK), (0, 0)))
    if name == "mla_wuq":
        return _wuq_to_mine(full)
    if name == "mla_wukv":
        return _wukv_to_mine(full)
    return full


def _grad_to_slabs(name, g, d):
    if g.ndim == 3:
        return g.astype(BF16)
    if name == "w_in":
        return _w_in_mine_to_slabs(g, d).astype(BF16)
    if name == "mla_wuq":
        g = _wuq_from_mine(g)
    elif name == "mla_wukv":
        g = _wukv_from_mine(g)
    s = _shard_cols(g) if name in COL_SHARDED else g.reshape(N_DEV, -1, g.shape[-1])
    return s.astype(BF16)


class _Rides:
    GATHER = {"in_proj": ("gla_wo", "mla_wuq", "mla_wukv", "mla_wo", "w_out"),
              "flash_fwd": ("ffn_w_in", "ffn_w_down")}
    SCATTER = {"flash_bwd": ("ffn_w_in", "ffn_w_down", "w_out", "gla_wo", "mla_wo", "gla_gk_w"),
               "in_proj_dw": ("mla_wuq", "mla_wukv"),
               "in_proj_dx": ("w_in",)}

    def __init__(self, send, d):
        self.send, self.d, self.recv = send, d, {}

    def ride(self, stage, grads):
        if stage in self.GATHER:
            return [self.send[n] for n in self.GATHER[stage]], False
        return [_grad_to_slabs(n, grads[n], self.d) for n in self.SCATTER[stage]], True

    def done(self, stage, rode, w):
        if stage in self.GATHER:
            for n, g in zip(self.GATHER[stage], rode):
                w[n] = _gathered_to_mine(n, g, self.d)
        else:
            self.recv.update(zip(self.SCATTER[stage], rode))


def kernel(x, c, positions, ada_w, ada_b, norm_mix_g, w_in, gla_gk_w, gla_gk_b, gla_onorm_g, gla_wo, mla_q_norm_g, mla_wuq, mla_kv_norm_g, mla_wukv, mla_wo, w_out, norm_ffn_g, ffn_w_in, ffn_w_down, final_norm_g, loss_target, m_ada_w, m_ada_b, m_norm_mix_g, m_w_in, m_gla_gk_w, m_gla_gk_b, m_gla_onorm_g, m_gla_wo, m_mla_q_norm_g, m_mla_wuq, m_mla_kv_norm_g, m_mla_wukv, m_mla_wo, m_w_out, m_norm_ffn_g, m_ffn_w_in, m_ffn_w_down, m_final_norm_g, v_ada_w, v_ada_b, v_norm_mix_g, v_w_in, v_gla_gk_w, v_gla_gk_b, v_gla_onorm_g, v_gla_wo, v_mla_q_norm_g, v_mla_wuq, v_mla_kv_norm_g, v_mla_wukv, v_mla_wo, v_w_out, v_norm_ffn_g, v_ffn_w_in, v_ffn_w_down, v_final_norm_g):
    wts = dict(ada_w=ada_w, ada_b=ada_b, norm_mix_g=norm_mix_g, w_in=w_in, gla_gk_w=gla_gk_w, gla_gk_b=gla_gk_b,
               gla_onorm_g=gla_onorm_g, gla_wo=gla_wo, mla_q_norm_g=mla_q_norm_g, mla_wuq=mla_wuq,
               mla_kv_norm_g=mla_kv_norm_g, mla_wukv=mla_wukv, mla_wo=mla_wo, w_out=w_out, norm_ffn_g=norm_ffn_g,
               ffn_w_in=ffn_w_in, ffn_w_down=ffn_w_down, final_norm_g=final_norm_g)
    mom_m = dict(zip(WEIGHTS, (m_ada_w, m_ada_b, m_norm_mix_g, m_w_in, m_gla_gk_w, m_gla_gk_b, m_gla_onorm_g,
                               m_gla_wo, m_mla_q_norm_g, m_mla_wuq, m_mla_kv_norm_g, m_mla_wukv, m_mla_wo, m_w_out,
                               m_norm_ffn_g, m_ffn_w_in, m_ffn_w_down, m_final_norm_g)))
    mom_v = dict(zip(WEIGHTS, (v_ada_w, v_ada_b, v_norm_mix_g, v_w_in, v_gla_gk_w, v_gla_gk_b, v_gla_onorm_g,
                               v_gla_wo, v_mla_q_norm_g, v_mla_wuq, v_mla_kv_norm_g, v_mla_wukv, v_mla_wo, v_w_out,
                               v_norm_ffn_g, v_ffn_w_in, v_ffn_w_down, v_final_norm_g)))
    seq, d = x.shape[1], x.shape[2]
    me = 4 * lax.axis_index("x") + 2 * lax.axis_index("y") + lax.axis_index("c")

    def two_d(a):
        return a.reshape(a.shape[-2], a.shape[-1]) if a.ndim >= 2 else a.reshape(1, -1)

    shard = {n: two_d(wts[n]) for n in BIG}
    send = {n: shard[n].astype(F32 if n == "gla_gk_w" else BF16) for n in BIG}
    got = _gather_once_per_chip([send["w_in"], send["gla_gk_w"], two_d(c)], name="comm_all_gather_first")
    c_all = got[2].reshape(N_DEV, d)
    w = dict(
        w_in=_gathered_to_mine("w_in", got[0], d), gla_gk_w=_gathered_to_mine("gla_gk_w", got[1], d),
        gla_gk_b=two_d(gla_gk_b), gla_onorm_g=two_d(gla_onorm_g), mla_q_norm_g=two_d(mla_q_norm_g),
        mla_kv_norm_g=two_d(mla_kv_norm_g), norm_mix_g=two_d(norm_mix_g), norm_ffn_g=two_d(norm_ffn_g),
        final_norm_g=two_d(final_norm_g))
    rides = _Rides(send, d)

    c_pad = jnp.pad(c_all, ((0, 16 - N_DEV), (0, 0)))
    (c_act,) = _rowwise(lambda cv: cv * _sigmoid(cv), [(c_pad, d, 0)], [], [(d, F32)], tile=16, name="silu_c")
    ada_w2 = two_d(ada_w)
    mod_part = _mm(c_act, ada_w2, name="mm_ada")[:N_DEV]
    (mod_all,) = _exchange([mod_part], scatter=False, name="comm_all_gather_mod")
    mod_mine = lax.dynamic_index_in_dim(mod_all, me, axis=1, keepdims=False).reshape(1, -1) + two_d(ada_b)
    mod8 = jnp.pad(mod_mine.reshape(6, d), ((0, 2), (0, 0)))

    inv_freq = ROPE_THETA ** (-jnp.arange(0, MLA_ROPE, 2, dtype=F32) / MLA_ROPE)
    ang = positions.reshape(seq, 1).astype(F32) * inv_freq[None, :]
    cos, sin, z32 = jnp.cos(ang), jnp.sin(ang), jnp.zeros((seq, 32), F32)
    tab = jnp.concatenate([cos, cos, z32, z32, -sin, z32, z32, z32, z32, sin, z32, z32], axis=1)
    loss_local, grad_x, _, small = _local_step(x.reshape(seq, d), loss_target.reshape(seq, d), tab, mod8, w, rides)

    recv = rides.recv
    pack = jnp.concatenate([small[n] for n in SMALL], axis=1)
    (pack_all,) = _exchange([pack], scatter=False, name="comm_all_gather_small")
    pack_all = pack_all.reshape(N_DEV, -1)

    res = {}
    for n in BIG:
        if n == "w_in":
            t_res = _adamw(jnp.swapaxes(wts[n], 1, 2), recv[n], jnp.swapaxes(mom_m[n], 1, 2),
                           jnp.swapaxes(mom_v[n], 1, 2), name="adamw_" + n)
            res[n] = tuple(jnp.swapaxes(a, 1, 2) for a in t_res)
        else:
            res[n] = _adamw(wts[n], recv[n], mom_m[n], mom_v[n], name="adamw_" + n)
    n_ada = ada_w2.shape[1]
    dmod_cols = lax.dynamic_slice_in_dim(pack_all[:, :6 * d], me * n_ada, n_ada, axis=1)

    def f_outer(cat, dm):
        acc = cat[:, 0:1] * dm[0:1]
        for b in range(1, N_DEV):
            acc = acc + cat[:, b:b + 1] * dm[b:b + 1]
        return acc

    (g_ada_w,) = _rowwise(f_outer, [(jnp.transpose(c_act[:N_DEV]), N_DEV, 0)], [dmod_cols], [(n_ada, F32)],
                          tile=256, name="ada_w_grad")
    res["ada_w"] = _adamw(ada_w, g_ada_w, m_ada_w, v_ada_w, name="adamw_ada_w")
    w_small = jnp.concatenate([two_d(wts[n]) for n in SMALL], axis=1)[None]
    m_small = jnp.concatenate([two_d(mom_m[n]) for n in SMALL], axis=1)[None]
    v_small = jnp.concatenate([two_d(mom_v[n]) for n in SMALL], axis=1)[None]
    small_res = _adamw(w_small, pack_all.reshape(N_DEV, 1, -1), m_small, v_small, name="adamw_small")
    off = 0
    for n in SMALL:
        width = wts[n].size
        res[n] = tuple(lax.slice_in_dim(a, off, off + width, axis=2) for a in small_res)
        off += width

    loss = lax.psum(loss_local, ("x", "y", "c"))
    outs = [loss, grad_x.reshape(x.shape)]
    for kind in range(4):
        outs += [res[n][kind].reshape(wts[n].shape) for n in WEIGHTS]
    return tuple(outs)
```

```python
import functools

import jax
import jax.numpy as jnp
from jax import lax
from jax.experimental import pallas as pl
from jax.experimental.pallas import tpu as pltpu

F32 = jnp.float32
BF16 = jnp.bfloat16

N_DEV = 8
GLA_HEADS = 4
GLA_DK = 256
GLA_DV = 512
GLA_GATE_RANK = 16
GLA_GATE_NORMALIZER = 16.0
GLA_CHUNK = 64
MLA_HEADS = 16
MLA_NOPE = 128
MLA_ROPE = 64
MLA_V = 128
MLA_QK_PAD = 256
ROPE_THETA = 10000.0
NORM_EPS = 1e-6
ATT_SCALE = (MLA_NOPE + MLA_ROPE) ** -0.5
GLA_QSCALE = GLA_DK ** -0.5

ADAM_LR = 0.001
ADAM_B1 = 0.9
ADAM_B2 = 0.999
ADAM_EPS = 1e-08
ADAM_WD = 0.01
ADAM_STEP = 10

LANE = 128
VMEM_LIMIT = 48 * 1024 * 1024
FLASH_BWD_VMEM = 58 * 1024 * 1024
MM_TILE_BYTES = 6 * 1024 * 1024
LOG2E = 1.4426950408889634
LN2 = 0.6931471805599453
NEG = -1e30

IN_NAMES = ("q", "k", "v", "g", "gk", "cq", "ckv", "kr", "ga", "gb")
MY_ORDER = ("v", "g", "ga", "gb", "q", "k", "cq", "ckv", "gk", "kr")

WEIGHTS = ("ada_w", "ada_b", "norm_mix_g", "w_in", "gla_gk_w", "gla_gk_b", "gla_onorm_g", "gla_wo",
           "mla_q_norm_g", "mla_wuq", "mla_kv_norm_g", "mla_wukv", "mla_wo", "w_out", "norm_ffn_g",
           "ffn_w_in", "ffn_w_down", "final_norm_g")
BIG = ("w_in", "gla_gk_w", "gla_wo", "mla_wuq", "mla_wukv", "mla_wo", "w_out", "ffn_w_in", "ffn_w_down")
SMALL = ("ada_b", "norm_mix_g", "gla_gk_b", "gla_onorm_g", "mla_q_norm_g", "mla_kv_norm_g", "norm_ffn_g",
         "final_norm_g")


def _in_layout(d):
    w = dict(q=d // 2, k=d // 2, v=d, g=d, gk=GLA_GATE_RANK, cq=d // 4, ckv=512, kr=MLA_ROPE, ga=d, gb=d)
    pw = {n: -(-w[n] // LANE) * LANE for n in w}
    ref_off, o = {}, 0
    for n in IN_NAMES:
        ref_off[n] = o
        o += w[n]
    my_off, o = {}, 0
    for n in MY_ORDER:
        assert o % pw[n] == 0
        my_off[n] = o
        o += pw[n]
    return w, pw, ref_off, my_off, o


def _cparams(sem=None, vmem_limit=VMEM_LIMIT):
    return pltpu.CompilerParams(dimension_semantics=sem, vmem_limit_bytes=vmem_limit)


def _dot(a, b, ca=1, cb=0):
    return lax.dot_general(a, b, (((ca,), (cb,)), ((), ())), preferred_element_type=F32)


def _tile(n, cap):
    if n <= cap:
        return n
    t = (cap // LANE) * LANE
    while t >= LANE:
        if n % t == 0:
            return t
        t -= LANE
    return n


def _mm(a, b, *, ta=False, tb=False, out_dtype=F32, name, ride=None, slabs=False):
    m, k = (a.shape[1], a.shape[0]) if ta else a.shape
    n = b.shape[0] if tb else b.shape[1]
    assert k == (b.shape[1] if tb else b.shape[0])
    wide = max(a.dtype.itemsize, b.dtype.itemsize) > 2
    tm, tn, tk = _tile(m, 1024), _tile(n, 1024), _tile(k, MM_TILE_BYTES // (1024 * (4 if wide else 2)))
    if slabs:
        tn = n // N_DEV
        assert tn % LANE == 0
    nk = k // tk

    def product(a_ref, b_ref):
        return _dot(a_ref[...].astype(BF16), b_ref[...].astype(BF16), 0 if ta else 1, 1 if tb else 0)

    def store(o_ref, val):
        if slabs:
            o_ref[0] = val.astype(o_ref.dtype)
        else:
            o_ref[...] = val.astype(o_ref.dtype)

    def body_one(a_ref, b_ref, o_ref):
        store(o_ref, product(a_ref, b_ref))

    def body_acc(a_ref, b_ref, o_ref, acc_ref):
        kk = pl.program_id(2)

        @pl.when(kk == 0)
        def _():
            acc_ref[...] = jnp.zeros_like(acc_ref)

        acc_ref[...] += product(a_ref, b_ref)

        @pl.when(kk == nk - 1)
        def _():
            store(o_ref, acc_ref[...])

    a_spec = (pl.BlockSpec((tk, tm), lambda i, j, kk: (kk, i)) if ta
              else pl.BlockSpec((tm, tk), lambda i, j, kk: (i, kk)))
    b_spec = (pl.BlockSpec((tn, tk), lambda i, j, kk: (j, kk)) if tb
              else pl.BlockSpec((tk, tn), lambda i, j, kk: (kk, j)))
    (out,), rode = _call(
        body_one if nk == 1 else body_acc, name=name, grid=(m // tm, n // tn, nk), in_specs=[a_spec, b_spec],
        out_specs=[pl.BlockSpec((1, tm, tn), lambda i, j, kk: (j, i, 0)) if slabs
                   else pl.BlockSpec((tm, tn), lambda i, j, kk: (i, j))],
        out_shape=[jax.ShapeDtypeStruct((N_DEV, m, tn) if slabs else (m, n), out_dtype)],
        scratch_shapes=[] if nk == 1 else [pltpu.VMEM((tm, tn), F32)],
        sem=("parallel", "parallel", "arbitrary"), args=(a, b), ride=ride)
    return out if ride is None else (out, rode)


def _rowwise(fn, rows, vecs, outs, sums=(), *, tile, name):
    t = rows[0][0].shape[0]
    tile = min(tile, t)
    assert t % tile == 0
    n_rows, n_vecs, n_outs = len(rows), len(vecs), len(outs)

    def body(*refs):
        ins = [r[...].astype(F32) for r in refs[:n_rows + n_vecs]]
        res = fn(*ins)
        if not isinstance(res, (tuple, list)):
            res = (res,)
        out_refs = refs[n_rows + n_vecs:]
        for r, val in zip(out_refs[:n_outs], res[:n_outs]):
            r[...] = val.astype(r.dtype)
        if sums:
            first = pl.program_id(0) == 0
            for r, val in zip(out_refs[n_outs:], res[n_outs:]):
                @pl.when(first)
                def _(r=r):
                    r[...] = jnp.zeros_like(r)
                r[...] += val

    in_specs = [pl.BlockSpec((tile, w), lambda i, cb=cb: (i, cb)) for (_, w, cb) in rows]
    in_specs += [pl.BlockSpec(v.shape, lambda i: (0, 0)) for v in vecs]
    out_specs = [pl.BlockSpec((tile, w), lambda i: (i, 0)) for (w, _) in outs]
    out_specs += [pl.BlockSpec((1, w), lambda i: (0, 0)) for w in sums]
    out_shape = [jax.ShapeDtypeStruct((t, w), dt) for (w, dt) in outs]
    out_shape += [jax.ShapeDtypeStruct((1, w), F32) for w in sums]
    res = pl.pallas_call(
        body, name=name, grid=(t // tile,), in_specs=in_specs, out_specs=out_specs, out_shape=out_shape,
        compiler_params=_cparams(("arbitrary",)),
    )(*[r[0] for r in rows], *vecs)
    return res


def _rstd(x):
    return lax.rsqrt(jnp.mean(x * x, axis=-1, keepdims=True) + NORM_EPS)


def _sigmoid(x):
    return 1.0 / (1.0 + jnp.exp(-x))


def _rms_bwd(dxh, xh, r):
    return r * (dxh - xh * jnp.mean(dxh * xh, axis=-1, keepdims=True))


def _rope(t, tab, sign):
    cosf, sin_a, sin_b = tab[:, :LANE], tab[:, LANE:2 * LANE], tab[:, 2 * LANE:]
    return t * cosf + sign * (pltpu.roll(t, 96, 1) * sin_a + pltpu.roll(t, 32, 1) * sin_b)


def _split3(x):
    hi = x.astype(BF16)
    r1 = x - hi.astype(F32)
    mid = r1.astype(BF16)
    lo = (r1 - mid.astype(F32)).astype(BF16)
    return hi, mid, lo


def _tri_sum(tri_bf16, x):
    hi, mid, lo = _split3(x)
    return _dot(tri_bf16, hi) + _dot(tri_bf16, mid) + _dot(tri_bf16, lo)


def _dot_nt2(a, b):
    a_hi = a.astype(BF16)
    a_lo = (a - a_hi.astype(F32)).astype(BF16)
    b_hi = b.astype(BF16)
    b_lo = (b - b_hi.astype(F32)).astype(BF16)
    return _dot(a_hi, b_hi, 1, 1) + _dot(a_hi, b_lo, 1, 1) + _dot(a_lo, b_hi, 1, 1)


GLA_PAIR = 2


def _gla_specs(t, rows, lay, reverse):
    nb = t // rows
    blk = (lambda i: nb - 1 - i) if reverse else (lambda i: i)
    wk, wv = GLA_PAIR * GLA_DK, GLA_PAIR * GLA_DV
    qb, kb, vb = lay["q"] // wk, lay["k"] // wk, lay["v"] // wv
    return [
        pl.BlockSpec((rows, wk), lambda g, i: (blk(i), qb + g)),
        pl.BlockSpec((rows, wk), lambda g, i: (blk(i), kb + g)),
        pl.BlockSpec((rows, wv), lambda g, i: (blk(i), vb + g)),
        pl.BlockSpec((rows, wk), lambda g, i: (blk(i), g)),
    ], blk


def _gla_fwd(p, la, lay):
    t = p.shape[0]
    rows = min(512, t)
    nb, nc = t // rows, rows // GLA_CHUNK
    c64 = GLA_CHUNK

    def body(q_ref, k_ref, v_ref, la_ref, o_ref, st_ref, s_ref):
        @pl.when(pl.program_id(1) == 0)
        def _():
            s_ref[...] = jnp.zeros_like(s_ref)

        r = lax.broadcasted_iota(jnp.int32, (c64, c64), 0)
        cc = lax.broadcasted_iota(jnp.int32, (c64, c64), 1)
        tril = cc <= r
        tril_b = tril.astype(BF16)
        state = [s_ref[hh] for hh in range(GLA_PAIR)]
        for c in range(nc):
            sl = pl.ds(c * c64, c64)
            for hh in range(GLA_PAIR):
                lk, lv = pl.ds(hh * GLA_DK, GLA_DK), pl.ds(hh * GLA_DV, GLA_DV)
                b = _tri_sum(tril_b, la_ref[sl, lk])
                b_last = b[c64 - 1:c64, :]
                q = q_ref[sl, lk].astype(F32) * GLA_QSCALE
                k = k_ref[sl, lk].astype(F32)
                v = v_ref[sl, lv].astype(BF16)
                qt_f = q * jnp.exp(b)
                qt = qt_f.astype(BF16)
                kh = (k * jnp.exp(b_last - b)).astype(BF16)
                s_prev = state[hh]
                st_ref[hh, c] = s_prev
                att = jnp.where(tril, _dot_nt2(qt_f, k * jnp.exp(-b)), 0.0)
                o_ref[sl, lv] = _dot(qt, s_prev.astype(BF16), 1, 1) + _dot(att.astype(BF16), v)
                state[hh] = s_prev * jnp.exp(b_last) + _dot(v, kh, 0, 0)
        for hh in range(GLA_PAIR):
            s_ref[hh] = state[hh]

    in_specs, _ = _gla_specs(t, rows, lay, False)
    return pl.pallas_call(
        body, name="gla_fwd", grid=(GLA_HEADS // GLA_PAIR, nb), in_specs=in_specs,
        out_specs=[pl.BlockSpec((rows, GLA_PAIR * GLA_DV), lambda g, i: (i, g)),
                   pl.BlockSpec((GLA_PAIR, nc, GLA_DV, GLA_DK), lambda g, i: (g, i, 0, 0))],
        out_shape=[jax.ShapeDtypeStruct((t, GLA_HEADS * GLA_DV), F32),
                   jax.ShapeDtypeStruct((GLA_HEADS, t // c64, GLA_DV, GLA_DK), F32)],
        scratch_shapes=[pltpu.VMEM((GLA_PAIR, GLA_DV, GLA_DK), F32)],
        compiler_params=_cparams(("parallel", "arbitrary")),
    )(p, p, p, la)


def _gla_bwd(p, la, do, states, lay):
    t = p.shape[0]
    rows = min(512, t)
    nb, nc = t // rows, rows // GLA_CHUNK
    c64 = GLA_CHUNK

    def body(q_ref, k_ref, v_ref, la_ref, do_ref, st_ref, dq_ref, dk_ref, dv_ref, dla_ref, ds_ref):
        @pl.when(pl.program_id(1) == 0)
        def _():
            ds_ref[...] = jnp.zeros_like(ds_ref)

        r = lax.broadcasted_iota(jnp.int32, (c64, c64), 0)
        cc = lax.broadcasted_iota(jnp.int32, (c64, c64), 1)
        tril = cc <= r
        tril_b = tril.astype(BF16)
        triu_b = (cc >= r).astype(BF16)
        dstate = [ds_ref[hh] for hh in range(GLA_PAIR)]
        for c in reversed(range(nc)):
            sl = pl.ds(c * c64, c64)
            for hh in range(GLA_PAIR):
                lk, lv = pl.ds(hh * GLA_DK, GLA_DK), pl.ds(hh * GLA_DV, GLA_DV)
                b = _tri_sum(tril_b, la_ref[sl, lk])
                b_last = b[c64 - 1:c64, :]
                eb, enb, ebl_b, ebl = jnp.exp(b), jnp.exp(-b), jnp.exp(b_last - b), jnp.exp(b_last)
                k = k_ref[sl, lk].astype(F32)
                qt_f = q_ref[sl, lk].astype(F32) * GLA_QSCALE * eb
                kt_f = k * enb
                kh_f = k * ebl_b
                qt, kt, kh = qt_f.astype(BF16), kt_f.astype(BF16), kh_f.astype(BF16)
                v_f = v_ref[sl, lv].astype(F32)
                dout_f = do_ref[sl, lv]
                v, dout = v_f.astype(BF16), dout_f.astype(BF16)
                s_prev = st_ref[hh, c]
                ds_next = dstate[hh]
                ds_next_b = ds_next.astype(BF16)
                att = jnp.where(tril, _dot_nt2(qt_f, kt_f), 0.0).astype(BF16)
                datt = jnp.where(tril, _dot_nt2(dout_f, v_f), 0.0).astype(BF16)
                dqt = _dot(dout, s_prev.astype(BF16)) + _dot(datt, kt)
                dkt = _dot(datt, qt, 0, 0)
                dv = _dot(att, dout, 0, 0) + _dot(kh, ds_next_b, 1, 1)
                dkh = _dot(v, ds_next_b)
                d_ebl = jnp.sum(ds_next * s_prev, axis=0, keepdims=True)
                dstate[hh] = ds_next * ebl + _dot(dout, qt, 0, 0)
                db = dqt * qt_f - dkt * kt_f - dkh * kh_f
                db_last = ebl * d_ebl + jnp.sum(dkh * kh_f, axis=0, keepdims=True)
                dq_ref[sl, lk] = (dqt * eb * GLA_QSCALE).astype(dq_ref.dtype)
                dk_ref[sl, lk] = (dkt * enb + dkh * ebl_b).astype(dk_ref.dtype)
                dv_ref[sl, lv] = dv.astype(dv_ref.dtype)
                dla_ref[sl, lk] = _tri_sum(triu_b, db) + db_last
        for hh in range(GLA_PAIR):
            ds_ref[hh] = dstate[hh]

    in_specs, blk = _gla_specs(t, rows, lay, True)
    wk, wv = GLA_PAIR * GLA_DK, GLA_PAIR * GLA_DV
    in_specs += [pl.BlockSpec((rows, wv), lambda g, i: (blk(i), g)),
                 pl.BlockSpec((GLA_PAIR, nc, GLA_DV, GLA_DK), lambda g, i: (g, blk(i), 0, 0))]
    dk_spec = pl.BlockSpec((rows, wk), lambda g, i: (blk(i), g))
    return pl.pallas_call(
        body, name="gla_bwd", grid=(GLA_HEADS // GLA_PAIR, nb), in_specs=in_specs,
        out_specs=[dk_spec, dk_spec, pl.BlockSpec((rows, wv), lambda g, i: (blk(i), g)), dk_spec],
        out_shape=[jax.ShapeDtypeStruct((t, GLA_HEADS * GLA_DK), BF16),
                   jax.ShapeDtypeStruct((t, GLA_HEADS * GLA_DK), BF16),
                   jax.ShapeDtypeStruct((t, GLA_HEADS * GLA_DV), BF16),
                   jax.ShapeDtypeStruct((t, GLA_HEADS * GLA_DK), F32)],
        scratch_shapes=[pltpu.VMEM((GLA_PAIR, GLA_DV, GLA_DK), F32)],
        compiler_params=_cparams(("parallel", "arbitrary")),
    )(p, p, p, la, do, states)


def _diag_mask(rows, cols, row0):
    row = row0 + lax.broadcasted_iota(jnp.int32, (rows, cols), 0)
    col = lax.broadcasted_iota(jnp.int32, (rows, cols), 1)
    return col <= row


QK_SPARE = MLA_NOPE + MLA_ROPE
N_SPARE = 3


def _with_spare(x, col, lane0):
    lane = lax.broadcasted_iota(jnp.int32, x.shape, 1)
    for n, term in enumerate(_split3(col)):
        x = jnp.where(lane == lane0 + n, term, x)
    return x


def _spare_ones(shape, lane0):
    lane = lax.broadcasted_iota(jnp.int32, shape, 1)
    return ((lane >= lane0) & (lane < lane0 + N_SPARE)).astype(F32)


def _flash_tiles(t):
    tq = min(1024, t)
    halves = 2 if tq % 32 == 0 else 1
    return tq, t // tq, halves, tq // halves


def _flash_fwd(q, k, vx, ride=None):
    t = q.shape[0]
    tq, nq, halves, hr = _flash_tiles(t)
    dqk, dv = MLA_QK_PAD, MLA_V

    def body(q_ref, k_ref, v_ref, o_ref, qa_ref, m_ref, acc_ref, s_ref):
        i = pl.program_id(1)
        m_ref[...] = jnp.full_like(m_ref, NEG)
        acc_ref[...] = jnp.zeros_like(acc_ref)

        def scores(j, slot):
            kb = k_ref[pl.ds(pl.multiple_of(j * tq, tq), tq), :]
            for hh in range(halves):
                s_ref[slot, pl.ds(hh * hr, hr), :] = _dot(q_ref[pl.ds(hh * hr, hr), :], kb, 1, 1)

        def consume(j, slot, masked):
            vb = v_ref[pl.ds(pl.multiple_of(j * tq, tq), tq), :]
            for hh in range(halves):
                rs = pl.ds(hh * hr, hr)
                s = s_ref[slot, rs, :]
                if masked:
                    s = jnp.where(_diag_mask(hr, tq, hh * hr), s, NEG)
                m_old = m_ref[rs, :]
                m_new = jnp.maximum(m_old, jnp.max(s, axis=1, keepdims=True))
                pr = jnp.exp2(s - m_new)
                acc_ref[rs, :] = jnp.exp2(m_old - m_new) * acc_ref[rs, :] + _dot(pr.astype(BF16), vb)
                m_ref[rs, :] = m_new

        def two_blocks(jj, carry):
            scores(2 * jj + 1, 1)
            consume(2 * jj, 0, False)
            scores(2 * jj + 2, 0)
            consume(2 * jj + 1, 1, False)
            return carry

        scores(0, 0)
        lax.fori_loop(0, i // 2, two_blocks, 0)

        @pl.when(i % 2 == 0)
        def _():
            consume(i, 0, True)

        @pl.when(i % 2 == 1)
        def _():
            scores(i, 1)
            consume(i - 1, 0, False)
            consume(i, 1, True)

        acc = acc_ref[...]
        l = acc[:, dv:dv + 1]
        o_ref[...] = acc[:, :dv] / l
        qa_ref[...] = _with_spare(q_ref[...], -(m_ref[...] + jnp.log(l) * LOG2E), QK_SPARE)

    outs, rode = _call(
        body, name="mla_flash_fwd", grid=(MLA_HEADS, nq),
        in_specs=[pl.BlockSpec((tq, dqk), lambda h, i: (i, h)),
                  pl.BlockSpec((t, dqk), lambda h, i: (0, h)),
                  pl.BlockSpec((t, 2 * dv), lambda h, i: (0, h))],
        out_specs=[pl.BlockSpec((tq, dv), lambda h, i: (i, h)),
                   pl.BlockSpec((tq, dqk), lambda h, i: (i, h))],
        out_shape=[jax.ShapeDtypeStruct((t, MLA_HEADS * dv), F32),
                   jax.ShapeDtypeStruct((t, MLA_HEADS * dqk), BF16)],
        scratch_shapes=[pltpu.VMEM((tq, 1), F32), pltpu.VMEM((tq, 2 * dv), F32), pltpu.VMEM((2, tq, tq), F32)],
        sem=("parallel", "arbitrary"), args=(q, k, vx), ride=ride)
    return tuple(outs) if ride is None else (tuple(outs), rode)


def _flash_bwd(qa, k, vx, doa, ride=None):
    t = qa.shape[0]
    tq, nq, halves, hr = _flash_tiles(t)
    dqk, dv = MLA_QK_PAD, MLA_V

    def body(k_ref, v_ref, q_ref, do_ref, dq_ref, dk_ref, dv_ref, dq_acc, dk_acc, dv_acc):
        j = pl.program_id(1)

        @pl.when(j == 0)
        def _():
            dq_acc[...] = jnp.zeros_like(dq_acc)

        kb = k_ref[...]
        vb = v_ref[...]
        dk_acc[...] = jnp.zeros_like(dk_acc)
        dv_acc[...] = jnp.zeros_like(dv_acc)

        def step(i, masked):
            for hh in range(halves):
                rs = pl.ds(pl.multiple_of(i * tq + hh * hr, hr), hr)
                qb = q_ref[rs, :]
                dout = do_ref[rs, :]
                nc = (hh + 1) * hr if masked else tq
                s = _dot(qb, kb[:nc], 1, 1)
                if masked:
                    s = jnp.where(_diag_mask(hr, nc, hh * hr), s, NEG)
                pr = jnp.exp2(s)
                ds = (pr * _dot(dout, vb[:nc], 1, 1)).astype(BF16)
                dv_acc[pl.ds(0, nc), :] += _dot(pr.astype(BF16), dout, 0, 0)
                dk_acc[pl.ds(0, nc), :] += _dot(ds, qb, 0, 0)
                dq_acc[rs, :] += _dot(ds, kb[:nc])

        def loop_body(i, carry):
            step(i, False)
            return carry

        step(j, True)
        lax.fori_loop(j + 1, nq, loop_body, 0)
        dk_ref[...] = (dk_acc[...] * LN2).astype(dk_ref.dtype)
        dv_ref[...] = dv_acc[:, :dv].astype(dv_ref.dtype)

        @pl.when(j == nq - 1)
        def _():
            dq_ref[...] = (dq_acc[...] * ATT_SCALE).astype(dq_ref.dtype)

    outs, rode = _call(
        body, name="mla_flash_bwd", grid=(MLA_HEADS, nq),
        in_specs=[pl.BlockSpec((tq, dqk), lambda h, j: (j, h)),
                  pl.BlockSpec((tq, 2 * dv), lambda h, j: (j, h)),
                  pl.BlockSpec((t, dqk), lambda h, j: (0, h)),
                  pl.BlockSpec((t, 2 * dv), lambda h, j: (0, h))],
        out_specs=[pl.BlockSpec((t, dqk), lambda h, j: (0, h)),
                   pl.BlockSpec((tq, dqk), lambda h, j: (j, h)),
                   pl.BlockSpec((tq, dv), lambda h, j: (j, h))],
        out_shape=[jax.ShapeDtypeStruct((t, MLA_HEADS * dqk), BF16),
                   jax.ShapeDtypeStruct((t, MLA_HEADS * dqk), BF16),
                   jax.ShapeDtypeStruct((t, MLA_HEADS * dv), BF16)],
        scratch_shapes=[pltpu.VMEM((t, dqk), F32), pltpu.VMEM((tq, dqk), F32), pltpu.VMEM((tq, 2 * dv), F32)],
        sem=("parallel", "arbitrary"), args=(k, vx, qa, doa), ride=ride, vmem_limit=FLASH_BWD_VMEM)
    return tuple(outs) if ride is None else (tuple(outs), rode)


class _NoRides:
    def ride(self, stage, grads):
        return None

    def done(self, stage, rode, w):
        pass


def _local_step(x, target, tab, mod8, w, rides=None):
    t, d = x.shape
    rides = rides or _NoRides()
    big = {}

    def riding(stage, fn):
        r = rides.ride(stage, big)
        res = fn(r)
        if r is None:
            return res
        rides.done(stage, res[1], w)
        return res[0]
    _, pw, _, lay, _ = _in_layout(d)
    ffn = ((8 * d // 3 + 255) // 256) * 256

    def blk(arr, name):
        return (arr, pw[name], lay[name] // pw[name])

    def full(arr):
        return (arr, arr.shape[1], 0)

    g1, g2, g3 = w["norm_mix_g"], w["norm_ffn_g"], w["final_norm_g"]

    def f_ln1(xv, mod, g):
        return (xv * _rstd(xv) * g) * (1.0 + mod[1:2]) + mod[0:1]

    (h,) = _rowwise(f_ln1, [full(x)], [mod8, g1], [(d, BF16)], tile=512, name="ln1_modulate")
    p = riding("in_proj", lambda r: _mm(h, w["w_in"], name="mm_in_proj", ride=r))

    def f_gk(pgk, gkw, gkb):
        z = _dot(pgk.astype(BF16), gkw.astype(BF16)) + gkb
        return (jnp.minimum(z, 0.0) - jnp.log(1.0 + jnp.exp(-jnp.abs(z)))) / GLA_GATE_NORMALIZER

    (la,) = _rowwise(f_gk, [blk(p, "gk")], [w["gla_gk_w"], w["gla_gk_b"]], [(GLA_HEADS * GLA_DK, F32)],
                     tile=512, name="gla_gate")
    o_gla, states = _gla_fwd(p, la, lay)

    def f_gla_out(ov, pg, g):
        parts = []
        for hh in range(GLA_HEADS):
            oh = ov[:, hh * GLA_DV:(hh + 1) * GLA_DV]
            ph = pg[:, hh * GLA_DV:(hh + 1) * GLA_DV]
            parts.append(oh * _rstd(oh) * g * (ph * _sigmoid(ph)))
        return jnp.concatenate(parts, axis=1)

    (o_n,) = _rowwise(f_gla_out, [full(o_gla), blk(p, "g")], [w["gla_onorm_g"]], [(d, BF16)], tile=512,
                      name="gla_out_norm")
    y_gla = _mm(o_n, w["gla_wo"], out_dtype=BF16, name="mm_gla_wo")

    def f_mla_prep(cq, ckv, kr, tb, gq, gkv):
        return cq * _rstd(cq) * gq, ckv * _rstd(ckv) * gkv, _rope(kr, tb, 1.0)

    cqn, ckvn, krr = _rowwise(f_mla_prep, [blk(p, "cq"), blk(p, "ckv"), blk(p, "kr"), full(tab)],
                              [w["mla_q_norm_g"], w["mla_kv_norm_g"]],
                              [(pw["cq"], BF16), (pw["ckv"], BF16), (LANE, F32)], tile=512, name="mla_prep")
    qlat = _mm(cqn, w["mla_wuq"], out_dtype=BF16, name="mm_mla_wuq")
    kvl = _mm(ckvn, w["mla_wukv"], out_dtype=BF16, name="mm_mla_wukv")
    hv = MLA_HEADS * MLA_V

    def f_qkv(ql, kn, vv, kr, tb):
        qs, ks, vx = [], [], []
        kr1 = kr + _spare_ones(kr.shape, MLA_ROPE)
        ones = _spare_ones(kr.shape, 0)
        for hh in range(MLA_HEADS):
            o0 = hh * MLA_QK_PAD
            qs += [ql[:, o0:o0 + LANE], _rope(ql[:, o0 + LANE:o0 + 2 * LANE], tb, 1.0)]
            ks += [kn[:, hh * LANE:(hh + 1) * LANE], kr1]
            vx += [vv[:, hh * MLA_V:(hh + 1) * MLA_V], ones]
        return (jnp.concatenate(qs, axis=1) * (ATT_SCALE * LOG2E), jnp.concatenate(ks, axis=1),
                jnp.concatenate(vx, axis=1))

    qa, ka, vxa = _rowwise(f_qkv, [full(qlat), (kvl, hv, 0), (kvl, hv, 1), full(krr), full(tab)], [],
                           [(MLA_HEADS * MLA_QK_PAD, BF16), (MLA_HEADS * MLA_QK_PAD, BF16), (2 * hv, BF16)],
                           tile=256, name="mla_qkv_build")
    o_mla, qa_lse = riding("flash_fwd", lambda r: _flash_fwd(qa, ka, vxa, ride=r))
    y_mla = _mm(o_mla, w["mla_wo"], out_dtype=BF16, name="mm_mla_wo")

    def f_merge(yg, ym, ga, gb):
        return _sigmoid(ga) * yg + _sigmoid(gb) * ym

    (merged,) = _rowwise(f_merge, [full(y_gla), full(y_mla), blk(p, "ga"), blk(p, "gb")], [], [(d, BF16)],
                         tile=512, name="merge")
    mix = _mm(merged, w["w_out"], name="mm_w_out")

    def f_res_ln2(xv, mx, mod, g):
        x2v = xv + mod[2:3] * mx
        return x2v, (x2v * _rstd(x2v) * g) * (1.0 + mod[4:5]) + mod[3:4]

    x2, h2 = _rowwise(f_res_ln2, [full(x), full(mix)], [mod8, g2], [(d, F32), (d, BF16)], tile=512,
                      name="res_ln2_modulate")
    gu = _mm(h2, w["ffn_w_in"], out_dtype=BF16, name="mm_ffn_in")

    def f_swiglu(gv, uv):
        return gv * _sigmoid(gv) * uv

    (act,) = _rowwise(f_swiglu, [(gu, ffn, 0), (gu, ffn, 1)], [], [(ffn, BF16)], tile=256, name="swiglu")
    f_out = _mm(act, w["ffn_w_down"], name="mm_ffn_down")

    def f_head(x2v, fv, tg, mod, g):
        x3 = x2v + mod[5:6] * fv
        r = _rstd(x3)
        xh = x3 * r
        e = xh * g - tg
        loss_rows = 0.5 * jnp.mean(e * e, axis=-1, keepdims=True)
        dy = e * (1.0 / d)
        dx3 = _rms_bwd(dy * g, xh, r)
        loss = jnp.broadcast_to(jnp.sum(loss_rows, axis=0, keepdims=True), (1, LANE))
        return (dx3, dx3 * mod[5:6], loss, jnp.sum(dy * xh, axis=0, keepdims=True),
                jnp.sum(dx3 * fv, axis=0, keepdims=True))

    dx3, df, loss_v, dg3, dgate_f = _rowwise(f_head, [full(x2), full(f_out), full(target)], [mod8, g3],
                                             [(d, F32), (d, BF16)], [LANE, d, d], tile=256, name="loss_head")
    da = _mm(df, w["ffn_w_down"], tb=True, out_dtype=BF16, name="mm_ffn_down_dx")
    big["ffn_w_down"] = _mm(act, df, ta=True, out_dtype=BF16, name="mm_ffn_down_dw")

    def f_swiglu_bwd(gv, uv, dav):
        sg = _sigmoid(gv)
        return jnp.concatenate([dav * uv * (sg * (1.0 + gv * (1.0 - sg))), dav * (gv * sg)], axis=1)

    (dgu,) = _rowwise(f_swiglu_bwd, [(gu, ffn, 0), (gu, ffn, 1), full(da)], [], [(2 * ffn, BF16)], tile=256,
                      name="swiglu_bwd")
    dh2 = _mm(dgu, w["ffn_w_in"], tb=True, name="mm_ffn_in_dx")
    big["ffn_w_in"] = _mm(h2, dgu, ta=True, out_dtype=BF16, slabs=True, name="mm_ffn_in_dw")

    def f_ln2_bwd(x2v, dh, dx3v, mx, mod, g):
        r = _rstd(x2v)
        xh = x2v * r
        dn = dh * (1.0 + mod[4:5])
        dx2 = dx3v + _rms_bwd(dn * g, xh, r)
        return (dx2, dx2 * mod[2:3],
                jnp.sum(dh * (xh * g), axis=0, keepdims=True), jnp.sum(dh, axis=0, keepdims=True),
                jnp.sum(dn * xh, axis=0, keepdims=True), jnp.sum(dx2 * mx, axis=0, keepdims=True))

    dx2, dmix, dscale_f, dshift_f, dg2, dgate_m = _rowwise(
        f_ln2_bwd, [full(x2), full(dh2), full(dx3), full(mix)], [mod8, g2], [(d, F32), (d, BF16)],
        [d, d, d, d], tile=256, name="ln2_bwd")
    dmerged = _mm(dmix, w["w_out"], tb=True, out_dtype=BF16, name="mm_w_out_dx")
    big["w_out"] = _mm(merged, dmix, ta=True, out_dtype=BF16, name="mm_w_out_dw")

    def f_merge_bwd(dm, yg, ym, ga, gb):
        sa, sb = _sigmoid(ga), _sigmoid(gb)
        return dm * sa, dm * sb, dm * yg * sa * (1.0 - sa), dm * ym * sb * (1.0 - sb)

    dy_gla, dy_mla, dp_ga, dp_gb = _rowwise(
        f_merge_bwd, [full(dmerged), full(y_gla), full(y_mla), blk(p, "ga"), blk(p, "gb")], [],
        [(d, BF16)] * 4, tile=256, name="merge_bwd")
    do_n = _mm(dy_gla, w["gla_wo"], tb=True, name="mm_gla_wo_dx")
    big["gla_wo"] = _mm(o_n, dy_gla, ta=True, out_dtype=BF16, name="mm_gla_wo_dw")
    do_m = _mm(dy_mla, w["mla_wo"], tb=True, out_dtype=BF16, name="mm_mla_wo_dx")
    big["mla_wo"] = _mm(o_mla, dy_mla, ta=True, out_dtype=BF16, name="mm_mla_wo_dw")

    def f_gla_out_bwd(don, ov, pg, g):
        dos, dpgs = [], []
        dg = jnp.zeros((1, GLA_DV), F32)
        for hh in range(GLA_HEADS):
            sl = slice(hh * GLA_DV, (hh + 1) * GLA_DV)
            oh, ph, dn = ov[:, sl], pg[:, sl], don[:, sl]
            r = _rstd(oh)
            xh = oh * r
            sg = _sigmoid(ph)
            dpre = dn * (ph * sg)
            dg = dg + jnp.sum(dpre * xh, axis=0, keepdims=True)
            dos.append(_rms_bwd(dpre * g, xh, r))
            dpgs.append(dn * (xh * g) * (sg * (1.0 + ph * (1.0 - sg))))
        return jnp.concatenate(dos, axis=1), jnp.concatenate(dpgs, axis=1), dg

    do_gla, dp_g, dg_on = _rowwise(f_gla_out_bwd, [full(do_n), full(o_gla), blk(p, "g")], [w["gla_onorm_g"]],
                                   [(d, F32), (d, BF16)], [GLA_DV], tile=256, name="gla_out_norm_bwd")
    dp_q, dp_k, dp_v, dla = _gla_bwd(p, la, do_gla, states, lay)

    def f_gk_bwd(dlav, pgk, gkw, gkb):
        z = _dot(pgk.astype(BF16), gkw.astype(BF16)) + gkb
        dz = dlav * (1.0 / GLA_GATE_NORMALIZER) * _sigmoid(-z)
        return dz, _dot(dz.astype(BF16), gkw.astype(BF16), 1, 1), jnp.sum(dz, axis=0, keepdims=True)

    dz, dp_gk, dgk_b = _rowwise(f_gk_bwd, [full(dla), blk(p, "gk")], [w["gla_gk_w"], w["gla_gk_b"]],
                                [(GLA_HEADS * GLA_DK, BF16), (LANE, BF16)], [GLA_HEADS * GLA_DK], tile=512,
                                name="gla_gate_bwd")
    p_gk = lax.slice_in_dim(p, lay["gk"], lay["gk"] + LANE, axis=1)
    big["gla_gk_w"] = _mm(p_gk, dz, ta=True, name="mm_gla_gk_dw")[:GLA_GATE_RANK]

    def f_do_aug(dom, om):
        parts = []
        for hh in range(MLA_HEADS):
            dh_ = dom[:, hh * MLA_V:(hh + 1) * MLA_V]
            delta = jnp.sum(dh_ * om[:, hh * MLA_V:(hh + 1) * MLA_V], axis=1, keepdims=True)
            parts += [dh_.astype(BF16), _with_spare(jnp.zeros(dh_.shape, BF16), -delta, 0)]
        return jnp.concatenate(parts, axis=1)

    (doa,) = _rowwise(f_do_aug, [full(do_m), full(o_mla)], [], [(2 * hv, BF16)], tile=256, name="mla_do_delta")
    dqa, dka, dva = riding("flash_bwd", lambda r: _flash_bwd(qa_lse, ka, vxa, doa, ride=r))

    def f_qkv_bwd(dq, dk, dvv, tb):
        dqs, dkn = [], []
        dkr = jnp.zeros((dq.shape[0], LANE), F32)
        for hh in range(MLA_HEADS):
            o0 = hh * MLA_QK_PAD
            dqs += [dq[:, o0:o0 + LANE], _rope(dq[:, o0 + LANE:o0 + 2 * LANE], tb, -1.0)]
            dkn.append(dk[:, o0:o0 + LANE])
            dkr = dkr + dk[:, o0 + LANE:o0 + 2 * LANE]
        return jnp.concatenate(dqs, axis=1), jnp.concatenate(dkn + [dvv], axis=1), dkr

    dqlat, dkvl, dkrr = _rowwise(f_qkv_bwd, [full(dqa), full(dka), full(dva), full(tab)], [],
                                 [(MLA_HEADS * MLA_QK_PAD, BF16), (2 * hv, BF16), (LANE, F32)], tile=256,
                                 name="mla_qkv_build_bwd")
    dcqn = _mm(dqlat, w["mla_wuq"], tb=True, name="mm_mla_wuq_dx")
    big["mla_wuq"] = _mm(cqn, dqlat, ta=True, out_dtype=BF16, name="mm_mla_wuq_dw")
    dckvn = _mm(dkvl, w["mla_wukv"], tb=True, name="mm_mla_wukv_dx")
    big["mla_wukv"] = _mm(ckvn, dkvl, ta=True, out_dtype=BF16, name="mm_mla_wukv_dw")

    def f_mla_prep_bwd(dq, dkv, dkr, cq, ckv, tb, gq, gkv):
        rq, rk = _rstd(cq), _rstd(ckv)
        xq, xk = cq * rq, ckv * rk
        return (_rms_bwd(dq * gq, xq, rq), _rms_bwd(dkv * gkv, xk, rk), _rope(dkr, tb, -1.0),
                jnp.sum(dq * xq, axis=0, keepdims=True), jnp.sum(dkv * xk, axis=0, keepdims=True))

    dp_cq, dp_ckv, dp_kr, dg_q, dg_kv = _rowwise(
        f_mla_prep_bwd, [full(dcqn), full(dckvn), full(dkrr), blk(p, "cq"), blk(p, "ckv"), full(tab)],
        [w["mla_q_norm_g"], w["mla_kv_norm_g"]], [(pw["cq"], BF16), (pw["ckv"], BF16), (LANE, BF16)],
        [pw["cq"], pw["ckv"]], tile=512, name="mla_prep_bwd")

    pieces = dict(v=dp_v, g=dp_g, ga=dp_ga, gb=dp_gb, q=dp_q, k=dp_k, cq=dp_cq, ckv=dp_ckv, gk=dp_gk, kr=dp_kr)
    dp = jnp.concatenate([pieces[n] for n in MY_ORDER], axis=1)
    big["w_in"] = riding("in_proj_dw", lambda r: _mm(dp, h, ta=True, out_dtype=BF16, name="mm_in_proj_dw", ride=r))
    dh = riding("in_proj_dx", lambda r: _mm(dp, w["w_in"], tb=True, name="mm_in_proj_dx", ride=r))

    def f_ln1_bwd(xv, dhv, dx2v, mod, g):
        r = _rstd(xv)
        xh = xv * r
        dn = dhv * (1.0 + mod[1:2])
        return (dx2v + _rms_bwd(dn * g, xh, r),
                jnp.sum(dhv * (xh * g), axis=0, keepdims=True), jnp.sum(dhv, axis=0, keepdims=True),
                jnp.sum(dn * xh, axis=0, keepdims=True))

    grad_x, dscale_m, dshift_m, dg1 = _rowwise(f_ln1_bwd, [full(x), full(dh), full(dx2)], [mod8, g1],
                                               [(d, F32)], [d, d, d], tile=256, name="ln1_bwd")

    dmod = jnp.concatenate([dshift_m, dscale_m, dgate_m, dshift_f, dscale_f, dgate_f], axis=1)
    small = dict(ada_b=dmod, norm_mix_g=dg1, gla_gk_b=dgk_b, gla_onorm_g=dg_on, mla_q_norm_g=dg_q,
                 mla_kv_norm_g=dg_kv, norm_ffn_g=dg2, final_norm_g=dg3)
    return loss_v[0, 0], grad_x, big, small


N_PEER = N_DEV - 1


def _exchange_copies(ins, outs, sems, scatter):
    send_sems, recv_sems, local_sems = sems
    x, y, c = lax.axis_index("x"), lax.axis_index("y"), lax.axis_index("c")
    me = 4 * x + 2 * y + c
    peers = []
    for rel in range(1, N_DEV):
        px = 1 - x if rel & 4 else x
        py = 1 - y if rel & 2 else y
        pc = 1 - c if rel & 1 else c
        peers.append(((px, py, pc), 4 * px + 2 * py + pc))

    def remote(a, k, src_slot, dst_slot):
        src = ins[a].at[src_slot] if scatter else ins[a]
        return pltpu.make_async_remote_copy(
            src_ref=src, dst_ref=outs[a].at[dst_slot], send_sem=send_sems.at[a * N_PEER + k],
            recv_sem=recv_sems.at[a * N_PEER + k], device_id=peers[k][0], device_id_type=pl.DeviceIdType.MESH)

    local, sends, recvs = [], [], []
    for a in range(len(ins)):
        src = ins[a].at[me] if scatter else ins[a]
        local.append(pltpu.make_async_copy(src, outs[a].at[me], local_sems.at[a]))
        for k in range(N_PEER):
            sends.append(remote(a, k, peers[k][1], me))
            recvs.append(remote(a, k, peers[k][1], peers[k][1]))
    return local, sends, recvs


def _exchange_start(ins, outs, sems, scatter):
    local, sends, _ = _exchange_copies(ins, outs, sems, scatter)
    for cp in local + sends:
        cp.start()


def _exchange_wait(ins, outs, sems, scatter):
    local, sends, recvs = _exchange_copies(ins, outs, sems, scatter)
    for cp in recvs:
        cp.wait_recv()
    for cp in sends:
        cp.wait_send()
    for cp in local:
        cp.wait()


def _exchange_shapes(arrs, scatter):
    n = len(arrs)
    out_shape = [jax.ShapeDtypeStruct(a.shape if scatter else (N_DEV,) + a.shape, a.dtype) for a in arrs]
    sems = [pltpu.SemaphoreType.DMA((n * N_PEER,)), pltpu.SemaphoreType.DMA((n * N_PEER,)),
            pltpu.SemaphoreType.DMA((n,))]
    return out_shape, sems


def _exchange(arrs, *, scatter, name):
    n = len(arrs)

    def body(*refs):
        ins, outs, sems = refs[:n], refs[n:2 * n], refs[2 * n:]
        _exchange_start(ins, outs, sems, scatter)
        _exchange_wait(ins, outs, sems, scatter)

    hbm = pl.BlockSpec(memory_space=pltpu.HBM)
    out_shape, sems = _exchange_shapes(arrs, scatter)
    return pl.pallas_call(body, name=name, in_specs=[hbm] * n, out_specs=[hbm] * n, out_shape=out_shape,
                          scratch_shapes=sems)(*arrs)


def _gather_once_per_chip(arrs, *, name):
    n = len(arrs)

    def body(*refs):
        ins, outs = refs[:n], refs[n:2 * n]
        send_sems, recv_sems, local_sems = refs[2 * n:]
        x, y, c = lax.axis_index("x"), lax.axis_index("y"), lax.axis_index("c")
        sibling = (x, y, 1 - c)
        chips = [(1 - x, y), (x, 1 - y), (1 - x, 1 - y)]

        def slot(px, py, pc):
            return 4 * px + 2 * py + pc

        def copy(a, k, block, to, src=None):
            dst = outs[a].at[slot(*block)]
            return pltpu.make_async_remote_copy(
                src_ref=dst if src is None else src, dst_ref=dst, send_sem=send_sems.at[a * N_PEER + k],
                recv_sem=recv_sems.at[a * N_PEER + k], device_id=to, device_id_type=pl.DeviceIdType.MESH)

        local, sends = [], []
        for a in range(n):
            local.append(pltpu.make_async_copy(ins[a], outs[a].at[slot(x, y, c)], local_sems.at[a]))
            sends.append(copy(a, 0, (x, y, c), sibling, src=ins[a]))
            sends += [copy(a, 1 + j, (x, y, c), (*chip, c), src=ins[a]) for j, chip in enumerate(chips)]
        for cp in local + sends:
            cp.start()
        for a in range(n):
            for j, chip in enumerate(chips):
                copy(a, 1 + j, (*chip, c), (x, y, c)).wait_recv()
                sends.append(copy(a, 4 + j, (*chip, c), sibling))
                sends[-1].start()
        for a in range(n):
            copy(a, 0, sibling, (x, y, c)).wait_recv()
            for j, chip in enumerate(chips):
                copy(a, 4 + j, (*chip, 1 - c), (x, y, c)).wait_recv()
        for cp in sends:
            cp.wait_send()
        for cp in local:
            cp.wait()

    hbm = pl.BlockSpec(memory_space=pltpu.HBM)
    out_shape, sems = _exchange_shapes(arrs, False)
    return pl.pallas_call(body, name=name, in_specs=[hbm] * n, out_specs=[hbm] * n, out_shape=out_shape,
                          scratch_shapes=sems)(*arrs)


def _call(body, *, name, grid, in_specs, out_specs, out_shape, scratch_shapes, sem, args, ride=None,
          vmem_limit=VMEM_LIMIT):
    if ride is None:
        res = pl.pallas_call(body, name=name, grid=grid, in_specs=in_specs, out_specs=out_specs, out_shape=out_shape,
                             scratch_shapes=scratch_shapes, compiler_params=_cparams(sem, vmem_limit))(*args)
        return res, None
    arrs, scatter = ride
    n, n_in, n_out, n_scr = len(arrs), len(in_specs), len(out_specs), len(scratch_shapes)
    x_shape, x_sems = _exchange_shapes(arrs, scatter)

    def hosted(*refs):
        c_in, x_in = refs[:n_in], refs[n_in:n_in + n]
        c_out, x_out = refs[n_in + n:n_in + n + n_out], refs[n_in + n + n_out:n_in + 2 * n + n_out]
        scr = refs[n_in + 2 * n + n_out:]
        c_scr, sems = scr[:n_scr], scr[n_scr:]
        first = functools.reduce(jnp.logical_and, [pl.program_id(a) == 0 for a in range(len(grid))])
        last = functools.reduce(jnp.logical_and, [pl.program_id(a) == grid[a] - 1 for a in range(len(grid))])

        @pl.when(first)
        def _():
            _exchange_start(x_in, x_out, sems, scatter)

        body(*c_in, *c_out, *c_scr)

        @pl.when(last)
        def _():
            _exchange_wait(x_in, x_out, sems, scatter)

    hbm = pl.BlockSpec(memory_space=pltpu.HBM)
    res = pl.pallas_call(
        hosted, name=name, grid=grid, in_specs=list(in_specs) + [hbm] * n, out_specs=list(out_specs) + [hbm] * n,
        out_shape=list(out_shape) + x_shape, scratch_shapes=list(scratch_shapes) + x_sems,
        compiler_params=_cparams(("arbitrary",) * len(grid), vmem_limit))(*args, *arrs)
    return res[:n_out], res[n_out:]


def _adamw_math(w, g, m, v):
    m_new = ADAM_B1 * m + (1.0 - ADAM_B1) * g
    v_new = ADAM_B2 * v + (1.0 - ADAM_B2) * (g * g)
    m_hat = m_new / (1.0 - ADAM_B1 ** ADAM_STEP)
    v_hat = v_new / (1.0 - ADAM_B2 ** ADAM_STEP)
    delta = -ADAM_LR * (m_hat / (jnp.sqrt(v_hat) + ADAM_EPS) + ADAM_WD * w)
    return delta, m_new, v_new


def _adamw_by_columns(w, g, m, v, *, name):
    _, r, c = w.shape
    tc = 256

    def body(w_ref, g_ref, m_ref, v_ref, go_ref, d_ref, mo_ref, vo_ref):
        gv = g_ref[0].astype(F32)
        for s in range(1, N_DEV):
            gv = gv + g_ref[s].astype(F32)
        delta, m_new, v_new = _adamw_math(w_ref[0], gv, m_ref[0], v_ref[0])
        go_ref[0] = gv
        d_ref[0] = delta
        mo_ref[0] = m_new
        vo_ref[0] = v_new

    spec = pl.BlockSpec((1, r, tc), lambda i: (0, 0, i))
    return pl.pallas_call(
        body, name=name, grid=(c // tc,), in_specs=[spec, pl.BlockSpec((N_DEV, r, tc), lambda i: (0, 0, i)), spec, spec],
        out_specs=[spec] * 4, out_shape=[jax.ShapeDtypeStruct((1, r, c), F32)] * 4,
        compiler_params=_cparams(("parallel",)),
    )(w, g, m, v)


def _adamw(w, g, m, v, *, name):
    _, r, c = w.shape
    slots = g.ndim == 3
    if r > 16 and r % 16:
        return _adamw_by_columns(w, g, m, v, name=name)
    tr = r
    for cand in (128, 64, 32, 16):
        if r % cand == 0 and r > cand:
            tr = cand
            break

    def body(w_ref, g_ref, m_ref, v_ref, go_ref, d_ref, mo_ref, vo_ref):
        if slots:
            gv = g_ref[0].astype(F32)
            for s in range(1, N_DEV):
                gv = gv + g_ref[s].astype(F32)
        else:
            gv = g_ref[...]
        delta, m_new, v_new = _adamw_math(w_ref[0], gv, m_ref[0], v_ref[0])
        go_ref[0] = gv
        d_ref[0] = delta
        mo_ref[0] = m_new
        vo_ref[0] = v_new

    spec = pl.BlockSpec((1, tr, c), lambda i: (0, i, 0))
    g_spec = pl.BlockSpec((N_DEV, tr, c), lambda i: (0, i, 0)) if slots else pl.BlockSpec((tr, c), lambda i: (i, 0))
    return pl.pallas_call(
        body, name=name, grid=(r // tr,), in_specs=[spec, g_spec, spec, spec], out_specs=[spec] * 4,
        out_shape=[jax.ShapeDtypeStruct((1, r, c), F32)] * 4, compiler_params=_cparams(("parallel",)),
    )(w, g, m, v)


def _unshard_cols(g):
    return jnp.transpose(g, (1, 0, 2)).reshape(g.shape[1], -1)

def _shard_cols(full):
    r = full.shape[0]
    return jnp.transpose(full.reshape(r, N_DEV, -1), (1, 0, 2))


def _w_in_pieces(d):
    wd, _, ref_off, _, total = _in_layout(d)
    shard = sum(wd.values()) // N_DEV
    out = []
    for n in IN_NAMES:
        r0, r1 = ref_off[n], ref_off[n] + wd[n]
        for s in range(N_DEV):
            lo, hi = max(r0, s * shard), min(r1, (s + 1) * shard)
            if lo < hi:
                out.append((n, s, lo - s * shard, lo - r0, hi - lo))
    return out


def _w_in_gathered_to_mine(g, d):
    wd, pw, _, _, _ = _in_layout(d)
    pieces = _w_in_pieces(d)
    cols = []
    for n in MY_ORDER:
        cols += [lax.slice_in_dim(g[s], a, a + wdt, axis=1) for (m, s, a, _, wdt) in pieces if m == n]
        if pw[n] != wd[n]:
            cols.append(jnp.zeros((g.shape[1], pw[n] - wd[n]), g.dtype))
    return jnp.concatenate(cols, axis=1)


def _w_in_mine_to_slabs(g_mine, d):
    _, _, _, my_off, _ = _in_layout(d)
    pieces = _w_in_pieces(d)
    slabs = []
    for s in range(N_DEV):
        parts = [lax.slice_in_dim(g_mine, my_off[m] + b, my_off[m] + b + wdt, axis=0)
                 for (m, s2, _, b, wdt) in pieces if s2 == s]
        slabs.append(jnp.concatenate(parts, axis=0))
    return jnp.stack(slabs)


def _wuq_to_mine(wq):
    r = wq.shape[0]
    w3 = wq.reshape(r, MLA_HEADS, MLA_NOPE + MLA_ROPE)
    w3 = jnp.pad(w3, ((0, 0), (0, 0), (0, MLA_QK_PAD - MLA_NOPE - MLA_ROPE)))
    return w3.reshape(r, MLA_HEADS * MLA_QK_PAD)


def _wuq_from_mine(g):
    r = g.shape[0]
    return g.reshape(r, MLA_HEADS, MLA_QK_PAD)[:, :, :MLA_NOPE + MLA_ROPE].reshape(r, -1)


def _wukv_to_mine(wkv):
    r = wkv.shape[0]
    w3 = wkv.reshape(r, MLA_HEADS, MLA_NOPE + MLA_V)
    return jnp.concatenate([w3[:, :, :MLA_NOPE].reshape(r, -1), w3[:, :, MLA_NOPE:].reshape(r, -1)], axis=1)


def _wukv_from_mine(g):
    r = g.shape[0]
    kn = g[:, :MLA_HEADS * MLA_NOPE].reshape(r, MLA_HEADS, MLA_NOPE)
    vv = g[:, MLA_HEADS * MLA_NOPE:].reshape(r, MLA_HEADS, MLA_V)
    return jnp.concatenate([kn, vv], axis=2).reshape(r, -1)


COL_SHARDED = ("w_in", "gla_gk_w", "mla_wuq", "mla_wukv", "ffn_w_in")


def _gathered_to_mine(name, g, d):
    if name == "w_in":
        return _w_in_gathered_to_mine(g, d)
    full = _unshard_cols(g) if name in COL_SHARDED else g.reshape(-1, g.shape[-1])
    if name == "gla_gk_w":
        return jnp.pad(full, ((0, LANE - GLA_GATE_RANK), (0, 0)))
    if name == "mla_wuq":
        return _wuq_to_mine(full)
    if name == "mla_wukv":
        return _wukv_to_mine(full)
    return full


def _grad_to_slabs(name, g, d):
    if g.ndim == 3:
        return g.astype(BF16)
    if name == "w_in":
        return _w_in_mine_to_slabs(g, d).astype(BF16)
    if name == "mla_wuq":
        g = _wuq_from_mine(g)
    elif name == "mla_wukv":
        g = _wukv_from_mine(g)
    s = _shard_cols(g) if name in COL_SHARDED else g.reshape(N_DEV, -1, g.shape[-1])
    return s.astype(BF16)


class _Rides:
    GATHER = {"in_proj": ("gla_wo", "mla_wuq", "mla_wukv", "mla_wo", "w_out"),
              "flash_fwd": ("ffn_w_in", "ffn_w_down")}
    SCATTER = {"flash_bwd": ("ffn_w_in", "ffn_w_down", "w_out", "gla_wo", "mla_wo", "gla_gk_w"),
               "in_proj_dw": ("mla_wuq", "mla_wukv"),
               "in_proj_dx": ("w_in",)}

    def __init__(self, send, d):
        self.send, self.d, self.recv = send, d, {}

    def ride(self, stage, grads):
        if stage in self.GATHER:
            return [self.send[n] for n in self.GATHER[stage]], False
        return [_grad_to_slabs(n, grads[n], self.d) for n in self.SCATTER[stage]], True

    def done(self, stage, rode, w):
        if stage in self.GATHER:
            for n, g in zip(self.GATHER[stage], rode):
                w[n] = _gathered_to_mine(n, g, self.d)
        else:
            self.recv.update(zip(self.SCATTER[stage], rode))


def kernel(x, c, positions, ada_w, ada_b, norm_mix_g, w_in, gla_gk_w, gla_gk_b, gla_onorm_g, gla_wo, mla_q_norm_g, mla_wuq, mla_kv_norm_g, mla_wukv, mla_wo, w_out, norm_ffn_g, ffn_w_in, ffn_w_down, final_norm_g, loss_target, m_ada_w, m_ada_b, m_norm_mix_g, m_w_in, m_gla_gk_w, m_gla_gk_b, m_gla_onorm_g, m_gla_wo, m_mla_q_norm_g, m_mla_wuq, m_mla_kv_norm_g, m_mla_wukv, m_mla_wo, m_w_out, m_norm_ffn_g, m_ffn_w_in, m_ffn_w_down, m_final_norm_g, v_ada_w, v_ada_b, v_norm_mix_g, v_w_in, v_gla_gk_w, v_gla_gk_b, v_gla_onorm_g, v_gla_wo, v_mla_q_norm_g, v_mla_wuq, v_mla_kv_norm_g, v_mla_wukv, v_mla_wo, v_w_out, v_norm_ffn_g, v_ffn_w_in, v_ffn_w_down, v_final_norm_g):
    wts = dict(ada_w=ada_w, ada_b=ada_b, norm_mix_g=norm_mix_g, w_in=w_in, gla_gk_w=gla_gk_w, gla_gk_b=gla_gk_b,
               gla_onorm_g=gla_onorm_g, gla_wo=gla_wo, mla_q_norm_g=mla_q_norm_g, mla_wuq=mla_wuq,
               mla_kv_norm_g=mla_kv_norm_g, mla_wukv=mla_wukv, mla_wo=mla_wo, w_out=w_out, norm_ffn_g=norm_ffn_g,
               ffn_w_in=ffn_w_in, ffn_w_down=ffn_w_down, final_norm_g=final_norm_g)
    mom_m = dict(zip(WEIGHTS, (m_ada_w, m_ada_b, m_norm_mix_g, m_w_in, m_gla_gk_w, m_gla_gk_b, m_gla_onorm_g,
                               m_gla_wo, m_mla_q_norm_g, m_mla_wuq, m_mla_kv_norm_g, m_mla_wukv, m_mla_wo, m_w_out,
                               m_norm_ffn_g, m_ffn_w_in, m_ffn_w_down, m_final_norm_g)))
    mom_v = dict(zip(WEIGHTS, (v_ada_w, v_ada_b, v_norm_mix_g, v_w_in, v_gla_gk_w, v_gla_gk_b, v_gla_onorm_g,
                               v_gla_wo, v_mla_q_norm_g, v_mla_wuq, v_mla_kv_norm_g, v_mla_wukv, v_mla_wo, v_w_out,
                               v_norm_ffn_g, v_ffn_w_in, v_ffn_w_down, v_final_norm_g)))
    seq, d = x.shape[1], x.shape[2]
    me = 4 * lax.axis_index("x") + 2 * lax.axis_index("y") + lax.axis_index("c")

    def two_d(a):
        return a.reshape(a.shape[-2], a.shape[-1]) if a.ndim >= 2 else a.reshape(1, -1)

    shard = {n: two_d(wts[n]) for n in BIG}
    send = {n: shard[n].astype(F32 if n == "gla_gk_w" else BF16) for n in BIG}
    got = _gather_once_per_chip([send["w_in"], send["gla_gk_w"], two_d(c)], name="comm_all_gather_first")
    c_all = got[2].reshape(N_DEV, d)
    w = dict(
        w_in=_gathered_to_mine("w_in", got[0], d), gla_gk_w=_gathered_to_mine("gla_gk_w", got[1], d),
        gla_gk_b=two_d(gla_gk_b), gla_onorm_g=two_d(gla_onorm_g), mla_q_norm_g=two_d(mla_q_norm_g),
        mla_kv_norm_g=two_d(mla_kv_norm_g), norm_mix_g=two_d(norm_mix_g), norm_ffn_g=two_d(norm_ffn_g),
        final_norm_g=two_d(final_norm_g))
    rides = _Rides(send, d)

    c_pad = jnp.pad(c_all, ((0, 16 - N_DEV), (0, 0)))
    (c_act,) = _rowwise(lambda cv: cv * _sigmoid(cv), [(c_pad, d, 0)], [], [(d, F32)], tile=16, name="silu_c")
    ada_w2 = two_d(ada_w)
    mod_part = _mm(c_act, ada_w2, name="mm_ada")[:N_DEV]
    (mod_all,) = _exchange([mod_part], scatter=False, name="comm_all_gather_mod")
    mod_mine = lax.dynamic_index_in_dim(mod_all, me, axis=1, keepdims=False).reshape(1, -1) + two_d(ada_b)
    mod8 = jnp.pad(mod_mine.reshape(6, d), ((0, 2), (0, 0)))

    inv_freq = ROPE_THETA ** (-jnp.arange(0, MLA_ROPE, 2, dtype=F32) / MLA_ROPE)
    ang = positions.reshape(seq, 1).astype(F32) * inv_freq[None, :]
    cos, sin, z32 = jnp.cos(ang), jnp.sin(ang), jnp.zeros((seq, 32), F32)
    tab = jnp.concatenate([cos, cos, z32, z32, -sin, z32, z32, z32, z32, sin, z32, z32], axis=1)
    loss_local, grad_x, _, small = _local_step(x.reshape(seq, d), loss_target.reshape(seq, d), tab, mod8, w, rides)

    recv = rides.recv
    pack = jnp.concatenate([small[n] for n in SMALL], axis=1)
    (pack_all,) = _exchange([pack], scatter=False, name="comm_all_gather_small")
    pack_all = pack_all.reshape(N_DEV, -1)

    res = {}
    for n in BIG:
        if n == "w_in":
            t_res = _adamw(jnp.swapaxes(wts[n], 1, 2), recv[n], jnp.swapaxes(mom_m[n], 1, 2),
                           jnp.swapaxes(mom_v[n], 1, 2), name="adamw_" + n)
            res[n] = tuple(jnp.swapaxes(a, 1, 2) for a in t_res)
        else:
            res[n] = _adamw(wts[n], recv[n], mom_m[n], mom_v[n], name="adamw_" + n)
    n_ada = ada_w2.shape[1]
    dmod_cols = lax.dynamic_slice_in_dim(pack_all[:, :6 * d], me * n_ada, n_ada, axis=1)

    def f_outer(cat, dm):
        acc = cat[:, 0:1] * dm[0:1]
        for b in range(1, N_DEV):
            acc = acc + cat[:, b:b + 1] * dm[b:b + 1]
        return acc

    (g_ada_w,) = _rowwise(f_outer, [(jnp.transpose(c_act[:N_DEV]), N_DEV, 0)], [dmod_cols], [(n_ada, F32)],
                          tile=256, name="ada_w_grad")
    res["ada_w"] = _adamw(ada_w, g_ada_w, m_ada_w, v_ada_w, name="adamw_ada_w")
    w_small = jnp.concatenate([two_d(wts[n]) for n in SMALL], axis=1)[None]
    m_small = jnp.concatenate([two_d(mom_m[n]) for n in SMALL], axis=1)[None]
    v_small = jnp.concatenate([two_d(mom_v[n]) for n in SMALL], axis=1)[None]
    small_res = _adamw(w_small, pack_all.reshape(N_DEV, 1, -1), m_small, v_small, name="adamw_small")
    off = 0
    for n in SMALL:
        width = wts[n].size
        res[n] = tuple(lax.slice_in_dim(a, off, off + width, axis=2) for a in small_res)
        off += width

    loss = lax.psum(loss_local, ("x", "y", "c"))
    outs = [loss, grad_x.reshape(x.shape)]
    for kind in range(4):
        outs += [res[n][kind].reshape(wts[n].shape) for n in WEIGHTS]
    return tuple(outs)
```

```python
import functools

import jax
import jax.numpy as jnp
from jax import lax
from jax.experimental import pallas as pl
from jax.experimental.pallas import tpu as pltpu

F32 = jnp.float32
BF16 = jnp.bfloat16

N_DEV = 8
GLA_HEADS = 4
GLA_DK = 256
GLA_DV = 512
GLA_GATE_RANK = 16
GLA_GATE_NORMALIZER = 16.0
GLA_CHUNK = 64
MLA_HEADS = 16
MLA_NOPE = 128
MLA_ROPE = 64
MLA_V = 128
MLA_QK_PAD = 256
ROPE_THETA = 10000.0
NORM_EPS = 1e-6
ATT_SCALE = (MLA_NOPE + MLA_ROPE) ** -0.5
GLA_QSCALE = GLA_DK ** -0.5

ADAM_LR = 0.001
ADAM_B1 = 0.9
ADAM_B2 = 0.999
ADAM_EPS = 1e-08
ADAM_WD = 0.01
ADAM_STEP = 10

LANE = 128
VMEM_LIMIT = 48 * 1024 * 1024
FLASH_BWD_VMEM = 58 * 1024 * 1024
MM_TILE_BYTES = 6 * 1024 * 1024
LOG2E = 1.4426950408889634
LN2 = 0.6931471805599453
NEG = -1e30

IN_NAMES = ("q", "k", "v", "g", "gk", "cq", "ckv", "kr", "ga", "gb")
MY_ORDER = ("v", "g", "ga", "gb", "q", "k", "cq", "ckv", "gk", "kr")

WEIGHTS = ("ada_w", "ada_b", "norm_mix_g", "w_in", "gla_gk_w", "gla_gk_b", "gla_onorm_g", "gla_wo",
           "mla_q_norm_g", "mla_wuq", "mla_kv_norm_g", "mla_wukv", "mla_wo", "w_out", "norm_ffn_g",
           "ffn_w_in", "ffn_w_down", "final_norm_g")
BIG = ("w_in", "gla_gk_w", "gla_wo", "mla_wuq", "mla_wukv", "mla_wo", "w_out", "ffn_w_in", "ffn_w_down")
SMALL = ("ada_b", "norm_mix_g", "gla_gk_b", "gla_onorm_g", "mla_q_norm_g", "mla_kv_norm_g", "norm_ffn_g",
         "final_norm_g")


def _in_layout(d):
    w = dict(q=d // 2, k=d // 2, v=d, g=d, gk=GLA_GATE_RANK, cq=d // 4, ckv=512, kr=MLA_ROPE, ga=d, gb=d)
    pw = {n: -(-w[n] // LANE) * LANE for n in w}
    ref_off, o = {}, 0
    for n in IN_NAMES:
        ref_off[n] = o
        o += w[n]
    my_off, o = {}, 0
    for n in MY_ORDER:
        assert o % pw[n] == 0
        my_off[n] = o
        o += pw[n]
    return w, pw, ref_off, my_off, o


def _cparams(sem=None, vmem_limit=VMEM_LIMIT):
    return pltpu.CompilerParams(dimension_semantics=sem, vmem_limit_bytes=vmem_limit)


def _dot(a, b, ca=1, cb=0):
    return lax.dot_general(a, b, (((ca,), (cb,)), ((), ())), preferred_element_type=F32)


def _tile(n, cap):
    if n <= cap:
        return n
    t = (cap // LANE) * LANE
    while t >= LANE:
        if n % t == 0:
            return t
        t -= LANE
    return n


def _mm(a, b, *, ta=False, tb=False, out_dtype=F32, name, ride=None, slabs=False):
    m, k = (a.shape[1], a.shape[0]) if ta else a.shape
    n = b.shape[0] if tb else b.shape[1]
    assert k == (b.shape[1] if tb else b.shape[0])
    wide = max(a.dtype.itemsize, b.dtype.itemsize) > 2
    tm, tn, tk = _tile(m, 1024), _tile(n, 1024), _tile(k, MM_TILE_BYTES // (1024 * (4 if wide else 2)))
    if slabs:
        tn = n // N_DEV
        assert tn % LANE == 0
    nk = k // tk

    def product(a_ref, b_ref):
        return _dot(a_ref[...].astype(BF16), b_ref[...].astype(BF16), 0 if ta else 1, 1 if tb else 0)

    def store(o_ref, val):
        if slabs:
            o_ref[0] = val.astype(o_ref.dtype)
        else:
            o_ref[...] = val.astype(o_ref.dtype)

    def body_one(a_ref, b_ref, o_ref):
        store(o_ref, product(a_ref, b_ref))

    def body_acc(a_ref, b_ref, o_ref, acc_ref):
        kk = pl.program_id(2)

        @pl.when(kk == 0)
        def _():
            acc_ref[...] = jnp.zeros_like(acc_ref)

        acc_ref[...] += product(a_ref, b_ref)

        @pl.when(kk == nk - 1)
        def _():
            store(o_ref, acc_ref[...])

    a_spec = (pl.BlockSpec((tk, tm), lambda i, j, kk: (kk, i)) if ta
              else pl.BlockSpec((tm, tk), lambda i, j, kk: (i, kk)))
    b_spec = (pl.BlockSpec((tn, tk), lambda i, j, kk: (j, kk)) if tb
              else pl.BlockSpec((tk, tn), lambda i, j, kk: (kk, j)))
    (out,), rode = _call(
        body_one if nk == 1 else body_acc, name=name, grid=(m // tm, n // tn, nk), in_specs=[a_spec, b_spec],
        out_specs=[pl.BlockSpec((1, tm, tn), lambda i, j, kk: (j, i, 0)) if slabs
                   else pl.BlockSpec((tm, tn), lambda i, j, kk: (i, j))],
        out_shape=[jax.ShapeDtypeStruct((N_DEV, m, tn) if slabs else (m, n), out_dtype)],
        scratch_shapes=[] if nk == 1 else [pltpu.VMEM((tm, tn), F32)],
        sem=("parallel", "parallel", "arbitrary"), args=(a, b), ride=ride)
    return out if ride is None else (out, rode)


def _rowwise(fn, rows, vecs, outs, sums=(), *, tile, name):
    t = rows[0][0].shape[0]
    tile = min(tile, t)
    assert t % tile == 0
    n_rows, n_vecs, n_outs = len(rows), len(vecs), len(outs)

    def body(*refs):
        ins = [r[...].astype(F32) for r in refs[:n_rows + n_vecs]]
        res = fn(*ins)
        if not isinstance(res, (tuple, list)):
            res = (res,)
        out_refs = refs[n_rows + n_vecs:]
        for r, val in zip(out_refs[:n_outs], res[:n_outs]):
            r[...] = val.astype(r.dtype)
        if sums:
            first = pl.program_id(0) == 0
            for r, val in zip(out_refs[n_outs:], res[n_outs:]):
                @pl.when(first)
                def _(r=r):
                    r[...] = jnp.zeros_like(r)
                r[...] += val

    in_specs = [pl.BlockSpec((tile, w), lambda i, cb=cb: (i, cb)) for (_, w, cb) in rows]
    in_specs += [pl.BlockSpec(v.shape, lambda i: (0, 0)) for v in vecs]
    out_specs = [pl.BlockSpec((tile, w), lambda i: (i, 0)) for (w, _) in outs]
    out_specs += [pl.BlockSpec((1, w), lambda i: (0, 0)) for w in sums]
    out_shape = [jax.ShapeDtypeStruct((t, w), dt) for (w, dt) in outs]
    out_shape += [jax.ShapeDtypeStruct((1, w), F32) for w in sums]
    res = pl.pallas_call(
        body, name=name, grid=(t // tile,), in_specs=in_specs, out_specs=out_specs, out_shape=out_shape,
        compiler_params=_cparams(("arbitrary",)),
    )(*[r[0] for r in rows], *vecs)
    return res


def _rstd(x):
    return lax.rsqrt(jnp.mean(x * x, axis=-1, keepdims=True) + NORM_EPS)


def _sigmoid(x):
    return 1.0 / (1.0 + jnp.exp(-x))


def _rms_bwd(dxh, xh, r):
    return r * (dxh - xh * jnp.mean(dxh * xh, axis=-1, keepdims=True))


def _rope(t, tab, sign):
    cosf, sin_a, sin_b = tab[:, :LANE], tab[:, LANE:2 * LANE], tab[:, 2 * LANE:]
    return t * cosf + sign * (pltpu.roll(t, 96, 1) * sin_a + pltpu.roll(t, 32, 1) * sin_b)


def _split3(x):
    hi = x.astype(BF16)
    r1 = x - hi.astype(F32)
    mid = r1.astype(BF16)
    lo = (r1 - mid.astype(F32)).astype(BF16)
    return hi, mid, lo


def _tri_sum(tri_bf16, x):
    hi, mid, lo = _split3(x)
    return _dot(tri_bf16, hi) + _dot(tri_bf16, mid) + _dot(tri_bf16, lo)


def _dot_nt2(a, b):
    a_hi = a.astype(BF16)
    a_lo = (a - a_hi.astype(F32)).astype(BF16)
    b_hi = b.astype(BF16)
    b_lo = (b - b_hi.astype(F32)).astype(BF16)
    return _dot(a_hi, b_hi, 1, 1) + _dot(a_hi, b_lo, 1, 1) + _dot(a_lo, b_hi, 1, 1)


GLA_PAIR = 2


def _gla_specs(t, rows, lay, reverse):
    nb = t // rows
    blk = (lambda i: nb - 1 - i) if reverse else (lambda i: i)
    wk, wv = GLA_PAIR * GLA_DK, GLA_PAIR * GLA_DV
    qb, kb, vb = lay["q"] // wk, lay["k"] // wk, lay["v"] // wv
    return [
        pl.BlockSpec((rows, wk), lambda g, i: (blk(i), qb + g)),
        pl.BlockSpec((rows, wk), lambda g, i: (blk(i), kb + g)),
        pl.BlockSpec((rows, wv), lambda g, i: (blk(i), vb + g)),
        pl.BlockSpec((rows, wk), lambda g, i: (blk(i), g)),
    ], blk


def _gla_fwd(p, la, lay):
    t = p.shape[0]
    rows = min(512, t)
    nb, nc = t // rows, rows // GLA_CHUNK
    c64 = GLA_CHUNK

    def body(q_ref, k_ref, v_ref, la_ref, o_ref, st_ref, s_ref):
        @pl.when(pl.program_id(1) == 0)
        def _():
            s_ref[...] = jnp.zeros_like(s_ref)

        r = lax.broadcasted_iota(jnp.int32, (c64, c64), 0)
        cc = lax.broadcasted_iota(jnp.int32, (c64, c64), 1)
        tril = cc <= r
        tril_b = tril.astype(BF16)
        state = [s_ref[hh] for hh in range(GLA_PAIR)]
        for c in range(nc):
            sl = pl.ds(c * c64, c64)
            for hh in range(GLA_PAIR):
                lk, lv = pl.ds(hh * GLA_DK, GLA_DK), pl.ds(hh * GLA_DV, GLA_DV)
                b = _tri_sum(tril_b, la_ref[sl, lk])
                b_last = b[c64 - 1:c64, :]
                q = q_ref[sl, lk].astype(F32) * GLA_QSCALE
                k = k_ref[sl, lk].astype(F32)
                v = v_ref[sl, lv].astype(BF16)
                qt_f = q * jnp.exp(b)
                qt = qt_f.astype(BF16)
                kh = (k * jnp.exp(b_last - b)).astype(BF16)
                s_prev = state[hh]
                st_ref[hh, c] = s_prev
                att = jnp.where(tril, _dot_nt2(qt_f, k * jnp.exp(-b)), 0.0)
                o_ref[sl, lv] = _dot(qt, s_prev.astype(BF16), 1, 1) + _dot(att.astype(BF16), v)
                state[hh] = s_prev * jnp.exp(b_last) + _dot(v, kh, 0, 0)
        for hh in range(GLA_PAIR):
            s_ref[hh] = state[hh]

    in_specs, _ = _gla_specs(t, rows, lay, False)
    return pl.pallas_call(
        body, name="gla_fwd", grid=(GLA_HEADS // GLA_PAIR, nb), in_specs=in_specs,
        out_specs=[pl.BlockSpec((rows, GLA_PAIR * GLA_DV), lambda g, i: (i, g)),
                   pl.BlockSpec((GLA_PAIR, nc, GLA_DV, GLA_DK), lambda g, i: (g, i, 0, 0))],
        out_shape=[jax.ShapeDtypeStruct((t, GLA_HEADS * GLA_DV), F32),
                   jax.ShapeDtypeStruct((GLA_HEADS, t // c64, GLA_DV, GLA_DK), F32)],
        scratch_shapes=[pltpu.VMEM((GLA_PAIR, GLA_DV, GLA_DK), F32)],
        compiler_params=_cparams(("parallel", "arbitrary")),
    )(p, p, p, la)


def _gla_bwd(p, la, do, states, lay):
    t = p.shape[0]
    rows = min(512, t)
    nb, nc = t // rows, rows // GLA_CHUNK
    c64 = GLA_CHUNK

    def body(q_ref, k_ref, v_ref, la_ref, do_ref, st_ref, dq_ref, dk_ref, dv_ref, dla_ref, ds_ref):
        @pl.when(pl.program_id(1) == 0)
        def _():
            ds_ref[...] = jnp.zeros_like(ds_ref)

        r = lax.broadcasted_iota(jnp.int32, (c64, c64), 0)
        cc = lax.broadcasted_iota(jnp.int32, (c64, c64), 1)
        tril = cc <= r
        tril_b = tril.astype(BF16)
        triu_b = (cc >= r).astype(BF16)
        dstate = [ds_ref[hh] for hh in range(GLA_PAIR)]
        for c in reversed(range(nc)):
            sl = pl.ds(c * c64, c64)
            for hh in range(GLA_PAIR):
                lk, lv = pl.ds(hh * GLA_DK, GLA_DK), pl.ds(hh * GLA_DV, GLA_DV)
                b = _tri_sum(tril_b, la_ref[sl, lk])
                b_last = b[c64 - 1:c64, :]
                eb, enb, ebl_b, ebl = jnp.exp(b), jnp.exp(-b), jnp.exp(b_last - b), jnp.exp(b_last)
                k = k_ref[sl, lk].astype(F32)
                qt_f = q_ref[sl, lk].astype(F32) * GLA_QSCALE * eb
                kt_f = k * enb
                kh_f = k * ebl_b
                qt, kt, kh = qt_f.astype(BF16), kt_f.astype(BF16), kh_f.astype(BF16)
                v_f = v_ref[sl, lv].astype(F32)
                dout_f = do_ref[sl, lv]
                v, dout = v_f.astype(BF16), dout_f.astype(BF16)
                s_prev = st_ref[hh, c]
                ds_next = dstate[hh]
                ds_next_b = ds_next.astype(BF16)
                att = jnp.where(tril, _dot_nt2(qt_f, kt_f), 0.0).astype(BF16)
                datt = jnp.where(tril, _dot_nt2(dout_f, v_f), 0.0).astype(BF16)
                dqt = _dot(dout, s_prev.astype(BF16)) + _dot(datt, kt)
                dkt = _dot(datt, qt, 0, 0)
                dv = _dot(att, dout, 0, 0) + _dot(kh, ds_next_b, 1, 1)
                dkh = _dot(v, ds_next_b)
                d_ebl = jnp.sum(ds_next * s_prev, axis=0, keepdims=True)
                dstate[hh] = ds_next * ebl + _dot(dout, qt, 0, 0)
                db = dqt * qt_f - dkt * kt_f - dkh * kh_f
                db_last = ebl * d_ebl + jnp.sum(dkh * kh_f, axis=0, keepdims=True)
                dq_ref[sl, lk] = (dqt * eb * GLA_QSCALE).astype(dq_ref.dtype)
                dk_ref[sl, lk] = (dkt * enb + dkh * ebl_b).astype(dk_ref.dtype)
                dv_ref[sl, lv] = dv.astype(dv_ref.dtype)
                dla_ref[sl, lk] = _tri_sum(triu_b, db) + db_last
        for hh in range(GLA_PAIR):
            ds_ref[hh] = dstate[hh]

    in_specs, blk = _gla_specs(t, rows, lay, True)
    wk, wv = GLA_PAIR * GLA_DK, GLA_PAIR * GLA_DV
    in_specs += [pl.BlockSpec((rows, wv), lambda g, i: (blk(i), g)),
                 pl.BlockSpec((GLA_PAIR, nc, GLA_DV, GLA_DK), lambda g, i: (g, blk(i), 0, 0))]
    dk_spec = pl.BlockSpec((rows, wk), lambda g, i: (blk(i), g))
    return pl.pallas_call(
        body, name="gla_bwd", grid=(GLA_HEADS // GLA_PAIR, nb), in_specs=in_specs,
        out_specs=[dk_spec, dk_spec, pl.BlockSpec((rows, wv), lambda g, i: (blk(i), g)), dk_spec],
        out_shape=[jax.ShapeDtypeStruct((t, GLA_HEADS * GLA_DK), BF16),
                   jax.ShapeDtypeStruct((t, GLA_HEADS * GLA_DK), BF16),
                   jax.ShapeDtypeStruct((t, GLA_HEADS * GLA_DV), BF16),
                   jax.ShapeDtypeStruct((t, GLA_HEADS * GLA_DK), F32)],
        scratch_shapes=[pltpu.VMEM((GLA_PAIR, GLA_DV, GLA_DK), F32)],
        compiler_params=_cparams(("parallel", "arbitrary")),
    )(p, p, p, la, do, states)


def _diag_mask(rows, cols, row0):
    row = row0 + lax.broadcasted_iota(jnp.int32, (rows, cols), 0)
    col = lax.broadcasted_iota(jnp.int32, (rows, cols), 1)
    return col <= row


QK_SPARE = MLA_NOPE + MLA_ROPE
N_SPARE = 3


def _with_spare(x, col, lane0):
    lane = lax.broadcasted_iota(jnp.int32, x.shape, 1)
    for n, term in enumerate(_split3(col)):
        x = jnp.where(lane == lane0 + n, term, x)
    return x


def _spare_ones(shape, lane0):
    lane = lax.broadcasted_iota(jnp.int32, shape, 1)
    return ((lane >= lane0) & (lane < lane0 + N_SPARE)).astype(F32)


def _flash_tiles(t):
    tq = min(1024, t)
    halves = 2 if tq % 32 == 0 else 1
    return tq, t // tq, halves, tq // halves


def _flash_fwd(q, k, vx, ride=None):
    t = q.shape[0]
    tq, nq, halves, hr = _flash_tiles(t)
    dqk, dv = MLA_QK_PAD, MLA_V

    def body(q_ref, k_ref, v_ref, o_ref, qa_ref, m_ref, acc_ref, s_ref):
        i = pl.program_id(1)
        m_ref[...] = jnp.full_like(m_ref, NEG)
        acc_ref[...] = jnp.zeros_like(acc_ref)

        def scores(j, slot):
            kb = k_ref[pl.ds(pl.multiple_of(j * tq, tq), tq), :]
            for hh in range(halves):
                s_ref[slot, pl.ds(hh * hr, hr), :] = _dot(q_ref[pl.ds(hh * hr, hr), :], kb, 1, 1)

        def consume(j, slot, masked):
            vb = v_ref[pl.ds(pl.multiple_of(j * tq, tq), tq), :]
            for hh in range(halves):
                rs = pl.ds(hh * hr, hr)
                s = s_ref[slot, rs, :]
                if masked:
                    s = jnp.where(_diag_mask(hr, tq, hh * hr), s, NEG)
                m_old = m_ref[rs, :]
                m_new = jnp.maximum(m_old, jnp.max(s, axis=1, keepdims=True))
                pr = jnp.exp2(s - m_new)
                acc_ref[rs, :] = jnp.exp2(m_old - m_new) * acc_ref[rs, :] + _dot(pr.astype(BF16), vb)
                m_ref[rs, :] = m_new

        def two_blocks(jj, carry):
            scores(2 * jj + 1, 1)
            consume(2 * jj, 0, False)
            scores(2 * jj + 2, 0)
            consume(2 * jj + 1, 1, False)
            return carry

        scores(0, 0)
        lax.fori_loop(0, i // 2, two_blocks, 0)

        @pl.when(i % 2 == 0)
        def _():
            consume(i, 0, True)

        @pl.when(i % 2 == 1)
        def _():
            scores(i, 1)
            consume(i - 1, 0, False)
            consume(i, 1, True)

        acc = acc_ref[...]
        l = acc[:, dv:dv + 1]
        o_ref[...] = acc[:, :dv] / l
        qa_ref[...] = _with_spare(q_ref[...], -(m_ref[...] + jnp.log(l) * LOG2E), QK_SPARE)

    outs, rode = _call(
        body, name="mla_flash_fwd", grid=(MLA_HEADS, nq),
        in_specs=[pl.BlockSpec((tq, dqk), lambda h, i: (i, h)),
                  pl.BlockSpec((t, dqk), lambda h, i: (0, h)),
                  pl.BlockSpec((t, 2 * dv), lambda h, i: (0, h))],
        out_specs=[pl.BlockSpec((tq, dv), lambda h, i: (i, h)),
                   pl.BlockSpec((tq, dqk), lambda h, i: (i, h))],
        out_shape=[jax.ShapeDtypeStruct((t, MLA_HEADS * dv), F32),
                   jax.ShapeDtypeStruct((t, MLA_HEADS * dqk), BF16)],
        scratch_shapes=[pltpu.VMEM((tq, 1), F32), pltpu.VMEM((tq, 2 * dv), F32), pltpu.VMEM((2, tq, tq), F32)],
        sem=("parallel", "arbitrary"), args=(q, k, vx), ride=ride)
    return tuple(outs) if ride is None else (tuple(outs), rode)


def _flash_bwd(qa, k, vx, doa, ride=None):
    t = qa.shape[0]
    tq, nq, halves, hr = _flash_tiles(t)
    dqk, dv = MLA_QK_PAD, MLA_V

    def body(k_ref, v_ref, q_ref, do_ref, dq_ref, dk_ref, dv_ref, dq_acc, dk_acc, dv_acc):
        j = pl.program_id(1)

        @pl.when(j == 0)
        def _():
            dq_acc[...] = jnp.zeros_like(dq_acc)

        kb = k_ref[...]
        vb = v_ref[...]
        dk_acc[...] = jnp.zeros_like(dk_acc)
        dv_acc[...] = jnp.zeros_like(dv_acc)

        def step(i, masked):
            for hh in range(halves):
                rs = pl.ds(pl.multiple_of(i * tq + hh * hr, hr), hr)
                qb = q_ref[rs, :]
                dout = do_ref[rs, :]
                nc = (hh + 1) * hr if masked else tq
                s = _dot(qb, kb[:nc], 1, 1)
                if masked:
                    s = jnp.where(_diag_mask(hr, nc, hh * hr), s, NEG)
                pr = jnp.exp2(s)
                ds = (pr * _dot(dout, vb[:nc], 1, 1)).astype(BF16)
                dv_acc[pl.ds(0, nc), :] += _dot(pr.astype(BF16), dout, 0, 0)
                dk_acc[pl.ds(0, nc), :] += _dot(ds, qb, 0, 0)
                dq_acc[rs, :] += _dot(ds, kb[:nc])

        def loop_body(i, carry):
            step(i, False)
            return carry

        step(j, True)
        lax.fori_loop(j + 1, nq, loop_body, 0)
        dk_ref[...] = (dk_acc[...] * LN2).astype(dk_ref.dtype)
        dv_ref[...] = dv_acc[:, :dv].astype(dv_ref.dtype)

        @pl.when(j == nq - 1)
        def _():
            dq_ref[...] = (dq_acc[...] * ATT_SCALE).astype(dq_ref.dtype)

    outs, rode = _call(
        body, name="mla_flash_bwd", grid=(MLA_HEADS, nq),
        in_specs=[pl.BlockSpec((tq, dqk), lambda h, j: (j, h)),
                  pl.BlockSpec((tq, 2 * dv), lambda h, j: (j, h)),
                  pl.BlockSpec((t, dqk), lambda h, j: (0, h)),
                  pl.BlockSpec((t, 2 * dv), lambda h, j: (0, h))],
        out_specs=[pl.BlockSpec((t, dqk), lambda h, j: (0, h)),
                   pl.BlockSpec((tq, dqk), lambda h, j: (j, h)),
                   pl.BlockSpec((tq, dv), lambda h, j: (j, h))],
        out_shape=[jax.ShapeDtypeStruct((t, MLA_HEADS * dqk), BF16),
                   jax.ShapeDtypeStruct((t, MLA_HEADS * dqk), BF16),
                   jax.ShapeDtypeStruct((t, MLA_HEADS * dv), BF16)],
        scratch_shapes=[pltpu.VMEM((t, dqk), F32), pltpu.VMEM((tq, dqk), F32), pltpu.VMEM((tq, 2 * dv), F32)],
        sem=("parallel", "arbitrary"), args=(k, vx, qa, doa), ride=ride, vmem_limit=FLASH_BWD_VMEM)
    return tuple(outs) if ride is None else (tuple(outs), rode)


class _NoRides:
    def ride(self, stage, grads):
        return None

    def done(self, stage, rode, w):
        pass


def _local_step(x, target, tab, mod8, w, rides=None):
    t, d = x.shape
    rides = rides or _NoRides()
    big = {}

    def riding(stage, fn):
        r = rides.ride(stage, big)
        res = fn(r)
        if r is None:
            return res
        rides.done(stage, res[1], w)
        return res[0]
    _, pw, _, lay, _ = _in_layout(d)
    ffn = ((8 * d // 3 + 255) // 256) * 256

    def blk(arr, name):
        return (arr, pw[name], lay[name] // pw[name])

    def full(arr):
        return (arr, arr.shape[1], 0)

    g1, g2, g3 = w["norm_mix_g"], w["norm_ffn_g"], w["final_norm_g"]

    def f_ln1(xv, mod, g):
        return (xv * _rstd(xv) * g) * (1.0 + mod[1:2]) + mod[0:1]

    (h,) = _rowwise(f_ln1, [full(x)], [mod8, g1], [(d, BF16)], tile=512, name="ln1_modulate")
    p = riding("in_proj", lambda r: _mm(h, w["w_in"], name="mm_in_proj", ride=r))

    def f_gk(pgk, gkw, gkb):
        z = _dot(pgk.astype(BF16), gkw.astype(BF16)) + gkb
        return (jnp.minimum(z, 0.0) - jnp.log(1.0 + jnp.exp(-jnp.abs(z)))) / GLA_GATE_NORMALIZER

    (la,) = _rowwise(f_gk, [blk(p, "gk")], [w["gla_gk_w"], w["gla_gk_b"]], [(GLA_HEADS * GLA_DK, F32)],
                     tile=512, name="gla_gate")
    o_gla, states = _gla_fwd(p, la, lay)

    def f_gla_out(ov, pg, g):
        parts = []
        for hh in range(GLA_HEADS):
            oh = ov[:, hh * GLA_DV:(hh + 1) * GLA_DV]
            ph = pg[:, hh * GLA_DV:(hh + 1) * GLA_DV]
            parts.append(oh * _rstd(oh) * g * (ph * _sigmoid(ph)))
        return jnp.concatenate(parts, axis=1)

    (o_n,) = _rowwise(f_gla_out, [full(o_gla), blk(p, "g")], [w["gla_onorm_g"]], [(d, BF16)], tile=512,
                      name="gla_out_norm")
    y_gla = _mm(o_n, w["gla_wo"], out_dtype=BF16, name="mm_gla_wo")

    def f_mla_prep(cq, ckv, kr, tb, gq, gkv):
        return cq * _rstd(cq) * gq, ckv * _rstd(ckv) * gkv, _rope(kr, tb, 1.0)

    cqn, ckvn, krr = _rowwise(f_mla_prep, [blk(p, "cq"), blk(p, "ckv"), blk(p, "kr"), full(tab)],
                              [w["mla_q_norm_g"], w["mla_kv_norm_g"]],
                              [(pw["cq"], BF16), (pw["ckv"], BF16), (LANE, F32)], tile=512, name="mla_prep")
    qlat = _mm(cqn, w["mla_wuq"], out_dtype=BF16, name="mm_mla_wuq")
    kvl = _mm(ckvn, w["mla_wukv"], out_dtype=BF16, name="mm_mla_wukv")
    hv = MLA_HEADS * MLA_V

    def f_qkv(ql, kn, vv, kr, tb):
        qs, ks, vx = [], [], []
        kr1 = kr + _spare_ones(kr.shape, MLA_ROPE)
        ones = _spare_ones(kr.shape, 0)
        for hh in range(MLA_HEADS):
            o0 = hh * MLA_QK_PAD
            qs += [ql[:, o0:o0 + LANE], _rope(ql[:, o0 + LANE:o0 + 2 * LANE], tb, 1.0)]
            ks += [kn[:, hh * LANE:(hh + 1) * LANE], kr1]
            vx += [vv[:, hh * MLA_V:(hh + 1) * MLA_V], ones]
        return (jnp.concatenate(qs, axis=1) * (ATT_SCALE * LOG2E), jnp.concatenate(ks, axis=1),
                jnp.concatenate(vx, axis=1))

    qa, ka, vxa = _rowwise(f_qkv, [full(qlat), (kvl, hv, 0), (kvl, hv, 1), full(krr), full(tab)], [],
                           [(MLA_HEADS * MLA_QK_PAD, BF16), (MLA_HEADS * MLA_QK_PAD, BF16), (2 * hv, BF16)],
                           tile=256, name="mla_qkv_build")
    o_mla, qa_lse = riding("flash_fwd", lambda r: _flash_fwd(qa, ka, vxa, ride=r))
    y_mla = _mm(o_mla, w["mla_wo"], out_dtype=BF16, name="mm_mla_wo")

    def f_merge(yg, ym, ga, gb):
        return _sigmoid(ga) * yg + _sigmoid(gb) * ym

    (merged,) = _rowwise(f_merge, [full(y_gla), full(y_mla), blk(p, "ga"), blk(p, "gb")], [], [(d, BF16)],
                         tile=512, name="merge")
    mix = _mm(merged, w["w_out"], name="mm_w_out")

    def f_res_ln2(xv, mx, mod, g):
        x2v = xv + mod[2:3] * mx
        return x2v, (x2v * _rstd(x2v) * g) * (1.0 + mod[4:5]) + mod[3:4]

    x2, h2 = _rowwise(f_res_ln2, [full(x), full(mix)], [mod8, g2], [(d, F32), (d, BF16)], tile=512,
                      name="res_ln2_modulate")
    gu = _mm(h2, w["ffn_w_in"], out_dtype=BF16, name="mm_ffn_in")

    def f_swiglu(gv, uv):
        return gv * _sigmoid(gv) * uv

    (act,) = _rowwise(f_swiglu, [(gu, ffn, 0), (gu, ffn, 1)], [], [(ffn, BF16)], tile=256, name="swiglu")
    f_out = _mm(act, w["ffn_w_down"], name="mm_ffn_down")

    def f_head(x2v, fv, tg, mod, g):
        x3 = x2v + mod[5:6] * fv
        r = _rstd(x3)
        xh = x3 * r
        e = xh * g - tg
        loss_rows = 0.5 * jnp.mean(e * e, axis=-1, keepdims=True)
        dy = e * (1.0 / d)
        dx3 = _rms_bwd(dy * g, xh, r)
        loss = jnp.broadcast_to(jnp.sum(loss_rows, axis=0, keepdims=True), (1, LANE))
        return (dx3, dx3 * mod[5:6], loss, jnp.sum(dy * xh, axis=0, keepdims=True),
                jnp.sum(dx3 * fv, axis=0, keepdims=True))

    dx3, df, loss_v, dg3, dgate_f = _rowwise(f_head, [full(x2), full(f_out), full(target)], [mod8, g3],
                                             [(d, F32), (d, BF16)], [LANE, d, d], tile=256, name="loss_head")
    da = _mm(df, w["ffn_w_down"], tb=True, out_dtype=BF16, name="mm_ffn_down_dx")
    big["ffn_w_down"] = _mm(act, df, ta=True, out_dtype=BF16, name="mm_ffn_down_dw")

    def f_swiglu_bwd(gv, uv, dav):
        sg = _sigmoid(gv)
        return jnp.concatenate([dav * uv * (sg * (1.0 + gv * (1.0 - sg))), dav * (gv * sg)], axis=1)

    (dgu,) = _rowwise(f_swiglu_bwd, [(gu, ffn, 0), (gu, ffn, 1), full(da)], [], [(2 * ffn, BF16)], tile=256,
                      name="swiglu_bwd")
    dh2 = _mm(dgu, w["ffn_w_in"], tb=True, name="mm_ffn_in_dx")
    big["ffn_w_in"] = _mm(h2, dgu, ta=True, out_dtype=BF16, slabs=True, name="mm_ffn_in_dw")

    def f_ln2_bwd(x2v, dh, dx3v, mx, mod, g):
        r = _rstd(x2v)
        xh = x2v * r
        dn = dh * (1.0 + mod[4:5])
        dx2 = dx3v + _rms_bwd(dn * g, xh, r)
        return (dx2, dx2 * mod[2:3],
                jnp.sum(dh * (xh * g), axis=0, keepdims=True), jnp.sum(dh, axis=0, keepdims=True),
                jnp.sum(dn * xh, axis=0, keepdims=True), jnp.sum(dx2 * mx, axis=0, keepdims=True))

    dx2, dmix, dscale_f, dshift_f, dg2, dgate_m = _rowwise(
        f_ln2_bwd, [full(x2), full(dh2), full(dx3), full(mix)], [mod8, g2], [(d, F32), (d, BF16)],
        [d, d, d, d], tile=256, name="ln2_bwd")
    dmerged = _mm(dmix, w["w_out"], tb=True, out_dtype=BF16, name="mm_w_out_dx")
    big["w_out"] = _mm(merged, dmix, ta=True, out_dtype=BF16, name="mm_w_out_dw")

    def f_merge_bwd(dm, yg, ym, ga, gb):
        sa, sb = _sigmoid(ga), _sigmoid(gb)
        return dm * sa, dm * sb, dm * yg * sa * (1.0 - sa), dm * ym * sb * (1.0 - sb)

    dy_gla, dy_mla, dp_ga, dp_gb = _rowwise(
        f_merge_bwd, [full(dmerged), full(y_gla), full(y_mla), blk(p, "ga"), blk(p, "gb")], [],
        [(d, BF16)] * 4, tile=256, name="merge_bwd")
    do_n = _mm(dy_gla, w["gla_wo"], tb=True, name="mm_gla_wo_dx")
    big["gla_wo"] = _mm(o_n, dy_gla, ta=True, out_dtype=BF16, name="mm_gla_wo_dw")
    do_m = _mm(dy_mla, w["mla_wo"], tb=True, out_dtype=BF16, name="mm_mla_wo_dx")
    big["mla_wo"] = _mm(o_mla, dy_mla, ta=True, out_dtype=BF16, name="mm_mla_wo_dw")

    def f_gla_out_bwd(don, ov, pg, g):
        dos, dpgs = [], []
        dg = jnp.zeros((1, GLA_DV), F32)
        for hh in range(GLA_HEADS):
            sl = slice(hh * GLA_DV, (hh + 1) * GLA_DV)
            oh, ph, dn = ov[:, sl], pg[:, sl], don[:, sl]
            r = _rstd(oh)
            xh = oh * r
            sg = _sigmoid(ph)
            dpre = dn * (ph * sg)
            dg = dg + jnp.sum(dpre * xh, axis=0, keepdims=True)
            dos.append(_rms_bwd(dpre * g, xh, r))
            dpgs.append(dn * (xh * g) * (sg * (1.0 + ph * (1.0 - sg))))
        return jnp.concatenate(dos, axis=1), jnp.concatenate(dpgs, axis=1), dg

    do_gla, dp_g, dg_on = _rowwise(f_gla_out_bwd, [full(do_n), full(o_gla), blk(p, "g")], [w["gla_onorm_g"]],
                                   [(d, F32), (d, BF16)], [GLA_DV], tile=256, name="gla_out_norm_bwd")
    dp_q, dp_k, dp_v, dla = _gla_bwd(p, la, do_gla, states, lay)

    def f_gk_bwd(dlav, pgk, gkw, gkb):
        z = _dot(pgk.astype(BF16), gkw.astype(BF16)) + gkb
        dz = dlav * (1.0 / GLA_GATE_NORMALIZER) * _sigmoid(-z)
        return dz, _dot(dz.astype(BF16), gkw.astype(BF16), 1, 1), jnp.sum(dz, axis=0, keepdims=True)

    dz, dp_gk, dgk_b = _rowwise(f_gk_bwd, [full(dla), blk(p, "gk")], [w["gla_gk_w"], w["gla_gk_b"]],
                                [(GLA_HEADS * GLA_DK, BF16), (LANE, BF16)], [GLA_HEADS * GLA_DK], tile=512,
                                name="gla_gate_bwd")
    p_gk = lax.slice_in_dim(p, lay["gk"], lay["gk"] + LANE, axis=1)
    big["gla_gk_w"] = _mm(p_gk, dz, ta=True, name="mm_gla_gk_dw")[:GLA_GATE_RANK]

    def f_do_aug(dom, om):
        parts = []
        for hh in range(MLA_HEADS):
            dh_ = dom[:, hh * MLA_V:(hh + 1) * MLA_V]
            delta = jnp.sum(dh_ * om[:, hh * MLA_V:(hh + 1) * MLA_V], axis=1, keepdims=True)
            parts += [dh_.astype(BF16), _with_spare(jnp.zeros(dh_.shape, BF16), -delta, 0)]
        return jnp.concatenate(parts, axis=1)

    (doa,) = _rowwise(f_do_aug, [full(do_m), full(o_mla)], [], [(2 * hv, BF16)], tile=256, name="mla_do_delta")
    dqa, dka, dva = riding("flash_bwd", lambda r: _flash_bwd(qa_lse, ka, vxa, doa, ride=r))

    def f_qkv_bwd(dq, dk, dvv, tb):
        dqs, dkn = [], []
        dkr = jnp.zeros((dq.shape[0], LANE), F32)
        for hh in range(MLA_HEADS):
            o0 = hh * MLA_QK_PAD
            dqs += [dq[:, o0:o0 + LANE], _rope(dq[:, o0 + LANE:o0 + 2 * LANE], tb, -1.0)]
            dkn.append(dk[:, o0:o0 + LANE])
            dkr = dkr + dk[:, o0 + LANE:o0 + 2 * LANE]
        return jnp.concatenate(dqs, axis=1), jnp.concatenate(dkn + [dvv], axis=1), dkr

    dqlat, dkvl, dkrr = _rowwise(f_qkv_bwd, [full(dqa), full(dka), full(dva), full(tab)], [],
                                 [(MLA_HEADS * MLA_QK_PAD, BF16), (2 * hv, BF16), (LANE, F32)], tile=256,
                                 name="mla_qkv_build_bwd")
    dcqn = _mm(dqlat, w["mla_wuq"], tb=True, name="mm_mla_wuq_dx")
    big["mla_wuq"] = _mm(cqn, dqlat, ta=True, out_dtype=BF16, name="mm_mla_wuq_dw")
    dckvn = _mm(dkvl, w["mla_wukv"], tb=True, name="mm_mla_wukv_dx")
    big["mla_wukv"] = _mm(ckvn, dkvl, ta=True, out_dtype=BF16, name="mm_mla_wukv_dw")

    def f_mla_prep_bwd(dq, dkv, dkr, cq, ckv, tb, gq, gkv):
        rq, rk = _rstd(cq), _rstd(ckv)
        xq, xk = cq * rq, ckv * rk
        return (_rms_bwd(dq * gq, xq, rq), _rms_bwd(dkv * gkv, xk, rk), _rope(dkr, tb, -1.0),
                jnp.sum(dq * xq, axis=0, keepdims=True), jnp.sum(dkv * xk, axis=0, keepdims=True))

    dp_cq, dp_ckv, dp_kr, dg_q, dg_kv = _rowwise(
        f_mla_prep_bwd, [full(dcqn), full(dckvn), full(dkrr), blk(p, "cq"), blk(p, "ckv"), full(tab)],
        [w["mla_q_norm_g"], w["mla_kv_norm_g"]], [(pw["cq"], BF16), (pw["ckv"], BF16), (LANE, BF16)],
        [pw["cq"], pw["ckv"]], tile=512, name="mla_prep_bwd")

    pieces = dict(v=dp_v, g=dp_g, ga=dp_ga, gb=dp_gb, q=dp_q, k=dp_k, cq=dp_cq, ckv=dp_ckv, gk=dp_gk, kr=dp_kr)
    dp = jnp.concatenate([pieces[n] for n in MY_ORDER], axis=1)
    big["w_in"] = riding("in_proj_dw", lambda r: _mm(dp, h, ta=True, out_dtype=BF16, name="mm_in_proj_dw", ride=r))
    dh = riding("in_proj_dx", lambda r: _mm(dp, w["w_in"], tb=True, name="mm_in_proj_dx", ride=r))

    def f_ln1_bwd(xv, dhv, dx2v, mod, g):
        r = _rstd(xv)
        xh = xv * r
        dn = dhv * (1.0 + mod[1:2])
        return (dx2v + _rms_bwd(dn * g, xh, r),
                jnp.sum(dhv * (xh * g), axis=0, keepdims=True), jnp.sum(dhv, axis=0, keepdims=True),
                jnp.sum(dn * xh, axis=0, keepdims=True))

    grad_x, dscale_m, dshift_m, dg1 = _rowwise(f_ln1_bwd, [full(x), full(dh), full(dx2)], [mod8, g1],
                                               [(d, F32)], [d, d, d], tile=256, name="ln1_bwd")

    dmod = jnp.concatenate([dshift_m, dscale_m, dgate_m, dshift_f, dscale_f, dgate_f], axis=1)
    small = dict(ada_b=dmod, norm_mix_g=dg1, gla_gk_b=dgk_b, gla_onorm_g=dg_on, mla_q_norm_g=dg_q,
                 mla_kv_norm_g=dg_kv, norm_ffn_g=dg2, final_norm_g=dg3)
    return loss_v[0, 0], grad_x, big, small


N_PEER = N_DEV - 1


def _exchange_copies(ins, outs, sems, scatter):
    send_sems, recv_sems, local_sems = sems
    x, y, c = lax.axis_index("x"), lax.axis_index("y"), lax.axis_index("c")
    me = 4 * x + 2 * y + c
    peers = []
    for rel in range(1, N_DEV):
        px = 1 - x if rel & 4 else x
        py = 1 - y if rel & 2 else y
        pc = 1 - c if rel & 1 else c
        peers.append(((px, py, pc), 4 * px + 2 * py + pc))

    def remote(a, k, src_slot, dst_slot):
        src = ins[a].at[src_slot] if scatter else ins[a]
        return pltpu.make_async_remote_copy(
            src_ref=src, dst_ref=outs[a].at[dst_slot], send_sem=send_sems.at[a * N_PEER + k],
            recv_sem=recv_sems.at[a * N_PEER + k], device_id=peers[k][0], device_id_type=pl.DeviceIdType.MESH)

    local, sends, recvs = [], [], []
    for a in range(len(ins)):
        src = ins[a].at[me] if scatter else ins[a]
        local.append(pltpu.make_async_copy(src, outs[a].at[me], local_sems.at[a]))
        for k in range(N_PEER):
            sends.append(remote(a, k, peers[k][1], me))
            recvs.append(remote(a, k, peers[k][1], peers[k][1]))
    return local, sends, recvs


def _exchange_start(ins, outs, sems, scatter):
    local, sends, _ = _exchange_copies(ins, outs, sems, scatter)
    for cp in local + sends:
        cp.start()


def _exchange_wait(ins, outs, sems, scatter):
    local, sends, recvs = _exchange_copies(ins, outs, sems, scatter)
    for cp in recvs:
        cp.wait_recv()
    for cp in sends:
        cp.wait_send()
    for cp in local:
        cp.wait()


def _exchange_shapes(arrs, scatter):
    n = len(arrs)
    out_shape = [jax.ShapeDtypeStruct(a.shape if scatter else (N_DEV,) + a.shape, a.dtype) for a in arrs]
    sems = [pltpu.SemaphoreType.DMA((n * N_PEER,)), pltpu.SemaphoreType.DMA((n * N_PEER,)),
            pltpu.SemaphoreType.DMA((n,))]
    return out_shape, sems


def _exchange(arrs, *, scatter, name):
    n = len(arrs)

    def body(*refs):
        ins, outs, sems = refs[:n], refs[n:2 * n], refs[2 * n:]
        _exchange_start(ins, outs, sems, scatter)
        _exchange_wait(ins, outs, sems, scatter)

    hbm = pl.BlockSpec(memory_space=pltpu.HBM)
    out_shape, sems = _exchange_shapes(arrs, scatter)
    return pl.pallas_call(body, name=name, in_specs=[hbm] * n, out_specs=[hbm] * n, out_shape=out_shape,
                          scratch_shapes=sems)(*arrs)


def _gather_once_per_chip(arrs, *, name):
    n = len(arrs)

    def body(*refs):
        ins, outs = refs[:n], refs[n:2 * n]
        send_sems, recv_sems, local_sems = refs[2 * n:]
        x, y, c = lax.axis_index("x"), lax.axis_index("y"), lax.axis_index("c")
        sibling = (x, y, 1 - c)
        chips = [(1 - x, y), (x, 1 - y), (1 - x, 1 - y)]

        def slot(px, py, pc):
            return 4 * px + 2 * py + pc

        def copy(a, k, block, to, src=None):
            dst = outs[a].at[slot(*block)]
            return pltpu.make_async_remote_copy(
                src_ref=dst if src is None else src, dst_ref=dst, send_sem=send_sems.at[a * N_PEER + k],
                recv_sem=recv_sems.at[a * N_PEER + k], device_id=to, device_id_type=pl.DeviceIdType.MESH)

        local, sends = [], []
        for a in range(n):
            local.append(pltpu.make_async_copy(ins[a], outs[a].at[slot(x, y, c)], local_sems.at[a]))
            sends.append(copy(a, 0, (x, y, c), sibling, src=ins[a]))
            sends += [copy(a, 1 + j, (x, y, c), (*chip, c), src=ins[a]) for j, chip in enumerate(chips)]
        for cp in local + sends:
            cp.start()
        for a in range(n):
            for j, chip in enumerate(chips):
                copy(a, 1 + j, (*chip, c), (x, y, c)).wait_recv()
                sends.append(copy(a, 4 + j, (*chip, c), sibling))
                sends[-1].start()
        for a in range(n):
            copy(a, 0, sibling, (x, y, c)).wait_recv()
            for j, chip in enumerate(chips):
                copy(a, 4 + j, (*chip, 1 - c), (x, y, c)).wait_recv()
        for cp in sends:
            cp.wait_send()
        for cp in local:
            cp.wait()

    hbm = pl.BlockSpec(memory_space=pltpu.HBM)
    out_shape, sems = _exchange_shapes(arrs, False)
    return pl.pallas_call(body, name=name, in_specs=[hbm] * n, out_specs=[hbm] * n, out_shape=out_shape,
                          scratch_shapes=sems)(*arrs)


def _call(body, *, name, grid, in_specs, out_specs, out_shape, scratch_shapes, sem, args, ride=None,
          vmem_limit=VMEM_LIMIT):
    if ride is None:
        res = pl.pallas_call(body, name=name, grid=grid, in_specs=in_specs, out_specs=out_specs, out_shape=out_shape,
                             scratch_shapes=scratch_shapes, compiler_params=_cparams(sem, vmem_limit))(*args)
        return res, None
    arrs, scatter = ride
    n, n_in, n_out, n_scr = len(arrs), len(in_specs), len(out_specs), len(scratch_shapes)
    x_shape, x_sems = _exchange_shapes(arrs, scatter)

    def hosted(*refs):
        c_in, x_in = refs[:n_in], refs[n_in:n_in + n]
        c_out, x_out = refs[n_in + n:n_in + n + n_out], refs[n_in + n + n_out:n_in + 2 * n + n_out]
        scr = refs[n_in + 2 * n + n_out:]
        c_scr, sems = scr[:n_scr], scr[n_scr:]
        first = functools.reduce(jnp.logical_and, [pl.program_id(a) == 0 for a in range(len(grid))])
        last = functools.reduce(jnp.logical_and, [pl.program_id(a) == grid[a] - 1 for a in range(len(grid))])

        @pl.when(first)
        def _():
            _exchange_start(x_in, x_out, sems, scatter)

        body(*c_in, *c_out, *c_scr)

        @pl.when(last)
        def _():
            _exchange_wait(x_in, x_out, sems, scatter)

    hbm = pl.BlockSpec(memory_space=pltpu.HBM)
    res = pl.pallas_call(
        hosted, name=name, grid=grid, in_specs=list(in_specs) + [hbm] * n, out_specs=list(out_specs) + [hbm] * n,
        out_shape=list(out_shape) + x_shape, scratch_shapes=list(scratch_shapes) + x_sems,
        compiler_params=_cparams(("arbitrary",) * len(grid), vmem_limit))(*args, *arrs)
    return res[:n_out], res[n_out:]


def _adamw_math(w, g, m, v):
    m_new = ADAM_B1 * m + (1.0 - ADAM_B1) * g
    v_new = ADAM_B2 * v + (1.0 - ADAM_B2) * (g * g)
    m_hat = m_new / (1.0 - ADAM_B1 ** ADAM_STEP)
    v_hat = v_new / (1.0 - ADAM_B2 ** ADAM_STEP)
    delta = -ADAM_LR * (m_hat / (jnp.sqrt(v_hat) + ADAM_EPS) + ADAM_WD * w)
    return delta, m_new, v_new


def _adamw_by_columns(w, g, m, v, *, name):
    _, r, c = w.shape
    tc = 256

    def body(w_ref, g_ref, m_ref, v_ref, go_ref, d_ref, mo_ref, vo_ref):
        gv = g_ref[0].astype(F32)
        for s in range(1, N_DEV):
            gv = gv + g_ref[s].astype(F32)
        delta, m_new, v_new = _adamw_math(w_ref[0], gv, m_ref[0], v_ref[0])
        go_ref[0] = gv
        d_ref[0] = delta
        mo_ref[0] = m_new
        vo_ref[0] = v_new

    spec = pl.BlockSpec((1, r, tc), lambda i: (0, 0, i))
    return pl.pallas_call(
        body, name=name, grid=(c // tc,), in_specs=[spec, pl.BlockSpec((N_DEV, r, tc), lambda i: (0, 0, i)), spec, spec],
        out_specs=[spec] * 4, out_shape=[jax.ShapeDtypeStruct((1, r, c), F32)] * 4,
        compiler_params=_cparams(("parallel",)),
    )(w, g, m, v)


def _adamw(w, g, m, v, *, name):
    _, r, c = w.shape
    slots = g.ndim == 3
    if r > 16 and r % 16:
        return _adamw_by_columns(w, g, m, v, name=name)
    tr = r
    for cand in (128, 64, 32, 16):
        if r % cand == 0 and r > cand:
            tr = cand
            break

    def body(w_ref, g_ref, m_ref, v_ref, go_ref, d_ref, mo_ref, vo_ref):
        if slots:
            gv = g_ref[0].astype(F32)
            for s in range(1, N_DEV):
                gv = gv + g_ref[s].astype(F32)
        else:
            gv = g_ref[...]
        delta, m_new, v_new = _adamw_math(w_ref[0], gv, m_ref[0], v_ref[0])
        go_ref[0] = gv
        d_ref[0] = delta
        mo_ref[0] = m_new
        vo_ref[0] = v_new

    spec = pl.BlockSpec((1, tr, c), lambda i: (0, i, 0))
    g_spec = pl.BlockSpec((N_DEV, tr, c), lambda i: (0, i, 0)) if slots else pl.BlockSpec((tr, c), lambda i: (i, 0))
    return pl.pallas_call(
        body, name=name, grid=(r // tr,), in_specs=[spec, g_spec, spec, spec], out_specs=[spec] * 4,
        out_shape=[jax.ShapeDtypeStruct((1, r, c), F32)] * 4, compiler_params=_cparams(("parallel",)),
    )(w, g, m, v)


def _unshard_cols(g):
    return jnp.transpose(g, (1, 0, 2)).reshape(g.shape[1], -1)

def _shard_cols(full):
    r = full.shape[0]
    return jnp.transpose(full.reshape(r, N_DEV, -1), (1, 0, 2))


def _w_in_pieces(d):
    wd, _, ref_off, _, total = _in_layout(d)
    shard = sum(wd.values()) // N_DEV
    out = []
    for n in IN_NAMES:
        r0, r1 = ref_off[n], ref_off[n] + wd[n]
        for s in range(N_DEV):
            lo, hi = max(r0, s * shard), min(r1, (s + 1) * shard)
            if lo < hi:
                out.append((n, s, lo - s * shard, lo - r0, hi - lo))
    return out


def _w_in_gathered_to_mine(g, d):
    wd, pw, _, _, _ = _in_layout(d)
    pieces = _w_in_pieces(d)
    cols = []
    for n in MY_ORDER:
        cols += [lax.slice_in_dim(g[s], a, a + wdt, axis=1) for (m, s, a, _, wdt) in pieces if m == n]
        if pw[n] != wd[n]:
            cols.append(jnp.zeros((g.shape[1], pw[n] - wd[n]), g.dtype))
    return jnp.concatenate(cols, axis=1)


def _w_in_mine_to_slabs(g_mine, d):
    _, _, _, my_off, _ = _in_layout(d)
    pieces = _w_in_pieces(d)
    slabs = []
    for s in range(N_DEV):
        parts = [lax.slice_in_dim(g_mine, my_off[m] + b, my_off[m] + b + wdt, axis=0)
                 for (m, s2, _, b, wdt) in pieces if s2 == s]
        slabs.append(jnp.concatenate(parts, axis=0))
    return jnp.stack(slabs)


def _wuq_to_mine(wq):
    r = wq.shape[0]
    w3 = wq.reshape(r, MLA_HEADS, MLA_NOPE + MLA_ROPE)
    w3 = jnp.pad(w3, ((0, 0), (0, 0), (0, MLA_QK_PAD - MLA_NOPE - MLA_ROPE)))
    return w3.reshape(r, MLA_HEADS * MLA_QK_PAD)


def _wuq_from_mine(g):
    r = g.shape[0]
    return g.reshape(r, MLA_HEADS, MLA_QK_PAD)[:, :, :MLA_NOPE + MLA_ROPE].reshape(r, -1)


def _wukv_to_mine(wkv):
    r = wkv.shape[0]
    w3 = wkv.reshape(r, MLA_HEADS, MLA_NOPE + MLA_V)
    return jnp.concatenate([w3[:, :, :MLA_NOPE].reshape(r, -1), w3[:, :, MLA_NOPE:].reshape(r, -1)], axis=1)


def _wukv_from_mine(g):
    r = g.shape[0]
    kn = g[:, :MLA_HEADS * MLA_NOPE].reshape(r, MLA_HEADS, MLA_NOPE)
    vv = g[:, MLA_HEADS * MLA_NOPE:].reshape(r, MLA_HEADS, MLA_V)
    return jnp.concatenate([kn, vv], axis=2).reshape(r, -1)


COL_SHARDED = ("w_in", "gla_gk_w", "mla_wuq", "mla_wukv", "ffn_w_in")


def _gathered_to_mine(name, g, d):
    if name == "w_in":
        return _w_in_gathered_to_mine(g, d)
    full = _unshard_cols(g) if name in COL_SHARDED else g.reshape(-1, g.shape[-1])
    if name == "gla_gk_w":
        return jnp.pad(full, ((0, LANE - GLA_GATE_RANK), (0, 0)))
    if name == "mla_wuq":
        return _wuq_to_mine(full)
    if name == "mla_wukv":
        return _wukv_to_mine(full)
    return full


def _grad_to_slabs(name, g, d):
    if g.ndim == 3:
        return g.astype(BF16)
    if name == "w_in":
        return _w_in_mine_to_slabs(g, d).astype(BF16)
    if name == "mla_wuq":
        g = _wuq_from_mine(g)
    elif name == "mla_wukv":
        g = _wukv_from_mine(g)
    s = _shard_cols(g) if name in COL_SHARDED else g.reshape(N_DEV, -1, g.shape[-1])
    return s.astype(BF16)


class _Rides:
    GATHER = {"in_proj": ("gla_wo", "mla_wuq", "mla_wukv", "mla_wo", "w_out"),
              "flash_fwd": ("ffn_w_in", "ffn_w_down")}
    SCATTER = {"flash_bwd": ("ffn_w_in", "ffn_w_down", "w_out", "gla_wo", "mla_wo", "gla_gk_w"),
               "in_proj_dw": ("mla_wuq", "mla_wukv"),
               "in_proj_dx": ("w_in",)}

    def __init__(self, send, d):
        self.send, self.d, self.recv = send, d, {}

    def ride(self, stage, grads):
        if stage in self.GATHER:
            return [self.send[n] for n in self.GATHER[stage]], False
        return [_grad_to_slabs(n, grads[n], self.d) for n in self.SCATTER[stage]], True

    def done(self, stage, rode, w):
        if stage in self.GATHER:
            for n, g in zip(self.GATHER[stage], rode):
                w[n] = _gathered_to_mine(n, g, self.d)
        else:
            self.recv.update(zip(self.SCATTER[stage], rode))


def kernel(x, c, positions, ada_w, ada_b, norm_mix_g, w_in, gla_gk_w, gla_gk_b, gla_onorm_g, gla_wo, mla_q_norm_g, mla_wuq, mla_kv_norm_g, mla_wukv, mla_wo, w_out, norm_ffn_g, ffn_w_in, ffn_w_down, final_norm_g, loss_target, m_ada_w, m_ada_b, m_norm_mix_g, m_w_in, m_gla_gk_w, m_gla_gk_b, m_gla_onorm_g, m_gla_wo, m_mla_q_norm_g, m_mla_wuq, m_mla_kv_norm_g, m_mla_wukv, m_mla_wo, m_w_out, m_norm_ffn_g, m_ffn_w_in, m_ffn_w_down, m_final_norm_g, v_ada_w, v_ada_b, v_norm_mix_g, v_w_in, v_gla_gk_w, v_gla_gk_b, v_gla_onorm_g, v_gla_wo, v_mla_q_norm_g, v_mla_wuq, v_mla_kv_norm_g, v_mla_wukv, v_mla_wo, v_w_out, v_norm_ffn_g, v_ffn_w_in, v_ffn_w_down, v_final_norm_g):
    wts = dict(ada_w=ada_w, ada_b=ada_b, norm_mix_g=norm_mix_g, w_in=w_in, gla_gk_w=gla_gk_w, gla_gk_b=gla_gk_b,
               gla_onorm_g=gla_onorm_g, gla_wo=gla_wo, mla_q_norm_g=mla_q_norm_g, mla_wuq=mla_wuq,
               mla_kv_norm_g=mla_kv_norm_g, mla_wukv=mla_wukv, mla_wo=mla_wo, w_out=w_out, norm_ffn_g=norm_ffn_g,
               ffn_w_in=ffn_w_in, ffn_w_down=ffn_w_down, final_norm_g=final_norm_g)
    mom_m = dict(zip(WEIGHTS, (m_ada_w, m_ada_b, m_norm_mix_g, m_w_in, m_gla_gk_w, m_gla_gk_b, m_gla_onorm_g,
                               m_gla_wo, m_mla_q_norm_g, m_mla_wuq, m_mla_kv_norm_g, m_mla_wukv, m_mla_wo, m_w_out,
                               m_norm_ffn_g, m_ffn_w_in, m_ffn_w_down, m_final_norm_g)))
    mom_v = dict(zip(WEIGHTS, (v_ada_w, v_ada_b, v_norm_mix_g, v_w_in, v_gla_gk_w, v_gla_gk_b, v_gla_onorm_g,
                               v_gla_wo, v_mla_q_norm_g, v_mla_wuq, v_mla_kv_norm_g, v_mla_wukv, v_mla_wo, v_w_out,
                               v_norm_ffn_g, v_ffn_w_in, v_ffn_w_down, v_final_norm_g)))
    seq, d = x.shape[1], x.shape[2]
    me = 4 * lax.axis_index("x") + 2 * lax.axis_index("y") + lax.axis_index("c")

    def two_d(a):
        return a.reshape(a.shape[-2], a.shape[-1]) if a.ndim >= 2 else a.reshape(1, -1)

    shard = {n: two_d(wts[n]) for n in BIG}
    send = {n: shard[n].astype(F32 if n == "gla_gk_w" else BF16) for n in BIG}
    got = _gather_once_per_chip([send["w_in"], send["gla_gk_w"], two_d(c)], name="comm_all_gather_first")
    c_all = got[2].reshape(N_DEV, d)
    w = dict(
        w_in=_gathered_to_mine("w_in", got[0], d), gla_gk_w=_gathered_to_mine("gla_gk_w", got[1], d),
        gla_gk_b=two_d(gla_gk_b), gla_onorm_g=two_d(gla_onorm_g), mla_q_norm_g=two_d(mla_q_norm_g),
        mla_kv_norm_g=two_d(mla_kv_norm_g), norm_mix_g=two_d(norm_mix_g), norm_ffn_g=two_d(norm_ffn_g),
        final_norm_g=two_d(final_norm_g))
    rides = _Rides(send, d)

    c_pad = jnp.pad(c_all, ((0, 16 - N_DEV), (0, 0)))
    (c_act,) = _rowwise(lambda cv: cv * _sigmoid(cv), [(c_pad, d, 0)], [], [(d, F32)], tile=16, name="silu_c")
    ada_w2 = two_d(ada_w)
    mod_part = _mm(c_act, ada_w2, name="mm_ada")[:N_DEV]
    (mod_all,) = _exchange([mod_part], scatter=False, name="comm_all_gather_mod")
    mod_mine = lax.dynamic_index_in_dim(mod_all, me, axis=1, keepdims=False).reshape(1, -1) + two_d(ada_b)
    mod8 = jnp.pad(mod_mine.reshape(6, d), ((0, 2), (0, 0)))

    inv_freq = ROPE_THETA ** (-jnp.arange(0, MLA_ROPE, 2, dtype=F32) / MLA_ROPE)
    ang = positions.reshape(seq, 1).astype(F32) * inv_freq[None, :]
    cos, sin, z32 = jnp.cos(ang), jnp.sin(ang), jnp.zeros((seq, 32), F32)
    tab = jnp.concatenate([cos, cos, z32, z32, -sin, z32, z32, z32, z32, sin, z32, z32], axis=1)
    loss_local, grad_x, _, small = _local_step(x.reshape(seq, d), loss_target.reshape(seq, d), tab, mod8, w, rides)

    recv = rides.recv
    pack = jnp.concatenate([small[n] for n in SMALL] + [jnp.full((1, LANE), loss_local, F32)], axis=1)
    (pack_all,) = _exchange([pack], scatter=False, name="comm_all_gather_small")
    pack_all = pack_all.reshape(N_DEV, -1)
    (loss_sum,) = _rowwise(lambda v: (jnp.sum(v, axis=0, keepdims=True),), [(pack_all[:, -LANE:], LANE, 0)], [], [],
                           [LANE], tile=N_DEV, name="loss_sum")
    pack_all = pack_all[:, :-LANE]

    res = {}
    for n in BIG:
        if n == "w_in":
            t_res = _adamw(jnp.swapaxes(wts[n], 1, 2), recv[n], jnp.swapaxes(mom_m[n], 1, 2),
                           jnp.swapaxes(mom_v[n], 1, 2), name="adamw_" + n)
            res[n] = tuple(jnp.swapaxes(a, 1, 2) for a in t_res)
        else:
            res[n] = _adamw(wts[n], recv[n], mom_m[n], mom_v[n], name="adamw_" + n)
    n_ada = ada_w2.shape[1]
    dmod_cols = lax.dynamic_slice_in_dim(pack_all[:, :6 * d], me * n_ada, n_ada, axis=1)

    def f_outer(cat, dm):
        acc = cat[:, 0:1] * dm[0:1]
        for b in range(1, N_DEV):
            acc = acc + cat[:, b:b + 1] * dm[b:b + 1]
        return acc

    (g_ada_w,) = _rowwise(f_outer, [(jnp.transpose(c_act[:N_DEV]), N_DEV, 0)], [dmod_cols], [(n_ada, F32)],
                          tile=256, name="ada_w_grad")
    res["ada_w"] = _adamw(ada_w, g_ada_w, m_ada_w, v_ada_w, name="adamw_ada_w")
    w_small = jnp.concatenate([two_d(wts[n]) for n in SMALL], axis=1)[None]
    m_small = jnp.concatenate([two_d(mom_m[n]) for n in SMALL], axis=1)[None]
    v_small = jnp.concatenate([two_d(mom_v[n]) for n in SMALL], axis=1)[None]
    small_res = _adamw(w_small, pack_all.reshape(N_DEV, 1, -1), m_small, v_small, name="adamw_small")
    off = 0
    for n in SMALL:
        width = wts[n].size
        res[n] = tuple(lax.slice_in_dim(a, off, off + width, axis=2) for a in small_res)
        off += width

    loss = loss_sum[0, 0]
    outs = [loss, grad_x.reshape(x.shape)]
    for kind in range(4):
        outs += [res[n][kind].reshape(wts[n].shape) for n in WEIGHTS]
    return tuple(outs)
```
